```python
import jax, jax.numpy as jnp
from jax import lax
import numpy as np

D_MODEL = 2048
BATCH = 8
SEQ = 4096
DEPTH = 1

D_MIX = D_MODEL
D_GMLP = D_MIX // 2
D_ATTN = D_MIX - D_GMLP
CHUNK = 128
GMLP_GROUPS = 8
GMLP_GROUP_DIM = D_GMLP // GMLP_GROUPS
HEAD_DIM = 64
N_Q_HEADS = D_ATTN // HEAD_DIM
N_KV_HEADS = 2
GQA_GROUP = N_Q_HEADS // N_KV_HEADS
D_KV = N_KV_HEADS * HEAD_DIM
WINDOW = 128
BLOCK = 128
ROPE_THETA = 10000.0
EPS = 1e-6

_SIZES = [D_GMLP, D_GMLP, D_GMLP, D_ATTN, D_KV, D_KV, D_ATTN]
_OFFS = list(np.cumsum(_SIZES)[:-1].tolist())
D_IN_PROJ = int(sum(_SIZES))
D_QKV = D_ATTN + 2 * D_KV
QKV_START = 3 * D_GMLP

kernel_name = "hybrid_gmlp_swa_sink_layer"


def rms_norm(x, g):
    xf = x.astype(jnp.float32)
    y = xf * lax.rsqrt(jnp.mean(xf * xf, axis=-1, keepdims=True) + EPS)
    return (y * g.astype(jnp.float32)).astype(x.dtype)


def layer_norm(x, g, b):
    xf = x.astype(jnp.float32)
    mu = jnp.mean(xf, axis=-1, keepdims=True)
    var = jnp.mean(jnp.square(xf - mu), axis=-1, keepdims=True)
    y = (xf - mu) * lax.rsqrt(var + EPS) * g.astype(jnp.float32) + b.astype(jnp.float32)
    return y.astype(x.dtype)


def rope(x, positions):
    half = HEAD_DIM // 2
    inv_freq = ROPE_THETA ** (-jnp.arange(half, dtype=jnp.float32) * (2.0 / HEAD_DIM))
    ang = positions.astype(jnp.float32)[..., None] * inv_freq
    cos = jnp.cos(ang)[:, :, None, :]
    sin = jnp.sin(ang)[:, :, None, :]
    xf = x.astype(jnp.float32)
    x1, x2 = xf[..., :half], xf[..., half:]
    return jnp.concatenate([x1 * cos - x2 * sin, x2 * cos + x1 * sin], axis=-1).astype(x.dtype)


def chunked_spatial_gating(u, v, ln_g, ln_b, w_s, b_s):
    B, S, _ = u.shape
    nc = S // CHUNK
    vn = layer_norm(v, ln_g, ln_b).reshape(B, nc, CHUNK, GMLP_GROUPS, GMLP_GROUP_DIM)
    causal = jnp.tril(jnp.ones((CHUNK, CHUNK), dtype=bool))
    w = w_s * causal.astype(w_s.dtype)
    mixed = jnp.einsum('gts,bnsgc->bntgc', w, vn)
    mixed = mixed + jnp.transpose(b_s)[None, None, :, :, None]
    return u * mixed.reshape(B, S, D_GMLP)


def sliding_window_attention(q, k, v, sinks):
    B, S, _, _ = q.shape
    nb = S // BLOCK
    qb = q.reshape(B, nb, BLOCK, N_KV_HEADS, GQA_GROUP, HEAD_DIM)
    kb = k.reshape(B, nb, BLOCK, N_KV_HEADS, HEAD_DIM)
    vb = v.reshape(B, nb, BLOCK, N_KV_HEADS, HEAD_DIM)
    pad = ((0, 0), (1, 0), (0, 0), (0, 0), (0, 0))
    kk = jnp.concatenate([jnp.pad(kb, pad)[:, :-1], kb], axis=2)
    vv = jnp.concatenate([jnp.pad(vb, pad)[:, :-1], vb], axis=2)
    scores = jnp.einsum('bnqhgd,bnkhd->bnhgqk', qb, kk,
                        preferred_element_type=jnp.float32) * (HEAD_DIM ** -0.5)
    qi = jnp.arange(BLOCK)[:, None] + BLOCK
    kj = jnp.arange(2 * BLOCK)[None, :]
    dist = qi - kj
    band = (dist >= 0) & (dist < WINDOW)
    blk = jnp.arange(nb)[:, None, None]
    valid = band[None] & ((blk * BLOCK + kj[None] - BLOCK) >= 0)
    scores = jnp.where(valid[None, :, None, None], scores, -jnp.inf)
    sink = sinks.astype(jnp.float32).reshape(N_KV_HEADS, GQA_GROUP)[None, None, :, :, None, None]
    m = jnp.maximum(jnp.max(scores, axis=-1, keepdims=True), sink)
    p = jnp.exp(scores - m)
    denom = jnp.sum(p, axis=-1, keepdims=True) + jnp.exp(sink - m)
    p = (p / denom).astype(v.dtype)
    out = jnp.einsum('bnhgqk,bnkhd->bnqhgd', p, vv)
    return out.reshape(B, S, N_Q_HEADS * HEAD_DIM)


def _fwd_setup_inputs(seed: int = 0) -> dict:
    key = jax.random.key(seed)
    ks = jax.random.split(key, 12)
    f32 = jnp.float32
    x = jax.random.normal(ks[0], (BATCH, SEQ, D_MODEL), f32)
    offsets = jax.random.randint(ks[1], (BATCH, 1), 0, 1024, dtype=jnp.int32)
    positions = (jnp.arange(SEQ, dtype=jnp.int32)[None, :] + offsets).astype(jnp.int32)
    g_pre = 1.0 + 0.02 * jax.random.normal(ks[2], (DEPTH, D_MODEL), f32)
    w_in = jax.random.normal(ks[3], (DEPTH, D_MODEL, D_IN_PROJ), f32) * (D_MODEL ** -0.5)
    b_qkv = 0.01 * jax.random.normal(ks[4], (DEPTH, D_QKV), f32)
    ln_v_g = 1.0 + 0.02 * jax.random.normal(ks[5], (DEPTH, D_GMLP), f32)
    ln_v_b = 0.01 * jax.random.normal(ks[6], (DEPTH, D_GMLP), f32)
    w_spatial = jax.random.normal(ks[7], (DEPTH, GMLP_GROUPS, CHUNK, CHUNK), f32) * (CHUNK ** -0.5)
    b_spatial = 1.0 + 0.02 * jax.random.normal(ks[8], (DEPTH, GMLP_GROUPS, CHUNK), f32)
    attn_sinks = jax.random.normal(ks[9], (DEPTH, N_Q_HEADS), f32)
    w_out = jax.random.normal(ks[10], (DEPTH, D_MIX, D_MODEL), f32) * (D_MIX ** -0.5)
    g_post = 1.0 + 0.02 * jax.random.normal(ks[11], (DEPTH, D_MODEL), f32)
    return {"x": x, "positions": positions, "g_pre": g_pre, "w_in": w_in, "b_qkv": b_qkv,
            "ln_v_g": ln_v_g, "ln_v_b": ln_v_b, "w_spatial": w_spatial, "b_spatial": b_spatial,
            "attn_sinks": attn_sinks, "w_out": w_out, "g_post": g_post}


def _fwd_reference(x, positions, g_pre, w_in, b_qkv, ln_v_g, ln_v_b, w_spatial, b_spatial,
              attn_sinks, w_out, g_post):
    B, S, _ = x.shape
    for l in range(DEPTH):
        h = rms_norm(x, g_pre[l])
        proj = jnp.einsum('bsd,de->bse', h, w_in[l])
        bias = jnp.concatenate([jnp.zeros((QKV_START,), proj.dtype), b_qkv[l].astype(proj.dtype),
                                jnp.zeros((D_ATTN,), proj.dtype)])
        proj = proj + bias
        u, v_g, z_a, q, k, v_a, z_b = jnp.split(proj, _OFFS, axis=-1)
        y_a = chunked_spatial_gating(u, v_g, ln_v_g[l], ln_v_b[l], w_spatial[l], b_spatial[l])
        y_a = y_a * jax.nn.silu(z_a)
        q = rope(q.reshape(B, S, N_Q_HEADS, HEAD_DIM), positions)
        k = rope(k.reshape(B, S, N_KV_HEADS, HEAD_DIM), positions)
        v_a = v_a.reshape(B, S, N_KV_HEADS, HEAD_DIM)
        y_b = sliding_window_attention(q, k, v_a, attn_sinks[l]) * jax.nn.silu(z_b)
        y = jnp.einsum('bse,ed->bsd', jnp.concatenate([y_a, y_b], axis=-1), w_out[l])
        x = x + rms_norm(y, g_post[l])
    return x


import jax as _jax
import jax.numpy as _jnp

TWIN_FORMAT = 'train_step'
FWD_PARAMS = ['x', 'positions', 'g_pre', 'w_in', 'b_qkv', 'ln_v_g', 'ln_v_b', 'w_spatial', 'b_spatial', 'attn_sinks', 'w_out', 'g_post']
TWIN_WEIGHTS = ['g_pre', 'w_in', 'b_qkv', 'ln_v_g', 'ln_v_b', 'w_spatial', 'b_spatial', 'attn_sinks', 'w_out', 'g_post']
TWIN_DIFF_INPUT = 'x'
TWIN_INPUTS = ['x', 'positions', 'g_pre', 'w_in', 'b_qkv', 'ln_v_g', 'ln_v_b', 'w_spatial', 'b_spatial', 'attn_sinks', 'w_out', 'g_post', 'loss_target', 'm_g_pre', 'm_w_in', 'm_b_qkv', 'm_ln_v_g', 'm_ln_v_b', 'm_w_spatial', 'm_b_spatial', 'm_attn_sinks', 'm_w_out', 'm_g_post', 'v_g_pre', 'v_w_in', 'v_b_qkv', 'v_ln_v_g', 'v_ln_v_b', 'v_w_spatial', 'v_b_spatial', 'v_attn_sinks', 'v_w_out', 'v_g_post']
TWIN_OUTPUTS = ['loss', 'grad_x', 'grad_g_pre', 'grad_w_in', 'grad_b_qkv', 'grad_ln_v_g', 'grad_ln_v_b', 'grad_w_spatial', 'grad_b_spatial', 'grad_attn_sinks', 'grad_w_out', 'grad_g_post', 'delta_g_pre', 'delta_w_in', 'delta_b_qkv', 'delta_ln_v_g', 'delta_ln_v_b', 'delta_w_spatial', 'delta_b_spatial', 'delta_attn_sinks', 'delta_w_out', 'delta_g_post', 'new_m_g_pre', 'new_m_w_in', 'new_m_b_qkv', 'new_m_ln_v_g', 'new_m_ln_v_b', 'new_m_w_spatial', 'new_m_b_spatial', 'new_m_attn_sinks', 'new_m_w_out', 'new_m_g_post', 'new_v_g_pre', 'new_v_w_in', 'new_v_b_qkv', 'new_v_ln_v_g', 'new_v_ln_v_b', 'new_v_w_spatial', 'new_v_b_spatial', 'new_v_attn_sinks', 'new_v_w_out', 'new_v_g_post']
TWIN_LEAF_KINDS = {'loss': 'loss', 'grad_x': 'grad_x', 'grad_g_pre': 'grad_w', 'grad_w_in': 'grad_w', 'grad_b_qkv': 'grad_w', 'grad_ln_v_g': 'grad_w', 'grad_ln_v_b': 'grad_w', 'grad_w_spatial': 'grad_w', 'grad_b_spatial': 'grad_w', 'grad_attn_sinks': 'grad_w', 'grad_w_out': 'grad_w', 'grad_g_post': 'grad_w', 'delta_g_pre': 'delta_w', 'delta_w_in': 'delta_w', 'delta_b_qkv': 'delta_w', 'delta_ln_v_g': 'delta_w', 'delta_ln_v_b': 'delta_w', 'delta_w_spatial': 'delta_w', 'delta_b_spatial': 'delta_w', 'delta_attn_sinks': 'delta_w', 'delta_w_out': 'delta_w', 'delta_g_post': 'delta_w', 'new_m_g_pre': 'new_m', 'new_m_w_in': 'new_m', 'new_m_b_qkv': 'new_m', 'new_m_ln_v_g': 'new_m', 'new_m_ln_v_b': 'new_m', 'new_m_w_spatial': 'new_m', 'new_m_b_spatial': 'new_m', 'new_m_attn_sinks': 'new_m', 'new_m_w_out': 'new_m', 'new_m_g_post': 'new_m', 'new_v_g_pre': 'new_v', 'new_v_w_in': 'new_v', 'new_v_b_qkv': 'new_v', 'new_v_ln_v_g': 'new_v', 'new_v_ln_v_b': 'new_v', 'new_v_w_spatial': 'new_v', 'new_v_b_spatial': 'new_v', 'new_v_attn_sinks': 'new_v', 'new_v_w_out': 'new_v', 'new_v_g_post': 'new_v'}


def _forward(args):
    return _fwd_reference(*[args[k] for k in FWD_PARAMS])


def _output_shape():
    def fwd():
        inp = _fwd_setup_inputs(0)
        return _fwd_reference(*[inp[k] for k in FWD_PARAMS])
    out = _jax.eval_shape(fwd)
    return out.shape, out.dtype

N_MICROBATCH = 1
ADAM_LR = 0.001
ADAM_B1 = 0.9
ADAM_B2 = 0.999
ADAM_EPS = 1e-08
ADAM_WD = 0.01
ADAM_STEP = 10
PER_EXAMPLE_BATCH_AXIS = {'x': 0, 'positions': 0, 'loss_target': 0}
SHARED_INPUTS = []
_WEIGHT_DTYPES = {'g_pre': _jnp.float32, 'w_in': _jnp.float32, 'b_qkv': _jnp.float32, 'ln_v_g': _jnp.float32, 'ln_v_b': _jnp.float32, 'w_spatial': _jnp.float32, 'b_spatial': _jnp.float32, 'attn_sinks': _jnp.float32, 'w_out': _jnp.float32, 'g_post': _jnp.float32}
MOMENT_SCALE = {'g_pre': 2.151545e-01, 'w_in': 1.269475e-01, 'b_qkv': 3.418981e-01, 'ln_v_g': 9.961172e-02, 'ln_v_b': 1.091078e-01, 'w_spatial': 9.767827e-02, 'b_spatial': 1.409812e-01, 'attn_sinks': 3.943600e-02, 'w_out': 1.430373e-01, 'g_post': 1.599497e+01}


def _to_microbatches(a, axis):
    t = _jnp.moveaxis(a, axis, 0)
    t = t.reshape((N_MICROBATCH, t.shape[0] // N_MICROBATCH) + t.shape[1:])
    return _jnp.moveaxis(t, 1, axis + 1)


def setup_inputs(seed: int = 0) -> dict:
    inp = _fwd_setup_inputs(seed)
    key = _jax.random.fold_in(_jax.random.key(seed), 7919)
    shape, _ = _output_shape()
    out = dict(inp)
    out["loss_target"] = _jax.random.normal(_jax.random.fold_in(key, 0), shape, _jnp.float32)
    for i, name in enumerate(TWIN_WEIGHTS):
        w = inp[name].astype(_jnp.float32)
        if MOMENT_SCALE is None:
            s = _jnp.sqrt(_jnp.mean(_jnp.square(w)) + 1e-30)
        else:
            s = MOMENT_SCALE[name]
        km, kv = _jax.random.split(_jax.random.fold_in(key, i + 1))
        out[name] = w
        out["m_" + name] = s * _jax.random.normal(km, w.shape, _jnp.float32)
        out["v_" + name] = (s * s) * _jax.random.uniform(kv, w.shape, _jnp.float32, 0.5, 1.5)
    if N_MICROBATCH > 1:
        for name, axis in PER_EXAMPLE_BATCH_AXIS.items():
            out[name] = _to_microbatches(out[name], axis)
    return {'x': out['x'], 'positions': out['positions'], 'g_pre': out['g_pre'], 'w_in': out['w_in'], 'b_qkv': out['b_qkv'], 'ln_v_g': out['ln_v_g'], 'ln_v_b': out['ln_v_b'], 'w_spatial': out['w_spatial'], 'b_spatial': out['b_spatial'], 'attn_sinks': out['attn_sinks'], 'w_out': out['w_out'], 'g_post': out['g_post'], 'loss_target': out['loss_target'], 'm_g_pre': out['m_g_pre'], 'm_w_in': out['m_w_in'], 'm_b_qkv': out['m_b_qkv'], 'm_ln_v_g': out['m_ln_v_g'], 'm_ln_v_b': out['m_ln_v_b'], 'm_w_spatial': out['m_w_spatial'], 'm_b_spatial': out['m_b_spatial'], 'm_attn_sinks': out['m_attn_sinks'], 'm_w_out': out['m_w_out'], 'm_g_post': out['m_g_post'], 'v_g_pre': out['v_g_pre'], 'v_w_in': out['v_w_in'], 'v_b_qkv': out['v_b_qkv'], 'v_ln_v_g': out['v_ln_v_g'], 'v_ln_v_b': out['v_ln_v_b'], 'v_w_spatial': out['v_w_spatial'], 'v_b_spatial': out['v_b_spatial'], 'v_attn_sinks': out['v_attn_sinks'], 'v_w_out': out['v_w_out'], 'v_g_post': out['v_g_post']}


def _loss(weights, diff, rest, loss_target):
    with _jax.named_scope("forward"):
        args = {**rest, TWIN_DIFF_INPUT: diff, **{k: w.astype(_WEIGHT_DTYPES[k]) for k, w in weights.items()}}
        y = _forward(args)
    with _jax.named_scope("loss_head"):
        err = _jnp.square(y.astype(_jnp.float32) - loss_target)
        return 0.5 * _jnp.sum(_jnp.mean(err, axis=-1)) if err.ndim else 0.5 * err


def _adamw(w, g, m, v):
    m = ADAM_B1 * m + (1.0 - ADAM_B1) * g
    v = ADAM_B2 * v + (1.0 - ADAM_B2) * _jnp.square(g)
    m_hat = m / (1.0 - ADAM_B1 ** ADAM_STEP)
    v_hat = v / (1.0 - ADAM_B2 ** ADAM_STEP)
    delta = -ADAM_LR * (m_hat / (_jnp.sqrt(v_hat) + ADAM_EPS) + ADAM_WD * w)
    return delta, m, v


def reference(x, positions, g_pre, w_in, b_qkv, ln_v_g, ln_v_b, w_spatial, b_spatial, attn_sinks, w_out, g_post, loss_target, m_g_pre, m_w_in, m_b_qkv, m_ln_v_g, m_ln_v_b, m_w_spatial, m_b_spatial, m_attn_sinks, m_w_out, m_g_post, v_g_pre, v_w_in, v_b_qkv, v_ln_v_g, v_ln_v_b, v_w_spatial, v_b_spatial, v_attn_sinks, v_w_out, v_g_post):
    given = dict(x=x, positions=positions, g_pre=g_pre, w_in=w_in, b_qkv=b_qkv, ln_v_g=ln_v_g, ln_v_b=ln_v_b, w_spatial=w_spatial, b_spatial=b_spatial, attn_sinks=attn_sinks, w_out=w_out, g_post=g_post, loss_target=loss_target, m_g_pre=m_g_pre, m_w_in=m_w_in, m_b_qkv=m_b_qkv, m_ln_v_g=m_ln_v_g, m_ln_v_b=m_ln_v_b, m_w_spatial=m_w_spatial, m_b_spatial=m_b_spatial, m_attn_sinks=m_attn_sinks, m_w_out=m_w_out, m_g_post=m_g_post, v_g_pre=v_g_pre, v_w_in=v_w_in, v_b_qkv=v_b_qkv, v_ln_v_g=v_ln_v_g, v_ln_v_b=v_ln_v_b, v_w_spatial=v_w_spatial, v_b_spatial=v_b_spatial, v_attn_sinks=v_attn_sinks, v_w_out=v_w_out, v_g_post=v_g_post)
    weights = {n: given[n] for n in TWIN_WEIGHTS}
    shared = {n: given[n] for n in SHARED_INPUTS}
    per_example = {n: given[n] for n in ['x', 'positions']}
    grad_fn = _jax.value_and_grad(_loss, argnums=(0, 1))

    def one_microbatch(ex, loss_target):
        ex = dict(ex)
        diff = ex.pop(TWIN_DIFF_INPUT)
        return grad_fn(weights, diff, {**shared, **ex}, loss_target)

    if N_MICROBATCH == 1:
        loss, (grad_w, grad_x) = one_microbatch(per_example, given["loss_target"])
    else:
        def body(carry, xs):
            loss_sum, grad_sum = carry
            l_k, (gw_k, gx_k) = one_microbatch(xs[0], xs[1])
            with _jax.named_scope("update"):
                return (loss_sum + l_k, _jax.tree.map(_jnp.add, grad_sum, gw_k)), gx_k

        init = (_jnp.zeros((), _jnp.float32), _jax.tree.map(_jnp.zeros_like, weights))
        (loss, grad_w), grad_x = _jax.lax.scan(body, init, (per_example, given["loss_target"]))
    with _jax.named_scope("update"):
        delta_w, new_m, new_v = {}, {}, {}
        for n in TWIN_WEIGHTS:
            delta_w[n], new_m[n], new_v[n] = _adamw(weights[n], grad_w[n], given["m_" + n], given["v_" + n])
    return (loss, grad_x, *[grad_w[n] for n in TWIN_WEIGHTS], *[delta_w[n] for n in TWIN_WEIGHTS],
            *[new_m[n] for n in TWIN_WEIGHTS], *[new_v[n] for n in TWIN_WEIGHTS])
```

```python
import functools

import jax
import jax.numpy as jnp
from jax import lax
from jax.experimental import pallas as pl
from jax.experimental.pallas import tpu as pltpu

F32 = jnp.float32
BF16 = jnp.bfloat16

D_MODEL = 2048
D_GMLP = 1024
D_ATTN = 1024
CHUNK = 128
GROUPS = 8
HEAD_DIM = 64
N_Q_HEADS = 16
N_KV_HEADS = 2
D_KV = N_KV_HEADS * HEAD_DIM
D_IN = 3 * D_GMLP + D_ATTN + 2 * D_KV + D_ATTN
OFF_U, OFF_V, OFF_ZA = 0, D_GMLP, 2 * D_GMLP
OFF_Q = 3 * D_GMLP
OFF_K = OFF_Q + D_ATTN
OFF_VA = OFF_K + D_KV
OFF_ZB = OFF_VA + D_KV
D_QKV = D_ATTN + 2 * D_KV
ROPE_THETA = 10000.0
EPS = 1e-6
SCALE = HEAD_DIM ** -0.5
NEG = -1e30
N_PAIRS = N_Q_HEADS // 2
PAIRS_PER_KV = N_PAIRS // N_KV_HEADS

ADAM_LR = 0.001
ADAM_B1 = 0.9
ADAM_B2 = 0.999
ADAM_EPS = 1e-08
ADAM_WD = 0.01
ADAM_STEP = 10

N_DEV = 8
LANES = 128
VMEM_LIMIT = 56 * 1024 * 1024

MESH = pl.DeviceIdType.MESH
ANY = pl.BlockSpec(memory_space=pl.ANY)


def _cparams(sem=None):
    return pltpu.CompilerParams(dimension_semantics=sem, vmem_limit_bytes=VMEM_LIMIT)


def _tile(n, prefs):
    for t in prefs:
        if n % t == 0:
            return t
    return n


def _sigmoid(z):
    return 1.0 / (1.0 + jnp.exp(-z))


def _dot(a, b, ca, cb):
    return lax.dot_general(a, b, (((ca,), (cb,)), ((), ())), preferred_element_type=F32)


def _my_place():
    return lax.axis_index("x"), lax.axis_index("y"), lax.axis_index("c")


def _chip_of(x, y, r):
    return (x ^ (r & 1), y ^ (r >> 1))


def allgather_rows(shards, name):
    n_arr = len(shards)

    def body(*refs):
        ins, outs = refs[:n_arr], refs[n_arr:2 * n_arr]
        send_sems, recv_sems, local_sems = refs[2 * n_arr:]
        x, y, c = _my_place()
        me, sibling = (x, y, c), (x, y, 1 - c)
        chips = [_chip_of(x, y, r) for r in (1, 2, 3)]

        def rows(a, px, py, pc):
            m = shards[a].shape[0]
            return outs[a].at[pl.ds(pl.multiple_of((4 * px + 2 * py + pc) * m, 8), m), :]

        def copy(a, k, block, to, src=None):
            return pltpu.make_async_remote_copy(
                src_ref=rows(a, *block) if src is None else src, dst_ref=rows(a, *block),
                send_sem=send_sems.at[a * 7 + k], recv_sem=recv_sems.at[a * 7 + k],
                device_id=to, device_id_type=MESH)

        mine = [pltpu.make_async_copy(ins[a], rows(a, *me), local_sems.at[a]) for a in range(n_arr)]
        for cp in mine:
            cp.start()
        first = []
        for a in range(n_arr):
            first.append(copy(a, 0, me, sibling, src=ins[a]))
            first += [copy(a, 1 + j, me, (*chip, c), src=ins[a]) for j, chip in enumerate(chips)]
        for cp in first:
            cp.start()
        passed = []
        for j, chip in enumerate(chips):
            for a in range(n_arr):
                copy(a, 1 + j, (*chip, c), me).wait_recv()
                fwd = copy(a, 4 + j, (*chip, c), sibling)
                fwd.start()
                passed.append(fwd)
        for a in range(n_arr):
            copy(a, 0, sibling, me).wait_recv()
            for j, chip in enumerate(chips):
                copy(a, 4 + j, (*chip, 1 - c), me).wait_recv()
        for cp in first + passed:
            cp.wait_send()
        for cp in mine:
            cp.wait()

    outs = pl.pallas_call(
        body, name=name,
        out_shape=[jax.ShapeDtypeStruct((N_DEV * s.shape[0], s.shape[1]), s.dtype) for s in shards],
        in_specs=[ANY] * n_arr, out_specs=[ANY] * n_arr,
        scratch_shapes=[pltpu.SemaphoreType.DMA((7 * n_arr,)), pltpu.SemaphoreType.DMA((7 * n_arr,)),
                        pltpu.SemaphoreType.DMA((n_arr,))],
    )(*shards)
    return list(outs)


def exchange_pair(parts, name):
    n_arr = len(parts)

    def body(*refs):
        ins, outs = refs[:n_arr], refs[n_arr:2 * n_arr]
        send_sems, recv_sems = refs[2 * n_arr:]
        x, y, c = _my_place()
        copies = []
        for a in range(n_arr):
            m = parts[a].shape[0] // N_DEV
            for r in range(4):
                cx, cy = _chip_of(x, y, r)
                owner = 4 * cx + 2 * cy + (1 - c)
                copies.append(pltpu.make_async_remote_copy(
                    src_ref=ins[a].at[pl.ds(pl.multiple_of(owner * m, 16), m), :], dst_ref=outs[a].at[r],
                    send_sem=send_sems.at[a * 4 + r], recv_sem=recv_sems.at[a * 4 + r],
                    device_id=(x, y, 1 - c), device_id_type=MESH))
        for cp in copies:
            cp.start()
        for cp in copies:
            cp.wait_recv()
        for cp in copies:
            cp.wait_send()

    outs = pl.pallas_call(
        body, name=name,
        out_shape=[jax.ShapeDtypeStruct((4, p.shape[0] // N_DEV, p.shape[1]), p.dtype) for p in parts],
        in_specs=[ANY] * n_arr, out_specs=[ANY] * n_arr,
        scratch_shapes=[pltpu.SemaphoreType.DMA((4 * n_arr,)), pltpu.SemaphoreType.DMA((4 * n_arr,))],
    )(*parts)
    return list(outs)


def exchange_chips(sums, name):
    n_arr = len(sums)

    def body(*refs):
        ins, outs = refs[:n_arr], refs[n_arr:2 * n_arr]
        send_sems, recv_sems = refs[2 * n_arr:]
        x, y, c = _my_place()
        copies = []
        for a in range(n_arr):
            for r in (1, 2, 3):
                copies.append(pltpu.make_async_remote_copy(
                    src_ref=ins[a].at[r], dst_ref=outs[a].at[r - 1],
                    send_sem=send_sems.at[a * 3 + r - 1], recv_sem=recv_sems.at[a * 3 + r - 1],
                    device_id=(*_chip_of(x, y, r), c), device_id_type=MESH))
        for cp in copies:
            cp.start()
        for cp in copies:
            cp.wait_recv()
        for cp in copies:
            cp.wait_send()

    outs = pl.pallas_call(
        body, name=name,
        out_shape=[jax.ShapeDtypeStruct((3,) + s.shape[1:], s.dtype) for s in sums],
        in_specs=[ANY] * n_arr, out_specs=[ANY] * n_arr,
        scratch_shapes=[pltpu.SemaphoreType.DMA((3 * n_arr,)), pltpu.SemaphoreType.DMA((3 * n_arr,))],
    )(*sums)
    return list(outs)


def pair_sum(part, got, owners, name):
    m, n = got.shape[1:]

    def body(own_ref, mine_ref, got_ref, out_ref):
        del own_ref
        out_ref[...] = (mine_ref[...].astype(F32) + got_ref[...].astype(F32)).astype(out_ref.dtype)

    return pl.pallas_call(
        body, name=name,
        grid_spec=pltpu.PrefetchScalarGridSpec(
            num_scalar_prefetch=1, grid=(4,),
            in_specs=[pl.BlockSpec((m, n), lambda r, own: (own[r], 0)),
                      pl.BlockSpec((None, m, n), lambda r, own: (r, 0, 0))],
            out_specs=pl.BlockSpec((None, m, n), lambda r, own: (r, 0, 0))),
        out_shape=jax.ShapeDtypeStruct((4, m, n), got.dtype),
        compiler_params=_cparams(("arbitrary",)),
    )(owners, part, got)


def chip_sum(sums, got, name):
    m, n = got.shape[1:]
    tr = _tile(m, (336, 256, 128))

    def body(s_ref, g_ref, out_ref):
        acc = s_ref[...].astype(F32)
        for r in range(3):
            acc = acc + g_ref[r].astype(F32)
        out_ref[...] = acc

    return pl.pallas_call(
        body, name=name, grid=(m // tr,),
        in_specs=[pl.BlockSpec((None, tr, n), lambda i: (0, i, 0)),
                  pl.BlockSpec((3, tr, n), lambda i: (0, i, 0))],
        out_specs=pl.BlockSpec((tr, n), lambda i: (i, 0)),
        out_shape=jax.ShapeDtypeStruct((m, n), F32),
        compiler_params=_cparams(("arbitrary",)),
    )(sums, got)


def in_proj(x, g_pre, wt, bias):
    s, d = x.shape
    tm = _tile(s, (512, 256, 128))
    tn = 768

    def body(x_ref, g_ref, w_ref, b_ref, proj_ref, h_ref):
        @pl.when(pl.program_id(1) == 0)
        def _():
            xv = x_ref[...]
            r = lax.rsqrt(jnp.mean(xv * xv, axis=-1, keepdims=True) + EPS)
            h_ref[...] = (xv * r * g_ref[...]).astype(BF16)

        acc = _dot(h_ref[...], w_ref[...], 1, 1)
        proj_ref[...] = (acc + b_ref[...]).astype(BF16)

    return pl.pallas_call(
        body, name="in_proj", grid=(s // tm, D_IN // tn),
        in_specs=[pl.BlockSpec((tm, d), lambda i, j: (i, 0)),
                  pl.BlockSpec((1, d), lambda i, j: (0, 0)),
                  pl.BlockSpec((tn, d), lambda i, j: (j, 0)),
                  pl.BlockSpec((1, tn), lambda i, j: (0, j))],
        out_specs=[pl.BlockSpec((tm, tn), lambda i, j: (i, j)),
                   pl.BlockSpec((tm, d), lambda i, j: (i, 0))],
        out_shape=[jax.ShapeDtypeStruct((s, D_IN), BF16), jax.ShapeDtypeStruct((s, d), BF16)],
        compiler_params=_cparams(("arbitrary", "arbitrary")),
    )(x, g_pre, wt, bias)


def out_proj_loss(cat, w_out, x, target, g_post):
    s, d = x.shape
    tm = _tile(s, (256, 128))

    def body(cat_ref, w_ref, x_ref, t_ref, g_ref, dy_ref, dout_ref, dg_ref, loss_ref):
        @pl.when(pl.program_id(0) == 0)
        def _():
            dg_ref[...] = jnp.zeros_like(dg_ref)
            loss_ref[...] = jnp.zeros_like(loss_ref)

        yv = _dot(cat_ref[...], w_ref[...], 1, 0)
        r = lax.rsqrt(jnp.mean(yv * yv, axis=-1, keepdims=True) + EPS)
        nrm = yv * r
        g = g_ref[...]
        err = x_ref[...] + nrm * g - t_ref[...]
        loss_ref[...] += 0.5 * jnp.sum(jnp.sum(err * err, axis=-1, keepdims=True), axis=0, keepdims=True) / d
        dout = err * (1.0 / d)
        dout_ref[...] = dout
        dg_ref[...] += jnp.sum(dout * nrm, axis=0, keepdims=True)
        dn = dout * g
        dy = r * (dn - nrm * jnp.mean(dn * nrm, axis=-1, keepdims=True))
        dy_ref[...] = dy.astype(BF16)

    return pl.pallas_call(
        body, name="out_proj_loss", grid=(s // tm,),
        in_specs=[pl.BlockSpec((tm, d), lambda i: (i, 0)),
                  pl.BlockSpec((d, d), lambda i: (0, 0)),
                  pl.BlockSpec((tm, d), lambda i: (i, 0)),
                  pl.BlockSpec((tm, d), lambda i: (i, 0)),
                  pl.BlockSpec((1, d), lambda i: (0, 0))],
        out_specs=[pl.BlockSpec((tm, d), lambda i: (i, 0)),
                   pl.BlockSpec((tm, d), lambda i: (i, 0)),
                   pl.BlockSpec((1, d), lambda i: (0, 0)),
                   pl.BlockSpec((1, LANES), lambda i: (0, 0))],
        out_shape=[jax.ShapeDtypeStruct((s, d), BF16), jax.ShapeDtypeStruct((s, d), F32),
                   jax.ShapeDtypeStruct((1, d), F32), jax.ShapeDtypeStruct((1, LANES), F32)],
        compiler_params=_cparams(("arbitrary",)),
    )(cat, w_out, x, target, g_post)


def matmul_nt(a, b, name):
    m, k = a.shape
    n = b.shape[0]
    tm = _tile(m, (512, 256, 128))

    def body(a_ref, b_ref, o_ref):
        o_ref[...] = _dot(a_ref[...], b_ref[...], 1, 1).astype(o_ref.dtype)

    return pl.pallas_call(
        body, name=name, grid=(m // tm,),
        in_specs=[pl.BlockSpec((tm, k), lambda i: (i, 0)), pl.BlockSpec((n, k), lambda i: (0, 0))],
        out_specs=pl.BlockSpec((tm, n), lambda i: (i, 0)),
        out_shape=jax.ShapeDtypeStruct((m, n), BF16),
        compiler_params=_cparams(("arbitrary",)),
    )(a, b)


def matmul_tn(a, b, tm, name):
    k, m = a.shape
    n = b.shape[1]
    tk = _tile(k, (1024, 512, 256, 128))
    nk = k // tk

    def body(a_ref, b_ref, o_ref, cs_ref, acc_ref, csacc_ref):
        kk = pl.program_id(1)

        @pl.when(kk == 0)
        def _():
            acc_ref[...] = jnp.zeros_like(acc_ref)
            csacc_ref[...] = jnp.zeros_like(csacc_ref)

        av = a_ref[...]
        acc_ref[...] += _dot(av, b_ref[...], 0, 0)
        csacc_ref[...] += jnp.sum(av.astype(F32), axis=0, keepdims=True)

        @pl.when(kk == nk - 1)
        def _():
            o_ref[...] = acc_ref[...].astype(o_ref.dtype)
            cs_ref[...] = csacc_ref[...]

    return pl.pallas_call(
        body, name=name, grid=(m // tm, nk),
        in_specs=[pl.BlockSpec((tk, tm), lambda i, j: (j, i)), pl.BlockSpec((tk, n), lambda i, j: (j, 0))],
        out_specs=[pl.BlockSpec((tm, n), lambda i, j: (i, 0)), pl.BlockSpec((1, tm), lambda i, j: (0, i))],
        out_shape=[jax.ShapeDtypeStruct((m, n), BF16), jax.ShapeDtypeStruct((1, m), F32)],
        scratch_shapes=[pltpu.VMEM((tm, n), F32), pltpu.VMEM((1, tm), F32)],
        compiler_params=_cparams(("arbitrary", "arbitrary")),
    )(a, b)


def in_proj_bwd(dproj, wt, x, g_pre, dout):
    s, d = x.shape
    tm = _tile(s, (512, 256, 128))
    tk = 768
    nk = D_IN // tk

    def body(dp_ref, w_ref, x_ref, g_ref, dout_ref, gx_ref, dg_ref, acc_ref):
        i, kk = pl.program_id(0), pl.program_id(1)

        @pl.when(jnp.logical_and(i == 0, kk == 0))
        def _():
            dg_ref[...] = jnp.zeros_like(dg_ref)

        @pl.when(kk == 0)
        def _():
            acc_ref[...] = jnp.zeros_like(acc_ref)

        acc_ref[...] += _dot(dp_ref[...], w_ref[...], 1, 0)

        @pl.when(kk == nk - 1)
        def _():
            for c0 in range(0, tm, CHUNK):
                rows = slice(c0, c0 + CHUNK)
                dh = acc_ref[rows, :]
                xv = x_ref[rows, :]
                r = lax.rsqrt(jnp.mean(xv * xv, axis=-1, keepdims=True) + EPS)
                xn = xv * r
                dg_ref[...] += jnp.sum(dh * xn, axis=0, keepdims=True)
                dn = dh * g_ref[...]
                gx_ref[rows, :] = dout_ref[rows, :] + r * (dn - xn * jnp.mean(dn * xn, axis=-1, keepdims=True))

    return pl.pallas_call(
        body, name="in_proj_bwd", grid=(s // tm, nk),
        in_specs=[pl.BlockSpec((tm, tk), lambda i, j: (i, j)),
                  pl.BlockSpec((tk, d), lambda i, j: (j, 0)),
                  pl.BlockSpec((tm, d), lambda i, j: (i, 0)),
                  pl.BlockSpec((1, d), lambda i, j: (0, 0)),
                  pl.BlockSpec((tm, d), lambda i, j: (i, 0))],
        out_specs=[pl.BlockSpec((tm, d), lambda i, j: (i, 0)), pl.BlockSpec((1, d), lambda i, j: (0, 0))],
        out_shape=[jax.ShapeDtypeStruct((s, d), F32), jax.ShapeDtypeStruct((1, d), F32)],
        scratch_shapes=[pltpu.VMEM((tm, d), F32)],
        compiler_params=_cparams(("arbitrary", "arbitrary")),
    )(dproj, wt, x, g_pre, dout)


def _lane_iota(shape):
    return lax.broadcasted_iota(jnp.int32, shape, len(shape) - 1)


def _rope_tables(pos_col, freq, sign):
    ang = pos_col.astype(F32) * freq
    return jnp.cos(ang), jnp.sin(ang) * sign


def _partner(v):
    low = (_lane_iota(v.shape) % HEAD_DIM) < (HEAD_DIM // 2)
    return jnp.where(low, pltpu.roll(v, LANES - HEAD_DIM // 2, 1), pltpu.roll(v, HEAD_DIM // 2, 1))


def _rope(v, cos, sin_signed):
    return v * cos + _partner(v) * sin_signed


def _rope_transposed(dv, cos, sin_signed):
    return dv * cos - _partner(dv) * sin_signed


def _both_halves(v, kv_head):
    keep = (_lane_iota(v.shape) >= HEAD_DIM) if kv_head else (_lane_iota(v.shape) < HEAD_DIM)
    return jnp.where(keep, v, pltpu.roll(v, HEAD_DIM, 1))


def _fold_halves(acc):
    return acc + pltpu.roll(acc, HEAD_DIM, 1)


def _by_half(a, b):
    shape = jnp.broadcast_shapes(jnp.shape(a), jnp.shape(b))
    return jnp.where(_lane_iota(shape) < HEAD_DIM, a, b)


def _stack_heads(pair):
    return jnp.concatenate([_by_half(pair, 0.0), _by_half(0.0, pair)], axis=0)


def _band_mask(has_prev):
    i = lax.broadcasted_iota(jnp.int32, (2 * CHUNK, 2 * CHUNK), 0) % CHUNK
    j = lax.broadcasted_iota(jnp.int32, (2 * CHUNK, 2 * CHUNK), 1)
    band = jnp.logical_and(j > i, j <= i + CHUNK)
    return jnp.logical_and(band, jnp.logical_or(j >= CHUNK, has_prev))


def _probs(qm2, kk2, valid, sink_col):
    sc = _dot(qm2, kk2, 1, 1) * SCALE
    sc = jnp.where(valid, sc, NEG)
    mx = jnp.maximum(jnp.max(sc, axis=-1, keepdims=True), sink_col)
    p = jnp.exp(sc - mx)
    es = jnp.exp(sink_col - mx)
    inv = 1.0 / (jnp.sum(p, axis=-1, keepdims=True) + es)
    return p * inv, es * inv


def _sink_col(sinks_ref, pair):
    row = lax.broadcasted_iota(jnp.int32, (2 * CHUNK, 1), 0)
    return jnp.where(row < CHUNK, sinks_ref[2 * pair], sinks_ref[2 * pair + 1])


def _layer_norm_parts(v):
    mu = jnp.mean(v, axis=-1, keepdims=True)
    xc = v - mu
    rstd = lax.rsqrt(jnp.mean(xc * xc, axis=-1, keepdims=True) + EPS)
    return xc * rstd, rstd


def _masked_spatial(w_ref, g):
    t = lax.broadcasted_iota(jnp.int32, (CHUNK, CHUNK), 0)
    sidx = lax.broadcasted_iota(jnp.int32, (CHUNK, CHUNK), 1)
    return jnp.where(t >= sidx, w_ref[g], 0.0).astype(BF16)


def _keys_values(kv_ref, kvp_ref, pos_ref, posp_ref, freq, sign):
    cos_c, sin_c = _rope_tables(pos_ref[...], freq, sign)
    cos_p, sin_p = _rope_tables(posp_ref[...], freq, sign)
    k_c = _rope(kv_ref[:, :D_KV].astype(F32), cos_c, sin_c)
    k_p = _rope(kvp_ref[:, :D_KV].astype(F32), cos_p, sin_p)
    keys = jnp.concatenate([k_p, k_c], axis=0)
    vals = jnp.concatenate([kvp_ref[:, D_KV:], kv_ref[:, D_KV:]], axis=0).astype(F32)
    return keys, vals, (cos_c, sin_c, cos_p, sin_p)


def mixer_fwd(proj, pos_col, freq, sign, ln_g, ln_b, w_sp, b_sp_rows, sinks):
    s = proj.shape[0]
    nb = s // CHUNK

    def body(sinks_ref, proj_ref, kvp_ref, pos_ref, posp_ref, freq_ref, sign_ref, lng_ref, lnb_ref, w_ref, b_ref,
             cat_ref):
        n = pl.program_id(0)
        freq, sign = freq_ref[...], sign_ref[...]
        xhat, _ = _layer_norm_parts(proj_ref[:, OFF_V:OFF_V + D_GMLP].astype(F32))
        vnb = (xhat * lng_ref[...] + lnb_ref[...]).astype(BF16)
        for g in range(GROUPS):
            cols = slice(g * CHUNK, (g + 1) * CHUNK)
            mixed = _dot(_masked_spatial(w_ref, g), vnb[:, cols], 1, 0) + b_ref[g]
            za = proj_ref[:, OFF_ZA + g * CHUNK:OFF_ZA + (g + 1) * CHUNK].astype(F32)
            u = proj_ref[:, OFF_U + g * CHUNK:OFF_U + (g + 1) * CHUNK].astype(F32)
            cat_ref[:, cols] = (u * mixed * (za * _sigmoid(za))).astype(BF16)
        kv_ref = proj_ref.at[:, OFF_K:OFF_K + 2 * D_KV]
        keys, vals, (cos_c, sin_c, _, _) = _keys_values(kv_ref, kvp_ref, pos_ref, posp_ref, freq, sign)
        valid = _band_mask(n > 0)
        for pair in range(N_PAIRS):
            kvh = pair // PAIRS_PER_KV
            kk2 = _both_halves(keys, kvh).astype(BF16)
            vv2 = _both_halves(vals, kvh).astype(BF16)
            qcols = slice(OFF_Q + pair * LANES, OFF_Q + (pair + 1) * LANES)
            q_pair = _rope(proj_ref[:, qcols].astype(F32), cos_c, sin_c)
            p, _ = _probs(_stack_heads(q_pair).astype(BF16), kk2, valid, _sink_col(sinks_ref, pair))
            o2 = _dot(p.astype(BF16), vv2, 1, 0)
            out_pair = _by_half(o2[:CHUNK], o2[CHUNK:])
            zb = proj_ref[:, OFF_ZB + pair * LANES:OFF_ZB + (pair + 1) * LANES].astype(F32)
            cat_ref[:, D_GMLP + pair * LANES:D_GMLP + (pair + 1) * LANES] = (
                out_pair * (zb * _sigmoid(zb))).astype(BF16)

    prev = lambda n, *_: (jnp.maximum(n - 1, 0), 0)
    kv_block = OFF_K // (2 * D_KV)
    return pl.pallas_call(
        body, name="mixer_fwd",
        grid_spec=pltpu.PrefetchScalarGridSpec(
            num_scalar_prefetch=1, grid=(nb,),
            in_specs=[pl.BlockSpec((CHUNK, D_IN), lambda n, *_: (n, 0)),
                      pl.BlockSpec((CHUNK, 2 * D_KV), lambda n, *_: (jnp.maximum(n - 1, 0), kv_block)),
                      pl.BlockSpec((CHUNK, 1), lambda n, *_: (n, 0)),
                      pl.BlockSpec((CHUNK, 1), prev),
                      pl.BlockSpec((1, LANES), lambda n, *_: (0, 0)),
                      pl.BlockSpec((1, LANES), lambda n, *_: (0, 0)),
                      pl.BlockSpec((1, D_GMLP), lambda n, *_: (0, 0)),
                      pl.BlockSpec((1, D_GMLP), lambda n, *_: (0, 0)),
                      pl.BlockSpec((GROUPS, CHUNK, CHUNK), lambda n, *_: (0, 0, 0)),
                      pl.BlockSpec((GROUPS, CHUNK, CHUNK), lambda n, *_: (0, 0, 0))],
            out_specs=pl.BlockSpec((CHUNK, D_GMLP + D_ATTN), lambda n, *_: (n, 0))),
        out_shape=jax.ShapeDtypeStruct((s, D_GMLP + D_ATTN), BF16),
        compiler_params=_cparams(("arbitrary",)),
    )(sinks, proj, proj, pos_col, pos_col, freq, sign, ln_g, ln_b, w_sp, b_sp_rows)


def mixer_bwd(proj, dcat, pos_col, freq, sign, ln_g, ln_b, w_sp, b_sp_rows, sinks):
    s = proj.shape[0]
    nb = s // CHUNK

    def body(sinks_ref, proj_ref, kvp_ref, dcat_ref, pos_ref, posp_ref, freq_ref, sign_ref, lng_ref, lnb_ref,
             w_ref, b_ref,
             dproj_ref, dw_ref, dbt_ref, dlng_ref, dlnb_ref, dsink_ref,
             pend_ref, pend_kv_ref, dbacc_ref):
        n = pl.program_id(0)

        @pl.when(n == 0)
        def _():
            dw_ref[...] = jnp.zeros_like(dw_ref)
            dbacc_ref[...] = jnp.zeros_like(dbacc_ref)
            dlng_ref[...] = jnp.zeros_like(dlng_ref)
            dlnb_ref[...] = jnp.zeros_like(dlnb_ref)
            dsink_ref[...] = jnp.zeros_like(dsink_ref)

        def flush(dkv_prev):
            @pl.when(n > 0)
            def _():
                dproj_ref[...] = pend_ref[...]
                dproj_ref[:, OFF_K:OFF_K + 2 * D_KV] = (pend_kv_ref[...] + dkv_prev).astype(BF16)

        @pl.when(n < nb)
        def _():
            freq, sign = freq_ref[...], sign_ref[...]
            kv_ref = proj_ref.at[:, OFF_K:OFF_K + 2 * D_KV]
            keys, vals, (cos_c, sin_c, cos_p, sin_p) = _keys_values(kv_ref, kvp_ref, pos_ref, posp_ref, freq, sign)
            valid = _band_mask(n > 0)
            lane_row = _lane_iota((1, LANES))
            dsink = jnp.zeros((1, LANES), F32)
            dk_heads, dv_heads = [], []
            dq_pairs, dzb_pairs = [], []
            for kvh in range(N_KV_HEADS):
                kk2 = _both_halves(keys, kvh).astype(BF16)
                vv2 = _both_halves(vals, kvh).astype(BF16)
                dkk = jnp.zeros((2 * CHUNK, LANES), F32)
                dvv = jnp.zeros((2 * CHUNK, LANES), F32)
                for pair in range(kvh * PAIRS_PER_KV, (kvh + 1) * PAIRS_PER_KV):
                    qcols = slice(OFF_Q + pair * LANES, OFF_Q + (pair + 1) * LANES)
                    q_pair = _rope(proj_ref[:, qcols].astype(F32), cos_c, sin_c)
                    qm2 = _stack_heads(q_pair).astype(BF16)
                    p, p_sink = _probs(qm2, kk2, valid, _sink_col(sinks_ref, pair))
                    pb = p.astype(BF16)
                    o2 = _dot(pb, vv2, 1, 0)
                    out_pair = _by_half(o2[:CHUNK], o2[CHUNK:])
                    zb = proj_ref[:, OFF_ZB + pair * LANES:OFF_ZB + (pair + 1) * LANES].astype(F32)
                    sg = _sigmoid(zb)
                    dyb = dcat_ref[:, D_GMLP + pair * LANES:D_GMLP + (pair + 1) * LANES].astype(F32)
                    dzb_pairs.append((dyb * out_pair * (sg * (1.0 + zb * (1.0 - sg)))).astype(BF16))
                    dom2 = _stack_heads(dyb * (zb * sg)).astype(BF16)
                    dp = _dot(dom2, vv2, 1, 1)
                    delta = jnp.sum(p * dp, axis=-1, keepdims=True)
                    ds = p * (dp - delta)
                    dsk = -(p_sink * delta)
                    dsink = dsink + jnp.where(lane_row == 2 * pair,
                                              jnp.sum(dsk[:CHUNK], axis=0, keepdims=True), 0.0)
                    dsink = dsink + jnp.where(lane_row == 2 * pair + 1,
                                              jnp.sum(dsk[CHUNK:], axis=0, keepdims=True), 0.0)
                    dsb = (ds * SCALE).astype(BF16)
                    dq2 = _dot(dsb, kk2, 1, 0)
                    dq_pairs.append(_rope_transposed(_by_half(dq2[:CHUNK], dq2[CHUNK:]), cos_c, sin_c).astype(BF16))
                    dkk = dkk + _dot(dsb, qm2, 0, 0)
                    dvv = dvv + _dot(pb, dom2, 0, 0)
                dk_heads.append(_fold_halves(dkk))
                dv_heads.append(_fold_halves(dvv))
            dk_rot = _by_half(dk_heads[0], dk_heads[1])
            dv_all = _by_half(dv_heads[0], dv_heads[1])
            dk_p = _rope_transposed(dk_rot[:CHUNK], cos_p, sin_p)
            dk_c = _rope_transposed(dk_rot[CHUNK:], cos_c, sin_c)
            flush(jnp.concatenate([dk_p, dv_all[:CHUNK]], axis=1))
            dsink_ref[...] += dsink
            pend_kv_ref[...] = jnp.concatenate([dk_c, dv_all[CHUNK:]], axis=1)
            for pair in range(N_PAIRS):
                pend_ref[:, OFF_Q + pair * LANES:OFF_Q + (pair + 1) * LANES] = dq_pairs[pair]
                pend_ref[:, OFF_ZB + pair * LANES:OFF_ZB + (pair + 1) * LANES] = dzb_pairs[pair]
            xhat, rstd = _layer_norm_parts(proj_ref[:, OFF_V:OFF_V + D_GMLP].astype(F32))
            lng = lng_ref[...]
            vnb = (xhat * lng + lnb_ref[...]).astype(BF16)
            dvn_cols = []
            for g in range(GROUPS):
                cols = slice(g * CHUNK, (g + 1) * CHUNK)
                wm = _masked_spatial(w_ref, g)
                mixed = _dot(wm, vnb[:, cols], 1, 0) + b_ref[g]
                za = proj_ref[:, OFF_ZA + g * CHUNK:OFF_ZA + (g + 1) * CHUNK].astype(F32)
                u = proj_ref[:, OFF_U + g * CHUNK:OFF_U + (g + 1) * CHUNK].astype(F32)
                dya = dcat_ref[:, cols].astype(F32)
                sg = _sigmoid(za)
                sz = za * sg
                pend_ref[:, OFF_U + g * CHUNK:OFF_U + (g + 1) * CHUNK] = (dya * mixed * sz).astype(BF16)
                pend_ref[:, OFF_ZA + g * CHUNK:OFF_ZA + (g + 1) * CHUNK] = (
                    dya * u * mixed * (sg * (1.0 + za * (1.0 - sg)))).astype(BF16)
                dmixed = dya * u * sz
                dmb = dmixed.astype(BF16)
                dbacc_ref[g] += dmixed
                dw_ref[g] += _dot(dmb, vnb[:, cols], 1, 1)
                dvn_cols.append(_dot(wm, dmb, 0, 0))
            dvn = jnp.concatenate(dvn_cols, axis=1)
            dlng_ref[...] += jnp.sum(dvn * xhat, axis=0, keepdims=True)
            dlnb_ref[...] += jnp.sum(dvn, axis=0, keepdims=True)
            dxh = dvn * lng
            dv = rstd * (dxh - jnp.mean(dxh, axis=-1, keepdims=True)
                         - xhat * jnp.mean(dxh * xhat, axis=-1, keepdims=True))
            pend_ref[:, OFF_V:OFF_V + D_GMLP] = dv.astype(BF16)

        @pl.when(n == nb)
        def _():
            flush(jnp.zeros((CHUNK, 2 * D_KV), F32))
            t = lax.broadcasted_iota(jnp.int32, (CHUNK, CHUNK), 0)
            sidx = lax.broadcasted_iota(jnp.int32, (CHUNK, CHUNK), 1)
            lane = _lane_iota((CHUNK, LANES))
            dbt = jnp.zeros((CHUNK, LANES), F32)
            for g in range(GROUPS):
                dw_ref[g] = jnp.where(t >= sidx, dw_ref[g], 0.0)
                dbt = jnp.where(lane == g, jnp.sum(dbacc_ref[g], axis=-1, keepdims=True), dbt)
            dbt_ref[...] = dbt

    cur = lambda n, *_: (jnp.minimum(n, nb - 1), 0)
    prev = lambda n, *_: (jnp.clip(n - 1, 0, nb - 1), 0)
    kv_block = OFF_K // (2 * D_KV)
    const2 = lambda n, *_: (0, 0)
    const3 = lambda n, *_: (0, 0, 0)
    return pl.pallas_call(
        body, name="mixer_bwd",
        grid_spec=pltpu.PrefetchScalarGridSpec(
            num_scalar_prefetch=1, grid=(nb + 1,),
            in_specs=[pl.BlockSpec((CHUNK, D_IN), cur),
                      pl.BlockSpec((CHUNK, 2 * D_KV), lambda n, *_: (jnp.clip(n - 1, 0, nb - 1), kv_block)),
                      pl.BlockSpec((CHUNK, D_GMLP + D_ATTN), cur),
                      pl.BlockSpec((CHUNK, 1), cur),
                      pl.BlockSpec((CHUNK, 1), prev),
                      pl.BlockSpec((1, LANES), const2),
                      pl.BlockSpec((1, LANES), const2),
                      pl.BlockSpec((1, D_GMLP), const2),
                      pl.BlockSpec((1, D_GMLP), const2),
                      pl.BlockSpec((GROUPS, CHUNK, CHUNK), const3),
                      pl.BlockSpec((GROUPS, CHUNK, CHUNK), const3)],
            out_specs=[pl.BlockSpec((CHUNK, D_IN), lambda n, *_: (jnp.maximum(n - 1, 0), 0)),
                       pl.BlockSpec((GROUPS, CHUNK, CHUNK), const3),
                       pl.BlockSpec((CHUNK, LANES), const2),
                       pl.BlockSpec((1, D_GMLP), const2),
                       pl.BlockSpec((1, D_GMLP), const2),
                       pl.BlockSpec((1, LANES), const2)],
            scratch_shapes=[pltpu.VMEM((CHUNK, D_IN), BF16), pltpu.VMEM((CHUNK, 2 * D_KV), F32),
                            pltpu.VMEM((GROUPS, CHUNK, CHUNK), F32)]),
        out_shape=[jax.ShapeDtypeStruct((s, D_IN), BF16),
                   jax.ShapeDtypeStruct((GROUPS, CHUNK, CHUNK), F32),
                   jax.ShapeDtypeStruct((CHUNK, LANES), F32),
                   jax.ShapeDtypeStruct((1, D_GMLP), F32),
                   jax.ShapeDtypeStruct((1, D_GMLP), F32),
                   jax.ShapeDtypeStruct((1, LANES), F32)],
        compiler_params=_cparams(("arbitrary",)),
    )(sinks, proj, proj, dcat, pos_col, pos_col, freq, sign, ln_g, ln_b, w_sp, b_sp_rows)


def _adamw_math(w, g, m, v):
    m = ADAM_B1 * m + (1.0 - ADAM_B1) * g
    v = ADAM_B2 * v + (1.0 - ADAM_B2) * (g * g)
    m_hat = m / (1.0 - ADAM_B1 ** ADAM_STEP)
    v_hat = v / (1.0 - ADAM_B2 ** ADAM_STEP)
    delta = -ADAM_LR * (m_hat / (jnp.sqrt(v_hat) + ADAM_EPS) + ADAM_WD * w)
    return delta, m, v


def adamw(w, g, m, v, name):
    r, c = w.shape
    tr = _tile(r, (256, 184, 128, 8))

    def body(w_ref, g_ref, m_ref, v_ref, d_ref, nm_ref, nv_ref):
        d_ref[...], nm_ref[...], nv_ref[...] = _adamw_math(w_ref[...], g_ref[...], m_ref[...], v_ref[...])

    spec = pl.BlockSpec((tr, c), lambda i: (i, 0))
    return pl.pallas_call(
        body, name=name, grid=(r // tr,), in_specs=[spec] * 4, out_specs=[spec] * 3,
        out_shape=[jax.ShapeDtypeStruct((r, c), F32)] * 3,
        compiler_params=_cparams(("arbitrary",)),
    )(w, g, m, v)


def adamw_small(packs, w, m, v):
    r = w.shape[0]

    def body(p_ref, w_ref, m_ref, v_ref, g_ref, d_ref, nm_ref, nv_ref):
        g = p_ref[0:r, :]
        for dev in range(1, N_DEV):
            g = g + p_ref[dev * r:(dev + 1) * r, :]
        g_ref[...] = g
        d_ref[...], nm_ref[...], nv_ref[...] = _adamw_math(w_ref[...], g, m_ref[...], v_ref[...])

    return pl.pallas_call(
        body, name="adamw_small", out_shape=[jax.ShapeDtypeStruct((r, LANES), F32)] * 4,
        compiler_params=_cparams(),
    )(packs, w, m, v)


_SMALL = (("g_pre", D_MODEL), ("b_qkv", D_QKV), ("ln_v_g", D_GMLP), ("ln_v_b", D_GMLP),
          ("w_spatial", GROUPS * CHUNK * CHUNK), ("b_spatial", GROUPS * CHUNK), ("attn_sinks", N_Q_HEADS),
          ("g_post", D_MODEL), ("loss", 1))
_TILE_ELEMS = 8 * LANES


def _pack(parts):
    rows = []
    for (_, size), a in zip(_SMALL, parts):
        flat = a.reshape(-1).astype(F32)
        padded = -(-size // _TILE_ELEMS) * _TILE_ELEMS
        rows.append(jnp.pad(flat, (0, padded - size)).reshape(-1, LANES))
    return jnp.concatenate(rows, axis=0)


def _unpack(pack, shapes):
    out, row = [], 0
    for (_, size), shape in zip(_SMALL, shapes):
        nrows = -(-size // _TILE_ELEMS) * 8
        out.append(pack[row:row + nrows].reshape(-1)[:size].reshape(shape))
        row += nrows
    return out


def kernel(x, positions, g_pre, w_in, b_qkv, ln_v_g, ln_v_b, w_spatial, b_spatial, attn_sinks, w_out, g_post, loss_target, m_g_pre, m_w_in, m_b_qkv, m_ln_v_g, m_ln_v_b, m_w_spatial, m_b_spatial, m_attn_sinks, m_w_out, m_g_post, v_g_pre, v_w_in, v_b_qkv, v_ln_v_g, v_ln_v_b, v_w_spatial, v_b_spatial, v_attn_sinks, v_w_out, v_g_post):
    x2, target = x[0], loss_target[0]
    seq = x2.shape[0]
    xi, yi, ci = _my_place()

    wt_shard = w_in[0].T.astype(BF16)
    wo_shard = w_out[0].astype(BF16)
    wt, wo = allgather_rows([wt_shard, wo_shard], "allgather_weights")

    bias = jnp.concatenate([jnp.zeros((1, OFF_Q), F32), b_qkv, jnp.zeros((1, D_ATTN), F32)], axis=1)
    proj, h = in_proj(x2, g_pre, wt, bias)
    pos_col = positions.reshape(seq, 1)
    half = HEAD_DIM // 2
    inv_freq = ROPE_THETA ** (-jnp.arange(half, dtype=F32) * (2.0 / HEAD_DIM))
    freq = jnp.tile(inv_freq, LANES // half).reshape(1, LANES)
    sign = jnp.tile(jnp.concatenate([-jnp.ones((half,), F32), jnp.ones((half,), F32)]), LANES // HEAD_DIM)
    sign = sign.reshape(1, LANES)
    b_rows = jnp.broadcast_to(b_spatial[0][:, :, None], (GROUPS, CHUNK, CHUNK))
    sinks = attn_sinks[0]
    cat = mixer_fwd(proj, pos_col, freq, sign, ln_v_g, ln_v_b, w_spatial[0], b_rows, sinks)
    dy, dout, d_g_post, loss_part = out_proj_loss(cat, wo, x2, target, g_post)

    dcat = matmul_nt(dy, wo, "out_proj_bwd")
    d_wo, _ = matmul_tn(cat, dy, 512, "w_out_grad")
    dproj, d_w_sp, d_b_sp_t, d_ln_g, d_ln_b, d_sinks = mixer_bwd(
        proj, dcat, pos_col, freq, sign, ln_v_g, ln_v_b, w_spatial[0], b_rows, sinks)
    d_wt, colsum = matmul_tn(dproj, h, 768, "w_in_grad")
    grad_x, d_g_pre = in_proj_bwd(dproj, wt, x2, g_pre, dout)

    owners = jnp.stack([4 * cx + 2 * cy + ci for cx, cy in (_chip_of(xi, yi, r) for r in range(4))]).astype(jnp.int32)
    got_wt, got_wo = exchange_pair([d_wt, d_wo], "grad_exchange_pair")
    sum_wt = pair_sum(d_wt, got_wt, owners, "grad_pair_sum_w_in")
    sum_wo = pair_sum(d_wo, got_wo, owners, "grad_pair_sum_w_out")
    far_wt, far_wo = exchange_chips([sum_wt, sum_wo], "grad_exchange_chips")
    g_wt = chip_sum(sum_wt, far_wt, "grad_chip_sum_w_in")
    g_wo = chip_sum(sum_wo, far_wo, "grad_chip_sum_w_out")
    g_w_in = g_wt.T

    d_b_qkv = colsum[:, OFF_Q:OFF_Q + D_QKV]
    d_b_sp = d_b_sp_t[:, :GROUPS].T
    small = [d_g_pre, d_b_qkv, d_ln_g, d_ln_b, d_w_sp, d_b_sp, d_sinks[:, :N_Q_HEADS], d_g_post, loss_part[:, :1]]
    (packs,) = allgather_rows([_pack(small)], "allgather_small_grads")
    zero = jnp.zeros((1, 1), F32)
    w_pack = _pack([g_pre, b_qkv, ln_v_g, ln_v_b, w_spatial, b_spatial, attn_sinks, g_post, zero])
    m_pack = _pack([m_g_pre, m_b_qkv, m_ln_v_g, m_ln_v_b, m_w_spatial, m_b_spatial, m_attn_sinks, m_g_post, zero])
    v_pack = _pack([v_g_pre, v_b_qkv, v_ln_v_g, v_ln_v_b, v_w_spatial, v_b_spatial, v_attn_sinks, v_g_post, zero])
    g_pack, d_pack, nm_pack, nv_pack = adamw_small(packs, w_pack, m_pack, v_pack)

    d_w_in, nm_w_in, nv_w_in = adamw(w_in[0], g_w_in, m_w_in[0], v_w_in[0], "adamw_w_in")
    d_w_out, nm_w_out, nv_w_out = adamw(w_out[0], g_wo, m_w_out[0], v_w_out[0], "adamw_w_out")

    shapes = [g_pre.shape, b_qkv.shape, ln_v_g.shape, ln_v_b.shape, w_spatial.shape, b_spatial.shape,
              attn_sinks.shape, g_post.shape, ()]
    grads = _unpack(g_pack, shapes)
    deltas = _unpack(d_pack, shapes)
    new_ms = _unpack(nm_pack, shapes)
    new_vs = _unpack(nv_pack, shapes)

    def assemble(small_list, w_in_leaf, w_out_leaf):
        gp, bq, lg, lb, ws, bs, sk, gpo, _ = small_list
        return [gp, w_in_leaf[None], bq, lg, lb, ws, bs, sk, w_out_leaf[None], gpo]

    loss = grads[-1]
    return (loss, grad_x[None],
            *assemble(grads, g_w_in, g_wo),
            *assemble(deltas, d_w_in, d_w_out),
            *assemble(new_ms, nm_w_in, nm_w_out),
            *assemble(new_vs, nv_w_in, nv_w_out))
```

```python
import functools

import jax
import jax.numpy as jnp
from jax import lax
from jax.experimental import pallas as pl
from jax.experimental.pallas import tpu as pltpu

F32 = jnp.float32
BF16 = jnp.bfloat16

D_MODEL = 2048
D_GMLP = 1024
D_ATTN = 1024
CHUNK = 128
GROUPS = 8
HEAD_DIM = 64
N_Q_HEADS = 16
N_KV_HEADS = 2
D_KV = N_KV_HEADS * HEAD_DIM
D_IN = 3 * D_GMLP + D_ATTN + 2 * D_KV + D_ATTN
OFF_U, OFF_V, OFF_ZA = 0, D_GMLP, 2 * D_GMLP
OFF_Q = 3 * D_GMLP
OFF_K = OFF_Q + D_ATTN
OFF_VA = OFF_K + D_KV
OFF_ZB = OFF_VA + D_KV
D_QKV = D_ATTN + 2 * D_KV
ROPE_THETA = 10000.0
EPS = 1e-6
SCALE = HEAD_DIM ** -0.5
NEG = -1e30
N_PAIRS = N_Q_HEADS // 2
PAIRS_PER_KV = N_PAIRS // N_KV_HEADS

ADAM_LR = 0.001
ADAM_B1 = 0.9
ADAM_B2 = 0.999
ADAM_EPS = 1e-08
ADAM_WD = 0.01
ADAM_STEP = 10

N_DEV = 8
LANES = 128
VMEM_LIMIT = 56 * 1024 * 1024

MESH = pl.DeviceIdType.MESH
ANY = pl.BlockSpec(memory_space=pl.ANY)


def _cparams(sem=None):
    return pltpu.CompilerParams(dimension_semantics=sem, vmem_limit_bytes=VMEM_LIMIT)


def _tile(n, prefs):
    for t in prefs:
        if n % t == 0:
            return t
    return n


def _sigmoid(z):
    return 1.0 / (1.0 + jnp.exp(-z))


def _dot(a, b, ca, cb):
    return lax.dot_general(a, b, (((ca,), (cb,)), ((), ())), preferred_element_type=F32)


def _my_place():
    return lax.axis_index("x"), lax.axis_index("y"), lax.axis_index("c")


def _chip_of(x, y, r):
    return (x ^ (r & 1), y ^ (r >> 1))


def allgather(shards, name, stack=False):
    n_arr = len(shards)

    def body(*refs):
        ins, outs = refs[:n_arr], refs[n_arr:2 * n_arr]
        send_sems, recv_sems, local_sems = refs[2 * n_arr:]
        x, y, c = _my_place()
        me, sibling = (x, y, c), (x, y, 1 - c)
        chips = [_chip_of(x, y, r) for r in (1, 2, 3)]

        def rows(a, px, py, pc):
            d = 4 * px + 2 * py + pc
            if stack:
                return outs[a].at[d]
            m = shards[a].shape[0]
            return outs[a].at[pl.ds(pl.multiple_of(d * m, 8), m), :]

        def copy(a, k, block, to, src=None):
            return pltpu.make_async_remote_copy(
                src_ref=rows(a, *block) if src is None else src, dst_ref=rows(a, *block),
                send_sem=send_sems.at[a * 7 + k], recv_sem=recv_sems.at[a * 7 + k],
                device_id=to, device_id_type=MESH)

        mine = [pltpu.make_async_copy(ins[a], rows(a, *me), local_sems.at[a]) for a in range(n_arr)]
        for cp in mine:
            cp.start()
        first = []
        for a in range(n_arr):
            first.append(copy(a, 0, me, sibling, src=ins[a]))
            first += [copy(a, 1 + j, me, (*chip, c), src=ins[a]) for j, chip in enumerate(chips)]
        for cp in first:
            cp.start()
        passed = []
        for j, chip in enumerate(chips):
            for a in range(n_arr):
                copy(a, 1 + j, (*chip, c), me).wait_recv()
                fwd = copy(a, 4 + j, (*chip, c), sibling)
                fwd.start()
                passed.append(fwd)
        for a in range(n_arr):
            copy(a, 0, sibling, me).wait_recv()
            for j, chip in enumerate(chips):
                copy(a, 4 + j, (*chip, 1 - c), me).wait_recv()
        for cp in first + passed:
            cp.wait_send()
        for cp in mine:
            cp.wait()

    def gathered(s):
        return (N_DEV, *s.shape) if stack else (N_DEV * s.shape[0], s.shape[1])

    outs = pl.pallas_call(
        body, name=name,
        out_shape=[jax.ShapeDtypeStruct(gathered(s), s.dtype) for s in shards],
        in_specs=[ANY] * n_arr, out_specs=[ANY] * n_arr,
        scratch_shapes=[pltpu.SemaphoreType.DMA((7 * n_arr,)), pltpu.SemaphoreType.DMA((7 * n_arr,)),
                        pltpu.SemaphoreType.DMA((n_arr,))],
    )(*shards)
    return list(outs)


def exchange_pair(parts, name):
    n_arr = len(parts)

    def body(*refs):
        ins, outs = refs[:n_arr], refs[n_arr:2 * n_arr]
        send_sems, recv_sems = refs[2 * n_arr:]
        x, y, c = _my_place()
        copies = []
        for a in range(n_arr):
            m = parts[a].shape[0] // N_DEV
            for r in range(4):
                cx, cy = _chip_of(x, y, r)
                owner = 4 * cx + 2 * cy + (1 - c)
                copies.append(pltpu.make_async_remote_copy(
                    src_ref=ins[a].at[pl.ds(pl.multiple_of(owner * m, 16), m), :], dst_ref=outs[a].at[r],
                    send_sem=send_sems.at[a * 4 + r], recv_sem=recv_sems.at[a * 4 + r],
                    device_id=(x, y, 1 - c), device_id_type=MESH))
        for cp in copies:
            cp.start()
        for cp in copies:
            cp.wait_recv()
        for cp in copies:
            cp.wait_send()

    outs = pl.pallas_call(
        body, name=name,
        out_shape=[jax.ShapeDtypeStruct((4, p.shape[0] // N_DEV, p.shape[1]), p.dtype) for p in parts],
        in_specs=[ANY] * n_arr, out_specs=[ANY] * n_arr,
        scratch_shapes=[pltpu.SemaphoreType.DMA((4 * n_arr,)), pltpu.SemaphoreType.DMA((4 * n_arr,))],
    )(*parts)
    return list(outs)


def exchange_chips(sums, name):
    n_arr = len(sums)

    def body(*refs):
        ins, outs = refs[:n_arr], refs[n_arr:2 * n_arr]
        send_sems, recv_sems = refs[2 * n_arr:]
        x, y, c = _my_place()
        copies = []
        for a in range(n_arr):
            for r in (1, 2, 3):
                copies.append(pltpu.make_async_remote_copy(
                    src_ref=ins[a].at[r], dst_ref=outs[a].at[r - 1],
                    send_sem=send_sems.at[a * 3 + r - 1], recv_sem=recv_sems.at[a * 3 + r - 1],
                    device_id=(*_chip_of(x, y, r), c), device_id_type=MESH))
        for cp in copies:
            cp.start()
        for cp in copies:
            cp.wait_recv()
        for cp in copies:
            cp.wait_send()

    outs = pl.pallas_call(
        body, name=name,
        out_shape=[jax.ShapeDtypeStruct((3,) + s.shape[1:], s.dtype) for s in sums],
        in_specs=[ANY] * n_arr, out_specs=[ANY] * n_arr,
        scratch_shapes=[pltpu.SemaphoreType.DMA((3 * n_arr,)), pltpu.SemaphoreType.DMA((3 * n_arr,))],
    )(*sums)
    return list(outs)


def pair_sum(part, got, owners, name):
    m, n = got.shape[1:]

    def body(own_ref, mine_ref, got_ref, out_ref):
        del own_ref
        out_ref[...] = (mine_ref[...].astype(F32) + got_ref[...].astype(F32)).astype(out_ref.dtype)

    return pl.pallas_call(
        body, name=name,
        grid_spec=pltpu.PrefetchScalarGridSpec(
            num_scalar_prefetch=1, grid=(4,),
            in_specs=[pl.BlockSpec((m, n), lambda r, own: (own[r], 0)),
                      pl.BlockSpec((None, m, n), lambda r, own: (r, 0, 0))],
            out_specs=pl.BlockSpec((None, m, n), lambda r, own: (r, 0, 0))),
        out_shape=jax.ShapeDtypeStruct((4, m, n), got.dtype),
        compiler_params=_cparams(("arbitrary",)),
    )(owners, part, got)


def in_proj(x, g_pre, wt, bias):
    s, d = x.shape
    tm = _tile(s, (512, 256, 128))
    tn = 768

    def body(x_ref, g_ref, w_ref, b_ref, proj_ref, h_ref):
        @pl.when(pl.program_id(1) == 0)
        def _():
            xv = x_ref[...]
            r = lax.rsqrt(jnp.mean(xv * xv, axis=-1, keepdims=True) + EPS)
            h_ref[...] = (xv * r * g_ref[...]).astype(BF16)

        acc = _dot(h_ref[...], w_ref[...], 1, 1)
        proj_ref[...] = (acc + b_ref[...]).astype(BF16)

    return pl.pallas_call(
        body, name="in_proj", grid=(s // tm, D_IN // tn),
        in_specs=[pl.BlockSpec((tm, d), lambda i, j: (i, 0)),
                  pl.BlockSpec((1, d), lambda i, j: (0, 0)),
                  pl.BlockSpec((tn, d), lambda i, j: (j, 0)),
                  pl.BlockSpec((1, tn), lambda i, j: (0, j))],
        out_specs=[pl.BlockSpec((tm, tn), lambda i, j: (i, j)),
                   pl.BlockSpec((tm, d), lambda i, j: (i, 0))],
        out_shape=[jax.ShapeDtypeStruct((s, D_IN), BF16), jax.ShapeDtypeStruct((s, d), BF16)],
        compiler_params=_cparams(("arbitrary", "arbitrary")),
    )(x, g_pre, wt, bias)


def out_proj_loss(cat, w_out, x, target, g_post):
    s, d = x.shape
    tm = _tile(s, (256, 128))

    def body(cat_ref, w_ref, x_ref, t_ref, g_ref, dy_ref, dout_ref, dg_ref, loss_ref):
        @pl.when(pl.program_id(0) == 0)
        def _():
            dg_ref[...] = jnp.zeros_like(dg_ref)
            loss_ref[...] = jnp.zeros_like(loss_ref)

        yv = _dot(cat_ref[...], w_ref[...], 1, 0)
        r = lax.rsqrt(jnp.mean(yv * yv, axis=-1, keepdims=True) + EPS)
        nrm = yv * r
        g = g_ref[...]
        err = x_ref[...] + nrm * g - t_ref[...]
        loss_ref[...] += 0.5 * jnp.sum(jnp.sum(err * err, axis=-1, keepdims=True), axis=0, keepdims=True) / d
        dout = err * (1.0 / d)
        dout_ref[...] = dout
        dg_ref[...] += jnp.sum(dout * nrm, axis=0, keepdims=True)
        dn = dout * g
        dy = r * (dn - nrm * jnp.mean(dn * nrm, axis=-1, keepdims=True))
        dy_ref[...] = dy.astype(BF16)

    return pl.pallas_call(
        body, name="out_proj_loss", grid=(s // tm,),
        in_specs=[pl.BlockSpec((tm, d), lambda i: (i, 0)),
                  pl.BlockSpec((d, d), lambda i: (0, 0)),
                  pl.BlockSpec((tm, d), lambda i: (i, 0)),
                  pl.BlockSpec((tm, d), lambda i: (i, 0)),
                  pl.BlockSpec((1, d), lambda i: (0, 0))],
        out_specs=[pl.BlockSpec((tm, d), lambda i: (i, 0)),
                   pl.BlockSpec((tm, d), lambda i: (i, 0)),
                   pl.BlockSpec((1, d), lambda i: (0, 0)),
                   pl.BlockSpec((1, LANES), lambda i: (0, 0))],
        out_shape=[jax.ShapeDtypeStruct((s, d), BF16), jax.ShapeDtypeStruct((s, d), F32),
                   jax.ShapeDtypeStruct((1, d), F32), jax.ShapeDtypeStruct((1, LANES), F32)],
        compiler_params=_cparams(("arbitrary",)),
    )(cat, w_out, x, target, g_post)


def matmul_nt(a, b, name):
    m, k = a.shape
    n = b.shape[0]
    tm = _tile(m, (512, 256, 128))

    def body(a_ref, b_ref, o_ref):
        o_ref[...] = _dot(a_ref[...], b_ref[...], 1, 1).astype(o_ref.dtype)

    return pl.pallas_call(
        body, name=name, grid=(m // tm,),
        in_specs=[pl.BlockSpec((tm, k), lambda i: (i, 0)), pl.BlockSpec((n, k), lambda i: (0, 0))],
        out_specs=pl.BlockSpec((tm, n), lambda i: (i, 0)),
        out_shape=jax.ShapeDtypeStruct((m, n), BF16),
        compiler_params=_cparams(("arbitrary",)),
    )(a, b)


def matmul_tn(a, b, tm, name):
    k, m = a.shape
    n = b.shape[1]
    tk = _tile(k, (1024, 512, 256, 128))
    nk = k // tk

    def body(a_ref, b_ref, o_ref, cs_ref, acc_ref, csacc_ref):
        kk = pl.program_id(1)

        @pl.when(kk == 0)
        def _():
            acc_ref[...] = jnp.zeros_like(acc_ref)
            csacc_ref[...] = jnp.zeros_like(csacc_ref)

        av = a_ref[...]
        acc_ref[...] += _dot(av, b_ref[...], 0, 0)
        csacc_ref[...] += jnp.sum(av.astype(F32), axis=0, keepdims=True)

        @pl.when(kk == nk - 1)
        def _():
            o_ref[...] = acc_ref[...].astype(o_ref.dtype)
            cs_ref[...] = csacc_ref[...]

    return pl.pallas_call(
        body, name=name, grid=(m // tm, nk),
        in_specs=[pl.BlockSpec((tk, tm), lambda i, j: (j, i)), pl.BlockSpec((tk, n), lambda i, j: (j, 0))],
        out_specs=[pl.BlockSpec((tm, n), lambda i, j: (i, 0)), pl.BlockSpec((1, tm), lambda i, j: (0, i))],
        out_shape=[jax.ShapeDtypeStruct((m, n), BF16), jax.ShapeDtypeStruct((1, m), F32)],
        scratch_shapes=[pltpu.VMEM((tm, n), F32), pltpu.VMEM((1, tm), F32)],
        compiler_params=_cparams(("arbitrary", "arbitrary")),
    )(a, b)


def in_proj_bwd(dproj, wt, x, g_pre, dout):
    s, d = x.shape
    tm = _tile(s, (512, 256, 128))
    tk = 768
    nk = D_IN // tk

    def body(dp_ref, w_ref, x_ref, g_ref, dout_ref, gx_ref, dg_ref, acc_ref):
        i, kk = pl.program_id(0), pl.program_id(1)

        @pl.when(jnp.logical_and(i == 0, kk == 0))
        def _():
            dg_ref[...] = jnp.zeros_like(dg_ref)

        @pl.when(kk == 0)
        def _():
            acc_ref[...] = jnp.zeros_like(acc_ref)

        acc_ref[...] += _dot(dp_ref[...], w_ref[...], 1, 0)

        @pl.when(kk == nk - 1)
        def _():
            for c0 in range(0, tm, CHUNK):
                rows = slice(c0, c0 + CHUNK)
                dh = acc_ref[rows, :]
                xv = x_ref[rows, :]
                r = lax.rsqrt(jnp.mean(xv * xv, axis=-1, keepdims=True) + EPS)
                xn = xv * r
                dg_ref[...] += jnp.sum(dh * xn, axis=0, keepdims=True)
                dn = dh * g_ref[...]
                gx_ref[rows, :] = dout_ref[rows, :] + r * (dn - xn * jnp.mean(dn * xn, axis=-1, keepdims=True))

    return pl.pallas_call(
        body, name="in_proj_bwd", grid=(s // tm, nk),
        in_specs=[pl.BlockSpec((tm, tk), lambda i, j: (i, j)),
                  pl.BlockSpec((tk, d), lambda i, j: (j, 0)),
                  pl.BlockSpec((tm, d), lambda i, j: (i, 0)),
                  pl.BlockSpec((1, d), lambda i, j: (0, 0)),
                  pl.BlockSpec((tm, d), lambda i, j: (i, 0))],
        out_specs=[pl.BlockSpec((tm, d), lambda i, j: (i, 0)), pl.BlockSpec((1, d), lambda i, j: (0, 0))],
        out_shape=[jax.ShapeDtypeStruct((s, d), F32), jax.ShapeDtypeStruct((1, d), F32)],
        scratch_shapes=[pltpu.VMEM((tm, d), F32)],
        compiler_params=_cparams(("arbitrary", "arbitrary")),
    )(dproj, wt, x, g_pre, dout)


def _lane_iota(shape):
    return lax.broadcasted_iota(jnp.int32, shape, len(shape) - 1)


def _rope_tables(pos_col, freq, sign):
    ang = pos_col.astype(F32) * freq
    return jnp.cos(ang), jnp.sin(ang) * sign


def _partner(v):
    low = (_lane_iota(v.shape) % HEAD_DIM) < (HEAD_DIM // 2)
    return jnp.where(low, pltpu.roll(v, LANES - HEAD_DIM // 2, 1), pltpu.roll(v, HEAD_DIM // 2, 1))


def _rope(v, cos, sin_signed):
    return v * cos + _partner(v) * sin_signed


def _rope_transposed(dv, cos, sin_signed):
    return dv * cos - _partner(dv) * sin_signed


def _both_halves(v, kv_head):
    keep = (_lane_iota(v.shape) >= HEAD_DIM) if kv_head else (_lane_iota(v.shape) < HEAD_DIM)
    return jnp.where(keep, v, pltpu.roll(v, HEAD_DIM, 1))


def _fold_halves(acc):
    return acc + pltpu.roll(acc, HEAD_DIM, 1)


def _by_half(a, b):
    shape = jnp.broadcast_shapes(jnp.shape(a), jnp.shape(b))
    return jnp.where(_lane_iota(shape) < HEAD_DIM, a, b)


def _stack_heads(pair):
    return jnp.concatenate([_by_half(pair, 0.0), _by_half(0.0, pair)], axis=0)


def _band_mask(has_prev):
    i = lax.broadcasted_iota(jnp.int32, (2 * CHUNK, 2 * CHUNK), 0) % CHUNK
    j = lax.broadcasted_iota(jnp.int32, (2 * CHUNK, 2 * CHUNK), 1)
    band = jnp.logical_and(j > i, j <= i + CHUNK)
    return jnp.logical_and(band, jnp.logical_or(j >= CHUNK, has_prev))


def _probs(qm2, kk2, valid, sink_col):
    sc = _dot(qm2, kk2, 1, 1) * SCALE
    sc = jnp.where(valid, sc, NEG)
    mx = jnp.maximum(jnp.max(sc, axis=-1, keepdims=True), sink_col)
    p = jnp.exp(sc - mx)
    es = jnp.exp(sink_col - mx)
    inv = 1.0 / (jnp.sum(p, axis=-1, keepdims=True) + es)
    return p * inv, es * inv


def _sink_col(sinks_ref, pair):
    row = lax.broadcasted_iota(jnp.int32, (2 * CHUNK, 1), 0)
    return jnp.where(row < CHUNK, sinks_ref[2 * pair], sinks_ref[2 * pair + 1])


def _layer_norm_parts(v):
    mu = jnp.mean(v, axis=-1, keepdims=True)
    xc = v - mu
    rstd = lax.rsqrt(jnp.mean(xc * xc, axis=-1, keepdims=True) + EPS)
    return xc * rstd, rstd


def _masked_spatial(w_ref, g):
    t = lax.broadcasted_iota(jnp.int32, (CHUNK, CHUNK), 0)
    sidx = lax.broadcasted_iota(jnp.int32, (CHUNK, CHUNK), 1)
    return jnp.where(t >= sidx, w_ref[g], 0.0).astype(BF16)


def _keys_values(kv_ref, kvp_ref, pos_ref, posp_ref, freq, sign):
    cos_c, sin_c = _rope_tables(pos_ref[...], freq, sign)
    cos_p, sin_p = _rope_tables(posp_ref[...], freq, sign)
    k_c = _rope(kv_ref[:, :D_KV].astype(F32), cos_c, sin_c)
    k_p = _rope(kvp_ref[:, :D_KV].astype(F32), cos_p, sin_p)
    keys = jnp.concatenate([k_p, k_c], axis=0)
    vals = jnp.concatenate([kvp_ref[:, D_KV:], kv_ref[:, D_KV:]], axis=0).astype(F32)
    return keys, vals, (cos_c, sin_c, cos_p, sin_p)


def mixer_fwd(proj, pos_col, freq, sign, ln_g, ln_b, w_sp, b_sp_rows, sinks):
    s = proj.shape[0]
    nb = s // CHUNK

    def body(sinks_ref, proj_ref, kvp_ref, pos_ref, posp_ref, freq_ref, sign_ref, lng_ref, lnb_ref, w_ref, b_ref,
             cat_ref):
        n = pl.program_id(0)
        freq, sign = freq_ref[...], sign_ref[...]
        xhat, _ = _layer_norm_parts(proj_ref[:, OFF_V:OFF_V + D_GMLP].astype(F32))
        vnb = (xhat * lng_ref[...] + lnb_ref[...]).astype(BF16)
        for g in range(GROUPS):
            cols = slice(g * CHUNK, (g + 1) * CHUNK)
            mixed = _dot(_masked_spatial(w_ref, g), vnb[:, cols], 1, 0) + b_ref[g]
            za = proj_ref[:, OFF_ZA + g * CHUNK:OFF_ZA + (g + 1) * CHUNK].astype(F32)
            u = proj_ref[:, OFF_U + g * CHUNK:OFF_U + (g + 1) * CHUNK].astype(F32)
            cat_ref[:, cols] = (u * mixed * (za * _sigmoid(za))).astype(BF16)
        kv_ref = proj_ref.at[:, OFF_K:OFF_K + 2 * D_KV]
        keys, vals, (cos_c, sin_c, _, _) = _keys_values(kv_ref, kvp_ref, pos_ref, posp_ref, freq, sign)
        valid = _band_mask(n > 0)
        for pair in range(N_PAIRS):
            kvh = pair // PAIRS_PER_KV
            kk2 = _both_halves(keys, kvh).astype(BF16)
            vv2 = _both_halves(vals, kvh).astype(BF16)
            qcols = slice(OFF_Q + pair * LANES, OFF_Q + (pair + 1) * LANES)
            q_pair = _rope(proj_ref[:, qcols].astype(F32), cos_c, sin_c)
            p, _ = _probs(_stack_heads(q_pair).astype(BF16), kk2, valid, _sink_col(sinks_ref, pair))
            o2 = _dot(p.astype(BF16), vv2, 1, 0)
            out_pair = _by_half(o2[:CHUNK], o2[CHUNK:])
            zb = proj_ref[:, OFF_ZB + pair * LANES:OFF_ZB + (pair + 1) * LANES].astype(F32)
            cat_ref[:, D_GMLP + pair * LANES:D_GMLP + (pair + 1) * LANES] = (
                out_pair * (zb * _sigmoid(zb))).astype(BF16)

    prev = lambda n, *_: (jnp.maximum(n - 1, 0), 0)
    kv_block = OFF_K // (2 * D_KV)
    return pl.pallas_call(
        body, name="mixer_fwd",
        grid_spec=pltpu.PrefetchScalarGridSpec(
            num_scalar_prefetch=1, grid=(nb,),
            in_specs=[pl.BlockSpec((CHUNK, D_IN), lambda n, *_: (n, 0)),
                      pl.BlockSpec((CHUNK, 2 * D_KV), lambda n, *_: (jnp.maximum(n - 1, 0), kv_block)),
                      pl.BlockSpec((CHUNK, 1), lambda n, *_: (n, 0)),
                      pl.BlockSpec((CHUNK, 1), prev),
                      pl.BlockSpec((1, LANES), lambda n, *_: (0, 0)),
                      pl.BlockSpec((1, LANES), lambda n, *_: (0, 0)),
                      pl.BlockSpec((1, D_GMLP), lambda n, *_: (0, 0)),
                      pl.BlockSpec((1, D_GMLP), lambda n, *_: (0, 0)),
                      pl.BlockSpec((GROUPS, CHUNK, CHUNK), lambda n, *_: (0, 0, 0)),
                      pl.BlockSpec((GROUPS, CHUNK, CHUNK), lambda n, *_: (0, 0, 0))],
            out_specs=pl.BlockSpec((CHUNK, D_GMLP + D_ATTN), lambda n, *_: (n, 0))),
        out_shape=jax.ShapeDtypeStruct((s, D_GMLP + D_ATTN), BF16),
        compiler_params=_cparams(("arbitrary",)),
    )(sinks, proj, proj, pos_col, pos_col, freq, sign, ln_g, ln_b, w_sp, b_sp_rows)


def mixer_bwd(proj, dcat, pos_col, freq, sign, ln_g, ln_b, w_sp, b_sp_rows, sinks):
    s = proj.shape[0]
    nb = s // CHUNK

    def body(sinks_ref, proj_ref, kvp_ref, dcat_ref, pos_ref, posp_ref, freq_ref, sign_ref, lng_ref, lnb_ref,
             w_ref, b_ref,
             dproj_ref, dw_ref, db_ref, dlng_ref, dlnb_ref, dsink_ref,
             pend_ref, pend_kv_ref, dbacc_ref):
        n = pl.program_id(0)

        @pl.when(n == 0)
        def _():
            dw_ref[...] = jnp.zeros_like(dw_ref)
            dbacc_ref[...] = jnp.zeros_like(dbacc_ref)
            dlng_ref[...] = jnp.zeros_like(dlng_ref)
            dlnb_ref[...] = jnp.zeros_like(dlnb_ref)
            dsink_ref[...] = jnp.zeros_like(dsink_ref)

        def flush(dkv_prev):
            @pl.when(n > 0)
            def _():
                dproj_ref[...] = pend_ref[...]
                dproj_ref[:, OFF_K:OFF_K + 2 * D_KV] = (pend_kv_ref[...] + dkv_prev).astype(BF16)

        @pl.when(n < nb)
        def _():
            freq, sign = freq_ref[...], sign_ref[...]
            kv_ref = proj_ref.at[:, OFF_K:OFF_K + 2 * D_KV]
            keys, vals, (cos_c, sin_c, cos_p, sin_p) = _keys_values(kv_ref, kvp_ref, pos_ref, posp_ref, freq, sign)
            valid = _band_mask(n > 0)
            lane_row = _lane_iota((1, LANES))
            dsink = jnp.zeros((1, LANES), F32)
            dk_heads, dv_heads = [], []
            dq_pairs, dzb_pairs = [], []
            for kvh in range(N_KV_HEADS):
                kk2 = _both_halves(keys, kvh).astype(BF16)
                vv2 = _both_halves(vals, kvh).astype(BF16)
                dkk = jnp.zeros((2 * CHUNK, LANES), F32)
                dvv = jnp.zeros((2 * CHUNK, LANES), F32)
                for pair in range(kvh * PAIRS_PER_KV, (kvh + 1) * PAIRS_PER_KV):
                    qcols = slice(OFF_Q + pair * LANES, OFF_Q + (pair + 1) * LANES)
                    q_pair = _rope(proj_ref[:, qcols].astype(F32), cos_c, sin_c)
                    qm2 = _stack_heads(q_pair).astype(BF16)
                    p, p_sink = _probs(qm2, kk2, valid, _sink_col(sinks_ref, pair))
                    pb = p.astype(BF16)
                    o2 = _dot(pb, vv2, 1, 0)
                    out_pair = _by_half(o2[:CHUNK], o2[CHUNK:])
                    zb = proj_ref[:, OFF_ZB + pair * LANES:OFF_ZB + (pair + 1) * LANES].astype(F32)
                    sg = _sigmoid(zb)
                    dyb = dcat_ref[:, D_GMLP + pair * LANES:D_GMLP + (pair + 1) * LANES].astype(F32)
                    dzb_pairs.append((dyb * out_pair * (sg * (1.0 + zb * (1.0 - sg)))).astype(BF16))
                    dom2 = _stack_heads(dyb * (zb * sg)).astype(BF16)
                    dp = _dot(dom2, vv2, 1, 1)
                    delta = jnp.sum(p * dp, axis=-1, keepdims=True)
                    ds = p * (dp - delta)
                    dsk = -(p_sink * delta)
                    dsink = dsink + jnp.where(lane_row == 2 * pair,
                                              jnp.sum(dsk[:CHUNK], axis=0, keepdims=True), 0.0)
                    dsink = dsink + jnp.where(lane_row == 2 * pair + 1,
                                              jnp.sum(dsk[CHUNK:], axis=0, keepdims=True), 0.0)
                    dsb = (ds * SCALE).astype(BF16)
                    dq2 = _dot(dsb, kk2, 1, 0)
                    dq_pairs.append(_rope_transposed(_by_half(dq2[:CHUNK], dq2[CHUNK:]), cos_c, sin_c).astype(BF16))
                    dkk = dkk + _dot(dsb, qm2, 0, 0)
                    dvv = dvv + _dot(pb, dom2, 0, 0)
                dk_heads.append(_fold_halves(dkk))
                dv_heads.append(_fold_halves(dvv))
            dk_rot = _by_half(dk_heads[0], dk_heads[1])
            dv_all = _by_half(dv_heads[0], dv_heads[1])
            dk_p = _rope_transposed(dk_rot[:CHUNK], cos_p, sin_p)
            dk_c = _rope_transposed(dk_rot[CHUNK:], cos_c, sin_c)
            flush(jnp.concatenate([dk_p, dv_all[:CHUNK]], axis=1))
            dsink_ref[...] += dsink
            pend_kv_ref[...] = jnp.concatenate([dk_c, dv_all[CHUNK:]], axis=1)
            for pair in range(N_PAIRS):
                pend_ref[:, OFF_Q + pair * LANES:OFF_Q + (pair + 1) * LANES] = dq_pairs[pair]
                pend_ref[:, OFF_ZB + pair * LANES:OFF_ZB + (pair + 1) * LANES] = dzb_pairs[pair]
            xhat, rstd = _layer_norm_parts(proj_ref[:, OFF_V:OFF_V + D_GMLP].astype(F32))
            lng = lng_ref[...]
            vnb = (xhat * lng + lnb_ref[...]).astype(BF16)
            dvn_cols = []
            for g in range(GROUPS):
                cols = slice(g * CHUNK, (g + 1) * CHUNK)
                wm = _masked_spatial(w_ref, g)
                mixed = _dot(wm, vnb[:, cols], 1, 0) + b_ref[g]
                za = proj_ref[:, OFF_ZA + g * CHUNK:OFF_ZA + (g + 1) * CHUNK].astype(F32)
                u = proj_ref[:, OFF_U + g * CHUNK:OFF_U + (g + 1) * CHUNK].astype(F32)
                dya = dcat_ref[:, cols].astype(F32)
                sg = _sigmoid(za)
                sz = za * sg
                pend_ref[:, OFF_U + g * CHUNK:OFF_U + (g + 1) * CHUNK] = (dya * mixed * sz).astype(BF16)
                pend_ref[:, OFF_ZA + g * CHUNK:OFF_ZA + (g + 1) * CHUNK] = (
                    dya * u * mixed * (sg * (1.0 + za * (1.0 - sg)))).astype(BF16)
                dmixed = dya * u * sz
                dmb = dmixed.astype(BF16)
                dbacc_ref[g] += dmixed
                dw_ref[g] += _dot(dmb, vnb[:, cols], 1, 1)
                dvn_cols.append(_dot(wm, dmb, 0, 0))
            dvn = jnp.concatenate(dvn_cols, axis=1)
            dlng_ref[...] += jnp.sum(dvn * xhat, axis=0, keepdims=True)
            dlnb_ref[...] += jnp.sum(dvn, axis=0, keepdims=True)
            dxh = dvn * lng
            dv = rstd * (dxh - jnp.mean(dxh, axis=-1, keepdims=True)
                         - xhat * jnp.mean(dxh * xhat, axis=-1, keepdims=True))
            pend_ref[:, OFF_V:OFF_V + D_GMLP] = dv.astype(BF16)

        @pl.when(n == nb)
        def _():
            flush(jnp.zeros((CHUNK, 2 * D_KV), F32))
            t = lax.broadcasted_iota(jnp.int32, (CHUNK, CHUNK), 0)
            sidx = lax.broadcasted_iota(jnp.int32, (CHUNK, CHUNK), 1)
            lane = _lane_iota((CHUNK, LANES))
            dbt = jnp.zeros((CHUNK, LANES), F32)
            for g in range(GROUPS):
                dw_ref[g] = jnp.where(t >= sidx, dw_ref[g], 0.0)
                dbt = jnp.where(lane == g, jnp.sum(dbacc_ref[g], axis=-1, keepdims=True), dbt)
            db_ref[...] = jnp.transpose(dbt)[:GROUPS, :]

    cur = lambda n, *_: (jnp.minimum(n, nb - 1), 0)
    prev = lambda n, *_: (jnp.clip(n - 1, 0, nb - 1), 0)
    kv_block = OFF_K // (2 * D_KV)
    const2 = lambda n, *_: (0, 0)
    const3 = lambda n, *_: (0, 0, 0)
    return pl.pallas_call(
        body, name="mixer_bwd",
        grid_spec=pltpu.PrefetchScalarGridSpec(
            num_scalar_prefetch=1, grid=(nb + 1,),
            in_specs=[pl.BlockSpec((CHUNK, D_IN), cur),
                      pl.BlockSpec((CHUNK, 2 * D_KV), lambda n, *_: (jnp.clip(n - 1, 0, nb - 1), kv_block)),
                      pl.BlockSpec((CHUNK, D_GMLP + D_ATTN), cur),
                      pl.BlockSpec((CHUNK, 1), cur),
                      pl.BlockSpec((CHUNK, 1), prev),
                      pl.BlockSpec((1, LANES), const2),
                      pl.BlockSpec((1, LANES), const2),
                      pl.BlockSpec((1, D_GMLP), const2),
                      pl.BlockSpec((1, D_GMLP), const2),
                      pl.BlockSpec((GROUPS, CHUNK, CHUNK), const3),
                      pl.BlockSpec((GROUPS, CHUNK, CHUNK), const3)],
            out_specs=[pl.BlockSpec((CHUNK, D_IN), lambda n, *_: (jnp.maximum(n - 1, 0), 0)),
                       pl.BlockSpec((GROUPS, CHUNK, CHUNK), const3),
                       pl.BlockSpec((GROUPS, CHUNK), const2),
                       pl.BlockSpec((1, D_GMLP), const2),
                       pl.BlockSpec((1, D_GMLP), const2),
                       pl.BlockSpec((1, LANES), const2)],
            scratch_shapes=[pltpu.VMEM((CHUNK, D_IN), BF16), pltpu.VMEM((CHUNK, 2 * D_KV), F32),
                            pltpu.VMEM((GROUPS, CHUNK, CHUNK), F32)]),
        out_shape=[jax.ShapeDtypeStruct((s, D_IN), BF16),
                   jax.ShapeDtypeStruct((GROUPS, CHUNK, CHUNK), F32),
                   jax.ShapeDtypeStruct((GROUPS, CHUNK), F32),
                   jax.ShapeDtypeStruct((1, D_GMLP), F32),
                   jax.ShapeDtypeStruct((1, D_GMLP), F32),
                   jax.ShapeDtypeStruct((1, LANES), F32)],
        compiler_params=_cparams(("arbitrary",)),
    )(sinks, proj, proj, dcat, pos_col, pos_col, freq, sign, ln_g, ln_b, w_sp, b_sp_rows)


def _adamw_math(w, g, m, v):
    m = ADAM_B1 * m + (1.0 - ADAM_B1) * g
    v = ADAM_B2 * v + (1.0 - ADAM_B2) * (g * g)
    m_hat = m / (1.0 - ADAM_B1 ** ADAM_STEP)
    v_hat = v / (1.0 - ADAM_B2 ** ADAM_STEP)
    delta = -ADAM_LR * (m_hat / (jnp.sqrt(v_hat) + ADAM_EPS) + ADAM_WD * w)
    return delta, m, v


def adamw_shard(sums, got, w, m, v, name):
    r, c = w.shape
    tr = _tile(r, (224, 256, 128, 8))

    def body(s_ref, got_ref, w_ref, m_ref, v_ref, g_ref, d_ref, nm_ref, nv_ref):
        g = s_ref[...].astype(F32)
        for rel in range(3):
            g = g + got_ref[rel].astype(F32)
        g_ref[...] = g
        d_ref[...], nm_ref[...], nv_ref[...] = _adamw_math(w_ref[...], g, m_ref[...], v_ref[...])

    spec = pl.BlockSpec((tr, c), lambda i: (i, 0))
    return pl.pallas_call(
        body, name=name, grid=(r // tr,),
        in_specs=[pl.BlockSpec((None, tr, c), lambda i: (0, i, 0)), pl.BlockSpec((3, tr, c), lambda i: (0, i, 0)),
                  spec, spec, spec],
        out_specs=[spec] * 4, out_shape=[jax.ShapeDtypeStruct((r, c), F32)] * 4,
        compiler_params=_cparams(("arbitrary",)),
    )(sums, got, w, m, v)


def adamw_small(gathered, lane_windows, params):
    n_par = len(params)

    def body(*refs):
        g_refs = refs[:n_par + 1]
        wmv_refs = refs[n_par + 1:4 * n_par + 1]
        out_refs = refs[4 * n_par + 1:]

        def total(ref):
            acc = ref[0]
            for dev in range(1, N_DEV):
                acc = acc + ref[dev]
            return acc

        for i in range(n_par):
            w_ref, m_ref, v_ref = wmv_refs[3 * i:3 * i + 3]
            g = total(g_refs[i])
            if lane_windows[i] is not None:
                start, size = lane_windows[i]
                g = g[..., start:start + size]
            delta, new_m, new_v = _adamw_math(w_ref[...], g, m_ref[...], v_ref[...])
            for ref, val in zip(out_refs[4 * i:4 * i + 4], (g, delta, new_m, new_v)):
                ref[...] = val
        out_refs[4 * n_par][...] = total(g_refs[n_par])

    flat = [a for wmv in params for a in wmv]
    out_shape = [jax.ShapeDtypeStruct(w.shape, F32) for (w, _, _) in params for _ in range(4)]
    out_shape.append(jax.ShapeDtypeStruct(gathered[-1].shape[1:], F32))
    outs = pl.pallas_call(body, name="adamw_small", out_shape=out_shape, compiler_params=_cparams())(*gathered, *flat)
    return [tuple(outs[4 * i:4 * i + 4]) for i in range(n_par)], outs[-1]


def kernel(x, positions, g_pre, w_in, b_qkv, ln_v_g, ln_v_b, w_spatial, b_spatial, attn_sinks, w_out, g_post, loss_target, m_g_pre, m_w_in, m_b_qkv, m_ln_v_g, m_ln_v_b, m_w_spatial, m_b_spatial, m_attn_sinks, m_w_out, m_g_post, v_g_pre, v_w_in, v_b_qkv, v_ln_v_g, v_ln_v_b, v_w_spatial, v_b_spatial, v_attn_sinks, v_w_out, v_g_post):
    x2, target = x[0], loss_target[0]
    seq = x2.shape[0]
    xi, yi, ci = _my_place()

    wt_shard = w_in[0].T.astype(BF16)
    wo_shard = w_out[0].astype(BF16)
    wt, wo = allgather([wt_shard, wo_shard], "allgather_weights")

    bias = jnp.concatenate([jnp.zeros((1, OFF_Q), F32), b_qkv, jnp.zeros((1, D_ATTN), F32)], axis=1)
    proj, h = in_proj(x2, g_pre, wt, bias)
    pos_col = positions.reshape(seq, 1)
    half = HEAD_DIM // 2
    inv_freq = ROPE_THETA ** (-jnp.arange(half, dtype=F32) * (2.0 / HEAD_DIM))
    freq = jnp.tile(inv_freq, LANES // half).reshape(1, LANES)
    sign = jnp.tile(jnp.concatenate([-jnp.ones((half,), F32), jnp.ones((half,), F32)]), LANES // HEAD_DIM)
    sign = sign.reshape(1, LANES)
    b_rows = jnp.broadcast_to(b_spatial[0][:, :, None], (GROUPS, CHUNK, CHUNK))
    sinks = attn_sinks[0]
    cat = mixer_fwd(proj, pos_col, freq, sign, ln_v_g, ln_v_b, w_spatial[0], b_rows, sinks)
    dy, dout, d_g_post, loss_part = out_proj_loss(cat, wo, x2, target, g_post)

    dcat = matmul_nt(dy, wo, "out_proj_bwd")
    d_wo, _ = matmul_tn(cat, dy, 512, "w_out_grad")
    dproj, d_w_sp, d_b_sp, d_ln_g, d_ln_b, d_sinks = mixer_bwd(
        proj, dcat, pos_col, freq, sign, ln_v_g, ln_v_b, w_spatial[0], b_rows, sinks)
    d_wt, colsum = matmul_tn(dproj, h, 768, "w_in_grad")
    grad_x, d_g_pre = in_proj_bwd(dproj, wt, x2, g_pre, dout)

    owners = jnp.stack([4 * cx + 2 * cy + ci for cx, cy in (_chip_of(xi, yi, r) for r in range(4))]).astype(jnp.int32)
    got_wt, got_wo = exchange_pair([d_wt, d_wo], "grad_exchange_pair")
    sum_wt = pair_sum(d_wt, got_wt, owners, "grad_pair_sum_w_in")
    sum_wo = pair_sum(d_wo, got_wo, owners, "grad_pair_sum_w_out")
    far_wt, far_wo = exchange_chips([sum_wt, sum_wo], "grad_exchange_chips")

    gathered = allgather([d_g_pre, colsum, d_ln_g, d_ln_b, d_w_sp, d_b_sp, d_sinks, d_g_post, loss_part],
                         "allgather_small_grads", stack=True)
    windows = [None, (OFF_Q, D_QKV), None, None, None, None, (0, N_Q_HEADS), None]
    small = [(g_pre, m_g_pre, v_g_pre), (b_qkv, m_b_qkv, v_b_qkv), (ln_v_g, m_ln_v_g, v_ln_v_g),
             (ln_v_b, m_ln_v_b, v_ln_v_b), (w_spatial[0], m_w_spatial[0], v_w_spatial[0]),
             (b_spatial[0], m_b_spatial[0], v_b_spatial[0]), (attn_sinks, m_attn_sinks, v_attn_sinks),
             (g_post, m_g_post, v_g_post)]
    small_out, loss_row = adamw_small(gathered, windows, small)
    lead = [False, False, False, False, True, True, False, False]
    small_out = [tuple(a[None] if ld else a for a in leaf) for leaf, ld in zip(small_out, lead)]

    wt_out = adamw_shard(sum_wt, far_wt, w_in[0].T, m_w_in[0].T, v_w_in[0].T, "adamw_w_in")
    wo_out = adamw_shard(sum_wo, far_wo, w_out[0], m_w_out[0], v_w_out[0], "adamw_w_out")

    def leaves(k):
        gp, bq, lg, lb, ws, bs, sk, gpo = (leaf[k] for leaf in small_out)
        return [gp, wt_out[k].T[None], bq, lg, lb, ws, bs, sk, wo_out[k][None], gpo]

    return (loss_row[0, 0], grad_x[None], *leaves(0), *leaves(1), *leaves(2), *leaves(3))
```

```python
import functools

import jax
import jax.numpy as jnp
from jax import lax
from jax.experimental import pallas as pl
from jax.experimental.pallas import tpu as pltpu

F32 = jnp.float32
BF16 = jnp.bfloat16

D_MODEL = 2048
D_GMLP = 1024
D_ATTN = 1024
CHUNK = 128
GROUPS = 8
HEAD_DIM = 64
N_Q_HEADS = 16
N_KV_HEADS = 2
D_KV = N_KV_HEADS * HEAD_DIM
D_IN = 3 * D_GMLP + D_ATTN + 2 * D_KV + D_ATTN
OFF_U, OFF_V, OFF_ZA = 0, D_GMLP, 2 * D_GMLP
OFF_Q = 3 * D_GMLP
OFF_K = OFF_Q + D_ATTN
OFF_VA = OFF_K + D_KV
OFF_ZB = OFF_VA + D_KV
D_QKV = D_ATTN + 2 * D_KV
ROPE_THETA = 10000.0
EPS = 1e-6
SCALE = HEAD_DIM ** -0.5
NEG = -1e30
N_PAIRS = N_Q_HEADS // 2
PAIRS_PER_KV = N_PAIRS // N_KV_HEADS

ADAM_LR = 0.001
ADAM_B1 = 0.9
ADAM_B2 = 0.999
ADAM_EPS = 1e-08
ADAM_WD = 0.01
ADAM_STEP = 10

N_DEV = 8
LANES = 128
VMEM_LIMIT = 56 * 1024 * 1024

MESH = pl.DeviceIdType.MESH
ANY = pl.BlockSpec(memory_space=pl.ANY)


def _cparams(sem=None):
    return pltpu.CompilerParams(dimension_semantics=sem, vmem_limit_bytes=VMEM_LIMIT)


def _tile(n, prefs):
    for t in prefs:
        if n % t == 0:
            return t
    return n


def _sigmoid(z):
    return 1.0 / (1.0 + jnp.exp(-z))


def _dot(a, b, ca, cb):
    return lax.dot_general(a, b, (((ca,), (cb,)), ((), ())), preferred_element_type=F32)


def _my_place():
    return lax.axis_index("x"), lax.axis_index("y"), lax.axis_index("c")


def _chip_of(x, y, r):
    return (x ^ (r & 1), y ^ (r >> 1))


def _peer(x, y, c, k):
    return (x ^ (k >> 2), y ^ ((k >> 1) & 1), c ^ (k & 1))


def _index(px, py, pc):
    return 4 * px + 2 * py + pc


class _Comm:
    def __init__(self, inputs, out_shape, scratch, bind):
        self.inputs, self.out_shape, self.scratch, self.bind = list(inputs), list(out_shape), list(scratch), bind


def gather_comm(shards, stack=False, direct=False):
    n_arr = len(shards)

    def bind(ins, outs, sems):
        send_sems, recv_sems, local_sems = sems
        x, y, c = _my_place()
        me, sibling = (x, y, c), (x, y, 1 - c)
        chips = [_chip_of(x, y, r) for r in (1, 2, 3)]

        def rows(a, px, py, pc):
            d = _index(px, py, pc)
            if stack:
                return outs[a].at[d]
            m = shards[a].shape[0]
            return outs[a].at[pl.ds(pl.multiple_of(d * m, 8), m), :]

        def copy(a, k, block, to, src=None):
            return pltpu.make_async_remote_copy(
                src_ref=rows(a, *block) if src is None else src, dst_ref=rows(a, *block),
                send_sem=send_sems.at[a * 7 + k], recv_sem=recv_sems.at[a * 7 + k],
                device_id=to, device_id_type=MESH)

        def mine(a):
            return pltpu.make_async_copy(ins[a], rows(a, *me), local_sems.at[a])

        def own_sends(a):
            if direct:
                return [copy(a, k - 1, me, _peer(x, y, c, k), src=ins[a]) for k in range(1, 8)]
            return ([copy(a, 0, me, sibling, src=ins[a])]
                    + [copy(a, 1 + j, me, (*chip, c), src=ins[a]) for j, chip in enumerate(chips)])

        def start():
            for a in range(n_arr):
                mine(a).start()
                for cp in own_sends(a):
                    cp.start()

        def relay():
            if direct:
                return
            for j, chip in enumerate(chips):
                for a in range(n_arr):
                    copy(a, 1 + j, (*chip, c), me).wait_recv()
                    copy(a, 4 + j, (*chip, c), sibling).start()

        def finish():
            for a in range(n_arr):
                if direct:
                    for k in range(1, 8):
                        copy(a, k - 1, _peer(x, y, c, k), me).wait_recv()
                else:
                    copy(a, 0, sibling, me).wait_recv()
                    for j, chip in enumerate(chips):
                        copy(a, 4 + j, (*chip, 1 - c), me).wait_recv()
                        copy(a, 4 + j, (*chip, c), sibling).wait_send()
                for cp in own_sends(a):
                    cp.wait_send()
                mine(a).wait()

        return start, relay, finish

    def gathered(s):
        return (N_DEV, *s.shape) if stack else (N_DEV * s.shape[0], s.shape[1])

    return _Comm(shards, [jax.ShapeDtypeStruct(gathered(s), s.dtype) for s in shards],
                 [pltpu.SemaphoreType.DMA((7 * n_arr,)), pltpu.SemaphoreType.DMA((7 * n_arr,)),
                  pltpu.SemaphoreType.DMA((n_arr,))], bind)


def scatter_comm(parts):
    n_arr = len(parts)

    def bind(ins, outs, sems):
        send_sems, recv_sems, local_sems = sems
        x, y, c = _my_place()
        my_index = _index(x, y, c)

        def block(a, d):
            m = parts[a].shape[0] // N_DEV
            return ins[a].at[pl.ds(pl.multiple_of(d * m, 16), m), :]

        def copy(a, k, slot):
            peer = _peer(x, y, c, k)
            return pltpu.make_async_remote_copy(
                src_ref=block(a, _index(*peer)), dst_ref=outs[a].at[slot],
                send_sem=send_sems.at[a * 7 + k - 1], recv_sem=recv_sems.at[a * 7 + k - 1],
                device_id=peer, device_id_type=MESH)

        def mine(a):
            return pltpu.make_async_copy(block(a, my_index), outs[a].at[my_index], local_sems.at[a])

        def start():
            for a in range(n_arr):
                mine(a).start()
                for k in range(1, 8):
                    copy(a, k, my_index).start()

        def finish():
            for a in range(n_arr):
                for k in range(1, 8):
                    copy(a, k, _index(*_peer(x, y, c, k))).wait_recv()
                    copy(a, k, my_index).wait_send()
                mine(a).wait()

        return start, (lambda: None), finish

    return _Comm(parts, [jax.ShapeDtypeStruct((N_DEV, p.shape[0] // N_DEV, p.shape[1]), p.dtype) for p in parts],
                 [pltpu.SemaphoreType.DMA((7 * n_arr,)), pltpu.SemaphoreType.DMA((7 * n_arr,)),
                  pltpu.SemaphoreType.DMA((n_arr,))], bind)


def pair_comm(parts):
    n_arr = len(parts)

    def bind(ins, outs, sems):
        send_sems, recv_sems = sems
        x, y, c = _my_place()

        def copies():
            out = []
            for a in range(n_arr):
                m = parts[a].shape[0] // N_DEV
                for r in range(4):
                    owner = _index(*_chip_of(x, y, r), 1 - c)
                    out.append(pltpu.make_async_remote_copy(
                        src_ref=ins[a].at[pl.ds(pl.multiple_of(owner * m, 16), m), :], dst_ref=outs[a].at[r],
                        send_sem=send_sems.at[a * 4 + r], recv_sem=recv_sems.at[a * 4 + r],
                        device_id=(x, y, 1 - c), device_id_type=MESH))
            return out

        def start():
            for cp in copies():
                cp.start()

        def finish():
            for cp in copies():
                cp.wait_recv()
                cp.wait_send()

        return start, (lambda: None), finish

    return _Comm(parts, [jax.ShapeDtypeStruct((4, p.shape[0] // N_DEV, p.shape[1]), p.dtype) for p in parts],
                 [pltpu.SemaphoreType.DMA((4 * n_arr,)), pltpu.SemaphoreType.DMA((4 * n_arr,))], bind)


def chips_comm(sums):
    n_arr = len(sums)

    def bind(ins, outs, sems):
        send_sems, recv_sems = sems
        x, y, c = _my_place()

        def copies():
            return [pltpu.make_async_remote_copy(
                src_ref=ins[a].at[r], dst_ref=outs[a].at[r - 1],
                send_sem=send_sems.at[a * 3 + r - 1], recv_sem=recv_sems.at[a * 3 + r - 1],
                device_id=(*_chip_of(x, y, r), c), device_id_type=MESH) for a in range(n_arr) for r in (1, 2, 3)]

        def start():
            for cp in copies():
                cp.start()

        def finish():
            for cp in copies():
                cp.wait_recv()
                cp.wait_send()

        return start, (lambda: None), finish

    return _Comm(sums, [jax.ShapeDtypeStruct((3,) + s.shape[1:], s.dtype) for s in sums],
                 [pltpu.SemaphoreType.DMA((3 * n_arr,)), pltpu.SemaphoreType.DMA((3 * n_arr,))], bind)


def run_comm(comm, name):
    n_in, n_out = len(comm.inputs), len(comm.out_shape)

    def body(*refs):
        start, relay, finish = comm.bind(refs[:n_in], refs[n_in:n_in + n_out], refs[n_in + n_out:])
        start()
        relay()
        finish()

    outs = pl.pallas_call(body, name=name, out_shape=comm.out_shape, in_specs=[ANY] * n_in,
                          out_specs=[ANY] * n_out, scratch_shapes=comm.scratch)(*comm.inputs)
    return list(outs)


class _Hosted:
    def __init__(self, comms):
        self.comms = list(comms)
        self.inputs = [a for cm in self.comms for a in cm.inputs]
        self.out_shape = [s for cm in self.comms for s in cm.out_shape]
        self.scratch = [s for cm in self.comms for s in cm.scratch]
        self.in_specs = [ANY] * len(self.inputs)
        self.out_specs = [ANY] * len(self.out_shape)

    def split(self, refs, n_in, n_out, n_scratch):
        ni, no = len(self.inputs), len(self.out_shape)
        ins, rest = refs[:n_in], refs[n_in:]
        c_ins, rest = rest[:ni], rest[ni:]
        outs, rest = rest[:n_out], rest[n_out:]
        c_outs, rest = rest[:no], rest[no:]
        scratch, c_sems = rest[:n_scratch], rest[n_scratch:]
        phases = []
        for cm in self.comms:
            a, b, s = len(cm.inputs), len(cm.out_shape), len(cm.scratch)
            phases.append(cm.bind(c_ins[:a], c_outs[:b], c_sems[:s]))
            c_ins, c_outs, c_sems = c_ins[a:], c_outs[b:], c_sems[s:]
        return ins, outs, scratch, phases


def _before_step(phases, step, n_steps):
    if not phases:
        return

    @pl.when(step == 0)
    def _():
        for start, _, _ in phases:
            start()

    @pl.when(step == n_steps // 2)
    def _():
        for _, relay, _ in phases:
            relay()


def _after_step(phases, step, n_steps):
    if not phases:
        return

    @pl.when(step == n_steps - 1)
    def _():
        for _, _, finish in phases:
            finish()


def pair_sum(part, got, owners, name):
    m, n = got.shape[1:]

    def body(own_ref, mine_ref, got_ref, out_ref):
        del own_ref
        out_ref[...] = (mine_ref[...].astype(F32) + got_ref[...].astype(F32)).astype(out_ref.dtype)

    return pl.pallas_call(
        body, name=name,
        grid_spec=pltpu.PrefetchScalarGridSpec(
            num_scalar_prefetch=1, grid=(4,),
            in_specs=[pl.BlockSpec((m, n), lambda r, own: (own[r], 0)),
                      pl.BlockSpec((None, m, n), lambda r, own: (r, 0, 0))],
            out_specs=pl.BlockSpec((None, m, n), lambda r, own: (r, 0, 0))),
        out_shape=jax.ShapeDtypeStruct((4, m, n), got.dtype),
        compiler_params=_cparams(("arbitrary",)),
    )(owners, part, got)


def in_proj(x, g_pre, wt, bias, comms=()):
    s, d = x.shape
    tm = _tile(s, (512, 256, 128))
    tn = 768
    ni, nj = s // tm, D_IN // tn
    hosted = _Hosted(comms)

    def body(*refs):
        (x_ref, g_ref, w_ref, b_ref), (proj_ref, h_ref), _, phases = hosted.split(refs, 4, 2, 0)
        step = pl.program_id(0) * nj + pl.program_id(1)
        _before_step(phases, step, ni * nj)

        @pl.when(pl.program_id(1) == 0)
        def _():
            xv = x_ref[...]
            r = lax.rsqrt(jnp.mean(xv * xv, axis=-1, keepdims=True) + EPS)
            h_ref[...] = (xv * r * g_ref[...]).astype(BF16)

        acc = _dot(h_ref[...], w_ref[...], 1, 1)
        proj_ref[...] = (acc + b_ref[...]).astype(BF16)
        _after_step(phases, step, ni * nj)

    return pl.pallas_call(
        body, name="in_proj", grid=(ni, nj),
        in_specs=[pl.BlockSpec((tm, d), lambda i, j: (i, 0)),
                  pl.BlockSpec((1, d), lambda i, j: (0, 0)),
                  pl.BlockSpec((tn, d), lambda i, j: (j, 0)),
                  pl.BlockSpec((1, tn), lambda i, j: (0, j))] + hosted.in_specs,
        out_specs=[pl.BlockSpec((tm, tn), lambda i, j: (i, j)),
                   pl.BlockSpec((tm, d), lambda i, j: (i, 0))] + hosted.out_specs,
        out_shape=[jax.ShapeDtypeStruct((s, D_IN), BF16), jax.ShapeDtypeStruct((s, d), BF16)] + hosted.out_shape,
        scratch_shapes=hosted.scratch,
        compiler_params=_cparams(("arbitrary", "arbitrary")),
    )(x, g_pre, wt, bias, *hosted.inputs)


def out_proj_loss(cat, w_out, x, target, g_post):
    s, d = x.shape
    tm = _tile(s, (256, 128))

    def body(cat_ref, w_ref, x_ref, t_ref, g_ref, dy_ref, dout_ref, dg_ref, loss_ref):
        @pl.when(pl.program_id(0) == 0)
        def _():
            dg_ref[...] = jnp.zeros_like(dg_ref)
            loss_ref[...] = jnp.zeros_like(loss_ref)

        yv = _dot(cat_ref[...], w_ref[...], 1, 0)
        r = lax.rsqrt(jnp.mean(yv * yv, axis=-1, keepdims=True) + EPS)
        nrm = yv * r
        g = g_ref[...]
        err = x_ref[...] + nrm * g - t_ref[...]
        loss_ref[...] += 0.5 * jnp.sum(jnp.sum(err * err, axis=-1, keepdims=True), axis=0, keepdims=True) / d
        dout = err * (1.0 / d)
        dout_ref[...] = dout
        dg_ref[...] += jnp.sum(dout * nrm, axis=0, keepdims=True)
        dn = dout * g
        dy = r * (dn - nrm * jnp.mean(dn * nrm, axis=-1, keepdims=True))
        dy_ref[...] = dy.astype(BF16)

    return pl.pallas_call(
        body, name="out_proj_loss", grid=(s // tm,),
        in_specs=[pl.BlockSpec((tm, d), lambda i: (i, 0)),
                  pl.BlockSpec((d, d), lambda i: (0, 0)),
                  pl.BlockSpec((tm, d), lambda i: (i, 0)),
                  pl.BlockSpec((tm, d), lambda i: (i, 0)),
                  pl.BlockSpec((1, d), lambda i: (0, 0))],
        out_specs=[pl.BlockSpec((tm, d), lambda i: (i, 0)),
                   pl.BlockSpec((tm, d), lambda i: (i, 0)),
                   pl.BlockSpec((1, d), lambda i: (0, 0)),
                   pl.BlockSpec((1, LANES), lambda i: (0, 0))],
        out_shape=[jax.ShapeDtypeStruct((s, d), BF16), jax.ShapeDtypeStruct((s, d), F32),
                   jax.ShapeDtypeStruct((1, d), F32), jax.ShapeDtypeStruct((1, LANES), F32)],
        compiler_params=_cparams(("arbitrary",)),
    )(cat, w_out, x, target, g_post)


def matmul_nt(a, b, name):
    m, k = a.shape
    n = b.shape[0]
    tm = _tile(m, (512, 256, 128))

    def body(a_ref, b_ref, o_ref):
        o_ref[...] = _dot(a_ref[...], b_ref[...], 1, 1).astype(o_ref.dtype)

    return pl.pallas_call(
        body, name=name, grid=(m // tm,),
        in_specs=[pl.BlockSpec((tm, k), lambda i: (i, 0)), pl.BlockSpec((n, k), lambda i: (0, 0))],
        out_specs=pl.BlockSpec((tm, n), lambda i: (i, 0)),
        out_shape=jax.ShapeDtypeStruct((m, n), BF16),
        compiler_params=_cparams(("arbitrary",)),
    )(a, b)


def matmul_tn(a, b, tm, name, comms=()):
    k, m = a.shape
    n = b.shape[1]
    tk = _tile(k, (1024, 512, 256, 128))
    nk = k // tk
    hosted = _Hosted(comms)

    def body(*refs):
        (a_ref, b_ref), (o_ref, cs_ref), (acc_ref, csacc_ref), phases = hosted.split(refs, 2, 2, 2)
        kk = pl.program_id(1)
        step = pl.program_id(0) * nk + kk
        _before_step(phases, step, (m // tm) * nk)

        @pl.when(kk == 0)
        def _():
            acc_ref[...] = jnp.zeros_like(acc_ref)
            csacc_ref[...] = jnp.zeros_like(csacc_ref)

        av = a_ref[...]
        acc_ref[...] += _dot(av, b_ref[...], 0, 0)
        csacc_ref[...] += jnp.sum(av.astype(F32), axis=0, keepdims=True)

        @pl.when(kk == nk - 1)
        def _():
            o_ref[...] = acc_ref[...].astype(o_ref.dtype)
            cs_ref[...] = csacc_ref[...]

        _after_step(phases, step, (m // tm) * nk)

    return pl.pallas_call(
        body, name=name, grid=(m // tm, nk),
        in_specs=[pl.BlockSpec((tk, tm), lambda i, j: (j, i)),
                  pl.BlockSpec((tk, n), lambda i, j: (j, 0))] + hosted.in_specs,
        out_specs=[pl.BlockSpec((tm, n), lambda i, j: (i, 0)),
                   pl.BlockSpec((1, tm), lambda i, j: (0, i))] + hosted.out_specs,
        out_shape=[jax.ShapeDtypeStruct((m, n), BF16), jax.ShapeDtypeStruct((1, m), F32)] + hosted.out_shape,
        scratch_shapes=[pltpu.VMEM((tm, n), F32), pltpu.VMEM((1, tm), F32)] + hosted.scratch,
        compiler_params=_cparams(("arbitrary", "arbitrary")),
    )(a, b, *hosted.inputs)


def in_proj_bwd(dproj, wt, x, g_pre, dout, comms=()):
    s, d = x.shape
    tm = _tile(s, (512, 256, 128))
    tk = 768
    nk = D_IN // tk
    hosted = _Hosted(comms)

    def body(*refs):
        (dp_ref, w_ref, x_ref, g_ref, dout_ref), (gx_ref, dg_ref), (acc_ref,), phases = hosted.split(refs, 5, 2, 1)
        i, kk = pl.program_id(0), pl.program_id(1)
        step = i * nk + kk
        _before_step(phases, step, (s // tm) * nk)

        @pl.when(jnp.logical_and(i == 0, kk == 0))
        def _():
            dg_ref[...] = jnp.zeros_like(dg_ref)

        @pl.when(kk == 0)
        def _():
            acc_ref[...] = jnp.zeros_like(acc_ref)

        acc_ref[...] += _dot(dp_ref[...], w_ref[...], 1, 0)

        @pl.when(kk == nk - 1)
        def _():
            for c0 in range(0, tm, CHUNK):
                rows = slice(c0, c0 + CHUNK)
                dh = acc_ref[rows, :]
                xv = x_ref[rows, :]
                r = lax.rsqrt(jnp.mean(xv * xv, axis=-1, keepdims=True) + EPS)
                xn = xv * r
                dg_ref[...] += jnp.sum(dh * xn, axis=0, keepdims=True)
                dn = dh * g_ref[...]
                gx_ref[rows, :] = dout_ref[rows, :] + r * (dn - xn * jnp.mean(dn * xn, axis=-1, keepdims=True))

        _after_step(phases, step, (s // tm) * nk)

    return pl.pallas_call(
        body, name="in_proj_bwd", grid=(s // tm, nk),
        in_specs=[pl.BlockSpec((tm, tk), lambda i, j: (i, j)),
                  pl.BlockSpec((tk, d), lambda i, j: (j, 0)),
                  pl.BlockSpec((tm, d), lambda i, j: (i, 0)),
                  pl.BlockSpec((1, d), lambda i, j: (0, 0)),
                  pl.BlockSpec((tm, d), lambda i, j: (i, 0))] + hosted.in_specs,
        out_specs=[pl.BlockSpec((tm, d), lambda i, j: (i, 0)),
                   pl.BlockSpec((1, d), lambda i, j: (0, 0))] + hosted.out_specs,
        out_shape=[jax.ShapeDtypeStruct((s, d), F32), jax.ShapeDtypeStruct((1, d), F32)] + hosted.out_shape,
        scratch_shapes=[pltpu.VMEM((tm, d), F32)] + hosted.scratch,
        compiler_params=_cparams(("arbitrary", "arbitrary")),
    )(dproj, wt, x, g_pre, dout, *hosted.inputs)


def _lane_iota(shape):
    return lax.broadcasted_iota(jnp.int32, shape, len(shape) - 1)


def _rope_tables(pos_col, freq, sign):
    ang = pos_col.astype(F32) * freq
    return jnp.cos(ang), jnp.sin(ang) * sign


def _partner(v):
    low = (_lane_iota(v.shape) % HEAD_DIM) < (HEAD_DIM // 2)
    return jnp.where(low, pltpu.roll(v, LANES - HEAD_DIM // 2, 1), pltpu.roll(v, HEAD_DIM // 2, 1))


def _rope(v, cos, sin_signed):
    return v * cos + _partner(v) * sin_signed


def _rope_transposed(dv, cos, sin_signed):
    return dv * cos - _partner(dv) * sin_signed


def _both_halves(v, kv_head):
    keep = (_lane_iota(v.shape) >= HEAD_DIM) if kv_head else (_lane_iota(v.shape) < HEAD_DIM)
    return jnp.where(keep, v, pltpu.roll(v, HEAD_DIM, 1))


def _fold_halves(acc):
    return acc + pltpu.roll(acc, HEAD_DIM, 1)


def _by_half(a, b):
    shape = jnp.broadcast_shapes(jnp.shape(a), jnp.shape(b))
    return jnp.where(_lane_iota(shape) < HEAD_DIM, a, b)


def _stack_heads(pair):
    return jnp.concatenate([_by_half(pair, 0.0), _by_half(0.0, pair)], axis=0)


def _band_mask(has_prev):
    i = lax.broadcasted_iota(jnp.int32, (2 * CHUNK, 2 * CHUNK), 0) % CHUNK
    j = lax.broadcasted_iota(jnp.int32, (2 * CHUNK, 2 * CHUNK), 1)
    band = jnp.logical_and(j > i, j <= i + CHUNK)
    return jnp.logical_and(band, jnp.logical_or(j >= CHUNK, has_prev))


def _probs(qm2, kk2, valid, sink_col):
    sc = _dot(qm2, kk2, 1, 1) * SCALE
    sc = jnp.where(valid, sc, NEG)
    mx = jnp.maximum(jnp.max(sc, axis=-1, keepdims=True), sink_col)
    p = jnp.exp(sc - mx)
    es = jnp.exp(sink_col - mx)
    inv = 1.0 / (jnp.sum(p, axis=-1, keepdims=True) + es)
    return p * inv, es * inv


def _sink_col(sinks_ref, pair):
    row = lax.broadcasted_iota(jnp.int32, (2 * CHUNK, 1), 0)
    return jnp.where(row < CHUNK, sinks_ref[2 * pair], sinks_ref[2 * pair + 1])


def _layer_norm_parts(v):
    mu = jnp.mean(v, axis=-1, keepdims=True)
    xc = v - mu
    rstd = lax.rsqrt(jnp.mean(xc * xc, axis=-1, keepdims=True) + EPS)
    return xc * rstd, rstd


def _masked_spatial(w_ref, g):
    t = lax.broadcasted_iota(jnp.int32, (CHUNK, CHUNK), 0)
    sidx = lax.broadcasted_iota(jnp.int32, (CHUNK, CHUNK), 1)
    return jnp.where(t >= sidx, w_ref[g], 0.0).astype(BF16)


def _keys_values(kv_ref, kvp_ref, pos_ref, posp_ref, freq, sign):
    cos_c, sin_c = _rope_tables(pos_ref[...], freq, sign)
    cos_p, sin_p = _rope_tables(posp_ref[...], freq, sign)
    k_c = _rope(kv_ref[:, :D_KV].astype(F32), cos_c, sin_c)
    k_p = _rope(kvp_ref[:, :D_KV].astype(F32), cos_p, sin_p)
    keys = jnp.concatenate([k_p, k_c], axis=0)
    vals = jnp.concatenate([kvp_ref[:, D_KV:], kv_ref[:, D_KV:]], axis=0).astype(F32)
    return keys, vals, (cos_c, sin_c, cos_p, sin_p)


def mixer_fwd(proj, pos_col, freq, sign, ln_g, ln_b, w_sp, b_sp_rows, sinks):
    s = proj.shape[0]
    nb = s // CHUNK

    def body(sinks_ref, proj_ref, kvp_ref, pos_ref, posp_ref, freq_ref, sign_ref, lng_ref, lnb_ref, w_ref, b_ref,
             cat_ref):
        n = pl.program_id(0)
        freq, sign = freq_ref[...], sign_ref[...]
        xhat, _ = _layer_norm_parts(proj_ref[:, OFF_V:OFF_V + D_GMLP].astype(F32))
        vnb = (xhat * lng_ref[...] + lnb_ref[...]).astype(BF16)
        for g in range(GROUPS):
            cols = slice(g * CHUNK, (g + 1) * CHUNK)
            mixed = _dot(_masked_spatial(w_ref, g), vnb[:, cols], 1, 0) + b_ref[g]
            za = proj_ref[:, OFF_ZA + g * CHUNK:OFF_ZA + (g + 1) * CHUNK].astype(F32)
            u = proj_ref[:, OFF_U + g * CHUNK:OFF_U + (g + 1) * CHUNK].astype(F32)
            cat_ref[:, cols] = (u * mixed * (za * _sigmoid(za))).astype(BF16)
        kv_ref = proj_ref.at[:, OFF_K:OFF_K + 2 * D_KV]
        keys, vals, (cos_c, sin_c, _, _) = _keys_values(kv_ref, kvp_ref, pos_ref, posp_ref, freq, sign)
        valid = _band_mask(n > 0)
        for pair in range(N_PAIRS):
            kvh = pair // PAIRS_PER_KV
            kk2 = _both_halves(keys, kvh).astype(BF16)
            vv2 = _both_halves(vals, kvh).astype(BF16)
            qcols = slice(OFF_Q + pair * LANES, OFF_Q + (pair + 1) * LANES)
            q_pair = _rope(proj_ref[:, qcols].astype(F32), cos_c, sin_c)
            p, _ = _probs(_stack_heads(q_pair).astype(BF16), kk2, valid, _sink_col(sinks_ref, pair))
            o2 = _dot(p.astype(BF16), vv2, 1, 0)
            out_pair = _by_half(o2[:CHUNK], o2[CHUNK:])
            zb = proj_ref[:, OFF_ZB + pair * LANES:OFF_ZB + (pair + 1) * LANES].astype(F32)
            cat_ref[:, D_GMLP + pair * LANES:D_GMLP + (pair + 1) * LANES] = (
                out_pair * (zb * _sigmoid(zb))).astype(BF16)

    prev = lambda n, *_: (jnp.maximum(n - 1, 0), 0)
    kv_block = OFF_K // (2 * D_KV)
    return pl.pallas_call(
        body, name="mixer_fwd",
        grid_spec=pltpu.PrefetchScalarGridSpec(
            num_scalar_prefetch=1, grid=(nb,),
            in_specs=[pl.BlockSpec((CHUNK, D_IN), lambda n, *_: (n, 0)),
                      pl.BlockSpec((CHUNK, 2 * D_KV), lambda n, *_: (jnp.maximum(n - 1, 0), kv_block)),
                      pl.BlockSpec((CHUNK, 1), lambda n, *_: (n, 0)),
                      pl.BlockSpec((CHUNK, 1), prev),
                      pl.BlockSpec((1, LANES), lambda n, *_: (0, 0)),
                      pl.BlockSpec((1, LANES), lambda n, *_: (0, 0)),
                      pl.BlockSpec((1, D_GMLP), lambda n, *_: (0, 0)),
                      pl.BlockSpec((1, D_GMLP), lambda n, *_: (0, 0)),
                      pl.BlockSpec((GROUPS, CHUNK, CHUNK), lambda n, *_: (0, 0, 0)),
                      pl.BlockSpec((GROUPS, CHUNK, CHUNK), lambda n, *_: (0, 0, 0))],
            out_specs=pl.BlockSpec((CHUNK, D_GMLP + D_ATTN), lambda n, *_: (n, 0))),
        out_shape=jax.ShapeDtypeStruct((s, D_GMLP + D_ATTN), BF16),
        compiler_params=_cparams(("arbitrary",)),
    )(sinks, proj, proj, pos_col, pos_col, freq, sign, ln_g, ln_b, w_sp, b_sp_rows)


def mixer_bwd(proj, dcat, pos_col, freq, sign, ln_g, ln_b, w_sp, b_sp_rows, sinks):
    s = proj.shape[0]
    nb = s // CHUNK

    def body(sinks_ref, proj_ref, kvp_ref, dcat_ref, pos_ref, posp_ref, freq_ref, sign_ref, lng_ref, lnb_ref,
             w_ref, b_ref,
             dproj_ref, dw_ref, db_ref, dlng_ref, dlnb_ref, dsink_ref,
             pend_ref, pend_kv_ref, dbacc_ref):
        n = pl.program_id(0)

        @pl.when(n == 0)
        def _():
            dw_ref[...] = jnp.zeros_like(dw_ref)
            dbacc_ref[...] = jnp.zeros_like(dbacc_ref)
            dlng_ref[...] = jnp.zeros_like(dlng_ref)
            dlnb_ref[...] = jnp.zeros_like(dlnb_ref)
            dsink_ref[...] = jnp.zeros_like(dsink_ref)

        def flush(dkv_prev):
            @pl.when(n > 0)
            def _():
                dproj_ref[...] = pend_ref[...]
                dproj_ref[:, OFF_K:OFF_K + 2 * D_KV] = (pend_kv_ref[...] + dkv_prev).astype(BF16)

        @pl.when(n < nb)
        def _():
            freq, sign = freq_ref[...], sign_ref[...]
            kv_ref = proj_ref.at[:, OFF_K:OFF_K + 2 * D_KV]
            keys, vals, (cos_c, sin_c, cos_p, sin_p) = _keys_values(kv_ref, kvp_ref, pos_ref, posp_ref, freq, sign)
            valid = _band_mask(n > 0)
            lane_row = _lane_iota((1, LANES))
            dsink = jnp.zeros((1, LANES), F32)
            dk_heads, dv_heads = [], []
            dq_pairs, dzb_pairs = [], []
            for kvh in range(N_KV_HEADS):
                kk2 = _both_halves(keys, kvh).astype(BF16)
                vv2 = _both_halves(vals, kvh).astype(BF16)
                dkk = jnp.zeros((2 * CHUNK, LANES), F32)
                dvv = jnp.zeros((2 * CHUNK, LANES), F32)
                for pair in range(kvh * PAIRS_PER_KV, (kvh + 1) * PAIRS_PER_KV):
                    qcols = slice(OFF_Q + pair * LANES, OFF_Q + (pair + 1) * LANES)
                    q_pair = _rope(proj_ref[:, qcols].astype(F32), cos_c, sin_c)
                    qm2 = _stack_heads(q_pair).astype(BF16)
                    p, p_sink = _probs(qm2, kk2, valid, _sink_col(sinks_ref, pair))
                    pb = p.astype(BF16)
                    o2 = _dot(pb, vv2, 1, 0)
                    out_pair = _by_half(o2[:CHUNK], o2[CHUNK:])
                    zb = proj_ref[:, OFF_ZB + pair * LANES:OFF_ZB + (pair + 1) * LANES].astype(F32)
                    sg = _sigmoid(zb)
                    dyb = dcat_ref[:, D_GMLP + pair * LANES:D_GMLP + (pair + 1) * LANES].astype(F32)
                    dzb_pairs.append((dyb * out_pair * (sg * (1.0 + zb * (1.0 - sg)))).astype(BF16))
                    dom2 = _stack_heads(dyb * (zb * sg)).astype(BF16)
                    dp = _dot(dom2, vv2, 1, 1)
                    delta = jnp.sum(p * dp, axis=-1, keepdims=True)
                    ds = p * (dp - delta)
                    dsk = -(p_sink * delta)
                    dsink = dsink + jnp.where(lane_row == 2 * pair,
                                              jnp.sum(dsk[:CHUNK], axis=0, keepdims=True), 0.0)
                    dsink = dsink + jnp.where(lane_row == 2 * pair + 1,
                                              jnp.sum(dsk[CHUNK:], axis=0, keepdims=True), 0.0)
                    dsb = (ds * SCALE).astype(BF16)
                    dq2 = _dot(dsb, kk2, 1, 0)
                    dq_pairs.append(_rope_transposed(_by_half(dq2[:CHUNK], dq2[CHUNK:]), cos_c, sin_c).astype(BF16))
                    dkk = dkk + _dot(dsb, qm2, 0, 0)
                    dvv = dvv + _dot(pb, dom2, 0, 0)
                dk_heads.append(_fold_halves(dkk))
                dv_heads.append(_fold_halves(dvv))
            dk_rot = _by_half(dk_heads[0], dk_heads[1])
            dv_all = _by_half(dv_heads[0], dv_heads[1])
            dk_p = _rope_transposed(dk_rot[:CHUNK], cos_p, sin_p)
            dk_c = _rope_transposed(dk_rot[CHUNK:], cos_c, sin_c)
            flush(jnp.concatenate([dk_p, dv_all[:CHUNK]], axis=1))
            dsink_ref[...] += dsink
            pend_kv_ref[...] = jnp.concatenate([dk_c, dv_all[CHUNK:]], axis=1)
            for pair in range(N_PAIRS):
                pend_ref[:, OFF_Q + pair * LANES:OFF_Q + (pair + 1) * LANES] = dq_pairs[pair]
                pend_ref[:, OFF_ZB + pair * LANES:OFF_ZB + (pair + 1) * LANES] = dzb_pairs[pair]
            xhat, rstd = _layer_norm_parts(proj_ref[:, OFF_V:OFF_V + D_GMLP].astype(F32))
            lng = lng_ref[...]
            vnb = (xhat * lng + lnb_ref[...]).astype(BF16)
            dvn_cols = []
            for g in range(GROUPS):
                cols = slice(g * CHUNK, (g + 1) * CHUNK)
                wm = _masked_spatial(w_ref, g)
                mixed = _dot(wm, vnb[:, cols], 1, 0) + b_ref[g]
                za = proj_ref[:, OFF_ZA + g * CHUNK:OFF_ZA + (g + 1) * CHUNK].astype(F32)
                u = proj_ref[:, OFF_U + g * CHUNK:OFF_U + (g + 1) * CHUNK].astype(F32)
                dya = dcat_ref[:, cols].astype(F32)
                sg = _sigmoid(za)
                sz = za * sg
                pend_ref[:, OFF_U + g * CHUNK:OFF_U + (g + 1) * CHUNK] = (dya * mixed * sz).astype(BF16)
                pend_ref[:, OFF_ZA + g * CHUNK:OFF_ZA + (g + 1) * CHUNK] = (
                    dya * u * mixed * (sg * (1.0 + za * (1.0 - sg)))).astype(BF16)
                dmixed = dya * u * sz
                dmb = dmixed.astype(BF16)
                dbacc_ref[g] += dmixed
                dw_ref[g] += _dot(dmb, vnb[:, cols], 1, 1)
                dvn_cols.append(_dot(wm, dmb, 0, 0))
            dvn = jnp.concatenate(dvn_cols, axis=1)
            dlng_ref[...] += jnp.sum(dvn * xhat, axis=0, keepdims=True)
            dlnb_ref[...] += jnp.sum(dvn, axis=0, keepdims=True)
            dxh = dvn * lng
            dv = rstd * (dxh - jnp.mean(dxh, axis=-1, keepdims=True)
                         - xhat * jnp.mean(dxh * xhat, axis=-1, keepdims=True))
            pend_ref[:, OFF_V:OFF_V + D_GMLP] = dv.astype(BF16)

        @pl.when(n == nb)
        def _():
            flush(jnp.zeros((CHUNK, 2 * D_KV), F32))
            t = lax.broadcasted_iota(jnp.int32, (CHUNK, CHUNK), 0)
            sidx = lax.broadcasted_iota(jnp.int32, (CHUNK, CHUNK), 1)
            lane = _lane_iota((CHUNK, LANES))
            dbt = jnp.zeros((CHUNK, LANES), F32)
            for g in range(GROUPS):
                dw_ref[g] = jnp.where(t >= sidx, dw_ref[g], 0.0)
                dbt = jnp.where(lane == g, jnp.sum(dbacc_ref[g], axis=-1, keepdims=True), dbt)
            db_ref[...] = jnp.transpose(dbt)[:GROUPS, :]

    cur = lambda n, *_: (jnp.minimum(n, nb - 1), 0)
    prev = lambda n, *_: (jnp.clip(n - 1, 0, nb - 1), 0)
    kv_block = OFF_K // (2 * D_KV)
    const2 = lambda n, *_: (0, 0)
    const3 = lambda n, *_: (0, 0, 0)
    return pl.pallas_call(
        body, name="mixer_bwd",
        grid_spec=pltpu.PrefetchScalarGridSpec(
            num_scalar_prefetch=1, grid=(nb + 1,),
            in_specs=[pl.BlockSpec((CHUNK, D_IN), cur),
                      pl.BlockSpec((CHUNK, 2 * D_KV), lambda n, *_: (jnp.clip(n - 1, 0, nb - 1), kv_block)),
                      pl.BlockSpec((CHUNK, D_GMLP + D_ATTN), cur),
                      pl.BlockSpec((CHUNK, 1), cur),
                      pl.BlockSpec((CHUNK, 1), prev),
                      pl.BlockSpec((1, LANES), const2),
                      pl.BlockSpec((1, LANES), const2),
                      pl.BlockSpec((1, D_GMLP), const2),
                      pl.BlockSpec((1, D_GMLP), const2),
                      pl.BlockSpec((GROUPS, CHUNK, CHUNK), const3),
                      pl.BlockSpec((GROUPS, CHUNK, CHUNK), const3)],
            out_specs=[pl.BlockSpec((CHUNK, D_IN), lambda n, *_: (jnp.maximum(n - 1, 0), 0)),
                       pl.BlockSpec((GROUPS, CHUNK, CHUNK), const3),
                       pl.BlockSpec((GROUPS, CHUNK), const2),
                       pl.BlockSpec((1, D_GMLP), const2),
                       pl.BlockSpec((1, D_GMLP), const2),
                       pl.BlockSpec((1, LANES), const2)],
            scratch_shapes=[pltpu.VMEM((CHUNK, D_IN), BF16), pltpu.VMEM((CHUNK, 2 * D_KV), F32),
                            pltpu.VMEM((GROUPS, CHUNK, CHUNK), F32)]),
        out_shape=[jax.ShapeDtypeStruct((s, D_IN), BF16),
                   jax.ShapeDtypeStruct((GROUPS, CHUNK, CHUNK), F32),
                   jax.ShapeDtypeStruct((GROUPS, CHUNK), F32),
                   jax.ShapeDtypeStruct((1, D_GMLP), F32),
                   jax.ShapeDtypeStruct((1, D_GMLP), F32),
                   jax.ShapeDtypeStruct((1, LANES), F32)],
        compiler_params=_cparams(("arbitrary",)),
    )(sinks, proj, proj, dcat, pos_col, pos_col, freq, sign, ln_g, ln_b, w_sp, b_sp_rows)


def _adamw_math(w, g, m, v):
    m = ADAM_B1 * m + (1.0 - ADAM_B1) * g
    v = ADAM_B2 * v + (1.0 - ADAM_B2) * (g * g)
    m_hat = m / (1.0 - ADAM_B1 ** ADAM_STEP)
    v_hat = v / (1.0 - ADAM_B2 ** ADAM_STEP)
    delta = -ADAM_LR * (m_hat / (jnp.sqrt(v_hat) + ADAM_EPS) + ADAM_WD * w)
    return delta, m, v


def adamw_shard(terms, w, m, v, name):
    r, c = w.shape
    tr = _tile(r, (224, 256, 128, 8))
    n_terms = len(terms)

    def body(*refs):
        w_ref, m_ref, v_ref, g_ref, d_ref, nm_ref, nv_ref = refs[n_terms:]
        g = None
        for ref, (_, slots) in zip(refs[:n_terms], terms):
            for k in range(slots):
                part = ref[k].astype(F32)
                g = part if g is None else g + part
        g_ref[...] = g
        d_ref[...], nm_ref[...], nv_ref[...] = _adamw_math(w_ref[...], g, m_ref[...], v_ref[...])

    spec = pl.BlockSpec((tr, c), lambda i: (i, 0))
    return pl.pallas_call(
        body, name=name, grid=(r // tr,),
        in_specs=[pl.BlockSpec((slots, tr, c), lambda i: (0, i, 0)) for _, slots in terms] + [spec] * 3,
        out_specs=[spec] * 4, out_shape=[jax.ShapeDtypeStruct((r, c), F32)] * 4,
        compiler_params=_cparams(("arbitrary",)),
    )(*[a for a, _ in terms], w, m, v)


def adamw_small(gathered, lane_windows, params):
    n_par = len(params)

    def body(*refs):
        g_refs = refs[:n_par + 1]
        wmv_refs = refs[n_par + 1:4 * n_par + 1]
        out_refs = refs[4 * n_par + 1:]

        def total(ref):
            acc = ref[0]
            for dev in range(1, N_DEV):
                acc = acc + ref[dev]
            return acc

        for i in range(n_par):
            w_ref, m_ref, v_ref = wmv_refs[3 * i:3 * i + 3]
            g = total(g_refs[i])
            if lane_windows[i] is not None:
                start, size = lane_windows[i]
                g = g[..., start:start + size]
            delta, new_m, new_v = _adamw_math(w_ref[...], g, m_ref[...], v_ref[...])
            for ref, val in zip(out_refs[4 * i:4 * i + 4], (g, delta, new_m, new_v)):
                ref[...] = val
        out_refs[4 * n_par][...] = total(g_refs[n_par])

    flat = [a for wmv in params for a in wmv]
    out_shape = [jax.ShapeDtypeStruct(w.shape, F32) for (w, _, _) in params for _ in range(4)]
    out_shape.append(jax.ShapeDtypeStruct(gathered[-1].shape[1:], F32))
    outs = pl.pallas_call(body, name="adamw_small", out_shape=out_shape, compiler_params=_cparams())(*gathered, *flat)
    return [tuple(outs[4 * i:4 * i + 4]) for i in range(n_par)], outs[-1]


def kernel(x, positions, g_pre, w_in, b_qkv, ln_v_g, ln_v_b, w_spatial, b_spatial, attn_sinks, w_out, g_post, loss_target, m_g_pre, m_w_in, m_b_qkv, m_ln_v_g, m_ln_v_b, m_w_spatial, m_b_spatial, m_attn_sinks, m_w_out, m_g_post, v_g_pre, v_w_in, v_b_qkv, v_ln_v_g, v_ln_v_b, v_w_spatial, v_b_spatial, v_attn_sinks, v_w_out, v_g_post):
    x2, target = x[0], loss_target[0]
    seq = x2.shape[0]
    xi, yi, ci = _my_place()

    wt_shard = w_in[0].T.astype(BF16)
    wo_shard = w_out[0].astype(BF16)
    (wt,) = run_comm(gather_comm([wt_shard]), "allgather_w_in")

    bias = jnp.concatenate([jnp.zeros((1, OFF_Q), F32), b_qkv, jnp.zeros((1, D_ATTN), F32)], axis=1)
    proj, h, wo = in_proj(x2, g_pre, wt, bias, comms=[gather_comm([wo_shard])])
    pos_col = positions.reshape(seq, 1)
    half = HEAD_DIM // 2
    inv_freq = ROPE_THETA ** (-jnp.arange(half, dtype=F32) * (2.0 / HEAD_DIM))
    freq = jnp.tile(inv_freq, LANES // half).reshape(1, LANES)
    sign = jnp.tile(jnp.concatenate([-jnp.ones((half,), F32), jnp.ones((half,), F32)]), LANES // HEAD_DIM)
    sign = sign.reshape(1, LANES)
    b_rows = jnp.broadcast_to(b_spatial[0][:, :, None], (GROUPS, CHUNK, CHUNK))
    sinks = attn_sinks[0]
    cat = mixer_fwd(proj, pos_col, freq, sign, ln_v_g, ln_v_b, w_spatial[0], b_rows, sinks)
    dy, dout, d_g_post, loss_part = out_proj_loss(cat, wo, x2, target, g_post)

    dcat = matmul_nt(dy, wo, "out_proj_bwd")
    d_wo, _ = matmul_tn(cat, dy, 512, "w_out_grad")
    dproj, d_w_sp, d_b_sp, d_ln_g, d_ln_b, d_sinks = mixer_bwd(
        proj, dcat, pos_col, freq, sign, ln_v_g, ln_v_b, w_spatial[0], b_rows, sinks)
    d_wt, colsum, parts_wo = matmul_tn(dproj, h, 768, "w_in_grad", comms=[scatter_comm([d_wo])])

    owners = jnp.stack([4 * cx + 2 * cy + ci for cx, cy in (_chip_of(xi, yi, r) for r in range(4))]).astype(jnp.int32)
    (got_wt,) = run_comm(pair_comm([d_wt]), "grad_exchange_pair")
    sum_wt = pair_sum(d_wt, got_wt, owners, "grad_pair_sum_w_in")
    small_parts = [colsum, d_ln_g, d_ln_b, d_w_sp, d_b_sp, d_sinks, d_g_post, loss_part]
    grad_x, d_g_pre, *landed = in_proj_bwd(
        dproj, wt, x2, g_pre, dout, comms=[gather_comm(small_parts, stack=True), chips_comm([sum_wt])])
    far_wt = landed[-1]
    (all_g_pre,) = run_comm(gather_comm([d_g_pre], stack=True, direct=True), "allgather_g_pre_grad")
    gathered = [all_g_pre] + landed[:-1]
    windows = [None, (OFF_Q, D_QKV), None, None, None, None, (0, N_Q_HEADS), None]
    small = [(g_pre, m_g_pre, v_g_pre), (b_qkv, m_b_qkv, v_b_qkv), (ln_v_g, m_ln_v_g, v_ln_v_g),
             (ln_v_b, m_ln_v_b, v_ln_v_b), (w_spatial[0], m_w_spatial[0], v_w_spatial[0]),
             (b_spatial[0], m_b_spatial[0], v_b_spatial[0]), (attn_sinks, m_attn_sinks, v_attn_sinks),
             (g_post, m_g_post, v_g_post)]
    small_out, loss_row = adamw_small(gathered, windows, small)
    lead = [False, False, False, False, True, True, False, False]
    small_out = [tuple(a[None] if ld else a for a in leaf) for leaf, ld in zip(small_out, lead)]

    wt_out = adamw_shard([(sum_wt, 1), (far_wt, 3)], w_in[0].T, m_w_in[0].T, v_w_in[0].T, "adamw_w_in")
    wo_out = adamw_shard([(parts_wo, N_DEV)], w_out[0], m_w_out[0], v_w_out[0], "adamw_w_out")

    def leaves(k):
        gp, bq, lg, lb, ws, bs, sk, gpo = (leaf[k] for leaf in small_out)
        return [gp, wt_out[k].T[None], bq, lg, lb, ws, bs, sk, wo_out[k][None], gpo]

    return (loss_row[0, 0], grad_x[None], *leaves(0), *leaves(1), *leaves(2), *leaves(3))
```

```python
import functools

import jax
import jax.numpy as jnp
from jax import lax
from jax.experimental import pallas as pl
from jax.experimental.pallas import tpu as pltpu

F32 = jnp.float32
BF16 = jnp.bfloat16

D_MODEL = 2048
D_GMLP = 1024
D_ATTN = 1024
CHUNK = 128
GROUPS = 8
HEAD_DIM = 64
N_Q_HEADS = 16
N_KV_HEADS = 2
D_KV = N_KV_HEADS * HEAD_DIM
D_IN = 3 * D_GMLP + D_ATTN + 2 * D_KV + D_ATTN
OFF_U, OFF_V, OFF_ZA = 0, D_GMLP, 2 * D_GMLP
OFF_Q = 3 * D_GMLP
OFF_K = OFF_Q + D_ATTN
OFF_VA = OFF_K + D_KV
OFF_ZB = OFF_VA + D_KV
D_QKV = D_ATTN + 2 * D_KV
ROPE_THETA = 10000.0
EPS = 1e-6
SCALE = HEAD_DIM ** -0.5
NEG = -1e30
N_PAIRS = N_Q_HEADS // 2
PAIRS_PER_KV = N_PAIRS // N_KV_HEADS

ADAM_LR = 0.001
ADAM_B1 = 0.9
ADAM_B2 = 0.999
ADAM_EPS = 1e-08
ADAM_WD = 0.01
ADAM_STEP = 10

N_DEV = 8
LANES = 128
VMEM_LIMIT = 56 * 1024 * 1024

MESH = pl.DeviceIdType.MESH
ANY = pl.BlockSpec(memory_space=pl.ANY)


def _cparams(sem=None):
    return pltpu.CompilerParams(dimension_semantics=sem, vmem_limit_bytes=VMEM_LIMIT)


def _tile(n, prefs):
    for t in prefs:
        if n % t == 0:
            return t
    return n


def _sigmoid(z):
    return 1.0 / (1.0 + jnp.exp(-z))


def _dot(a, b, ca, cb):
    return lax.dot_general(a, b, (((ca,), (cb,)), ((), ())), preferred_element_type=F32)


def _my_place():
    return lax.axis_index("x"), lax.axis_index("y"), lax.axis_index("c")


def _chip_of(x, y, r):
    return (x ^ (r & 1), y ^ (r >> 1))


def _peer(x, y, c, k):
    return (x ^ (k >> 2), y ^ ((k >> 1) & 1), c ^ (k & 1))


def _index(px, py, pc):
    return 4 * px + 2 * py + pc


class _Comm:
    def __init__(self, inputs, out_shape, scratch, bind):
        self.inputs, self.out_shape, self.scratch, self.bind = list(inputs), list(out_shape), list(scratch), bind


def gather_comm(shards, stack=False, direct=False):
    n_arr = len(shards)

    def bind(ins, outs, sems):
        send_sems, recv_sems, local_sems = sems
        x, y, c = _my_place()
        me, sibling = (x, y, c), (x, y, 1 - c)
        chips = [_chip_of(x, y, r) for r in (1, 2, 3)]

        def rows(a, px, py, pc):
            d = _index(px, py, pc)
            if stack:
                return outs[a].at[d]
            m = shards[a].shape[0]
            return outs[a].at[pl.ds(pl.multiple_of(d * m, 8), m), :]

        def copy(a, k, block, to, src=None):
            return pltpu.make_async_remote_copy(
                src_ref=rows(a, *block) if src is None else src, dst_ref=rows(a, *block),
                send_sem=send_sems.at[a * 7 + k], recv_sem=recv_sems.at[a * 7 + k],
                device_id=to, device_id_type=MESH)

        def mine(a):
            return pltpu.make_async_copy(ins[a], rows(a, *me), local_sems.at[a])

        def own_sends(a):
            if direct:
                return [copy(a, k - 1, me, _peer(x, y, c, k), src=ins[a]) for k in range(1, 8)]
            return ([copy(a, 0, me, sibling, src=ins[a])]
                    + [copy(a, 1 + j, me, (*chip, c), src=ins[a]) for j, chip in enumerate(chips)])

        def start():
            for a in range(n_arr):
                mine(a).start()
                for cp in own_sends(a):
                    cp.start()

        def relay():
            if direct:
                return
            for j, chip in enumerate(chips):
                for a in range(n_arr):
                    copy(a, 1 + j, (*chip, c), me).wait_recv()
                    copy(a, 4 + j, (*chip, c), sibling).start()

        def finish():
            for a in range(n_arr):
                if direct:
                    for k in range(1, 8):
                        copy(a, k - 1, _peer(x, y, c, k), me).wait_recv()
                else:
                    copy(a, 0, sibling, me).wait_recv()
                    for j, chip in enumerate(chips):
                        copy(a, 4 + j, (*chip, 1 - c), me).wait_recv()
                        copy(a, 4 + j, (*chip, c), sibling).wait_send()
                for cp in own_sends(a):
                    cp.wait_send()
                mine(a).wait()

        return start, relay, finish

    def gathered(s):
        return (N_DEV, *s.shape) if stack else (N_DEV * s.shape[0], s.shape[1])

    return _Comm(shards, [jax.ShapeDtypeStruct(gathered(s), s.dtype) for s in shards],
                 [pltpu.SemaphoreType.DMA((7 * n_arr,)), pltpu.SemaphoreType.DMA((7 * n_arr,)),
                  pltpu.SemaphoreType.DMA((n_arr,))], bind)


def scatter_comm(parts):
    n_arr = len(parts)

    def bind(ins, outs, sems):
        send_sems, recv_sems, local_sems = sems
        x, y, c = _my_place()
        my_index = _index(x, y, c)

        def block(a, d):
            m = parts[a].shape[0] // N_DEV
            return ins[a].at[pl.ds(pl.multiple_of(d * m, 16), m), :]

        def copy(a, k, slot):
            peer = _peer(x, y, c, k)
            return pltpu.make_async_remote_copy(
                src_ref=block(a, _index(*peer)), dst_ref=outs[a].at[slot],
                send_sem=send_sems.at[a * 7 + k - 1], recv_sem=recv_sems.at[a * 7 + k - 1],
                device_id=peer, device_id_type=MESH)

        def mine(a):
            return pltpu.make_async_copy(block(a, my_index), outs[a].at[my_index], local_sems.at[a])

        def start():
            for a in range(n_arr):
                mine(a).start()
                for k in range(1, 8):
                    copy(a, k, my_index).start()

        def finish():
            for a in range(n_arr):
                for k in range(1, 8):
                    copy(a, k, _index(*_peer(x, y, c, k))).wait_recv()
                    copy(a, k, my_index).wait_send()
                mine(a).wait()

        return start, (lambda: None), finish

    return _Comm(parts, [jax.ShapeDtypeStruct((N_DEV, p.shape[0] // N_DEV, p.shape[1]), p.dtype) for p in parts],
                 [pltpu.SemaphoreType.DMA((7 * n_arr,)), pltpu.SemaphoreType.DMA((7 * n_arr,)),
                  pltpu.SemaphoreType.DMA((n_arr,))], bind)


def pair_comm(parts):
    n_arr = len(parts)

    def bind(ins, outs, sems):
        send_sems, recv_sems = sems
        x, y, c = _my_place()

        def copies():
            out = []
            for a in range(n_arr):
                m = parts[a].shape[0] // N_DEV
                for r in range(4):
                    owner = _index(*_chip_of(x, y, r), 1 - c)
                    out.append(pltpu.make_async_remote_copy(
                        src_ref=ins[a].at[pl.ds(pl.multiple_of(owner * m, 16), m), :], dst_ref=outs[a].at[r],
                        send_sem=send_sems.at[a * 4 + r], recv_sem=recv_sems.at[a * 4 + r],
                        device_id=(x, y, 1 - c), device_id_type=MESH))
            return out

        def start():
            for cp in copies():
                cp.start()

        def finish():
            for cp in copies():
                cp.wait_recv()
                cp.wait_send()

        return start, (lambda: None), finish

    return _Comm(parts, [jax.ShapeDtypeStruct((4, p.shape[0] // N_DEV, p.shape[1]), p.dtype) for p in parts],
                 [pltpu.SemaphoreType.DMA((4 * n_arr,)), pltpu.SemaphoreType.DMA((4 * n_arr,))], bind)


def chips_comm(sums):
    n_arr = len(sums)

    def bind(ins, outs, sems):
        send_sems, recv_sems = sems
        x, y, c = _my_place()

        def copies():
            return [pltpu.make_async_remote_copy(
                src_ref=ins[a].at[r], dst_ref=outs[a].at[r - 1],
                send_sem=send_sems.at[a * 3 + r - 1], recv_sem=recv_sems.at[a * 3 + r - 1],
                device_id=(*_chip_of(x, y, r), c), device_id_type=MESH) for a in range(n_arr) for r in (1, 2, 3)]

        def start():
            for cp in copies():
                cp.start()

        def finish():
            for cp in copies():
                cp.wait_recv()
                cp.wait_send()

        return start, (lambda: None), finish

    return _Comm(sums, [jax.ShapeDtypeStruct((3,) + s.shape[1:], s.dtype) for s in sums],
                 [pltpu.SemaphoreType.DMA((3 * n_arr,)), pltpu.SemaphoreType.DMA((3 * n_arr,))], bind)


def run_comm(comm, name):
    n_in, n_out = len(comm.inputs), len(comm.out_shape)

    def body(*refs):
        start, relay, finish = comm.bind(refs[:n_in], refs[n_in:n_in + n_out], refs[n_in + n_out:])
        start()
        relay()
        finish()

    outs = pl.pallas_call(body, name=name, out_shape=comm.out_shape, in_specs=[ANY] * n_in,
                          out_specs=[ANY] * n_out, scratch_shapes=comm.scratch)(*comm.inputs)
    return list(outs)


class _Hosted:
    def __init__(self, comms):
        self.comms = list(comms)
        self.inputs = [a for cm in self.comms for a in cm.inputs]
        self.out_shape = [s for cm in self.comms for s in cm.out_shape]
        self.scratch = [s for cm in self.comms for s in cm.scratch]
        self.in_specs = [ANY] * len(self.inputs)
        self.out_specs = [ANY] * len(self.out_shape)

    def split(self, refs, n_in, n_out, n_scratch):
        ni, no = len(self.inputs), len(self.out_shape)
        ins, rest = refs[:n_in], refs[n_in:]
        c_ins, rest = rest[:ni], rest[ni:]
        outs, rest = rest[:n_out], rest[n_out:]
        c_outs, rest = rest[:no], rest[no:]
        scratch, c_sems = rest[:n_scratch], rest[n_scratch:]
        phases = []
        for cm in self.comms:
            a, b, s = len(cm.inputs), len(cm.out_shape), len(cm.scratch)
            phases.append(cm.bind(c_ins[:a], c_outs[:b], c_sems[:s]))
            c_ins, c_outs, c_sems = c_ins[a:], c_outs[b:], c_sems[s:]
        return ins, outs, scratch, phases


def _before_step(phases, step, n_steps):
    if not phases:
        return

    @pl.when(step == 0)
    def _():
        for start, _, _ in phases:
            start()

    @pl.when(step == n_steps // 2)
    def _():
        for _, relay, _ in phases:
            relay()


def _after_step(phases, step, n_steps):
    if not phases:
        return

    @pl.when(step == n_steps - 1)
    def _():
        for _, _, finish in phases:
            finish()


def pair_sum(part, got, owners, name):
    m, n = got.shape[1:]

    def body(own_ref, mine_ref, got_ref, out_ref):
        del own_ref
        out_ref[...] = (mine_ref[...].astype(F32) + got_ref[...].astype(F32)).astype(out_ref.dtype)

    return pl.pallas_call(
        body, name=name,
        grid_spec=pltpu.PrefetchScalarGridSpec(
            num_scalar_prefetch=1, grid=(4,),
            in_specs=[pl.BlockSpec((m, n), lambda r, own: (own[r], 0)),
                      pl.BlockSpec((None, m, n), lambda r, own: (r, 0, 0))],
            out_specs=pl.BlockSpec((None, m, n), lambda r, own: (r, 0, 0))),
        out_shape=jax.ShapeDtypeStruct((4, m, n), got.dtype),
        compiler_params=_cparams(("arbitrary",)),
    )(owners, part, got)


def in_proj(x, g_pre, wt, bias, comms=()):
    s, d = x.shape
    tm = _tile(s, (512, 256, 128))
    tn = 768
    ni, nj = s // tm, D_IN // tn
    hosted = _Hosted(comms)

    def body(*refs):
        (x_ref, g_ref, w_ref, b_ref), (proj_ref, h_ref), _, phases = hosted.split(refs, 4, 2, 0)
        step = pl.program_id(0) * nj + pl.program_id(1)
        _before_step(phases, step, ni * nj)

        @pl.when(pl.program_id(1) == 0)
        def _():
            xv = x_ref[...]
            r = lax.rsqrt(jnp.mean(xv * xv, axis=-1, keepdims=True) + EPS)
            h_ref[...] = (xv * r * g_ref[...]).astype(BF16)

        acc = _dot(h_ref[...], w_ref[...], 1, 1)
        proj_ref[...] = (acc + b_ref[...]).astype(BF16)
        _after_step(phases, step, ni * nj)

    return pl.pallas_call(
        body, name="in_proj", grid=(ni, nj),
        in_specs=[pl.BlockSpec((tm, d), lambda i, j: (i, 0)),
                  pl.BlockSpec((1, d), lambda i, j: (0, 0)),
                  pl.BlockSpec((tn, d), lambda i, j: (j, 0)),
                  pl.BlockSpec((1, tn), lambda i, j: (0, j))] + hosted.in_specs,
        out_specs=[pl.BlockSpec((tm, tn), lambda i, j: (i, j)),
                   pl.BlockSpec((tm, d), lambda i, j: (i, 0))] + hosted.out_specs,
        out_shape=[jax.ShapeDtypeStruct((s, D_IN), BF16), jax.ShapeDtypeStruct((s, d), BF16)] + hosted.out_shape,
        scratch_shapes=hosted.scratch,
        compiler_params=_cparams(("arbitrary", "arbitrary")),
    )(x, g_pre, wt, bias, *hosted.inputs)


def out_proj_loss(cat, w_out, x, target, g_post):
    s, d = x.shape
    tm = _tile(s, (256, 128))

    def body(cat_ref, w_ref, x_ref, t_ref, g_ref, dy_ref, dout_ref, dg_ref, loss_ref):
        @pl.when(pl.program_id(0) == 0)
        def _():
            dg_ref[...] = jnp.zeros_like(dg_ref)
            loss_ref[...] = jnp.zeros_like(loss_ref)

        yv = _dot(cat_ref[...], w_ref[...], 1, 0)
        r = lax.rsqrt(jnp.mean(yv * yv, axis=-1, keepdims=True) + EPS)
        nrm = yv * r
        g = g_ref[...]
        err = x_ref[...] + nrm * g - t_ref[...]
        loss_ref[...] += 0.5 * jnp.sum(jnp.sum(err * err, axis=-1, keepdims=True), axis=0, keepdims=True) / d
        dout = err * (1.0 / d)
        dout_ref[...] = dout
        dg_ref[...] += jnp.sum(dout * nrm, axis=0, keepdims=True)
        dn = dout * g
        dy = r * (dn - nrm * jnp.mean(dn * nrm, axis=-1, keepdims=True))
        dy_ref[...] = dy.astype(BF16)

    return pl.pallas_call(
        body, name="out_proj_loss", grid=(s // tm,),
        in_specs=[pl.BlockSpec((tm, d), lambda i: (i, 0)),
                  pl.BlockSpec((d, d), lambda i: (0, 0)),
                  pl.BlockSpec((tm, d), lambda i: (i, 0)),
                  pl.BlockSpec((tm, d), lambda i: (i, 0)),
                  pl.BlockSpec((1, d), lambda i: (0, 0))],
        out_specs=[pl.BlockSpec((tm, d), lambda i: (i, 0)),
                   pl.BlockSpec((tm, d), lambda i: (i, 0)),
                   pl.BlockSpec((1, d), lambda i: (0, 0)),
                   pl.BlockSpec((1, LANES), lambda i: (0, 0))],
        out_shape=[jax.ShapeDtypeStruct((s, d), BF16), jax.ShapeDtypeStruct((s, d), F32),
                   jax.ShapeDtypeStruct((1, d), F32), jax.ShapeDtypeStruct((1, LANES), F32)],
        compiler_params=_cparams(("arbitrary",)),
    )(cat, w_out, x, target, g_post)


def matmul_nt(a, b, name):
    m, k = a.shape
    n = b.shape[0]
    tm = _tile(m, (512, 256, 128))

    def body(a_ref, b_ref, o_ref):
        o_ref[...] = _dot(a_ref[...], b_ref[...], 1, 1).astype(o_ref.dtype)

    return pl.pallas_call(
        body, name=name, grid=(m // tm,),
        in_specs=[pl.BlockSpec((tm, k), lambda i: (i, 0)), pl.BlockSpec((n, k), lambda i: (0, 0))],
        out_specs=pl.BlockSpec((tm, n), lambda i: (i, 0)),
        out_shape=jax.ShapeDtypeStruct((m, n), BF16),
        compiler_params=_cparams(("arbitrary",)),
    )(a, b)


def matmul_tn(a, b, tm, name, comms=()):
    k, m = a.shape
    n = b.shape[1]
    tk = _tile(k, (1024, 512, 256, 128))
    nk = k // tk
    hosted = _Hosted(comms)

    def body(*refs):
        (a_ref, b_ref), (o_ref, cs_ref), (acc_ref, csacc_ref), phases = hosted.split(refs, 2, 2, 2)
        kk = pl.program_id(1)
        step = pl.program_id(0) * nk + kk
        _before_step(phases, step, (m // tm) * nk)

        @pl.when(kk == 0)
        def _():
            acc_ref[...] = jnp.zeros_like(acc_ref)
            csacc_ref[...] = jnp.zeros_like(csacc_ref)

        av = a_ref[...]
        acc_ref[...] += _dot(av, b_ref[...], 0, 0)
        csacc_ref[...] += jnp.sum(av.astype(F32), axis=0, keepdims=True)

        @pl.when(kk == nk - 1)
        def _():
            o_ref[...] = acc_ref[...].astype(o_ref.dtype)
            cs_ref[...] = csacc_ref[...]

        _after_step(phases, step, (m // tm) * nk)

    return pl.pallas_call(
        body, name=name, grid=(m // tm, nk),
        in_specs=[pl.BlockSpec((tk, tm), lambda i, j: (j, i)),
                  pl.BlockSpec((tk, n), lambda i, j: (j, 0))] + hosted.in_specs,
        out_specs=[pl.BlockSpec((tm, n), lambda i, j: (i, 0)),
                   pl.BlockSpec((1, tm), lambda i, j: (0, i))] + hosted.out_specs,
        out_shape=[jax.ShapeDtypeStruct((m, n), BF16), jax.ShapeDtypeStruct((1, m), F32)] + hosted.out_shape,
        scratch_shapes=[pltpu.VMEM((tm, n), F32), pltpu.VMEM((1, tm), F32)] + hosted.scratch,
        compiler_params=_cparams(("arbitrary", "arbitrary")),
    )(a, b, *hosted.inputs)


def in_proj_bwd(dproj, wt, x, g_pre, dout, comms=()):
    s, d = x.shape
    tm = _tile(s, (512, 256, 128))
    tk = 768
    nk = D_IN // tk
    hosted = _Hosted(comms)

    def body(*refs):
        (dp_ref, w_ref, x_ref, g_ref, dout_ref), (gx_ref, dg_ref), (acc_ref,), phases = hosted.split(refs, 5, 2, 1)
        i, kk = pl.program_id(0), pl.program_id(1)
        step = i * nk + kk
        _before_step(phases, step, (s // tm) * nk)

        @pl.when(jnp.logical_and(i == 0, kk == 0))
        def _():
            dg_ref[...] = jnp.zeros_like(dg_ref)

        @pl.when(kk == 0)
        def _():
            acc_ref[...] = jnp.zeros_like(acc_ref)

        acc_ref[...] += _dot(dp_ref[...], w_ref[...], 1, 0)

        @pl.when(kk == nk - 1)
        def _():
            for c0 in range(0, tm, CHUNK):
                rows = slice(c0, c0 + CHUNK)
                dh = acc_ref[rows, :]
                xv = x_ref[rows, :]
                r = lax.rsqrt(jnp.mean(xv * xv, axis=-1, keepdims=True) + EPS)
                xn = xv * r
                dg_ref[...] += jnp.sum(dh * xn, axis=0, keepdims=True)
                dn = dh * g_ref[...]
                gx_ref[rows, :] = dout_ref[rows, :] + r * (dn - xn * jnp.mean(dn * xn, axis=-1, keepdims=True))

        _after_step(phases, step, (s // tm) * nk)

    return pl.pallas_call(
        body, name="in_proj_bwd", grid=(s // tm, nk),
        in_specs=[pl.BlockSpec((tm, tk), lambda i, j: (i, j)),
                  pl.BlockSpec((tk, d), lambda i, j: (j, 0)),
                  pl.BlockSpec((tm, d), lambda i, j: (i, 0)),
                  pl.BlockSpec((1, d), lambda i, j: (0, 0)),
                  pl.BlockSpec((tm, d), lambda i, j: (i, 0))] + hosted.in_specs,
        out_specs=[pl.BlockSpec((tm, d), lambda i, j: (i, 0)),
                   pl.BlockSpec((1, d), lambda i, j: (0, 0))] + hosted.out_specs,
        out_shape=[jax.ShapeDtypeStruct((s, d), F32), jax.ShapeDtypeStruct((1, d), F32)] + hosted.out_shape,
        scratch_shapes=[pltpu.VMEM((tm, d), F32)] + hosted.scratch,
        compiler_params=_cparams(("arbitrary", "arbitrary")),
    )(dproj, wt, x, g_pre, dout, *hosted.inputs)


def _lane_iota(shape):
    return lax.broadcasted_iota(jnp.int32, shape, len(shape) - 1)


def rope_tables(pos_col, freq, sign, comms=()):
    s = pos_col.shape[0]
    tr = _tile(s, (512, 256, 128))
    hosted = _Hosted(comms)

    def body(*refs):
        (pos_ref, freq_ref, sign_ref), (out_ref,), _, phases = hosted.split(refs, 3, 1, 0)
        _before_step(phases, pl.program_id(0), s // tr)
        ang = pos_ref[...].astype(F32) * freq_ref[...]
        out_ref[:, :LANES] = jnp.cos(ang)
        out_ref[:, LANES:] = jnp.sin(ang) * sign_ref[...]
        _after_step(phases, pl.program_id(0), s // tr)

    return pl.pallas_call(
        body, name="rope_tables", grid=(s // tr,),
        in_specs=[pl.BlockSpec((tr, 1), lambda i: (i, 0)), pl.BlockSpec((1, LANES), lambda i: (0, 0)),
                  pl.BlockSpec((1, LANES), lambda i: (0, 0))] + hosted.in_specs,
        out_specs=[pl.BlockSpec((tr, 2 * LANES), lambda i: (i, 0))] + hosted.out_specs,
        out_shape=[jax.ShapeDtypeStruct((s, 2 * LANES), F32)] + hosted.out_shape,
        scratch_shapes=hosted.scratch,
        compiler_params=_cparams(("arbitrary",)),
    )(pos_col, freq, sign, *hosted.inputs)


def _partner(v):
    low = (_lane_iota(v.shape) % HEAD_DIM) < (HEAD_DIM // 2)
    return jnp.where(low, pltpu.roll(v, LANES - HEAD_DIM // 2, 1), pltpu.roll(v, HEAD_DIM // 2, 1))


def _rope(v, cos, sin_signed):
    return v * cos + _partner(v) * sin_signed


def _rope_transposed(dv, cos, sin_signed):
    return dv * cos - _partner(dv) * sin_signed


def _both_halves(v, kv_head):
    keep = (_lane_iota(v.shape) >= HEAD_DIM) if kv_head else (_lane_iota(v.shape) < HEAD_DIM)
    return jnp.where(keep, v, pltpu.roll(v, HEAD_DIM, 1))


def _fold_halves(acc):
    return acc + pltpu.roll(acc, HEAD_DIM, 1)


def _by_half(a, b):
    shape = jnp.broadcast_shapes(jnp.shape(a), jnp.shape(b))
    return jnp.where(_lane_iota(shape) < HEAD_DIM, a, b)


def _stack_heads(pair):
    return jnp.concatenate([_by_half(pair, 0.0), _by_half(0.0, pair)], axis=0)


def _band_bias(has_prev):
    i = lax.broadcasted_iota(jnp.int32, (2 * CHUNK, 2 * CHUNK), 0) % CHUNK
    j = lax.broadcasted_iota(jnp.int32, (2 * CHUNK, 2 * CHUNK), 1)
    band = jnp.logical_and(j > i, j <= i + CHUNK)
    return jnp.where(jnp.logical_and(band, jnp.logical_or(j >= CHUNK, has_prev)), 0.0, NEG)


def _probs(qm2, kk2, bias, sink_col):
    sc = _dot(qm2, kk2, 1, 1) + bias
    mx = jnp.maximum(jnp.max(sc, axis=-1, keepdims=True), sink_col)
    p = jnp.exp(sc - mx)
    es = jnp.exp(sink_col - mx)
    inv = 1.0 / (jnp.sum(p, axis=-1, keepdims=True) + es)
    return p * inv, es * inv


def _sink_col(sinks_ref, pair):
    row = lax.broadcasted_iota(jnp.int32, (2 * CHUNK, 1), 0)
    return jnp.where(row < CHUNK, sinks_ref[2 * pair], sinks_ref[2 * pair + 1])


def _layer_norm_parts(v):
    mu = jnp.mean(v, axis=-1, keepdims=True)
    xc = v - mu
    rstd = lax.rsqrt(jnp.mean(xc * xc, axis=-1, keepdims=True) + EPS)
    return xc * rstd, rstd


def _masked_spatial(w_ref, g):
    t = lax.broadcasted_iota(jnp.int32, (CHUNK, CHUNK), 0)
    sidx = lax.broadcasted_iota(jnp.int32, (CHUNK, CHUNK), 1)
    return jnp.where(t >= sidx, w_ref[g], 0.0).astype(BF16)


def _keys_values(kv_ref, kvp_ref, rope_ref, ropep_ref):
    cos_c, sin_c = rope_ref[:, :LANES], rope_ref[:, LANES:]
    cos_p, sin_p = ropep_ref[:, :LANES], ropep_ref[:, LANES:]
    k_c = _rope(kv_ref[:, :D_KV].astype(F32), cos_c, sin_c)
    k_p = _rope(kvp_ref[:, :D_KV].astype(F32), cos_p, sin_p)
    keys = jnp.concatenate([k_p, k_c], axis=0)
    vals = jnp.concatenate([kvp_ref[:, D_KV:], kv_ref[:, D_KV:]], axis=0).astype(F32)
    return keys, vals, (cos_c, sin_c, cos_p, sin_p)


def mixer_fwd(proj, rope, ln_g, ln_b, w_sp, b_sp_rows, sinks):
    s = proj.shape[0]
    nb = s // CHUNK

    def body(sinks_ref, proj_ref, kvp_ref, rope_ref, ropep_ref, lng_ref, lnb_ref, w_ref, b_ref, cat_ref):
        n = pl.program_id(0)
        xhat, _ = _layer_norm_parts(proj_ref[:, OFF_V:OFF_V + D_GMLP].astype(F32))
        vnb = (xhat * lng_ref[...] + lnb_ref[...]).astype(BF16)
        for g in range(GROUPS):
            cols = slice(g * CHUNK, (g + 1) * CHUNK)
            mixed = _dot(_masked_spatial(w_ref, g), vnb[:, cols], 1, 0) + b_ref[g]
            za = proj_ref[:, OFF_ZA + g * CHUNK:OFF_ZA + (g + 1) * CHUNK].astype(F32)
            u = proj_ref[:, OFF_U + g * CHUNK:OFF_U + (g + 1) * CHUNK].astype(F32)
            cat_ref[:, cols] = (u * mixed * (za * _sigmoid(za))).astype(BF16)
        kv_ref = proj_ref.at[:, OFF_K:OFF_K + 2 * D_KV]
        keys, vals, (cos_c, sin_c, _, _) = _keys_values(kv_ref, kvp_ref, rope_ref, ropep_ref)
        cos_q, sin_q = cos_c * SCALE, sin_c * SCALE
        bias = _band_bias(n > 0)
        for pair in range(N_PAIRS):
            kvh = pair // PAIRS_PER_KV
            kk2 = _both_halves(keys, kvh).astype(BF16)
            vv2 = _both_halves(vals, kvh).astype(BF16)
            qcols = slice(OFF_Q + pair * LANES, OFF_Q + (pair + 1) * LANES)
            q_pair = _rope(proj_ref[:, qcols].astype(F32), cos_q, sin_q)
            p, _ = _probs(_stack_heads(q_pair).astype(BF16), kk2, bias, _sink_col(sinks_ref, pair))
            o2 = _dot(p.astype(BF16), vv2, 1, 0)
            out_pair = _by_half(o2[:CHUNK], o2[CHUNK:])
            zb = proj_ref[:, OFF_ZB + pair * LANES:OFF_ZB + (pair + 1) * LANES].astype(F32)
            cat_ref[:, D_GMLP + pair * LANES:D_GMLP + (pair + 1) * LANES] = (
                out_pair * (zb * _sigmoid(zb))).astype(BF16)

    prev = lambda n, *_: (jnp.maximum(n - 1, 0), 0)
    kv_block = OFF_K // (2 * D_KV)
    return pl.pallas_call(
        body, name="mixer_fwd",
        grid_spec=pltpu.PrefetchScalarGridSpec(
            num_scalar_prefetch=1, grid=(nb,),
            in_specs=[pl.BlockSpec((CHUNK, D_IN), lambda n, *_: (n, 0)),
                      pl.BlockSpec((CHUNK, 2 * D_KV), lambda n, *_: (jnp.maximum(n - 1, 0), kv_block)),
                      pl.BlockSpec((CHUNK, 2 * LANES), lambda n, *_: (n, 0)),
                      pl.BlockSpec((CHUNK, 2 * LANES), prev),
                      pl.BlockSpec((1, D_GMLP), lambda n, *_: (0, 0)),
                      pl.BlockSpec((1, D_GMLP), lambda n, *_: (0, 0)),
                      pl.BlockSpec((GROUPS, CHUNK, CHUNK), lambda n, *_: (0, 0, 0)),
                      pl.BlockSpec((GROUPS, CHUNK, CHUNK), lambda n, *_: (0, 0, 0))],
            out_specs=pl.BlockSpec((CHUNK, D_GMLP + D_ATTN), lambda n, *_: (n, 0))),
        out_shape=jax.ShapeDtypeStruct((s, D_GMLP + D_ATTN), BF16),
        compiler_params=_cparams(("arbitrary",)),
    )(sinks, proj, proj, rope, rope, ln_g, ln_b, w_sp, b_sp_rows)


def mixer_bwd(proj, dcat, rope, ln_g, ln_b, w_sp, b_sp_rows, sinks):
    s = proj.shape[0]
    nb = s // CHUNK

    def body(sinks_ref, proj_ref, kvp_ref, dcat_ref, rope_ref, ropep_ref, lng_ref, lnb_ref, w_ref, b_ref,
             dproj_ref, dw_ref, db_ref, dlng_ref, dlnb_ref, dsink_ref,
             pend_ref, pend_kv_ref, dbacc_ref):
        n = pl.program_id(0)

        @pl.when(n == 0)
        def _():
            dw_ref[...] = jnp.zeros_like(dw_ref)
            dbacc_ref[...] = jnp.zeros_like(dbacc_ref)
            dlng_ref[...] = jnp.zeros_like(dlng_ref)
            dlnb_ref[...] = jnp.zeros_like(dlnb_ref)
            dsink_ref[...] = jnp.zeros_like(dsink_ref)

        def flush(dkv_prev):
            @pl.when(n > 0)
            def _():
                dproj_ref[...] = pend_ref[...]
                dproj_ref[:, OFF_K:OFF_K + 2 * D_KV] = (pend_kv_ref[...] + dkv_prev).astype(BF16)

        @pl.when(n < nb)
        def _():
            kv_ref = proj_ref.at[:, OFF_K:OFF_K + 2 * D_KV]
            keys, vals, (cos_c, sin_c, cos_p, sin_p) = _keys_values(kv_ref, kvp_ref, rope_ref, ropep_ref)
            cos_q, sin_q = cos_c * SCALE, sin_c * SCALE
            bias = _band_bias(n > 0)
            lane_row = _lane_iota((1, LANES))
            dsink = jnp.zeros((1, LANES), F32)
            dk_heads, dv_heads = [], []
            dq_pairs, dzb_pairs = [], []
            for kvh in range(N_KV_HEADS):
                kk2 = _both_halves(keys, kvh).astype(BF16)
                vv2 = _both_halves(vals, kvh).astype(BF16)
                dkk = jnp.zeros((2 * CHUNK, LANES), F32)
                dvv = jnp.zeros((2 * CHUNK, LANES), F32)
                for pair in range(kvh * PAIRS_PER_KV, (kvh + 1) * PAIRS_PER_KV):
                    qcols = slice(OFF_Q + pair * LANES, OFF_Q + (pair + 1) * LANES)
                    q_pair = _rope(proj_ref[:, qcols].astype(F32), cos_q, sin_q)
                    qm2 = _stack_heads(q_pair).astype(BF16)
                    p, p_sink = _probs(qm2, kk2, bias, _sink_col(sinks_ref, pair))
                    pb = p.astype(BF16)
                    o2 = _dot(pb, vv2, 1, 0)
                    out_pair = _by_half(o2[:CHUNK], o2[CHUNK:])
                    zb = proj_ref[:, OFF_ZB + pair * LANES:OFF_ZB + (pair + 1) * LANES].astype(F32)
                    sg = _sigmoid(zb)
                    dyb = dcat_ref[:, D_GMLP + pair * LANES:D_GMLP + (pair + 1) * LANES].astype(F32)
                    dzb_pairs.append((dyb * out_pair * (sg * (1.0 + zb * (1.0 - sg)))).astype(BF16))
                    dom2 = _stack_heads(dyb * (zb * sg)).astype(BF16)
                    dp = _dot(dom2, vv2, 1, 1)
                    delta = jnp.sum(p * dp, axis=-1, keepdims=True)
                    ds = p * (dp - delta)
                    dsk = -(p_sink * delta)
                    dsink = dsink + jnp.where(lane_row == 2 * pair,
                                              jnp.sum(dsk[:CHUNK], axis=0, keepdims=True), 0.0)
                    dsink = dsink + jnp.where(lane_row == 2 * pair + 1,
                                              jnp.sum(dsk[CHUNK:], axis=0, keepdims=True), 0.0)
                    dsb = ds.astype(BF16)
                    dq2 = _dot(dsb, kk2, 1, 0)
                    dq_pairs.append(_rope_transposed(_by_half(dq2[:CHUNK], dq2[CHUNK:]), cos_q, sin_q).astype(BF16))
                    dkk = dkk + _dot(dsb, qm2, 0, 0)
                    dvv = dvv + _dot(pb, dom2, 0, 0)
                dk_heads.append(_fold_halves(dkk))
                dv_heads.append(_fold_halves(dvv))
            dk_rot = _by_half(dk_heads[0], dk_heads[1])
            dv_all = _by_half(dv_heads[0], dv_heads[1])
            dk_p = _rope_transposed(dk_rot[:CHUNK], cos_p, sin_p)
            dk_c = _rope_transposed(dk_rot[CHUNK:], cos_c, sin_c)
            flush(jnp.concatenate([dk_p, dv_all[:CHUNK]], axis=1))
            dsink_ref[...] += dsink
            pend_kv_ref[...] = jnp.concatenate([dk_c, dv_all[CHUNK:]], axis=1)
            for pair in range(N_PAIRS):
                pend_ref[:, OFF_Q + pair * LANES:OFF_Q + (pair + 1) * LANES] = dq_pairs[pair]
                pend_ref[:, OFF_ZB + pair * LANES:OFF_ZB + (pair + 1) * LANES] = dzb_pairs[pair]
            xhat, rstd = _layer_norm_parts(proj_ref[:, OFF_V:OFF_V + D_GMLP].astype(F32))
            lng = lng_ref[...]
            vnb = (xhat * lng + lnb_ref[...]).astype(BF16)
            dvn_cols = []
            for g in range(GROUPS):
                cols = slice(g * CHUNK, (g + 1) * CHUNK)
                wm = _masked_spatial(w_ref, g)
                mixed = _dot(wm, vnb[:, cols], 1, 0) + b_ref[g]
                za = proj_ref[:, OFF_ZA + g * CHUNK:OFF_ZA + (g + 1) * CHUNK].astype(F32)
                u = proj_ref[:, OFF_U + g * CHUNK:OFF_U + (g + 1) * CHUNK].astype(F32)
                dya = dcat_ref[:, cols].astype(F32)
                sg = _sigmoid(za)
                sz = za * sg
                pend_ref[:, OFF_U + g * CHUNK:OFF_U + (g + 1) * CHUNK] = (dya * mixed * sz).astype(BF16)
                pend_ref[:, OFF_ZA + g * CHUNK:OFF_ZA + (g + 1) * CHUNK] = (
                    dya * u * mixed * (sg * (1.0 + za * (1.0 - sg)))).astype(BF16)
                dmixed = dya * u * sz
                dmb = dmixed.astype(BF16)
                dbacc_ref[g] += dmixed
                dw_ref[g] += _dot(dmb, vnb[:, cols], 1, 1)
                dvn_cols.append(_dot(wm, dmb, 0, 0))
            dvn = jnp.concatenate(dvn_cols, axis=1)
            dlng_ref[...] += jnp.sum(dvn * xhat, axis=0, keepdims=True)
            dlnb_ref[...] += jnp.sum(dvn, axis=0, keepdims=True)
            dxh = dvn * lng
            dv = rstd * (dxh - jnp.mean(dxh, axis=-1, keepdims=True)
                         - xhat * jnp.mean(dxh * xhat, axis=-1, keepdims=True))
            pend_ref[:, OFF_V:OFF_V + D_GMLP] = dv.astype(BF16)

        @pl.when(n == nb)
        def _():
            flush(jnp.zeros((CHUNK, 2 * D_KV), F32))
            t = lax.broadcasted_iota(jnp.int32, (CHUNK, CHUNK), 0)
            sidx = lax.broadcasted_iota(jnp.int32, (CHUNK, CHUNK), 1)
            lane = _lane_iota((CHUNK, LANES))
            dbt = jnp.zeros((CHUNK, LANES), F32)
            for g in range(GROUPS):
                dw_ref[g] = jnp.where(t >= sidx, dw_ref[g], 0.0)
                dbt = jnp.where(lane == g, jnp.sum(dbacc_ref[g], axis=-1, keepdims=True), dbt)
            db_ref[...] = jnp.transpose(dbt)[:GROUPS, :]

    cur = lambda n, *_: (jnp.minimum(n, nb - 1), 0)
    prev = lambda n, *_: (jnp.clip(n - 1, 0, nb - 1), 0)
    kv_block = OFF_K // (2 * D_KV)
    const2 = lambda n, *_: (0, 0)
    const3 = lambda n, *_: (0, 0, 0)
    return pl.pallas_call(
        body, name="mixer_bwd",
        grid_spec=pltpu.PrefetchScalarGridSpec(
            num_scalar_prefetch=1, grid=(nb + 1,),
            in_specs=[pl.BlockSpec((CHUNK, D_IN), cur),
                      pl.BlockSpec((CHUNK, 2 * D_KV), lambda n, *_: (jnp.clip(n - 1, 0, nb - 1), kv_block)),
                      pl.BlockSpec((CHUNK, D_GMLP + D_ATTN), cur),
                      pl.BlockSpec((CHUNK, 2 * LANES), cur),
                      pl.BlockSpec((CHUNK, 2 * LANES), prev),
                      pl.BlockSpec((1, D_GMLP), const2),
                      pl.BlockSpec((1, D_GMLP), const2),
                      pl.BlockSpec((GROUPS, CHUNK, CHUNK), const3),
                      pl.BlockSpec((GROUPS, CHUNK, CHUNK), const3)],
            out_specs=[pl.BlockSpec((CHUNK, D_IN), lambda n, *_: (jnp.maximum(n - 1, 0), 0)),
                       pl.BlockSpec((GROUPS, CHUNK, CHUNK), const3),
                       pl.BlockSpec((GROUPS, CHUNK), const2),
                       pl.BlockSpec((1, D_GMLP), const2),
                       pl.BlockSpec((1, D_GMLP), const2),
                       pl.BlockSpec((1, LANES), const2)],
            scratch_shapes=[pltpu.VMEM((CHUNK, D_IN), BF16), pltpu.VMEM((CHUNK, 2 * D_KV), F32),
                            pltpu.VMEM((GROUPS, CHUNK, CHUNK), F32)]),
        out_shape=[jax.ShapeDtypeStruct((s, D_IN), BF16),
                   jax.ShapeDtypeStruct((GROUPS, CHUNK, CHUNK), F32),
                   jax.ShapeDtypeStruct((GROUPS, CHUNK), F32),
                   jax.ShapeDtypeStruct((1, D_GMLP), F32),
                   jax.ShapeDtypeStruct((1, D_GMLP), F32),
                   jax.ShapeDtypeStruct((1, LANES), F32)],
        compiler_params=_cparams(("arbitrary",)),
    )(sinks, proj, proj, dcat, rope, rope, ln_g, ln_b, w_sp, b_sp_rows)


def _adamw_math(w, g, m, v):
    m = ADAM_B1 * m + (1.0 - ADAM_B1) * g
    v = ADAM_B2 * v + (1.0 - ADAM_B2) * (g * g)
    m_hat = m / (1.0 - ADAM_B1 ** ADAM_STEP)
    v_hat = v / (1.0 - ADAM_B2 ** ADAM_STEP)
    delta = -ADAM_LR * (m_hat / (jnp.sqrt(v_hat) + ADAM_EPS) + ADAM_WD * w)
    return delta, m, v


def adamw_shard(terms, w, m, v, name):
    r, c = w.shape
    tr = _tile(r, (224, 256, 128, 8))
    n_terms = len(terms)

    def body(*refs):
        w_ref, m_ref, v_ref, g_ref, d_ref, nm_ref, nv_ref = refs[n_terms:]
        g = None
        for ref, (_, slots) in zip(refs[:n_terms], terms):
            for k in range(slots):
                part = ref[k].astype(F32)
                g = part if g is None else g + part
        g_ref[...] = g
        d_ref[...], nm_ref[...], nv_ref[...] = _adamw_math(w_ref[...], g, m_ref[...], v_ref[...])

    spec = pl.BlockSpec((tr, c), lambda i: (i, 0))
    return pl.pallas_call(
        body, name=name, grid=(r // tr,),
        in_specs=[pl.BlockSpec((slots, tr, c), lambda i: (0, i, 0)) for _, slots in terms] + [spec] * 3,
        out_specs=[spec] * 4, out_shape=[jax.ShapeDtypeStruct((r, c), F32)] * 4,
        compiler_params=_cparams(("arbitrary",)),
    )(*[a for a, _ in terms], w, m, v)


def adamw_small(gathered, lane_windows, params):
    n_par = len(params)

    def body(*refs):
        g_refs = refs[:n_par + 1]
        wmv_refs = refs[n_par + 1:4 * n_par + 1]
        out_refs = refs[4 * n_par + 1:]

        def total(ref):
            acc = ref[0]
            for dev in range(1, N_DEV):
                acc = acc + ref[dev]
            return acc

        for i in range(n_par):
            w_ref, m_ref, v_ref = wmv_refs[3 * i:3 * i + 3]
            g = total(g_refs[i])
            if lane_windows[i] is not None:
                start, size = lane_windows[i]
                g = g[..., start:start + size]
            delta, new_m, new_v = _adamw_math(w_ref[...], g, m_ref[...], v_ref[...])
            for ref, val in zip(out_refs[4 * i:4 * i + 4], (g, delta, new_m, new_v)):
                ref[...] = val
        out_refs[4 * n_par][...] = total(g_refs[n_par])

    flat = [a for wmv in params for a in wmv]
    out_shape = [jax.ShapeDtypeStruct(w.shape, F32) for (w, _, _) in params for _ in range(4)]
    out_shape.append(jax.ShapeDtypeStruct(gathered[-1].shape[1:], F32))
    outs = pl.pallas_call(body, name="adamw_small", out_shape=out_shape, compiler_params=_cparams())(*gathered, *flat)
    return [tuple(outs[4 * i:4 * i + 4]) for i in range(n_par)], outs[-1]


def kernel(x, positions, g_pre, w_in, b_qkv, ln_v_g, ln_v_b, w_spatial, b_spatial, attn_sinks, w_out, g_post, loss_target, m_g_pre, m_w_in, m_b_qkv, m_ln_v_g, m_ln_v_b, m_w_spatial, m_b_spatial, m_attn_sinks, m_w_out, m_g_post, v_g_pre, v_w_in, v_b_qkv, v_ln_v_g, v_ln_v_b, v_w_spatial, v_b_spatial, v_attn_sinks, v_w_out, v_g_post):
    x2, target = x[0], loss_target[0]
    seq = x2.shape[0]
    xi, yi, ci = _my_place()

    wt_shard = w_in[0].T.astype(BF16)
    wo_shard = w_out[0].astype(BF16)
    pos_col = positions.reshape(seq, 1)
    half = HEAD_DIM // 2
    inv_freq = ROPE_THETA ** (-jnp.arange(half, dtype=F32) * (2.0 / HEAD_DIM))
    freq = jnp.tile(inv_freq, LANES // half).reshape(1, LANES)
    sign = jnp.tile(jnp.concatenate([-jnp.ones((half,), F32), jnp.ones((half,), F32)]), LANES // HEAD_DIM)
    sign = sign.reshape(1, LANES)
    rope, wt = rope_tables(pos_col, freq, sign, comms=[gather_comm([wt_shard])])

    bias = jnp.concatenate([jnp.zeros((1, OFF_Q), F32), b_qkv, jnp.zeros((1, D_ATTN), F32)], axis=1)
    proj, h, wo = in_proj(x2, g_pre, wt, bias, comms=[gather_comm([wo_shard])])
    b_rows = jnp.broadcast_to(b_spatial[0][:, :, None], (GROUPS, CHUNK, CHUNK))
    sinks = attn_sinks[0]
    cat = mixer_fwd(proj, rope, ln_v_g, ln_v_b, w_spatial[0], b_rows, sinks)
    dy, dout, d_g_post, loss_part = out_proj_loss(cat, wo, x2, target, g_post)

    dcat = matmul_nt(dy, wo, "out_proj_bwd")
    d_wo, _ = matmul_tn(cat, dy, 512, "w_out_grad")
    dproj, d_w_sp, d_b_sp, d_ln_g, d_ln_b, d_sinks = mixer_bwd(
        proj, dcat, rope, ln_v_g, ln_v_b, w_spatial[0], b_rows, sinks)
    d_wt, colsum, parts_wo = matmul_tn(dproj, h, 768, "w_in_grad", comms=[scatter_comm([d_wo])])

    owners = jnp.stack([4 * cx + 2 * cy + ci for cx, cy in (_chip_of(xi, yi, r) for r in range(4))]).astype(jnp.int32)
    (got_wt,) = run_comm(pair_comm([d_wt]), "grad_exchange_pair")
    sum_wt = pair_sum(d_wt, got_wt, owners, "grad_pair_sum_w_in")
    small_parts = [colsum, d_ln_g, d_ln_b, d_w_sp, d_b_sp, d_sinks, d_g_post, loss_part]
    grad_x, d_g_pre, *landed = in_proj_bwd(
        dproj, wt, x2, g_pre, dout, comms=[gather_comm(small_parts, stack=True), chips_comm([sum_wt])])
    far_wt = landed[-1]
    (all_g_pre,) = run_comm(gather_comm([d_g_pre], stack=True, direct=True), "allgather_g_pre_grad")
    gathered = [all_g_pre] + landed[:-1]
    windows = [None, (OFF_Q, D_QKV), None, None, None, None, (0, N_Q_HEADS), None]
    small = [(g_pre, m_g_pre, v_g_pre), (b_qkv, m_b_qkv, v_b_qkv), (ln_v_g, m_ln_v_g, v_ln_v_g),
             (ln_v_b, m_ln_v_b, v_ln_v_b), (w_spatial[0], m_w_spatial[0], v_w_spatial[0]),
             (b_spatial[0], m_b_spatial[0], v_b_spatial[0]), (attn_sinks, m_attn_sinks, v_attn_sinks),
             (g_post, m_g_post, v_g_post)]
    small_out, loss_row = adamw_small(gathered, windows, small)
    lead = [False, False, False, False, True, True, False, False]
    small_out = [tuple(a[None] if ld else a for a in leaf) for leaf, ld in zip(small_out, lead)]

    wt_out = adamw_shard([(sum_wt, 1), (far_wt, 3)], w_in[0].T, m_w_in[0].T, v_w_in[0].T, "adamw_w_in")
    wo_out = adamw_shard([(parts_wo, N_DEV)], w_out[0], m_w_out[0], v_w_out[0], "adamw_w_out")

    def leaves(k):
        gp, bq, lg, lb, ws, bs, sk, gpo = (leaf[k] for leaf in small_out)
        return [gp, wt_out[k].T[None], bq, lg, lb, ws, bs, sk, wo_out[k][None], gpo]

    return (loss_row[0, 0], grad_x[None], *leaves(0), *leaves(1), *leaves(2), *leaves(3))
```

```python
import functools

import jax
import jax.numpy as jnp
from jax import lax
from jax.experimental import pallas as pl
from jax.experimental.pallas import tpu as pltpu

F32 = jnp.float32
BF16 = jnp.bfloat16

D_MODEL = 2048
D_GMLP = 1024
D_ATTN = 1024
CHUNK = 128
GROUPS = 8
HEAD_DIM = 64
N_Q_HEADS = 16
N_KV_HEADS = 2
D_KV = N_KV_HEADS * HEAD_DIM
D_IN = 3 * D_GMLP + D_ATTN + 2 * D_KV + D_ATTN
OFF_U, OFF_V, OFF_ZA = 0, D_GMLP, 2 * D_GMLP
OFF_Q = 3 * D_GMLP
OFF_K = OFF_Q + D_ATTN
OFF_VA = OFF_K + D_KV
OFF_ZB = OFF_VA + D_KV
D_QKV = D_ATTN + 2 * D_KV
ROPE_THETA = 10000.0
EPS = 1e-6
SCALE = HEAD_DIM ** -0.5
NEG = -1e30
N_PAIRS = N_Q_HEADS // 2
PAIRS_PER_KV = N_PAIRS // N_KV_HEADS

ADAM_LR = 0.001
ADAM_B1 = 0.9
ADAM_B2 = 0.999
ADAM_EPS = 1e-08
ADAM_WD = 0.01
ADAM_STEP = 10

N_DEV = 8
LANES = 128
VMEM_LIMIT = 56 * 1024 * 1024

MESH = pl.DeviceIdType.MESH
ANY = pl.BlockSpec(memory_space=pl.ANY)


def _cparams(sem=None):
    return pltpu.CompilerParams(dimension_semantics=sem, vmem_limit_bytes=VMEM_LIMIT)


def _tile(n, prefs):
    for t in prefs:
        if n % t == 0:
            return t
    return n


def _sigmoid(z):
    return 1.0 / (1.0 + jnp.exp(-z))


def _dot(a, b, ca, cb):
    return lax.dot_general(a, b, (((ca,), (cb,)), ((), ())), preferred_element_type=F32)


def _my_place():
    return lax.axis_index("x"), lax.axis_index("y"), lax.axis_index("c")


def _chip_of(x, y, r):
    return (x ^ (r & 1), y ^ (r >> 1))


def _peer(x, y, c, k):
    return (x ^ (k >> 2), y ^ ((k >> 1) & 1), c ^ (k & 1))


def _index(px, py, pc):
    return 4 * px + 2 * py + pc


class _Comm:
    def __init__(self, inputs, out_shape, scratch, bind):
        self.inputs, self.out_shape, self.scratch, self.bind = list(inputs), list(out_shape), list(scratch), bind


def gather_comm(shards, stack=False, direct=False):
    n_arr = len(shards)

    def bind(ins, outs, sems):
        send_sems, recv_sems, local_sems = sems
        x, y, c = _my_place()
        me, sibling = (x, y, c), (x, y, 1 - c)
        chips = [_chip_of(x, y, r) for r in (1, 2, 3)]

        def rows(a, px, py, pc):
            d = _index(px, py, pc)
            if stack:
                return outs[a].at[d]
            m = shards[a].shape[0]
            return outs[a].at[pl.ds(pl.multiple_of(d * m, 8), m), :]

        def copy(a, k, block, to, src=None):
            return pltpu.make_async_remote_copy(
                src_ref=rows(a, *block) if src is None else src, dst_ref=rows(a, *block),
                send_sem=send_sems.at[a * 7 + k], recv_sem=recv_sems.at[a * 7 + k],
                device_id=to, device_id_type=MESH)

        def mine(a):
            return pltpu.make_async_copy(ins[a], rows(a, *me), local_sems.at[a])

        def own_sends(a):
            if direct:
                return [copy(a, k - 1, me, _peer(x, y, c, k), src=ins[a]) for k in range(1, 8)]
            return ([copy(a, 0, me, sibling, src=ins[a])]
                    + [copy(a, 1 + j, me, (*chip, c), src=ins[a]) for j, chip in enumerate(chips)])

        def start():
            for a in range(n_arr):
                mine(a).start()
                for cp in own_sends(a):
                    cp.start()

        def relay():
            if direct:
                return
            for j, chip in enumerate(chips):
                for a in range(n_arr):
                    copy(a, 1 + j, (*chip, c), me).wait_recv()
                    copy(a, 4 + j, (*chip, c), sibling).start()

        def finish():
            for a in range(n_arr):
                if direct:
                    for k in range(1, 8):
                        copy(a, k - 1, _peer(x, y, c, k), me).wait_recv()
                else:
                    copy(a, 0, sibling, me).wait_recv()
                    for j, chip in enumerate(chips):
                        copy(a, 4 + j, (*chip, 1 - c), me).wait_recv()
                        copy(a, 4 + j, (*chip, c), sibling).wait_send()
                for cp in own_sends(a):
                    cp.wait_send()
                mine(a).wait()

        return start, relay, finish

    def gathered(s):
        return (N_DEV, *s.shape) if stack else (N_DEV * s.shape[0], s.shape[1])

    return _Comm(shards, [jax.ShapeDtypeStruct(gathered(s), s.dtype) for s in shards],
                 [pltpu.SemaphoreType.DMA((7 * n_arr,)), pltpu.SemaphoreType.DMA((7 * n_arr,)),
                  pltpu.SemaphoreType.DMA((n_arr,))], bind)


def scatter_comm(parts):
    n_arr = len(parts)

    def bind(ins, outs, sems):
        send_sems, recv_sems, local_sems = sems
        x, y, c = _my_place()
        my_index = _index(x, y, c)

        def block(a, d):
            m = parts[a].shape[0] // N_DEV
            return ins[a].at[pl.ds(pl.multiple_of(d * m, 16), m), :]

        def copy(a, k, slot):
            peer = _peer(x, y, c, k)
            return pltpu.make_async_remote_copy(
                src_ref=block(a, _index(*peer)), dst_ref=outs[a].at[slot],
                send_sem=send_sems.at[a * 7 + k - 1], recv_sem=recv_sems.at[a * 7 + k - 1],
                device_id=peer, device_id_type=MESH)

        def mine(a):
            return pltpu.make_async_copy(block(a, my_index), outs[a].at[my_index], local_sems.at[a])

        def start():
            for a in range(n_arr):
                mine(a).start()
                for k in range(1, 8):
                    copy(a, k, my_index).start()

        def finish():
            for a in range(n_arr):
                for k in range(1, 8):
                    copy(a, k, _index(*_peer(x, y, c, k))).wait_recv()
                    copy(a, k, my_index).wait_send()
                mine(a).wait()

        return start, (lambda: None), finish

    return _Comm(parts, [jax.ShapeDtypeStruct((N_DEV, p.shape[0] // N_DEV, p.shape[1]), p.dtype) for p in parts],
                 [pltpu.SemaphoreType.DMA((7 * n_arr,)), pltpu.SemaphoreType.DMA((7 * n_arr,)),
                  pltpu.SemaphoreType.DMA((n_arr,))], bind)


def pair_comm(parts):
    n_arr = len(parts)

    def bind(ins, outs, sems):
        send_sems, recv_sems = sems
        x, y, c = _my_place()

        def copies():
            out = []
            for a in range(n_arr):
                m = parts[a].shape[0] // N_DEV
                for r in range(4):
                    owner = _index(*_chip_of(x, y, r), 1 - c)
                    out.append(pltpu.make_async_remote_copy(
                        src_ref=ins[a].at[pl.ds(pl.multiple_of(owner * m, 16), m), :], dst_ref=outs[a].at[r],
                        send_sem=send_sems.at[a * 4 + r], recv_sem=recv_sems.at[a * 4 + r],
                        device_id=(x, y, 1 - c), device_id_type=MESH))
            return out

        def start():
            for cp in copies():
                cp.start()

        def finish():
            for cp in copies():
                cp.wait_recv()
                cp.wait_send()

        return start, (lambda: None), finish

    return _Comm(parts, [jax.ShapeDtypeStruct((4, p.shape[0] // N_DEV, p.shape[1]), p.dtype) for p in parts],
                 [pltpu.SemaphoreType.DMA((4 * n_arr,)), pltpu.SemaphoreType.DMA((4 * n_arr,))], bind)


def chips_comm(sums):
    n_arr = len(sums)

    def bind(ins, outs, sems):
        send_sems, recv_sems = sems
        x, y, c = _my_place()

        def copies():
            return [pltpu.make_async_remote_copy(
                src_ref=ins[a].at[r], dst_ref=outs[a].at[r - 1],
                send_sem=send_sems.at[a * 3 + r - 1], recv_sem=recv_sems.at[a * 3 + r - 1],
                device_id=(*_chip_of(x, y, r), c), device_id_type=MESH) for a in range(n_arr) for r in (1, 2, 3)]

        def start():
            for cp in copies():
                cp.start()

        def finish():
            for cp in copies():
                cp.wait_recv()
                cp.wait_send()

        return start, (lambda: None), finish

    return _Comm(sums, [jax.ShapeDtypeStruct((3,) + s.shape[1:], s.dtype) for s in sums],
                 [pltpu.SemaphoreType.DMA((3 * n_arr,)), pltpu.SemaphoreType.DMA((3 * n_arr,))], bind)


def run_comm(comm, name):
    n_in, n_out = len(comm.inputs), len(comm.out_shape)

    def body(*refs):
        start, relay, finish = comm.bind(refs[:n_in], refs[n_in:n_in + n_out], refs[n_in + n_out:])
        start()
        relay()
        finish()

    outs = pl.pallas_call(body, name=name, out_shape=comm.out_shape, in_specs=[ANY] * n_in,
                          out_specs=[ANY] * n_out, scratch_shapes=comm.scratch)(*comm.inputs)
    return list(outs)


class _Hosted:
    def __init__(self, comms):
        self.comms = list(comms)
        self.inputs = [a for cm in self.comms for a in cm.inputs]
        self.out_shape = [s for cm in self.comms for s in cm.out_shape]
        self.scratch = [s for cm in self.comms for s in cm.scratch]
        self.in_specs = [ANY] * len(self.inputs)
        self.out_specs = [ANY] * len(self.out_shape)

    def split(self, refs, n_in, n_out, n_scratch):
        ni, no = len(self.inputs), len(self.out_shape)
        ins, rest = refs[:n_in], refs[n_in:]
        c_ins, rest = rest[:ni], rest[ni:]
        outs, rest = rest[:n_out], rest[n_out:]
        c_outs, rest = rest[:no], rest[no:]
        scratch, c_sems = rest[:n_scratch], rest[n_scratch:]
        phases = []
        for cm in self.comms:
            a, b, s = len(cm.inputs), len(cm.out_shape), len(cm.scratch)
            phases.append(cm.bind(c_ins[:a], c_outs[:b], c_sems[:s]))
            c_ins, c_outs, c_sems = c_ins[a:], c_outs[b:], c_sems[s:]
        return ins, outs, scratch, phases


def _before_step(phases, step, n_steps):
    if not phases:
        return

    @pl.when(step == 0)
    def _():
        for start, _, _ in phases:
            start()

    @pl.when(step == n_steps // 2)
    def _():
        for _, relay, _ in phases:
            relay()


def _after_step(phases, step, n_steps):
    if not phases:
        return

    @pl.when(step == n_steps - 1)
    def _():
        for _, _, finish in phases:
            finish()


def pair_sum(part, got, owners, name):
    m, n = got.shape[1:]

    def body(own_ref, mine_ref, got_ref, out_ref):
        del own_ref
        out_ref[...] = (mine_ref[...].astype(F32) + got_ref[...].astype(F32)).astype(out_ref.dtype)

    return pl.pallas_call(
        body, name=name,
        grid_spec=pltpu.PrefetchScalarGridSpec(
            num_scalar_prefetch=1, grid=(4,),
            in_specs=[pl.BlockSpec((m, n), lambda r, own: (own[r], 0)),
                      pl.BlockSpec((None, m, n), lambda r, own: (r, 0, 0))],
            out_specs=pl.BlockSpec((None, m, n), lambda r, own: (r, 0, 0))),
        out_shape=jax.ShapeDtypeStruct((4, m, n), got.dtype),
        compiler_params=_cparams(("arbitrary",)),
    )(owners, part, got)


def in_proj(x, g_pre, wt, bias, comms=()):
    s, d = x.shape
    tm = _tile(s, (512, 256, 128))
    tn = 768
    ni, nj = s // tm, D_IN // tn
    hosted = _Hosted(comms)

    def body(*refs):
        (x_ref, g_ref, w_ref, b_ref), (proj_ref, h_ref), _, phases = hosted.split(refs, 4, 2, 0)
        step = pl.program_id(0) * nj + pl.program_id(1)
        _before_step(phases, step, ni * nj)

        @pl.when(pl.program_id(1) == 0)
        def _():
            xv = x_ref[...]
            r = lax.rsqrt(jnp.mean(xv * xv, axis=-1, keepdims=True) + EPS)
            h_ref[...] = (xv * r * g_ref[...]).astype(BF16)

        acc = _dot(h_ref[...], w_ref[...], 1, 1)
        proj_ref[...] = (acc + b_ref[...]).astype(BF16)
        _after_step(phases, step, ni * nj)

    return pl.pallas_call(
        body, name="in_proj", grid=(ni, nj),
        in_specs=[pl.BlockSpec((tm, d), lambda i, j: (i, 0)),
                  pl.BlockSpec((1, d), lambda i, j: (0, 0)),
                  pl.BlockSpec((tn, d), lambda i, j: (j, 0)),
                  pl.BlockSpec((1, tn), lambda i, j: (0, j))] + hosted.in_specs,
        out_specs=[pl.BlockSpec((tm, tn), lambda i, j: (i, j)),
                   pl.BlockSpec((tm, d), lambda i, j: (i, 0))] + hosted.out_specs,
        out_shape=[jax.ShapeDtypeStruct((s, D_IN), BF16), jax.ShapeDtypeStruct((s, d), BF16)] + hosted.out_shape,
        scratch_shapes=hosted.scratch,
        compiler_params=_cparams(("arbitrary", "arbitrary")),
    )(x, g_pre, wt, bias, *hosted.inputs)


def out_proj_loss(cat, w_out, x, target, g_post):
    s, d = x.shape
    tm = _tile(s, (256, 128))

    def body(cat_ref, w_ref, x_ref, t_ref, g_ref, dy_ref, dout_ref, dg_ref, loss_ref):
        @pl.when(pl.program_id(0) == 0)
        def _():
            dg_ref[...] = jnp.zeros_like(dg_ref)
            loss_ref[...] = jnp.zeros_like(loss_ref)

        g = g_ref[...]
        for c0 in range(0, tm, CHUNK):
            rows = slice(c0, c0 + CHUNK)
            yv = _dot(cat_ref[rows, :], w_ref[...], 1, 0)
            r = lax.rsqrt(jnp.mean(yv * yv, axis=-1, keepdims=True) + EPS)
            nrm = yv * r
            err = x_ref[rows, :] + nrm * g - t_ref[rows, :]
            loss_ref[...] += 0.5 * jnp.sum(jnp.sum(err * err, axis=-1, keepdims=True), axis=0, keepdims=True) / d
            dout = err * (1.0 / d)
            dout_ref[rows, :] = dout
            dg_ref[...] += jnp.sum(dout * nrm, axis=0, keepdims=True)
            dn = dout * g
            dy = r * (dn - nrm * jnp.mean(dn * nrm, axis=-1, keepdims=True))
            dy_ref[rows, :] = dy.astype(BF16)

    return pl.pallas_call(
        body, name="out_proj_loss", grid=(s // tm,),
        in_specs=[pl.BlockSpec((tm, d), lambda i: (i, 0)),
                  pl.BlockSpec((d, d), lambda i: (0, 0)),
                  pl.BlockSpec((tm, d), lambda i: (i, 0)),
                  pl.BlockSpec((tm, d), lambda i: (i, 0)),
                  pl.BlockSpec((1, d), lambda i: (0, 0))],
        out_specs=[pl.BlockSpec((tm, d), lambda i: (i, 0)),
                   pl.BlockSpec((tm, d), lambda i: (i, 0)),
                   pl.BlockSpec((1, d), lambda i: (0, 0)),
                   pl.BlockSpec((1, LANES), lambda i: (0, 0))],
        out_shape=[jax.ShapeDtypeStruct((s, d), BF16), jax.ShapeDtypeStruct((s, d), F32),
                   jax.ShapeDtypeStruct((1, d), F32), jax.ShapeDtypeStruct((1, LANES), F32)],
        compiler_params=_cparams(("arbitrary",)),
    )(cat, w_out, x, target, g_post)


def matmul_nt(a, b, name):
    m, k = a.shape
    n = b.shape[0]
    tm = _tile(m, (512, 256, 128))

    def body(a_ref, b_ref, o_ref):
        o_ref[...] = _dot(a_ref[...], b_ref[...], 1, 1).astype(o_ref.dtype)

    return pl.pallas_call(
        body, name=name, grid=(m // tm,),
        in_specs=[pl.BlockSpec((tm, k), lambda i: (i, 0)), pl.BlockSpec((n, k), lambda i: (0, 0))],
        out_specs=pl.BlockSpec((tm, n), lambda i: (i, 0)),
        out_shape=jax.ShapeDtypeStruct((m, n), BF16),
        compiler_params=_cparams(("arbitrary",)),
    )(a, b)


def matmul_tn(a, b, tm, name, comms=()):
    k, m = a.shape
    n = b.shape[1]
    steps = m // tm
    hosted = _Hosted(comms)

    def body(*refs):
        (a_ref, b_hbm), (o_ref, cs_ref), (b_ref, b_sem), phases = hosted.split(refs, 2, 2, 2)
        step = pl.program_id(0)
        _before_step(phases, step, steps)

        @pl.when(step == 0)
        def _():
            load = pltpu.make_async_copy(b_hbm, b_ref, b_sem)
            load.start()
            load.wait()

        o_ref[...] = _dot(a_ref[...], b_ref[...], 0, 0).astype(o_ref.dtype)
        rows = _tile(k, (512, 128))
        cs = jnp.zeros((1, tm), F32)
        for r0 in range(0, k, rows):
            cs = cs + jnp.sum(a_ref[r0:r0 + rows, :].astype(F32), axis=0, keepdims=True)
        cs_ref[...] = cs
        _after_step(phases, step, steps)

    return pl.pallas_call(
        body, name=name, grid=(steps,),
        in_specs=[pl.BlockSpec((k, tm), lambda i: (0, i)), ANY] + hosted.in_specs,
        out_specs=[pl.BlockSpec((tm, n), lambda i: (i, 0)), pl.BlockSpec((1, tm), lambda i: (0, i))] + hosted.out_specs,
        out_shape=[jax.ShapeDtypeStruct((m, n), BF16), jax.ShapeDtypeStruct((1, m), F32)] + hosted.out_shape,
        scratch_shapes=[pltpu.VMEM((k, n), b.dtype), pltpu.SemaphoreType.DMA] + hosted.scratch,
        compiler_params=_cparams(("arbitrary",)),
    )(a, b, *hosted.inputs)


def in_proj_bwd(dproj, wt, x, g_pre, dout, comms=()):
    s, d = x.shape
    tm = _tile(s, (256, 128))
    steps = s // tm
    hosted = _Hosted(comms)

    def body(*refs):
        (dp_ref, w_hbm, x_ref, g_ref, dout_ref), (gx_ref, dg_ref), (w_ref, w_sem), phases = hosted.split(refs, 5, 2, 2)
        step = pl.program_id(0)
        _before_step(phases, step, steps)

        @pl.when(step == 0)
        def _():
            dg_ref[...] = jnp.zeros_like(dg_ref)
            load = pltpu.make_async_copy(w_hbm, w_ref, w_sem)
            load.start()
            load.wait()

        for c0 in range(0, tm, CHUNK):
            rows = slice(c0, c0 + CHUNK)
            dh = _dot(dp_ref[rows, :], w_ref[...], 1, 0)
            xv = x_ref[rows, :]
            r = lax.rsqrt(jnp.mean(xv * xv, axis=-1, keepdims=True) + EPS)
            xn = xv * r
            dg_ref[...] += jnp.sum(dh * xn, axis=0, keepdims=True)
            dn = dh * g_ref[...]
            gx_ref[rows, :] = dout_ref[rows, :] + r * (dn - xn * jnp.mean(dn * xn, axis=-1, keepdims=True))

        _after_step(phases, step, steps)

    return pl.pallas_call(
        body, name="in_proj_bwd", grid=(steps,),
        in_specs=[pl.BlockSpec((tm, D_IN), lambda i: (i, 0)),
                  ANY,
                  pl.BlockSpec((tm, d), lambda i: (i, 0)),
                  pl.BlockSpec((1, d), lambda i: (0, 0)),
                  pl.BlockSpec((tm, d), lambda i: (i, 0))] + hosted.in_specs,
        out_specs=[pl.BlockSpec((tm, d), lambda i: (i, 0)),
                   pl.BlockSpec((1, d), lambda i: (0, 0))] + hosted.out_specs,
        out_shape=[jax.ShapeDtypeStruct((s, d), F32), jax.ShapeDtypeStruct((1, d), F32)] + hosted.out_shape,
        scratch_shapes=[pltpu.VMEM((D_IN, d), BF16), pltpu.SemaphoreType.DMA] + hosted.scratch,
        compiler_params=_cparams(("arbitrary",)),
    )(dproj, wt, x, g_pre, dout, *hosted.inputs)


def _lane_iota(shape):
    return lax.broadcasted_iota(jnp.int32, shape, len(shape) - 1)


def rope_tables(pos_col, freq, sign, comms=()):
    s = pos_col.shape[0]
    tr = _tile(s, (512, 256, 128))
    hosted = _Hosted(comms)

    def body(*refs):
        (pos_ref, freq_ref, sign_ref), (out_ref,), _, phases = hosted.split(refs, 3, 1, 0)
        _before_step(phases, pl.program_id(0), s // tr)
        ang = pos_ref[...].astype(F32) * freq_ref[...]
        out_ref[:, :LANES] = jnp.cos(ang)
        out_ref[:, LANES:] = jnp.sin(ang) * sign_ref[...]
        _after_step(phases, pl.program_id(0), s // tr)

    return pl.pallas_call(
        body, name="rope_tables", grid=(s // tr,),
        in_specs=[pl.BlockSpec((tr, 1), lambda i: (i, 0)), pl.BlockSpec((1, LANES), lambda i: (0, 0)),
                  pl.BlockSpec((1, LANES), lambda i: (0, 0))] + hosted.in_specs,
        out_specs=[pl.BlockSpec((tr, 2 * LANES), lambda i: (i, 0))] + hosted.out_specs,
        out_shape=[jax.ShapeDtypeStruct((s, 2 * LANES), F32)] + hosted.out_shape,
        scratch_shapes=hosted.scratch,
        compiler_params=_cparams(("arbitrary",)),
    )(pos_col, freq, sign, *hosted.inputs)


def _partner(v):
    low = (_lane_iota(v.shape) % HEAD_DIM) < (HEAD_DIM // 2)
    return jnp.where(low, pltpu.roll(v, LANES - HEAD_DIM // 2, 1), pltpu.roll(v, HEAD_DIM // 2, 1))


def _rope(v, cos, sin_signed):
    return v * cos + _partner(v) * sin_signed


def _rope_transposed(dv, cos, sin_signed):
    return dv * cos - _partner(dv) * sin_signed


def _both_halves(v, kv_head):
    keep = (_lane_iota(v.shape) >= HEAD_DIM) if kv_head else (_lane_iota(v.shape) < HEAD_DIM)
    return jnp.where(keep, v, pltpu.roll(v, HEAD_DIM, 1))


def _fold_halves(acc):
    return acc + pltpu.roll(acc, HEAD_DIM, 1)


def _by_half(a, b):
    shape = jnp.broadcast_shapes(jnp.shape(a), jnp.shape(b))
    return jnp.where(_lane_iota(shape) < HEAD_DIM, a, b)


def _stack_heads(pair):
    return jnp.concatenate([_by_half(pair, 0.0), _by_half(0.0, pair)], axis=0)


def _band_bias(has_prev):
    i = lax.broadcasted_iota(jnp.int32, (2 * CHUNK, 2 * CHUNK), 0) % CHUNK
    j = lax.broadcasted_iota(jnp.int32, (2 * CHUNK, 2 * CHUNK), 1)
    band = jnp.logical_and(j > i, j <= i + CHUNK)
    return jnp.where(jnp.logical_and(band, jnp.logical_or(j >= CHUNK, has_prev)), 0.0, NEG)


def _probs(qm2, kk2, bias, sink_col):
    sc = _dot(qm2, kk2, 1, 1) + bias
    mx = jnp.maximum(jnp.max(sc, axis=-1, keepdims=True), sink_col)
    p = jnp.exp(sc - mx)
    es = jnp.exp(sink_col - mx)
    inv = 1.0 / (jnp.sum(p, axis=-1, keepdims=True) + es)
    return p * inv, es * inv


def _sink_col(sinks_ref, pair):
    row = lax.broadcasted_iota(jnp.int32, (2 * CHUNK, 1), 0)
    return jnp.where(row < CHUNK, sinks_ref[2 * pair], sinks_ref[2 * pair + 1])


def _layer_norm_parts(v):
    mu = jnp.mean(v, axis=-1, keepdims=True)
    xc = v - mu
    rstd = lax.rsqrt(jnp.mean(xc * xc, axis=-1, keepdims=True) + EPS)
    return xc * rstd, rstd


def _masked_spatial(w_ref, g):
    t = lax.broadcasted_iota(jnp.int32, (CHUNK, CHUNK), 0)
    sidx = lax.broadcasted_iota(jnp.int32, (CHUNK, CHUNK), 1)
    return jnp.where(t >= sidx, w_ref[g], 0.0).astype(BF16)


def _keys_values(kv_ref, kvp_ref, rope_ref, ropep_ref):
    cos_c, sin_c = rope_ref[:, :LANES], rope_ref[:, LANES:]
    cos_p, sin_p = ropep_ref[:, :LANES], ropep_ref[:, LANES:]
    k_c = _rope(kv_ref[:, :D_KV].astype(F32), cos_c, sin_c)
    k_p = _rope(kvp_ref[:, :D_KV].astype(F32), cos_p, sin_p)
    keys = jnp.concatenate([k_p, k_c], axis=0)
    vals = jnp.concatenate([kvp_ref[:, D_KV:], kv_ref[:, D_KV:]], axis=0).astype(F32)
    return keys, vals, (cos_c, sin_c, cos_p, sin_p)


def mixer_fwd(proj, rope, ln_g, ln_b, w_sp, b_sp_rows, sinks):
    s = proj.shape[0]
    nb = s // CHUNK

    def body(sinks_ref, proj_ref, kvp_ref, rope_ref, ropep_ref, lng_ref, lnb_ref, w_ref, b_ref, cat_ref):
        n = pl.program_id(0)
        xhat, _ = _layer_norm_parts(proj_ref[:, OFF_V:OFF_V + D_GMLP].astype(F32))
        vnb = (xhat * lng_ref[...] + lnb_ref[...]).astype(BF16)
        for g in range(GROUPS):
            cols = slice(g * CHUNK, (g + 1) * CHUNK)
            mixed = _dot(_masked_spatial(w_ref, g), vnb[:, cols], 1, 0) + b_ref[g]
            za = proj_ref[:, OFF_ZA + g * CHUNK:OFF_ZA + (g + 1) * CHUNK].astype(F32)
            u = proj_ref[:, OFF_U + g * CHUNK:OFF_U + (g + 1) * CHUNK].astype(F32)
            cat_ref[:, cols] = (u * mixed * (za * _sigmoid(za))).astype(BF16)
        kv_ref = proj_ref.at[:, OFF_K:OFF_K + 2 * D_KV]
        keys, vals, (cos_c, sin_c, _, _) = _keys_values(kv_ref, kvp_ref, rope_ref, ropep_ref)
        cos_q, sin_q = cos_c * SCALE, sin_c * SCALE
        bias = _band_bias(n > 0)
        for pair in range(N_PAIRS):
            kvh = pair // PAIRS_PER_KV
            kk2 = _both_halves(keys, kvh).astype(BF16)
            vv2 = _both_halves(vals, kvh).astype(BF16)
            qcols = slice(OFF_Q + pair * LANES, OFF_Q + (pair + 1) * LANES)
            q_pair = _rope(proj_ref[:, qcols].astype(F32), cos_q, sin_q)
            p, _ = _probs(_stack_heads(q_pair).astype(BF16), kk2, bias, _sink_col(sinks_ref, pair))
            o2 = _dot(p.astype(BF16), vv2, 1, 0)
            out_pair = _by_half(o2[:CHUNK], o2[CHUNK:])
            zb = proj_ref[:, OFF_ZB + pair * LANES:OFF_ZB + (pair + 1) * LANES].astype(F32)
            cat_ref[:, D_GMLP + pair * LANES:D_GMLP + (pair + 1) * LANES] = (
                out_pair * (zb * _sigmoid(zb))).astype(BF16)

    prev = lambda n, *_: (jnp.maximum(n - 1, 0), 0)
    kv_block = OFF_K // (2 * D_KV)
    return pl.pallas_call(
        body, name="mixer_fwd",
        grid_spec=pltpu.PrefetchScalarGridSpec(
            num_scalar_prefetch=1, grid=(nb,),
            in_specs=[pl.BlockSpec((CHUNK, D_IN), lambda n, *_: (n, 0)),
                      pl.BlockSpec((CHUNK, 2 * D_KV), lambda n, *_: (jnp.maximum(n - 1, 0), kv_block)),
                      pl.BlockSpec((CHUNK, 2 * LANES), lambda n, *_: (n, 0)),
                      pl.BlockSpec((CHUNK, 2 * LANES), prev),
                      pl.BlockSpec((1, D_GMLP), lambda n, *_: (0, 0)),
                      pl.BlockSpec((1, D_GMLP), lambda n, *_: (0, 0)),
                      pl.BlockSpec((GROUPS, CHUNK, CHUNK), lambda n, *_: (0, 0, 0)),
                      pl.BlockSpec((GROUPS, CHUNK, CHUNK), lambda n, *_: (0, 0, 0))],
            out_specs=pl.BlockSpec((CHUNK, D_GMLP + D_ATTN), lambda n, *_: (n, 0))),
        out_shape=jax.ShapeDtypeStruct((s, D_GMLP + D_ATTN), BF16),
        compiler_params=_cparams(("arbitrary",)),
    )(sinks, proj, proj, rope, rope, ln_g, ln_b, w_sp, b_sp_rows)


def mixer_bwd(proj, dcat, rope, ln_g, ln_b, w_sp, b_sp_rows, sinks):
    s = proj.shape[0]
    nb = s // CHUNK

    def body(sinks_ref, proj_ref, kvp_ref, dcat_ref, rope_ref, ropep_ref, lng_ref, lnb_ref, w_ref, b_ref,
             dproj_ref, dw_ref, db_ref, dlng_ref, dlnb_ref, dsink_ref,
             pend_ref, pend_kv_ref, dbacc_ref):
        n = pl.program_id(0)

        @pl.when(n == 0)
        def _():
            dw_ref[...] = jnp.zeros_like(dw_ref)
            dbacc_ref[...] = jnp.zeros_like(dbacc_ref)
            dlng_ref[...] = jnp.zeros_like(dlng_ref)
            dlnb_ref[...] = jnp.zeros_like(dlnb_ref)
            dsink_ref[...] = jnp.zeros_like(dsink_ref)

        def flush(dkv_prev):
            @pl.when(n > 0)
            def _():
                dproj_ref[...] = pend_ref[...]
                dproj_ref[:, OFF_K:OFF_K + 2 * D_KV] = (pend_kv_ref[...] + dkv_prev).astype(BF16)

        @pl.when(n < nb)
        def _():
            kv_ref = proj_ref.at[:, OFF_K:OFF_K + 2 * D_KV]
            keys, vals, (cos_c, sin_c, cos_p, sin_p) = _keys_values(kv_ref, kvp_ref, rope_ref, ropep_ref)
            cos_q, sin_q = cos_c * SCALE, sin_c * SCALE
            bias = _band_bias(n > 0)
            lane_row = _lane_iota((1, LANES))
            dsink = jnp.zeros((1, LANES), F32)
            dk_heads, dv_heads = [], []
            dq_pairs, dzb_pairs = [], []
            for kvh in range(N_KV_HEADS):
                kk2 = _both_halves(keys, kvh).astype(BF16)
                vv2 = _both_halves(vals, kvh).astype(BF16)
                dkk = jnp.zeros((2 * CHUNK, LANES), F32)
                dvv = jnp.zeros((2 * CHUNK, LANES), F32)
                for pair in range(kvh * PAIRS_PER_KV, (kvh + 1) * PAIRS_PER_KV):
                    qcols = slice(OFF_Q + pair * LANES, OFF_Q + (pair + 1) * LANES)
                    q_pair = _rope(proj_ref[:, qcols].astype(F32), cos_q, sin_q)
                    qm2 = _stack_heads(q_pair).astype(BF16)
                    p, p_sink = _probs(qm2, kk2, bias, _sink_col(sinks_ref, pair))
                    pb = p.astype(BF16)
                    o2 = _dot(pb, vv2, 1, 0)
                    out_pair = _by_half(o2[:CHUNK], o2[CHUNK:])
                    zb = proj_ref[:, OFF_ZB + pair * LANES:OFF_ZB + (pair + 1) * LANES].astype(F32)
                    sg = _sigmoid(zb)
                    dyb = dcat_ref[:, D_GMLP + pair * LANES:D_GMLP + (pair + 1) * LANES].astype(F32)
                    dzb_pairs.append((dyb * out_pair * (sg * (1.0 + zb * (1.0 - sg)))).astype(BF16))
                    dom2 = _stack_heads(dyb * (zb * sg)).astype(BF16)
                    dp = _dot(dom2, vv2, 1, 1)
                    delta = jnp.sum(p * dp, axis=-1, keepdims=True)
                    ds = p * (dp - delta)
                    dsk = -(p_sink * delta)
                    dsink = dsink + jnp.where(lane_row == 2 * pair,
                                              jnp.sum(dsk[:CHUNK], axis=0, keepdims=True), 0.0)
                    dsink = dsink + jnp.where(lane_row == 2 * pair + 1,
                                              jnp.sum(dsk[CHUNK:], axis=0, keepdims=True), 0.0)
                    dsb = ds.astype(BF16)
                    dq2 = _dot(dsb, kk2, 1, 0)
                    dq_pairs.append(_rope_transposed(_by_half(dq2[:CHUNK], dq2[CHUNK:]), cos_q, sin_q).astype(BF16))
                    dkk = dkk + _dot(dsb, qm2, 0, 0)
                    dvv = dvv + _dot(pb, dom2, 0, 0)
                dk_heads.append(_fold_halves(dkk))
                dv_heads.append(_fold_halves(dvv))
            dk_rot = _by_half(dk_heads[0], dk_heads[1])
            dv_all = _by_half(dv_heads[0], dv_heads[1])
            dk_p = _rope_transposed(dk_rot[:CHUNK], cos_p, sin_p)
            dk_c = _rope_transposed(dk_rot[CHUNK:], cos_c, sin_c)
            flush(jnp.concatenate([dk_p, dv_all[:CHUNK]], axis=1))
            dsink_ref[...] += dsink
            pend_kv_ref[...] = jnp.concatenate([dk_c, dv_all[CHUNK:]], axis=1)
            for pair in range(N_PAIRS):
                pend_ref[:, OFF_Q + pair * LANES:OFF_Q + (pair + 1) * LANES] = dq_pairs[pair]
                pend_ref[:, OFF_ZB + pair * LANES:OFF_ZB + (pair + 1) * LANES] = dzb_pairs[pair]
            xhat, rstd = _layer_norm_parts(proj_ref[:, OFF_V:OFF_V + D_GMLP].astype(F32))
            lng = lng_ref[...]
            vnb = (xhat * lng + lnb_ref[...]).astype(BF16)
            dvn_cols = []
            for g in range(GROUPS):
                cols = slice(g * CHUNK, (g + 1) * CHUNK)
                wm = _masked_spatial(w_ref, g)
                mixed = _dot(wm, vnb[:, cols], 1, 0) + b_ref[g]
                za = proj_ref[:, OFF_ZA + g * CHUNK:OFF_ZA + (g + 1) * CHUNK].astype(F32)
                u = proj_ref[:, OFF_U + g * CHUNK:OFF_U + (g + 1) * CHUNK].astype(F32)
                dya = dcat_ref[:, cols].astype(F32)
                sg = _sigmoid(za)
                sz = za * sg
                pend_ref[:, OFF_U + g * CHUNK:OFF_U + (g + 1) * CHUNK] = (dya * mixed * sz).astype(BF16)
                pend_ref[:, OFF_ZA + g * CHUNK:OFF_ZA + (g + 1) * CHUNK] = (
                    dya * u * mixed * (sg * (1.0 + za * (1.0 - sg)))).astype(BF16)
                dmixed = dya * u * sz
                dmb = dmixed.astype(BF16)
                dbacc_ref[g] += dmixed
                dw_ref[g] += _dot(dmb, vnb[:, cols], 1, 1)
                dvn_cols.append(_dot(wm, dmb, 0, 0))
            dvn = jnp.concatenate(dvn_cols, axis=1)
            dlng_ref[...] += jnp.sum(dvn * xhat, axis=0, keepdims=True)
            dlnb_ref[...] += jnp.sum(dvn, axis=0, keepdims=True)
            dxh = dvn * lng
            dv = rstd * (dxh - jnp.mean(dxh, axis=-1, keepdims=True)
                         - xhat * jnp.mean(dxh * xhat, axis=-1, keepdims=True))
            pend_ref[:, OFF_V:OFF_V + D_GMLP] = dv.astype(BF16)

        @pl.when(n == nb)
        def _():
            flush(jnp.zeros((CHUNK, 2 * D_KV), F32))
            t = lax.broadcasted_iota(jnp.int32, (CHUNK, CHUNK), 0)
            sidx = lax.broadcasted_iota(jnp.int32, (CHUNK, CHUNK), 1)
            lane = _lane_iota((CHUNK, LANES))
            dbt = jnp.zeros((CHUNK, LANES), F32)
            for g in range(GROUPS):
                dw_ref[g] = jnp.where(t >= sidx, dw_ref[g], 0.0)
                dbt = jnp.where(lane == g, jnp.sum(dbacc_ref[g], axis=-1, keepdims=True), dbt)
            db_ref[...] = jnp.transpose(dbt)[:GROUPS, :]

    cur = lambda n, *_: (jnp.minimum(n, nb - 1), 0)
    prev = lambda n, *_: (jnp.clip(n - 1, 0, nb - 1), 0)
    kv_block = OFF_K // (2 * D_KV)
    const2 = lambda n, *_: (0, 0)
    const3 = lambda n, *_: (0, 0, 0)
    return pl.pallas_call(
        body, name="mixer_bwd",
        grid_spec=pltpu.PrefetchScalarGridSpec(
            num_scalar_prefetch=1, grid=(nb + 1,),
            in_specs=[pl.BlockSpec((CHUNK, D_IN), cur),
                      pl.BlockSpec((CHUNK, 2 * D_KV), lambda n, *_: (jnp.clip(n - 1, 0, nb - 1), kv_block)),
                      pl.BlockSpec((CHUNK, D_GMLP + D_ATTN), cur),
                      pl.BlockSpec((CHUNK, 2 * LANES), cur),
                      pl.BlockSpec((CHUNK, 2 * LANES), prev),
                      pl.BlockSpec((1, D_GMLP), const2),
                      pl.BlockSpec((1, D_GMLP), const2),
                      pl.BlockSpec((GROUPS, CHUNK, CHUNK), const3),
                      pl.BlockSpec((GROUPS, CHUNK, CHUNK), const3)],
            out_specs=[pl.BlockSpec((CHUNK, D_IN), lambda n, *_: (jnp.maximum(n - 1, 0), 0)),
                       pl.BlockSpec((GROUPS, CHUNK, CHUNK), const3),
                       pl.BlockSpec((GROUPS, CHUNK), const2),
                       pl.BlockSpec((1, D_GMLP), const2),
                       pl.BlockSpec((1, D_GMLP), const2),
                       pl.BlockSpec((1, LANES), const2)],
            scratch_shapes=[pltpu.VMEM((CHUNK, D_IN), BF16), pltpu.VMEM((CHUNK, 2 * D_KV), F32),
                            pltpu.VMEM((GROUPS, CHUNK, CHUNK), F32)]),
        out_shape=[jax.ShapeDtypeStruct((s, D_IN), BF16),
                   jax.ShapeDtypeStruct((GROUPS, CHUNK, CHUNK), F32),
                   jax.ShapeDtypeStruct((GROUPS, CHUNK), F32),
                   jax.ShapeDtypeStruct((1, D_GMLP), F32),
                   jax.ShapeDtypeStruct((1, D_GMLP), F32),
                   jax.ShapeDtypeStruct((1, LANES), F32)],
        compiler_params=_cparams(("arbitrary",)),
    )(sinks, proj, proj, dcat, rope, rope, ln_g, ln_b, w_sp, b_sp_rows)


def _adamw_math(w, g, m, v):
    m = ADAM_B1 * m + (1.0 - ADAM_B1) * g
    v = ADAM_B2 * v + (1.0 - ADAM_B2) * (g * g)
    m_hat = m / (1.0 - ADAM_B1 ** ADAM_STEP)
    v_hat = v / (1.0 - ADAM_B2 ** ADAM_STEP)
    delta = -ADAM_LR * (m_hat / (jnp.sqrt(v_hat) + ADAM_EPS) + ADAM_WD * w)
    return delta, m, v


def adamw_shard(terms, w, m, v, name):
    r, c = w.shape
    tr = _tile(r, (224, 256, 128, 8))
    n_terms = len(terms)

    def body(*refs):
        w_ref, m_ref, v_ref, g_ref, d_ref, nm_ref, nv_ref = refs[n_terms:]
        g = None
        for ref, (_, slots) in zip(refs[:n_terms], terms):
            for k in range(slots):
                part = ref[k].astype(F32)
                g = part if g is None else g + part
        g_ref[...] = g
        d_ref[...], nm_ref[...], nv_ref[...] = _adamw_math(w_ref[...], g, m_ref[...], v_ref[...])

    spec = pl.BlockSpec((tr, c), lambda i: (i, 0))
    return pl.pallas_call(
        body, name=name, grid=(r // tr,),
        in_specs=[pl.BlockSpec((slots, tr, c), lambda i: (0, i, 0)) for _, slots in terms] + [spec] * 3,
        out_specs=[spec] * 4, out_shape=[jax.ShapeDtypeStruct((r, c), F32)] * 4,
        compiler_params=_cparams(("arbitrary",)),
    )(*[a for a, _ in terms], w, m, v)


def adamw_small(gathered, lane_windows, params):
    n_par = len(params)

    def body(*refs):
        g_refs = refs[:n_par + 1]
        wmv_refs = refs[n_par + 1:4 * n_par + 1]
        out_refs = refs[4 * n_par + 1:]

        def total(ref):
            acc = ref[0]
            for dev in range(1, N_DEV):
                acc = acc + ref[dev]
            return acc

        for i in range(n_par):
            w_ref, m_ref, v_ref = wmv_refs[3 * i:3 * i + 3]
            g = total(g_refs[i])
            if lane_windows[i] is not None:
                start, size = lane_windows[i]
                g = g[..., start:start + size]
            delta, new_m, new_v = _adamw_math(w_ref[...], g, m_ref[...], v_ref[...])
            for ref, val in zip(out_refs[4 * i:4 * i + 4], (g, delta, new_m, new_v)):
                ref[...] = val
        out_refs[4 * n_par][...] = total(g_refs[n_par])

    flat = [a for wmv in params for a in wmv]
    out_shape = [jax.ShapeDtypeStruct(w.shape, F32) for (w, _, _) in params for _ in range(4)]
    out_shape.append(jax.ShapeDtypeStruct(gathered[-1].shape[1:], F32))
    outs = pl.pallas_call(body, name="adamw_small", out_shape=out_shape, compiler_params=_cparams())(*gathered, *flat)
    return [tuple(outs[4 * i:4 * i + 4]) for i in range(n_par)], outs[-1]


def kernel(x, positions, g_pre, w_in, b_qkv, ln_v_g, ln_v_b, w_spatial, b_spatial, attn_sinks, w_out, g_post, loss_target, m_g_pre, m_w_in, m_b_qkv, m_ln_v_g, m_ln_v_b, m_w_spatial, m_b_spatial, m_attn_sinks, m_w_out, m_g_post, v_g_pre, v_w_in, v_b_qkv, v_ln_v_g, v_ln_v_b, v_w_spatial, v_b_spatial, v_attn_sinks, v_w_out, v_g_post):
    x2, target = x[0], loss_target[0]
    seq = x2.shape[0]
    xi, yi, ci = _my_place()

    wt_shard = w_in[0].T.astype(BF16)
    wo_shard = w_out[0].astype(BF16)
    pos_col = positions.reshape(seq, 1)
    half = HEAD_DIM // 2
    inv_freq = ROPE_THETA ** (-jnp.arange(half, dtype=F32) * (2.0 / HEAD_DIM))
    freq = jnp.tile(inv_freq, LANES // half).reshape(1, LANES)
    sign = jnp.tile(jnp.concatenate([-jnp.ones((half,), F32), jnp.ones((half,), F32)]), LANES // HEAD_DIM)
    sign = sign.reshape(1, LANES)
    rope, wt = rope_tables(pos_col, freq, sign, comms=[gather_comm([wt_shard])])

    bias = jnp.concatenate([jnp.zeros((1, OFF_Q), F32), b_qkv, jnp.zeros((1, D_ATTN), F32)], axis=1)
    proj, h, wo = in_proj(x2, g_pre, wt, bias, comms=[gather_comm([wo_shard])])
    b_rows = jnp.broadcast_to(b_spatial[0][:, :, None], (GROUPS, CHUNK, CHUNK))
    sinks = attn_sinks[0]
    cat = mixer_fwd(proj, rope, ln_v_g, ln_v_b, w_spatial[0], b_rows, sinks)
    dy, dout, d_g_post, loss_part = out_proj_loss(cat, wo, x2, target, g_post)

    dcat = matmul_nt(dy, wo, "out_proj_bwd")
    d_wo, _ = matmul_tn(cat, dy, 512, "w_out_grad")
    dproj, d_w_sp, d_b_sp, d_ln_g, d_ln_b, d_sinks = mixer_bwd(
        proj, dcat, rope, ln_v_g, ln_v_b, w_spatial[0], b_rows, sinks)
    d_wt, colsum, parts_wo = matmul_tn(dproj, h, 768, "w_in_grad", comms=[scatter_comm([d_wo])])

    owners = jnp.stack([4 * cx + 2 * cy + ci for cx, cy in (_chip_of(xi, yi, r) for r in range(4))]).astype(jnp.int32)
    (got_wt,) = run_comm(pair_comm([d_wt]), "grad_exchange_pair")
    sum_wt = pair_sum(d_wt, got_wt, owners, "grad_pair_sum_w_in")
    small_parts = [colsum, d_ln_g, d_ln_b, d_w_sp, d_b_sp, d_sinks, d_g_post, loss_part]
    grad_x, d_g_pre, *landed = in_proj_bwd(
        dproj, wt, x2, g_pre, dout, comms=[gather_comm(small_parts, stack=True), chips_comm([sum_wt])])
    far_wt = landed[-1]
    (all_g_pre,) = run_comm(gather_comm([d_g_pre], stack=True, direct=True), "allgather_g_pre_grad")
    gathered = [all_g_pre] + landed[:-1]
    windows = [None, (OFF_Q, D_QKV), None, None, None, None, (0, N_Q_HEADS), None]
    small = [(g_pre, m_g_pre, v_g_pre), (b_qkv, m_b_qkv, v_b_qkv), (ln_v_g, m_ln_v_g, v_ln_v_g),
             (ln_v_b, m_ln_v_b, v_ln_v_b), (w_spatial[0], m_w_spatial[0], v_w_spatial[0]),
             (b_spatial[0], m_b_spatial[0], v_b_spatial[0]), (attn_sinks, m_attn_sinks, v_attn_sinks),
             (g_post, m_g_post, v_g_post)]
    small_out, loss_row = adamw_small(gathered, windows, small)
    lead = [False, False, False, False, True, True, False, False]
    small_out = [tuple(a[None] if ld else a for a in leaf) for leaf, ld in zip(small_out, lead)]

    wt_out = adamw_shard([(sum_wt, 1), (far_wt, 3)], w_in[0].T, m_w_in[0].T, v_w_in[0].T, "adamw_w_in")
    wo_out = adamw_shard([(parts_wo, N_DEV)], w_out[0], m_w_out[0], v_w_out[0], "adamw_w_out")

    def leaves(k):
        gp, bq, lg, lb, ws, bs, sk, gpo = (leaf[k] for leaf in small_out)
        return [gp, wt_out[k].T[None], bq, lg, lb, ws, bs, sk, wo_out[k][None], gpo]

    return (loss_row[0, 0], grad_x[None], *leaves(0), *leaves(1), *leaves(2), *leaves(3))
```

```python
import functools

import jax
import jax.numpy as jnp
from jax import lax
from jax.experimental import pallas as pl
from jax.experimental.pallas import tpu as pltpu

F32 = jnp.float32
BF16 = jnp.bfloat16

D_MODEL = 2048
D_GMLP = 1024
D_ATTN = 1024
CHUNK = 128
GROUPS = 8
HEAD_DIM = 64
N_Q_HEADS = 16
N_KV_HEADS = 2
D_KV = N_KV_HEADS * HEAD_DIM
D_IN = 3 * D_GMLP + D_ATTN + 2 * D_KV + D_ATTN
OFF_U, OFF_V, OFF_ZA = 0, D_GMLP, 2 * D_GMLP
OFF_Q = 3 * D_GMLP
OFF_K = OFF_Q + D_ATTN
OFF_VA = OFF_K + D_KV
OFF_ZB = OFF_VA + D_KV
D_QKV = D_ATTN + 2 * D_KV
ROPE_THETA = 10000.0
EPS = 1e-6
SCALE = HEAD_DIM ** -0.5
NEG = -1e30
N_PAIRS = N_Q_HEADS // 2
PAIRS_PER_KV = N_PAIRS // N_KV_HEADS

ADAM_LR = 0.001
ADAM_B1 = 0.9
ADAM_B2 = 0.999
ADAM_EPS = 1e-08
ADAM_WD = 0.01
ADAM_STEP = 10

N_DEV = 8
LANES = 128
VMEM_LIMIT = 56 * 1024 * 1024

MESH = pl.DeviceIdType.MESH
ANY = pl.BlockSpec(memory_space=pl.ANY)


def _cparams(sem=None):
    return pltpu.CompilerParams(dimension_semantics=sem, vmem_limit_bytes=VMEM_LIMIT)


def _tile(n, prefs):
    for t in prefs:
        if n % t == 0:
            return t
    return n


def _sigmoid(z):
    return 1.0 / (1.0 + jnp.exp(-z))


def _dot(a, b, ca, cb):
    return lax.dot_general(a, b, (((ca,), (cb,)), ((), ())), preferred_element_type=F32)


def _my_place():
    return lax.axis_index("x"), lax.axis_index("y"), lax.axis_index("c")


def _chip_of(x, y, r):
    return (x ^ (r & 1), y ^ (r >> 1))


def _peer(x, y, c, k):
    return (x ^ (k >> 2), y ^ ((k >> 1) & 1), c ^ (k & 1))


def _index(px, py, pc):
    return 4 * px + 2 * py + pc


class _Comm:
    def __init__(self, inputs, out_shape, scratch, bind):
        self.inputs, self.out_shape, self.scratch, self.bind = list(inputs), list(out_shape), list(scratch), bind


def gather_comm(shards, stack=False, direct=False):
    n_arr = len(shards)

    def bind(ins, outs, sems):
        send_sems, recv_sems, local_sems = sems
        x, y, c = _my_place()
        me, sibling = (x, y, c), (x, y, 1 - c)
        chips = [_chip_of(x, y, r) for r in (1, 2, 3)]

        def rows(a, px, py, pc):
            d = _index(px, py, pc)
            if stack:
                return outs[a].at[d]
            m = shards[a].shape[0]
            return outs[a].at[pl.ds(pl.multiple_of(d * m, 8), m), :]

        def copy(a, k, block, to, src=None):
            return pltpu.make_async_remote_copy(
                src_ref=rows(a, *block) if src is None else src, dst_ref=rows(a, *block),
                send_sem=send_sems.at[a * 7 + k], recv_sem=recv_sems.at[a * 7 + k],
                device_id=to, device_id_type=MESH)

        def mine(a):
            return pltpu.make_async_copy(ins[a], rows(a, *me), local_sems.at[a])

        def own_sends(a):
            if direct:
                return [copy(a, k - 1, me, _peer(x, y, c, k), src=ins[a]) for k in range(1, 8)]
            return ([copy(a, 0, me, sibling, src=ins[a])]
                    + [copy(a, 1 + j, me, (*chip, c), src=ins[a]) for j, chip in enumerate(chips)])

        def start():
            for a in range(n_arr):
                mine(a).start()
                for cp in own_sends(a):
                    cp.start()

        def relay():
            if direct:
                return
            for j, chip in enumerate(chips):
                for a in range(n_arr):
                    copy(a, 1 + j, (*chip, c), me).wait_recv()
                    copy(a, 4 + j, (*chip, c), sibling).start()

        def finish():
            for a in range(n_arr):
                if direct:
                    for k in range(1, 8):
                        copy(a, k - 1, _peer(x, y, c, k), me).wait_recv()
                else:
                    copy(a, 0, sibling, me).wait_recv()
                    for j, chip in enumerate(chips):
                        copy(a, 4 + j, (*chip, 1 - c), me).wait_recv()
                        copy(a, 4 + j, (*chip, c), sibling).wait_send()
                for cp in own_sends(a):
                    cp.wait_send()
                mine(a).wait()

        return start, relay, finish

    def gathered(s):
        return (N_DEV, *s.shape) if stack else (N_DEV * s.shape[0], s.shape[1])

    return _Comm(shards, [jax.ShapeDtypeStruct(gathered(s), s.dtype) for s in shards],
                 [pltpu.SemaphoreType.DMA((7 * n_arr,)), pltpu.SemaphoreType.DMA((7 * n_arr,)),
                  pltpu.SemaphoreType.DMA((n_arr,))], bind)


def scatter_comm(parts):
    n_arr = len(parts)

    def bind(ins, outs, sems):
        send_sems, recv_sems, local_sems = sems
        x, y, c = _my_place()
        my_index = _index(x, y, c)

        def block(a, d):
            m = parts[a].shape[0] // N_DEV
            return ins[a].at[pl.ds(pl.multiple_of(d * m, 16), m), :]

        def copy(a, k, slot):
            peer = _peer(x, y, c, k)
            return pltpu.make_async_remote_copy(
                src_ref=block(a, _index(*peer)), dst_ref=outs[a].at[slot],
                send_sem=send_sems.at[a * 7 + k - 1], recv_sem=recv_sems.at[a * 7 + k - 1],
                device_id=peer, device_id_type=MESH)

        def mine(a):
            return pltpu.make_async_copy(block(a, my_index), outs[a].at[my_index], local_sems.at[a])

        def start():
            for a in range(n_arr):
                mine(a).start()
                for k in range(1, 8):
                    copy(a, k, my_index).start()

        def finish():
            for a in range(n_arr):
                for k in range(1, 8):
                    copy(a, k, _index(*_peer(x, y, c, k))).wait_recv()
                    copy(a, k, my_index).wait_send()
                mine(a).wait()

        return start, (lambda: None), finish

    return _Comm(parts, [jax.ShapeDtypeStruct((N_DEV, p.shape[0] // N_DEV, p.shape[1]), p.dtype) for p in parts],
                 [pltpu.SemaphoreType.DMA((7 * n_arr,)), pltpu.SemaphoreType.DMA((7 * n_arr,)),
                  pltpu.SemaphoreType.DMA((n_arr,))], bind)


def pair_comm(parts):
    n_arr = len(parts)

    def bind(ins, outs, sems):
        send_sems, recv_sems = sems
        x, y, c = _my_place()

        def copies():
            out = []
            for a in range(n_arr):
                m = parts[a].shape[0] // N_DEV
                for r in range(4):
                    owner = _index(*_chip_of(x, y, r), 1 - c)
                    out.append(pltpu.make_async_remote_copy(
                        src_ref=ins[a].at[pl.ds(pl.multiple_of(owner * m, 16), m), :], dst_ref=outs[a].at[r],
                        send_sem=send_sems.at[a * 4 + r], recv_sem=recv_sems.at[a * 4 + r],
                        device_id=(x, y, 1 - c), device_id_type=MESH))
            return out

        def start():
            for cp in copies():
                cp.start()

        def finish():
            for cp in copies():
                cp.wait_recv()
                cp.wait_send()

        return start, (lambda: None), finish

    return _Comm(parts, [jax.ShapeDtypeStruct((4, p.shape[0] // N_DEV, p.shape[1]), p.dtype) for p in parts],
                 [pltpu.SemaphoreType.DMA((4 * n_arr,)), pltpu.SemaphoreType.DMA((4 * n_arr,))], bind)


def chips_comm(sums):
    n_arr = len(sums)

    def bind(ins, outs, sems):
        send_sems, recv_sems = sems
        x, y, c = _my_place()

        def copies():
            return [pltpu.make_async_remote_copy(
                src_ref=ins[a].at[r], dst_ref=outs[a].at[r - 1],
                send_sem=send_sems.at[a * 3 + r - 1], recv_sem=recv_sems.at[a * 3 + r - 1],
                device_id=(*_chip_of(x, y, r), c), device_id_type=MESH) for a in range(n_arr) for r in (1, 2, 3)]

        def start():
            for cp in copies():
                cp.start()

        def finish():
            for cp in copies():
                cp.wait_recv()
                cp.wait_send()

        return start, (lambda: None), finish

    return _Comm(sums, [jax.ShapeDtypeStruct((3,) + s.shape[1:], s.dtype) for s in sums],
                 [pltpu.SemaphoreType.DMA((3 * n_arr,)), pltpu.SemaphoreType.DMA((3 * n_arr,))], bind)


def run_comm(comm, name):
    n_in, n_out = len(comm.inputs), len(comm.out_shape)

    def body(*refs):
        start, relay, finish = comm.bind(refs[:n_in], refs[n_in:n_in + n_out], refs[n_in + n_out:])
        start()
        relay()
        finish()

    outs = pl.pallas_call(body, name=name, out_shape=comm.out_shape, in_specs=[ANY] * n_in,
                          out_specs=[ANY] * n_out, scratch_shapes=comm.scratch)(*comm.inputs)
    return list(outs)


class _Hosted:
    def __init__(self, comms):
        self.comms = list(comms)
        self.inputs = [a for cm in self.comms for a in cm.inputs]
        self.out_shape = [s for cm in self.comms for s in cm.out_shape]
        self.scratch = [s for cm in self.comms for s in cm.scratch]
        self.in_specs = [ANY] * len(self.inputs)
        self.out_specs = [ANY] * len(self.out_shape)

    def split(self, refs, n_in, n_out, n_scratch):
        ni, no = len(self.inputs), len(self.out_shape)
        ins, rest = refs[:n_in], refs[n_in:]
        c_ins, rest = rest[:ni], rest[ni:]
        outs, rest = rest[:n_out], rest[n_out:]
        c_outs, rest = rest[:no], rest[no:]
        scratch, c_sems = rest[:n_scratch], rest[n_scratch:]
        phases = []
        for cm in self.comms:
            a, b, s = len(cm.inputs), len(cm.out_shape), len(cm.scratch)
            phases.append(cm.bind(c_ins[:a], c_outs[:b], c_sems[:s]))
            c_ins, c_outs, c_sems = c_ins[a:], c_outs[b:], c_sems[s:]
        return ins, outs, scratch, phases


def _before_step(phases, step, n_steps):
    if not phases:
        return

    @pl.when(step == 0)
    def _():
        for start, _, _ in phases:
            start()

    @pl.when(step == n_steps // 2)
    def _():
        for _, relay, _ in phases:
            relay()


def _after_step(phases, step, n_steps):
    if not phases:
        return

    @pl.when(step == n_steps - 1)
    def _():
        for _, _, finish in phases:
            finish()


def pair_sum(part, got, owners, name):
    m, n = got.shape[1:]

    def body(own_ref, mine_ref, got_ref, out_ref):
        del own_ref
        out_ref[...] = (mine_ref[...].astype(F32) + got_ref[...].astype(F32)).astype(out_ref.dtype)

    return pl.pallas_call(
        body, name=name,
        grid_spec=pltpu.PrefetchScalarGridSpec(
            num_scalar_prefetch=1, grid=(4,),
            in_specs=[pl.BlockSpec((m, n), lambda r, own: (own[r], 0)),
                      pl.BlockSpec((None, m, n), lambda r, own: (r, 0, 0))],
            out_specs=pl.BlockSpec((None, m, n), lambda r, own: (r, 0, 0))),
        out_shape=jax.ShapeDtypeStruct((4, m, n), got.dtype),
        compiler_params=_cparams(("arbitrary",)),
    )(owners, part, got)


def in_proj(x, g_pre, wt, bias, comms=()):
    s, d = x.shape
    tm = _tile(s, (512, 256, 128))
    tn = 768
    ni, nj = s // tm, D_IN // tn
    hosted = _Hosted(comms)

    def body(*refs):
        (x_ref, g_ref, w_ref, b_ref), (proj_ref, h_ref), _, phases = hosted.split(refs, 4, 2, 0)
        step = pl.program_id(0) * nj + pl.program_id(1)
        _before_step(phases, step, ni * nj)

        @pl.when(pl.program_id(1) == 0)
        def _():
            xv = x_ref[...]
            r = lax.rsqrt(jnp.mean(xv * xv, axis=-1, keepdims=True) + EPS)
            h_ref[...] = (xv * r * g_ref[...]).astype(BF16)

        acc = _dot(h_ref[...], w_ref[...], 1, 1)
        proj_ref[...] = (acc + b_ref[...]).astype(BF16)
        _after_step(phases, step, ni * nj)

    return pl.pallas_call(
        body, name="in_proj", grid=(ni, nj),
        in_specs=[pl.BlockSpec((tm, d), lambda i, j: (i, 0)),
                  pl.BlockSpec((1, d), lambda i, j: (0, 0)),
                  pl.BlockSpec((tn, d), lambda i, j: (j, 0)),
                  pl.BlockSpec((1, tn), lambda i, j: (0, j))] + hosted.in_specs,
        out_specs=[pl.BlockSpec((tm, tn), lambda i, j: (i, j)),
                   pl.BlockSpec((tm, d), lambda i, j: (i, 0))] + hosted.out_specs,
        out_shape=[jax.ShapeDtypeStruct((s, D_IN), BF16), jax.ShapeDtypeStruct((s, d), BF16)] + hosted.out_shape,
        scratch_shapes=hosted.scratch,
        compiler_params=_cparams(("arbitrary", "arbitrary")),
    )(x, g_pre, wt, bias, *hosted.inputs)


def in_proj_gather(x, pos_col, freq, sign, g_pre, wt_shard, bias):
    s, d = x.shape
    tm = _tile(s, (256, 128))
    nt = s // tm
    m = wt_shard.shape[0]
    half = D_IN // 2
    xi = lax.axis_index("x")
    order = jnp.stack([xi, 1 - xi]).astype(jnp.int32)

    def body(order_ref, x_ref, pos_ref, freq_ref, sign_ref, g_ref, b_ref, shard_ref,
             proj_ref, h_ref, rope_ref, wt_ref, w_vmem, send_sems, recv_sems, local_sems):
        del order_ref
        p, i = pl.program_id(0), pl.program_id(1)
        xx, yy, cc = _my_place()
        me, sibling = (xx, yy, cc), (xx, yy, 1 - cc)
        chips = [_chip_of(xx, yy, r) for r in (1, 2, 3)]

        def rows(px, py, pc):
            return wt_ref.at[pl.ds(pl.multiple_of(_index(px, py, pc) * m, 16), m), :]

        def copy(k, block, to, src=None):
            return pltpu.make_async_remote_copy(
                src_ref=rows(*block) if src is None else src, dst_ref=rows(*block),
                send_sem=send_sems.at[k], recv_sem=recv_sems.at[k], device_id=to, device_id_type=MESH)

        def mine():
            return pltpu.make_async_copy(shard_ref, rows(*me), local_sems.at[0])

        def to_sibling():
            return copy(0, me, sibling, src=shard_ref)

        def to_chip(j):
            return copy(1 + j, me, (*chips[j], cc), src=shard_ref)

        def relay(j):
            copy(1 + j, (*chips[j], cc), me).wait_recv()
            copy(4 + j, (*chips[j], cc), sibling).start()

        def relayed(j):
            copy(4 + j, (*chips[j], 1 - cc), me).wait_recv()

        def load_half(which, slot):
            rows_of_half = wt_ref.at[pl.ds(pl.multiple_of(which * half, 16), half), :]
            load = pltpu.make_async_copy(rows_of_half, w_vmem.at[slot], local_sems.at[1 + slot])
            load.start()
            load.wait()

        @pl.when(jnp.logical_and(p == 0, i == 0))
        def _():
            mine().start()
            to_sibling().start()
            to_chip(1).start()
            to_chip(0).start()
            copy(0, sibling, me).wait_recv()
            relay(1)
            relayed(1)
            mine().wait()
            to_chip(1).wait_send()
            to_chip(0).wait_send()
            to_chip(2).start()
            load_half(xx, 0)

        @pl.when(jnp.logical_and(p == 1, i == 0))
        def _():
            relayed(0)
            relayed(2)
            load_half(1 - xx, 1)

        xv = x_ref[...]
        r = lax.rsqrt(jnp.mean(xv * xv, axis=-1, keepdims=True) + EPS)
        hb = (xv * r * g_ref[...]).astype(BF16)
        proj_ref[...] = (_dot(hb, w_vmem[p], 1, 1) + b_ref[...]).astype(BF16)

        @pl.when(p == 0)
        def _():
            h_ref[...] = hb
            ang = pos_ref[...].astype(F32) * freq_ref[...]
            rope_ref[:, :LANES] = jnp.cos(ang)
            rope_ref[:, LANES:] = jnp.sin(ang) * sign_ref[...]

        @pl.when(jnp.logical_and(p == 0, i == nt - 1))
        def _():
            relay(0)
            relay(2)

        @pl.when(jnp.logical_and(p == 1, i == nt - 1))
        def _():
            to_sibling().wait_send()
            to_chip(2).wait_send()
            for j in range(3):
                copy(4 + j, (*chips[j], cc), sibling).wait_send()

    const = lambda p, i, o: (0, 0)
    once = lambda p, i, o: (jnp.where(p == 0, i, nt - 1), 0)
    return pl.pallas_call(
        body, name="in_proj_gather",
        grid_spec=pltpu.PrefetchScalarGridSpec(
            num_scalar_prefetch=1, grid=(2, nt),
            in_specs=[pl.BlockSpec((tm, d), lambda p, i, o: (i, 0)),
                      pl.BlockSpec((tm, 1), lambda p, i, o: (i, 0)),
                      pl.BlockSpec((1, LANES), const),
                      pl.BlockSpec((1, LANES), const),
                      pl.BlockSpec((1, d), const),
                      pl.BlockSpec((1, half), lambda p, i, o: (0, o[p])),
                      ANY],
            out_specs=[pl.BlockSpec((tm, half), lambda p, i, o: (i, o[p])),
                       pl.BlockSpec((tm, d), once),
                       pl.BlockSpec((tm, 2 * LANES), once),
                       ANY],
            scratch_shapes=[pltpu.VMEM((2, half, d), BF16), pltpu.SemaphoreType.DMA((7,)),
                            pltpu.SemaphoreType.DMA((7,)), pltpu.SemaphoreType.DMA((3,))]),
        out_shape=[jax.ShapeDtypeStruct((s, D_IN), BF16), jax.ShapeDtypeStruct((s, d), BF16),
                   jax.ShapeDtypeStruct((s, 2 * LANES), F32), jax.ShapeDtypeStruct((D_IN, d), BF16)],
        compiler_params=_cparams(("arbitrary", "arbitrary")),
    )(order, x, pos_col, freq, sign, g_pre, bias, wt_shard)


def out_proj_loss(cat, w_out, x, target, g_post):
    s, d = x.shape
    tm = _tile(s, (256, 128))

    def body(cat_ref, w_ref, x_ref, t_ref, g_ref, dy_ref, dout_ref, dg_ref, loss_ref):
        @pl.when(pl.program_id(0) == 0)
        def _():
            dg_ref[...] = jnp.zeros_like(dg_ref)
            loss_ref[...] = jnp.zeros_like(loss_ref)

        g = g_ref[...]
        for c0 in range(0, tm, CHUNK):
            rows = slice(c0, c0 + CHUNK)
            yv = _dot(cat_ref[rows, :], w_ref[...], 1, 0)
            r = lax.rsqrt(jnp.mean(yv * yv, axis=-1, keepdims=True) + EPS)
            nrm = yv * r
            err = x_ref[rows, :] + nrm * g - t_ref[rows, :]
            loss_ref[...] += 0.5 * jnp.sum(jnp.sum(err * err, axis=-1, keepdims=True), axis=0, keepdims=True) / d
            dout = err * (1.0 / d)
            dout_ref[rows, :] = dout
            dg_ref[...] += jnp.sum(dout * nrm, axis=0, keepdims=True)
            dn = dout * g
            dy = r * (dn - nrm * jnp.mean(dn * nrm, axis=-1, keepdims=True))
            dy_ref[rows, :] = dy.astype(BF16)

    return pl.pallas_call(
        body, name="out_proj_loss", grid=(s // tm,),
        in_specs=[pl.BlockSpec((tm, d), lambda i: (i, 0)),
                  pl.BlockSpec((d, d), lambda i: (0, 0)),
                  pl.BlockSpec((tm, d), lambda i: (i, 0)),
                  pl.BlockSpec((tm, d), lambda i: (i, 0)),
                  pl.BlockSpec((1, d), lambda i: (0, 0))],
        out_specs=[pl.BlockSpec((tm, d), lambda i: (i, 0)),
                   pl.BlockSpec((tm, d), lambda i: (i, 0)),
                   pl.BlockSpec((1, d), lambda i: (0, 0)),
                   pl.BlockSpec((1, LANES), lambda i: (0, 0))],
        out_shape=[jax.ShapeDtypeStruct((s, d), BF16), jax.ShapeDtypeStruct((s, d), F32),
                   jax.ShapeDtypeStruct((1, d), F32), jax.ShapeDtypeStruct((1, LANES), F32)],
        compiler_params=_cparams(("arbitrary",)),
    )(cat, w_out, x, target, g_post)


def matmul_nt(a, b, name):
    m, k = a.shape
    n = b.shape[0]
    tm = _tile(m, (512, 256, 128))

    def body(a_ref, b_ref, o_ref):
        o_ref[...] = _dot(a_ref[...], b_ref[...], 1, 1).astype(o_ref.dtype)

    return pl.pallas_call(
        body, name=name, grid=(m // tm,),
        in_specs=[pl.BlockSpec((tm, k), lambda i: (i, 0)), pl.BlockSpec((n, k), lambda i: (0, 0))],
        out_specs=pl.BlockSpec((tm, n), lambda i: (i, 0)),
        out_shape=jax.ShapeDtypeStruct((m, n), BF16),
        compiler_params=_cparams(("arbitrary",)),
    )(a, b)


def matmul_tn(a, b, tm, name, comms=()):
    k, m = a.shape
    n = b.shape[1]
    steps = m // tm
    hosted = _Hosted(comms)

    def body(*refs):
        (a_ref, b_hbm), (o_ref, cs_ref), (b_ref, b_sem), phases = hosted.split(refs, 2, 2, 2)
        step = pl.program_id(0)
        _before_step(phases, step, steps)

        @pl.when(step == 0)
        def _():
            load = pltpu.make_async_copy(b_hbm, b_ref, b_sem)
            load.start()
            load.wait()

        o_ref[...] = _dot(a_ref[...], b_ref[...], 0, 0).astype(o_ref.dtype)
        rows = _tile(k, (512, 128))
        cs = jnp.zeros((1, tm), F32)
        for r0 in range(0, k, rows):
            cs = cs + jnp.sum(a_ref[r0:r0 + rows, :].astype(F32), axis=0, keepdims=True)
        cs_ref[...] = cs
        _after_step(phases, step, steps)

    return pl.pallas_call(
        body, name=name, grid=(steps,),
        in_specs=[pl.BlockSpec((k, tm), lambda i: (0, i)), ANY] + hosted.in_specs,
        out_specs=[pl.BlockSpec((tm, n), lambda i: (i, 0)), pl.BlockSpec((1, tm), lambda i: (0, i))] + hosted.out_specs,
        out_shape=[jax.ShapeDtypeStruct((m, n), BF16), jax.ShapeDtypeStruct((1, m), F32)] + hosted.out_shape,
        scratch_shapes=[pltpu.VMEM((k, n), b.dtype), pltpu.SemaphoreType.DMA] + hosted.scratch,
        compiler_params=_cparams(("arbitrary",)),
    )(a, b, *hosted.inputs)


def in_proj_bwd(dproj, wt, x, g_pre, dout, comms=()):
    s, d = x.shape
    tm = _tile(s, (256, 128))
    steps = s // tm
    hosted = _Hosted(comms)

    def body(*refs):
        (dp_ref, w_hbm, x_ref, g_ref, dout_ref), (gx_ref, dg_ref), (w_ref, w_sem), phases = hosted.split(refs, 5, 2, 2)
        step = pl.program_id(0)
        _before_step(phases, step, steps)

        @pl.when(step == 0)
        def _():
            dg_ref[...] = jnp.zeros_like(dg_ref)
            load = pltpu.make_async_copy(w_hbm, w_ref, w_sem)
            load.start()
            load.wait()

        for c0 in range(0, tm, CHUNK):
            rows = slice(c0, c0 + CHUNK)
            dh = _dot(dp_ref[rows, :], w_ref[...], 1, 0)
            xv = x_ref[rows, :]
            r = lax.rsqrt(jnp.mean(xv * xv, axis=-1, keepdims=True) + EPS)
            xn = xv * r
            dg_ref[...] += jnp.sum(dh * xn, axis=0, keepdims=True)
            dn = dh * g_ref[...]
            gx_ref[rows, :] = dout_ref[rows, :] + r * (dn - xn * jnp.mean(dn * xn, axis=-1, keepdims=True))

        _after_step(phases, step, steps)

    return pl.pallas_call(
        body, name="in_proj_bwd", grid=(steps,),
        in_specs=[pl.BlockSpec((tm, D_IN), lambda i: (i, 0)),
                  ANY,
                  pl.BlockSpec((tm, d), lambda i: (i, 0)),
                  pl.BlockSpec((1, d), lambda i: (0, 0)),
                  pl.BlockSpec((tm, d), lambda i: (i, 0))] + hosted.in_specs,
        out_specs=[pl.BlockSpec((tm, d), lambda i: (i, 0)),
                   pl.BlockSpec((1, d), lambda i: (0, 0))] + hosted.out_specs,
        out_shape=[jax.ShapeDtypeStruct((s, d), F32), jax.ShapeDtypeStruct((1, d), F32)] + hosted.out_shape,
        scratch_shapes=[pltpu.VMEM((D_IN, d), BF16), pltpu.SemaphoreType.DMA] + hosted.scratch,
        compiler_params=_cparams(("arbitrary",)),
    )(dproj, wt, x, g_pre, dout, *hosted.inputs)


def _lane_iota(shape):
    return lax.broadcasted_iota(jnp.int32, shape, len(shape) - 1)


def rope_tables(pos_col, freq, sign, comms=()):
    s = pos_col.shape[0]
    tr = _tile(s, (512, 256, 128))
    hosted = _Hosted(comms)

    def body(*refs):
        (pos_ref, freq_ref, sign_ref), (out_ref,), _, phases = hosted.split(refs, 3, 1, 0)
        _before_step(phases, pl.program_id(0), s // tr)
        ang = pos_ref[...].astype(F32) * freq_ref[...]
        out_ref[:, :LANES] = jnp.cos(ang)
        out_ref[:, LANES:] = jnp.sin(ang) * sign_ref[...]
        _after_step(phases, pl.program_id(0), s // tr)

    return pl.pallas_call(
        body, name="rope_tables", grid=(s // tr,),
        in_specs=[pl.BlockSpec((tr, 1), lambda i: (i, 0)), pl.BlockSpec((1, LANES), lambda i: (0, 0)),
                  pl.BlockSpec((1, LANES), lambda i: (0, 0))] + hosted.in_specs,
        out_specs=[pl.BlockSpec((tr, 2 * LANES), lambda i: (i, 0))] + hosted.out_specs,
        out_shape=[jax.ShapeDtypeStruct((s, 2 * LANES), F32)] + hosted.out_shape,
        scratch_shapes=hosted.scratch,
        compiler_params=_cparams(("arbitrary",)),
    )(pos_col, freq, sign, *hosted.inputs)


def _partner(v):
    low = (_lane_iota(v.shape) % HEAD_DIM) < (HEAD_DIM // 2)
    return jnp.where(low, pltpu.roll(v, LANES - HEAD_DIM // 2, 1), pltpu.roll(v, HEAD_DIM // 2, 1))


def _rope(v, cos, sin_signed):
    return v * cos + _partner(v) * sin_signed


def _rope_transposed(dv, cos, sin_signed):
    return dv * cos - _partner(dv) * sin_signed


def _both_halves(v, kv_head):
    keep = (_lane_iota(v.shape) >= HEAD_DIM) if kv_head else (_lane_iota(v.shape) < HEAD_DIM)
    return jnp.where(keep, v, pltpu.roll(v, HEAD_DIM, 1))


def _fold_halves(acc):
    return acc + pltpu.roll(acc, HEAD_DIM, 1)


def _by_half(a, b):
    shape = jnp.broadcast_shapes(jnp.shape(a), jnp.shape(b))
    return jnp.where(_lane_iota(shape) < HEAD_DIM, a, b)


def _stack_heads(pair):
    return jnp.concatenate([_by_half(pair, 0.0), _by_half(0.0, pair)], axis=0)


def _band_bias(has_prev):
    i = lax.broadcasted_iota(jnp.int32, (2 * CHUNK, 2 * CHUNK), 0) % CHUNK
    j = lax.broadcasted_iota(jnp.int32, (2 * CHUNK, 2 * CHUNK), 1)
    band = jnp.logical_and(j > i, j <= i + CHUNK)
    return jnp.where(jnp.logical_and(band, jnp.logical_or(j >= CHUNK, has_prev)), 0.0, NEG)


def _probs(qm2, kk2, bias, sink_col):
    sc = _dot(qm2, kk2, 1, 1) + bias
    mx = jnp.maximum(jnp.max(sc, axis=-1, keepdims=True), sink_col)
    p = jnp.exp(sc - mx)
    es = jnp.exp(sink_col - mx)
    inv = 1.0 / (jnp.sum(p, axis=-1, keepdims=True) + es)
    return p * inv, es * inv


def _sink_col(sinks_ref, pair):
    row = lax.broadcasted_iota(jnp.int32, (2 * CHUNK, 1), 0)
    return jnp.where(row < CHUNK, sinks_ref[2 * pair], sinks_ref[2 * pair + 1])


def _layer_norm_parts(v):
    mu = jnp.mean(v, axis=-1, keepdims=True)
    xc = v - mu
    rstd = lax.rsqrt(jnp.mean(xc * xc, axis=-1, keepdims=True) + EPS)
    return xc * rstd, rstd


def _masked_spatial(w_ref, g):
    t = lax.broadcasted_iota(jnp.int32, (CHUNK, CHUNK), 0)
    sidx = lax.broadcasted_iota(jnp.int32, (CHUNK, CHUNK), 1)
    return jnp.where(t >= sidx, w_ref[g], 0.0).astype(BF16)


def _keys_values(kv_ref, kvp_ref, rope_ref, ropep_ref):
    cos_c, sin_c = rope_ref[:, :LANES], rope_ref[:, LANES:]
    cos_p, sin_p = ropep_ref[:, :LANES], ropep_ref[:, LANES:]
    k_c = _rope(kv_ref[:, :D_KV].astype(F32), cos_c, sin_c)
    k_p = _rope(kvp_ref[:, :D_KV].astype(F32), cos_p, sin_p)
    keys = jnp.concatenate([k_p, k_c], axis=0)
    vals = jnp.concatenate([kvp_ref[:, D_KV:], kv_ref[:, D_KV:]], axis=0).astype(F32)
    return keys, vals, (cos_c, sin_c, cos_p, sin_p)


def mixer_fwd(proj, rope, ln_g, ln_b, w_sp, b_sp_rows, sinks, comms=()):
    s = proj.shape[0]
    nb = s // CHUNK
    hosted = _Hosted(comms)

    def body(sinks_ref, *refs):
        ((proj_ref, kvp_ref, rope_ref, ropep_ref, lng_ref, lnb_ref, w_ref, b_ref), (cat_ref,), _,
         phases) = hosted.split(refs, 8, 1, 0)
        n = pl.program_id(0)
        _before_step(phases, n, nb)
        xhat, _ = _layer_norm_parts(proj_ref[:, OFF_V:OFF_V + D_GMLP].astype(F32))
        vnb = (xhat * lng_ref[...] + lnb_ref[...]).astype(BF16)
        for g in range(GROUPS):
            cols = slice(g * CHUNK, (g + 1) * CHUNK)
            mixed = _dot(_masked_spatial(w_ref, g), vnb[:, cols], 1, 0) + b_ref[g]
            za = proj_ref[:, OFF_ZA + g * CHUNK:OFF_ZA + (g + 1) * CHUNK].astype(F32)
            u = proj_ref[:, OFF_U + g * CHUNK:OFF_U + (g + 1) * CHUNK].astype(F32)
            cat_ref[:, cols] = (u * mixed * (za * _sigmoid(za))).astype(BF16)
        kv_ref = proj_ref.at[:, OFF_K:OFF_K + 2 * D_KV]
        keys, vals, (cos_c, sin_c, _, _) = _keys_values(kv_ref, kvp_ref, rope_ref, ropep_ref)
        cos_q, sin_q = cos_c * SCALE, sin_c * SCALE
        bias = _band_bias(n > 0)
        for pair in range(N_PAIRS):
            kvh = pair // PAIRS_PER_KV
            kk2 = _both_halves(keys, kvh).astype(BF16)
            vv2 = _both_halves(vals, kvh).astype(BF16)
            qcols = slice(OFF_Q + pair * LANES, OFF_Q + (pair + 1) * LANES)
            q_pair = _rope(proj_ref[:, qcols].astype(F32), cos_q, sin_q)
            p, _ = _probs(_stack_heads(q_pair).astype(BF16), kk2, bias, _sink_col(sinks_ref, pair))
            o2 = _dot(p.astype(BF16), vv2, 1, 0)
            out_pair = _by_half(o2[:CHUNK], o2[CHUNK:])
            zb = proj_ref[:, OFF_ZB + pair * LANES:OFF_ZB + (pair + 1) * LANES].astype(F32)
            cat_ref[:, D_GMLP + pair * LANES:D_GMLP + (pair + 1) * LANES] = (
                out_pair * (zb * _sigmoid(zb))).astype(BF16)
        _after_step(phases, n, nb)

    prev = lambda n, *_: (jnp.maximum(n - 1, 0), 0)
    kv_block = OFF_K // (2 * D_KV)
    return pl.pallas_call(
        body, name="mixer_fwd",
        grid_spec=pltpu.PrefetchScalarGridSpec(
            num_scalar_prefetch=1, grid=(nb,),
            in_specs=[pl.BlockSpec((CHUNK, D_IN), lambda n, *_: (n, 0)),
                      pl.BlockSpec((CHUNK, 2 * D_KV), lambda n, *_: (jnp.maximum(n - 1, 0), kv_block)),
                      pl.BlockSpec((CHUNK, 2 * LANES), lambda n, *_: (n, 0)),
                      pl.BlockSpec((CHUNK, 2 * LANES), prev),
                      pl.BlockSpec((1, D_GMLP), lambda n, *_: (0, 0)),
                      pl.BlockSpec((1, D_GMLP), lambda n, *_: (0, 0)),
                      pl.BlockSpec((GROUPS, CHUNK, CHUNK), lambda n, *_: (0, 0, 0)),
                      pl.BlockSpec((GROUPS, CHUNK, CHUNK), lambda n, *_: (0, 0, 0))] + hosted.in_specs,
            out_specs=[pl.BlockSpec((CHUNK, D_GMLP + D_ATTN), lambda n, *_: (n, 0))] + hosted.out_specs,
            scratch_shapes=hosted.scratch),
        out_shape=[jax.ShapeDtypeStruct((s, D_GMLP + D_ATTN), BF16)] + hosted.out_shape,
        compiler_params=_cparams(("arbitrary",)),
    )(sinks, proj, proj, rope, rope, ln_g, ln_b, w_sp, b_sp_rows, *hosted.inputs)


def mixer_bwd(proj, dcat, rope, ln_g, ln_b, w_sp, b_sp_rows, sinks):
    s = proj.shape[0]
    nb = s // CHUNK

    def body(sinks_ref, proj_ref, kvp_ref, dcat_ref, rope_ref, ropep_ref, lng_ref, lnb_ref, w_ref, b_ref,
             dproj_ref, dw_ref, db_ref, dlng_ref, dlnb_ref, dsink_ref,
             pend_ref, pend_kv_ref, dbacc_ref):
        n = pl.program_id(0)

        @pl.when(n == 0)
        def _():
            dw_ref[...] = jnp.zeros_like(dw_ref)
            dbacc_ref[...] = jnp.zeros_like(dbacc_ref)
            dlng_ref[...] = jnp.zeros_like(dlng_ref)
            dlnb_ref[...] = jnp.zeros_like(dlnb_ref)
            dsink_ref[...] = jnp.zeros_like(dsink_ref)

        def flush(dkv_prev):
            @pl.when(n > 0)
            def _():
                dproj_ref[...] = pend_ref[...]
                dproj_ref[:, OFF_K:OFF_K + 2 * D_KV] = (pend_kv_ref[...] + dkv_prev).astype(BF16)

        @pl.when(n < nb)
        def _():
            kv_ref = proj_ref.at[:, OFF_K:OFF_K + 2 * D_KV]
            keys, vals, (cos_c, sin_c, cos_p, sin_p) = _keys_values(kv_ref, kvp_ref, rope_ref, ropep_ref)
            cos_q, sin_q = cos_c * SCALE, sin_c * SCALE
            bias = _band_bias(n > 0)
            lane_row = _lane_iota((1, LANES))
            dsink = jnp.zeros((1, LANES), F32)
            dk_heads, dv_heads = [], []
            dq_pairs, dzb_pairs = [], []
            for kvh in range(N_KV_HEADS):
                kk2 = _both_halves(keys, kvh).astype(BF16)
                vv2 = _both_halves(vals, kvh).astype(BF16)
                dkk = jnp.zeros((2 * CHUNK, LANES), F32)
                dvv = jnp.zeros((2 * CHUNK, LANES), F32)
                for pair in range(kvh * PAIRS_PER_KV, (kvh + 1) * PAIRS_PER_KV):
                    qcols = slice(OFF_Q + pair * LANES, OFF_Q + (pair + 1) * LANES)
                    q_pair = _rope(proj_ref[:, qcols].astype(F32), cos_q, sin_q)
                    qm2 = _stack_heads(q_pair).astype(BF16)
                    p, p_sink = _probs(qm2, kk2, bias, _sink_col(sinks_ref, pair))
                    pb = p.astype(BF16)
                    o2 = _dot(pb, vv2, 1, 0)
                    out_pair = _by_half(o2[:CHUNK], o2[CHUNK:])
                    zb = proj_ref[:, OFF_ZB + pair * LANES:OFF_ZB + (pair + 1) * LANES].astype(F32)
                    sg = _sigmoid(zb)
                    dyb = dcat_ref[:, D_GMLP + pair * LANES:D_GMLP + (pair + 1) * LANES].astype(F32)
                    dzb_pairs.append((dyb * out_pair * (sg * (1.0 + zb * (1.0 - sg)))).astype(BF16))
                    dom2 = _stack_heads(dyb * (zb * sg)).astype(BF16)
                    dp = _dot(dom2, vv2, 1, 1)
                    delta = jnp.sum(p * dp, axis=-1, keepdims=True)
                    ds = p * (dp - delta)
                    dsk = -(p_sink * delta)
                    dsink = dsink + jnp.where(lane_row == 2 * pair,
                                              jnp.sum(dsk[:CHUNK], axis=0, keepdims=True), 0.0)
                    dsink = dsink + jnp.where(lane_row == 2 * pair + 1,
                                              jnp.sum(dsk[CHUNK:], axis=0, keepdims=True), 0.0)
                    dsb = ds.astype(BF16)
                    dq2 = _dot(dsb, kk2, 1, 0)
                    dq_pairs.append(_rope_transposed(_by_half(dq2[:CHUNK], dq2[CHUNK:]), cos_q, sin_q).astype(BF16))
                    dkk = dkk + _dot(dsb, qm2, 0, 0)
                    dvv = dvv + _dot(pb, dom2, 0, 0)
                dk_heads.append(_fold_halves(dkk))
                dv_heads.append(_fold_halves(dvv))
            dk_rot = _by_half(dk_heads[0], dk_heads[1])
            dv_all = _by_half(dv_heads[0], dv_heads[1])
            dk_p = _rope_transposed(dk_rot[:CHUNK], cos_p, sin_p)
            dk_c = _rope_transposed(dk_rot[CHUNK:], cos_c, sin_c)
            flush(jnp.concatenate([dk_p, dv_all[:CHUNK]], axis=1))
            dsink_ref[...] += dsink
            pend_kv_ref[...] = jnp.concatenate([dk_c, dv_all[CHUNK:]], axis=1)
            for pair in range(N_PAIRS):
                pend_ref[:, OFF_Q + pair * LANES:OFF_Q + (pair + 1) * LANES] = dq_pairs[pair]
                pend_ref[:, OFF_ZB + pair * LANES:OFF_ZB + (pair + 1) * LANES] = dzb_pairs[pair]
            xhat, rstd = _layer_norm_parts(proj_ref[:, OFF_V:OFF_V + D_GMLP].astype(F32))
            lng = lng_ref[...]
            vnb = (xhat * lng + lnb_ref[...]).astype(BF16)
            dvn_cols = []
            for g in range(GROUPS):
                cols = slice(g * CHUNK, (g + 1) * CHUNK)
                wm = _masked_spatial(w_ref, g)
                mixed = _dot(wm, vnb[:, cols], 1, 0) + b_ref[g]
                za = proj_ref[:, OFF_ZA + g * CHUNK:OFF_ZA + (g + 1) * CHUNK].astype(F32)
                u = proj_ref[:, OFF_U + g * CHUNK:OFF_U + (g + 1) * CHUNK].astype(F32)
                dya = dcat_ref[:, cols].astype(F32)
                sg = _sigmoid(za)
                sz = za * sg
                pend_ref[:, OFF_U + g * CHUNK:OFF_U + (g + 1) * CHUNK] = (dya * mixed * sz).astype(BF16)
                pend_ref[:, OFF_ZA + g * CHUNK:OFF_ZA + (g + 1) * CHUNK] = (
                    dya * u * mixed * (sg * (1.0 + za * (1.0 - sg)))).astype(BF16)
                dmixed = dya * u * sz
                dmb = dmixed.astype(BF16)
                dbacc_ref[g] += dmixed
                dw_ref[g] += _dot(dmb, vnb[:, cols], 1, 1)
                dvn_cols.append(_dot(wm, dmb, 0, 0))
            dvn = jnp.concatenate(dvn_cols, axis=1)
            dlng_ref[...] += jnp.sum(dvn * xhat, axis=0, keepdims=True)
            dlnb_ref[...] += jnp.sum(dvn, axis=0, keepdims=True)
            dxh = dvn * lng
            dv = rstd * (dxh - jnp.mean(dxh, axis=-1, keepdims=True)
                         - xhat * jnp.mean(dxh * xhat, axis=-1, keepdims=True))
            pend_ref[:, OFF_V:OFF_V + D_GMLP] = dv.astype(BF16)

        @pl.when(n == nb)
        def _():
            flush(jnp.zeros((CHUNK, 2 * D_KV), F32))
            t = lax.broadcasted_iota(jnp.int32, (CHUNK, CHUNK), 0)
            sidx = lax.broadcasted_iota(jnp.int32, (CHUNK, CHUNK), 1)
            lane = _lane_iota((CHUNK, LANES))
            dbt = jnp.zeros((CHUNK, LANES), F32)
            for g in range(GROUPS):
                dw_ref[g] = jnp.where(t >= sidx, dw_ref[g], 0.0)
                dbt = jnp.where(lane == g, jnp.sum(dbacc_ref[g], axis=-1, keepdims=True), dbt)
            db_ref[...] = jnp.transpose(dbt)[:GROUPS, :]

    cur = lambda n, *_: (jnp.minimum(n, nb - 1), 0)
    prev = lambda n, *_: (jnp.clip(n - 1, 0, nb - 1), 0)
    kv_block = OFF_K // (2 * D_KV)
    const2 = lambda n, *_: (0, 0)
    const3 = lambda n, *_: (0, 0, 0)
    return pl.pallas_call(
        body, name="mixer_bwd",
        grid_spec=pltpu.PrefetchScalarGridSpec(
            num_scalar_prefetch=1, grid=(nb + 1,),
            in_specs=[pl.BlockSpec((CHUNK, D_IN), cur),
                      pl.BlockSpec((CHUNK, 2 * D_KV), lambda n, *_: (jnp.clip(n - 1, 0, nb - 1), kv_block)),
                      pl.BlockSpec((CHUNK, D_GMLP + D_ATTN), cur),
                      pl.BlockSpec((CHUNK, 2 * LANES), cur),
                      pl.BlockSpec((CHUNK, 2 * LANES), prev),
                      pl.BlockSpec((1, D_GMLP), const2),
                      pl.BlockSpec((1, D_GMLP), const2),
                      pl.BlockSpec((GROUPS, CHUNK, CHUNK), const3),
                      pl.BlockSpec((GROUPS, CHUNK, CHUNK), const3)],
            out_specs=[pl.BlockSpec((CHUNK, D_IN), lambda n, *_: (jnp.maximum(n - 1, 0), 0)),
                       pl.BlockSpec((GROUPS, CHUNK, CHUNK), const3),
                       pl.BlockSpec((GROUPS, CHUNK), const2),
                       pl.BlockSpec((1, D_GMLP), const2),
                       pl.BlockSpec((1, D_GMLP), const2),
                       pl.BlockSpec((1, LANES), const2)],
            scratch_shapes=[pltpu.VMEM((CHUNK, D_IN), BF16), pltpu.VMEM((CHUNK, 2 * D_KV), F32),
                            pltpu.VMEM((GROUPS, CHUNK, CHUNK), F32)]),
        out_shape=[jax.ShapeDtypeStruct((s, D_IN), BF16),
                   jax.ShapeDtypeStruct((GROUPS, CHUNK, CHUNK), F32),
                   jax.ShapeDtypeStruct((GROUPS, CHUNK), F32),
                   jax.ShapeDtypeStruct((1, D_GMLP), F32),
                   jax.ShapeDtypeStruct((1, D_GMLP), F32),
                   jax.ShapeDtypeStruct((1, LANES), F32)],
        compiler_params=_cparams(("arbitrary",)),
    )(sinks, proj, proj, dcat, rope, rope, ln_g, ln_b, w_sp, b_sp_rows)


def _adamw_math(w, g, m, v):
    m = ADAM_B1 * m + (1.0 - ADAM_B1) * g
    v = ADAM_B2 * v + (1.0 - ADAM_B2) * (g * g)
    m_hat = m / (1.0 - ADAM_B1 ** ADAM_STEP)
    v_hat = v / (1.0 - ADAM_B2 ** ADAM_STEP)
    delta = -ADAM_LR * (m_hat / (jnp.sqrt(v_hat) + ADAM_EPS) + ADAM_WD * w)
    return delta, m, v


def adamw_shard(terms, w, m, v, name):
    r, c = w.shape
    tr = _tile(r, (224, 256, 128, 8))
    n_terms = len(terms)

    def body(*refs):
        w_ref, m_ref, v_ref, g_ref, d_ref, nm_ref, nv_ref = refs[n_terms:]
        g = None
        for ref, (_, slots) in zip(refs[:n_terms], terms):
            for k in range(slots):
                part = ref[k].astype(F32)
                g = part if g is None else g + part
        g_ref[...] = g
        d_ref[...], nm_ref[...], nv_ref[...] = _adamw_math(w_ref[...], g, m_ref[...], v_ref[...])

    spec = pl.BlockSpec((tr, c), lambda i: (i, 0))
    return pl.pallas_call(
        body, name=name, grid=(r // tr,),
        in_specs=[pl.BlockSpec((slots, tr, c), lambda i: (0, i, 0)) for _, slots in terms] + [spec] * 3,
        out_specs=[spec] * 4, out_shape=[jax.ShapeDtypeStruct((r, c), F32)] * 4,
        compiler_params=_cparams(("arbitrary",)),
    )(*[a for a, _ in terms], w, m, v)


def adamw_small(gathered, lane_windows, params):
    n_par = len(params)

    def body(*refs):
        g_refs = refs[:n_par + 1]
        wmv_refs = refs[n_par + 1:4 * n_par + 1]
        out_refs = refs[4 * n_par + 1:]

        def total(ref):
            acc = ref[0]
            for dev in range(1, N_DEV):
                acc = acc + ref[dev]
            return acc

        for i in range(n_par):
            w_ref, m_ref, v_ref = wmv_refs[3 * i:3 * i + 3]
            g = total(g_refs[i])
            if lane_windows[i] is not None:
                start, size = lane_windows[i]
                g = g[..., start:start + size]
            delta, new_m, new_v = _adamw_math(w_ref[...], g, m_ref[...], v_ref[...])
            for ref, val in zip(out_refs[4 * i:4 * i + 4], (g, delta, new_m, new_v)):
                ref[...] = val
        out_refs[4 * n_par][...] = total(g_refs[n_par])

    flat = [a for wmv in params for a in wmv]
    out_shape = [jax.ShapeDtypeStruct(w.shape, F32) for (w, _, _) in params for _ in range(4)]
    out_shape.append(jax.ShapeDtypeStruct(gathered[-1].shape[1:], F32))
    outs = pl.pallas_call(body, name="adamw_small", out_shape=out_shape, compiler_params=_cparams())(*gathered, *flat)
    return [tuple(outs[4 * i:4 * i + 4]) for i in range(n_par)], outs[-1]


def kernel(x, positions, g_pre, w_in, b_qkv, ln_v_g, ln_v_b, w_spatial, b_spatial, attn_sinks, w_out, g_post, loss_target, m_g_pre, m_w_in, m_b_qkv, m_ln_v_g, m_ln_v_b, m_w_spatial, m_b_spatial, m_attn_sinks, m_w_out, m_g_post, v_g_pre, v_w_in, v_b_qkv, v_ln_v_g, v_ln_v_b, v_w_spatial, v_b_spatial, v_attn_sinks, v_w_out, v_g_post):
    x2, target = x[0], loss_target[0]
    seq = x2.shape[0]
    xi, yi, ci = _my_place()

    wt_shard = w_in[0].T.astype(BF16)
    wo_shard = w_out[0].astype(BF16)
    pos_col = positions.reshape(seq, 1)
    half = HEAD_DIM // 2
    inv_freq = ROPE_THETA ** (-jnp.arange(half, dtype=F32) * (2.0 / HEAD_DIM))
    freq = jnp.tile(inv_freq, LANES // half).reshape(1, LANES)
    sign = jnp.tile(jnp.concatenate([-jnp.ones((half,), F32), jnp.ones((half,), F32)]), LANES // HEAD_DIM)
    sign = sign.reshape(1, LANES)
    bias = jnp.concatenate([jnp.zeros((1, OFF_Q), F32), b_qkv, jnp.zeros((1, D_ATTN), F32)], axis=1)
    proj, h, rope, wt = in_proj_gather(x2, pos_col, freq, sign, g_pre, wt_shard, bias)

    b_rows = jnp.broadcast_to(b_spatial[0][:, :, None], (GROUPS, CHUNK, CHUNK))
    sinks = attn_sinks[0]
    cat, wo = mixer_fwd(proj, rope, ln_v_g, ln_v_b, w_spatial[0], b_rows, sinks, comms=[gather_comm([wo_shard])])
    dy, dout, d_g_post, loss_part = out_proj_loss(cat, wo, x2, target, g_post)

    dcat = matmul_nt(dy, wo, "out_proj_bwd")
    d_wo, _ = matmul_tn(cat, dy, 512, "w_out_grad")
    dproj, d_w_sp, d_b_sp, d_ln_g, d_ln_b, d_sinks = mixer_bwd(
        proj, dcat, rope, ln_v_g, ln_v_b, w_spatial[0], b_rows, sinks)
    d_wt, colsum, parts_wo = matmul_tn(dproj, h, 768, "w_in_grad", comms=[scatter_comm([d_wo])])

    owners = jnp.stack([4 * cx + 2 * cy + ci for cx, cy in (_chip_of(xi, yi, r) for r in range(4))]).astype(jnp.int32)
    (got_wt,) = run_comm(pair_comm([d_wt]), "grad_exchange_pair")
    sum_wt = pair_sum(d_wt, got_wt, owners, "grad_pair_sum_w_in")
    small_parts = [colsum, d_ln_g, d_ln_b, d_w_sp, d_b_sp, d_sinks, d_g_post, loss_part]
    grad_x, d_g_pre, *landed = in_proj_bwd(
        dproj, wt, x2, g_pre, dout, comms=[gather_comm(small_parts, stack=True), chips_comm([sum_wt])])
    far_wt = landed[-1]
    (all_g_pre,) = run_comm(gather_comm([d_g_pre], stack=True, direct=True), "allgather_g_pre_grad")
    gathered = [all_g_pre] + landed[:-1]
    windows = [None, (OFF_Q, D_QKV), None, None, None, None, (0, N_Q_HEADS), None]
    small = [(g_pre, m_g_pre, v_g_pre), (b_qkv, m_b_qkv, v_b_qkv), (ln_v_g, m_ln_v_g, v_ln_v_g),
             (ln_v_b, m_ln_v_b, v_ln_v_b), (w_spatial[0], m_w_spatial[0], v_w_spatial[0]),
             (b_spatial[0], m_b_spatial[0], v_b_spatial[0]), (attn_sinks, m_attn_sinks, v_attn_sinks),
             (g_post, m_g_post, v_g_post)]
    small_out, loss_row = adamw_small(gathered, windows, small)
    lead = [False, False, False, False, True, True, False, False]
    small_out = [tuple(a[None] if ld else a for a in leaf) for leaf, ld in zip(small_out, lead)]

    wt_out = adamw_shard([(sum_wt, 1), (far_wt, 3)], w_in[0].T, m_w_in[0].T, v_w_in[0].T, "adamw_w_in")
    wo_out = adamw_shard([(parts_wo, N_DEV)], w_out[0], m_w_out[0], v_w_out[0], "adamw_w_out")

    def leaves(k):
        gp, bq, lg, lb, ws, bs, sk, gpo = (leaf[k] for leaf in small_out)
        return [gp, wt_out[k].T[None], bq, lg, lb, ws, bs, sk, wo_out[k][None], gpo]

    return (loss_row[0, 0], grad_x[None], *leaves(0), *leaves(1), *leaves(2), *leaves(3))
```

```python
import functools

import jax
import jax.numpy as jnp
from jax import lax
from jax.experimental import pallas as pl
from jax.experimental.pallas import tpu as pltpu

F32 = jnp.float32
BF16 = jnp.bfloat16

D_MODEL = 2048
D_GMLP = 1024
D_ATTN = 1024
CHUNK = 128
GROUPS = 8
HEAD_DIM = 64
N_Q_HEADS = 16
N_KV_HEADS = 2
D_KV = N_KV_HEADS * HEAD_DIM
D_IN = 3 * D_GMLP + D_ATTN + 2 * D_KV + D_ATTN
OFF_U, OFF_V, OFF_ZA = 0, D_GMLP, 2 * D_GMLP
OFF_Q = 3 * D_GMLP
OFF_K = OFF_Q + D_ATTN
OFF_VA = OFF_K + D_KV
OFF_ZB = OFF_VA + D_KV
D_QKV = D_ATTN + 2 * D_KV
ROPE_THETA = 10000.0
EPS = 1e-6
SCALE = HEAD_DIM ** -0.5
NEG = -1e30
N_PAIRS = N_Q_HEADS // 2
PAIRS_PER_KV = N_PAIRS // N_KV_HEADS

ADAM_LR = 0.001
ADAM_B1 = 0.9
ADAM_B2 = 0.999
ADAM_EPS = 1e-08
ADAM_WD = 0.01
ADAM_STEP = 10

N_DEV = 8
LANES = 128
VMEM_LIMIT = 56 * 1024 * 1024

MESH = pl.DeviceIdType.MESH
ANY = pl.BlockSpec(memory_space=pl.ANY)


def _cparams(sem=None):
    return pltpu.CompilerParams(dimension_semantics=sem, vmem_limit_bytes=VMEM_LIMIT)


def _tile(n, prefs):
    for t in prefs:
        if n % t == 0:
            return t
    return n


def _sigmoid(z):
    return 1.0 / (1.0 + jnp.exp(-z))


def _dot(a, b, ca, cb):
    return lax.dot_general(a, b, (((ca,), (cb,)), ((), ())), preferred_element_type=F32)


def _my_place():
    return lax.axis_index("x"), lax.axis_index("y"), lax.axis_index("c")


def _chip_of(x, y, r):
    return (x ^ (r & 1), y ^ (r >> 1))


def _peer(x, y, c, k):
    return (x ^ (k >> 2), y ^ ((k >> 1) & 1), c ^ (k & 1))


def _index(px, py, pc):
    return 4 * px + 2 * py + pc


class _Comm:
    def __init__(self, inputs, out_shape, scratch, bind):
        self.inputs, self.out_shape, self.scratch, self.bind = list(inputs), list(out_shape), list(scratch), bind


def gather_comm(shards, stack=False, direct=False):
    n_arr = len(shards)

    def bind(ins, outs, sems):
        send_sems, recv_sems, local_sems = sems
        x, y, c = _my_place()
        me, sibling = (x, y, c), (x, y, 1 - c)
        chips = [_chip_of(x, y, r) for r in (1, 2, 3)]

        def rows(a, px, py, pc):
            d = _index(px, py, pc)
            if stack:
                return outs[a].at[d]
            m = shards[a].shape[0]
            return outs[a].at[pl.ds(pl.multiple_of(d * m, 8), m), :]

        def copy(a, k, block, to, src=None):
            return pltpu.make_async_remote_copy(
                src_ref=rows(a, *block) if src is None else src, dst_ref=rows(a, *block),
                send_sem=send_sems.at[a * 7 + k], recv_sem=recv_sems.at[a * 7 + k],
                device_id=to, device_id_type=MESH)

        def mine(a):
            return pltpu.make_async_copy(ins[a], rows(a, *me), local_sems.at[a])

        def own_sends(a):
            if direct:
                return [copy(a, k - 1, me, _peer(x, y, c, k), src=ins[a]) for k in range(1, 8)]
            return ([copy(a, 0, me, sibling, src=ins[a])]
                    + [copy(a, 1 + j, me, (*chip, c), src=ins[a]) for j, chip in enumerate(chips)])

        def start():
            for a in range(n_arr):
                mine(a).start()
                for cp in own_sends(a):
                    cp.start()

        def relay():
            if direct:
                return
            for j, chip in enumerate(chips):
                for a in range(n_arr):
                    copy(a, 1 + j, (*chip, c), me).wait_recv()
                    copy(a, 4 + j, (*chip, c), sibling).start()

        def finish():
            for a in range(n_arr):
                if direct:
                    for k in range(1, 8):
                        copy(a, k - 1, _peer(x, y, c, k), me).wait_recv()
                else:
                    copy(a, 0, sibling, me).wait_recv()
                    for j, chip in enumerate(chips):
                        copy(a, 4 + j, (*chip, 1 - c), me).wait_recv()
                        copy(a, 4 + j, (*chip, c), sibling).wait_send()
                for cp in own_sends(a):
                    cp.wait_send()
                mine(a).wait()

        return start, relay, finish

    def gathered(s):
        return (N_DEV, *s.shape) if stack else (N_DEV * s.shape[0], s.shape[1])

    return _Comm(shards, [jax.ShapeDtypeStruct(gathered(s), s.dtype) for s in shards],
                 [pltpu.SemaphoreType.DMA((7 * n_arr,)), pltpu.SemaphoreType.DMA((7 * n_arr,)),
                  pltpu.SemaphoreType.DMA((n_arr,))], bind)


def scatter_comm(parts):
    n_arr = len(parts)

    def bind(ins, outs, sems):
        send_sems, recv_sems, local_sems = sems
        x, y, c = _my_place()
        my_index = _index(x, y, c)

        def block(a, d):
            m = parts[a].shape[0] // N_DEV
            return ins[a].at[pl.ds(pl.multiple_of(d * m, 16), m), :]

        def copy(a, k, slot):
            peer = _peer(x, y, c, k)
            return pltpu.make_async_remote_copy(
                src_ref=block(a, _index(*peer)), dst_ref=outs[a].at[slot],
                send_sem=send_sems.at[a * 7 + k - 1], recv_sem=recv_sems.at[a * 7 + k - 1],
                device_id=peer, device_id_type=MESH)

        def mine(a):
            return pltpu.make_async_copy(block(a, my_index), outs[a].at[my_index], local_sems.at[a])

        def start():
            for a in range(n_arr):
                mine(a).start()
                for k in range(1, 8):
                    copy(a, k, my_index).start()

        def finish():
            for a in range(n_arr):
                for k in range(1, 8):
                    copy(a, k, _index(*_peer(x, y, c, k))).wait_recv()
                    copy(a, k, my_index).wait_send()
                mine(a).wait()

        return start, (lambda: None), finish

    return _Comm(parts, [jax.ShapeDtypeStruct((N_DEV, p.shape[0] // N_DEV, p.shape[1]), p.dtype) for p in parts],
                 [pltpu.SemaphoreType.DMA((7 * n_arr,)), pltpu.SemaphoreType.DMA((7 * n_arr,)),
                  pltpu.SemaphoreType.DMA((n_arr,))], bind)


def pair_comm(parts):
    n_arr = len(parts)

    def bind(ins, outs, sems):
        send_sems, recv_sems = sems
        x, y, c = _my_place()

        def copies():
            out = []
            for a in range(n_arr):
                m = parts[a].shape[0] // N_DEV
                for r in range(4):
                    owner = _index(*_chip_of(x, y, r), 1 - c)
                    out.append(pltpu.make_async_remote_copy(
                        src_ref=ins[a].at[pl.ds(pl.multiple_of(owner * m, 16), m), :], dst_ref=outs[a].at[r],
                        send_sem=send_sems.at[a * 4 + r], recv_sem=recv_sems.at[a * 4 + r],
                        device_id=(x, y, 1 - c), device_id_type=MESH))
            return out

        def start():
            for cp in copies():
                cp.start()

        def finish():
            for cp in copies():
                cp.wait_recv()
                cp.wait_send()

        return start, (lambda: None), finish

    return _Comm(parts, [jax.ShapeDtypeStruct((4, p.shape[0] // N_DEV, p.shape[1]), p.dtype) for p in parts],
                 [pltpu.SemaphoreType.DMA((4 * n_arr,)), pltpu.SemaphoreType.DMA((4 * n_arr,))], bind)


def chips_comm(sums):
    n_arr = len(sums)

    def bind(ins, outs, sems):
        send_sems, recv_sems = sems
        x, y, c = _my_place()

        def copies():
            return [pltpu.make_async_remote_copy(
                src_ref=ins[a].at[r], dst_ref=outs[a].at[r - 1],
                send_sem=send_sems.at[a * 3 + r - 1], recv_sem=recv_sems.at[a * 3 + r - 1],
                device_id=(*_chip_of(x, y, r), c), device_id_type=MESH) for a in range(n_arr) for r in (1, 2, 3)]

        def start():
            for cp in copies():
                cp.start()

        def finish():
            for cp in copies():
                cp.wait_recv()
                cp.wait_send()

        return start, (lambda: None), finish

    return _Comm(sums, [jax.ShapeDtypeStruct((3,) + s.shape[1:], s.dtype) for s in sums],
                 [pltpu.SemaphoreType.DMA((3 * n_arr,)), pltpu.SemaphoreType.DMA((3 * n_arr,))], bind)


def run_comm(comm, name):
    n_in, n_out = len(comm.inputs), len(comm.out_shape)

    def body(*refs):
        start, relay, finish = comm.bind(refs[:n_in], refs[n_in:n_in + n_out], refs[n_in + n_out:])
        start()
        relay()
        finish()

    outs = pl.pallas_call(body, name=name, out_shape=comm.out_shape, in_specs=[ANY] * n_in,
                          out_specs=[ANY] * n_out, scratch_shapes=comm.scratch)(*comm.inputs)
    return list(outs)


class _Hosted:
    def __init__(self, comms):
        self.comms = list(comms)
        self.inputs = [a for cm in self.comms for a in cm.inputs]
        self.out_shape = [s for cm in self.comms for s in cm.out_shape]
        self.scratch = [s for cm in self.comms for s in cm.scratch]
        self.in_specs = [ANY] * len(self.inputs)
        self.out_specs = [ANY] * len(self.out_shape)

    def split(self, refs, n_in, n_out, n_scratch):
        ni, no = len(self.inputs), len(self.out_shape)
        ins, rest = refs[:n_in], refs[n_in:]
        c_ins, rest = rest[:ni], rest[ni:]
        outs, rest = rest[:n_out], rest[n_out:]
        c_outs, rest = rest[:no], rest[no:]
        scratch, c_sems = rest[:n_scratch], rest[n_scratch:]
        phases = []
        for cm in self.comms:
            a, b, s = len(cm.inputs), len(cm.out_shape), len(cm.scratch)
            phases.append(cm.bind(c_ins[:a], c_outs[:b], c_sems[:s]))
            c_ins, c_outs, c_sems = c_ins[a:], c_outs[b:], c_sems[s:]
        return ins, outs, scratch, phases


def _before_step(phases, step, n_steps):
    if not phases:
        return

    @pl.when(step == 0)
    def _():
        for start, _, _ in phases:
            start()

    @pl.when(step == n_steps // 2)
    def _():
        for _, relay, _ in phases:
            relay()


def _after_step(phases, step, n_steps):
    if not phases:
        return

    @pl.when(step == n_steps - 1)
    def _():
        for _, _, finish in phases:
            finish()


def pair_sum(part, got, owners, name):
    m, n = got.shape[1:]

    def body(own_ref, mine_ref, got_ref, out_ref):
        del own_ref
        out_ref[...] = (mine_ref[...].astype(F32) + got_ref[...].astype(F32)).astype(out_ref.dtype)

    return pl.pallas_call(
        body, name=name,
        grid_spec=pltpu.PrefetchScalarGridSpec(
            num_scalar_prefetch=1, grid=(4,),
            in_specs=[pl.BlockSpec((m, n), lambda r, own: (own[r], 0)),
                      pl.BlockSpec((None, m, n), lambda r, own: (r, 0, 0))],
            out_specs=pl.BlockSpec((None, m, n), lambda r, own: (r, 0, 0))),
        out_shape=jax.ShapeDtypeStruct((4, m, n), got.dtype),
        compiler_params=_cparams(("arbitrary",)),
    )(owners, part, got)


def in_proj(x, g_pre, wt, bias, comms=()):
    s, d = x.shape
    tm = _tile(s, (512, 256, 128))
    tn = 768
    ni, nj = s // tm, D_IN // tn
    hosted = _Hosted(comms)

    def body(*refs):
        (x_ref, g_ref, w_ref, b_ref), (proj_ref, h_ref), _, phases = hosted.split(refs, 4, 2, 0)
        step = pl.program_id(0) * nj + pl.program_id(1)
        _before_step(phases, step, ni * nj)

        @pl.when(pl.program_id(1) == 0)
        def _():
            xv = x_ref[...]
            r = lax.rsqrt(jnp.mean(xv * xv, axis=-1, keepdims=True) + EPS)
            h_ref[...] = (xv * r * g_ref[...]).astype(BF16)

        acc = _dot(h_ref[...], w_ref[...], 1, 1)
        proj_ref[...] = (acc + b_ref[...]).astype(BF16)
        _after_step(phases, step, ni * nj)

    return pl.pallas_call(
        body, name="in_proj", grid=(ni, nj),
        in_specs=[pl.BlockSpec((tm, d), lambda i, j: (i, 0)),
                  pl.BlockSpec((1, d), lambda i, j: (0, 0)),
                  pl.BlockSpec((tn, d), lambda i, j: (j, 0)),
                  pl.BlockSpec((1, tn), lambda i, j: (0, j))] + hosted.in_specs,
        out_specs=[pl.BlockSpec((tm, tn), lambda i, j: (i, j)),
                   pl.BlockSpec((tm, d), lambda i, j: (i, 0))] + hosted.out_specs,
        out_shape=[jax.ShapeDtypeStruct((s, D_IN), BF16), jax.ShapeDtypeStruct((s, d), BF16)] + hosted.out_shape,
        scratch_shapes=hosted.scratch,
        compiler_params=_cparams(("arbitrary", "arbitrary")),
    )(x, g_pre, wt, bias, *hosted.inputs)


def in_proj_gather(x, pos_col, freq, sign, g_pre, wt_shard, bias):
    s, d = x.shape
    tm = _tile(s, (512, 256, 128))
    nt = s // tm
    m = wt_shard.shape[0]
    half = D_IN // 2
    xi = lax.axis_index("x")
    order = jnp.stack([xi, 1 - xi]).astype(jnp.int32)

    def body(order_ref, x_ref, pos_ref, freq_ref, sign_ref, g_ref, b_ref, shard_ref,
             proj_ref, h_ref, rope_ref, wt_ref, w_vmem, send_sems, recv_sems, local_sems):
        del order_ref
        p, i = pl.program_id(0), pl.program_id(1)
        xx, yy, cc = _my_place()
        me, sibling = (xx, yy, cc), (xx, yy, 1 - cc)
        chips = [_chip_of(xx, yy, r) for r in (1, 2, 3)]

        def rows(px, py, pc):
            return wt_ref.at[pl.ds(pl.multiple_of(_index(px, py, pc) * m, 16), m), :]

        def copy(k, block, to, src=None):
            return pltpu.make_async_remote_copy(
                src_ref=rows(*block) if src is None else src, dst_ref=rows(*block),
                send_sem=send_sems.at[k], recv_sem=recv_sems.at[k], device_id=to, device_id_type=MESH)

        def mine():
            return pltpu.make_async_copy(shard_ref, rows(*me), local_sems.at[0])

        def to_sibling():
            return copy(0, me, sibling, src=shard_ref)

        def to_chip(j):
            return copy(1 + j, me, (*chips[j], cc), src=shard_ref)

        def relay(j):
            copy(1 + j, (*chips[j], cc), me).wait_recv()
            copy(4 + j, (*chips[j], cc), sibling).start()

        def relayed(j):
            copy(4 + j, (*chips[j], 1 - cc), me).wait_recv()

        def load_half(which, slot):
            rows_of_half = wt_ref.at[pl.ds(pl.multiple_of(which * half, 16), half), :]
            load = pltpu.make_async_copy(rows_of_half, w_vmem.at[slot], local_sems.at[1 + slot])
            load.start()
            load.wait()

        @pl.when(jnp.logical_and(p == 0, i == 0))
        def _():
            mine().start()
            to_sibling().start()
            to_chip(1).start()
            to_chip(0).start()
            copy(0, sibling, me).wait_recv()
            relay(1)
            relayed(1)
            mine().wait()
            to_chip(1).wait_send()
            to_chip(0).wait_send()
            to_chip(2).start()
            load_half(xx, 0)

        @pl.when(jnp.logical_and(p == 1, i == 0))
        def _():
            relayed(0)
            relayed(2)
            load_half(1 - xx, 1)

        xv = x_ref[...]
        r = lax.rsqrt(jnp.mean(xv * xv, axis=-1, keepdims=True) + EPS)
        hb = (xv * r * g_ref[...]).astype(BF16)
        proj_ref[...] = (_dot(hb, w_vmem[p], 1, 1) + b_ref[...]).astype(BF16)

        @pl.when(p == 0)
        def _():
            h_ref[...] = hb
            ang = pos_ref[...].astype(F32) * freq_ref[...]
            rope_ref[:, :LANES] = jnp.cos(ang)
            rope_ref[:, LANES:] = jnp.sin(ang) * sign_ref[...]

        @pl.when(jnp.logical_and(p == 0, i == nt - 1))
        def _():
            relay(0)
            relay(2)

        @pl.when(jnp.logical_and(p == 1, i == nt - 1))
        def _():
            to_sibling().wait_send()
            to_chip(2).wait_send()
            for j in range(3):
                copy(4 + j, (*chips[j], cc), sibling).wait_send()

    const = lambda p, i, o: (0, 0)
    once = lambda p, i, o: (jnp.where(p == 0, i, nt - 1), 0)
    return pl.pallas_call(
        body, name="in_proj_gather",
        grid_spec=pltpu.PrefetchScalarGridSpec(
            num_scalar_prefetch=1, grid=(2, nt),
            in_specs=[pl.BlockSpec((tm, d), lambda p, i, o: (i, 0)),
                      pl.BlockSpec((tm, 1), lambda p, i, o: (i, 0)),
                      pl.BlockSpec((1, LANES), const),
                      pl.BlockSpec((1, LANES), const),
                      pl.BlockSpec((1, d), const),
                      pl.BlockSpec((1, half), lambda p, i, o: (0, o[p])),
                      ANY],
            out_specs=[pl.BlockSpec((tm, half), lambda p, i, o: (i, o[p])),
                       pl.BlockSpec((tm, d), once),
                       pl.BlockSpec((tm, 2 * LANES), once),
                       ANY],
            scratch_shapes=[pltpu.VMEM((2, half, d), BF16), pltpu.SemaphoreType.DMA((7,)),
                            pltpu.SemaphoreType.DMA((7,)), pltpu.SemaphoreType.DMA((3,))]),
        out_shape=[jax.ShapeDtypeStruct((s, D_IN), BF16), jax.ShapeDtypeStruct((s, d), BF16),
                   jax.ShapeDtypeStruct((s, 2 * LANES), F32), jax.ShapeDtypeStruct((D_IN, d), BF16)],
        compiler_params=_cparams(("arbitrary", "arbitrary")),
    )(order, x, pos_col, freq, sign, g_pre, bias, wt_shard)


def out_proj_loss(cat, w_out, x, target, g_post):
    s, d = x.shape
    tm = _tile(s, (256, 128))

    def body(cat_ref, w_ref, x_ref, t_ref, g_ref, dy_ref, dout_ref, dg_ref, loss_ref):
        @pl.when(pl.program_id(0) == 0)
        def _():
            dg_ref[...] = jnp.zeros_like(dg_ref)
            loss_ref[...] = jnp.zeros_like(loss_ref)

        g = g_ref[...]
        y_all = _dot(cat_ref[...], w_ref[...], 1, 0)
        for c0 in range(0, tm, CHUNK):
            rows = slice(c0, c0 + CHUNK)
            yv = y_all[rows, :]
            r = lax.rsqrt(jnp.mean(yv * yv, axis=-1, keepdims=True) + EPS)
            nrm = yv * r
            err = x_ref[rows, :] + nrm * g - t_ref[rows, :]
            loss_ref[...] += 0.5 * jnp.sum(jnp.sum(err * err, axis=-1, keepdims=True), axis=0, keepdims=True) / d
            dout = err * (1.0 / d)
            dout_ref[rows, :] = dout
            dg_ref[...] += jnp.sum(dout * nrm, axis=0, keepdims=True)
            dn = dout * g
            dy = r * (dn - nrm * jnp.mean(dn * nrm, axis=-1, keepdims=True))
            dy_ref[rows, :] = dy.astype(BF16)

    return pl.pallas_call(
        body, name="out_proj_loss", grid=(s // tm,),
        in_specs=[pl.BlockSpec((tm, d), lambda i: (i, 0)),
                  pl.BlockSpec((d, d), lambda i: (0, 0)),
                  pl.BlockSpec((tm, d), lambda i: (i, 0)),
                  pl.BlockSpec((tm, d), lambda i: (i, 0)),
                  pl.BlockSpec((1, d), lambda i: (0, 0))],
        out_specs=[pl.BlockSpec((tm, d), lambda i: (i, 0)),
                   pl.BlockSpec((tm, d), lambda i: (i, 0)),
                   pl.BlockSpec((1, d), lambda i: (0, 0)),
                   pl.BlockSpec((1, LANES), lambda i: (0, 0))],
        out_shape=[jax.ShapeDtypeStruct((s, d), BF16), jax.ShapeDtypeStruct((s, d), F32),
                   jax.ShapeDtypeStruct((1, d), F32), jax.ShapeDtypeStruct((1, LANES), F32)],
        compiler_params=_cparams(("arbitrary",)),
    )(cat, w_out, x, target, g_post)


def matmul_nt(a, b, name):
    m, k = a.shape
    n = b.shape[0]
    tm = _tile(m, (512, 256, 128))

    def body(a_ref, b_ref, o_ref):
        o_ref[...] = _dot(a_ref[...], b_ref[...], 1, 1).astype(o_ref.dtype)

    return pl.pallas_call(
        body, name=name, grid=(m // tm,),
        in_specs=[pl.BlockSpec((tm, k), lambda i: (i, 0)), pl.BlockSpec((n, k), lambda i: (0, 0))],
        out_specs=pl.BlockSpec((tm, n), lambda i: (i, 0)),
        out_shape=jax.ShapeDtypeStruct((m, n), BF16),
        compiler_params=_cparams(("arbitrary",)),
    )(a, b)


def matmul_tn(a, b, tm, name, comms=()):
    k, m = a.shape
    n = b.shape[1]
    steps = m // tm
    hosted = _Hosted(comms)

    def body(*refs):
        (a_ref, b_hbm), (o_ref, cs_ref), (b_ref, b_sem), phases = hosted.split(refs, 2, 2, 2)
        step = pl.program_id(0)
        _before_step(phases, step, steps)

        @pl.when(step == 0)
        def _():
            load = pltpu.make_async_copy(b_hbm, b_ref, b_sem)
            load.start()
            load.wait()

        o_ref[...] = _dot(a_ref[...], b_ref[...], 0, 0).astype(o_ref.dtype)
        rows = _tile(k, (512, 128))
        cs = jnp.zeros((1, tm), F32)
        for r0 in range(0, k, rows):
            cs = cs + jnp.sum(a_ref[r0:r0 + rows, :].astype(F32), axis=0, keepdims=True)
        cs_ref[...] = cs
        _after_step(phases, step, steps)

    return pl.pallas_call(
        body, name=name, grid=(steps,),
        in_specs=[pl.BlockSpec((k, tm), lambda i: (0, i)), ANY] + hosted.in_specs,
        out_specs=[pl.BlockSpec((tm, n), lambda i: (i, 0)), pl.BlockSpec((1, tm), lambda i: (0, i))] + hosted.out_specs,
        out_shape=[jax.ShapeDtypeStruct((m, n), BF16), jax.ShapeDtypeStruct((1, m), F32)] + hosted.out_shape,
        scratch_shapes=[pltpu.VMEM((k, n), b.dtype), pltpu.SemaphoreType.DMA] + hosted.scratch,
        compiler_params=_cparams(("arbitrary",)),
    )(a, b, *hosted.inputs)


def in_proj_bwd(dproj, wt, x, g_pre, dout, comms=()):
    s, d = x.shape
    tm = _tile(s, (512, 256, 128))
    steps = s // tm
    nsub = tm // CHUNK
    hosted = _Hosted(comms)

    def body(*refs):
        ((dp_ref, w_hbm, x_hbm, g_ref, dout_hbm), (gx_hbm, dg_ref),
         (w_ref, w_sem, xbuf, dbuf, gbuf, in_sems, out_sems), phases) = hosted.split(refs, 5, 2, 7)
        step = pl.program_id(0)
        _before_step(phases, step, steps)

        def rows_of(ref, c):
            return ref.at[pl.ds(pl.multiple_of(step * tm + c * CHUNK, CHUNK), CHUNK), :]

        def fetches(c):
            slot = c % 2
            return (pltpu.make_async_copy(rows_of(x_hbm, c), xbuf.at[slot], in_sems.at[slot]),
                    pltpu.make_async_copy(rows_of(dout_hbm, c), dbuf.at[slot], in_sems.at[2 + slot]))

        def put(c):
            return pltpu.make_async_copy(gbuf.at[c % 2], rows_of(gx_hbm, c), out_sems.at[c % 2])

        for cp in fetches(0):
            cp.start()

        @pl.when(step == 0)
        def _():
            dg_ref[...] = jnp.zeros_like(dg_ref)
            load = pltpu.make_async_copy(w_hbm, w_ref, w_sem)
            load.start()
            load.wait()

        dh_all = _dot(dp_ref[...], w_ref[...], 1, 0)
        for c in range(nsub):
            slot = c % 2
            if c + 1 < nsub:
                for cp in fetches(c + 1):
                    cp.start()
            for cp in fetches(c):
                cp.wait()
            if c >= 2:
                put(c - 2).wait()
            dh = dh_all[c * CHUNK:(c + 1) * CHUNK, :]
            xv = xbuf[slot]
            r = lax.rsqrt(jnp.mean(xv * xv, axis=-1, keepdims=True) + EPS)
            xn = xv * r
            dg_ref[...] += jnp.sum(dh * xn, axis=0, keepdims=True)
            dn = dh * g_ref[...]
            gbuf[slot] = dbuf[slot] + r * (dn - xn * jnp.mean(dn * xn, axis=-1, keepdims=True))
            put(c).start()
        for c in range(max(nsub - 2, 0), nsub):
            put(c).wait()

        _after_step(phases, step, steps)

    side = pltpu.VMEM((2, CHUNK, d), F32)
    return pl.pallas_call(
        body, name="in_proj_bwd", grid=(steps,),
        in_specs=[pl.BlockSpec((tm, D_IN), lambda i: (i, 0)), ANY, ANY,
                  pl.BlockSpec((1, d), lambda i: (0, 0)), ANY] + hosted.in_specs,
        out_specs=[ANY, pl.BlockSpec((1, d), lambda i: (0, 0))] + hosted.out_specs,
        out_shape=[jax.ShapeDtypeStruct((s, d), F32), jax.ShapeDtypeStruct((1, d), F32)] + hosted.out_shape,
        scratch_shapes=[pltpu.VMEM((D_IN, d), BF16), pltpu.SemaphoreType.DMA, side, side, side,
                        pltpu.SemaphoreType.DMA((4,)), pltpu.SemaphoreType.DMA((2,))] + hosted.scratch,
        compiler_params=_cparams(("arbitrary",)),
    )(dproj, wt, x, g_pre, dout, *hosted.inputs)


def _lane_iota(shape):
    return lax.broadcasted_iota(jnp.int32, shape, len(shape) - 1)


def rope_tables(pos_col, freq, sign, comms=()):
    s = pos_col.shape[0]
    tr = _tile(s, (512, 256, 128))
    hosted = _Hosted(comms)

    def body(*refs):
        (pos_ref, freq_ref, sign_ref), (out_ref,), _, phases = hosted.split(refs, 3, 1, 0)
        _before_step(phases, pl.program_id(0), s // tr)
        ang = pos_ref[...].astype(F32) * freq_ref[...]
        out_ref[:, :LANES] = jnp.cos(ang)
        out_ref[:, LANES:] = jnp.sin(ang) * sign_ref[...]
        _after_step(phases, pl.program_id(0), s // tr)

    return pl.pallas_call(
        body, name="rope_tables", grid=(s // tr,),
        in_specs=[pl.BlockSpec((tr, 1), lambda i: (i, 0)), pl.BlockSpec((1, LANES), lambda i: (0, 0)),
                  pl.BlockSpec((1, LANES), lambda i: (0, 0))] + hosted.in_specs,
        out_specs=[pl.BlockSpec((tr, 2 * LANES), lambda i: (i, 0))] + hosted.out_specs,
        out_shape=[jax.ShapeDtypeStruct((s, 2 * LANES), F32)] + hosted.out_shape,
        scratch_shapes=hosted.scratch,
        compiler_params=_cparams(("arbitrary",)),
    )(pos_col, freq, sign, *hosted.inputs)


def _partner(v):
    low = (_lane_iota(v.shape) % HEAD_DIM) < (HEAD_DIM // 2)
    return jnp.where(low, pltpu.roll(v, LANES - HEAD_DIM // 2, 1), pltpu.roll(v, HEAD_DIM // 2, 1))


def _rope(v, cos, sin_signed):
    return v * cos + _partner(v) * sin_signed


def _rope_transposed(dv, cos, sin_signed):
    return dv * cos - _partner(dv) * sin_signed


def _both_halves(v, kv_head):
    keep = (_lane_iota(v.shape) >= HEAD_DIM) if kv_head else (_lane_iota(v.shape) < HEAD_DIM)
    return jnp.where(keep, v, pltpu.roll(v, HEAD_DIM, 1))


def _fold_halves(acc):
    return acc + pltpu.roll(acc, HEAD_DIM, 1)


def _by_half(a, b):
    shape = jnp.broadcast_shapes(jnp.shape(a), jnp.shape(b))
    return jnp.where(_lane_iota(shape) < HEAD_DIM, a, b)


def _stack_heads(pair):
    return jnp.concatenate([_by_half(pair, 0.0), _by_half(0.0, pair)], axis=0)


def _band_bias(has_prev):
    i = lax.broadcasted_iota(jnp.int32, (2 * CHUNK, 2 * CHUNK), 0) % CHUNK
    j = lax.broadcasted_iota(jnp.int32, (2 * CHUNK, 2 * CHUNK), 1)
    band = jnp.logical_and(j > i, j <= i + CHUNK)
    return jnp.where(jnp.logical_and(band, jnp.logical_or(j >= CHUNK, has_prev)), 0.0, NEG)


def _probs(qm2, kk2, bias, sink_col):
    sc = _dot(qm2, kk2, 1, 1) + bias
    mx = jnp.maximum(jnp.max(sc, axis=-1, keepdims=True), sink_col)
    p = jnp.exp(sc - mx)
    es = jnp.exp(sink_col - mx)
    inv = 1.0 / (jnp.sum(p, axis=-1, keepdims=True) + es)
    return p * inv, es * inv


def _sink_col(sinks_ref, pair):
    row = lax.broadcasted_iota(jnp.int32, (2 * CHUNK, 1), 0)
    return jnp.where(row < CHUNK, sinks_ref[2 * pair], sinks_ref[2 * pair + 1])


def _layer_norm_parts(v):
    mu = jnp.mean(v, axis=-1, keepdims=True)
    xc = v - mu
    rstd = lax.rsqrt(jnp.mean(xc * xc, axis=-1, keepdims=True) + EPS)
    return xc * rstd, rstd


def _masked_spatial(w_ref, g):
    t = lax.broadcasted_iota(jnp.int32, (CHUNK, CHUNK), 0)
    sidx = lax.broadcasted_iota(jnp.int32, (CHUNK, CHUNK), 1)
    return jnp.where(t >= sidx, w_ref[g], 0.0).astype(BF16)


def _keys_values(kv_ref, kvp_ref, rope_ref, ropep_ref):
    cos_c, sin_c = rope_ref[:, :LANES], rope_ref[:, LANES:]
    cos_p, sin_p = ropep_ref[:, :LANES], ropep_ref[:, LANES:]
    k_c = _rope(kv_ref[:, :D_KV].astype(F32), cos_c, sin_c)
    k_p = _rope(kvp_ref[:, :D_KV].astype(F32), cos_p, sin_p)
    keys = jnp.concatenate([k_p, k_c], axis=0)
    vals = jnp.concatenate([kvp_ref[:, D_KV:], kv_ref[:, D_KV:]], axis=0).astype(F32)
    return keys, vals, (cos_c, sin_c, cos_p, sin_p)


def mixer_fwd(proj, rope, ln_g, ln_b, w_sp, b_sp_rows, sinks, comms=()):
    s = proj.shape[0]
    nb = s // CHUNK
    hosted = _Hosted(comms)

    def body(sinks_ref, *refs):
        ((proj_ref, kvp_ref, rope_ref, ropep_ref, lng_ref, lnb_ref, w_ref, b_ref), (cat_ref,), _,
         phases) = hosted.split(refs, 8, 1, 0)
        n = pl.program_id(0)
        _before_step(phases, n, nb)
        xhat, _ = _layer_norm_parts(proj_ref[:, OFF_V:OFF_V + D_GMLP].astype(F32))
        vnb = (xhat * lng_ref[...] + lnb_ref[...]).astype(BF16)
        for g in range(GROUPS):
            cols = slice(g * CHUNK, (g + 1) * CHUNK)
            mixed = _dot(_masked_spatial(w_ref, g), vnb[:, cols], 1, 0) + b_ref[g]
            za = proj_ref[:, OFF_ZA + g * CHUNK:OFF_ZA + (g + 1) * CHUNK].astype(F32)
            u = proj_ref[:, OFF_U + g * CHUNK:OFF_U + (g + 1) * CHUNK].astype(F32)
            cat_ref[:, cols] = (u * mixed * (za * _sigmoid(za))).astype(BF16)
        kv_ref = proj_ref.at[:, OFF_K:OFF_K + 2 * D_KV]
        keys, vals, (cos_c, sin_c, _, _) = _keys_values(kv_ref, kvp_ref, rope_ref, ropep_ref)
        cos_q, sin_q = cos_c * SCALE, sin_c * SCALE
        bias = _band_bias(n > 0)
        for pair in range(N_PAIRS):
            kvh = pair // PAIRS_PER_KV
            kk2 = _both_halves(keys, kvh).astype(BF16)
            vv2 = _both_halves(vals, kvh).astype(BF16)
            qcols = slice(OFF_Q + pair * LANES, OFF_Q + (pair + 1) * LANES)
            q_pair = _rope(proj_ref[:, qcols].astype(F32), cos_q, sin_q)
            p, _ = _probs(_stack_heads(q_pair).astype(BF16), kk2, bias, _sink_col(sinks_ref, pair))
            o2 = _dot(p.astype(BF16), vv2, 1, 0)
            out_pair = _by_half(o2[:CHUNK], o2[CHUNK:])
            zb = proj_ref[:, OFF_ZB + pair * LANES:OFF_ZB + (pair + 1) * LANES].astype(F32)
            cat_ref[:, D_GMLP + pair * LANES:D_GMLP + (pair + 1) * LANES] = (
                out_pair * (zb * _sigmoid(zb))).astype(BF16)
        _after_step(phases, n, nb)

    prev = lambda n, *_: (jnp.maximum(n - 1, 0), 0)
    kv_block = OFF_K // (2 * D_KV)
    return pl.pallas_call(
        body, name="mixer_fwd",
        grid_spec=pltpu.PrefetchScalarGridSpec(
            num_scalar_prefetch=1, grid=(nb,),
            in_specs=[pl.BlockSpec((CHUNK, D_IN), lambda n, *_: (n, 0)),
                      pl.BlockSpec((CHUNK, 2 * D_KV), lambda n, *_: (jnp.maximum(n - 1, 0), kv_block)),
                      pl.BlockSpec((CHUNK, 2 * LANES), lambda n, *_: (n, 0)),
                      pl.BlockSpec((CHUNK, 2 * LANES), prev),
                      pl.BlockSpec((1, D_GMLP), lambda n, *_: (0, 0)),
                      pl.BlockSpec((1, D_GMLP), lambda n, *_: (0, 0)),
                      pl.BlockSpec((GROUPS, CHUNK, CHUNK), lambda n, *_: (0, 0, 0)),
                      pl.BlockSpec((GROUPS, CHUNK, CHUNK), lambda n, *_: (0, 0, 0))] + hosted.in_specs,
            out_specs=[pl.BlockSpec((CHUNK, D_GMLP + D_ATTN), lambda n, *_: (n, 0))] + hosted.out_specs,
            scratch_shapes=hosted.scratch),
        out_shape=[jax.ShapeDtypeStruct((s, D_GMLP + D_ATTN), BF16)] + hosted.out_shape,
        compiler_params=_cparams(("arbitrary",)),
    )(sinks, proj, proj, rope, rope, ln_g, ln_b, w_sp, b_sp_rows, *hosted.inputs)


def mixer_bwd(proj, dcat, rope, ln_g, ln_b, w_sp, b_sp_rows, sinks):
    s = proj.shape[0]
    nb = s // CHUNK

    def body(sinks_ref, proj_ref, kvp_ref, dcat_ref, rope_ref, ropep_ref, lng_ref, lnb_ref, w_ref, b_ref,
             dproj_ref, dw_ref, db_ref, dlng_ref, dlnb_ref, dsink_ref,
             pend_ref, pend_kv_ref, dbacc_ref):
        n = pl.program_id(0)

        @pl.when(n == 0)
        def _():
            dw_ref[...] = jnp.zeros_like(dw_ref)
            dbacc_ref[...] = jnp.zeros_like(dbacc_ref)
            dlng_ref[...] = jnp.zeros_like(dlng_ref)
            dlnb_ref[...] = jnp.zeros_like(dlnb_ref)
            dsink_ref[...] = jnp.zeros_like(dsink_ref)

        def flush(dkv_prev):
            @pl.when(n > 0)
            def _():
                dproj_ref[...] = pend_ref[...]
                dproj_ref[:, OFF_K:OFF_K + 2 * D_KV] = (pend_kv_ref[...] + dkv_prev).astype(BF16)

        @pl.when(n < nb)
        def _():
            kv_ref = proj_ref.at[:, OFF_K:OFF_K + 2 * D_KV]
            keys, vals, (cos_c, sin_c, cos_p, sin_p) = _keys_values(kv_ref, kvp_ref, rope_ref, ropep_ref)
            cos_q, sin_q = cos_c * SCALE, sin_c * SCALE
            bias = _band_bias(n > 0)
            lane_row = _lane_iota((1, LANES))
            dsink = jnp.zeros((1, LANES), F32)
            dk_heads, dv_heads = [], []
            dq_pairs, dzb_pairs = [], []
            for kvh in range(N_KV_HEADS):
                kk2 = _both_halves(keys, kvh).astype(BF16)
                vv2 = _both_halves(vals, kvh).astype(BF16)
                dkk = jnp.zeros((2 * CHUNK, LANES), F32)
                dvv = jnp.zeros((2 * CHUNK, LANES), F32)
                for pair in range(kvh * PAIRS_PER_KV, (kvh + 1) * PAIRS_PER_KV):
                    qcols = slice(OFF_Q + pair * LANES, OFF_Q + (pair + 1) * LANES)
                    q_pair = _rope(proj_ref[:, qcols].astype(F32), cos_q, sin_q)
                    qm2 = _stack_heads(q_pair).astype(BF16)
                    p, p_sink = _probs(qm2, kk2, bias, _sink_col(sinks_ref, pair))
                    pb = p.astype(BF16)
                    o2 = _dot(pb, vv2, 1, 0)
                    out_pair = _by_half(o2[:CHUNK], o2[CHUNK:])
                    zb = proj_ref[:, OFF_ZB + pair * LANES:OFF_ZB + (pair + 1) * LANES].astype(F32)
                    sg = _sigmoid(zb)
                    dyb = dcat_ref[:, D_GMLP + pair * LANES:D_GMLP + (pair + 1) * LANES].astype(F32)
                    dzb_pairs.append((dyb * out_pair * (sg * (1.0 + zb * (1.0 - sg)))).astype(BF16))
                    dom2 = _stack_heads(dyb * (zb * sg)).astype(BF16)
                    dp = _dot(dom2, vv2, 1, 1)
                    delta = jnp.sum(p * dp, axis=-1, keepdims=True)
                    ds = p * (dp - delta)
                    dsk = -(p_sink * delta)
                    dsink = dsink + jnp.where(lane_row == 2 * pair,
                                              jnp.sum(dsk[:CHUNK], axis=0, keepdims=True), 0.0)
                    dsink = dsink + jnp.where(lane_row == 2 * pair + 1,
                                              jnp.sum(dsk[CHUNK:], axis=0, keepdims=True), 0.0)
                    dsb = ds.astype(BF16)
                    dq2 = _dot(dsb, kk2, 1, 0)
                    dq_pairs.append(_rope_transposed(_by_half(dq2[:CHUNK], dq2[CHUNK:]), cos_q, sin_q).astype(BF16))
                    dkk = dkk + _dot(dsb, qm2, 0, 0)
                    dvv = dvv + _dot(pb, dom2, 0, 0)
                dk_heads.append(_fold_halves(dkk))
                dv_heads.append(_fold_halves(dvv))
            dk_rot = _by_half(dk_heads[0], dk_heads[1])
            dv_all = _by_half(dv_heads[0], dv_heads[1])
            dk_p = _rope_transposed(dk_rot[:CHUNK], cos_p, sin_p)
            dk_c = _rope_transposed(dk_rot[CHUNK:], cos_c, sin_c)
            flush(jnp.concatenate([dk_p, dv_all[:CHUNK]], axis=1))
            dsink_ref[...] += dsink
            pend_kv_ref[...] = jnp.concatenate([dk_c, dv_all[CHUNK:]], axis=1)
            for pair in range(N_PAIRS):
                pend_ref[:, OFF_Q + pair * LANES:OFF_Q + (pair + 1) * LANES] = dq_pairs[pair]
                pend_ref[:, OFF_ZB + pair * LANES:OFF_ZB + (pair + 1) * LANES] = dzb_pairs[pair]
            xhat, rstd = _layer_norm_parts(proj_ref[:, OFF_V:OFF_V + D_GMLP].astype(F32))
            lng = lng_ref[...]
            vnb = (xhat * lng + lnb_ref[...]).astype(BF16)
            dvn_cols = []
            for g in range(GROUPS):
                cols = slice(g * CHUNK, (g + 1) * CHUNK)
                wm = _masked_spatial(w_ref, g)
                mixed = _dot(wm, vnb[:, cols], 1, 0) + b_ref[g]
                za = proj_ref[:, OFF_ZA + g * CHUNK:OFF_ZA + (g + 1) * CHUNK].astype(F32)
                u = proj_ref[:, OFF_U + g * CHUNK:OFF_U + (g + 1) * CHUNK].astype(F32)
                dya = dcat_ref[:, cols].astype(F32)
                sg = _sigmoid(za)
                sz = za * sg
                pend_ref[:, OFF_U + g * CHUNK:OFF_U + (g + 1) * CHUNK] = (dya * mixed * sz).astype(BF16)
                pend_ref[:, OFF_ZA + g * CHUNK:OFF_ZA + (g + 1) * CHUNK] = (
                    dya * u * mixed * (sg * (1.0 + za * (1.0 - sg)))).astype(BF16)
                dmixed = dya * u * sz
                dmb = dmixed.astype(BF16)
                dbacc_ref[g] += dmixed
                dw_ref[g] += _dot(dmb, vnb[:, cols], 1, 1)
                dvn_cols.append(_dot(wm, dmb, 0, 0))
            dvn = jnp.concatenate(dvn_cols, axis=1)
            dlng_ref[...] += jnp.sum(dvn * xhat, axis=0, keepdims=True)
            dlnb_ref[...] += jnp.sum(dvn, axis=0, keepdims=True)
            dxh = dvn * lng
            dv = rstd * (dxh - jnp.mean(dxh, axis=-1, keepdims=True)
                         - xhat * jnp.mean(dxh * xhat, axis=-1, keepdims=True))
            pend_ref[:, OFF_V:OFF_V + D_GMLP] = dv.astype(BF16)

        @pl.when(n == nb)
        def _():
            flush(jnp.zeros((CHUNK, 2 * D_KV), F32))
            t = lax.broadcasted_iota(jnp.int32, (CHUNK, CHUNK), 0)
            sidx = lax.broadcasted_iota(jnp.int32, (CHUNK, CHUNK), 1)
            lane = _lane_iota((CHUNK, LANES))
            dbt = jnp.zeros((CHUNK, LANES), F32)
            for g in range(GROUPS):
                dw_ref[g] = jnp.where(t >= sidx, dw_ref[g], 0.0)
                dbt = jnp.where(lane == g, jnp.sum(dbacc_ref[g], axis=-1, keepdims=True), dbt)
            db_ref[...] = jnp.transpose(dbt)[:GROUPS, :]

    cur = lambda n, *_: (jnp.minimum(n, nb - 1), 0)
    prev = lambda n, *_: (jnp.clip(n - 1, 0, nb - 1), 0)
    kv_block = OFF_K // (2 * D_KV)
    const2 = lambda n, *_: (0, 0)
    const3 = lambda n, *_: (0, 0, 0)
    return pl.pallas_call(
        body, name="mixer_bwd",
        grid_spec=pltpu.PrefetchScalarGridSpec(
            num_scalar_prefetch=1, grid=(nb + 1,),
            in_specs=[pl.BlockSpec((CHUNK, D_IN), cur),
                      pl.BlockSpec((CHUNK, 2 * D_KV), lambda n, *_: (jnp.clip(n - 1, 0, nb - 1), kv_block)),
                      pl.BlockSpec((CHUNK, D_GMLP + D_ATTN), cur),
                      pl.BlockSpec((CHUNK, 2 * LANES), cur),
                      pl.BlockSpec((CHUNK, 2 * LANES), prev),
                      pl.BlockSpec((1, D_GMLP), const2),
                      pl.BlockSpec((1, D_GMLP), const2),
                      pl.BlockSpec((GROUPS, CHUNK, CHUNK), const3),
                      pl.BlockSpec((GROUPS, CHUNK, CHUNK), const3)],
            out_specs=[pl.BlockSpec((CHUNK, D_IN), lambda n, *_: (jnp.maximum(n - 1, 0), 0)),
                       pl.BlockSpec((GROUPS, CHUNK, CHUNK), const3),
                       pl.BlockSpec((GROUPS, CHUNK), const2),
                       pl.BlockSpec((1, D_GMLP), const2),
                       pl.BlockSpec((1, D_GMLP), const2),
                       pl.BlockSpec((1, LANES), const2)],
            scratch_shapes=[pltpu.VMEM((CHUNK, D_IN), BF16), pltpu.VMEM((CHUNK, 2 * D_KV), F32),
                            pltpu.VMEM((GROUPS, CHUNK, CHUNK), F32)]),
        out_shape=[jax.ShapeDtypeStruct((s, D_IN), BF16),
                   jax.ShapeDtypeStruct((GROUPS, CHUNK, CHUNK), F32),
                   jax.ShapeDtypeStruct((GROUPS, CHUNK), F32),
                   jax.ShapeDtypeStruct((1, D_GMLP), F32),
                   jax.ShapeDtypeStruct((1, D_GMLP), F32),
                   jax.ShapeDtypeStruct((1, LANES), F32)],
        compiler_params=_cparams(("arbitrary",)),
    )(sinks, proj, proj, dcat, rope, rope, ln_g, ln_b, w_sp, b_sp_rows)


def _adamw_math(w, g, m, v):
    m = ADAM_B1 * m + (1.0 - ADAM_B1) * g
    v = ADAM_B2 * v + (1.0 - ADAM_B2) * (g * g)
    m_hat = m / (1.0 - ADAM_B1 ** ADAM_STEP)
    v_hat = v / (1.0 - ADAM_B2 ** ADAM_STEP)
    delta = -ADAM_LR * (m_hat / (jnp.sqrt(v_hat) + ADAM_EPS) + ADAM_WD * w)
    return delta, m, v


def adamw_shard(terms, w, m, v, name):
    r, c = w.shape
    tr = _tile(r, (224, 256, 128, 8))
    n_terms = len(terms)

    def body(*refs):
        w_ref, m_ref, v_ref, g_ref, d_ref, nm_ref, nv_ref = refs[n_terms:]
        g = None
        for ref, (_, slots) in zip(refs[:n_terms], terms):
            for k in range(slots):
                part = ref[k].astype(F32)
                g = part if g is None else g + part
        g_ref[...] = g
        d_ref[...], nm_ref[...], nv_ref[...] = _adamw_math(w_ref[...], g, m_ref[...], v_ref[...])

    spec = pl.BlockSpec((tr, c), lambda i: (i, 0))
    return pl.pallas_call(
        body, name=name, grid=(r // tr,),
        in_specs=[pl.BlockSpec((slots, tr, c), lambda i: (0, i, 0)) for _, slots in terms] + [spec] * 3,
        out_specs=[spec] * 4, out_shape=[jax.ShapeDtypeStruct((r, c), F32)] * 4,
        compiler_params=_cparams(("arbitrary",)),
    )(*[a for a, _ in terms], w, m, v)


def adamw_small(gathered, lane_windows, params):
    n_par = len(params)

    def body(*refs):
        g_refs = refs[:n_par + 1]
        wmv_refs = refs[n_par + 1:4 * n_par + 1]
        out_refs = refs[4 * n_par + 1:]

        def total(ref):
            acc = ref[0]
            for dev in range(1, N_DEV):
                acc = acc + ref[dev]
            return acc

        for i in range(n_par):
            w_ref, m_ref, v_ref = wmv_refs[3 * i:3 * i + 3]
            g = total(g_refs[i])
            if lane_windows[i] is not None:
                start, size = lane_windows[i]
                g = g[..., start:start + size]
            delta, new_m, new_v = _adamw_math(w_ref[...], g, m_ref[...], v_ref[...])
            for ref, val in zip(out_refs[4 * i:4 * i + 4], (g, delta, new_m, new_v)):
                ref[...] = val
        out_refs[4 * n_par][...] = total(g_refs[n_par])

    flat = [a for wmv in params for a in wmv]
    out_shape = [jax.ShapeDtypeStruct(w.shape, F32) for (w, _, _) in params for _ in range(4)]
    out_shape.append(jax.ShapeDtypeStruct(gathered[-1].shape[1:], F32))
    outs = pl.pallas_call(body, name="adamw_small", out_shape=out_shape, compiler_params=_cparams())(*gathered, *flat)
    return [tuple(outs[4 * i:4 * i + 4]) for i in range(n_par)], outs[-1]


def kernel(x, positions, g_pre, w_in, b_qkv, ln_v_g, ln_v_b, w_spatial, b_spatial, attn_sinks, w_out, g_post, loss_target, m_g_pre, m_w_in, m_b_qkv, m_ln_v_g, m_ln_v_b, m_w_spatial, m_b_spatial, m_attn_sinks, m_w_out, m_g_post, v_g_pre, v_w_in, v_b_qkv, v_ln_v_g, v_ln_v_b, v_w_spatial, v_b_spatial, v_attn_sinks, v_w_out, v_g_post):
    x2, target = x[0], loss_target[0]
    seq = x2.shape[0]
    xi, yi, ci = _my_place()

    wt_shard = w_in[0].T.astype(BF16)
    wo_shard = w_out[0].astype(BF16)
    pos_col = positions.reshape(seq, 1)
    half = HEAD_DIM // 2
    inv_freq = ROPE_THETA ** (-jnp.arange(half, dtype=F32) * (2.0 / HEAD_DIM))
    freq = jnp.tile(inv_freq, LANES // half).reshape(1, LANES)
    sign = jnp.tile(jnp.concatenate([-jnp.ones((half,), F32), jnp.ones((half,), F32)]), LANES // HEAD_DIM)
    sign = sign.reshape(1, LANES)
    bias = jnp.concatenate([jnp.zeros((1, OFF_Q), F32), b_qkv, jnp.zeros((1, D_ATTN), F32)], axis=1)
    proj, h, rope, wt = in_proj_gather(x2, pos_col, freq, sign, g_pre, wt_shard, bias)

    b_rows = jnp.broadcast_to(b_spatial[0][:, :, None], (GROUPS, CHUNK, CHUNK))
    sinks = attn_sinks[0]
    cat, wo = mixer_fwd(proj, rope, ln_v_g, ln_v_b, w_spatial[0], b_rows, sinks, comms=[gather_comm([wo_shard])])
    dy, dout, d_g_post, loss_part = out_proj_loss(cat, wo, x2, target, g_post)

    dcat = matmul_nt(dy, wo, "out_proj_bwd")
    d_wo, _ = matmul_tn(cat, dy, 512, "w_out_grad")
    dproj, d_w_sp, d_b_sp, d_ln_g, d_ln_b, d_sinks = mixer_bwd(
        proj, dcat, rope, ln_v_g, ln_v_b, w_spatial[0], b_rows, sinks)
    d_wt, colsum, parts_wo = matmul_tn(dproj, h, 768, "w_in_grad", comms=[scatter_comm([d_wo])])

    owners = jnp.stack([4 * cx + 2 * cy + ci for cx, cy in (_chip_of(xi, yi, r) for r in range(4))]).astype(jnp.int32)
    (got_wt,) = run_comm(pair_comm([d_wt]), "grad_exchange_pair")
    sum_wt = pair_sum(d_wt, got_wt, owners, "grad_pair_sum_w_in")
    small_parts = [colsum, d_ln_g, d_ln_b, d_w_sp, d_b_sp, d_sinks, d_g_post, loss_part]
    grad_x, d_g_pre, *landed = in_proj_bwd(
        dproj, wt, x2, g_pre, dout, comms=[gather_comm(small_parts, stack=True), chips_comm([sum_wt])])
    far_wt = landed[-1]
    (all_g_pre,) = run_comm(gather_comm([d_g_pre], stack=True, direct=True), "allgather_g_pre_grad")
    gathered = [all_g_pre] + landed[:-1]
    windows = [None, (OFF_Q, D_QKV), None, None, None, None, (0, N_Q_HEADS), None]
    small = [(g_pre, m_g_pre, v_g_pre), (b_qkv, m_b_qkv, v_b_qkv), (ln_v_g, m_ln_v_g, v_ln_v_g),
             (ln_v_b, m_ln_v_b, v_ln_v_b), (w_spatial[0], m_w_spatial[0], v_w_spatial[0]),
             (b_spatial[0], m_b_spatial[0], v_b_spatial[0]), (attn_sinks, m_attn_sinks, v_attn_sinks),
             (g_post, m_g_post, v_g_post)]
    small_out, loss_row = adamw_small(gathered, windows, small)
    lead = [False, False, False, False, True, True, False, False]
    small_out = [tuple(a[None] if ld else a for a in leaf) for leaf, ld in zip(small_out, lead)]

    wt_out = adamw_shard([(sum_wt, 1), (far_wt, 3)], w_in[0].T, m_w_in[0].T, v_w_in[0].T, "adamw_w_in")
    wo_out = adamw_shard([(parts_wo, N_DEV)], w_out[0], m_w_out[0], v_w_out[0], "adamw_w_out")

    def leaves(k):
        gp, bq, lg, lb, ws, bs, sk, gpo = (leaf[k] for leaf in small_out)
        return [gp, wt_out[k].T[None], bq, lg, lb, ws, bs, sk, wo_out[k][None], gpo]

    return (loss_row[0, 0], grad_x[None], *leaves(0), *leaves(1), *leaves(2), *leaves(3))
```

```python
import functools

import jax
import jax.numpy as jnp
from jax import lax
from jax.experimental import pallas as pl
from jax.experimental.pallas import tpu as pltpu

F32 = jnp.float32
BF16 = jnp.bfloat16

D_MODEL = 2048
D_GMLP = 1024
D_ATTN = 1024
CHUNK = 128
GROUPS = 8
HEAD_DIM = 64
N_Q_HEADS = 16
N_KV_HEADS = 2
D_KV = N_KV_HEADS * HEAD_DIM
D_IN = 3 * D_GMLP + D_ATTN + 2 * D_KV + D_ATTN
OFF_U, OFF_V, OFF_ZA = 0, D_GMLP, 2 * D_GMLP
OFF_Q = 3 * D_GMLP
OFF_K = OFF_Q + D_ATTN
OFF_VA = OFF_K + D_KV
OFF_ZB = OFF_VA + D_KV
D_QKV = D_ATTN + 2 * D_KV
ROPE_THETA = 10000.0
EPS = 1e-6
SCALE = HEAD_DIM ** -0.5
NEG = -1e30
N_PAIRS = N_Q_HEADS // 2
PAIRS_PER_KV = N_PAIRS // N_KV_HEADS

ADAM_LR = 0.001
ADAM_B1 = 0.9
ADAM_B2 = 0.999
ADAM_EPS = 1e-08
ADAM_WD = 0.01
ADAM_STEP = 10

N_DEV = 8
LANES = 128
VMEM_LIMIT = 56 * 1024 * 1024

MESH = pl.DeviceIdType.MESH
ANY = pl.BlockSpec(memory_space=pl.ANY)


def _cparams(sem=None):
    return pltpu.CompilerParams(dimension_semantics=sem, vmem_limit_bytes=VMEM_LIMIT)


def _tile(n, prefs):
    for t in prefs:
        if n % t == 0:
            return t
    return n


def _sigmoid(z):
    return 1.0 / (1.0 + jnp.exp(-z))


def _dot(a, b, ca, cb):
    return lax.dot_general(a, b, (((ca,), (cb,)), ((), ())), preferred_element_type=F32)


def _my_place():
    return lax.axis_index("x"), lax.axis_index("y"), lax.axis_index("c")


def _chip_of(x, y, r):
    return (x ^ (r & 1), y ^ (r >> 1))


def _peer(x, y, c, k):
    return (x ^ (k >> 2), y ^ ((k >> 1) & 1), c ^ (k & 1))


def _index(px, py, pc):
    return 4 * px + 2 * py + pc


class _Comm:
    def __init__(self, inputs, out_shape, scratch, bind):
        self.inputs, self.out_shape, self.scratch, self.bind = list(inputs), list(out_shape), list(scratch), bind


def gather_comm(shards, stack=False, direct=False):
    n_arr = len(shards)

    def bind(ins, outs, sems):
        send_sems, recv_sems, local_sems = sems
        x, y, c = _my_place()
        me, sibling = (x, y, c), (x, y, 1 - c)
        chips = [_chip_of(x, y, r) for r in (1, 2, 3)]

        def rows(a, px, py, pc):
            d = _index(px, py, pc)
            if stack:
                return outs[a].at[d]
            m = shards[a].shape[0]
            return outs[a].at[pl.ds(pl.multiple_of(d * m, 8), m), :]

        def copy(a, k, block, to, src=None):
            return pltpu.make_async_remote_copy(
                src_ref=rows(a, *block) if src is None else src, dst_ref=rows(a, *block),
                send_sem=send_sems.at[a * 7 + k], recv_sem=recv_sems.at[a * 7 + k],
                device_id=to, device_id_type=MESH)

        def mine(a):
            return pltpu.make_async_copy(ins[a], rows(a, *me), local_sems.at[a])

        def own_sends(a):
            if direct:
                return [copy(a, k - 1, me, _peer(x, y, c, k), src=ins[a]) for k in range(1, 8)]
            return ([copy(a, 0, me, sibling, src=ins[a])]
                    + [copy(a, 1 + j, me, (*chip, c), src=ins[a]) for j, chip in enumerate(chips)])

        def start():
            for a in range(n_arr):
                mine(a).start()
                for cp in own_sends(a):
                    cp.start()

        def relay():
            if direct:
                return
            for j, chip in enumerate(chips):
                for a in range(n_arr):
                    copy(a, 1 + j, (*chip, c), me).wait_recv()
                    copy(a, 4 + j, (*chip, c), sibling).start()

        def finish():
            for a in range(n_arr):
                if direct:
                    for k in range(1, 8):
                        copy(a, k - 1, _peer(x, y, c, k), me).wait_recv()
                else:
                    copy(a, 0, sibling, me).wait_recv()
                    for j, chip in enumerate(chips):
                        copy(a, 4 + j, (*chip, 1 - c), me).wait_recv()
                        copy(a, 4 + j, (*chip, c), sibling).wait_send()
                for cp in own_sends(a):
                    cp.wait_send()
                mine(a).wait()

        return start, relay, finish

    def gathered(s):
        return (N_DEV, *s.shape) if stack else (N_DEV * s.shape[0], s.shape[1])

    return _Comm(shards, [jax.ShapeDtypeStruct(gathered(s), s.dtype) for s in shards],
                 [pltpu.SemaphoreType.DMA((7 * n_arr,)), pltpu.SemaphoreType.DMA((7 * n_arr,)),
                  pltpu.SemaphoreType.DMA((n_arr,))], bind)


def scatter_comm(parts):
    n_arr = len(parts)

    def bind(ins, outs, sems):
        send_sems, recv_sems, local_sems = sems
        x, y, c = _my_place()
        my_index = _index(x, y, c)

        def block(a, d):
            m = parts[a].shape[0] // N_DEV
            return ins[a].at[pl.ds(pl.multiple_of(d * m, 16), m), :]

        def copy(a, k, slot):
            peer = _peer(x, y, c, k)
            return pltpu.make_async_remote_copy(
                src_ref=block(a, _index(*peer)), dst_ref=outs[a].at[slot],
                send_sem=send_sems.at[a * 7 + k - 1], recv_sem=recv_sems.at[a * 7 + k - 1],
                device_id=peer, device_id_type=MESH)

        def mine(a):
            return pltpu.make_async_copy(block(a, my_index), outs[a].at[my_index], local_sems.at[a])

        def start():
            for a in range(n_arr):
                mine(a).start()
                for k in range(1, 8):
                    copy(a, k, my_index).start()

        def finish():
            for a in range(n_arr):
                for k in range(1, 8):
                    copy(a, k, _index(*_peer(x, y, c, k))).wait_recv()
                    copy(a, k, my_index).wait_send()
                mine(a).wait()

        return start, (lambda: None), finish

    return _Comm(parts, [jax.ShapeDtypeStruct((N_DEV, p.shape[0] // N_DEV, p.shape[1]), p.dtype) for p in parts],
                 [pltpu.SemaphoreType.DMA((7 * n_arr,)), pltpu.SemaphoreType.DMA((7 * n_arr,)),
                  pltpu.SemaphoreType.DMA((n_arr,))], bind)


def pair_comm(parts):
    n_arr = len(parts)

    def bind(ins, outs, sems):
        send_sems, recv_sems = sems
        x, y, c = _my_place()

        def copies():
            out = []
            for a in range(n_arr):
                m = parts[a].shape[0] // N_DEV
                for r in range(4):
                    owner = _index(*_chip_of(x, y, r), 1 - c)
                    out.append(pltpu.make_async_remote_copy(
                        src_ref=ins[a].at[pl.ds(pl.multiple_of(owner * m, 16), m), :], dst_ref=outs[a].at[r],
                        send_sem=send_sems.at[a * 4 + r], recv_sem=recv_sems.at[a * 4 + r],
                        device_id=(x, y, 1 - c), device_id_type=MESH))
            return out

        def start():
            for cp in copies():
                cp.start()

        def finish():
            for cp in copies():
                cp.wait_recv()
                cp.wait_send()

        return start, (lambda: None), finish

    return _Comm(parts, [jax.ShapeDtypeStruct((4, p.shape[0] // N_DEV, p.shape[1]), p.dtype) for p in parts],
                 [pltpu.SemaphoreType.DMA((4 * n_arr,)), pltpu.SemaphoreType.DMA((4 * n_arr,))], bind)


def chips_comm(sums):
    n_arr = len(sums)

    def bind(ins, outs, sems):
        send_sems, recv_sems = sems
        x, y, c = _my_place()

        def copies():
            return [pltpu.make_async_remote_copy(
                src_ref=ins[a].at[r], dst_ref=outs[a].at[r - 1],
                send_sem=send_sems.at[a * 3 + r - 1], recv_sem=recv_sems.at[a * 3 + r - 1],
                device_id=(*_chip_of(x, y, r), c), device_id_type=MESH) for a in range(n_arr) for r in (1, 2, 3)]

        def start():
            for cp in copies():
                cp.start()

        def finish():
            for cp in copies():
                cp.wait_recv()
                cp.wait_send()

        return start, (lambda: None), finish

    return _Comm(sums, [jax.ShapeDtypeStruct((3,) + s.shape[1:], s.dtype) for s in sums],
                 [pltpu.SemaphoreType.DMA((3 * n_arr,)), pltpu.SemaphoreType.DMA((3 * n_arr,))], bind)


def run_comm(comm, name):
    n_in, n_out = len(comm.inputs), len(comm.out_shape)

    def body(*refs):
        start, relay, finish = comm.bind(refs[:n_in], refs[n_in:n_in + n_out], refs[n_in + n_out:])
        start()
        relay()
        finish()

    outs = pl.pallas_call(body, name=name, out_shape=comm.out_shape, in_specs=[ANY] * n_in,
                          out_specs=[ANY] * n_out, scratch_shapes=comm.scratch)(*comm.inputs)
    return list(outs)


class _Hosted:
    def __init__(self, comms):
        self.comms = list(comms)
        self.inputs = [a for cm in self.comms for a in cm.inputs]
        self.out_shape = [s for cm in self.comms for s in cm.out_shape]
        self.scratch = [s for cm in self.comms for s in cm.scratch]
        self.in_specs = [ANY] * len(self.inputs)
        self.out_specs = [ANY] * len(self.out_shape)

    def split(self, refs, n_in, n_out, n_scratch):
        ni, no = len(self.inputs), len(self.out_shape)
        ins, rest = refs[:n_in], refs[n_in:]
        c_ins, rest = rest[:ni], rest[ni:]
        outs, rest = rest[:n_out], rest[n_out:]
        c_outs, rest = rest[:no], rest[no:]
        scratch, c_sems = rest[:n_scratch], rest[n_scratch:]
        phases = []
        for cm in self.comms:
            a, b, s = len(cm.inputs), len(cm.out_shape), len(cm.scratch)
            phases.append(cm.bind(c_ins[:a], c_outs[:b], c_sems[:s]))
            c_ins, c_outs, c_sems = c_ins[a:], c_outs[b:], c_sems[s:]
        return ins, outs, scratch, phases


def _before_step(phases, step, n_steps):
    if not phases:
        return

    @pl.when(step == 0)
    def _():
        for start, _, _ in phases:
            start()

    @pl.when(step == n_steps // 2)
    def _():
        for _, relay, _ in phases:
            relay()


def _after_step(phases, step, n_steps):
    if not phases:
        return

    @pl.when(step == n_steps - 1)
    def _():
        for _, _, finish in phases:
            finish()


def pair_sum(part, got, owners, name):
    m, n = got.shape[1:]

    def body(own_ref, mine_ref, got_ref, out_ref):
        del own_ref
        out_ref[...] = (mine_ref[...].astype(F32) + got_ref[...].astype(F32)).astype(out_ref.dtype)

    return pl.pallas_call(
        body, name=name,
        grid_spec=pltpu.PrefetchScalarGridSpec(
            num_scalar_prefetch=1, grid=(4,),
            in_specs=[pl.BlockSpec((m, n), lambda r, own: (own[r], 0)),
                      pl.BlockSpec((None, m, n), lambda r, own: (r, 0, 0))],
            out_specs=pl.BlockSpec((None, m, n), lambda r, own: (r, 0, 0))),
        out_shape=jax.ShapeDtypeStruct((4, m, n), got.dtype),
        compiler_params=_cparams(("arbitrary",)),
    )(owners, part, got)


def in_proj(x, g_pre, wt, bias, comms=()):
    s, d = x.shape
    tm = _tile(s, (512, 256, 128))
    tn = 768
    ni, nj = s // tm, D_IN // tn
    hosted = _Hosted(comms)

    def body(*refs):
        (x_ref, g_ref, w_ref, b_ref), (proj_ref, h_ref), _, phases = hosted.split(refs, 4, 2, 0)
        step = pl.program_id(0) * nj + pl.program_id(1)
        _before_step(phases, step, ni * nj)

        @pl.when(pl.program_id(1) == 0)
        def _():
            xv = x_ref[...]
            r = lax.rsqrt(jnp.mean(xv * xv, axis=-1, keepdims=True) + EPS)
            h_ref[...] = (xv * r * g_ref[...]).astype(BF16)

        acc = _dot(h_ref[...], w_ref[...], 1, 1)
        proj_ref[...] = (acc + b_ref[...]).astype(BF16)
        _after_step(phases, step, ni * nj)

    return pl.pallas_call(
        body, name="in_proj", grid=(ni, nj),
        in_specs=[pl.BlockSpec((tm, d), lambda i, j: (i, 0)),
                  pl.BlockSpec((1, d), lambda i, j: (0, 0)),
                  pl.BlockSpec((tn, d), lambda i, j: (j, 0)),
                  pl.BlockSpec((1, tn), lambda i, j: (0, j))] + hosted.in_specs,
        out_specs=[pl.BlockSpec((tm, tn), lambda i, j: (i, j)),
                   pl.BlockSpec((tm, d), lambda i, j: (i, 0))] + hosted.out_specs,
        out_shape=[jax.ShapeDtypeStruct((s, D_IN), BF16), jax.ShapeDtypeStruct((s, d), BF16)] + hosted.out_shape,
        scratch_shapes=hosted.scratch,
        compiler_params=_cparams(("arbitrary", "arbitrary")),
    )(x, g_pre, wt, bias, *hosted.inputs)


def in_proj_gather(x, pos_col, freq, sign, g_pre, wt_shard, bias):
    s, d = x.shape
    tm = _tile(s, (512, 256, 128))
    nt = s // tm
    m = wt_shard.shape[0]
    half = D_IN // 2
    xi = lax.axis_index("x")
    order = jnp.stack([xi, 1 - xi]).astype(jnp.int32)

    def body(order_ref, x_ref, pos_ref, freq_ref, sign_ref, g_ref, b_ref, shard_ref,
             proj_ref, h_ref, rope_ref, wt_ref, w_vmem, send_sems, recv_sems, local_sems):
        del order_ref
        p, i = pl.program_id(0), pl.program_id(1)
        xx, yy, cc = _my_place()
        me, sibling = (xx, yy, cc), (xx, yy, 1 - cc)
        chips = [_chip_of(xx, yy, r) for r in (1, 2, 3)]

        def rows(px, py, pc):
            return wt_ref.at[pl.ds(pl.multiple_of(_index(px, py, pc) * m, 16), m), :]

        def copy(k, block, to, src=None):
            return pltpu.make_async_remote_copy(
                src_ref=rows(*block) if src is None else src, dst_ref=rows(*block),
                send_sem=send_sems.at[k], recv_sem=recv_sems.at[k], device_id=to, device_id_type=MESH)

        def mine():
            return pltpu.make_async_copy(shard_ref, rows(*me), local_sems.at[0])

        def to_sibling():
            return copy(0, me, sibling, src=shard_ref)

        def to_chip(j):
            return copy(1 + j, me, (*chips[j], cc), src=shard_ref)

        def relay(j):
            copy(1 + j, (*chips[j], cc), me).wait_recv()
            copy(4 + j, (*chips[j], cc), sibling).start()

        def relayed(j):
            copy(4 + j, (*chips[j], 1 - cc), me).wait_recv()

        def load_half(which, slot):
            rows_of_half = wt_ref.at[pl.ds(pl.multiple_of(which * half, 16), half), :]
            load = pltpu.make_async_copy(rows_of_half, w_vmem.at[slot], local_sems.at[1 + slot])
            load.start()
            load.wait()

        @pl.when(jnp.logical_and(p == 0, i == 0))
        def _():
            mine().start()
            to_sibling().start()
            to_chip(1).start()
            to_chip(0).start()
            copy(0, sibling, me).wait_recv()
            relay(1)
            relayed(1)
            mine().wait()
            to_chip(1).wait_send()
            to_chip(0).wait_send()
            to_chip(2).start()
            load_half(xx, 0)

        @pl.when(jnp.logical_and(p == 1, i == 0))
        def _():
            relayed(0)
            relayed(2)
            load_half(1 - xx, 1)

        xv = x_ref[...]
        r = lax.rsqrt(jnp.mean(xv * xv, axis=-1, keepdims=True) + EPS)
        hb = (xv * r * g_ref[...]).astype(BF16)
        proj_ref[...] = (_dot(hb, w_vmem[p], 1, 1) + b_ref[...]).astype(BF16)

        @pl.when(p == 0)
        def _():
            h_ref[...] = hb
            ang = pos_ref[...].astype(F32) * freq_ref[...]
            rope_ref[:, :LANES] = jnp.cos(ang)
            rope_ref[:, LANES:] = jnp.sin(ang) * sign_ref[...]

        @pl.when(jnp.logical_and(p == 0, i == nt - 1))
        def _():
            relay(0)
            relay(2)

        @pl.when(jnp.logical_and(p == 1, i == nt - 1))
        def _():
            to_sibling().wait_send()
            to_chip(2).wait_send()
            for j in range(3):
                copy(4 + j, (*chips[j], cc), sibling).wait_send()

    const = lambda p, i, o: (0, 0)
    once = lambda p, i, o: (jnp.where(p == 0, i, nt - 1), 0)
    return pl.pallas_call(
        body, name="in_proj_gather",
        grid_spec=pltpu.PrefetchScalarGridSpec(
            num_scalar_prefetch=1, grid=(2, nt),
            in_specs=[pl.BlockSpec((tm, d), lambda p, i, o: (i, 0)),
                      pl.BlockSpec((tm, 1), lambda p, i, o: (i, 0)),
                      pl.BlockSpec((1, LANES), const),
                      pl.BlockSpec((1, LANES), const),
                      pl.BlockSpec((1, d), const),
                      pl.BlockSpec((1, half), lambda p, i, o: (0, o[p])),
                      ANY],
            out_specs=[pl.BlockSpec((tm, half), lambda p, i, o: (i, o[p])),
                       pl.BlockSpec((tm, d), once),
                       pl.BlockSpec((tm, 2 * LANES), once),
                       ANY],
            scratch_shapes=[pltpu.VMEM((2, half, d), BF16), pltpu.SemaphoreType.DMA((7,)),
                            pltpu.SemaphoreType.DMA((7,)), pltpu.SemaphoreType.DMA((3,))]),
        out_shape=[jax.ShapeDtypeStruct((s, D_IN), BF16), jax.ShapeDtypeStruct((s, d), BF16),
                   jax.ShapeDtypeStruct((s, 2 * LANES), F32), jax.ShapeDtypeStruct((D_IN, d), BF16)],
        compiler_params=_cparams(("arbitrary", "arbitrary")),
    )(order, x, pos_col, freq, sign, g_pre, bias, wt_shard)


def out_proj_loss(cat, w_out, x, target, g_post):
    s, d = x.shape
    tm = _tile(s, (256, 128))

    def body(cat_ref, w_ref, x_ref, t_ref, g_ref, dy_ref, dout_ref, dg_ref, loss_ref):
        @pl.when(pl.program_id(0) == 0)
        def _():
            dg_ref[...] = jnp.zeros_like(dg_ref)
            loss_ref[...] = jnp.zeros_like(loss_ref)

        g = g_ref[...]
        y_all = _dot(cat_ref[...], w_ref[...], 1, 0)
        for c0 in range(0, tm, CHUNK):
            rows = slice(c0, c0 + CHUNK)
            yv = y_all[rows, :]
            r = lax.rsqrt(jnp.mean(yv * yv, axis=-1, keepdims=True) + EPS)
            nrm = yv * r
            err = x_ref[rows, :] + nrm * g - t_ref[rows, :]
            loss_ref[...] += 0.5 * jnp.sum(jnp.sum(err * err, axis=-1, keepdims=True), axis=0, keepdims=True) / d
            dout = err * (1.0 / d)
            dout_ref[rows, :] = dout
            dg_ref[...] += jnp.sum(dout * nrm, axis=0, keepdims=True)
            dn = dout * g
            dy = r * (dn - nrm * jnp.mean(dn * nrm, axis=-1, keepdims=True))
            dy_ref[rows, :] = dy.astype(BF16)

    return pl.pallas_call(
        body, name="out_proj_loss", grid=(s // tm,),
        in_specs=[pl.BlockSpec((tm, d), lambda i: (i, 0)),
                  pl.BlockSpec((d, d), lambda i: (0, 0)),
                  pl.BlockSpec((tm, d), lambda i: (i, 0)),
                  pl.BlockSpec((tm, d), lambda i: (i, 0)),
                  pl.BlockSpec((1, d), lambda i: (0, 0))],
        out_specs=[pl.BlockSpec((tm, d), lambda i: (i, 0)),
                   pl.BlockSpec((tm, d), lambda i: (i, 0)),
                   pl.BlockSpec((1, d), lambda i: (0, 0)),
                   pl.BlockSpec((1, LANES), lambda i: (0, 0))],
        out_shape=[jax.ShapeDtypeStruct((s, d), BF16), jax.ShapeDtypeStruct((s, d), F32),
                   jax.ShapeDtypeStruct((1, d), F32), jax.ShapeDtypeStruct((1, LANES), F32)],
        compiler_params=_cparams(("arbitrary",)),
    )(cat, w_out, x, target, g_post)


def matmul_nt(a, b, name):
    m, k = a.shape
    n = b.shape[0]
    tm = _tile(m, (512, 256, 128))

    def body(a_ref, b_ref, o_ref):
        o_ref[...] = _dot(a_ref[...], b_ref[...], 1, 1).astype(o_ref.dtype)

    return pl.pallas_call(
        body, name=name, grid=(m // tm,),
        in_specs=[pl.BlockSpec((tm, k), lambda i: (i, 0)), pl.BlockSpec((n, k), lambda i: (0, 0))],
        out_specs=pl.BlockSpec((tm, n), lambda i: (i, 0)),
        out_shape=jax.ShapeDtypeStruct((m, n), BF16),
        compiler_params=_cparams(("arbitrary",)),
    )(a, b)


def matmul_tn(a, b, tm, name, comms=()):
    k, m = a.shape
    n = b.shape[1]
    steps = m // tm
    hosted = _Hosted(comms)

    def body(*refs):
        (a_ref, b_hbm), (o_ref, cs_ref), (b_ref, b_sem), phases = hosted.split(refs, 2, 2, 2)
        step = pl.program_id(0)
        _before_step(phases, step, steps)

        @pl.when(step == 0)
        def _():
            load = pltpu.make_async_copy(b_hbm, b_ref, b_sem)
            load.start()
            load.wait()

        o_ref[...] = _dot(a_ref[...], b_ref[...], 0, 0).astype(o_ref.dtype)
        rows = _tile(k, (512, 128))
        cs = jnp.zeros((1, tm), F32)
        for r0 in range(0, k, rows):
            cs = cs + jnp.sum(a_ref[r0:r0 + rows, :].astype(F32), axis=0, keepdims=True)
        cs_ref[...] = cs
        _after_step(phases, step, steps)

    return pl.pallas_call(
        body, name=name, grid=(steps,),
        in_specs=[pl.BlockSpec((k, tm), lambda i: (0, i)), ANY] + hosted.in_specs,
        out_specs=[pl.BlockSpec((tm, n), lambda i: (i, 0)), pl.BlockSpec((1, tm), lambda i: (0, i))] + hosted.out_specs,
        out_shape=[jax.ShapeDtypeStruct((m, n), BF16), jax.ShapeDtypeStruct((1, m), F32)] + hosted.out_shape,
        scratch_shapes=[pltpu.VMEM((k, n), b.dtype), pltpu.SemaphoreType.DMA] + hosted.scratch,
        compiler_params=_cparams(("arbitrary",)),
    )(a, b, *hosted.inputs)


def in_proj_bwd(dproj, wt, x, g_pre, dout, comms=()):
    s, d = x.shape
    tm = _tile(s, (512, 256, 128))
    steps = s // tm
    nsub = tm // CHUNK
    hosted = _Hosted(comms)

    def body(*refs):
        ((dp_ref, w_hbm, x_hbm, g_ref, dout_hbm), (gx_hbm, dg_ref),
         (w_ref, w_sem, xbuf, dbuf, gbuf, in_sems, out_sems), phases) = hosted.split(refs, 5, 2, 7)
        step = pl.program_id(0)
        _before_step(phases, step, steps)

        def rows_of(ref, c):
            return ref.at[pl.ds(pl.multiple_of(step * tm + c * CHUNK, CHUNK), CHUNK), :]

        def fetches(c):
            slot = c % 2
            return (pltpu.make_async_copy(rows_of(x_hbm, c), xbuf.at[slot], in_sems.at[slot]),
                    pltpu.make_async_copy(rows_of(dout_hbm, c), dbuf.at[slot], in_sems.at[2 + slot]))

        def put(c):
            return pltpu.make_async_copy(gbuf.at[c % 2], rows_of(gx_hbm, c), out_sems.at[c % 2])

        for cp in fetches(0):
            cp.start()

        @pl.when(step == 0)
        def _():
            dg_ref[...] = jnp.zeros_like(dg_ref)
            load = pltpu.make_async_copy(w_hbm, w_ref, w_sem)
            load.start()
            load.wait()

        dh_all = _dot(dp_ref[...], w_ref[...], 1, 0)
        for c in range(nsub):
            slot = c % 2
            if c + 1 < nsub:
                for cp in fetches(c + 1):
                    cp.start()
            for cp in fetches(c):
                cp.wait()
            if c >= 2:
                put(c - 2).wait()
            dh = dh_all[c * CHUNK:(c + 1) * CHUNK, :]
            xv = xbuf[slot]
            r = lax.rsqrt(jnp.mean(xv * xv, axis=-1, keepdims=True) + EPS)
            xn = xv * r
            dg_ref[...] += jnp.sum(dh * xn, axis=0, keepdims=True)
            dn = dh * g_ref[...]
            gbuf[slot] = dbuf[slot] + r * (dn - xn * jnp.mean(dn * xn, axis=-1, keepdims=True))
            put(c).start()
        for c in range(max(nsub - 2, 0), nsub):
            put(c).wait()

        _after_step(phases, step, steps)

    side = pltpu.VMEM((2, CHUNK, d), F32)
    return pl.pallas_call(
        body, name="in_proj_bwd", grid=(steps,),
        in_specs=[pl.BlockSpec((tm, D_IN), lambda i: (i, 0)), ANY, ANY,
                  pl.BlockSpec((1, d), lambda i: (0, 0)), ANY] + hosted.in_specs,
        out_specs=[ANY, pl.BlockSpec((1, d), lambda i: (0, 0))] + hosted.out_specs,
        out_shape=[jax.ShapeDtypeStruct((s, d), F32), jax.ShapeDtypeStruct((1, d), F32)] + hosted.out_shape,
        scratch_shapes=[pltpu.VMEM((D_IN, d), BF16), pltpu.SemaphoreType.DMA, side, side, side,
                        pltpu.SemaphoreType.DMA((4,)), pltpu.SemaphoreType.DMA((2,))] + hosted.scratch,
        compiler_params=_cparams(("arbitrary",)),
    )(dproj, wt, x, g_pre, dout, *hosted.inputs)


def _lane_iota(shape):
    return lax.broadcasted_iota(jnp.int32, shape, len(shape) - 1)


def rope_tables(pos_col, freq, sign, comms=()):
    s = pos_col.shape[0]
    tr = _tile(s, (512, 256, 128))
    hosted = _Hosted(comms)

    def body(*refs):
        (pos_ref, freq_ref, sign_ref), (out_ref,), _, phases = hosted.split(refs, 3, 1, 0)
        _before_step(phases, pl.program_id(0), s // tr)
        ang = pos_ref[...].astype(F32) * freq_ref[...]
        out_ref[:, :LANES] = jnp.cos(ang)
        out_ref[:, LANES:] = jnp.sin(ang) * sign_ref[...]
        _after_step(phases, pl.program_id(0), s // tr)

    return pl.pallas_call(
        body, name="rope_tables", grid=(s // tr,),
        in_specs=[pl.BlockSpec((tr, 1), lambda i: (i, 0)), pl.BlockSpec((1, LANES), lambda i: (0, 0)),
                  pl.BlockSpec((1, LANES), lambda i: (0, 0))] + hosted.in_specs,
        out_specs=[pl.BlockSpec((tr, 2 * LANES), lambda i: (i, 0))] + hosted.out_specs,
        out_shape=[jax.ShapeDtypeStruct((s, 2 * LANES), F32)] + hosted.out_shape,
        scratch_shapes=hosted.scratch,
        compiler_params=_cparams(("arbitrary",)),
    )(pos_col, freq, sign, *hosted.inputs)


def _partner(v):
    low = (_lane_iota(v.shape) % HEAD_DIM) < (HEAD_DIM // 2)
    return jnp.where(low, pltpu.roll(v, LANES - HEAD_DIM // 2, 1), pltpu.roll(v, HEAD_DIM // 2, 1))


def _rope(v, cos, sin_signed):
    return v * cos + _partner(v) * sin_signed


def _rope_transposed(dv, cos, sin_signed):
    return dv * cos - _partner(dv) * sin_signed


def _both_halves(v, kv_head):
    keep = (_lane_iota(v.shape) >= HEAD_DIM) if kv_head else (_lane_iota(v.shape) < HEAD_DIM)
    return jnp.where(keep, v, pltpu.roll(v, HEAD_DIM, 1))


def _fold_halves(acc):
    return acc + pltpu.roll(acc, HEAD_DIM, 1)


def _by_half(a, b):
    shape = jnp.broadcast_shapes(jnp.shape(a), jnp.shape(b))
    return jnp.where(_lane_iota(shape) < HEAD_DIM, a, b)


def _stack_heads(pair):
    return jnp.concatenate([_by_half(pair, 0.0), _by_half(0.0, pair)], axis=0)


def _band_bias(has_prev):
    i = lax.broadcasted_iota(jnp.int32, (2 * CHUNK, 2 * CHUNK), 0) % CHUNK
    j = lax.broadcasted_iota(jnp.int32, (2 * CHUNK, 2 * CHUNK), 1)
    band = jnp.logical_and(j > i, j <= i + CHUNK)
    return jnp.where(jnp.logical_and(band, jnp.logical_or(j >= CHUNK, has_prev)), 0.0, NEG)


def _probs(qm2, kk2, bias, sink_col):
    sc = _dot(qm2, kk2, 1, 1) + bias
    mx = jnp.maximum(jnp.max(sc, axis=-1, keepdims=True), sink_col)
    p = jnp.exp(sc - mx)
    es = jnp.exp(sink_col - mx)
    inv = 1.0 / (jnp.sum(p, axis=-1, keepdims=True) + es)
    return p * inv, es * inv


def _probs_staged(qm2s, kk2s, bias, sink_cols):
    k = range(len(qm2s))
    scs = [_dot(qm2s[i], kk2s[i], 1, 1) + bias for i in k]
    mxs = [jnp.maximum(jnp.max(scs[i], axis=-1, keepdims=True), sink_cols[i]) for i in k]
    ps = [jnp.exp(scs[i] - mxs[i]) for i in k]
    ess = [jnp.exp(sink_cols[i] - mxs[i]) for i in k]
    invs = [1.0 / (jnp.sum(ps[i], axis=-1, keepdims=True) + ess[i]) for i in k]
    return [ps[i] * invs[i] for i in k], [ess[i] * invs[i] for i in k]


def _sink_col(sinks_ref, pair):
    row = lax.broadcasted_iota(jnp.int32, (2 * CHUNK, 1), 0)
    return jnp.where(row < CHUNK, sinks_ref[2 * pair], sinks_ref[2 * pair + 1])


def _layer_norm_parts(v):
    mu = jnp.mean(v, axis=-1, keepdims=True)
    xc = v - mu
    rstd = lax.rsqrt(jnp.mean(xc * xc, axis=-1, keepdims=True) + EPS)
    return xc * rstd, rstd


def _masked_spatial(w_ref, g):
    t = lax.broadcasted_iota(jnp.int32, (CHUNK, CHUNK), 0)
    sidx = lax.broadcasted_iota(jnp.int32, (CHUNK, CHUNK), 1)
    return jnp.where(t >= sidx, w_ref[g], 0.0).astype(BF16)


def _keys_values(kv_ref, kvp_ref, rope_ref, ropep_ref):
    cos_c, sin_c = rope_ref[:, :LANES], rope_ref[:, LANES:]
    cos_p, sin_p = ropep_ref[:, :LANES], ropep_ref[:, LANES:]
    k_c = _rope(kv_ref[:, :D_KV].astype(F32), cos_c, sin_c)
    k_p = _rope(kvp_ref[:, :D_KV].astype(F32), cos_p, sin_p)
    keys = jnp.concatenate([k_p, k_c], axis=0)
    vals = jnp.concatenate([kvp_ref[:, D_KV:], kv_ref[:, D_KV:]], axis=0).astype(F32)
    return keys, vals, (cos_c, sin_c, cos_p, sin_p)


def mixer_fwd(proj, rope, ln_g, ln_b, w_sp, b_sp_rows, sinks, comms=()):
    s = proj.shape[0]
    nb = s // CHUNK
    hosted = _Hosted(comms)

    def body(sinks_ref, *refs):
        ((proj_ref, kvp_ref, rope_ref, ropep_ref, lng_ref, lnb_ref, w_ref, b_ref), (cat_ref,), _,
         phases) = hosted.split(refs, 8, 1, 0)
        n = pl.program_id(0)
        _before_step(phases, n, nb)
        xhat, _ = _layer_norm_parts(proj_ref[:, OFF_V:OFF_V + D_GMLP].astype(F32))
        vnb = (xhat * lng_ref[...] + lnb_ref[...]).astype(BF16)
        mixeds = [_dot(_masked_spatial(w_ref, g), vnb[:, g * CHUNK:(g + 1) * CHUNK], 1, 0) + b_ref[g]
                  for g in range(GROUPS)]
        for g in range(GROUPS):
            za = proj_ref[:, OFF_ZA + g * CHUNK:OFF_ZA + (g + 1) * CHUNK].astype(F32)
            u = proj_ref[:, OFF_U + g * CHUNK:OFF_U + (g + 1) * CHUNK].astype(F32)
            cat_ref[:, g * CHUNK:(g + 1) * CHUNK] = (u * mixeds[g] * (za * _sigmoid(za))).astype(BF16)
        kv_ref = proj_ref.at[:, OFF_K:OFF_K + 2 * D_KV]
        keys, vals, (cos_c, sin_c, _, _) = _keys_values(kv_ref, kvp_ref, rope_ref, ropep_ref)
        cos_q, sin_q = cos_c * SCALE, sin_c * SCALE
        bias = _band_bias(n > 0)
        kk2 = [_both_halves(keys, kvh).astype(BF16) for kvh in range(N_KV_HEADS)]
        vv2 = [_both_halves(vals, kvh).astype(BF16) for kvh in range(N_KV_HEADS)]
        pairs = range(N_PAIRS)
        qms = [_stack_heads(_rope(proj_ref[:, OFF_Q + pair * LANES:OFF_Q + (pair + 1) * LANES].astype(F32),
                                  cos_q, sin_q)).astype(BF16) for pair in pairs]
        probs = _probs_staged(qms, [kk2[pair // PAIRS_PER_KV] for pair in pairs], bias,
                              [_sink_col(sinks_ref, pair) for pair in pairs])[0]
        outs = [_dot(probs[pair].astype(BF16), vv2[pair // PAIRS_PER_KV], 1, 0) for pair in pairs]
        for pair in pairs:
            out_pair = _by_half(outs[pair][:CHUNK], outs[pair][CHUNK:])
            zb = proj_ref[:, OFF_ZB + pair * LANES:OFF_ZB + (pair + 1) * LANES].astype(F32)
            cat_ref[:, D_GMLP + pair * LANES:D_GMLP + (pair + 1) * LANES] = (
                out_pair * (zb * _sigmoid(zb))).astype(BF16)
        _after_step(phases, n, nb)

    prev = lambda n, *_: (jnp.maximum(n - 1, 0), 0)
    kv_block = OFF_K // (2 * D_KV)
    return pl.pallas_call(
        body, name="mixer_fwd",
        grid_spec=pltpu.PrefetchScalarGridSpec(
            num_scalar_prefetch=1, grid=(nb,),
            in_specs=[pl.BlockSpec((CHUNK, D_IN), lambda n, *_: (n, 0)),
                      pl.BlockSpec((CHUNK, 2 * D_KV), lambda n, *_: (jnp.maximum(n - 1, 0), kv_block)),
                      pl.BlockSpec((CHUNK, 2 * LANES), lambda n, *_: (n, 0)),
                      pl.BlockSpec((CHUNK, 2 * LANES), prev),
                      pl.BlockSpec((1, D_GMLP), lambda n, *_: (0, 0)),
                      pl.BlockSpec((1, D_GMLP), lambda n, *_: (0, 0)),
                      pl.BlockSpec((GROUPS, CHUNK, CHUNK), lambda n, *_: (0, 0, 0)),
                      pl.BlockSpec((GROUPS, CHUNK, CHUNK), lambda n, *_: (0, 0, 0))] + hosted.in_specs,
            out_specs=[pl.BlockSpec((CHUNK, D_GMLP + D_ATTN), lambda n, *_: (n, 0))] + hosted.out_specs,
            scratch_shapes=hosted.scratch),
        out_shape=[jax.ShapeDtypeStruct((s, D_GMLP + D_ATTN), BF16)] + hosted.out_shape,
        compiler_params=_cparams(("arbitrary",)),
    )(sinks, proj, proj, rope, rope, ln_g, ln_b, w_sp, b_sp_rows, *hosted.inputs)


def mixer_bwd(proj, dcat, rope, ln_g, ln_b, w_sp, b_sp_rows, sinks):
    s = proj.shape[0]
    nb = s // CHUNK

    def body(sinks_ref, proj_ref, kvp_ref, dcat_ref, rope_ref, ropep_ref, lng_ref, lnb_ref, w_ref, b_ref,
             dproj_ref, dw_ref, db_ref, dlng_ref, dlnb_ref, dsink_ref,
             pend_ref, pend_kv_ref, dbacc_ref):
        n = pl.program_id(0)

        @pl.when(n == 0)
        def _():
            dw_ref[...] = jnp.zeros_like(dw_ref)
            dbacc_ref[...] = jnp.zeros_like(dbacc_ref)
            dlng_ref[...] = jnp.zeros_like(dlng_ref)
            dlnb_ref[...] = jnp.zeros_like(dlnb_ref)
            dsink_ref[...] = jnp.zeros_like(dsink_ref)

        def flush(dkv_prev):
            @pl.when(n > 0)
            def _():
                dproj_ref[...] = pend_ref[...]
                dproj_ref[:, OFF_K:OFF_K + 2 * D_KV] = (pend_kv_ref[...] + dkv_prev).astype(BF16)

        @pl.when(n < nb)
        def _():
            kv_ref = proj_ref.at[:, OFF_K:OFF_K + 2 * D_KV]
            keys, vals, (cos_c, sin_c, cos_p, sin_p) = _keys_values(kv_ref, kvp_ref, rope_ref, ropep_ref)
            cos_q, sin_q = cos_c * SCALE, sin_c * SCALE
            bias = _band_bias(n > 0)
            lane_row = _lane_iota((1, LANES))
            dsink = jnp.zeros((1, LANES), F32)
            dk_heads, dv_heads = [], []
            dq_pairs, dzb_pairs = [], []
            for kvh in range(N_KV_HEADS):
                kk2 = _both_halves(keys, kvh).astype(BF16)
                vv2 = _both_halves(vals, kvh).astype(BF16)
                pairs = list(range(kvh * PAIRS_PER_KV, (kvh + 1) * PAIRS_PER_KV))
                k4 = range(PAIRS_PER_KV)
                qm2s = [_stack_heads(_rope(proj_ref[:, OFF_Q + pair * LANES:OFF_Q + (pair + 1) * LANES].astype(F32),
                                           cos_q, sin_q)).astype(BF16) for pair in pairs]
                ps, p_sinks = _probs_staged(qm2s, [kk2] * PAIRS_PER_KV, bias,
                                            [_sink_col(sinks_ref, pair) for pair in pairs])
                pbs = [p.astype(BF16) for p in ps]
                o2s = [_dot(pb, vv2, 1, 0) for pb in pbs]
                zbs = [proj_ref[:, OFF_ZB + pair * LANES:OFF_ZB + (pair + 1) * LANES].astype(F32) for pair in pairs]
                sgs = [_sigmoid(zb) for zb in zbs]
                dybs = [dcat_ref[:, D_GMLP + pair * LANES:D_GMLP + (pair + 1) * LANES].astype(F32) for pair in pairs]
                for i in k4:
                    out_pair = _by_half(o2s[i][:CHUNK], o2s[i][CHUNK:])
                    dzb_pairs.append((dybs[i] * out_pair * (sgs[i] * (1.0 + zbs[i] * (1.0 - sgs[i])))).astype(BF16))
                dom2s = [_stack_heads(dybs[i] * (zbs[i] * sgs[i])).astype(BF16) for i in k4]
                dps = [_dot(dom2, vv2, 1, 1) for dom2 in dom2s]
                deltas = [jnp.sum(ps[i] * dps[i], axis=-1, keepdims=True) for i in k4]
                dss = [ps[i] * (dps[i] - deltas[i]) for i in k4]
                for i, pair in enumerate(pairs):
                    dsk = -(p_sinks[i] * deltas[i])
                    dsink = dsink + jnp.where(lane_row == 2 * pair,
                                              jnp.sum(dsk[:CHUNK], axis=0, keepdims=True), 0.0)
                    dsink = dsink + jnp.where(lane_row == 2 * pair + 1,
                                              jnp.sum(dsk[CHUNK:], axis=0, keepdims=True), 0.0)
                dsbs = [ds.astype(BF16) for ds in dss]
                dq2s = [_dot(dsb, kk2, 1, 0) for dsb in dsbs]
                for dq2 in dq2s:
                    dq_pairs.append(_rope_transposed(_by_half(dq2[:CHUNK], dq2[CHUNK:]), cos_q, sin_q).astype(BF16))
                dkks = [_dot(dsbs[i], qm2s[i], 0, 0) for i in k4]
                dvvs = [_dot(pbs[i], dom2s[i], 0, 0) for i in k4]
                dk_heads.append(_fold_halves((dkks[0] + dkks[1]) + (dkks[2] + dkks[3])))
                dv_heads.append(_fold_halves((dvvs[0] + dvvs[1]) + (dvvs[2] + dvvs[3])))
            dk_rot = _by_half(dk_heads[0], dk_heads[1])
            dv_all = _by_half(dv_heads[0], dv_heads[1])
            dk_p = _rope_transposed(dk_rot[:CHUNK], cos_p, sin_p)
            dk_c = _rope_transposed(dk_rot[CHUNK:], cos_c, sin_c)
            flush(jnp.concatenate([dk_p, dv_all[:CHUNK]], axis=1))
            dsink_ref[...] += dsink
            pend_kv_ref[...] = jnp.concatenate([dk_c, dv_all[CHUNK:]], axis=1)
            for pair in range(N_PAIRS):
                pend_ref[:, OFF_Q + pair * LANES:OFF_Q + (pair + 1) * LANES] = dq_pairs[pair]
                pend_ref[:, OFF_ZB + pair * LANES:OFF_ZB + (pair + 1) * LANES] = dzb_pairs[pair]
            xhat, rstd = _layer_norm_parts(proj_ref[:, OFF_V:OFF_V + D_GMLP].astype(F32))
            lng = lng_ref[...]
            vnb = (xhat * lng + lnb_ref[...]).astype(BF16)
            dvn_cols = []
            for g in range(GROUPS):
                cols = slice(g * CHUNK, (g + 1) * CHUNK)
                wm = _masked_spatial(w_ref, g)
                mixed = _dot(wm, vnb[:, cols], 1, 0) + b_ref[g]
                za = proj_ref[:, OFF_ZA + g * CHUNK:OFF_ZA + (g + 1) * CHUNK].astype(F32)
                u = proj_ref[:, OFF_U + g * CHUNK:OFF_U + (g + 1) * CHUNK].astype(F32)
                dya = dcat_ref[:, cols].astype(F32)
                sg = _sigmoid(za)
                sz = za * sg
                pend_ref[:, OFF_U + g * CHUNK:OFF_U + (g + 1) * CHUNK] = (dya * mixed * sz).astype(BF16)
                pend_ref[:, OFF_ZA + g * CHUNK:OFF_ZA + (g + 1) * CHUNK] = (
                    dya * u * mixed * (sg * (1.0 + za * (1.0 - sg)))).astype(BF16)
                dmixed = dya * u * sz
                dmb = dmixed.astype(BF16)
                dbacc_ref[g] += dmixed
                dw_ref[g] += _dot(dmb, vnb[:, cols], 1, 1)
                dvn_cols.append(_dot(wm, dmb, 0, 0))
            dvn = jnp.concatenate(dvn_cols, axis=1)
            dlng_ref[...] += jnp.sum(dvn * xhat, axis=0, keepdims=True)
            dlnb_ref[...] += jnp.sum(dvn, axis=0, keepdims=True)
            dxh = dvn * lng
            dv = rstd * (dxh - jnp.mean(dxh, axis=-1, keepdims=True)
                         - xhat * jnp.mean(dxh * xhat, axis=-1, keepdims=True))
            pend_ref[:, OFF_V:OFF_V + D_GMLP] = dv.astype(BF16)

        @pl.when(n == nb)
        def _():
            flush(jnp.zeros((CHUNK, 2 * D_KV), F32))
            t = lax.broadcasted_iota(jnp.int32, (CHUNK, CHUNK), 0)
            sidx = lax.broadcasted_iota(jnp.int32, (CHUNK, CHUNK), 1)
            lane = _lane_iota((CHUNK, LANES))
            dbt = jnp.zeros((CHUNK, LANES), F32)
            for g in range(GROUPS):
                dw_ref[g] = jnp.where(t >= sidx, dw_ref[g], 0.0)
                dbt = jnp.where(lane == g, jnp.sum(dbacc_ref[g], axis=-1, keepdims=True), dbt)
            db_ref[...] = jnp.transpose(dbt)[:GROUPS, :]

    cur = lambda n, *_: (jnp.minimum(n, nb - 1), 0)
    prev = lambda n, *_: (jnp.clip(n - 1, 0, nb - 1), 0)
    kv_block = OFF_K // (2 * D_KV)
    const2 = lambda n, *_: (0, 0)
    const3 = lambda n, *_: (0, 0, 0)
    return pl.pallas_call(
        body, name="mixer_bwd",
        grid_spec=pltpu.PrefetchScalarGridSpec(
            num_scalar_prefetch=1, grid=(nb + 1,),
            in_specs=[pl.BlockSpec((CHUNK, D_IN), cur),
                      pl.BlockSpec((CHUNK, 2 * D_KV), lambda n, *_: (jnp.clip(n - 1, 0, nb - 1), kv_block)),
                      pl.BlockSpec((CHUNK, D_GMLP + D_ATTN), cur),
                      pl.BlockSpec((CHUNK, 2 * LANES), cur),
                      pl.BlockSpec((CHUNK, 2 * LANES), prev),
                      pl.BlockSpec((1, D_GMLP), const2),
                      pl.BlockSpec((1, D_GMLP), const2),
                      pl.BlockSpec((GROUPS, CHUNK, CHUNK), const3),
                      pl.BlockSpec((GROUPS, CHUNK, CHUNK), const3)],
            out_specs=[pl.BlockSpec((CHUNK, D_IN), lambda n, *_: (jnp.maximum(n - 1, 0), 0)),
                       pl.BlockSpec((GROUPS, CHUNK, CHUNK), const3),
                       pl.BlockSpec((GROUPS, CHUNK), const2),
                       pl.BlockSpec((1, D_GMLP), const2),
                       pl.BlockSpec((1, D_GMLP), const2),
                       pl.BlockSpec((1, LANES), const2)],
            scratch_shapes=[pltpu.VMEM((CHUNK, D_IN), BF16), pltpu.VMEM((CHUNK, 2 * D_KV), F32),
                            pltpu.VMEM((GROUPS, CHUNK, CHUNK), F32)]),
        out_shape=[jax.ShapeDtypeStruct((s, D_IN), BF16),
                   jax.ShapeDtypeStruct((GROUPS, CHUNK, CHUNK), F32),
                   jax.ShapeDtypeStruct((GROUPS, CHUNK), F32),
                   jax.ShapeDtypeStruct((1, D_GMLP), F32),
                   jax.ShapeDtypeStruct((1, D_GMLP), F32),
                   jax.ShapeDtypeStruct((1, LANES), F32)],
        compiler_params=_cparams(("arbitrary",)),
    )(sinks, proj, proj, dcat, rope, rope, ln_g, ln_b, w_sp, b_sp_rows)


def _adamw_math(w, g, m, v):
    m = ADAM_B1 * m + (1.0 - ADAM_B1) * g
    v = ADAM_B2 * v + (1.0 - ADAM_B2) * (g * g)
    m_hat = m / (1.0 - ADAM_B1 ** ADAM_STEP)
    v_hat = v / (1.0 - ADAM_B2 ** ADAM_STEP)
    delta = -ADAM_LR * (m_hat / (jnp.sqrt(v_hat) + ADAM_EPS) + ADAM_WD * w)
    return delta, m, v


def adamw_shard(terms, w, m, v, name):
    r, c = w.shape
    tr = _tile(r, (224, 256, 128, 8))
    n_terms = len(terms)

    def body(*refs):
        w_ref, m_ref, v_ref, g_ref, d_ref, nm_ref, nv_ref = refs[n_terms:]
        g = None
        for ref, (_, slots) in zip(refs[:n_terms], terms):
            for k in range(slots):
                part = ref[k].astype(F32)
                g = part if g is None else g + part
        g_ref[...] = g
        d_ref[...], nm_ref[...], nv_ref[...] = _adamw_math(w_ref[...], g, m_ref[...], v_ref[...])

    spec = pl.BlockSpec((tr, c), lambda i: (i, 0))
    return pl.pallas_call(
        body, name=name, grid=(r // tr,),
        in_specs=[pl.BlockSpec((slots, tr, c), lambda i: (0, i, 0)) for _, slots in terms] + [spec] * 3,
        out_specs=[spec] * 4, out_shape=[jax.ShapeDtypeStruct((r, c), F32)] * 4,
        compiler_params=_cparams(("arbitrary",)),
    )(*[a for a, _ in terms], w, m, v)


def adamw_small(gathered, lane_windows, params):
    n_par = len(params)

    def body(*refs):
        g_refs = refs[:n_par + 1]
        wmv_refs = refs[n_par + 1:4 * n_par + 1]
        out_refs = refs[4 * n_par + 1:]

        def total(ref):
            acc = ref[0]
            for dev in range(1, N_DEV):
                acc = acc + ref[dev]
            return acc

        for i in range(n_par):
            w_ref, m_ref, v_ref = wmv_refs[3 * i:3 * i + 3]
            g = total(g_refs[i])
            if lane_windows[i] is not None:
                start, size = lane_windows[i]
                g = g[..., start:start + size]
            delta, new_m, new_v = _adamw_math(w_ref[...], g, m_ref[...], v_ref[...])
            for ref, val in zip(out_refs[4 * i:4 * i + 4], (g, delta, new_m, new_v)):
                ref[...] = val
        out_refs[4 * n_par][...] = total(g_refs[n_par])

    flat = [a for wmv in params for a in wmv]
    out_shape = [jax.ShapeDtypeStruct(w.shape, F32) for (w, _, _) in params for _ in range(4)]
    out_shape.append(jax.ShapeDtypeStruct(gathered[-1].shape[1:], F32))
    outs = pl.pallas_call(body, name="adamw_small", out_shape=out_shape, compiler_params=_cparams())(*gathered, *flat)
    return [tuple(outs[4 * i:4 * i + 4]) for i in range(n_par)], outs[-1]


def kernel(x, positions, g_pre, w_in, b_qkv, ln_v_g, ln_v_b, w_spatial, b_spatial, attn_sinks, w_out, g_post, loss_target, m_g_pre, m_w_in, m_b_qkv, m_ln_v_g, m_ln_v_b, m_w_spatial, m_b_spatial, m_attn_sinks, m_w_out, m_g_post, v_g_pre, v_w_in, v_b_qkv, v_ln_v_g, v_ln_v_b, v_w_spatial, v_b_spatial, v_attn_sinks, v_w_out, v_g_post):
    x2, target = x[0], loss_target[0]
    seq = x2.shape[0]
    xi, yi, ci = _my_place()

    wt_shard = w_in[0].T.astype(BF16)
    wo_shard = w_out[0].astype(BF16)
    pos_col = positions.reshape(seq, 1)
    half = HEAD_DIM // 2
    inv_freq = ROPE_THETA ** (-jnp.arange(half, dtype=F32) * (2.0 / HEAD_DIM))
    freq = jnp.tile(inv_freq, LANES // half).reshape(1, LANES)
    sign = jnp.tile(jnp.concatenate([-jnp.ones((half,), F32), jnp.ones((half,), F32)]), LANES // HEAD_DIM)
    sign = sign.reshape(1, LANES)
    bias = jnp.concatenate([jnp.zeros((1, OFF_Q), F32), b_qkv, jnp.zeros((1, D_ATTN), F32)], axis=1)
    proj, h, rope, wt = in_proj_gather(x2, pos_col, freq, sign, g_pre, wt_shard, bias)

    b_rows = jnp.broadcast_to(b_spatial[0][:, :, None], (GROUPS, CHUNK, CHUNK))
    sinks = attn_sinks[0]
    cat, wo = mixer_fwd(proj, rope, ln_v_g, ln_v_b, w_spatial[0], b_rows, sinks, comms=[gather_comm([wo_shard])])
    dy, dout, d_g_post, loss_part = out_proj_loss(cat, wo, x2, target, g_post)

    dcat = matmul_nt(dy, wo, "out_proj_bwd")
    d_wo, _ = matmul_tn(cat, dy, 512, "w_out_grad")
    dproj, d_w_sp, d_b_sp, d_ln_g, d_ln_b, d_sinks = mixer_bwd(
        proj, dcat, rope, ln_v_g, ln_v_b, w_spatial[0], b_rows, sinks)
    small_parts = [d_ln_g, d_ln_b, d_w_sp, d_b_sp, d_sinks, d_g_post, loss_part]
    d_wt, colsum, *landed = matmul_tn(dproj, h, 768, "w_in_grad",
                                      comms=[gather_comm(small_parts, stack=True), scatter_comm([d_wo])])
    parts_wo = landed[-1]

    owners = jnp.stack([4 * cx + 2 * cy + ci for cx, cy in (_chip_of(xi, yi, r) for r in range(4))]).astype(jnp.int32)
    (got_wt,) = run_comm(pair_comm([d_wt]), "grad_exchange_pair")
    sum_wt = pair_sum(d_wt, got_wt, owners, "grad_pair_sum_w_in")
    grad_x, d_g_pre, far_wt = in_proj_bwd(dproj, wt, x2, g_pre, dout, comms=[chips_comm([sum_wt])])
    late = run_comm(gather_comm([d_g_pre, colsum], stack=True, direct=True), "allgather_late_grads")
    gathered = late + landed[:-1]
    windows = [None, (OFF_Q, D_QKV), None, None, None, None, (0, N_Q_HEADS), None]
    small = [(g_pre, m_g_pre, v_g_pre), (b_qkv, m_b_qkv, v_b_qkv), (ln_v_g, m_ln_v_g, v_ln_v_g),
             (ln_v_b, m_ln_v_b, v_ln_v_b), (w_spatial[0], m_w_spatial[0], v_w_spatial[0]),
             (b_spatial[0], m_b_spatial[0], v_b_spatial[0]), (attn_sinks, m_attn_sinks, v_attn_sinks),
             (g_post, m_g_post, v_g_post)]
    small_out, loss_row = adamw_small(gathered, windows, small)
    lead = [False, False, False, False, True, True, False, False]
    small_out = [tuple(a[None] if ld else a for a in leaf) for leaf, ld in zip(small_out, lead)]

    wt_out = adamw_shard([(sum_wt, 1), (far_wt, 3)], w_in[0].T, m_w_in[0].T, v_w_in[0].T, "adamw_w_in")
    wo_out = adamw_shard([(parts_wo, N_DEV)], w_out[0], m_w_out[0], v_w_out[0], "adamw_w_out")

    def leaves(k):
        gp, bq, lg, lb, ws, bs, sk, gpo = (leaf[k] for leaf in small_out)
        return [gp, wt_out[k].T[None], bq, lg, lb, ws, bs, sk, wo_out[k][None], gpo]

    return (loss_row[0, 0], grad_x[None], *leaves(0), *leaves(1), *leaves(2), *leaves(3))
```

```python
import functools

import jax
import jax.numpy as jnp
from jax import lax
from jax.experimental import pallas as pl
from jax.experimental.pallas import tpu as pltpu

F32 = jnp.float32
BF16 = jnp.bfloat16

D_MODEL = 2048
D_GMLP = 1024
D_ATTN = 1024
CHUNK = 128
GROUPS = 8
HEAD_DIM = 64
N_Q_HEADS = 16
N_KV_HEADS = 2
D_KV = N_KV_HEADS * HEAD_DIM
D_IN = 3 * D_GMLP + D_ATTN + 2 * D_KV + D_ATTN
OFF_U, OFF_V, OFF_ZA = 0, D_GMLP, 2 * D_GMLP
OFF_Q = 3 * D_GMLP
OFF_K = OFF_Q + D_ATTN
OFF_VA = OFF_K + D_KV
OFF_ZB = OFF_VA + D_KV
D_QKV = D_ATTN + 2 * D_KV
ROPE_THETA = 10000.0
EPS = 1e-6
SCALE = HEAD_DIM ** -0.5
NEG = -1e30
N_PAIRS = N_Q_HEADS // 2
PAIRS_PER_KV = N_PAIRS // N_KV_HEADS

ADAM_LR = 0.001
ADAM_B1 = 0.9
ADAM_B2 = 0.999
ADAM_EPS = 1e-08
ADAM_WD = 0.01
ADAM_STEP = 10

N_DEV = 8
LANES = 128
VMEM_LIMIT = 56 * 1024 * 1024

MESH = pl.DeviceIdType.MESH
ANY = pl.BlockSpec(memory_space=pl.ANY)


def _cparams(sem=None):
    return pltpu.CompilerParams(dimension_semantics=sem, vmem_limit_bytes=VMEM_LIMIT)


def _tile(n, prefs):
    for t in prefs:
        if n % t == 0:
            return t
    return n


def _sigmoid(z):
    return 1.0 / (1.0 + jnp.exp(-z))


def _dot(a, b, ca, cb):
    return lax.dot_general(a, b, (((ca,), (cb,)), ((), ())), preferred_element_type=F32)


def _my_place():
    return lax.axis_index("x"), lax.axis_index("y"), lax.axis_index("c")


def _chip_of(x, y, r):
    return (x ^ (r & 1), y ^ (r >> 1))


def _peer(x, y, c, k):
    return (x ^ (k >> 2), y ^ ((k >> 1) & 1), c ^ (k & 1))


def _index(px, py, pc):
    return 4 * px + 2 * py + pc


class _Comm:
    def __init__(self, inputs, out_shape, scratch, bind):
        self.inputs, self.out_shape, self.scratch, self.bind = list(inputs), list(out_shape), list(scratch), bind


def gather_comm(shards, stack=False, direct=False):
    n_arr = len(shards)

    def bind(ins, outs, sems):
        send_sems, recv_sems, local_sems = sems
        x, y, c = _my_place()
        me, sibling = (x, y, c), (x, y, 1 - c)
        chips = [_chip_of(x, y, r) for r in (1, 2, 3)]

        def rows(a, px, py, pc):
            d = _index(px, py, pc)
            if stack:
                return outs[a].at[d]
            m = shards[a].shape[0]
            return outs[a].at[pl.ds(pl.multiple_of(d * m, 8), m), :]

        def copy(a, k, block, to, src=None):
            return pltpu.make_async_remote_copy(
                src_ref=rows(a, *block) if src is None else src, dst_ref=rows(a, *block),
                send_sem=send_sems.at[a * 7 + k], recv_sem=recv_sems.at[a * 7 + k],
                device_id=to, device_id_type=MESH)

        def mine(a):
            return pltpu.make_async_copy(ins[a], rows(a, *me), local_sems.at[a])

        def own_sends(a):
            if direct:
                return [copy(a, k - 1, me, _peer(x, y, c, k), src=ins[a]) for k in range(1, 8)]
            return ([copy(a, 0, me, sibling, src=ins[a])]
                    + [copy(a, 1 + j, me, (*chip, c), src=ins[a]) for j, chip in enumerate(chips)])

        def start():
            for a in range(n_arr):
                mine(a).start()
                for cp in own_sends(a):
                    cp.start()

        def relay():
            if direct:
                return
            for j, chip in enumerate(chips):
                for a in range(n_arr):
                    copy(a, 1 + j, (*chip, c), me).wait_recv()
                    copy(a, 4 + j, (*chip, c), sibling).start()

        def finish():
            for a in range(n_arr):
                if direct:
                    for k in range(1, 8):
                        copy(a, k - 1, _peer(x, y, c, k), me).wait_recv()
                else:
                    copy(a, 0, sibling, me).wait_recv()
                    for j, chip in enumerate(chips):
                        copy(a, 4 + j, (*chip, 1 - c), me).wait_recv()
                        copy(a, 4 + j, (*chip, c), sibling).wait_send()
                for cp in own_sends(a):
                    cp.wait_send()
                mine(a).wait()

        return start, relay, finish

    def gathered(s):
        return (N_DEV, *s.shape) if stack else (N_DEV * s.shape[0], s.shape[1])

    return _Comm(shards, [jax.ShapeDtypeStruct(gathered(s), s.dtype) for s in shards],
                 [pltpu.SemaphoreType.DMA((7 * n_arr,)), pltpu.SemaphoreType.DMA((7 * n_arr,)),
                  pltpu.SemaphoreType.DMA((n_arr,))], bind)


def scatter_comm(parts):
    n_arr = len(parts)

    def bind(ins, outs, sems):
        send_sems, recv_sems, local_sems = sems
        x, y, c = _my_place()
        my_index = _index(x, y, c)

        def block(a, d):
            m = parts[a].shape[0] // N_DEV
            return ins[a].at[pl.ds(pl.multiple_of(d * m, 16), m), :]

        def copy(a, k, slot):
            peer = _peer(x, y, c, k)
            return pltpu.make_async_remote_copy(
                src_ref=block(a, _index(*peer)), dst_ref=outs[a].at[slot],
                send_sem=send_sems.at[a * 7 + k - 1], recv_sem=recv_sems.at[a * 7 + k - 1],
                device_id=peer, device_id_type=MESH)

        def mine(a):
            return pltpu.make_async_copy(block(a, my_index), outs[a].at[my_index], local_sems.at[a])

        def start():
            for a in range(n_arr):
                mine(a).start()
                for k in range(1, 8):
                    copy(a, k, my_index).start()

        def finish():
            for a in range(n_arr):
                for k in range(1, 8):
                    copy(a, k, _index(*_peer(x, y, c, k))).wait_recv()
                    copy(a, k, my_index).wait_send()
                mine(a).wait()

        return start, (lambda: None), finish

    return _Comm(parts, [jax.ShapeDtypeStruct((N_DEV, p.shape[0] // N_DEV, p.shape[1]), p.dtype) for p in parts],
                 [pltpu.SemaphoreType.DMA((7 * n_arr,)), pltpu.SemaphoreType.DMA((7 * n_arr,)),
                  pltpu.SemaphoreType.DMA((n_arr,))], bind)


def pair_comm(parts):
    n_arr = len(parts)

    def bind(ins, outs, sems):
        send_sems, recv_sems = sems
        x, y, c = _my_place()

        def copies():
            out = []
            for a in range(n_arr):
                m = parts[a].shape[0] // N_DEV
                for r in range(4):
                    owner = _index(*_chip_of(x, y, r), 1 - c)
                    out.append(pltpu.make_async_remote_copy(
                        src_ref=ins[a].at[pl.ds(pl.multiple_of(owner * m, 16), m), :], dst_ref=outs[a].at[r],
                        send_sem=send_sems.at[a * 4 + r], recv_sem=recv_sems.at[a * 4 + r],
                        device_id=(x, y, 1 - c), device_id_type=MESH))
            return out

        def start():
            for cp in copies():
                cp.start()

        def finish():
            for cp in copies():
                cp.wait_recv()
                cp.wait_send()

        return start, (lambda: None), finish

    return _Comm(parts, [jax.ShapeDtypeStruct((4, p.shape[0] // N_DEV, p.shape[1]), p.dtype) for p in parts],
                 [pltpu.SemaphoreType.DMA((4 * n_arr,)), pltpu.SemaphoreType.DMA((4 * n_arr,))], bind)


def chips_comm(sums):
    n_arr = len(sums)

    def bind(ins, outs, sems):
        send_sems, recv_sems = sems
        x, y, c = _my_place()

        def copies():
            return [pltpu.make_async_remote_copy(
                src_ref=ins[a].at[r], dst_ref=outs[a].at[r - 1],
                send_sem=send_sems.at[a * 3 + r - 1], recv_sem=recv_sems.at[a * 3 + r - 1],
                device_id=(*_chip_of(x, y, r), c), device_id_type=MESH) for a in range(n_arr) for r in (1, 2, 3)]

        def start():
            for cp in copies():
                cp.start()

        def finish():
            for cp in copies():
                cp.wait_recv()
                cp.wait_send()

        return start, (lambda: None), finish

    return _Comm(sums, [jax.ShapeDtypeStruct((3,) + s.shape[1:], s.dtype) for s in sums],
                 [pltpu.SemaphoreType.DMA((3 * n_arr,)), pltpu.SemaphoreType.DMA((3 * n_arr,))], bind)


def run_comm(comm, name):
    n_in, n_out = len(comm.inputs), len(comm.out_shape)

    def body(*refs):
        start, relay, finish = comm.bind(refs[:n_in], refs[n_in:n_in + n_out], refs[n_in + n_out:])
        start()
        relay()
        finish()

    outs = pl.pallas_call(body, name=name, out_shape=comm.out_shape, in_specs=[ANY] * n_in,
                          out_specs=[ANY] * n_out, scratch_shapes=comm.scratch)(*comm.inputs)
    return list(outs)


class _Hosted:
    def __init__(self, comms):
        self.comms = list(comms)
        self.inputs = [a for cm in self.comms for a in cm.inputs]
        self.out_shape = [s for cm in self.comms for s in cm.out_shape]
        self.scratch = [s for cm in self.comms for s in cm.scratch]
        self.in_specs = [ANY] * len(self.inputs)
        self.out_specs = [ANY] * len(self.out_shape)

    def split(self, refs, n_in, n_out, n_scratch):
        ni, no = len(self.inputs), len(self.out_shape)
        ins, rest = refs[:n_in], refs[n_in:]
        c_ins, rest = rest[:ni], rest[ni:]
        outs, rest = rest[:n_out], rest[n_out:]
        c_outs, rest = rest[:no], rest[no:]
        scratch, c_sems = rest[:n_scratch], rest[n_scratch:]
        phases = []
        for cm in self.comms:
            a, b, s = len(cm.inputs), len(cm.out_shape), len(cm.scratch)
            phases.append(cm.bind(c_ins[:a], c_outs[:b], c_sems[:s]))
            c_ins, c_outs, c_sems = c_ins[a:], c_outs[b:], c_sems[s:]
        return ins, outs, scratch, phases


def _before_step(phases, step, n_steps):
    if not phases:
        return

    @pl.when(step == 0)
    def _():
        for start, _, _ in phases:
            start()

    @pl.when(step == n_steps // 2)
    def _():
        for _, relay, _ in phases:
            relay()


def _after_step(phases, step, n_steps):
    if not phases:
        return

    @pl.when(step == n_steps - 1)
    def _():
        for _, _, finish in phases:
            finish()


def pair_sum(part, got, owners, name):
    m, n = got.shape[1:]

    def body(own_ref, mine_ref, got_ref, out_ref):
        del own_ref
        out_ref[...] = (mine_ref[...].astype(F32) + got_ref[...].astype(F32)).astype(out_ref.dtype)

    return pl.pallas_call(
        body, name=name,
        grid_spec=pltpu.PrefetchScalarGridSpec(
            num_scalar_prefetch=1, grid=(4,),
            in_specs=[pl.BlockSpec((m, n), lambda r, own: (own[r], 0)),
                      pl.BlockSpec((None, m, n), lambda r, own: (r, 0, 0))],
            out_specs=pl.BlockSpec((None, m, n), lambda r, own: (r, 0, 0))),
        out_shape=jax.ShapeDtypeStruct((4, m, n), got.dtype),
        compiler_params=_cparams(("arbitrary",)),
    )(owners, part, got)


def in_proj(x, g_pre, wt, bias, comms=()):
    s, d = x.shape
    tm = _tile(s, (512, 256, 128))
    tn = 768
    ni, nj = s // tm, D_IN // tn
    hosted = _Hosted(comms)

    def body(*refs):
        (x_ref, g_ref, w_ref, b_ref), (proj_ref, h_ref), _, phases = hosted.split(refs, 4, 2, 0)
        step = pl.program_id(0) * nj + pl.program_id(1)
        _before_step(phases, step, ni * nj)

        @pl.when(pl.program_id(1) == 0)
        def _():
            xv = x_ref[...]
            r = lax.rsqrt(jnp.mean(xv * xv, axis=-1, keepdims=True) + EPS)
            h_ref[...] = (xv * r * g_ref[...]).astype(BF16)

        acc = _dot(h_ref[...], w_ref[...], 1, 1)
        proj_ref[...] = (acc + b_ref[...]).astype(BF16)
        _after_step(phases, step, ni * nj)

    return pl.pallas_call(
        body, name="in_proj", grid=(ni, nj),
        in_specs=[pl.BlockSpec((tm, d), lambda i, j: (i, 0)),
                  pl.BlockSpec((1, d), lambda i, j: (0, 0)),
                  pl.BlockSpec((tn, d), lambda i, j: (j, 0)),
                  pl.BlockSpec((1, tn), lambda i, j: (0, j))] + hosted.in_specs,
        out_specs=[pl.BlockSpec((tm, tn), lambda i, j: (i, j)),
                   pl.BlockSpec((tm, d), lambda i, j: (i, 0))] + hosted.out_specs,
        out_shape=[jax.ShapeDtypeStruct((s, D_IN), BF16), jax.ShapeDtypeStruct((s, d), BF16)] + hosted.out_shape,
        scratch_shapes=hosted.scratch,
        compiler_params=_cparams(("arbitrary", "arbitrary")),
    )(x, g_pre, wt, bias, *hosted.inputs)


def in_proj_gather(x, pos_col, freq, sign, g_pre, wt_shard, bias):
    s, d = x.shape
    tm = _tile(s, (512, 256, 128))
    nt = s // tm
    m = wt_shard.shape[0]
    half = D_IN // 2
    xi = lax.axis_index("x")
    order = jnp.stack([xi, 1 - xi]).astype(jnp.int32)

    def body(order_ref, x_ref, pos_ref, freq_ref, sign_ref, g_ref, b_ref, shard_ref,
             proj_ref, h_ref, rope_ref, wt_ref, w_vmem, send_sems, recv_sems, local_sems):
        del order_ref
        p, i = pl.program_id(0), pl.program_id(1)
        xx, yy, cc = _my_place()
        me, sibling = (xx, yy, cc), (xx, yy, 1 - cc)
        chips = [_chip_of(xx, yy, r) for r in (1, 2, 3)]

        def rows(px, py, pc):
            return wt_ref.at[pl.ds(pl.multiple_of(_index(px, py, pc) * m, 16), m), :]

        def copy(k, block, to, src=None):
            return pltpu.make_async_remote_copy(
                src_ref=rows(*block) if src is None else src, dst_ref=rows(*block),
                send_sem=send_sems.at[k], recv_sem=recv_sems.at[k], device_id=to, device_id_type=MESH)

        def mine():
            return pltpu.make_async_copy(shard_ref, rows(*me), local_sems.at[0])

        def to_sibling():
            return copy(0, me, sibling, src=shard_ref)

        def to_chip(j):
            return copy(1 + j, me, (*chips[j], cc), src=shard_ref)

        def relay(j):
            copy(1 + j, (*chips[j], cc), me).wait_recv()
            copy(4 + j, (*chips[j], cc), sibling).start()

        def relayed(j):
            copy(4 + j, (*chips[j], 1 - cc), me).wait_recv()

        def load_half(which, slot):
            rows_of_half = wt_ref.at[pl.ds(pl.multiple_of(which * half, 16), half), :]
            load = pltpu.make_async_copy(rows_of_half, w_vmem.at[slot], local_sems.at[1 + slot])
            load.start()
            load.wait()

        @pl.when(jnp.logical_and(p == 0, i == 0))
        def _():
            mine().start()
            to_sibling().start()
            to_chip(1).start()
            to_chip(0).start()
            copy(0, sibling, me).wait_recv()
            relay(1)
            relayed(1)
            mine().wait()
            to_chip(1).wait_send()
            to_chip(0).wait_send()
            to_chip(2).start()
            load_half(xx, 0)

        @pl.when(jnp.logical_and(p == 1, i == 0))
        def _():
            relayed(0)
            relayed(2)
            load_half(1 - xx, 1)

        xv = x_ref[...]
        r = lax.rsqrt(jnp.mean(xv * xv, axis=-1, keepdims=True) + EPS)
        hb = (xv * r * g_ref[...]).astype(BF16)
        proj_ref[...] = (_dot(hb, w_vmem[p], 1, 1) + b_ref[...]).astype(BF16)

        @pl.when(p == 0)
        def _():
            h_ref[...] = hb
            ang = pos_ref[...].astype(F32) * freq_ref[...]
            rope_ref[:, :LANES] = jnp.cos(ang)
            rope_ref[:, LANES:] = jnp.sin(ang) * sign_ref[...]

        @pl.when(jnp.logical_and(p == 0, i == 1))
        def _():
            relay(0)

        @pl.when(jnp.logical_and(p == 0, i == nt - 1))
        def _():
            relay(2)

        @pl.when(jnp.logical_and(p == 1, i == nt - 1))
        def _():
            to_sibling().wait_send()
            to_chip(2).wait_send()
            for j in range(3):
                copy(4 + j, (*chips[j], cc), sibling).wait_send()

    const = lambda p, i, o: (0, 0)
    once = lambda p, i, o: (jnp.where(p == 0, i, nt - 1), 0)
    return pl.pallas_call(
        body, name="in_proj_gather",
        grid_spec=pltpu.PrefetchScalarGridSpec(
            num_scalar_prefetch=1, grid=(2, nt),
            in_specs=[pl.BlockSpec((tm, d), lambda p, i, o: (i, 0)),
                      pl.BlockSpec((tm, 1), lambda p, i, o: (i, 0)),
                      pl.BlockSpec((1, LANES), const),
                      pl.BlockSpec((1, LANES), const),
                      pl.BlockSpec((1, d), const),
                      pl.BlockSpec((1, half), lambda p, i, o: (0, o[p])),
                      ANY],
            out_specs=[pl.BlockSpec((tm, half), lambda p, i, o: (i, o[p])),
                       pl.BlockSpec((tm, d), once),
                       pl.BlockSpec((tm, 2 * LANES), once),
                       ANY],
            scratch_shapes=[pltpu.VMEM((2, half, d), BF16), pltpu.SemaphoreType.DMA((7,)),
                            pltpu.SemaphoreType.DMA((7,)), pltpu.SemaphoreType.DMA((3,))]),
        out_shape=[jax.ShapeDtypeStruct((s, D_IN), BF16), jax.ShapeDtypeStruct((s, d), BF16),
                   jax.ShapeDtypeStruct((s, 2 * LANES), F32), jax.ShapeDtypeStruct((D_IN, d), BF16)],
        compiler_params=_cparams(("arbitrary", "arbitrary")),
    )(order, x, pos_col, freq, sign, g_pre, bias, wt_shard)


def out_proj_loss(cat, w_out, x, target, g_post):
    s, d = x.shape
    tm = _tile(s, (256, 128))

    def body(cat_ref, w_ref, x_ref, t_ref, g_ref, dy_ref, dout_ref, dg_ref, loss_ref):
        @pl.when(pl.program_id(0) == 0)
        def _():
            dg_ref[...] = jnp.zeros_like(dg_ref)
            loss_ref[...] = jnp.zeros_like(loss_ref)

        g = g_ref[...]
        y_all = _dot(cat_ref[...], w_ref[...], 1, 0)
        for c0 in range(0, tm, CHUNK):
            rows = slice(c0, c0 + CHUNK)
            yv = y_all[rows, :]
            r = lax.rsqrt(jnp.mean(yv * yv, axis=-1, keepdims=True) + EPS)
            nrm = yv * r
            err = x_ref[rows, :] + nrm * g - t_ref[rows, :]
            loss_ref[...] += 0.5 * jnp.sum(jnp.sum(err * err, axis=-1, keepdims=True), axis=0, keepdims=True) / d
            dout = err * (1.0 / d)
            dout_ref[rows, :] = dout
            dg_ref[...] += jnp.sum(dout * nrm, axis=0, keepdims=True)
            dn = dout * g
            dy = r * (dn - nrm * jnp.mean(dn * nrm, axis=-1, keepdims=True))
            dy_ref[rows, :] = dy.astype(BF16)

    return pl.pallas_call(
        body, name="out_proj_loss", grid=(s // tm,),
        in_specs=[pl.BlockSpec((tm, d), lambda i: (i, 0)),
                  pl.BlockSpec((d, d), lambda i: (0, 0)),
                  pl.BlockSpec((tm, d), lambda i: (i, 0)),
                  pl.BlockSpec((tm, d), lambda i: (i, 0)),
                  pl.BlockSpec((1, d), lambda i: (0, 0))],
        out_specs=[pl.BlockSpec((tm, d), lambda i: (i, 0)),
                   pl.BlockSpec((tm, d), lambda i: (i, 0)),
                   pl.BlockSpec((1, d), lambda i: (0, 0)),
                   pl.BlockSpec((1, LANES), lambda i: (0, 0))],
        out_shape=[jax.ShapeDtypeStruct((s, d), BF16), jax.ShapeDtypeStruct((s, d), F32),
                   jax.ShapeDtypeStruct((1, d), F32), jax.ShapeDtypeStruct((1, LANES), F32)],
        compiler_params=_cparams(("arbitrary",)),
    )(cat, w_out, x, target, g_post)


def matmul_nt(a, b, name):
    m, k = a.shape
    n = b.shape[0]
    tm = _tile(m, (512, 256, 128))

    def body(a_ref, b_ref, o_ref):
        o_ref[...] = _dot(a_ref[...], b_ref[...], 1, 1).astype(o_ref.dtype)

    return pl.pallas_call(
        body, name=name, grid=(m // tm,),
        in_specs=[pl.BlockSpec((tm, k), lambda i: (i, 0)), pl.BlockSpec((n, k), lambda i: (0, 0))],
        out_specs=pl.BlockSpec((tm, n), lambda i: (i, 0)),
        out_shape=jax.ShapeDtypeStruct((m, n), BF16),
        compiler_params=_cparams(("arbitrary",)),
    )(a, b)


def matmul_tn(a, b, tm, name, comms=()):
    k, m = a.shape
    n = b.shape[1]
    steps = m // tm
    hosted = _Hosted(comms)

    def body(*refs):
        (a_ref, b_hbm), (o_ref, cs_ref), (b_ref, b_sem), phases = hosted.split(refs, 2, 2, 2)
        step = pl.program_id(0)
        _before_step(phases, step, steps)

        @pl.when(step == 0)
        def _():
            load = pltpu.make_async_copy(b_hbm, b_ref, b_sem)
            load.start()
            load.wait()

        o_ref[...] = _dot(a_ref[...], b_ref[...], 0, 0).astype(o_ref.dtype)
        rows = _tile(k, (512, 128))
        cs = jnp.zeros((1, tm), F32)
        for r0 in range(0, k, rows):
            cs = cs + jnp.sum(a_ref[r0:r0 + rows, :].astype(F32), axis=0, keepdims=True)
        cs_ref[...] = cs
        _after_step(phases, step, steps)

    return pl.pallas_call(
        body, name=name, grid=(steps,),
        in_specs=[pl.BlockSpec((k, tm), lambda i: (0, i)), ANY] + hosted.in_specs,
        out_specs=[pl.BlockSpec((tm, n), lambda i: (i, 0)), pl.BlockSpec((1, tm), lambda i: (0, i))] + hosted.out_specs,
        out_shape=[jax.ShapeDtypeStruct((m, n), BF16), jax.ShapeDtypeStruct((1, m), F32)] + hosted.out_shape,
        scratch_shapes=[pltpu.VMEM((k, n), b.dtype), pltpu.SemaphoreType.DMA] + hosted.scratch,
        compiler_params=_cparams(("arbitrary",)),
    )(a, b, *hosted.inputs)


def in_proj_bwd(dproj, wt, x, g_pre, dout, comms=()):
    s, d = x.shape
    tm = _tile(s, (512, 256, 128))
    steps = s // tm
    nsub = tm // CHUNK
    hosted = _Hosted(comms)

    def body(*refs):
        ((dp_ref, w_hbm, x_hbm, g_ref, dout_hbm), (gx_hbm, dg_ref),
         (w_ref, w_sem, xbuf, dbuf, gbuf, in_sems, out_sems), phases) = hosted.split(refs, 5, 2, 7)
        step = pl.program_id(0)
        _before_step(phases, step, steps)

        def rows_of(ref, c):
            return ref.at[pl.ds(pl.multiple_of(step * tm + c * CHUNK, CHUNK), CHUNK), :]

        def fetches(c):
            slot = c % 2
            return (pltpu.make_async_copy(rows_of(x_hbm, c), xbuf.at[slot], in_sems.at[slot]),
                    pltpu.make_async_copy(rows_of(dout_hbm, c), dbuf.at[slot], in_sems.at[2 + slot]))

        def put(c):
            return pltpu.make_async_copy(gbuf.at[c % 2], rows_of(gx_hbm, c), out_sems.at[c % 2])

        for cp in fetches(0):
            cp.start()

        @pl.when(step == 0)
        def _():
            dg_ref[...] = jnp.zeros_like(dg_ref)
            load = pltpu.make_async_copy(w_hbm, w_ref, w_sem)
            load.start()
            load.wait()

        dh_all = _dot(dp_ref[...], w_ref[...], 1, 0)
        for c in range(nsub):
            slot = c % 2
            if c + 1 < nsub:
                for cp in fetches(c + 1):
                    cp.start()
            for cp in fetches(c):
                cp.wait()
            if c >= 2:
                put(c - 2).wait()
            dh = dh_all[c * CHUNK:(c + 1) * CHUNK, :]
            xv = xbuf[slot]
            r = lax.rsqrt(jnp.mean(xv * xv, axis=-1, keepdims=True) + EPS)
            xn = xv * r
            dg_ref[...] += jnp.sum(dh * xn, axis=0, keepdims=True)
            dn = dh * g_ref[...]
            gbuf[slot] = dbuf[slot] + r * (dn - xn * jnp.mean(dn * xn, axis=-1, keepdims=True))
            put(c).start()
        for c in range(max(nsub - 2, 0), nsub):
            put(c).wait()

        _after_step(phases, step, steps)

    side = pltpu.VMEM((2, CHUNK, d), F32)
    row = pl.BlockSpec((1, d), lambda i: (0, 0))
    return pl.pallas_call(
        body, name="in_proj_bwd", grid=(steps,),
        in_specs=[pl.BlockSpec((tm, D_IN), lambda i: (i, 0)), ANY, ANY, row, ANY] + hosted.in_specs,
        out_specs=[ANY, row] + hosted.out_specs,
        out_shape=[jax.ShapeDtypeStruct((s, d), F32), jax.ShapeDtypeStruct((1, d), F32)] + hosted.out_shape,
        scratch_shapes=[pltpu.VMEM((D_IN, d), BF16), pltpu.SemaphoreType.DMA, side, side, side,
                        pltpu.SemaphoreType.DMA((4,)), pltpu.SemaphoreType.DMA((2,))] + hosted.scratch,
        compiler_params=_cparams(("arbitrary",)),
    )(dproj, wt, x, g_pre, dout, *hosted.inputs)


def _lane_iota(shape):
    return lax.broadcasted_iota(jnp.int32, shape, len(shape) - 1)


def rope_tables(pos_col, freq, sign, comms=()):
    s = pos_col.shape[0]
    tr = _tile(s, (512, 256, 128))
    hosted = _Hosted(comms)

    def body(*refs):
        (pos_ref, freq_ref, sign_ref), (out_ref,), _, phases = hosted.split(refs, 3, 1, 0)
        _before_step(phases, pl.program_id(0), s // tr)
        ang = pos_ref[...].astype(F32) * freq_ref[...]
        out_ref[:, :LANES] = jnp.cos(ang)
        out_ref[:, LANES:] = jnp.sin(ang) * sign_ref[...]
        _after_step(phases, pl.program_id(0), s // tr)

    return pl.pallas_call(
        body, name="rope_tables", grid=(s // tr,),
        in_specs=[pl.BlockSpec((tr, 1), lambda i: (i, 0)), pl.BlockSpec((1, LANES), lambda i: (0, 0)),
                  pl.BlockSpec((1, LANES), lambda i: (0, 0))] + hosted.in_specs,
        out_specs=[pl.BlockSpec((tr, 2 * LANES), lambda i: (i, 0))] + hosted.out_specs,
        out_shape=[jax.ShapeDtypeStruct((s, 2 * LANES), F32)] + hosted.out_shape,
        scratch_shapes=hosted.scratch,
        compiler_params=_cparams(("arbitrary",)),
    )(pos_col, freq, sign, *hosted.inputs)


def _partner(v):
    low = (_lane_iota(v.shape) % HEAD_DIM) < (HEAD_DIM // 2)
    return jnp.where(low, pltpu.roll(v, LANES - HEAD_DIM // 2, 1), pltpu.roll(v, HEAD_DIM // 2, 1))


def _rope(v, cos, sin_signed):
    return v * cos + _partner(v) * sin_signed


def _rope_transposed(dv, cos, sin_signed):
    return dv * cos - _partner(dv) * sin_signed


def _both_halves(v, kv_head):
    keep = (_lane_iota(v.shape) >= HEAD_DIM) if kv_head else (_lane_iota(v.shape) < HEAD_DIM)
    return jnp.where(keep, v, pltpu.roll(v, HEAD_DIM, 1))


def _fold_halves(acc):
    return acc + pltpu.roll(acc, HEAD_DIM, 1)


def _by_half(a, b):
    shape = jnp.broadcast_shapes(jnp.shape(a), jnp.shape(b))
    return jnp.where(_lane_iota(shape) < HEAD_DIM, a, b)


def _stack_heads(pair):
    return jnp.concatenate([_by_half(pair, 0.0), _by_half(0.0, pair)], axis=0)


def _band_bias(has_prev):
    i = lax.broadcasted_iota(jnp.int32, (2 * CHUNK, 2 * CHUNK), 0) % CHUNK
    j = lax.broadcasted_iota(jnp.int32, (2 * CHUNK, 2 * CHUNK), 1)
    band = jnp.logical_and(j > i, j <= i + CHUNK)
    return jnp.where(jnp.logical_and(band, jnp.logical_or(j >= CHUNK, has_prev)), 0.0, NEG)


def _probs(qm2, kk2, bias, sink_col):
    sc = _dot(qm2, kk2, 1, 1) + bias
    mx = jnp.maximum(jnp.max(sc, axis=-1, keepdims=True), sink_col)
    p = jnp.exp(sc - mx)
    es = jnp.exp(sink_col - mx)
    inv = 1.0 / (jnp.sum(p, axis=-1, keepdims=True) + es)
    return p * inv, es * inv


def _probs_staged(qm2s, kk2s, bias, sink_cols):
    k = range(len(qm2s))
    scs = [_dot(qm2s[i], kk2s[i], 1, 1) + bias for i in k]
    mxs = [jnp.maximum(jnp.max(scs[i], axis=-1, keepdims=True), sink_cols[i]) for i in k]
    ps = [jnp.exp(scs[i] - mxs[i]) for i in k]
    ess = [jnp.exp(sink_cols[i] - mxs[i]) for i in k]
    invs = [1.0 / (jnp.sum(ps[i], axis=-1, keepdims=True) + ess[i]) for i in k]
    return [ps[i] * invs[i] for i in k], [ess[i] * invs[i] for i in k]


def _sink_col(sinks_ref, pair):
    row = lax.broadcasted_iota(jnp.int32, (2 * CHUNK, 1), 0)
    return jnp.where(row < CHUNK, sinks_ref[2 * pair], sinks_ref[2 * pair + 1])


def _layer_norm_parts(v):
    mu = jnp.mean(v, axis=-1, keepdims=True)
    xc = v - mu
    rstd = lax.rsqrt(jnp.mean(xc * xc, axis=-1, keepdims=True) + EPS)
    return xc * rstd, rstd


def _masked_spatial(w_ref, g):
    t = lax.broadcasted_iota(jnp.int32, (CHUNK, CHUNK), 0)
    sidx = lax.broadcasted_iota(jnp.int32, (CHUNK, CHUNK), 1)
    return jnp.where(t >= sidx, w_ref[g], 0.0).astype(BF16)


def _keys_values(kv_ref, kvp_ref, rope_ref, ropep_ref):
    cos_c, sin_c = rope_ref[:, :LANES], rope_ref[:, LANES:]
    cos_p, sin_p = ropep_ref[:, :LANES], ropep_ref[:, LANES:]
    k_c = _rope(kv_ref[:, :D_KV].astype(F32), cos_c, sin_c)
    k_p = _rope(kvp_ref[:, :D_KV].astype(F32), cos_p, sin_p)
    keys = jnp.concatenate([k_p, k_c], axis=0)
    vals = jnp.concatenate([kvp_ref[:, D_KV:], kv_ref[:, D_KV:]], axis=0).astype(F32)
    return keys, vals, (cos_c, sin_c, cos_p, sin_p)


def mixer_fwd(proj, rope, ln_g, ln_b, w_sp, b_sp_rows, sinks, comms=()):
    s = proj.shape[0]
    nb = s // CHUNK
    hosted = _Hosted(comms)

    def body(sinks_ref, *refs):
        ((proj_ref, kvp_ref, rope_ref, ropep_ref, lng_ref, lnb_ref, w_ref, b_ref), (cat_ref,), _,
         phases) = hosted.split(refs, 8, 1, 0)
        n = pl.program_id(0)
        _before_step(phases, n, nb)
        xhat, _ = _layer_norm_parts(proj_ref[:, OFF_V:OFF_V + D_GMLP].astype(F32))
        vnb = (xhat * lng_ref[...] + lnb_ref[...]).astype(BF16)
        mixeds = [_dot(_masked_spatial(w_ref, g), vnb[:, g * CHUNK:(g + 1) * CHUNK], 1, 0) + b_ref[g]
                  for g in range(GROUPS)]
        for g in range(GROUPS):
            za = proj_ref[:, OFF_ZA + g * CHUNK:OFF_ZA + (g + 1) * CHUNK].astype(F32)
            u = proj_ref[:, OFF_U + g * CHUNK:OFF_U + (g + 1) * CHUNK].astype(F32)
            cat_ref[:, g * CHUNK:(g + 1) * CHUNK] = (u * mixeds[g] * (za * _sigmoid(za))).astype(BF16)
        kv_ref = proj_ref.at[:, OFF_K:OFF_K + 2 * D_KV]
        keys, vals, (cos_c, sin_c, _, _) = _keys_values(kv_ref, kvp_ref, rope_ref, ropep_ref)
        cos_q, sin_q = cos_c * SCALE, sin_c * SCALE
        bias = _band_bias(n > 0)
        kk2 = [_both_halves(keys, kvh).astype(BF16) for kvh in range(N_KV_HEADS)]
        vv2 = [_both_halves(vals, kvh).astype(BF16) for kvh in range(N_KV_HEADS)]
        pairs = range(N_PAIRS)
        qms = [_stack_heads(_rope(proj_ref[:, OFF_Q + pair * LANES:OFF_Q + (pair + 1) * LANES].astype(F32),
                                  cos_q, sin_q)).astype(BF16) for pair in pairs]
        probs = _probs_staged(qms, [kk2[pair // PAIRS_PER_KV] for pair in pairs], bias,
                              [_sink_col(sinks_ref, pair) for pair in pairs])[0]
        outs = [_dot(probs[pair].astype(BF16), vv2[pair // PAIRS_PER_KV], 1, 0) for pair in pairs]
        for pair in pairs:
            out_pair = _by_half(outs[pair][:CHUNK], outs[pair][CHUNK:])
            zb = proj_ref[:, OFF_ZB + pair * LANES:OFF_ZB + (pair + 1) * LANES].astype(F32)
            cat_ref[:, D_GMLP + pair * LANES:D_GMLP + (pair + 1) * LANES] = (
                out_pair * (zb * _sigmoid(zb))).astype(BF16)
        _after_step(phases, n, nb)

    prev = lambda n, *_: (jnp.maximum(n - 1, 0), 0)
    kv_block = OFF_K // (2 * D_KV)
    return pl.pallas_call(
        body, name="mixer_fwd",
        grid_spec=pltpu.PrefetchScalarGridSpec(
            num_scalar_prefetch=1, grid=(nb,),
            in_specs=[pl.BlockSpec((CHUNK, D_IN), lambda n, *_: (n, 0)),
                      pl.BlockSpec((CHUNK, 2 * D_KV), lambda n, *_: (jnp.maximum(n - 1, 0), kv_block)),
                      pl.BlockSpec((CHUNK, 2 * LANES), lambda n, *_: (n, 0)),
                      pl.BlockSpec((CHUNK, 2 * LANES), prev),
                      pl.BlockSpec((1, D_GMLP), lambda n, *_: (0, 0)),
                      pl.BlockSpec((1, D_GMLP), lambda n, *_: (0, 0)),
                      pl.BlockSpec((GROUPS, CHUNK, CHUNK), lambda n, *_: (0, 0, 0)),
                      pl.BlockSpec((GROUPS, CHUNK, CHUNK), lambda n, *_: (0, 0, 0))] + hosted.in_specs,
            out_specs=[pl.BlockSpec((CHUNK, D_GMLP + D_ATTN), lambda n, *_: (n, 0))] + hosted.out_specs,
            scratch_shapes=hosted.scratch),
        out_shape=[jax.ShapeDtypeStruct((s, D_GMLP + D_ATTN), BF16)] + hosted.out_shape,
        compiler_params=_cparams(("arbitrary",)),
    )(sinks, proj, proj, rope, rope, ln_g, ln_b, w_sp, b_sp_rows, *hosted.inputs)


def mixer_bwd(proj, dcat, rope, ln_g, ln_b, w_sp, b_sp_rows, sinks, comms=()):
    s = proj.shape[0]
    nb = s // CHUNK
    hosted = _Hosted(comms)

    def body(sinks_ref, *refs):
        ((proj_ref, kvp_ref, dcat_ref, rope_ref, ropep_ref, lng_ref, lnb_ref, w_ref, b_ref),
         (dproj_ref, dw_ref, db_ref, dlng_ref, dlnb_ref, dsink_ref),
         (pend_ref, pend_kv_ref, dbacc_ref), phases) = hosted.split(refs, 9, 6, 3)
        n = pl.program_id(0)
        _before_step(phases, n, nb + 1)

        @pl.when(n == 0)
        def _():
            dw_ref[...] = jnp.zeros_like(dw_ref)
            dbacc_ref[...] = jnp.zeros_like(dbacc_ref)
            dlng_ref[...] = jnp.zeros_like(dlng_ref)
            dlnb_ref[...] = jnp.zeros_like(dlnb_ref)
            dsink_ref[...] = jnp.zeros_like(dsink_ref)

        @pl.when(n > 0)
        def _():
            dproj_ref[...] = pend_ref[...]

        def flush(dkv_prev):
            @pl.when(n > 0)
            def _():
                dproj_ref[:, OFF_K:OFF_K + 2 * D_KV] = (pend_kv_ref[...] + dkv_prev).astype(BF16)

        @pl.when(n < nb)
        def _():
            kv_ref = proj_ref.at[:, OFF_K:OFF_K + 2 * D_KV]
            keys, vals, (cos_c, sin_c, cos_p, sin_p) = _keys_values(kv_ref, kvp_ref, rope_ref, ropep_ref)
            cos_q, sin_q = cos_c * SCALE, sin_c * SCALE
            bias = _band_bias(n > 0)
            lane_row = _lane_iota((1, LANES))
            dsink = jnp.zeros((1, LANES), F32)
            dk_heads, dv_heads = [], []
            for kvh in range(N_KV_HEADS):
                kk2 = _both_halves(keys, kvh).astype(BF16)
                vv2 = _both_halves(vals, kvh).astype(BF16)
                pairs = list(range(kvh * PAIRS_PER_KV, (kvh + 1) * PAIRS_PER_KV))
                k4 = range(PAIRS_PER_KV)
                qm2s = [_stack_heads(_rope(proj_ref[:, OFF_Q + pair * LANES:OFF_Q + (pair + 1) * LANES].astype(F32),
                                           cos_q, sin_q)).astype(BF16) for pair in pairs]
                ps, p_sinks = _probs_staged(qm2s, [kk2] * PAIRS_PER_KV, bias,
                                            [_sink_col(sinks_ref, pair) for pair in pairs])
                pbs = [p.astype(BF16) for p in ps]
                o2s = [_dot(pb, vv2, 1, 0) for pb in pbs]
                zbs = [proj_ref[:, OFF_ZB + pair * LANES:OFF_ZB + (pair + 1) * LANES].astype(F32) for pair in pairs]
                sgs = [_sigmoid(zb) for zb in zbs]
                dybs = [dcat_ref[:, D_GMLP + pair * LANES:D_GMLP + (pair + 1) * LANES].astype(F32) for pair in pairs]
                for i, pair in enumerate(pairs):
                    out_pair = _by_half(o2s[i][:CHUNK], o2s[i][CHUNK:])
                    pend_ref[:, OFF_ZB + pair * LANES:OFF_ZB + (pair + 1) * LANES] = (
                        dybs[i] * out_pair * (sgs[i] * (1.0 + zbs[i] * (1.0 - sgs[i])))).astype(BF16)
                dom2s = [_stack_heads(dybs[i] * (zbs[i] * sgs[i])).astype(BF16) for i in k4]
                dps = [_dot(dom2, vv2, 1, 1) for dom2 in dom2s]
                deltas = [jnp.sum(ps[i] * dps[i], axis=-1, keepdims=True) for i in k4]
                dss = [ps[i] * (dps[i] - deltas[i]) for i in k4]
                for i, pair in enumerate(pairs):
                    dsk = -(p_sinks[i] * deltas[i])
                    dsink = dsink + jnp.where(lane_row == 2 * pair,
                                              jnp.sum(dsk[:CHUNK], axis=0, keepdims=True), 0.0)
                    dsink = dsink + jnp.where(lane_row == 2 * pair + 1,
                                              jnp.sum(dsk[CHUNK:], axis=0, keepdims=True), 0.0)
                dsbs = [ds.astype(BF16) for ds in dss]
                dq2s = [_dot(dsb, kk2, 1, 0) for dsb in dsbs]
                for pair, dq2 in zip(pairs, dq2s):
                    pend_ref[:, OFF_Q + pair * LANES:OFF_Q + (pair + 1) * LANES] = _rope_transposed(
                        _by_half(dq2[:CHUNK], dq2[CHUNK:]), cos_q, sin_q).astype(BF16)
                dkks = [_dot(dsbs[i], qm2s[i], 0, 0) for i in k4]
                dvvs = [_dot(pbs[i], dom2s[i], 0, 0) for i in k4]
                dk_heads.append(_fold_halves((dkks[0] + dkks[1]) + (dkks[2] + dkks[3])))
                dv_heads.append(_fold_halves((dvvs[0] + dvvs[1]) + (dvvs[2] + dvvs[3])))
            dk_rot = _by_half(dk_heads[0], dk_heads[1])
            dv_all = _by_half(dv_heads[0], dv_heads[1])
            dk_p = _rope_transposed(dk_rot[:CHUNK], cos_p, sin_p)
            dk_c = _rope_transposed(dk_rot[CHUNK:], cos_c, sin_c)
            flush(jnp.concatenate([dk_p, dv_all[:CHUNK]], axis=1))
            dsink_ref[...] += dsink
            pend_kv_ref[...] = jnp.concatenate([dk_c, dv_all[CHUNK:]], axis=1)
            xhat, rstd = _layer_norm_parts(proj_ref[:, OFF_V:OFF_V + D_GMLP].astype(F32))
            lng = lng_ref[...]
            vnb = (xhat * lng + lnb_ref[...]).astype(BF16)
            dvn_cols = []
            for g in range(GROUPS):
                cols = slice(g * CHUNK, (g + 1) * CHUNK)
                wm = _masked_spatial(w_ref, g)
                mixed = _dot(wm, vnb[:, cols], 1, 0) + b_ref[g]
                za = proj_ref[:, OFF_ZA + g * CHUNK:OFF_ZA + (g + 1) * CHUNK].astype(F32)
                u = proj_ref[:, OFF_U + g * CHUNK:OFF_U + (g + 1) * CHUNK].astype(F32)
                dya = dcat_ref[:, cols].astype(F32)
                sg = _sigmoid(za)
                sz = za * sg
                pend_ref[:, OFF_U + g * CHUNK:OFF_U + (g + 1) * CHUNK] = (dya * mixed * sz).astype(BF16)
                pend_ref[:, OFF_ZA + g * CHUNK:OFF_ZA + (g + 1) * CHUNK] = (
                    dya * u * mixed * (sg * (1.0 + za * (1.0 - sg)))).astype(BF16)
                dmixed = dya * u * sz
                dmb = dmixed.astype(BF16)
                dbacc_ref[g] += dmixed
                dw_ref[g] += _dot(dmb, vnb[:, cols], 1, 1)
                dvn_cols.append(_dot(wm, dmb, 0, 0))
            dvn = jnp.concatenate(dvn_cols, axis=1)
            dlng_ref[...] += jnp.sum(dvn * xhat, axis=0, keepdims=True)
            dlnb_ref[...] += jnp.sum(dvn, axis=0, keepdims=True)
            dxh = dvn * lng
            dv = rstd * (dxh - jnp.mean(dxh, axis=-1, keepdims=True)
                         - xhat * jnp.mean(dxh * xhat, axis=-1, keepdims=True))
            pend_ref[:, OFF_V:OFF_V + D_GMLP] = dv.astype(BF16)

        @pl.when(n == nb)
        def _():
            flush(jnp.zeros((CHUNK, 2 * D_KV), F32))
            t = lax.broadcasted_iota(jnp.int32, (CHUNK, CHUNK), 0)
            sidx = lax.broadcasted_iota(jnp.int32, (CHUNK, CHUNK), 1)
            lane = _lane_iota((CHUNK, LANES))
            dbt = jnp.zeros((CHUNK, LANES), F32)
            for g in range(GROUPS):
                dw_ref[g] = jnp.where(t >= sidx, dw_ref[g], 0.0)
                dbt = jnp.where(lane == g, jnp.sum(dbacc_ref[g], axis=-1, keepdims=True), dbt)
            db_ref[...] = jnp.transpose(dbt)[:GROUPS, :]

        _after_step(phases, n, nb + 1)

    cur = lambda n, *_: (jnp.minimum(n, nb - 1), 0)
    prev = lambda n, *_: (jnp.clip(n - 1, 0, nb - 1), 0)
    kv_block = OFF_K // (2 * D_KV)
    const2 = lambda n, *_: (0, 0)
    const3 = lambda n, *_: (0, 0, 0)
    return pl.pallas_call(
        body, name="mixer_bwd",
        grid_spec=pltpu.PrefetchScalarGridSpec(
            num_scalar_prefetch=1, grid=(nb + 1,),
            in_specs=[pl.BlockSpec((CHUNK, D_IN), cur),
                      pl.BlockSpec((CHUNK, 2 * D_KV), lambda n, *_: (jnp.clip(n - 1, 0, nb - 1), kv_block)),
                      pl.BlockSpec((CHUNK, D_GMLP + D_ATTN), cur),
                      pl.BlockSpec((CHUNK, 2 * LANES), cur),
                      pl.BlockSpec((CHUNK, 2 * LANES), prev),
                      pl.BlockSpec((1, D_GMLP), const2),
                      pl.BlockSpec((1, D_GMLP), const2),
                      pl.BlockSpec((GROUPS, CHUNK, CHUNK), const3),
                      pl.BlockSpec((GROUPS, CHUNK, CHUNK), const3)] + hosted.in_specs,
            out_specs=[pl.BlockSpec((CHUNK, D_IN), lambda n, *_: (jnp.maximum(n - 1, 0), 0)),
                       pl.BlockSpec((GROUPS, CHUNK, CHUNK), const3),
                       pl.BlockSpec((GROUPS, CHUNK), const2),
                       pl.BlockSpec((1, D_GMLP), const2),
                       pl.BlockSpec((1, D_GMLP), const2),
                       pl.BlockSpec((1, LANES), const2)] + hosted.out_specs,
            scratch_shapes=[pltpu.VMEM((CHUNK, D_IN), BF16), pltpu.VMEM((CHUNK, 2 * D_KV), F32),
                            pltpu.VMEM((GROUPS, CHUNK, CHUNK), F32)] + hosted.scratch),
        out_shape=[jax.ShapeDtypeStruct((s, D_IN), BF16),
                   jax.ShapeDtypeStruct((GROUPS, CHUNK, CHUNK), F32),
                   jax.ShapeDtypeStruct((GROUPS, CHUNK), F32),
                   jax.ShapeDtypeStruct((1, D_GMLP), F32),
                   jax.ShapeDtypeStruct((1, D_GMLP), F32),
                   jax.ShapeDtypeStruct((1, LANES), F32)] + hosted.out_shape,
        compiler_params=_cparams(("arbitrary",)),
    )(sinks, proj, proj, dcat, rope, rope, ln_g, ln_b, w_sp, b_sp_rows, *hosted.inputs)


def _adamw_math(w, g, m, v):
    m = ADAM_B1 * m + (1.0 - ADAM_B1) * g
    v = ADAM_B2 * v + (1.0 - ADAM_B2) * (g * g)
    m_hat = m / (1.0 - ADAM_B1 ** ADAM_STEP)
    v_hat = v / (1.0 - ADAM_B2 ** ADAM_STEP)
    delta = -ADAM_LR * (m_hat / (jnp.sqrt(v_hat) + ADAM_EPS) + ADAM_WD * w)
    return delta, m, v


def adamw_shard(terms, w, m, v, name):
    r, c = w.shape
    tr = _tile(r, (224, 256, 128, 8))
    n_terms = len(terms)

    def body(*refs):
        w_ref, m_ref, v_ref, g_ref, d_ref, nm_ref, nv_ref = refs[n_terms:]
        g = None
        for ref, (_, slots) in zip(refs[:n_terms], terms):
            for k in range(slots):
                part = ref[k].astype(F32)
                g = part if g is None else g + part
        g_ref[...] = g
        d_ref[...], nm_ref[...], nv_ref[...] = _adamw_math(w_ref[...], g, m_ref[...], v_ref[...])

    spec = pl.BlockSpec((tr, c), lambda i: (i, 0))
    return pl.pallas_call(
        body, name=name, grid=(r // tr,),
        in_specs=[pl.BlockSpec((slots, tr, c), lambda i: (0, i, 0)) for _, slots in terms] + [spec] * 3,
        out_specs=[spec] * 4, out_shape=[jax.ShapeDtypeStruct((r, c), F32)] * 4,
        compiler_params=_cparams(("arbitrary",)),
    )(*[a for a, _ in terms], w, m, v)


def adamw_small(gathered, lane_windows, params):
    n_par = len(params)

    def body(*refs):
        g_refs = refs[:n_par + 1]
        wmv_refs = refs[n_par + 1:4 * n_par + 1]
        out_refs = refs[4 * n_par + 1:]

        def total(ref):
            acc = ref[0]
            for dev in range(1, N_DEV):
                acc = acc + ref[dev]
            return acc

        for i in range(n_par):
            w_ref, m_ref, v_ref = wmv_refs[3 * i:3 * i + 3]
            g = total(g_refs[i])
            if lane_windows[i] is not None:
                start, size = lane_windows[i]
                g = g[..., start:start + size]
            delta, new_m, new_v = _adamw_math(w_ref[...], g, m_ref[...], v_ref[...])
            for ref, val in zip(out_refs[4 * i:4 * i + 4], (g, delta, new_m, new_v)):
                ref[...] = val
        out_refs[4 * n_par][...] = total(g_refs[n_par])

    flat = [a for wmv in params for a in wmv]
    out_shape = [jax.ShapeDtypeStruct(w.shape, F32) for (w, _, _) in params for _ in range(4)]
    out_shape.append(jax.ShapeDtypeStruct(gathered[-1].shape[1:], F32))
    outs = pl.pallas_call(body, name="adamw_small", out_shape=out_shape, compiler_params=_cparams())(*gathered, *flat)
    return [tuple(outs[4 * i:4 * i + 4]) for i in range(n_par)], outs[-1]


def kernel(x, positions, g_pre, w_in, b_qkv, ln_v_g, ln_v_b, w_spatial, b_spatial, attn_sinks, w_out, g_post, loss_target, m_g_pre, m_w_in, m_b_qkv, m_ln_v_g, m_ln_v_b, m_w_spatial, m_b_spatial, m_attn_sinks, m_w_out, m_g_post, v_g_pre, v_w_in, v_b_qkv, v_ln_v_g, v_ln_v_b, v_w_spatial, v_b_spatial, v_attn_sinks, v_w_out, v_g_post):
    x2, target = x[0], loss_target[0]
    seq = x2.shape[0]
    xi, yi, ci = _my_place()

    wt_shard = w_in[0].T.astype(BF16)
    wo_shard = w_out[0].astype(BF16)
    pos_col = positions.reshape(seq, 1)
    half = HEAD_DIM // 2
    inv_freq = ROPE_THETA ** (-jnp.arange(half, dtype=F32) * (2.0 / HEAD_DIM))
    freq = jnp.tile(inv_freq, LANES // half).reshape(1, LANES)
    sign = jnp.tile(jnp.concatenate([-jnp.ones((half,), F32), jnp.ones((half,), F32)]), LANES // HEAD_DIM)
    sign = sign.reshape(1, LANES)
    bias = jnp.concatenate([jnp.zeros((1, OFF_Q), F32), b_qkv, jnp.zeros((1, D_ATTN), F32)], axis=1)
    proj, h, rope, wt = in_proj_gather(x2, pos_col, freq, sign, g_pre, wt_shard, bias)

    b_rows = jnp.broadcast_to(b_spatial[0][:, :, None], (GROUPS, CHUNK, CHUNK))
    sinks = attn_sinks[0]
    cat, wo = mixer_fwd(proj, rope, ln_v_g, ln_v_b, w_spatial[0], b_rows, sinks, comms=[gather_comm([wo_shard])])
    dy, dout, d_g_post, loss_part = out_proj_loss(cat, wo, x2, target, g_post)

    dcat = matmul_nt(dy, wo, "out_proj_bwd")
    d_wo, _ = matmul_tn(cat, dy, 512, "w_out_grad")
    dproj, d_w_sp, d_b_sp, d_ln_g, d_ln_b, d_sinks, parts_wo = mixer_bwd(
        proj, dcat, rope, ln_v_g, ln_v_b, w_spatial[0], b_rows, sinks, comms=[scatter_comm([d_wo])])
    small_parts = [d_ln_g, d_ln_b, d_w_sp, d_b_sp, d_sinks, d_g_post, loss_part]
    d_wt, colsum, *landed = matmul_tn(dproj, h, 768, "w_in_grad", comms=[gather_comm(small_parts, stack=True)])

    owners = jnp.stack([4 * cx + 2 * cy + ci for cx, cy in (_chip_of(xi, yi, r) for r in range(4))]).astype(jnp.int32)
    (got_wt,) = run_comm(pair_comm([d_wt]), "grad_exchange_pair")
    sum_wt = pair_sum(d_wt, got_wt, owners, "grad_pair_sum_w_in")
    grad_x, d_g_pre, far_wt = in_proj_bwd(dproj, wt, x2, g_pre, dout, comms=[chips_comm([sum_wt])])
    late = run_comm(gather_comm([d_g_pre, colsum], stack=True, direct=True), "allgather_late_grads")
    gathered = late + landed
    windows = [None, (OFF_Q, D_QKV), None, None, None, None, (0, N_Q_HEADS), None]
    small = [(g_pre, m_g_pre, v_g_pre), (b_qkv, m_b_qkv, v_b_qkv), (ln_v_g, m_ln_v_g, v_ln_v_g),
             (ln_v_b, m_ln_v_b, v_ln_v_b), (w_spatial[0], m_w_spatial[0], v_w_spatial[0]),
             (b_spatial[0], m_b_spatial[0], v_b_spatial[0]), (attn_sinks, m_attn_sinks, v_attn_sinks),
             (g_post, m_g_post, v_g_post)]
    small_out, loss_row = adamw_small(gathered, windows, small)
    lead = [False, False, False, False, True, True, False, False]
    small_out = [tuple(a[None] if ld else a for a in leaf) for leaf, ld in zip(small_out, lead)]

    wt_out = adamw_shard([(sum_wt, 1), (far_wt, 3)], w_in[0].T, m_w_in[0].T, v_w_in[0].T, "adamw_w_in")
    wo_out = adamw_shard([(parts_wo, N_DEV)], w_out[0], m_w_out[0], v_w_out[0], "adamw_w_out")

    def leaves(k):
        gp, bq, lg, lb, ws, bs, sk, gpo = (leaf[k] for leaf in small_out)
        return [gp, wt_out[k].T[None], bq, lg, lb, ws, bs, sk, wo_out[k][None], gpo]

    return (loss_row[0, 0], grad_x[None], *leaves(0), *leaves(1), *leaves(2), *leaves(3))
```

```python
import functools

import jax
import jax.numpy as jnp
from jax import lax
from jax.experimental import pallas as pl
from jax.experimental.pallas import tpu as pltpu

F32 = jnp.float32
BF16 = jnp.bfloat16

D_MODEL = 2048
D_GMLP = 1024
D_ATTN = 1024
CHUNK = 128
GROUPS = 8
HEAD_DIM = 64
N_Q_HEADS = 16
N_KV_HEADS = 2
D_KV = N_KV_HEADS * HEAD_DIM
D_IN = 3 * D_GMLP + D_ATTN + 2 * D_KV + D_ATTN
OFF_U, OFF_V, OFF_ZA = 0, D_GMLP, 2 * D_GMLP
OFF_Q = 3 * D_GMLP
OFF_K = OFF_Q + D_ATTN
OFF_VA = OFF_K + D_KV
OFF_ZB = OFF_VA + D_KV
D_QKV = D_ATTN + 2 * D_KV
ROPE_THETA = 10000.0
EPS = 1e-6
SCALE = HEAD_DIM ** -0.5
NEG = -1e30
N_PAIRS = N_Q_HEADS // 2
PAIRS_PER_KV = N_PAIRS // N_KV_HEADS

ADAM_LR = 0.001
ADAM_B1 = 0.9
ADAM_B2 = 0.999
ADAM_EPS = 1e-08
ADAM_WD = 0.01
ADAM_STEP = 10

N_DEV = 8
LANES = 128
VMEM_LIMIT = 56 * 1024 * 1024

MESH = pl.DeviceIdType.MESH
ANY = pl.BlockSpec(memory_space=pl.ANY)


def _cparams(sem=None):
    return pltpu.CompilerParams(dimension_semantics=sem, vmem_limit_bytes=VMEM_LIMIT)


def _tile(n, prefs):
    for t in prefs:
        if n % t == 0:
            return t
    return n


def _sigmoid(z):
    return 1.0 / (1.0 + jnp.exp(-z))


def _dot(a, b, ca, cb):
    return lax.dot_general(a, b, (((ca,), (cb,)), ((), ())), preferred_element_type=F32)


def _my_place():
    return lax.axis_index("x"), lax.axis_index("y"), lax.axis_index("c")


def _chip_of(x, y, r):
    return (x ^ (r & 1), y ^ (r >> 1))


def _peer(x, y, c, k):
    return (x ^ (k >> 2), y ^ ((k >> 1) & 1), c ^ (k & 1))


def _index(px, py, pc):
    return 4 * px + 2 * py + pc


class _Comm:
    def __init__(self, inputs, out_shape, scratch, bind):
        self.inputs, self.out_shape, self.scratch, self.bind = list(inputs), list(out_shape), list(scratch), bind


def gather_comm(shards, stack=False, direct=False):
    n_arr = len(shards)

    def bind(ins, outs, sems):
        send_sems, recv_sems, local_sems = sems
        x, y, c = _my_place()
        me, sibling = (x, y, c), (x, y, 1 - c)
        chips = [_chip_of(x, y, r) for r in (1, 2, 3)]

        def rows(a, px, py, pc):
            d = _index(px, py, pc)
            if stack:
                return outs[a].at[d]
            m = shards[a].shape[0]
            return outs[a].at[pl.ds(pl.multiple_of(d * m, 8), m), :]

        def copy(a, k, block, to, src=None):
            return pltpu.make_async_remote_copy(
                src_ref=rows(a, *block) if src is None else src, dst_ref=rows(a, *block),
                send_sem=send_sems.at[a * 7 + k], recv_sem=recv_sems.at[a * 7 + k],
                device_id=to, device_id_type=MESH)

        def mine(a):
            return pltpu.make_async_copy(ins[a], rows(a, *me), local_sems.at[a])

        def own_sends(a):
            if direct:
                return [copy(a, k - 1, me, _peer(x, y, c, k), src=ins[a]) for k in range(1, 8)]
            return ([copy(a, 0, me, sibling, src=ins[a])]
                    + [copy(a, 1 + j, me, (*chip, c), src=ins[a]) for j, chip in enumerate(chips)])

        def start():
            for a in range(n_arr):
                mine(a).start()
                for cp in own_sends(a):
                    cp.start()

        def relay():
            if direct:
                return
            for j, chip in enumerate(chips):
                for a in range(n_arr):
                    copy(a, 1 + j, (*chip, c), me).wait_recv()
                    copy(a, 4 + j, (*chip, c), sibling).start()

        def finish():
            for a in range(n_arr):
                if direct:
                    for k in range(1, 8):
                        copy(a, k - 1, _peer(x, y, c, k), me).wait_recv()
                else:
                    copy(a, 0, sibling, me).wait_recv()
                    for j, chip in enumerate(chips):
                        copy(a, 4 + j, (*chip, 1 - c), me).wait_recv()
                        copy(a, 4 + j, (*chip, c), sibling).wait_send()
                for cp in own_sends(a):
                    cp.wait_send()
                mine(a).wait()

        return start, relay, finish

    def gathered(s):
        return (N_DEV, *s.shape) if stack else (N_DEV * s.shape[0], s.shape[1])

    return _Comm(shards, [jax.ShapeDtypeStruct(gathered(s), s.dtype) for s in shards],
                 [pltpu.SemaphoreType.DMA((7 * n_arr,)), pltpu.SemaphoreType.DMA((7 * n_arr,)),
                  pltpu.SemaphoreType.DMA((n_arr,))], bind)


def scatter_comm(parts):
    n_arr = len(parts)

    def bind(ins, outs, sems):
        send_sems, recv_sems, local_sems = sems
        x, y, c = _my_place()
        my_index = _index(x, y, c)

        def block(a, d):
            m = parts[a].shape[0] // N_DEV
            return ins[a].at[pl.ds(pl.multiple_of(d * m, 16), m), :]

        def copy(a, k, slot):
            peer = _peer(x, y, c, k)
            return pltpu.make_async_remote_copy(
                src_ref=block(a, _index(*peer)), dst_ref=outs[a].at[slot],
                send_sem=send_sems.at[a * 7 + k - 1], recv_sem=recv_sems.at[a * 7 + k - 1],
                device_id=peer, device_id_type=MESH)

        def mine(a):
            return pltpu.make_async_copy(block(a, my_index), outs[a].at[my_index], local_sems.at[a])

        def start():
            for a in range(n_arr):
                mine(a).start()
                for k in range(1, 8):
                    copy(a, k, my_index).start()

        def finish():
            for a in range(n_arr):
                for k in range(1, 8):
                    copy(a, k, _index(*_peer(x, y, c, k))).wait_recv()
                    copy(a, k, my_index).wait_send()
                mine(a).wait()

        return start, (lambda: None), finish

    return _Comm(parts, [jax.ShapeDtypeStruct((N_DEV, p.shape[0] // N_DEV, p.shape[1]), p.dtype) for p in parts],
                 [pltpu.SemaphoreType.DMA((7 * n_arr,)), pltpu.SemaphoreType.DMA((7 * n_arr,)),
                  pltpu.SemaphoreType.DMA((n_arr,))], bind)


def pair_comm(parts):
    n_arr = len(parts)

    def bind(ins, outs, sems):
        send_sems, recv_sems = sems
        x, y, c = _my_place()

        def copies():
            out = []
            for a in range(n_arr):
                m = parts[a].shape[0] // N_DEV
                for r in range(4):
                    owner = _index(*_chip_of(x, y, r), 1 - c)
                    out.append(pltpu.make_async_remote_copy(
                        src_ref=ins[a].at[pl.ds(pl.multiple_of(owner * m, 16), m), :], dst_ref=outs[a].at[r],
                        send_sem=send_sems.at[a * 4 + r], recv_sem=recv_sems.at[a * 4 + r],
                        device_id=(x, y, 1 - c), device_id_type=MESH))
            return out

        def start():
            for cp in copies():
                cp.start()

        def finish():
            for cp in copies():
                cp.wait_recv()
                cp.wait_send()

        return start, (lambda: None), finish

    return _Comm(parts, [jax.ShapeDtypeStruct((4, p.shape[0] // N_DEV, p.shape[1]), p.dtype) for p in parts],
                 [pltpu.SemaphoreType.DMA((4 * n_arr,)), pltpu.SemaphoreType.DMA((4 * n_arr,))], bind)


def chips_comm(sums):
    n_arr = len(sums)

    def bind(ins, outs, sems):
        send_sems, recv_sems = sems
        x, y, c = _my_place()

        def copies():
            return [pltpu.make_async_remote_copy(
                src_ref=ins[a].at[r], dst_ref=outs[a].at[r - 1],
                send_sem=send_sems.at[a * 3 + r - 1], recv_sem=recv_sems.at[a * 3 + r - 1],
                device_id=(*_chip_of(x, y, r), c), device_id_type=MESH) for a in range(n_arr) for r in (1, 2, 3)]

        def start():
            for cp in copies():
                cp.start()

        def finish():
            for cp in copies():
                cp.wait_recv()
                cp.wait_send()

        return start, (lambda: None), finish

    return _Comm(sums, [jax.ShapeDtypeStruct((3,) + s.shape[1:], s.dtype) for s in sums],
                 [pltpu.SemaphoreType.DMA((3 * n_arr,)), pltpu.SemaphoreType.DMA((3 * n_arr,))], bind)


def run_comm(comm, name):
    n_in, n_out = len(comm.inputs), len(comm.out_shape)

    def body(*refs):
        start, relay, finish = comm.bind(refs[:n_in], refs[n_in:n_in + n_out], refs[n_in + n_out:])
        start()
        relay()
        finish()

    outs = pl.pallas_call(body, name=name, out_shape=comm.out_shape, in_specs=[ANY] * n_in,
                          out_specs=[ANY] * n_out, scratch_shapes=comm.scratch)(*comm.inputs)
    return list(outs)


class _Hosted:
    def __init__(self, comms):
        self.comms = list(comms)
        self.inputs = [a for cm in self.comms for a in cm.inputs]
        self.out_shape = [s for cm in self.comms for s in cm.out_shape]
        self.scratch = [s for cm in self.comms for s in cm.scratch]
        self.in_specs = [ANY] * len(self.inputs)
        self.out_specs = [ANY] * len(self.out_shape)

    def split(self, refs, n_in, n_out, n_scratch):
        ni, no = len(self.inputs), len(self.out_shape)
        ins, rest = refs[:n_in], refs[n_in:]
        c_ins, rest = rest[:ni], rest[ni:]
        outs, rest = rest[:n_out], rest[n_out:]
        c_outs, rest = rest[:no], rest[no:]
        scratch, c_sems = rest[:n_scratch], rest[n_scratch:]
        phases = []
        for cm in self.comms:
            a, b, s = len(cm.inputs), len(cm.out_shape), len(cm.scratch)
            phases.append(cm.bind(c_ins[:a], c_outs[:b], c_sems[:s]))
            c_ins, c_outs, c_sems = c_ins[a:], c_outs[b:], c_sems[s:]
        return ins, outs, scratch, phases


def _before_step(phases, step, n_steps):
    if not phases:
        return

    @pl.when(step == 0)
    def _():
        for start, _, _ in phases:
            start()

    @pl.when(step == n_steps // 2)
    def _():
        for _, relay, _ in phases:
            relay()


def _after_step(phases, step, n_steps):
    if not phases:
        return

    @pl.when(step == n_steps - 1)
    def _():
        for _, _, finish in phases:
            finish()


def pair_sum(part, got, owners, name):
    m, n = got.shape[1:]

    def body(own_ref, mine_ref, got_ref, out_ref):
        del own_ref
        out_ref[...] = (mine_ref[...].astype(F32) + got_ref[...].astype(F32)).astype(out_ref.dtype)

    return pl.pallas_call(
        body, name=name,
        grid_spec=pltpu.PrefetchScalarGridSpec(
            num_scalar_prefetch=1, grid=(4,),
            in_specs=[pl.BlockSpec((m, n), lambda r, own: (own[r], 0)),
                      pl.BlockSpec((None, m, n), lambda r, own: (r, 0, 0))],
            out_specs=pl.BlockSpec((None, m, n), lambda r, own: (r, 0, 0))),
        out_shape=jax.ShapeDtypeStruct((4, m, n), got.dtype),
        compiler_params=_cparams(("arbitrary",)),
    )(owners, part, got)


def in_proj(x, g_pre, wt, bias, comms=()):
    s, d = x.shape
    tm = _tile(s, (512, 256, 128))
    tn = 768
    ni, nj = s // tm, D_IN // tn
    hosted = _Hosted(comms)

    def body(*refs):
        (x_ref, g_ref, w_ref, b_ref), (proj_ref, h_ref), _, phases = hosted.split(refs, 4, 2, 0)
        step = pl.program_id(0) * nj + pl.program_id(1)
        _before_step(phases, step, ni * nj)

        @pl.when(pl.program_id(1) == 0)
        def _():
            xv = x_ref[...]
            r = lax.rsqrt(jnp.mean(xv * xv, axis=-1, keepdims=True) + EPS)
            h_ref[...] = (xv * r * g_ref[...]).astype(BF16)

        acc = _dot(h_ref[...], w_ref[...], 1, 1)
        proj_ref[...] = (acc + b_ref[...]).astype(BF16)
        _after_step(phases, step, ni * nj)

    return pl.pallas_call(
        body, name="in_proj", grid=(ni, nj),
        in_specs=[pl.BlockSpec((tm, d), lambda i, j: (i, 0)),
                  pl.BlockSpec((1, d), lambda i, j: (0, 0)),
                  pl.BlockSpec((tn, d), lambda i, j: (j, 0)),
                  pl.BlockSpec((1, tn), lambda i, j: (0, j))] + hosted.in_specs,
        out_specs=[pl.BlockSpec((tm, tn), lambda i, j: (i, j)),
                   pl.BlockSpec((tm, d), lambda i, j: (i, 0))] + hosted.out_specs,
        out_shape=[jax.ShapeDtypeStruct((s, D_IN), BF16), jax.ShapeDtypeStruct((s, d), BF16)] + hosted.out_shape,
        scratch_shapes=hosted.scratch,
        compiler_params=_cparams(("arbitrary", "arbitrary")),
    )(x, g_pre, wt, bias, *hosted.inputs)


def in_proj_gather(x, pos_col, freq, sign, g_pre, wt_shard, bias):
    s, d = x.shape
    tm = _tile(s, (512, 256, 128))
    nt = s // tm
    m = wt_shard.shape[0]
    half = D_IN // 2
    xi = lax.axis_index("x")
    order = jnp.stack([xi, 1 - xi]).astype(jnp.int32)

    def body(order_ref, x_ref, pos_ref, freq_ref, sign_ref, g_ref, b_ref, shard_ref,
             proj_ref, h_ref, rope_ref, wt_ref, w_vmem, send_sems, recv_sems, local_sems):
        del order_ref
        p, i = pl.program_id(0), pl.program_id(1)
        xx, yy, cc = _my_place()
        me, sibling = (xx, yy, cc), (xx, yy, 1 - cc)
        chips = [_chip_of(xx, yy, r) for r in (1, 2, 3)]

        def rows(px, py, pc):
            return wt_ref.at[pl.ds(pl.multiple_of(_index(px, py, pc) * m, 16), m), :]

        def copy(k, block, to, src=None):
            return pltpu.make_async_remote_copy(
                src_ref=rows(*block) if src is None else src, dst_ref=rows(*block),
                send_sem=send_sems.at[k], recv_sem=recv_sems.at[k], device_id=to, device_id_type=MESH)

        def mine():
            return pltpu.make_async_copy(shard_ref, rows(*me), local_sems.at[0])

        def to_sibling():
            return copy(0, me, sibling, src=shard_ref)

        def to_chip(j):
            return copy(1 + j, me, (*chips[j], cc), src=shard_ref)

        def relay(j):
            copy(1 + j, (*chips[j], cc), me).wait_recv()
            copy(4 + j, (*chips[j], cc), sibling).start()

        def relayed(j):
            copy(4 + j, (*chips[j], 1 - cc), me).wait_recv()

        def load_half(which, slot):
            rows_of_half = wt_ref.at[pl.ds(pl.multiple_of(which * half, 16), half), :]
            load = pltpu.make_async_copy(rows_of_half, w_vmem.at[slot], local_sems.at[1 + slot])
            load.start()
            load.wait()

        @pl.when(jnp.logical_and(p == 0, i == 0))
        def _():
            mine().start()
            to_sibling().start()
            to_chip(1).start()
            to_chip(0).start()
            copy(0, sibling, me).wait_recv()
            relay(1)
            relayed(1)
            mine().wait()
            to_chip(1).wait_send()
            to_chip(0).wait_send()
            to_chip(2).start()
            load_half(xx, 0)

        @pl.when(jnp.logical_and(p == 1, i == 0))
        def _():
            relayed(0)
            relayed(2)
            load_half(1 - xx, 1)

        xv = x_ref[...]
        r = lax.rsqrt(jnp.mean(xv * xv, axis=-1, keepdims=True) + EPS)
        hb = (xv * r * g_ref[...]).astype(BF16)
        proj_ref[...] = (_dot(hb, w_vmem[p], 1, 1) + b_ref[...]).astype(BF16)

        @pl.when(p == 0)
        def _():
            h_ref[...] = hb
            ang = pos_ref[...].astype(F32) * freq_ref[...]
            rope_ref[:, :LANES] = jnp.cos(ang)
            rope_ref[:, LANES:] = jnp.sin(ang) * sign_ref[...]

        @pl.when(jnp.logical_and(p == 0, i == 1))
        def _():
            relay(0)

        @pl.when(jnp.logical_and(p == 0, i == nt - 1))
        def _():
            relay(2)

        @pl.when(jnp.logical_and(p == 1, i == nt - 1))
        def _():
            to_sibling().wait_send()
            to_chip(2).wait_send()
            for j in range(3):
                copy(4 + j, (*chips[j], cc), sibling).wait_send()

    const = lambda p, i, o: (0, 0)
    once = lambda p, i, o: (jnp.where(p == 0, i, nt - 1), 0)
    return pl.pallas_call(
        body, name="in_proj_gather",
        grid_spec=pltpu.PrefetchScalarGridSpec(
            num_scalar_prefetch=1, grid=(2, nt),
            in_specs=[pl.BlockSpec((tm, d), lambda p, i, o: (i, 0)),
                      pl.BlockSpec((tm, 1), lambda p, i, o: (i, 0)),
                      pl.BlockSpec((1, LANES), const),
                      pl.BlockSpec((1, LANES), const),
                      pl.BlockSpec((1, d), const),
                      pl.BlockSpec((1, half), lambda p, i, o: (0, o[p])),
                      ANY],
            out_specs=[pl.BlockSpec((tm, half), lambda p, i, o: (i, o[p])),
                       pl.BlockSpec((tm, d), once),
                       pl.BlockSpec((tm, 2 * LANES), once),
                       ANY],
            scratch_shapes=[pltpu.VMEM((2, half, d), BF16), pltpu.SemaphoreType.DMA((7,)),
                            pltpu.SemaphoreType.DMA((7,)), pltpu.SemaphoreType.DMA((3,))]),
        out_shape=[jax.ShapeDtypeStruct((s, D_IN), BF16), jax.ShapeDtypeStruct((s, d), BF16),
                   jax.ShapeDtypeStruct((s, 2 * LANES), F32), jax.ShapeDtypeStruct((D_IN, d), BF16)],
        compiler_params=_cparams(("arbitrary", "arbitrary")),
    )(order, x, pos_col, freq, sign, g_pre, bias, wt_shard)


def out_proj_loss(cat, w_out, x, target, g_post):
    s, d = x.shape
    tm = _tile(s, (256, 128))

    def body(cat_ref, w_ref, x_ref, t_ref, g_ref, dy_ref, dout_ref, dg_ref, loss_ref):
        @pl.when(pl.program_id(0) == 0)
        def _():
            dg_ref[...] = jnp.zeros_like(dg_ref)
            loss_ref[...] = jnp.zeros_like(loss_ref)

        g = g_ref[...]
        y_all = _dot(cat_ref[...], w_ref[...], 1, 0)
        for c0 in range(0, tm, CHUNK):
            rows = slice(c0, c0 + CHUNK)
            yv = y_all[rows, :]
            r = lax.rsqrt(jnp.mean(yv * yv, axis=-1, keepdims=True) + EPS)
            nrm = yv * r
            err = x_ref[rows, :] + nrm * g - t_ref[rows, :]
            loss_ref[...] += 0.5 * jnp.sum(jnp.sum(err * err, axis=-1, keepdims=True), axis=0, keepdims=True) / d
            dout = err * (1.0 / d)
            dout_ref[rows, :] = dout
            dg_ref[...] += jnp.sum(dout * nrm, axis=0, keepdims=True)
            dn = dout * g
            dy = r * (dn - nrm * jnp.mean(dn * nrm, axis=-1, keepdims=True))
            dy_ref[rows, :] = dy.astype(BF16)

    return pl.pallas_call(
        body, name="out_proj_loss", grid=(s // tm,),
        in_specs=[pl.BlockSpec((tm, d), lambda i: (i, 0)),
                  pl.BlockSpec((d, d), lambda i: (0, 0)),
                  pl.BlockSpec((tm, d), lambda i: (i, 0)),
                  pl.BlockSpec((tm, d), lambda i: (i, 0)),
                  pl.BlockSpec((1, d), lambda i: (0, 0))],
        out_specs=[pl.BlockSpec((tm, d), lambda i: (i, 0)),
                   pl.BlockSpec((tm, d), lambda i: (i, 0)),
                   pl.BlockSpec((1, d), lambda i: (0, 0)),
                   pl.BlockSpec((1, LANES), lambda i: (0, 0))],
        out_shape=[jax.ShapeDtypeStruct((s, d), BF16), jax.ShapeDtypeStruct((s, d), F32),
                   jax.ShapeDtypeStruct((1, d), F32), jax.ShapeDtypeStruct((1, LANES), F32)],
        compiler_params=_cparams(("arbitrary",)),
    )(cat, w_out, x, target, g_post)


def matmul_nt(a, b, name):
    m, k = a.shape
    n = b.shape[0]
    tm = _tile(m, (512, 256, 128))

    def body(a_ref, b_ref, o_ref):
        o_ref[...] = _dot(a_ref[...], b_ref[...], 1, 1).astype(o_ref.dtype)

    return pl.pallas_call(
        body, name=name, grid=(m // tm,),
        in_specs=[pl.BlockSpec((tm, k), lambda i: (i, 0)), pl.BlockSpec((n, k), lambda i: (0, 0))],
        out_specs=pl.BlockSpec((tm, n), lambda i: (i, 0)),
        out_shape=jax.ShapeDtypeStruct((m, n), BF16),
        compiler_params=_cparams(("arbitrary",)),
    )(a, b)


def matmul_tn(a, b, tm, name, comms=()):
    k, m = a.shape
    n = b.shape[1]
    steps = m // tm
    hosted = _Hosted(comms)

    def body(*refs):
        (a_ref, b_hbm), (o_ref, cs_ref), (b_ref, b_sem), phases = hosted.split(refs, 2, 2, 2)
        step = pl.program_id(0)
        _before_step(phases, step, steps)

        @pl.when(step == 0)
        def _():
            load = pltpu.make_async_copy(b_hbm, b_ref, b_sem)
            load.start()
            load.wait()

        o_ref[...] = _dot(a_ref[...], b_ref[...], 0, 0).astype(o_ref.dtype)
        rows = _tile(k, (512, 128))
        cs = jnp.zeros((1, tm), F32)
        for r0 in range(0, k, rows):
            cs = cs + jnp.sum(a_ref[r0:r0 + rows, :].astype(F32), axis=0, keepdims=True)
        cs_ref[...] = cs
        _after_step(phases, step, steps)

    return pl.pallas_call(
        body, name=name, grid=(steps,),
        in_specs=[pl.BlockSpec((k, tm), lambda i: (0, i)), ANY] + hosted.in_specs,
        out_specs=[pl.BlockSpec((tm, n), lambda i: (i, 0)), pl.BlockSpec((1, tm), lambda i: (0, i))] + hosted.out_specs,
        out_shape=[jax.ShapeDtypeStruct((m, n), BF16), jax.ShapeDtypeStruct((1, m), F32)] + hosted.out_shape,
        scratch_shapes=[pltpu.VMEM((k, n), b.dtype), pltpu.SemaphoreType.DMA] + hosted.scratch,
        compiler_params=_cparams(("arbitrary",)),
    )(a, b, *hosted.inputs)


def in_proj_bwd(dproj, wt, x, g_pre, dout, comms=()):
    s, d = x.shape
    tm = _tile(s, (512, 256, 128))
    steps = s // tm
    nsub = tm // CHUNK
    kchunks = [(k0, 1024) for k0 in range(0, 5120, 1024)] + [(5120, 256)]
    ksplit = len(kchunks)
    hosted = _Hosted(comms)

    def body(*refs):
        ((*dp_refs, w_hbm, x_hbm, g_ref, dout_hbm), (gx_hbm, dg_ref),
         (w_ref, w_sem, xbuf, dbuf, gbuf, in_sems, out_sems), phases) = hosted.split(refs, 4 + ksplit, 2, 7)
        step = pl.program_id(0)
        _before_step(phases, step, steps)

        def rows_of(ref, c):
            return ref.at[pl.ds(pl.multiple_of(step * tm + c * CHUNK, CHUNK), CHUNK), :]

        def fetches(c):
            return (pltpu.make_async_copy(rows_of(x_hbm, c), xbuf.at[c], in_sems.at[c]),
                    pltpu.make_async_copy(rows_of(dout_hbm, c), dbuf.at[c], in_sems.at[nsub + c]))

        def put(c):
            return pltpu.make_async_copy(gbuf.at[c % 2], rows_of(gx_hbm, c), out_sems.at[c % 2])

        for c in range(nsub):
            for cp in fetches(c):
                cp.start()

        @pl.when(step == 0)
        def _():
            dg_ref[...] = jnp.zeros_like(dg_ref)
            load = pltpu.make_async_copy(w_hbm, w_ref, w_sem)
            load.start()
            load.wait()

        dh_all = None
        for (k0, kw), dp_ref in zip(kchunks, dp_refs):
            part = _dot(dp_ref[...], w_ref[k0:k0 + kw, :], 1, 0)
            dh_all = part if dh_all is None else dh_all + part
        for c in range(nsub):
            for cp in fetches(c):
                cp.wait()
            if c >= 2:
                put(c - 2).wait()
            dh = dh_all[c * CHUNK:(c + 1) * CHUNK, :]
            xv = xbuf[c]
            r = lax.rsqrt(jnp.mean(xv * xv, axis=-1, keepdims=True) + EPS)
            xn = xv * r
            dg_ref[...] += jnp.sum(dh * xn, axis=0, keepdims=True)
            dn = dh * g_ref[...]
            gbuf[c % 2] = dbuf[c] + r * (dn - xn * jnp.mean(dn * xn, axis=-1, keepdims=True))
            put(c).start()
        for c in range(max(nsub - 2, 0), nsub):
            put(c).wait()

        _after_step(phases, step, steps)

    side_in, side_out = pltpu.VMEM((nsub, CHUNK, d), F32), pltpu.VMEM((2, CHUNK, d), F32)
    row = pl.BlockSpec((1, d), lambda i: (0, 0))
    return pl.pallas_call(
        body, name="in_proj_bwd", grid=(steps,),
        in_specs=[pl.BlockSpec((tm, kw), functools.partial(lambda j, i: (i, j), k0 // kw)) for k0, kw in kchunks]
        + [ANY, ANY, row, ANY] + hosted.in_specs,
        out_specs=[ANY, row] + hosted.out_specs,
        out_shape=[jax.ShapeDtypeStruct((s, d), F32), jax.ShapeDtypeStruct((1, d), F32)] + hosted.out_shape,
        scratch_shapes=[pltpu.VMEM((D_IN, d), BF16), pltpu.SemaphoreType.DMA, side_in, side_in, side_out,
                        pltpu.SemaphoreType.DMA((2 * nsub,)), pltpu.SemaphoreType.DMA((2,))] + hosted.scratch,
        compiler_params=_cparams(("arbitrary",)),
    )(*([dproj] * ksplit), wt, x, g_pre, dout, *hosted.inputs)


def _lane_iota(shape):
    return lax.broadcasted_iota(jnp.int32, shape, len(shape) - 1)


def rope_tables(pos_col, freq, sign, comms=()):
    s = pos_col.shape[0]
    tr = _tile(s, (512, 256, 128))
    hosted = _Hosted(comms)

    def body(*refs):
        (pos_ref, freq_ref, sign_ref), (out_ref,), _, phases = hosted.split(refs, 3, 1, 0)
        _before_step(phases, pl.program_id(0), s // tr)
        ang = pos_ref[...].astype(F32) * freq_ref[...]
        out_ref[:, :LANES] = jnp.cos(ang)
        out_ref[:, LANES:] = jnp.sin(ang) * sign_ref[...]
        _after_step(phases, pl.program_id(0), s // tr)

    return pl.pallas_call(
        body, name="rope_tables", grid=(s // tr,),
        in_specs=[pl.BlockSpec((tr, 1), lambda i: (i, 0)), pl.BlockSpec((1, LANES), lambda i: (0, 0)),
                  pl.BlockSpec((1, LANES), lambda i: (0, 0))] + hosted.in_specs,
        out_specs=[pl.BlockSpec((tr, 2 * LANES), lambda i: (i, 0))] + hosted.out_specs,
        out_shape=[jax.ShapeDtypeStruct((s, 2 * LANES), F32)] + hosted.out_shape,
        scratch_shapes=hosted.scratch,
        compiler_params=_cparams(("arbitrary",)),
    )(pos_col, freq, sign, *hosted.inputs)


def _partner(v):
    low = (_lane_iota(v.shape) % HEAD_DIM) < (HEAD_DIM // 2)
    return jnp.where(low, pltpu.roll(v, LANES - HEAD_DIM // 2, 1), pltpu.roll(v, HEAD_DIM // 2, 1))


def _rope(v, cos, sin_signed):
    return v * cos + _partner(v) * sin_signed


def _rope_transposed(dv, cos, sin_signed):
    return dv * cos - _partner(dv) * sin_signed


def _both_halves(v, kv_head):
    keep = (_lane_iota(v.shape) >= HEAD_DIM) if kv_head else (_lane_iota(v.shape) < HEAD_DIM)
    return jnp.where(keep, v, pltpu.roll(v, HEAD_DIM, 1))


def _fold_halves(acc):
    return acc + pltpu.roll(acc, HEAD_DIM, 1)


def _by_half(a, b):
    shape = jnp.broadcast_shapes(jnp.shape(a), jnp.shape(b))
    return jnp.where(_lane_iota(shape) < HEAD_DIM, a, b)


def _stack_heads(pair):
    return jnp.concatenate([_by_half(pair, 0.0), _by_half(0.0, pair)], axis=0)


def _band_bias(has_prev):
    i = lax.broadcasted_iota(jnp.int32, (2 * CHUNK, 2 * CHUNK), 0) % CHUNK
    j = lax.broadcasted_iota(jnp.int32, (2 * CHUNK, 2 * CHUNK), 1)
    band = jnp.logical_and(j > i, j <= i + CHUNK)
    return jnp.where(jnp.logical_and(band, jnp.logical_or(j >= CHUNK, has_prev)), 0.0, NEG)


def _probs(qm2, kk2, bias, sink_col):
    sc = _dot(qm2, kk2, 1, 1) + bias
    mx = jnp.maximum(jnp.max(sc, axis=-1, keepdims=True), sink_col)
    p = jnp.exp(sc - mx)
    es = jnp.exp(sink_col - mx)
    inv = 1.0 / (jnp.sum(p, axis=-1, keepdims=True) + es)
    return p * inv, es * inv


def _probs_staged(qm2s, kk2s, bias, sink_cols):
    k = range(len(qm2s))
    scs = [_dot(qm2s[i], kk2s[i], 1, 1) + bias for i in k]
    mxs = [jnp.maximum(jnp.max(scs[i], axis=-1, keepdims=True), sink_cols[i]) for i in k]
    ps = [jnp.exp(scs[i] - mxs[i]) for i in k]
    ess = [jnp.exp(sink_cols[i] - mxs[i]) for i in k]
    invs = [1.0 / (jnp.sum(ps[i], axis=-1, keepdims=True) + ess[i]) for i in k]
    return [ps[i] * invs[i] for i in k], [ess[i] * invs[i] for i in k]


def _sink_col(sinks_ref, pair):
    row = lax.broadcasted_iota(jnp.int32, (2 * CHUNK, 1), 0)
    return jnp.where(row < CHUNK, sinks_ref[2 * pair], sinks_ref[2 * pair + 1])


def _layer_norm_parts(v):
    mu = jnp.mean(v, axis=-1, keepdims=True)
    xc = v - mu
    rstd = lax.rsqrt(jnp.mean(xc * xc, axis=-1, keepdims=True) + EPS)
    return xc * rstd, rstd


def _masked_spatial(w_ref, g):
    t = lax.broadcasted_iota(jnp.int32, (CHUNK, CHUNK), 0)
    sidx = lax.broadcasted_iota(jnp.int32, (CHUNK, CHUNK), 1)
    return jnp.where(t >= sidx, w_ref[g], 0.0).astype(BF16)


def _keys_values(kv_ref, kvp_ref, rope_ref, ropep_ref):
    cos_c, sin_c = rope_ref[:, :LANES], rope_ref[:, LANES:]
    cos_p, sin_p = ropep_ref[:, :LANES], ropep_ref[:, LANES:]
    k_c = _rope(kv_ref[:, :D_KV].astype(F32), cos_c, sin_c)
    k_p = _rope(kvp_ref[:, :D_KV].astype(F32), cos_p, sin_p)
    keys = jnp.concatenate([k_p, k_c], axis=0)
    vals = jnp.concatenate([kvp_ref[:, D_KV:], kv_ref[:, D_KV:]], axis=0).astype(F32)
    return keys, vals, (cos_c, sin_c, cos_p, sin_p)


def mixer_fwd(proj, rope, ln_g, ln_b, w_sp, b_sp_rows, sinks, comms=()):
    s = proj.shape[0]
    nb = s // CHUNK
    hosted = _Hosted(comms)

    def body(sinks_ref, *refs):
        ((proj_ref, kvp_ref, rope_ref, ropep_ref, lng_ref, lnb_ref, w_ref, b_ref), (cat_ref,), _,
         phases) = hosted.split(refs, 8, 1, 0)
        n = pl.program_id(0)
        _before_step(phases, n, nb)
        xhat, _ = _layer_norm_parts(proj_ref[:, OFF_V:OFF_V + D_GMLP].astype(F32))
        vnb = (xhat * lng_ref[...] + lnb_ref[...]).astype(BF16)
        mixeds = [_dot(_masked_spatial(w_ref, g), vnb[:, g * CHUNK:(g + 1) * CHUNK], 1, 0) + b_ref[g]
                  for g in range(GROUPS)]
        for g in range(GROUPS):
            za = proj_ref[:, OFF_ZA + g * CHUNK:OFF_ZA + (g + 1) * CHUNK].astype(F32)
            u = proj_ref[:, OFF_U + g * CHUNK:OFF_U + (g + 1) * CHUNK].astype(F32)
            cat_ref[:, g * CHUNK:(g + 1) * CHUNK] = (u * mixeds[g] * (za * _sigmoid(za))).astype(BF16)
        kv_ref = proj_ref.at[:, OFF_K:OFF_K + 2 * D_KV]
        keys, vals, (cos_c, sin_c, _, _) = _keys_values(kv_ref, kvp_ref, rope_ref, ropep_ref)
        cos_q, sin_q = cos_c * SCALE, sin_c * SCALE
        bias = _band_bias(n > 0)
        kk2 = [_both_halves(keys, kvh).astype(BF16) for kvh in range(N_KV_HEADS)]
        vv2 = [_both_halves(vals, kvh).astype(BF16) for kvh in range(N_KV_HEADS)]
        pairs = range(N_PAIRS)
        qms = [_stack_heads(_rope(proj_ref[:, OFF_Q + pair * LANES:OFF_Q + (pair + 1) * LANES].astype(F32),
                                  cos_q, sin_q)).astype(BF16) for pair in pairs]
        probs = _probs_staged(qms, [kk2[pair // PAIRS_PER_KV] for pair in pairs], bias,
                              [_sink_col(sinks_ref, pair) for pair in pairs])[0]
        outs = [_dot(probs[pair].astype(BF16), vv2[pair // PAIRS_PER_KV], 1, 0) for pair in pairs]
        for pair in pairs:
            out_pair = _by_half(outs[pair][:CHUNK], outs[pair][CHUNK:])
            zb = proj_ref[:, OFF_ZB + pair * LANES:OFF_ZB + (pair + 1) * LANES].astype(F32)
            cat_ref[:, D_GMLP + pair * LANES:D_GMLP + (pair + 1) * LANES] = (
                out_pair * (zb * _sigmoid(zb))).astype(BF16)
        _after_step(phases, n, nb)

    prev = lambda n, *_: (jnp.maximum(n - 1, 0), 0)
    kv_block = OFF_K // (2 * D_KV)
    return pl.pallas_call(
        body, name="mixer_fwd",
        grid_spec=pltpu.PrefetchScalarGridSpec(
            num_scalar_prefetch=1, grid=(nb,),
            in_specs=[pl.BlockSpec((CHUNK, D_IN), lambda n, *_: (n, 0)),
                      pl.BlockSpec((CHUNK, 2 * D_KV), lambda n, *_: (jnp.maximum(n - 1, 0), kv_block)),
                      pl.BlockSpec((CHUNK, 2 * LANES), lambda n, *_: (n, 0)),
                      pl.BlockSpec((CHUNK, 2 * LANES), prev),
                      pl.BlockSpec((1, D_GMLP), lambda n, *_: (0, 0)),
                      pl.BlockSpec((1, D_GMLP), lambda n, *_: (0, 0)),
                      pl.BlockSpec((GROUPS, CHUNK, CHUNK), lambda n, *_: (0, 0, 0)),
                      pl.BlockSpec((GROUPS, CHUNK, CHUNK), lambda n, *_: (0, 0, 0))] + hosted.in_specs,
            out_specs=[pl.BlockSpec((CHUNK, D_GMLP + D_ATTN), lambda n, *_: (n, 0))] + hosted.out_specs,
            scratch_shapes=hosted.scratch),
        out_shape=[jax.ShapeDtypeStruct((s, D_GMLP + D_ATTN), BF16)] + hosted.out_shape,
        compiler_params=_cparams(("arbitrary",)),
    )(sinks, proj, proj, rope, rope, ln_g, ln_b, w_sp, b_sp_rows, *hosted.inputs)


def mixer_bwd(proj, dcat, rope, ln_g, ln_b, w_sp, b_sp_rows, sinks, comms=()):
    s = proj.shape[0]
    nb = s // CHUNK
    hosted = _Hosted(comms)

    def body(sinks_ref, *refs):
        ((proj_ref, kvp_ref, dcat_ref, rope_ref, ropep_ref, lng_ref, lnb_ref, w_ref, b_ref),
         (dproj_ref, dw_ref, db_ref, dlng_ref, dlnb_ref, dsink_ref),
         (pend_ref, pend_kv_ref, dbacc_ref), phases) = hosted.split(refs, 9, 6, 3)
        n = pl.program_id(0)
        _before_step(phases, n, nb + 1)

        @pl.when(n == 0)
        def _():
            dw_ref[...] = jnp.zeros_like(dw_ref)
            dbacc_ref[...] = jnp.zeros_like(dbacc_ref)
            dlng_ref[...] = jnp.zeros_like(dlng_ref)
            dlnb_ref[...] = jnp.zeros_like(dlnb_ref)
            dsink_ref[...] = jnp.zeros_like(dsink_ref)

        @pl.when(n > 0)
        def _():
            dproj_ref[...] = pend_ref[...]

        def flush(dkv_prev):
            @pl.when(n > 0)
            def _():
                dproj_ref[:, OFF_K:OFF_K + 2 * D_KV] = (pend_kv_ref[...] + dkv_prev).astype(BF16)

        @pl.when(n < nb)
        def _():
            kv_ref = proj_ref.at[:, OFF_K:OFF_K + 2 * D_KV]
            keys, vals, (cos_c, sin_c, cos_p, sin_p) = _keys_values(kv_ref, kvp_ref, rope_ref, ropep_ref)
            cos_q, sin_q = cos_c * SCALE, sin_c * SCALE
            bias = _band_bias(n > 0)
            lane_row = _lane_iota((1, LANES))
            dsink = jnp.zeros((1, LANES), F32)
            dk_heads, dv_heads = [], []
            for kvh in range(N_KV_HEADS):
                kk2 = _both_halves(keys, kvh).astype(BF16)
                vv2 = _both_halves(vals, kvh).astype(BF16)
                pairs = list(range(kvh * PAIRS_PER_KV, (kvh + 1) * PAIRS_PER_KV))
                k4 = range(PAIRS_PER_KV)
                qm2s = [_stack_heads(_rope(proj_ref[:, OFF_Q + pair * LANES:OFF_Q + (pair + 1) * LANES].astype(F32),
                                           cos_q, sin_q)).astype(BF16) for pair in pairs]
                ps, p_sinks = _probs_staged(qm2s, [kk2] * PAIRS_PER_KV, bias,
                                            [_sink_col(sinks_ref, pair) for pair in pairs])
                pbs = [p.astype(BF16) for p in ps]
                o2s = [_dot(pb, vv2, 1, 0) for pb in pbs]
                zbs = [proj_ref[:, OFF_ZB + pair * LANES:OFF_ZB + (pair + 1) * LANES].astype(F32) for pair in pairs]
                sgs = [_sigmoid(zb) for zb in zbs]
                dybs = [dcat_ref[:, D_GMLP + pair * LANES:D_GMLP + (pair + 1) * LANES].astype(F32) for pair in pairs]
                for i, pair in enumerate(pairs):
                    out_pair = _by_half(o2s[i][:CHUNK], o2s[i][CHUNK:])
                    pend_ref[:, OFF_ZB + pair * LANES:OFF_ZB + (pair + 1) * LANES] = (
                        dybs[i] * out_pair * (sgs[i] * (1.0 + zbs[i] * (1.0 - sgs[i])))).astype(BF16)
                dom2s = [_stack_heads(dybs[i] * (zbs[i] * sgs[i])).astype(BF16) for i in k4]
                dps = [_dot(dom2, vv2, 1, 1) for dom2 in dom2s]
                deltas = [jnp.sum(ps[i] * dps[i], axis=-1, keepdims=True) for i in k4]
                dss = [ps[i] * (dps[i] - deltas[i]) for i in k4]
                for i, pair in enumerate(pairs):
                    dsk = -(p_sinks[i] * deltas[i])
                    dsink = dsink + jnp.where(lane_row == 2 * pair,
                                              jnp.sum(dsk[:CHUNK], axis=0, keepdims=True), 0.0)
                    dsink = dsink + jnp.where(lane_row == 2 * pair + 1,
                                              jnp.sum(dsk[CHUNK:], axis=0, keepdims=True), 0.0)
                dsbs = [ds.astype(BF16) for ds in dss]
                dq2s = [_dot(dsb, kk2, 1, 0) for dsb in dsbs]
                for pair, dq2 in zip(pairs, dq2s):
                    pend_ref[:, OFF_Q + pair * LANES:OFF_Q + (pair + 1) * LANES] = _rope_transposed(
                        _by_half(dq2[:CHUNK], dq2[CHUNK:]), cos_q, sin_q).astype(BF16)
                dkks = [_dot(dsbs[i], qm2s[i], 0, 0) for i in k4]
                dvvs = [_dot(pbs[i], dom2s[i], 0, 0) for i in k4]
                dk_heads.append(_fold_halves((dkks[0] + dkks[1]) + (dkks[2] + dkks[3])))
                dv_heads.append(_fold_halves((dvvs[0] + dvvs[1]) + (dvvs[2] + dvvs[3])))
            dk_rot = _by_half(dk_heads[0], dk_heads[1])
            dv_all = _by_half(dv_heads[0], dv_heads[1])
            dk_p = _rope_transposed(dk_rot[:CHUNK], cos_p, sin_p)
            dk_c = _rope_transposed(dk_rot[CHUNK:], cos_c, sin_c)
            flush(jnp.concatenate([dk_p, dv_all[:CHUNK]], axis=1))
            dsink_ref[...] += dsink
            pend_kv_ref[...] = jnp.concatenate([dk_c, dv_all[CHUNK:]], axis=1)
            xhat, rstd = _layer_norm_parts(proj_ref[:, OFF_V:OFF_V + D_GMLP].astype(F32))
            lng = lng_ref[...]
            vnb = (xhat * lng + lnb_ref[...]).astype(BF16)
            dvn_cols = []
            for g in range(GROUPS):
                cols = slice(g * CHUNK, (g + 1) * CHUNK)
                wm = _masked_spatial(w_ref, g)
                mixed = _dot(wm, vnb[:, cols], 1, 0) + b_ref[g]
                za = proj_ref[:, OFF_ZA + g * CHUNK:OFF_ZA + (g + 1) * CHUNK].astype(F32)
                u = proj_ref[:, OFF_U + g * CHUNK:OFF_U + (g + 1) * CHUNK].astype(F32)
                dya = dcat_ref[:, cols].astype(F32)
                sg = _sigmoid(za)
                sz = za * sg
                pend_ref[:, OFF_U + g * CHUNK:OFF_U + (g + 1) * CHUNK] = (dya * mixed * sz).astype(BF16)
                pend_ref[:, OFF_ZA + g * CHUNK:OFF_ZA + (g + 1) * CHUNK] = (
                    dya * u * mixed * (sg * (1.0 + za * (1.0 - sg)))).astype(BF16)
                dmixed = dya * u * sz
                dmb = dmixed.astype(BF16)
                dbacc_ref[g] += dmixed
                dw_ref[g] += _dot(dmb, vnb[:, cols], 1, 1)
                dvn_cols.append(_dot(wm, dmb, 0, 0))
            dvn = jnp.concatenate(dvn_cols, axis=1)
            dlng_ref[...] += jnp.sum(dvn * xhat, axis=0, keepdims=True)
            dlnb_ref[...] += jnp.sum(dvn, axis=0, keepdims=True)
            dxh = dvn * lng
            dv = rstd * (dxh - jnp.mean(dxh, axis=-1, keepdims=True)
                         - xhat * jnp.mean(dxh * xhat, axis=-1, keepdims=True))
            pend_ref[:, OFF_V:OFF_V + D_GMLP] = dv.astype(BF16)

        @pl.when(n == nb)
        def _():
            flush(jnp.zeros((CHUNK, 2 * D_KV), F32))
            t = lax.broadcasted_iota(jnp.int32, (CHUNK, CHUNK), 0)
            sidx = lax.broadcasted_iota(jnp.int32, (CHUNK, CHUNK), 1)
            lane = _lane_iota((CHUNK, LANES))
            dbt = jnp.zeros((CHUNK, LANES), F32)
            for g in range(GROUPS):
                dw_ref[g] = jnp.where(t >= sidx, dw_ref[g], 0.0)
                dbt = jnp.where(lane == g, jnp.sum(dbacc_ref[g], axis=-1, keepdims=True), dbt)
            db_ref[...] = jnp.transpose(dbt)[:GROUPS, :]

        _after_step(phases, n, nb + 1)

    cur = lambda n, *_: (jnp.minimum(n, nb - 1), 0)
    prev = lambda n, *_: (jnp.clip(n - 1, 0, nb - 1), 0)
    kv_block = OFF_K // (2 * D_KV)
    const2 = lambda n, *_: (0, 0)
    const3 = lambda n, *_: (0, 0, 0)
    return pl.pallas_call(
        body, name="mixer_bwd",
        grid_spec=pltpu.PrefetchScalarGridSpec(
            num_scalar_prefetch=1, grid=(nb + 1,),
            in_specs=[pl.BlockSpec((CHUNK, D_IN), cur),
                      pl.BlockSpec((CHUNK, 2 * D_KV), lambda n, *_: (jnp.clip(n - 1, 0, nb - 1), kv_block)),
                      pl.BlockSpec((CHUNK, D_GMLP + D_ATTN), cur),
                      pl.BlockSpec((CHUNK, 2 * LANES), cur),
                      pl.BlockSpec((CHUNK, 2 * LANES), prev),
                      pl.BlockSpec((1, D_GMLP), const2),
                      pl.BlockSpec((1, D_GMLP), const2),
                      pl.BlockSpec((GROUPS, CHUNK, CHUNK), const3),
                      pl.BlockSpec((GROUPS, CHUNK, CHUNK), const3)] + hosted.in_specs,
            out_specs=[pl.BlockSpec((CHUNK, D_IN), lambda n, *_: (jnp.maximum(n - 1, 0), 0)),
                       pl.BlockSpec((GROUPS, CHUNK, CHUNK), const3),
                       pl.BlockSpec((GROUPS, CHUNK), const2),
                       pl.BlockSpec((1, D_GMLP), const2),
                       pl.BlockSpec((1, D_GMLP), const2),
                       pl.BlockSpec((1, LANES), const2)] + hosted.out_specs,
            scratch_shapes=[pltpu.VMEM((CHUNK, D_IN), BF16), pltpu.VMEM((CHUNK, 2 * D_KV), F32),
                            pltpu.VMEM((GROUPS, CHUNK, CHUNK), F32)] + hosted.scratch),
        out_shape=[jax.ShapeDtypeStruct((s, D_IN), BF16),
                   jax.ShapeDtypeStruct((GROUPS, CHUNK, CHUNK), F32),
                   jax.ShapeDtypeStruct((GROUPS, CHUNK), F32),
                   jax.ShapeDtypeStruct((1, D_GMLP), F32),
                   jax.ShapeDtypeStruct((1, D_GMLP), F32),
                   jax.ShapeDtypeStruct((1, LANES), F32)] + hosted.out_shape,
        compiler_params=_cparams(("arbitrary",)),
    )(sinks, proj, proj, dcat, rope, rope, ln_g, ln_b, w_sp, b_sp_rows, *hosted.inputs)


def _adamw_math(w, g, m, v):
    m = ADAM_B1 * m + (1.0 - ADAM_B1) * g
    v = ADAM_B2 * v + (1.0 - ADAM_B2) * (g * g)
    m_hat = m / (1.0 - ADAM_B1 ** ADAM_STEP)
    v_hat = v / (1.0 - ADAM_B2 ** ADAM_STEP)
    delta = -ADAM_LR * (m_hat / (jnp.sqrt(v_hat) + ADAM_EPS) + ADAM_WD * w)
    return delta, m, v


def adamw_shard(terms, w, m, v, name):
    r, c = w.shape
    tr = _tile(r, (224, 256, 128, 8))
    n_terms = len(terms)

    def body(*refs):
        w_ref, m_ref, v_ref, g_ref, d_ref, nm_ref, nv_ref = refs[n_terms:]
        g = None
        for ref, (_, slots) in zip(refs[:n_terms], terms):
            for k in range(slots):
                part = ref[k].astype(F32)
                g = part if g is None else g + part
        g_ref[...] = g
        d_ref[...], nm_ref[...], nv_ref[...] = _adamw_math(w_ref[...], g, m_ref[...], v_ref[...])

    spec = pl.BlockSpec((tr, c), lambda i: (i, 0))
    return pl.pallas_call(
        body, name=name, grid=(r // tr,),
        in_specs=[pl.BlockSpec((slots, tr, c), lambda i: (0, i, 0)) for _, slots in terms] + [spec] * 3,
        out_specs=[spec] * 4, out_shape=[jax.ShapeDtypeStruct((r, c), F32)] * 4,
        compiler_params=_cparams(("arbitrary",)),
    )(*[a for a, _ in terms], w, m, v)


def adamw_small(gathered, lane_windows, params):
    n_par = len(params)

    def body(*refs):
        g_refs = refs[:n_par + 1]
        wmv_refs = refs[n_par + 1:4 * n_par + 1]
        out_refs = refs[4 * n_par + 1:]

        def total(ref):
            acc = ref[0]
            for dev in range(1, N_DEV):
                acc = acc + ref[dev]
            return acc

        for i in range(n_par):
            w_ref, m_ref, v_ref = wmv_refs[3 * i:3 * i + 3]
            g = total(g_refs[i])
            if lane_windows[i] is not None:
                start, size = lane_windows[i]
                g = g[..., start:start + size]
            delta, new_m, new_v = _adamw_math(w_ref[...], g, m_ref[...], v_ref[...])
            for ref, val in zip(out_refs[4 * i:4 * i + 4], (g, delta, new_m, new_v)):
                ref[...] = val
        out_refs[4 * n_par][...] = total(g_refs[n_par])

    flat = [a for wmv in params for a in wmv]
    out_shape = [jax.ShapeDtypeStruct(w.shape, F32) for (w, _, _) in params for _ in range(4)]
    out_shape.append(jax.ShapeDtypeStruct(gathered[-1].shape[1:], F32))
    outs = pl.pallas_call(body, name="adamw_small", out_shape=out_shape, compiler_params=_cparams())(*gathered, *flat)
    return [tuple(outs[4 * i:4 * i + 4]) for i in range(n_par)], outs[-1]


def kernel(x, positions, g_pre, w_in, b_qkv, ln_v_g, ln_v_b, w_spatial, b_spatial, attn_sinks, w_out, g_post, loss_target, m_g_pre, m_w_in, m_b_qkv, m_ln_v_g, m_ln_v_b, m_w_spatial, m_b_spatial, m_attn_sinks, m_w_out, m_g_post, v_g_pre, v_w_in, v_b_qkv, v_ln_v_g, v_ln_v_b, v_w_spatial, v_b_spatial, v_attn_sinks, v_w_out, v_g_post):
    x2, target = x[0], loss_target[0]
    seq = x2.shape[0]
    xi, yi, ci = _my_place()

    wt_shard = w_in[0].T.astype(BF16)
    wo_shard = w_out[0].astype(BF16)
    pos_col = positions.reshape(seq, 1)
    half = HEAD_DIM // 2
    inv_freq = ROPE_THETA ** (-jnp.arange(half, dtype=F32) * (2.0 / HEAD_DIM))
    freq = jnp.tile(inv_freq, LANES // half).reshape(1, LANES)
    sign = jnp.tile(jnp.concatenate([-jnp.ones((half,), F32), jnp.ones((half,), F32)]), LANES // HEAD_DIM)
    sign = sign.reshape(1, LANES)
    bias = jnp.concatenate([jnp.zeros((1, OFF_Q), F32), b_qkv, jnp.zeros((1, D_ATTN), F32)], axis=1)
    proj, h, rope, wt = in_proj_gather(x2, pos_col, freq, sign, g_pre, wt_shard, bias)

    b_rows = jnp.broadcast_to(b_spatial[0][:, :, None], (GROUPS, CHUNK, CHUNK))
    sinks = attn_sinks[0]
    cat, wo = mixer_fwd(proj, rope, ln_v_g, ln_v_b, w_spatial[0], b_rows, sinks, comms=[gather_comm([wo_shard])])
    dy, dout, d_g_post, loss_part = out_proj_loss(cat, wo, x2, target, g_post)

    dcat = matmul_nt(dy, wo, "out_proj_bwd")
    d_wo, _ = matmul_tn(cat, dy, 512, "w_out_grad")
    dproj, d_w_sp, d_b_sp, d_ln_g, d_ln_b, d_sinks, parts_wo = mixer_bwd(
        proj, dcat, rope, ln_v_g, ln_v_b, w_spatial[0], b_rows, sinks, comms=[scatter_comm([d_wo])])
    small_parts = [d_ln_g, d_ln_b, d_w_sp, d_b_sp, d_sinks, d_g_post, loss_part]
    d_wt, colsum, *landed = matmul_tn(dproj, h, 768, "w_in_grad", comms=[gather_comm(small_parts, stack=True)])

    owners = jnp.stack([4 * cx + 2 * cy + ci for cx, cy in (_chip_of(xi, yi, r) for r in range(4))]).astype(jnp.int32)
    (got_wt,) = run_comm(pair_comm([d_wt]), "grad_exchange_pair")
    sum_wt = pair_sum(d_wt, got_wt, owners, "grad_pair_sum_w_in")
    grad_x, d_g_pre, far_wt = in_proj_bwd(dproj, wt, x2, g_pre, dout, comms=[chips_comm([sum_wt])])
    late = run_comm(gather_comm([d_g_pre, colsum], stack=True, direct=True), "allgather_late_grads")
    gathered = late + landed
    windows = [None, (OFF_Q, D_QKV), None, None, None, None, (0, N_Q_HEADS), None]
    small = [(g_pre, m_g_pre, v_g_pre), (b_qkv, m_b_qkv, v_b_qkv), (ln_v_g, m_ln_v_g, v_ln_v_g),
             (ln_v_b, m_ln_v_b, v_ln_v_b), (w_spatial[0], m_w_spatial[0], v_w_spatial[0]),
             (b_spatial[0], m_b_spatial[0], v_b_spatial[0]), (attn_sinks, m_attn_sinks, v_attn_sinks),
             (g_post, m_g_post, v_g_post)]
    small_out, loss_row = adamw_small(gathered, windows, small)
    lead = [False, False, False, False, True, True, False, False]
    small_out = [tuple(a[None] if ld else a for a in leaf) for leaf, ld in zip(small_out, lead)]

    wt_out = adamw_shard([(sum_wt, 1), (far_wt, 3)], w_in[0].T, m_w_in[0].T, v_w_in[0].T, "adamw_w_in")
    wo_out = adamw_shard([(parts_wo, N_DEV)], w_out[0], m_w_out[0], v_w_out[0], "adamw_w_out")

    def leaves(k):
        gp, bq, lg, lb, ws, bs, sk, gpo = (leaf[k] for leaf in small_out)
        return [gp, wt_out[k].T[None], bq, lg, lb, ws, bs, sk, wo_out[k][None], gpo]

    return (loss_row[0, 0], grad_x[None], *leaves(0), *leaves(1), *leaves(2), *leaves(3))
```

```python
import functools

import jax
import jax.numpy as jnp
from jax import lax
from jax.experimental import pallas as pl
from jax.experimental.pallas import tpu as pltpu

F32 = jnp.float32
BF16 = jnp.bfloat16

D_MODEL = 2048
D_GMLP = 1024
D_ATTN = 1024
CHUNK = 128
GROUPS = 8
HEAD_DIM = 64
N_Q_HEADS = 16
N_KV_HEADS = 2
D_KV = N_KV_HEADS * HEAD_DIM
D_IN = 3 * D_GMLP + D_ATTN + 2 * D_KV + D_ATTN
OFF_U, OFF_V, OFF_ZA = 0, D_GMLP, 2 * D_GMLP
OFF_Q = 3 * D_GMLP
OFF_K = OFF_Q + D_ATTN
OFF_VA = OFF_K + D_KV
OFF_ZB = OFF_VA + D_KV
D_QKV = D_ATTN + 2 * D_KV
ROPE_THETA = 10000.0
EPS = 1e-6
SCALE = HEAD_DIM ** -0.5
NEG = -1e30
N_PAIRS = N_Q_HEADS // 2
PAIRS_PER_KV = N_PAIRS // N_KV_HEADS

ADAM_LR = 0.001
ADAM_B1 = 0.9
ADAM_B2 = 0.999
ADAM_EPS = 1e-08
ADAM_WD = 0.01
ADAM_STEP = 10

N_DEV = 8
LANES = 128
VMEM_LIMIT = 56 * 1024 * 1024

MESH = pl.DeviceIdType.MESH
ANY = pl.BlockSpec(memory_space=pl.ANY)


def _cparams(sem=None):
    return pltpu.CompilerParams(dimension_semantics=sem, vmem_limit_bytes=VMEM_LIMIT)


def _tile(n, prefs):
    for t in prefs:
        if n % t == 0:
            return t
    return n


def _sigmoid(z):
    return 1.0 / (1.0 + jnp.exp(-z))


def _dot(a, b, ca, cb):
    return lax.dot_general(a, b, (((ca,), (cb,)), ((), ())), preferred_element_type=F32)


def _my_place():
    return lax.axis_index("x"), lax.axis_index("y"), lax.axis_index("c")


def _chip_of(x, y, r):
    return (x ^ (r & 1), y ^ (r >> 1))


def _peer(x, y, c, k):
    return (x ^ (k >> 2), y ^ ((k >> 1) & 1), c ^ (k & 1))


def _index(px, py, pc):
    return 4 * px + 2 * py + pc


class _Comm:
    def __init__(self, inputs, out_shape, scratch, bind):
        self.inputs, self.out_shape, self.scratch, self.bind = list(inputs), list(out_shape), list(scratch), bind


def gather_comm(shards, stack=False, direct=False):
    n_arr = len(shards)

    def bind(ins, outs, sems):
        send_sems, recv_sems, local_sems = sems
        x, y, c = _my_place()
        me, sibling = (x, y, c), (x, y, 1 - c)
        chips = [_chip_of(x, y, r) for r in (1, 2, 3)]

        def rows(a, px, py, pc):
            d = _index(px, py, pc)
            if stack:
                return outs[a].at[d]
            m = shards[a].shape[0]
            return outs[a].at[pl.ds(pl.multiple_of(d * m, 8), m), :]

        def copy(a, k, block, to, src=None):
            return pltpu.make_async_remote_copy(
                src_ref=rows(a, *block) if src is None else src, dst_ref=rows(a, *block),
                send_sem=send_sems.at[a * 7 + k], recv_sem=recv_sems.at[a * 7 + k],
                device_id=to, device_id_type=MESH)

        def mine(a):
            return pltpu.make_async_copy(ins[a], rows(a, *me), local_sems.at[a])

        def own_sends(a):
            if direct:
                return [copy(a, k - 1, me, _peer(x, y, c, k), src=ins[a]) for k in range(1, 8)]
            return ([copy(a, 0, me, sibling, src=ins[a])]
                    + [copy(a, 1 + j, me, (*chip, c), src=ins[a]) for j, chip in enumerate(chips)])

        def start():
            for a in range(n_arr):
                mine(a).start()
                for cp in own_sends(a):
                    cp.start()

        def relay():
            if direct:
                return
            for j, chip in enumerate(chips):
                for a in range(n_arr):
                    copy(a, 1 + j, (*chip, c), me).wait_recv()
                    copy(a, 4 + j, (*chip, c), sibling).start()

        def finish():
            for a in range(n_arr):
                if direct:
                    for k in range(1, 8):
                        copy(a, k - 1, _peer(x, y, c, k), me).wait_recv()
                else:
                    copy(a, 0, sibling, me).wait_recv()
                    for j, chip in enumerate(chips):
                        copy(a, 4 + j, (*chip, 1 - c), me).wait_recv()
                        copy(a, 4 + j, (*chip, c), sibling).wait_send()
                for cp in own_sends(a):
                    cp.wait_send()
                mine(a).wait()

        return start, relay, finish

    def gathered(s):
        return (N_DEV, *s.shape) if stack else (N_DEV * s.shape[0], s.shape[1])

    return _Comm(shards, [jax.ShapeDtypeStruct(gathered(s), s.dtype) for s in shards],
                 [pltpu.SemaphoreType.DMA((7 * n_arr,)), pltpu.SemaphoreType.DMA((7 * n_arr,)),
                  pltpu.SemaphoreType.DMA((n_arr,))], bind)


def scatter_comm(parts):
    n_arr = len(parts)

    def bind(ins, outs, sems):
        send_sems, recv_sems, local_sems = sems
        x, y, c = _my_place()
        my_index = _index(x, y, c)

        def block(a, d):
            m = parts[a].shape[0] // N_DEV
            return ins[a].at[pl.ds(pl.multiple_of(d * m, 16), m), :]

        def copy(a, k, slot):
            peer = _peer(x, y, c, k)
            return pltpu.make_async_remote_copy(
                src_ref=block(a, _index(*peer)), dst_ref=outs[a].at[slot],
                send_sem=send_sems.at[a * 7 + k - 1], recv_sem=recv_sems.at[a * 7 + k - 1],
                device_id=peer, device_id_type=MESH)

        def mine(a):
            return pltpu.make_async_copy(block(a, my_index), outs[a].at[my_index], local_sems.at[a])

        def start():
            for a in range(n_arr):
                mine(a).start()
                for k in range(1, 8):
                    copy(a, k, my_index).start()

        def finish():
            for a in range(n_arr):
                for k in range(1, 8):
                    copy(a, k, _index(*_peer(x, y, c, k))).wait_recv()
                    copy(a, k, my_index).wait_send()
                mine(a).wait()

        return start, (lambda: None), finish

    return _Comm(parts, [jax.ShapeDtypeStruct((N_DEV, p.shape[0] // N_DEV, p.shape[1]), p.dtype) for p in parts],
                 [pltpu.SemaphoreType.DMA((7 * n_arr,)), pltpu.SemaphoreType.DMA((7 * n_arr,)),
                  pltpu.SemaphoreType.DMA((n_arr,))], bind)


def pair_comm(parts):
    n_arr = len(parts)

    def bind(ins, outs, sems):
        send_sems, recv_sems = sems
        x, y, c = _my_place()

        def copies():
            out = []
            for a in range(n_arr):
                m = parts[a].shape[0] // N_DEV
                for r in range(4):
                    owner = _index(*_chip_of(x, y, r), 1 - c)
                    out.append(pltpu.make_async_remote_copy(
                        src_ref=ins[a].at[pl.ds(pl.multiple_of(owner * m, 16), m), :], dst_ref=outs[a].at[r],
                        send_sem=send_sems.at[a * 4 + r], recv_sem=recv_sems.at[a * 4 + r],
                        device_id=(x, y, 1 - c), device_id_type=MESH))
            return out

        def start():
            for cp in copies():
                cp.start()

        def finish():
            for cp in copies():
                cp.wait_recv()
                cp.wait_send()

        return start, (lambda: None), finish

    return _Comm(parts, [jax.ShapeDtypeStruct((4, p.shape[0] // N_DEV, p.shape[1]), p.dtype) for p in parts],
                 [pltpu.SemaphoreType.DMA((4 * n_arr,)), pltpu.SemaphoreType.DMA((4 * n_arr,))], bind)


def chips_comm(sums):
    n_arr = len(sums)

    def bind(ins, outs, sems):
        send_sems, recv_sems = sems
        x, y, c = _my_place()

        def copies():
            return [pltpu.make_async_remote_copy(
                src_ref=ins[a].at[r], dst_ref=outs[a].at[r - 1],
                send_sem=send_sems.at[a * 3 + r - 1], recv_sem=recv_sems.at[a * 3 + r - 1],
                device_id=(*_chip_of(x, y, r), c), device_id_type=MESH) for a in range(n_arr) for r in (1, 2, 3)]

        def start():
            for cp in copies():
                cp.start()

        def finish():
            for cp in copies():
                cp.wait_recv()
                cp.wait_send()

        return start, (lambda: None), finish

    return _Comm(sums, [jax.ShapeDtypeStruct((3,) + s.shape[1:], s.dtype) for s in sums],
                 [pltpu.SemaphoreType.DMA((3 * n_arr,)), pltpu.SemaphoreType.DMA((3 * n_arr,))], bind)


def run_comm(comm, name):
    n_in, n_out = len(comm.inputs), len(comm.out_shape)

    def body(*refs):
        start, relay, finish = comm.bind(refs[:n_in], refs[n_in:n_in + n_out], refs[n_in + n_out:])
        start()
        relay()
        finish()

    outs = pl.pallas_call(body, name=name, out_shape=comm.out_shape, in_specs=[ANY] * n_in,
                          out_specs=[ANY] * n_out, scratch_shapes=comm.scratch)(*comm.inputs)
    return list(outs)


class _Hosted:
    def __init__(self, comms):
        self.comms = list(comms)
        self.inputs = [a for cm in self.comms for a in cm.inputs]
        self.out_shape = [s for cm in self.comms for s in cm.out_shape]
        self.scratch = [s for cm in self.comms for s in cm.scratch]
        self.in_specs = [ANY] * len(self.inputs)
        self.out_specs = [ANY] * len(self.out_shape)

    def split(self, refs, n_in, n_out, n_scratch):
        ni, no = len(self.inputs), len(self.out_shape)
        ins, rest = refs[:n_in], refs[n_in:]
        c_ins, rest = rest[:ni], rest[ni:]
        outs, rest = rest[:n_out], rest[n_out:]
        c_outs, rest = rest[:no], rest[no:]
        scratch, c_sems = rest[:n_scratch], rest[n_scratch:]
        phases = []
        for cm in self.comms:
            a, b, s = len(cm.inputs), len(cm.out_shape), len(cm.scratch)
            phases.append(cm.bind(c_ins[:a], c_outs[:b], c_sems[:s]))
            c_ins, c_outs, c_sems = c_ins[a:], c_outs[b:], c_sems[s:]
        return ins, outs, scratch, phases


def _before_step(phases, step, n_steps):
    if not phases:
        return

    @pl.when(step == 0)
    def _():
        for start, _, _ in phases:
            start()

    @pl.when(step == n_steps // 2)
    def _():
        for _, relay, _ in phases:
            relay()


def _after_step(phases, step, n_steps):
    if not phases:
        return

    @pl.when(step == n_steps - 1)
    def _():
        for _, _, finish in phases:
            finish()


def pair_sum(part, got, owners, name):
    m, n = got.shape[1:]

    def body(own_ref, mine_ref, got_ref, out_ref):
        del own_ref
        out_ref[...] = (mine_ref[...].astype(F32) + got_ref[...].astype(F32)).astype(out_ref.dtype)

    return pl.pallas_call(
        body, name=name,
        grid_spec=pltpu.PrefetchScalarGridSpec(
            num_scalar_prefetch=1, grid=(4,),
            in_specs=[pl.BlockSpec((m, n), lambda r, own: (own[r], 0)),
                      pl.BlockSpec((None, m, n), lambda r, own: (r, 0, 0))],
            out_specs=pl.BlockSpec((None, m, n), lambda r, own: (r, 0, 0))),
        out_shape=jax.ShapeDtypeStruct((4, m, n), got.dtype),
        compiler_params=_cparams(("arbitrary",)),
    )(owners, part, got)


def in_proj(x, g_pre, wt, bias, comms=()):
    s, d = x.shape
    tm = _tile(s, (512, 256, 128))
    tn = 768
    ni, nj = s // tm, D_IN // tn
    hosted = _Hosted(comms)

    def body(*refs):
        (x_ref, g_ref, w_ref, b_ref), (proj_ref, h_ref), _, phases = hosted.split(refs, 4, 2, 0)
        step = pl.program_id(0) * nj + pl.program_id(1)
        _before_step(phases, step, ni * nj)

        @pl.when(pl.program_id(1) == 0)
        def _():
            xv = x_ref[...]
            r = lax.rsqrt(jnp.mean(xv * xv, axis=-1, keepdims=True) + EPS)
            h_ref[...] = (xv * r * g_ref[...]).astype(BF16)

        acc = _dot(h_ref[...], w_ref[...], 1, 1)
        proj_ref[...] = (acc + b_ref[...]).astype(BF16)
        _after_step(phases, step, ni * nj)

    return pl.pallas_call(
        body, name="in_proj", grid=(ni, nj),
        in_specs=[pl.BlockSpec((tm, d), lambda i, j: (i, 0)),
                  pl.BlockSpec((1, d), lambda i, j: (0, 0)),
                  pl.BlockSpec((tn, d), lambda i, j: (j, 0)),
                  pl.BlockSpec((1, tn), lambda i, j: (0, j))] + hosted.in_specs,
        out_specs=[pl.BlockSpec((tm, tn), lambda i, j: (i, j)),
                   pl.BlockSpec((tm, d), lambda i, j: (i, 0))] + hosted.out_specs,
        out_shape=[jax.ShapeDtypeStruct((s, D_IN), BF16), jax.ShapeDtypeStruct((s, d), BF16)] + hosted.out_shape,
        scratch_shapes=hosted.scratch,
        compiler_params=_cparams(("arbitrary", "arbitrary")),
    )(x, g_pre, wt, bias, *hosted.inputs)


def in_proj_gather(x, pos_col, freq, sign, g_pre, wt_shard, bias):
    s, d = x.shape
    tm = _tile(s, (512, 256, 128))
    nt = s // tm
    m = wt_shard.shape[0]
    half = D_IN // 2
    xi = lax.axis_index("x")
    order = jnp.stack([xi, 1 - xi]).astype(jnp.int32)

    def body(order_ref, x_ref, pos_ref, freq_ref, sign_ref, g_ref, b_ref, shard_ref,
             proj_ref, h_ref, rope_ref, wt_ref, w_vmem, send_sems, recv_sems, local_sems):
        del order_ref
        p, i = pl.program_id(0), pl.program_id(1)
        xx, yy, cc = _my_place()
        me, sibling = (xx, yy, cc), (xx, yy, 1 - cc)
        chips = [_chip_of(xx, yy, r) for r in (1, 2, 3)]

        def rows(px, py, pc):
            return wt_ref.at[pl.ds(pl.multiple_of(_index(px, py, pc) * m, 16), m), :]

        def copy(k, block, to, src=None):
            return pltpu.make_async_remote_copy(
                src_ref=rows(*block) if src is None else src, dst_ref=rows(*block),
                send_sem=send_sems.at[k], recv_sem=recv_sems.at[k], device_id=to, device_id_type=MESH)

        def mine():
            return pltpu.make_async_copy(shard_ref, rows(*me), local_sems.at[0])

        def to_sibling():
            return copy(0, me, sibling, src=shard_ref)

        def to_chip(j):
            return copy(1 + j, me, (*chips[j], cc), src=shard_ref)

        def relay(j):
            copy(1 + j, (*chips[j], cc), me).wait_recv()
            copy(4 + j, (*chips[j], cc), sibling).start()

        def relayed(j):
            copy(4 + j, (*chips[j], 1 - cc), me).wait_recv()

        def load_half(which, slot):
            rows_of_half = wt_ref.at[pl.ds(pl.multiple_of(which * half, 16), half), :]
            load = pltpu.make_async_copy(rows_of_half, w_vmem.at[slot], local_sems.at[1 + slot])
            load.start()
            load.wait()

        @pl.when(jnp.logical_and(p == 0, i == 0))
        def _():
            mine().start()
            to_sibling().start()
            to_chip(1).start()
            to_chip(0).start()
            copy(0, sibling, me).wait_recv()
            relay(1)
            relayed(1)
            mine().wait()
            to_chip(1).wait_send()
            to_chip(0).wait_send()
            to_chip(2).start()
            load_half(xx, 0)

        @pl.when(jnp.logical_and(p == 1, i == 0))
        def _():
            relayed(0)
            relayed(2)
            load_half(1 - xx, 1)

        xv = x_ref[...]
        r = lax.rsqrt(jnp.mean(xv * xv, axis=-1, keepdims=True) + EPS)
        hb = (xv * r * g_ref[...]).astype(BF16)
        proj_ref[...] = (_dot(hb, w_vmem[p], 1, 1) + b_ref[...]).astype(BF16)

        @pl.when(p == 0)
        def _():
            h_ref[...] = hb
            ang = pos_ref[...].astype(F32) * freq_ref[...]
            rope_ref[:, :LANES] = jnp.cos(ang)
            rope_ref[:, LANES:] = jnp.sin(ang) * sign_ref[...]

        @pl.when(jnp.logical_and(p == 0, i == 1))
        def _():
            relay(0)

        @pl.when(jnp.logical_and(p == 0, i == nt - 1))
        def _():
            relay(2)

        @pl.when(jnp.logical_and(p == 1, i == nt - 1))
        def _():
            to_sibling().wait_send()
            to_chip(2).wait_send()
            for j in range(3):
                copy(4 + j, (*chips[j], cc), sibling).wait_send()

    const = lambda p, i, o: (0, 0)
    once = lambda p, i, o: (jnp.where(p == 0, i, nt - 1), 0)
    return pl.pallas_call(
        body, name="in_proj_gather",
        grid_spec=pltpu.PrefetchScalarGridSpec(
            num_scalar_prefetch=1, grid=(2, nt),
            in_specs=[pl.BlockSpec((tm, d), lambda p, i, o: (i, 0)),
                      pl.BlockSpec((tm, 1), lambda p, i, o: (i, 0)),
                      pl.BlockSpec((1, LANES), const),
                      pl.BlockSpec((1, LANES), const),
                      pl.BlockSpec((1, d), const),
                      pl.BlockSpec((1, half), lambda p, i, o: (0, o[p])),
                      ANY],
            out_specs=[pl.BlockSpec((tm, half), lambda p, i, o: (i, o[p])),
                       pl.BlockSpec((tm, d), once),
                       pl.BlockSpec((tm, 2 * LANES), once),
                       ANY],
            scratch_shapes=[pltpu.VMEM((2, half, d), BF16), pltpu.SemaphoreType.DMA((7,)),
                            pltpu.SemaphoreType.DMA((7,)), pltpu.SemaphoreType.DMA((3,))]),
        out_shape=[jax.ShapeDtypeStruct((s, D_IN), BF16), jax.ShapeDtypeStruct((s, d), BF16),
                   jax.ShapeDtypeStruct((s, 2 * LANES), F32), jax.ShapeDtypeStruct((D_IN, d), BF16)],
        compiler_params=_cparams(("arbitrary", "arbitrary")),
    )(order, x, pos_col, freq, sign, g_pre, bias, wt_shard)


def out_proj_loss(cat, w_out, x, target, g_post):
    s, d = x.shape
    tm = _tile(s, (256, 128))

    def body(cat_ref, w_ref, x_ref, t_ref, g_ref, dy_ref, dout_ref, dg_ref, loss_ref):
        @pl.when(pl.program_id(0) == 0)
        def _():
            dg_ref[...] = jnp.zeros_like(dg_ref)
            loss_ref[...] = jnp.zeros_like(loss_ref)

        g = g_ref[...]
        ys = [_dot(cat_ref[c0:c0 + CHUNK, :], w_ref[...], 1, 0) for c0 in range(0, tm, CHUNK)]
        for c0 in range(0, tm, CHUNK):
            rows = slice(c0, c0 + CHUNK)
            yv = ys[c0 // CHUNK]
            r = lax.rsqrt(jnp.mean(yv * yv, axis=-1, keepdims=True) + EPS)
            nrm = yv * r
            err = x_ref[rows, :] + nrm * g - t_ref[rows, :]
            loss_ref[...] += 0.5 * jnp.sum(jnp.sum(err * err, axis=-1, keepdims=True), axis=0, keepdims=True) / d
            dout = err * (1.0 / d)
            dout_ref[rows, :] = dout
            dg_ref[...] += jnp.sum(dout * nrm, axis=0, keepdims=True)
            dn = dout * g
            dy = r * (dn - nrm * jnp.mean(dn * nrm, axis=-1, keepdims=True))
            dy_ref[rows, :] = dy.astype(BF16)

    return pl.pallas_call(
        body, name="out_proj_loss", grid=(s // tm,),
        in_specs=[pl.BlockSpec((tm, d), lambda i: (i, 0)),
                  pl.BlockSpec((d, d), lambda i: (0, 0)),
                  pl.BlockSpec((tm, d), lambda i: (i, 0)),
                  pl.BlockSpec((tm, d), lambda i: (i, 0)),
                  pl.BlockSpec((1, d), lambda i: (0, 0))],
        out_specs=[pl.BlockSpec((tm, d), lambda i: (i, 0)),
                   pl.BlockSpec((tm, d), lambda i: (i, 0)),
                   pl.BlockSpec((1, d), lambda i: (0, 0)),
                   pl.BlockSpec((1, LANES), lambda i: (0, 0))],
        out_shape=[jax.ShapeDtypeStruct((s, d), BF16), jax.ShapeDtypeStruct((s, d), F32),
                   jax.ShapeDtypeStruct((1, d), F32), jax.ShapeDtypeStruct((1, LANES), F32)],
        compiler_params=_cparams(("arbitrary",)),
    )(cat, w_out, x, target, g_post)


def matmul_nt(a, b, name):
    m, k = a.shape
    n = b.shape[0]
    tm = _tile(m, (512, 256, 128))

    def body(a_ref, b_ref, o_ref):
        o_ref[...] = _dot(a_ref[...], b_ref[...], 1, 1).astype(o_ref.dtype)

    return pl.pallas_call(
        body, name=name, grid=(m // tm,),
        in_specs=[pl.BlockSpec((tm, k), lambda i: (i, 0)), pl.BlockSpec((n, k), lambda i: (0, 0))],
        out_specs=pl.BlockSpec((tm, n), lambda i: (i, 0)),
        out_shape=jax.ShapeDtypeStruct((m, n), BF16),
        compiler_params=_cparams(("arbitrary",)),
    )(a, b)


def matmul_tn(a, b, tm, name, comms=()):
    k, m = a.shape
    n = b.shape[1]
    steps = m // tm
    hosted = _Hosted(comms)

    kc = _tile(k, (1024, 128))
    pieces = k // kc

    def body(*refs):
        (a_ref, b_hbm), (o_ref, cs_ref), (b_ref, b_sems), phases = hosted.split(refs, 2, 2, 2)
        step = pl.program_id(0)
        _before_step(phases, step, steps)

        def b_load(j):
            return pltpu.make_async_copy(b_hbm.at[j * kc:(j + 1) * kc, :], b_ref.at[j * kc:(j + 1) * kc, :], b_sems.at[j])

        @pl.when(step == 0)
        def _():
            for j in range(pieces):
                b_load(j).start()
            acc = None
            for j in range(pieces):
                b_load(j).wait()
                part = _dot(a_ref[j * kc:(j + 1) * kc, :], b_ref[j * kc:(j + 1) * kc, :], 0, 0)
                acc = part if acc is None else acc + part
            o_ref[...] = acc.astype(o_ref.dtype)

        @pl.when(step > 0)
        def _():
            o_ref[...] = _dot(a_ref[...], b_ref[...], 0, 0).astype(o_ref.dtype)

        rows = _tile(k, (512, 128))
        cs = jnp.zeros((1, tm), F32)
        for r0 in range(0, k, rows):
            cs = cs + jnp.sum(a_ref[r0:r0 + rows, :].astype(F32), axis=0, keepdims=True)
        cs_ref[...] = cs
        _after_step(phases, step, steps)

    return pl.pallas_call(
        body, name=name, grid=(steps,),
        in_specs=[pl.BlockSpec((k, tm), lambda i: (0, i)), ANY] + hosted.in_specs,
        out_specs=[pl.BlockSpec((tm, n), lambda i: (i, 0)), pl.BlockSpec((1, tm), lambda i: (0, i))] + hosted.out_specs,
        out_shape=[jax.ShapeDtypeStruct((m, n), BF16), jax.ShapeDtypeStruct((1, m), F32)] + hosted.out_shape,
        scratch_shapes=[pltpu.VMEM((k, n), b.dtype), pltpu.SemaphoreType.DMA((pieces,))] + hosted.scratch,
        compiler_params=_cparams(("arbitrary",)),
    )(a, b, *hosted.inputs)


def in_proj_bwd(dproj, wt, x, g_pre, dout, comms=()):
    s, d = x.shape
    tm = _tile(s, (512, 256, 128))
    steps = s // tm
    nsub = tm // CHUNK
    kchunks = [(k0, 1024) for k0 in range(0, 5120, 1024)] + [(5120, 256)]
    ksplit = len(kchunks)
    hosted = _Hosted(comms)

    def body(*refs):
        ((*dp_refs, w_hbm, x_hbm, g_ref, dout_hbm), (gx_hbm, dg_ref),
         (w_ref, w_sems, xbuf, dbuf, gbuf, in_sems, out_sems), phases) = hosted.split(refs, 4 + ksplit, 2, 7)
        step = pl.program_id(0)
        _before_step(phases, step, steps)

        def rows_of(ref, c):
            return ref.at[pl.ds(pl.multiple_of(step * tm + c * CHUNK, CHUNK), CHUNK), :]

        def fetches(c):
            return (pltpu.make_async_copy(rows_of(x_hbm, c), xbuf.at[c], in_sems.at[c]),
                    pltpu.make_async_copy(rows_of(dout_hbm, c), dbuf.at[c], in_sems.at[nsub + c]))

        def put(c):
            return pltpu.make_async_copy(gbuf.at[c % 2], rows_of(gx_hbm, c), out_sems.at[c % 2])

        for c in range(nsub):
            for cp in fetches(c):
                cp.start()

        def w_load(j):
            k0, kw = kchunks[j]
            return pltpu.make_async_copy(w_hbm.at[k0:k0 + kw, :], w_ref.at[k0:k0 + kw, :], w_sems.at[j])

        @pl.when(step == 0)
        def _():
            dg_ref[...] = jnp.zeros_like(dg_ref)
            for j in range(ksplit):
                w_load(j).start()

        dh_all = None
        for j, ((k0, kw), dp_ref) in enumerate(zip(kchunks, dp_refs)):
            @pl.when(step == 0)
            def _():
                w_load(j).wait()

            part = _dot(dp_ref[...], w_ref[k0:k0 + kw, :], 1, 0)
            dh_all = part if dh_all is None else dh_all + part
        for c in range(nsub):
            for cp in fetches(c):
                cp.wait()
            if c >= 2:
                put(c - 2).wait()
            elif c < nsub:
                @pl.when(step > 0)
                def _():
                    put(max(nsub - 2, 0) + c).wait()
            dh = dh_all[c * CHUNK:(c + 1) * CHUNK, :]
            xv = xbuf[c]
            r = lax.rsqrt(jnp.mean(xv * xv, axis=-1, keepdims=True) + EPS)
            xn = xv * r
            dg_ref[...] += jnp.sum(dh * xn, axis=0, keepdims=True)
            dn = dh * g_ref[...]
            gbuf[c % 2] = dbuf[c] + r * (dn - xn * jnp.mean(dn * xn, axis=-1, keepdims=True))
            put(c).start()
        @pl.when(step == steps - 1)
        def _():
            for c in range(max(nsub - 2, 0), nsub):
                put(c).wait()

        _after_step(phases, step, steps)

    side_in, side_out = pltpu.VMEM((nsub, CHUNK, d), F32), pltpu.VMEM((2, CHUNK, d), F32)
    row = pl.BlockSpec((1, d), lambda i: (0, 0))
    return pl.pallas_call(
        body, name="in_proj_bwd", grid=(steps,),
        in_specs=[pl.BlockSpec((tm, kw), functools.partial(lambda j, i: (i, j), k0 // kw)) for k0, kw in kchunks]
        + [ANY, ANY, row, ANY] + hosted.in_specs,
        out_specs=[ANY, row] + hosted.out_specs,
        out_shape=[jax.ShapeDtypeStruct((s, d), F32), jax.ShapeDtypeStruct((1, d), F32)] + hosted.out_shape,
        scratch_shapes=[pltpu.VMEM((D_IN, d), BF16), pltpu.SemaphoreType.DMA((ksplit,)), side_in, side_in, side_out,
                        pltpu.SemaphoreType.DMA((2 * nsub,)), pltpu.SemaphoreType.DMA((2,))] + hosted.scratch,
        compiler_params=_cparams(("arbitrary",)),
    )(*([dproj] * ksplit), wt, x, g_pre, dout, *hosted.inputs)


def _lane_iota(shape):
    return lax.broadcasted_iota(jnp.int32, shape, len(shape) - 1)


def rope_tables(pos_col, freq, sign, comms=()):
    s = pos_col.shape[0]
    tr = _tile(s, (512, 256, 128))
    hosted = _Hosted(comms)

    def body(*refs):
        (pos_ref, freq_ref, sign_ref), (out_ref,), _, phases = hosted.split(refs, 3, 1, 0)
        _before_step(phases, pl.program_id(0), s // tr)
        ang = pos_ref[...].astype(F32) * freq_ref[...]
        out_ref[:, :LANES] = jnp.cos(ang)
        out_ref[:, LANES:] = jnp.sin(ang) * sign_ref[...]
        _after_step(phases, pl.program_id(0), s // tr)

    return pl.pallas_call(
        body, name="rope_tables", grid=(s // tr,),
        in_specs=[pl.BlockSpec((tr, 1), lambda i: (i, 0)), pl.BlockSpec((1, LANES), lambda i: (0, 0)),
                  pl.BlockSpec((1, LANES), lambda i: (0, 0))] + hosted.in_specs,
        out_specs=[pl.BlockSpec((tr, 2 * LANES), lambda i: (i, 0))] + hosted.out_specs,
        out_shape=[jax.ShapeDtypeStruct((s, 2 * LANES), F32)] + hosted.out_shape,
        scratch_shapes=hosted.scratch,
        compiler_params=_cparams(("arbitrary",)),
    )(pos_col, freq, sign, *hosted.inputs)


def _partner(v):
    low = (_lane_iota(v.shape) % HEAD_DIM) < (HEAD_DIM // 2)
    return jnp.where(low, pltpu.roll(v, LANES - HEAD_DIM // 2, 1), pltpu.roll(v, HEAD_DIM // 2, 1))


def _rope(v, cos, sin_signed):
    return v * cos + _partner(v) * sin_signed


def _rope_transposed(dv, cos, sin_signed):
    return dv * cos - _partner(dv) * sin_signed


def _both_halves(v, kv_head):
    keep = (_lane_iota(v.shape) >= HEAD_DIM) if kv_head else (_lane_iota(v.shape) < HEAD_DIM)
    return jnp.where(keep, v, pltpu.roll(v, HEAD_DIM, 1))


def _fold_halves(acc):
    return acc + pltpu.roll(acc, HEAD_DIM, 1)


def _by_half(a, b):
    shape = jnp.broadcast_shapes(jnp.shape(a), jnp.shape(b))
    return jnp.where(_lane_iota(shape) < HEAD_DIM, a, b)


def _stack_heads(pair):
    return jnp.concatenate([_by_half(pair, 0.0), _by_half(0.0, pair)], axis=0)


def _band_bias(has_prev):
    i = lax.broadcasted_iota(jnp.int32, (2 * CHUNK, 2 * CHUNK), 0) % CHUNK
    j = lax.broadcasted_iota(jnp.int32, (2 * CHUNK, 2 * CHUNK), 1)
    band = jnp.logical_and(j > i, j <= i + CHUNK)
    return jnp.where(jnp.logical_and(band, jnp.logical_or(j >= CHUNK, has_prev)), 0.0, NEG)


def _probs(qm2, kk2, bias, sink_col):
    sc = _dot(qm2, kk2, 1, 1) + bias
    mx = jnp.maximum(jnp.max(sc, axis=-1, keepdims=True), sink_col)
    p = jnp.exp(sc - mx)
    es = jnp.exp(sink_col - mx)
    inv = 1.0 / (jnp.sum(p, axis=-1, keepdims=True) + es)
    return p * inv, es * inv


def _probs_staged(qm2s, kk2s, bias, sink_cols):
    k = range(len(qm2s))
    scs = [_dot(qm2s[i], kk2s[i], 1, 1) + bias for i in k]
    mxs = [jnp.maximum(jnp.max(scs[i], axis=-1, keepdims=True), sink_cols[i]) for i in k]
    ps = [jnp.exp(scs[i] - mxs[i]) for i in k]
    ess = [jnp.exp(sink_cols[i] - mxs[i]) for i in k]
    invs = [1.0 / (jnp.sum(ps[i], axis=-1, keepdims=True) + ess[i]) for i in k]
    return [ps[i] * invs[i] for i in k], [ess[i] * invs[i] for i in k]


def _sink_col(sinks_ref, pair):
    row = lax.broadcasted_iota(jnp.int32, (2 * CHUNK, 1), 0)
    return jnp.where(row < CHUNK, sinks_ref[2 * pair], sinks_ref[2 * pair + 1])


def _layer_norm_parts(v):
    mu = jnp.mean(v, axis=-1, keepdims=True)
    xc = v - mu
    rstd = lax.rsqrt(jnp.mean(xc * xc, axis=-1, keepdims=True) + EPS)
    return xc * rstd, rstd


def _masked_spatial(w_ref, g):
    t = lax.broadcasted_iota(jnp.int32, (CHUNK, CHUNK), 0)
    sidx = lax.broadcasted_iota(jnp.int32, (CHUNK, CHUNK), 1)
    return jnp.where(t >= sidx, w_ref[g], 0.0).astype(BF16)


def _keys_values(kv_ref, kvp_ref, rope_ref, ropep_ref):
    cos_c, sin_c = rope_ref[:, :LANES], rope_ref[:, LANES:]
    cos_p, sin_p = ropep_ref[:, :LANES], ropep_ref[:, LANES:]
    k_c = _rope(kv_ref[:, :D_KV].astype(F32), cos_c, sin_c)
    k_p = _rope(kvp_ref[:, :D_KV].astype(F32), cos_p, sin_p)
    keys = jnp.concatenate([k_p, k_c], axis=0)
    vals = jnp.concatenate([kvp_ref[:, D_KV:], kv_ref[:, D_KV:]], axis=0).astype(F32)
    return keys, vals, (cos_c, sin_c, cos_p, sin_p)


def mixer_fwd(proj, rope, ln_g, ln_b, w_sp, b_sp_rows, sinks, comms=()):
    s = proj.shape[0]
    nb = s // CHUNK
    hosted = _Hosted(comms)

    def body(sinks_ref, *refs):
        ((proj_ref, kvp_ref, rope_ref, ropep_ref, lng_ref, lnb_ref, w_ref, b_ref), (cat_ref,), _,
         phases) = hosted.split(refs, 8, 1, 0)
        n = pl.program_id(0)
        _before_step(phases, n, nb)
        xhat, _ = _layer_norm_parts(proj_ref[:, OFF_V:OFF_V + D_GMLP].astype(F32))
        vnb = (xhat * lng_ref[...] + lnb_ref[...]).astype(BF16)
        mixeds = [_dot(_masked_spatial(w_ref, g), vnb[:, g * CHUNK:(g + 1) * CHUNK], 1, 0) + b_ref[g]
                  for g in range(GROUPS)]
        for g in range(GROUPS):
            za = proj_ref[:, OFF_ZA + g * CHUNK:OFF_ZA + (g + 1) * CHUNK].astype(F32)
            u = proj_ref[:, OFF_U + g * CHUNK:OFF_U + (g + 1) * CHUNK].astype(F32)
            cat_ref[:, g * CHUNK:(g + 1) * CHUNK] = (u * mixeds[g] * (za * _sigmoid(za))).astype(BF16)
        kv_ref = proj_ref.at[:, OFF_K:OFF_K + 2 * D_KV]
        keys, vals, (cos_c, sin_c, _, _) = _keys_values(kv_ref, kvp_ref, rope_ref, ropep_ref)
        cos_q, sin_q = cos_c * SCALE, sin_c * SCALE
        bias = _band_bias(n > 0)
        kk2 = [_both_halves(keys, kvh).astype(BF16) for kvh in range(N_KV_HEADS)]
        vv2 = [_both_halves(vals, kvh).astype(BF16) for kvh in range(N_KV_HEADS)]
        pairs = range(N_PAIRS)
        qms = [_stack_heads(_rope(proj_ref[:, OFF_Q + pair * LANES:OFF_Q + (pair + 1) * LANES].astype(F32),
                                  cos_q, sin_q)).astype(BF16) for pair in pairs]
        probs = _probs_staged(qms, [kk2[pair // PAIRS_PER_KV] for pair in pairs], bias,
                              [_sink_col(sinks_ref, pair) for pair in pairs])[0]
        outs = [_dot(probs[pair].astype(BF16), vv2[pair // PAIRS_PER_KV], 1, 0) for pair in pairs]
        for pair in pairs:
            out_pair = _by_half(outs[pair][:CHUNK], outs[pair][CHUNK:])
            zb = proj_ref[:, OFF_ZB + pair * LANES:OFF_ZB + (pair + 1) * LANES].astype(F32)
            cat_ref[:, D_GMLP + pair * LANES:D_GMLP + (pair + 1) * LANES] = (
                out_pair * (zb * _sigmoid(zb))).astype(BF16)
        _after_step(phases, n, nb)

    prev = lambda n, *_: (jnp.maximum(n - 1, 0), 0)
    kv_block = OFF_K // (2 * D_KV)
    return pl.pallas_call(
        body, name="mixer_fwd",
        grid_spec=pltpu.PrefetchScalarGridSpec(
            num_scalar_prefetch=1, grid=(nb,),
            in_specs=[pl.BlockSpec((CHUNK, D_IN), lambda n, *_: (n, 0)),
                      pl.BlockSpec((CHUNK, 2 * D_KV), lambda n, *_: (jnp.maximum(n - 1, 0), kv_block)),
                      pl.BlockSpec((CHUNK, 2 * LANES), lambda n, *_: (n, 0)),
                      pl.BlockSpec((CHUNK, 2 * LANES), prev),
                      pl.BlockSpec((1, D_GMLP), lambda n, *_: (0, 0)),
                      pl.BlockSpec((1, D_GMLP), lambda n, *_: (0, 0)),
                      pl.BlockSpec((GROUPS, CHUNK, CHUNK), lambda n, *_: (0, 0, 0)),
                      pl.BlockSpec((GROUPS, CHUNK, CHUNK), lambda n, *_: (0, 0, 0))] + hosted.in_specs,
            out_specs=[pl.BlockSpec((CHUNK, D_GMLP + D_ATTN), lambda n, *_: (n, 0))] + hosted.out_specs,
            scratch_shapes=hosted.scratch),
        out_shape=[jax.ShapeDtypeStruct((s, D_GMLP + D_ATTN), BF16)] + hosted.out_shape,
        compiler_params=_cparams(("arbitrary",)),
    )(sinks, proj, proj, rope, rope, ln_g, ln_b, w_sp, b_sp_rows, *hosted.inputs)


def mixer_bwd(proj, dcat, rope, ln_g, ln_b, w_sp, b_sp_rows, sinks, comms=()):
    s = proj.shape[0]
    nb = s // CHUNK
    hosted = _Hosted(comms)

    def body(sinks_ref, *refs):
        ((proj_ref, kvp_ref, dcat_ref, rope_ref, ropep_ref, lng_ref, lnb_ref, w_ref, b_ref),
         (dproj_ref, dw_ref, db_ref, dlng_ref, dlnb_ref, dsink_ref),
         (pend_ref, pend_kv_ref, dbacc_ref), phases) = hosted.split(refs, 9, 6, 3)
        n = pl.program_id(0)
        _before_step(phases, n, nb + 1)

        @pl.when(n == 0)
        def _():
            dw_ref[...] = jnp.zeros_like(dw_ref)
            dbacc_ref[...] = jnp.zeros_like(dbacc_ref)
            dlng_ref[...] = jnp.zeros_like(dlng_ref)
            dlnb_ref[...] = jnp.zeros_like(dlnb_ref)
            dsink_ref[...] = jnp.zeros_like(dsink_ref)

        @pl.when(n > 0)
        def _():
            dproj_ref[...] = pend_ref[...]

        def flush(dkv_prev):
            @pl.when(n > 0)
            def _():
                dproj_ref[:, OFF_K:OFF_K + 2 * D_KV] = (pend_kv_ref[...] + dkv_prev).astype(BF16)

        @pl.when(n < nb)
        def _():
            kv_ref = proj_ref.at[:, OFF_K:OFF_K + 2 * D_KV]
            keys, vals, (cos_c, sin_c, cos_p, sin_p) = _keys_values(kv_ref, kvp_ref, rope_ref, ropep_ref)
            cos_q, sin_q = cos_c * SCALE, sin_c * SCALE
            bias = _band_bias(n > 0)
            lane_row = _lane_iota((1, LANES))
            dsink = jnp.zeros((1, LANES), F32)
            dk_heads, dv_heads = [], []
            for kvh in range(N_KV_HEADS):
                kk2 = _both_halves(keys, kvh).astype(BF16)
                vv2 = _both_halves(vals, kvh).astype(BF16)
                pairs = list(range(kvh * PAIRS_PER_KV, (kvh + 1) * PAIRS_PER_KV))
                k4 = range(PAIRS_PER_KV)
                qm2s = [_stack_heads(_rope(proj_ref[:, OFF_Q + pair * LANES:OFF_Q + (pair + 1) * LANES].astype(F32),
                                           cos_q, sin_q)).astype(BF16) for pair in pairs]
                ps, p_sinks = _probs_staged(qm2s, [kk2] * PAIRS_PER_KV, bias,
                                            [_sink_col(sinks_ref, pair) for pair in pairs])
                pbs = [p.astype(BF16) for p in ps]
                o2s = [_dot(pb, vv2, 1, 0) for pb in pbs]
                zbs = [proj_ref[:, OFF_ZB + pair * LANES:OFF_ZB + (pair + 1) * LANES].astype(F32) for pair in pairs]
                sgs = [_sigmoid(zb) for zb in zbs]
                dybs = [dcat_ref[:, D_GMLP + pair * LANES:D_GMLP + (pair + 1) * LANES].astype(F32) for pair in pairs]
                for i, pair in enumerate(pairs):
                    out_pair = _by_half(o2s[i][:CHUNK], o2s[i][CHUNK:])
                    pend_ref[:, OFF_ZB + pair * LANES:OFF_ZB + (pair + 1) * LANES] = (
                        dybs[i] * out_pair * (sgs[i] * (1.0 + zbs[i] * (1.0 - sgs[i])))).astype(BF16)
                dom2s = [_stack_heads(dybs[i] * (zbs[i] * sgs[i])).astype(BF16) for i in k4]
                dps = [_dot(dom2, vv2, 1, 1) for dom2 in dom2s]
                deltas = [jnp.sum(ps[i] * dps[i], axis=-1, keepdims=True) for i in k4]
                dss = [ps[i] * (dps[i] - deltas[i]) for i in k4]
                for i, pair in enumerate(pairs):
                    dsk = -(p_sinks[i] * deltas[i])
                    dsink = dsink + jnp.where(lane_row == 2 * pair,
                                              jnp.sum(dsk[:CHUNK], axis=0, keepdims=True), 0.0)
                    dsink = dsink + jnp.where(lane_row == 2 * pair + 1,
                                              jnp.sum(dsk[CHUNK:], axis=0, keepdims=True), 0.0)
                dsbs = [ds.astype(BF16) for ds in dss]
                dq2s = [_dot(dsb, kk2, 1, 0) for dsb in dsbs]
                for pair, dq2 in zip(pairs, dq2s):
                    pend_ref[:, OFF_Q + pair * LANES:OFF_Q + (pair + 1) * LANES] = _rope_transposed(
                        _by_half(dq2[:CHUNK], dq2[CHUNK:]), cos_q, sin_q).astype(BF16)
                dkks = [_dot(dsbs[i], qm2s[i], 0, 0) for i in k4]
                dvvs = [_dot(pbs[i], dom2s[i], 0, 0) for i in k4]
                dk_heads.append(_fold_halves((dkks[0] + dkks[1]) + (dkks[2] + dkks[3])))
                dv_heads.append(_fold_halves((dvvs[0] + dvvs[1]) + (dvvs[2] + dvvs[3])))
            dk_rot = _by_half(dk_heads[0], dk_heads[1])
            dv_all = _by_half(dv_heads[0], dv_heads[1])
            dk_p = _rope_transposed(dk_rot[:CHUNK], cos_p, sin_p)
            dk_c = _rope_transposed(dk_rot[CHUNK:], cos_c, sin_c)
            flush(jnp.concatenate([dk_p, dv_all[:CHUNK]], axis=1))
            dsink_ref[...] += dsink
            pend_kv_ref[...] = jnp.concatenate([dk_c, dv_all[CHUNK:]], axis=1)
            xhat, rstd = _layer_norm_parts(proj_ref[:, OFF_V:OFF_V + D_GMLP].astype(F32))
            lng = lng_ref[...]
            vnb = (xhat * lng + lnb_ref[...]).astype(BF16)
            dvn_cols = []
            for g in range(GROUPS):
                cols = slice(g * CHUNK, (g + 1) * CHUNK)
                wm = _masked_spatial(w_ref, g)
                mixed = _dot(wm, vnb[:, cols], 1, 0) + b_ref[g]
                za = proj_ref[:, OFF_ZA + g * CHUNK:OFF_ZA + (g + 1) * CHUNK].astype(F32)
                u = proj_ref[:, OFF_U + g * CHUNK:OFF_U + (g + 1) * CHUNK].astype(F32)
                dya = dcat_ref[:, cols].astype(F32)
                sg = _sigmoid(za)
                sz = za * sg
                pend_ref[:, OFF_U + g * CHUNK:OFF_U + (g + 1) * CHUNK] = (dya * mixed * sz).astype(BF16)
                pend_ref[:, OFF_ZA + g * CHUNK:OFF_ZA + (g + 1) * CHUNK] = (
                    dya * u * mixed * (sg * (1.0 + za * (1.0 - sg)))).astype(BF16)
                dmixed = dya * u * sz
                dmb = dmixed.astype(BF16)
                dbacc_ref[g] += dmixed
                dw_ref[g] += _dot(dmb, vnb[:, cols], 1, 1)
                dvn_cols.append(_dot(wm, dmb, 0, 0))
            dvn = jnp.concatenate(dvn_cols, axis=1)
            dlng_ref[...] += jnp.sum(dvn * xhat, axis=0, keepdims=True)
            dlnb_ref[...] += jnp.sum(dvn, axis=0, keepdims=True)
            dxh = dvn * lng
            dv = rstd * (dxh - jnp.mean(dxh, axis=-1, keepdims=True)
                         - xhat * jnp.mean(dxh * xhat, axis=-1, keepdims=True))
            pend_ref[:, OFF_V:OFF_V + D_GMLP] = dv.astype(BF16)

        @pl.when(n == nb)
        def _():
            flush(jnp.zeros((CHUNK, 2 * D_KV), F32))
            t = lax.broadcasted_iota(jnp.int32, (CHUNK, CHUNK), 0)
            sidx = lax.broadcasted_iota(jnp.int32, (CHUNK, CHUNK), 1)
            lane = _lane_iota((CHUNK, LANES))
            dbt = jnp.zeros((CHUNK, LANES), F32)
            for g in range(GROUPS):
                dw_ref[g] = jnp.where(t >= sidx, dw_ref[g], 0.0)
                dbt = jnp.where(lane == g, jnp.sum(dbacc_ref[g], axis=-1, keepdims=True), dbt)
            db_ref[...] = jnp.transpose(dbt)[:GROUPS, :]

        _after_step(phases, n, nb + 1)

    cur = lambda n, *_: (jnp.minimum(n, nb - 1), 0)
    prev = lambda n, *_: (jnp.clip(n - 1, 0, nb - 1), 0)
    kv_block = OFF_K // (2 * D_KV)
    const2 = lambda n, *_: (0, 0)
    const3 = lambda n, *_: (0, 0, 0)
    return pl.pallas_call(
        body, name="mixer_bwd",
        grid_spec=pltpu.PrefetchScalarGridSpec(
            num_scalar_prefetch=1, grid=(nb + 1,),
            in_specs=[pl.BlockSpec((CHUNK, D_IN), cur),
                      pl.BlockSpec((CHUNK, 2 * D_KV), lambda n, *_: (jnp.clip(n - 1, 0, nb - 1), kv_block)),
                      pl.BlockSpec((CHUNK, D_GMLP + D_ATTN), cur),
                      pl.BlockSpec((CHUNK, 2 * LANES), cur),
                      pl.BlockSpec((CHUNK, 2 * LANES), prev),
                      pl.BlockSpec((1, D_GMLP), const2),
                      pl.BlockSpec((1, D_GMLP), const2),
                      pl.BlockSpec((GROUPS, CHUNK, CHUNK), const3),
                      pl.BlockSpec((GROUPS, CHUNK, CHUNK), const3)] + hosted.in_specs,
            out_specs=[pl.BlockSpec((CHUNK, D_IN), lambda n, *_: (jnp.maximum(n - 1, 0), 0)),
                       pl.BlockSpec((GROUPS, CHUNK, CHUNK), const3),
                       pl.BlockSpec((GROUPS, CHUNK), const2),
                       pl.BlockSpec((1, D_GMLP), const2),
                       pl.BlockSpec((1, D_GMLP), const2),
                       pl.BlockSpec((1, LANES), const2)] + hosted.out_specs,
            scratch_shapes=[pltpu.VMEM((CHUNK, D_IN), BF16), pltpu.VMEM((CHUNK, 2 * D_KV), F32),
                            pltpu.VMEM((GROUPS, CHUNK, CHUNK), F32)] + hosted.scratch),
        out_shape=[jax.ShapeDtypeStruct((s, D_IN), BF16),
                   jax.ShapeDtypeStruct((GROUPS, CHUNK, CHUNK), F32),
                   jax.ShapeDtypeStruct((GROUPS, CHUNK), F32),
                   jax.ShapeDtypeStruct((1, D_GMLP), F32),
                   jax.ShapeDtypeStruct((1, D_GMLP), F32),
                   jax.ShapeDtypeStruct((1, LANES), F32)] + hosted.out_shape,
        compiler_params=_cparams(("arbitrary",)),
    )(sinks, proj, proj, dcat, rope, rope, ln_g, ln_b, w_sp, b_sp_rows, *hosted.inputs)


def _adamw_math(w, g, m, v):
    m = ADAM_B1 * m + (1.0 - ADAM_B1) * g
    v = ADAM_B2 * v + (1.0 - ADAM_B2) * (g * g)
    m_hat = m / (1.0 - ADAM_B1 ** ADAM_STEP)
    v_hat = v / (1.0 - ADAM_B2 ** ADAM_STEP)
    delta = -ADAM_LR * (m_hat / (jnp.sqrt(v_hat) + ADAM_EPS) + ADAM_WD * w)
    return delta, m, v


def adamw_shard(terms, w, m, v, name):
    r, c = w.shape
    tr = _tile(r, (224, 256, 128, 8))
    n_terms = len(terms)

    def body(*refs):
        w_ref, m_ref, v_ref, g_ref, d_ref, nm_ref, nv_ref = refs[n_terms:]
        g = None
        for ref, (_, slots) in zip(refs[:n_terms], terms):
            for k in range(slots):
                part = ref[k].astype(F32)
                g = part if g is None else g + part
        g_ref[...] = g
        d_ref[...], nm_ref[...], nv_ref[...] = _adamw_math(w_ref[...], g, m_ref[...], v_ref[...])

    spec = pl.BlockSpec((tr, c), lambda i: (i, 0))
    return pl.pallas_call(
        body, name=name, grid=(r // tr,),
        in_specs=[pl.BlockSpec((slots, tr, c), lambda i: (0, i, 0)) for _, slots in terms] + [spec] * 3,
        out_specs=[spec] * 4, out_shape=[jax.ShapeDtypeStruct((r, c), F32)] * 4,
        compiler_params=_cparams(("arbitrary",)),
    )(*[a for a, _ in terms], w, m, v)


def adamw_small(gathered, lane_windows, params):
    n_par = len(params)

    def body(*refs):
        g_refs = refs[:n_par + 1]
        wmv_refs = refs[n_par + 1:4 * n_par + 1]
        out_refs = refs[4 * n_par + 1:]

        def total(ref):
            acc = ref[0]
            for dev in range(1, N_DEV):
                acc = acc + ref[dev]
            return acc

        for i in range(n_par):
            w_ref, m_ref, v_ref = wmv_refs[3 * i:3 * i + 3]
            g = total(g_refs[i])
            if lane_windows[i] is not None:
                start, size = lane_windows[i]
                g = g[..., start:start + size]
            delta, new_m, new_v = _adamw_math(w_ref[...], g, m_ref[...], v_ref[...])
            for ref, val in zip(out_refs[4 * i:4 * i + 4], (g, delta, new_m, new_v)):
                ref[...] = val
        out_refs[4 * n_par][...] = total(g_refs[n_par])

    flat = [a for wmv in params for a in wmv]
    out_shape = [jax.ShapeDtypeStruct(w.shape, F32) for (w, _, _) in params for _ in range(4)]
    out_shape.append(jax.ShapeDtypeStruct(gathered[-1].shape[1:], F32))
    outs = pl.pallas_call(body, name="adamw_small", out_shape=out_shape, compiler_params=_cparams())(*gathered, *flat)
    return [tuple(outs[4 * i:4 * i + 4]) for i in range(n_par)], outs[-1]


def kernel(x, positions, g_pre, w_in, b_qkv, ln_v_g, ln_v_b, w_spatial, b_spatial, attn_sinks, w_out, g_post, loss_target, m_g_pre, m_w_in, m_b_qkv, m_ln_v_g, m_ln_v_b, m_w_spatial, m_b_spatial, m_attn_sinks, m_w_out, m_g_post, v_g_pre, v_w_in, v_b_qkv, v_ln_v_g, v_ln_v_b, v_w_spatial, v_b_spatial, v_attn_sinks, v_w_out, v_g_post):
    x2, target = x[0], loss_target[0]
    seq = x2.shape[0]
    xi, yi, ci = _my_place()

    wt_shard = w_in[0].T.astype(BF16)
    wo_shard = w_out[0].astype(BF16)
    pos_col = positions.reshape(seq, 1)
    half = HEAD_DIM // 2
    inv_freq = ROPE_THETA ** (-jnp.arange(half, dtype=F32) * (2.0 / HEAD_DIM))
    freq = jnp.tile(inv_freq, LANES // half).reshape(1, LANES)
    sign = jnp.tile(jnp.concatenate([-jnp.ones((half,), F32), jnp.ones((half,), F32)]), LANES // HEAD_DIM)
    sign = sign.reshape(1, LANES)
    bias = jnp.concatenate([jnp.zeros((1, OFF_Q), F32), b_qkv, jnp.zeros((1, D_ATTN), F32)], axis=1)
    proj, h, rope, wt = in_proj_gather(x2, pos_col, freq, sign, g_pre, wt_shard, bias)

    b_rows = jnp.broadcast_to(b_spatial[0][:, :, None], (GROUPS, CHUNK, CHUNK))
    sinks = attn_sinks[0]
    cat, wo = mixer_fwd(proj, rope, ln_v_g, ln_v_b, w_spatial[0], b_rows, sinks, comms=[gather_comm([wo_shard])])
    dy, dout, d_g_post, loss_part = out_proj_loss(cat, wo, x2, target, g_post)

    dcat = matmul_nt(dy, wo, "out_proj_bwd")
    d_wo, _ = matmul_tn(cat, dy, 512, "w_out_grad")
    dproj, d_w_sp, d_b_sp, d_ln_g, d_ln_b, d_sinks, parts_wo = mixer_bwd(
        proj, dcat, rope, ln_v_g, ln_v_b, w_spatial[0], b_rows, sinks, comms=[scatter_comm([d_wo])])
    small_parts = [d_ln_g, d_ln_b, d_w_sp, d_b_sp, d_sinks, d_g_post, loss_part]
    d_wt, colsum, *landed = matmul_tn(dproj, h, 768, "w_in_grad", comms=[gather_comm(small_parts, stack=True)])

    owners = jnp.stack([4 * cx + 2 * cy + ci for cx, cy in (_chip_of(xi, yi, r) for r in range(4))]).astype(jnp.int32)
    (got_wt,) = run_comm(pair_comm([d_wt]), "grad_exchange_pair")
    sum_wt = pair_sum(d_wt, got_wt, owners, "grad_pair_sum_w_in")
    grad_x, d_g_pre, far_wt = in_proj_bwd(dproj, wt, x2, g_pre, dout, comms=[chips_comm([sum_wt])])
    late = run_comm(gather_comm([d_g_pre, colsum], stack=True, direct=True), "allgather_late_grads")
    gathered = late + landed
    windows = [None, (OFF_Q, D_QKV), None, None, None, None, (0, N_Q_HEADS), None]
    small = [(g_pre, m_g_pre, v_g_pre), (b_qkv, m_b_qkv, v_b_qkv), (ln_v_g, m_ln_v_g, v_ln_v_g),
             (ln_v_b, m_ln_v_b, v_ln_v_b), (w_spatial[0], m_w_spatial[0], v_w_spatial[0]),
             (b_spatial[0], m_b_spatial[0], v_b_spatial[0]), (attn_sinks, m_attn_sinks, v_attn_sinks),
             (g_post, m_g_post, v_g_post)]
    small_out, loss_row = adamw_small(gathered, windows, small)
    lead = [False, False, False, False, True, True, False, False]
    small_out = [tuple(a[None] if ld else a for a in leaf) for leaf, ld in zip(small_out, lead)]

    wt_out = adamw_shard([(sum_wt, 1), (far_wt, 3)], w_in[0].T, m_w_in[0].T, v_w_in[0].T, "adamw_w_in")
    wo_out = adamw_shard([(parts_wo, N_DEV)], w_out[0], m_w_out[0], v_w_out[0], "adamw_w_out")

    def leaves(k):
        gp, bq, lg, lb, ws, bs, sk, gpo = (leaf[k] for leaf in small_out)
        return [gp, wt_out[k].T[None], bq, lg, lb, ws, bs, sk, wo_out[k][None], gpo]

    return (loss_row[0, 0], grad_x[None], *leaves(0), *leaves(1), *leaves(2), *leaves(3))
```

```python
import functools

import jax
import jax.numpy as jnp
from jax import lax
from jax.experimental import pallas as pl
from jax.experimental.pallas import tpu as pltpu

F32 = jnp.float32
BF16 = jnp.bfloat16

D_MODEL = 2048
D_GMLP = 1024
D_ATTN = 1024
CHUNK = 128
GROUPS = 8
HEAD_DIM = 64
N_Q_HEADS = 16
N_KV_HEADS = 2
D_KV = N_KV_HEADS * HEAD_DIM
D_IN = 3 * D_GMLP + D_ATTN + 2 * D_KV + D_ATTN
OFF_U, OFF_V, OFF_ZA = 0, D_GMLP, 2 * D_GMLP
OFF_Q = 3 * D_GMLP
OFF_K = OFF_Q + D_ATTN
OFF_VA = OFF_K + D_KV
OFF_ZB = OFF_VA + D_KV
D_QKV = D_ATTN + 2 * D_KV
ROPE_THETA = 10000.0
EPS = 1e-6
SCALE = HEAD_DIM ** -0.5
NEG = -1e30
N_PAIRS = N_Q_HEADS // 2
PAIRS_PER_KV = N_PAIRS // N_KV_HEADS

ADAM_LR = 0.001
ADAM_B1 = 0.9
ADAM_B2 = 0.999
ADAM_EPS = 1e-08
ADAM_WD = 0.01
ADAM_STEP = 10

N_DEV = 8
LANES = 128
VMEM_LIMIT = 56 * 1024 * 1024

MESH = pl.DeviceIdType.MESH
ANY = pl.BlockSpec(memory_space=pl.ANY)


def _cparams(sem=None):
    return pltpu.CompilerParams(dimension_semantics=sem, vmem_limit_bytes=VMEM_LIMIT)


def _tile(n, prefs):
    for t in prefs:
        if n % t == 0:
            return t
    return n


def _sigmoid(z):
    return 1.0 / (1.0 + jnp.exp(-z))


def _dot(a, b, ca, cb):
    return lax.dot_general(a, b, (((ca,), (cb,)), ((), ())), preferred_element_type=F32)


def _my_place():
    return lax.axis_index("x"), lax.axis_index("y"), lax.axis_index("c")


def _chip_of(x, y, r):
    return (x ^ (r & 1), y ^ (r >> 1))


def _peer(x, y, c, k):
    return (x ^ (k >> 2), y ^ ((k >> 1) & 1), c ^ (k & 1))


def _index(px, py, pc):
    return 4 * px + 2 * py + pc


class _Comm:
    def __init__(self, inputs, out_shape, scratch, bind):
        self.inputs, self.out_shape, self.scratch, self.bind = list(inputs), list(out_shape), list(scratch), bind


def gather_comm(shards, stack=False, direct=False):
    n_arr = len(shards)

    def bind(ins, outs, sems):
        send_sems, recv_sems, local_sems = sems
        x, y, c = _my_place()
        me, sibling = (x, y, c), (x, y, 1 - c)
        chips = [_chip_of(x, y, r) for r in (1, 2, 3)]

        def rows(a, px, py, pc):
            d = _index(px, py, pc)
            if stack:
                return outs[a].at[d]
            m = shards[a].shape[0]
            return outs[a].at[pl.ds(pl.multiple_of(d * m, 8), m), :]

        def copy(a, k, block, to, src=None):
            return pltpu.make_async_remote_copy(
                src_ref=rows(a, *block) if src is None else src, dst_ref=rows(a, *block),
                send_sem=send_sems.at[a * 7 + k], recv_sem=recv_sems.at[a * 7 + k],
                device_id=to, device_id_type=MESH)

        def mine(a):
            return pltpu.make_async_copy(ins[a], rows(a, *me), local_sems.at[a])

        def own_sends(a):
            if direct:
                return [copy(a, k - 1, me, _peer(x, y, c, k), src=ins[a]) for k in range(1, 8)]
            return ([copy(a, 0, me, sibling, src=ins[a])]
                    + [copy(a, 1 + j, me, (*chip, c), src=ins[a]) for j, chip in enumerate(chips)])

        def start():
            for a in range(n_arr):
                mine(a).start()
                for cp in own_sends(a):
                    cp.start()

        def relay():
            if direct:
                return
            for j, chip in enumerate(chips):
                for a in range(n_arr):
                    copy(a, 1 + j, (*chip, c), me).wait_recv()
                    copy(a, 4 + j, (*chip, c), sibling).start()

        def finish():
            for a in range(n_arr):
                if direct:
                    for k in range(1, 8):
                        copy(a, k - 1, _peer(x, y, c, k), me).wait_recv()
                else:
                    copy(a, 0, sibling, me).wait_recv()
                    for j, chip in enumerate(chips):
                        copy(a, 4 + j, (*chip, 1 - c), me).wait_recv()
                        copy(a, 4 + j, (*chip, c), sibling).wait_send()
                for cp in own_sends(a):
                    cp.wait_send()
                mine(a).wait()

        return start, relay, finish

    def gathered(s):
        return (N_DEV, *s.shape) if stack else (N_DEV * s.shape[0], s.shape[1])

    return _Comm(shards, [jax.ShapeDtypeStruct(gathered(s), s.dtype) for s in shards],
                 [pltpu.SemaphoreType.DMA((7 * n_arr,)), pltpu.SemaphoreType.DMA((7 * n_arr,)),
                  pltpu.SemaphoreType.DMA((n_arr,))], bind)


def scatter_comm(parts):
    n_arr = len(parts)

    def bind(ins, outs, sems):
        send_sems, recv_sems, local_sems = sems
        x, y, c = _my_place()
        my_index = _index(x, y, c)

        def block(a, d):
            m = parts[a].shape[0] // N_DEV
            return ins[a].at[pl.ds(pl.multiple_of(d * m, 16), m), :]

        def copy(a, k, slot):
            peer = _peer(x, y, c, k)
            return pltpu.make_async_remote_copy(
                src_ref=block(a, _index(*peer)), dst_ref=outs[a].at[slot],
                send_sem=send_sems.at[a * 7 + k - 1], recv_sem=recv_sems.at[a * 7 + k - 1],
                device_id=peer, device_id_type=MESH)

        def mine(a):
            return pltpu.make_async_copy(block(a, my_index), outs[a].at[my_index], local_sems.at[a])

        def start():
            for a in range(n_arr):
                mine(a).start()
                for k in range(1, 8):
                    copy(a, k, my_index).start()

        def finish():
            for a in range(n_arr):
                for k in range(1, 8):
                    copy(a, k, _index(*_peer(x, y, c, k))).wait_recv()
                    copy(a, k, my_index).wait_send()
                mine(a).wait()

        return start, (lambda: None), finish

    return _Comm(parts, [jax.ShapeDtypeStruct((N_DEV, p.shape[0] // N_DEV, p.shape[1]), p.dtype) for p in parts],
                 [pltpu.SemaphoreType.DMA((7 * n_arr,)), pltpu.SemaphoreType.DMA((7 * n_arr,)),
                  pltpu.SemaphoreType.DMA((n_arr,))], bind)


def pair_comm(parts):
    n_arr = len(parts)

    def bind(ins, outs, sems):
        send_sems, recv_sems = sems
        x, y, c = _my_place()

        def copies():
            out = []
            for a in range(n_arr):
                m = parts[a].shape[0] // N_DEV
                for r in range(4):
                    owner = _index(*_chip_of(x, y, r), 1 - c)
                    out.append(pltpu.make_async_remote_copy(
                        src_ref=ins[a].at[pl.ds(pl.multiple_of(owner * m, 16), m), :], dst_ref=outs[a].at[r],
                        send_sem=send_sems.at[a * 4 + r], recv_sem=recv_sems.at[a * 4 + r],
                        device_id=(x, y, 1 - c), device_id_type=MESH))
            return out

        def start():
            for cp in copies():
                cp.start()

        def finish():
            for cp in copies():
                cp.wait_recv()
                cp.wait_send()

        return start, (lambda: None), finish

    return _Comm(parts, [jax.ShapeDtypeStruct((4, p.shape[0] // N_DEV, p.shape[1]), p.dtype) for p in parts],
                 [pltpu.SemaphoreType.DMA((4 * n_arr,)), pltpu.SemaphoreType.DMA((4 * n_arr,))], bind)


def chips_comm(sums):
    n_arr = len(sums)

    def bind(ins, outs, sems):
        send_sems, recv_sems = sems
        x, y, c = _my_place()

        def copies():
            return [pltpu.make_async_remote_copy(
                src_ref=ins[a].at[r], dst_ref=outs[a].at[r - 1],
                send_sem=send_sems.at[a * 3 + r - 1], recv_sem=recv_sems.at[a * 3 + r - 1],
                device_id=(*_chip_of(x, y, r), c), device_id_type=MESH) for a in range(n_arr) for r in (1, 2, 3)]

        def start():
            for cp in copies():
                cp.start()

        def finish():
            for cp in copies():
                cp.wait_recv()
                cp.wait_send()

        return start, (lambda: None), finish

    return _Comm(sums, [jax.ShapeDtypeStruct((3,) + s.shape[1:], s.dtype) for s in sums],
                 [pltpu.SemaphoreType.DMA((3 * n_arr,)), pltpu.SemaphoreType.DMA((3 * n_arr,))], bind)


def run_comm(comm, name):
    n_in, n_out = len(comm.inputs), len(comm.out_shape)

    def body(*refs):
        start, relay, finish = comm.bind(refs[:n_in], refs[n_in:n_in + n_out], refs[n_in + n_out:])
        start()
        relay()
        finish()

    outs = pl.pallas_call(body, name=name, out_shape=comm.out_shape, in_specs=[ANY] * n_in,
                          out_specs=[ANY] * n_out, scratch_shapes=comm.scratch)(*comm.inputs)
    return list(outs)


class _Hosted:
    def __init__(self, comms):
        self.comms = list(comms)
        self.inputs = [a for cm in self.comms for a in cm.inputs]
        self.out_shape = [s for cm in self.comms for s in cm.out_shape]
        self.scratch = [s for cm in self.comms for s in cm.scratch]
        self.in_specs = [ANY] * len(self.inputs)
        self.out_specs = [ANY] * len(self.out_shape)

    def split(self, refs, n_in, n_out, n_scratch):
        ni, no = len(self.inputs), len(self.out_shape)
        ins, rest = refs[:n_in], refs[n_in:]
        c_ins, rest = rest[:ni], rest[ni:]
        outs, rest = rest[:n_out], rest[n_out:]
        c_outs, rest = rest[:no], rest[no:]
        scratch, c_sems = rest[:n_scratch], rest[n_scratch:]
        phases = []
        for cm in self.comms:
            a, b, s = len(cm.inputs), len(cm.out_shape), len(cm.scratch)
            phases.append(cm.bind(c_ins[:a], c_outs[:b], c_sems[:s]))
            c_ins, c_outs, c_sems = c_ins[a:], c_outs[b:], c_sems[s:]
        return ins, outs, scratch, phases


def _before_step(phases, step, n_steps):
    if not phases:
        return

    @pl.when(step == 0)
    def _():
        for start, _, _ in phases:
            start()

    @pl.when(step == n_steps // 2)
    def _():
        for _, relay, _ in phases:
            relay()


def _after_step(phases, step, n_steps):
    if not phases:
        return

    @pl.when(step == n_steps - 1)
    def _():
        for _, _, finish in phases:
            finish()


def pair_sum(part, got, owners, name):
    m, n = got.shape[1:]

    def body(own_ref, mine_ref, got_ref, out_ref):
        del own_ref
        out_ref[...] = (mine_ref[...].astype(F32) + got_ref[...].astype(F32)).astype(out_ref.dtype)

    return pl.pallas_call(
        body, name=name,
        grid_spec=pltpu.PrefetchScalarGridSpec(
            num_scalar_prefetch=1, grid=(4,),
            in_specs=[pl.BlockSpec((m, n), lambda r, own: (own[r], 0)),
                      pl.BlockSpec((None, m, n), lambda r, own: (r, 0, 0))],
            out_specs=pl.BlockSpec((None, m, n), lambda r, own: (r, 0, 0))),
        out_shape=jax.ShapeDtypeStruct((4, m, n), got.dtype),
        compiler_params=_cparams(("arbitrary",)),
    )(owners, part, got)


def in_proj(x, g_pre, wt, bias, comms=()):
    s, d = x.shape
    tm = _tile(s, (512, 256, 128))
    tn = 768
    ni, nj = s // tm, D_IN // tn
    hosted = _Hosted(comms)

    def body(*refs):
        (x_ref, g_ref, w_ref, b_ref), (proj_ref, h_ref), _, phases = hosted.split(refs, 4, 2, 0)
        step = pl.program_id(0) * nj + pl.program_id(1)
        _before_step(phases, step, ni * nj)

        @pl.when(pl.program_id(1) == 0)
        def _():
            xv = x_ref[...]
            r = lax.rsqrt(jnp.mean(xv * xv, axis=-1, keepdims=True) + EPS)
            h_ref[...] = (xv * r * g_ref[...]).astype(BF16)

        acc = _dot(h_ref[...], w_ref[...], 1, 1)
        proj_ref[...] = (acc + b_ref[...]).astype(BF16)
        _after_step(phases, step, ni * nj)

    return pl.pallas_call(
        body, name="in_proj", grid=(ni, nj),
        in_specs=[pl.BlockSpec((tm, d), lambda i, j: (i, 0)),
                  pl.BlockSpec((1, d), lambda i, j: (0, 0)),
                  pl.BlockSpec((tn, d), lambda i, j: (j, 0)),
                  pl.BlockSpec((1, tn), lambda i, j: (0, j))] + hosted.in_specs,
        out_specs=[pl.BlockSpec((tm, tn), lambda i, j: (i, j)),
                   pl.BlockSpec((tm, d), lambda i, j: (i, 0))] + hosted.out_specs,
        out_shape=[jax.ShapeDtypeStruct((s, D_IN), BF16), jax.ShapeDtypeStruct((s, d), BF16)] + hosted.out_shape,
        scratch_shapes=hosted.scratch,
        compiler_params=_cparams(("arbitrary", "arbitrary")),
    )(x, g_pre, wt, bias, *hosted.inputs)


def in_proj_gather(x, pos_col, freq, sign, g_pre, wt_shard, bias):
    s, d = x.shape
    tm = _tile(s, (512, 256, 128))
    nt = s // tm
    m = wt_shard.shape[0]
    half = D_IN // 2
    xi = lax.axis_index("x")
    order = jnp.stack([xi, 1 - xi]).astype(jnp.int32)

    def body(order_ref, x_ref, pos_ref, freq_ref, sign_ref, g_ref, b_ref, shard_ref,
             proj_ref, h_ref, rope_ref, wt_ref, w_vmem, send_sems, recv_sems, local_sems):
        del order_ref
        p, i = pl.program_id(0), pl.program_id(1)
        xx, yy, cc = _my_place()
        me, sibling = (xx, yy, cc), (xx, yy, 1 - cc)
        chips = [_chip_of(xx, yy, r) for r in (1, 2, 3)]

        def rows(px, py, pc):
            return wt_ref.at[pl.ds(pl.multiple_of(_index(px, py, pc) * m, 16), m), :]

        def copy(k, block, to, src=None):
            return pltpu.make_async_remote_copy(
                src_ref=rows(*block) if src is None else src, dst_ref=rows(*block),
                send_sem=send_sems.at[k], recv_sem=recv_sems.at[k], device_id=to, device_id_type=MESH)

        def mine():
            return pltpu.make_async_copy(shard_ref, rows(*me), local_sems.at[0])

        def to_sibling():
            return copy(0, me, sibling, src=shard_ref)

        def to_chip(j):
            return copy(1 + j, me, (*chips[j], cc), src=shard_ref)

        def relay(j):
            copy(1 + j, (*chips[j], cc), me).wait_recv()
            copy(4 + j, (*chips[j], cc), sibling).start()

        def relayed(j):
            copy(4 + j, (*chips[j], 1 - cc), me).wait_recv()

        def load_half(which, slot):
            rows_of_half = wt_ref.at[pl.ds(pl.multiple_of(which * half, 16), half), :]
            load = pltpu.make_async_copy(rows_of_half, w_vmem.at[slot], local_sems.at[1 + slot])
            load.start()
            load.wait()

        @pl.when(jnp.logical_and(p == 0, i == 0))
        def _():
            mine().start()
            to_sibling().start()
            to_chip(1).start()
            to_chip(0).start()
            copy(0, sibling, me).wait_recv()
            relay(1)
            relayed(1)
            mine().wait()
            to_chip(1).wait_send()
            to_chip(0).wait_send()
            to_chip(2).start()
            load_half(xx, 0)

        @pl.when(jnp.logical_and(p == 1, i == 0))
        def _():
            relayed(0)
            relayed(2)
            load_half(1 - xx, 1)

        xv = x_ref[...]
        r = lax.rsqrt(jnp.mean(xv * xv, axis=-1, keepdims=True) + EPS)
        hb = (xv * r * g_ref[...]).astype(BF16)
        proj_ref[...] = (_dot(hb, w_vmem[p], 1, 1) + b_ref[...]).astype(BF16)

        @pl.when(p == 0)
        def _():
            h_ref[...] = hb
            ang = pos_ref[...].astype(F32) * freq_ref[...]
            rope_ref[:, :LANES] = jnp.cos(ang)
            rope_ref[:, LANES:] = jnp.sin(ang) * sign_ref[...]

        @pl.when(jnp.logical_and(p == 0, i == 1))
        def _():
            relay(0)

        @pl.when(jnp.logical_and(p == 0, i == nt - 1))
        def _():
            relay(2)

        @pl.when(jnp.logical_and(p == 1, i == nt - 1))
        def _():
            to_sibling().wait_send()
            to_chip(2).wait_send()
            for j in range(3):
                copy(4 + j, (*chips[j], cc), sibling).wait_send()

    const = lambda p, i, o: (0, 0)
    once = lambda p, i, o: (jnp.where(p == 0, i, nt - 1), 0)
    return pl.pallas_call(
        body, name="in_proj_gather",
        grid_spec=pltpu.PrefetchScalarGridSpec(
            num_scalar_prefetch=1, grid=(2, nt),
            in_specs=[pl.BlockSpec((tm, d), lambda p, i, o: (i, 0)),
                      pl.BlockSpec((tm, 1), lambda p, i, o: (i, 0)),
                      pl.BlockSpec((1, LANES), const),
                      pl.BlockSpec((1, LANES), const),
                      pl.BlockSpec((1, d), const),
                      pl.BlockSpec((1, half), lambda p, i, o: (0, o[p])),
                      ANY],
            out_specs=[pl.BlockSpec((tm, half), lambda p, i, o: (i, o[p])),
                       pl.BlockSpec((tm, d), once),
                       pl.BlockSpec((tm, 2 * LANES), once),
                       ANY],
            scratch_shapes=[pltpu.VMEM((2, half, d), BF16), pltpu.SemaphoreType.DMA((7,)),
                            pltpu.SemaphoreType.DMA((7,)), pltpu.SemaphoreType.DMA((3,))]),
        out_shape=[jax.ShapeDtypeStruct((s, D_IN), BF16), jax.ShapeDtypeStruct((s, d), BF16),
                   jax.ShapeDtypeStruct((s, 2 * LANES), F32), jax.ShapeDtypeStruct((D_IN, d), BF16)],
        compiler_params=_cparams(("arbitrary", "arbitrary")),
    )(order, x, pos_col, freq, sign, g_pre, bias, wt_shard)


def out_proj_loss(cat, w_out, x, target, g_post):
    s, d = x.shape
    tm = _tile(s, (256, 128))

    def body(cat_ref, w_ref, x_ref, t_ref, g_ref, dy_ref, dout_ref, dg_ref, loss_ref):
        @pl.when(pl.program_id(0) == 0)
        def _():
            dg_ref[...] = jnp.zeros_like(dg_ref)
            loss_ref[...] = jnp.zeros_like(loss_ref)

        g = g_ref[...]
        ys = [_dot(cat_ref[c0:c0 + CHUNK, :], w_ref[...], 1, 0) for c0 in range(0, tm, CHUNK)]
        for c0 in range(0, tm, CHUNK):
            rows = slice(c0, c0 + CHUNK)
            yv = ys[c0 // CHUNK]
            r = lax.rsqrt(jnp.mean(yv * yv, axis=-1, keepdims=True) + EPS)
            nrm = yv * r
            err = x_ref[rows, :] + nrm * g - t_ref[rows, :]
            loss_ref[...] += 0.5 * jnp.sum(jnp.sum(err * err, axis=-1, keepdims=True), axis=0, keepdims=True) / d
            dout = err * (1.0 / d)
            dout_ref[rows, :] = dout
            dg_ref[...] += jnp.sum(dout * nrm, axis=0, keepdims=True)
            dn = dout * g
            dy = r * (dn - nrm * jnp.mean(dn * nrm, axis=-1, keepdims=True))
            dy_ref[rows, :] = dy.astype(BF16)

    return pl.pallas_call(
        body, name="out_proj_loss", grid=(s // tm,),
        in_specs=[pl.BlockSpec((tm, d), lambda i: (i, 0)),
                  pl.BlockSpec((d, d), lambda i: (0, 0)),
                  pl.BlockSpec((tm, d), lambda i: (i, 0)),
                  pl.BlockSpec((tm, d), lambda i: (i, 0)),
                  pl.BlockSpec((1, d), lambda i: (0, 0))],
        out_specs=[pl.BlockSpec((tm, d), lambda i: (i, 0)),
                   pl.BlockSpec((tm, d), lambda i: (i, 0)),
                   pl.BlockSpec((1, d), lambda i: (0, 0)),
                   pl.BlockSpec((1, LANES), lambda i: (0, 0))],
        out_shape=[jax.ShapeDtypeStruct((s, d), BF16), jax.ShapeDtypeStruct((s, d), F32),
                   jax.ShapeDtypeStruct((1, d), F32), jax.ShapeDtypeStruct((1, LANES), F32)],
        compiler_params=_cparams(("arbitrary",)),
    )(cat, w_out, x, target, g_post)


def matmul_nt(a, b, name):
    m, k = a.shape
    n = b.shape[0]
    tm = _tile(m, (512, 256, 128))

    def body(a_ref, b_ref, o_ref):
        o_ref[...] = _dot(a_ref[...], b_ref[...], 1, 1).astype(o_ref.dtype)

    return pl.pallas_call(
        body, name=name, grid=(m // tm,),
        in_specs=[pl.BlockSpec((tm, k), lambda i: (i, 0)), pl.BlockSpec((n, k), lambda i: (0, 0))],
        out_specs=pl.BlockSpec((tm, n), lambda i: (i, 0)),
        out_shape=jax.ShapeDtypeStruct((m, n), BF16),
        compiler_params=_cparams(("arbitrary",)),
    )(a, b)


def matmul_tn(a, b, tm, name, comms=()):
    k, m = a.shape
    n = b.shape[1]
    steps = m // tm
    hosted = _Hosted(comms)

    kc = _tile(k, (1024, 128))
    pieces = k // kc

    def body(*refs):
        (a_ref, b_hbm), (o_ref, cs_ref), (b_ref, b_sems), phases = hosted.split(refs, 2, 2, 2)
        step = pl.program_id(0)
        _before_step(phases, step, steps)

        def b_load(j):
            return pltpu.make_async_copy(b_hbm.at[j * kc:(j + 1) * kc, :], b_ref.at[j * kc:(j + 1) * kc, :], b_sems.at[j])

        @pl.when(step == 0)
        def _():
            for j in range(pieces):
                b_load(j).start()
            acc = None
            for j in range(pieces):
                b_load(j).wait()
                part = _dot(a_ref[j * kc:(j + 1) * kc, :], b_ref[j * kc:(j + 1) * kc, :], 0, 0)
                acc = part if acc is None else acc + part
            o_ref[...] = acc.astype(o_ref.dtype)

        @pl.when(step > 0)
        def _():
            o_ref[...] = _dot(a_ref[...], b_ref[...], 0, 0).astype(o_ref.dtype)

        rows = _tile(k, (512, 128))
        cs = jnp.zeros((1, tm), F32)
        for r0 in range(0, k, rows):
            cs = cs + jnp.sum(a_ref[r0:r0 + rows, :].astype(F32), axis=0, keepdims=True)
        cs_ref[...] = cs
        _after_step(phases, step, steps)

    return pl.pallas_call(
        body, name=name, grid=(steps,),
        in_specs=[pl.BlockSpec((k, tm), lambda i: (0, i)), ANY] + hosted.in_specs,
        out_specs=[pl.BlockSpec((tm, n), lambda i: (i, 0)), pl.BlockSpec((1, tm), lambda i: (0, i))] + hosted.out_specs,
        out_shape=[jax.ShapeDtypeStruct((m, n), BF16), jax.ShapeDtypeStruct((1, m), F32)] + hosted.out_shape,
        scratch_shapes=[pltpu.VMEM((k, n), b.dtype), pltpu.SemaphoreType.DMA((pieces,))] + hosted.scratch,
        compiler_params=_cparams(("arbitrary",)),
    )(a, b, *hosted.inputs)


def in_proj_bwd(dproj, wt, x, g_pre, dout, comms=()):
    s, d = x.shape
    tm = _tile(s, (512, 256, 128))
    steps = s // tm
    nsub = tm // CHUNK
    kchunks = [(k0, 1024) for k0 in range(0, 5120, 1024)] + [(5120, 256)]
    ksplit = len(kchunks)
    hosted = _Hosted(comms)

    def body(*refs):
        ((*dp_refs, w_hbm, x_hbm, g_ref, dout_hbm), (gx_hbm, dg_ref),
         (w_ref, w_sems, xbuf, dbuf, gbuf, in_sems, out_sems), phases) = hosted.split(refs, 4 + ksplit, 2, 7)
        step = pl.program_id(0)
        _before_step(phases, step, steps)

        def rows_of(ref, c):
            return ref.at[pl.ds(pl.multiple_of(step * tm + c * CHUNK, CHUNK), CHUNK), :]

        def fetches(c):
            return (pltpu.make_async_copy(rows_of(x_hbm, c), xbuf.at[c], in_sems.at[c]),
                    pltpu.make_async_copy(rows_of(dout_hbm, c), dbuf.at[c], in_sems.at[nsub + c]))

        def put(c):
            return pltpu.make_async_copy(gbuf.at[c % 2], rows_of(gx_hbm, c), out_sems.at[c % 2])

        for c in range(nsub):
            for cp in fetches(c):
                cp.start()

        def w_load(j):
            k0, kw = kchunks[j]
            return pltpu.make_async_copy(w_hbm.at[k0:k0 + kw, :], w_ref.at[k0:k0 + kw, :], w_sems.at[j])

        @pl.when(step == 0)
        def _():
            dg_ref[...] = jnp.zeros_like(dg_ref)
            for j in range(ksplit):
                w_load(j).start()

        dh_all = None
        for j, ((k0, kw), dp_ref) in enumerate(zip(kchunks, dp_refs)):
            @pl.when(step == 0)
            def _():
                w_load(j).wait()

            part = _dot(dp_ref[...], w_ref[k0:k0 + kw, :], 1, 0)
            dh_all = part if dh_all is None else dh_all + part
        for c in range(nsub):
            for cp in fetches(c):
                cp.wait()
            if c >= 2:
                put(c - 2).wait()
            elif c < nsub:
                @pl.when(step > 0)
                def _():
                    put(max(nsub - 2, 0) + c).wait()
            dh = dh_all[c * CHUNK:(c + 1) * CHUNK, :]
            xv = xbuf[c]
            r = lax.rsqrt(jnp.mean(xv * xv, axis=-1, keepdims=True) + EPS)
            xn = xv * r
            dg_ref[...] += jnp.sum(dh * xn, axis=0, keepdims=True)
            dn = dh * g_ref[...]
            gbuf[c % 2] = dbuf[c] + r * (dn - xn * jnp.mean(dn * xn, axis=-1, keepdims=True))
            put(c).start()
        @pl.when(step == steps - 1)
        def _():
            for c in range(max(nsub - 2, 0), nsub):
                put(c).wait()

        _after_step(phases, step, steps)

    side_in, side_out = pltpu.VMEM((nsub, CHUNK, d), F32), pltpu.VMEM((2, CHUNK, d), F32)
    row = pl.BlockSpec((1, d), lambda i: (0, 0))
    return pl.pallas_call(
        body, name="in_proj_bwd", grid=(steps,),
        in_specs=[pl.BlockSpec((tm, kw), functools.partial(lambda j, i: (i, j), k0 // kw)) for k0, kw in kchunks]
        + [ANY, ANY, row, ANY] + hosted.in_specs,
        out_specs=[ANY, row] + hosted.out_specs,
        out_shape=[jax.ShapeDtypeStruct((s, d), F32), jax.ShapeDtypeStruct((1, d), F32)] + hosted.out_shape,
        scratch_shapes=[pltpu.VMEM((D_IN, d), BF16), pltpu.SemaphoreType.DMA((ksplit,)), side_in, side_in, side_out,
                        pltpu.SemaphoreType.DMA((2 * nsub,)), pltpu.SemaphoreType.DMA((2,))] + hosted.scratch,
        compiler_params=_cparams(("arbitrary",)),
    )(*([dproj] * ksplit), wt, x, g_pre, dout, *hosted.inputs)


def _lane_iota(shape):
    return lax.broadcasted_iota(jnp.int32, shape, len(shape) - 1)


def rope_tables(pos_col, freq, sign, comms=()):
    s = pos_col.shape[0]
    tr = _tile(s, (512, 256, 128))
    hosted = _Hosted(comms)

    def body(*refs):
        (pos_ref, freq_ref, sign_ref), (out_ref,), _, phases = hosted.split(refs, 3, 1, 0)
        _before_step(phases, pl.program_id(0), s // tr)
        ang = pos_ref[...].astype(F32) * freq_ref[...]
        out_ref[:, :LANES] = jnp.cos(ang)
        out_ref[:, LANES:] = jnp.sin(ang) * sign_ref[...]
        _after_step(phases, pl.program_id(0), s // tr)

    return pl.pallas_call(
        body, name="rope_tables", grid=(s // tr,),
        in_specs=[pl.BlockSpec((tr, 1), lambda i: (i, 0)), pl.BlockSpec((1, LANES), lambda i: (0, 0)),
                  pl.BlockSpec((1, LANES), lambda i: (0, 0))] + hosted.in_specs,
        out_specs=[pl.BlockSpec((tr, 2 * LANES), lambda i: (i, 0))] + hosted.out_specs,
        out_shape=[jax.ShapeDtypeStruct((s, 2 * LANES), F32)] + hosted.out_shape,
        scratch_shapes=hosted.scratch,
        compiler_params=_cparams(("arbitrary",)),
    )(pos_col, freq, sign, *hosted.inputs)


def _partner(v):
    low = (_lane_iota(v.shape) % HEAD_DIM) < (HEAD_DIM // 2)
    return jnp.where(low, pltpu.roll(v, LANES - HEAD_DIM // 2, 1), pltpu.roll(v, HEAD_DIM // 2, 1))


def _rope(v, cos, sin_signed):
    return v * cos + _partner(v) * sin_signed


def _rope_transposed(dv, cos, sin_signed):
    return dv * cos - _partner(dv) * sin_signed


def _both_halves(v, kv_head):
    keep = (_lane_iota(v.shape) >= HEAD_DIM) if kv_head else (_lane_iota(v.shape) < HEAD_DIM)
    return jnp.where(keep, v, pltpu.roll(v, HEAD_DIM, 1))


def _fold_halves(acc):
    return acc + pltpu.roll(acc, HEAD_DIM, 1)


def _by_half(a, b):
    shape = jnp.broadcast_shapes(jnp.shape(a), jnp.shape(b))
    return jnp.where(_lane_iota(shape) < HEAD_DIM, a, b)


def _stack_heads(pair):
    return jnp.concatenate([_by_half(pair, 0.0), _by_half(0.0, pair)], axis=0)


def _band_bias(has_prev):
    i = lax.broadcasted_iota(jnp.int32, (2 * CHUNK, 2 * CHUNK), 0) % CHUNK
    j = lax.broadcasted_iota(jnp.int32, (2 * CHUNK, 2 * CHUNK), 1)
    band = jnp.logical_and(j > i, j <= i + CHUNK)
    return jnp.where(jnp.logical_and(band, jnp.logical_or(j >= CHUNK, has_prev)), 0.0, NEG)


def _probs(qm2, kk2, bias, sink_col):
    sc = _dot(qm2, kk2, 1, 1) + bias
    mx = jnp.maximum(jnp.max(sc, axis=-1, keepdims=True), sink_col)
    p = jnp.exp(sc - mx)
    es = jnp.exp(sink_col - mx)
    inv = 1.0 / (jnp.sum(p, axis=-1, keepdims=True) + es)
    return p * inv, es * inv


def _probs_staged(qm2s, kk2s, bias, sink_cols):
    k = range(len(qm2s))
    scs = [_dot(qm2s[i], kk2s[i], 1, 1) + bias for i in k]
    mxs = [jnp.maximum(jnp.max(scs[i], axis=-1, keepdims=True), sink_cols[i]) for i in k]
    ps = [jnp.exp(scs[i] - mxs[i]) for i in k]
    ess = [jnp.exp(sink_cols[i] - mxs[i]) for i in k]
    invs = [1.0 / (jnp.sum(ps[i], axis=-1, keepdims=True) + ess[i]) for i in k]
    return [ps[i] * invs[i] for i in k], [ess[i] * invs[i] for i in k]


def _sink_col(sinks_ref, pair):
    row = lax.broadcasted_iota(jnp.int32, (2 * CHUNK, 1), 0)
    return jnp.where(row < CHUNK, sinks_ref[2 * pair], sinks_ref[2 * pair + 1])


def _layer_norm_parts(v):
    mu = jnp.mean(v, axis=-1, keepdims=True)
    xc = v - mu
    rstd = lax.rsqrt(jnp.mean(xc * xc, axis=-1, keepdims=True) + EPS)
    return xc * rstd, rstd


def _masked_spatial(w_ref, g):
    t = lax.broadcasted_iota(jnp.int32, (CHUNK, CHUNK), 0)
    sidx = lax.broadcasted_iota(jnp.int32, (CHUNK, CHUNK), 1)
    return jnp.where(t >= sidx, w_ref[g], 0.0).astype(BF16)


def _keys_values(kv_ref, kvp_ref, rope_ref, ropep_ref):
    cos_c, sin_c = rope_ref[:, :LANES], rope_ref[:, LANES:]
    cos_p, sin_p = ropep_ref[:, :LANES], ropep_ref[:, LANES:]
    k_c = _rope(kv_ref[:, :D_KV].astype(F32), cos_c, sin_c)
    k_p = _rope(kvp_ref[:, :D_KV].astype(F32), cos_p, sin_p)
    keys = jnp.concatenate([k_p, k_c], axis=0)
    vals = jnp.concatenate([kvp_ref[:, D_KV:], kv_ref[:, D_KV:]], axis=0).astype(F32)
    return keys, vals, (cos_c, sin_c, cos_p, sin_p)


def mixer_fwd(proj, rope, ln_g, ln_b, w_sp, b_sp_rows, sinks, comms=()):
    s = proj.shape[0]
    nb = s // CHUNK
    hosted = _Hosted(comms)

    def body(sinks_ref, *refs):
        ((proj_ref, kvp_ref, rope_ref, ropep_ref, lng_ref, lnb_ref, w_ref, b_ref), (cat_ref, p_ref), _,
         phases) = hosted.split(refs, 8, 2, 0)
        n = pl.program_id(0)
        _before_step(phases, n, nb)
        xhat, _ = _layer_norm_parts(proj_ref[:, OFF_V:OFF_V + D_GMLP].astype(F32))
        vnb = (xhat * lng_ref[...] + lnb_ref[...]).astype(BF16)
        mixeds = [_dot(_masked_spatial(w_ref, g), vnb[:, g * CHUNK:(g + 1) * CHUNK], 1, 0) + b_ref[g]
                  for g in range(GROUPS)]
        for g in range(GROUPS):
            za = proj_ref[:, OFF_ZA + g * CHUNK:OFF_ZA + (g + 1) * CHUNK].astype(F32)
            u = proj_ref[:, OFF_U + g * CHUNK:OFF_U + (g + 1) * CHUNK].astype(F32)
            cat_ref[:, g * CHUNK:(g + 1) * CHUNK] = (u * mixeds[g] * (za * _sigmoid(za))).astype(BF16)
        kv_ref = proj_ref.at[:, OFF_K:OFF_K + 2 * D_KV]
        keys, vals, (cos_c, sin_c, _, _) = _keys_values(kv_ref, kvp_ref, rope_ref, ropep_ref)
        cos_q, sin_q = cos_c * SCALE, sin_c * SCALE
        bias = _band_bias(n > 0)
        kk2 = [_both_halves(keys, kvh).astype(BF16) for kvh in range(N_KV_HEADS)]
        vv2 = [_both_halves(vals, kvh).astype(BF16) for kvh in range(N_KV_HEADS)]
        pairs = range(N_PAIRS)
        qms = [_stack_heads(_rope(proj_ref[:, OFF_Q + pair * LANES:OFF_Q + (pair + 1) * LANES].astype(F32),
                                  cos_q, sin_q)).astype(BF16) for pair in pairs]
        probs, sink_probs = _probs_staged(qms, [kk2[pair // PAIRS_PER_KV] for pair in pairs], bias,
                                          [_sink_col(sinks_ref, pair) for pair in pairs])
        pbs = [p.astype(BF16) for p in probs]
        outs = [_dot(pbs[pair], vv2[pair // PAIRS_PER_KV], 1, 0) for pair in pairs]
        first_col = _lane_iota((2 * CHUNK, 2 * CHUNK)) == 0
        for pair in pairs:
            p_ref[0, pair] = jnp.where(first_col, sink_probs[pair].astype(BF16), pbs[pair])
        for pair in pairs:
            out_pair = _by_half(outs[pair][:CHUNK], outs[pair][CHUNK:])
            zb = proj_ref[:, OFF_ZB + pair * LANES:OFF_ZB + (pair + 1) * LANES].astype(F32)
            cat_ref[:, D_GMLP + pair * LANES:D_GMLP + (pair + 1) * LANES] = (
                out_pair * (zb * _sigmoid(zb))).astype(BF16)
        _after_step(phases, n, nb)

    prev = lambda n, *_: (jnp.maximum(n - 1, 0), 0)
    kv_block = OFF_K // (2 * D_KV)
    return pl.pallas_call(
        body, name="mixer_fwd",
        grid_spec=pltpu.PrefetchScalarGridSpec(
            num_scalar_prefetch=1, grid=(nb,),
            in_specs=[pl.BlockSpec((CHUNK, D_IN), lambda n, *_: (n, 0)),
                      pl.BlockSpec((CHUNK, 2 * D_KV), lambda n, *_: (jnp.maximum(n - 1, 0), kv_block)),
                      pl.BlockSpec((CHUNK, 2 * LANES), lambda n, *_: (n, 0)),
                      pl.BlockSpec((CHUNK, 2 * LANES), prev),
                      pl.BlockSpec((1, D_GMLP), lambda n, *_: (0, 0)),
                      pl.BlockSpec((1, D_GMLP), lambda n, *_: (0, 0)),
                      pl.BlockSpec((GROUPS, CHUNK, CHUNK), lambda n, *_: (0, 0, 0)),
                      pl.BlockSpec((GROUPS, CHUNK, CHUNK), lambda n, *_: (0, 0, 0))] + hosted.in_specs,
            out_specs=[pl.BlockSpec((CHUNK, D_GMLP + D_ATTN), lambda n, *_: (n, 0)),
                       pl.BlockSpec((1, N_PAIRS, 2 * CHUNK, 2 * CHUNK), lambda n, *_: (n, 0, 0, 0))]
            + hosted.out_specs,
            scratch_shapes=hosted.scratch),
        out_shape=[jax.ShapeDtypeStruct((s, D_GMLP + D_ATTN), BF16),
                   jax.ShapeDtypeStruct((nb, N_PAIRS, 2 * CHUNK, 2 * CHUNK), BF16)] + hosted.out_shape,
        compiler_params=_cparams(("arbitrary",)),
    )(sinks, proj, proj, rope, rope, ln_g, ln_b, w_sp, b_sp_rows, *hosted.inputs)


def mixer_bwd(proj, dcat, probs, rope, ln_g, ln_b, w_sp, b_sp_rows, comms=()):
    s = proj.shape[0]
    nb = s // CHUNK
    hosted = _Hosted(comms)

    def body(*refs):
        ((proj_ref, kvp_ref, dcat_ref, p_ref, rope_ref, ropep_ref, lng_ref, lnb_ref, w_ref, b_ref),
         (dproj_ref, dw_ref, db_ref, dlng_ref, dlnb_ref, dsink_ref),
         (pend_ref, pend_kv_ref, dbacc_ref), phases) = hosted.split(refs, 10, 6, 3)
        n = pl.program_id(0)
        _before_step(phases, n, nb + 1)

        @pl.when(n == 0)
        def _():
            dw_ref[...] = jnp.zeros_like(dw_ref)
            dbacc_ref[...] = jnp.zeros_like(dbacc_ref)
            dlng_ref[...] = jnp.zeros_like(dlng_ref)
            dlnb_ref[...] = jnp.zeros_like(dlnb_ref)
            dsink_ref[...] = jnp.zeros_like(dsink_ref)

        @pl.when(n > 0)
        def _():
            dproj_ref[...] = pend_ref[...]

        def flush(dkv_prev):
            @pl.when(n > 0)
            def _():
                dproj_ref[:, OFF_K:OFF_K + 2 * D_KV] = (pend_kv_ref[...] + dkv_prev).astype(BF16)

        @pl.when(n < nb)
        def _():
            kv_ref = proj_ref.at[:, OFF_K:OFF_K + 2 * D_KV]
            keys, vals, (cos_c, sin_c, cos_p, sin_p) = _keys_values(kv_ref, kvp_ref, rope_ref, ropep_ref)
            cos_q, sin_q = cos_c * SCALE, sin_c * SCALE
            first_col = _lane_iota((2 * CHUNK, 2 * CHUNK)) == 0
            lane_row = _lane_iota((1, LANES))
            dsink = jnp.zeros((1, LANES), F32)
            dk_heads, dv_heads = [], []
            for kvh in range(N_KV_HEADS):
                kk2 = _both_halves(keys, kvh).astype(BF16)
                vv2 = _both_halves(vals, kvh).astype(BF16)
                pairs = list(range(kvh * PAIRS_PER_KV, (kvh + 1) * PAIRS_PER_KV))
                k4 = range(PAIRS_PER_KV)
                qm2s = [_stack_heads(_rope(proj_ref[:, OFF_Q + pair * LANES:OFF_Q + (pair + 1) * LANES].astype(F32),
                                           cos_q, sin_q)).astype(BF16) for pair in pairs]
                kept = [p_ref[0, pair] for pair in pairs]
                pbs = [jnp.where(first_col, jnp.zeros_like(kp), kp) for kp in kept]
                ps = [pb.astype(F32) for pb in pbs]
                p_sinks = [kp[:, 0:1].astype(F32) for kp in kept]
                o2s = [_dot(pb, vv2, 1, 0) for pb in pbs]
                zbs = [proj_ref[:, OFF_ZB + pair * LANES:OFF_ZB + (pair + 1) * LANES].astype(F32) for pair in pairs]
                sgs = [_sigmoid(zb) for zb in zbs]
                dybs = [dcat_ref[:, D_GMLP + pair * LANES:D_GMLP + (pair + 1) * LANES].astype(F32) for pair in pairs]
                for i, pair in enumerate(pairs):
                    out_pair = _by_half(o2s[i][:CHUNK], o2s[i][CHUNK:])
                    pend_ref[:, OFF_ZB + pair * LANES:OFF_ZB + (pair + 1) * LANES] = (
                        dybs[i] * out_pair * (sgs[i] * (1.0 + zbs[i] * (1.0 - sgs[i])))).astype(BF16)
                dom2s = [_stack_heads(dybs[i] * (zbs[i] * sgs[i])).astype(BF16) for i in k4]
                dps = [_dot(dom2, vv2, 1, 1) for dom2 in dom2s]
                deltas = [jnp.sum(ps[i] * dps[i], axis=-1, keepdims=True) for i in k4]
                dss = [ps[i] * (dps[i] - deltas[i]) for i in k4]
                for i, pair in enumerate(pairs):
                    dsk = -(p_sinks[i] * deltas[i])
                    dsink = dsink + jnp.where(lane_row == 2 * pair,
                                              jnp.sum(dsk[:CHUNK], axis=0, keepdims=True), 0.0)
                    dsink = dsink + jnp.where(lane_row == 2 * pair + 1,
                                              jnp.sum(dsk[CHUNK:], axis=0, keepdims=True), 0.0)
                dsbs = [ds.astype(BF16) for ds in dss]
                dq2s = [_dot(dsb, kk2, 1, 0) for dsb in dsbs]
                for pair, dq2 in zip(pairs, dq2s):
                    pend_ref[:, OFF_Q + pair * LANES:OFF_Q + (pair + 1) * LANES] = _rope_transposed(
                        _by_half(dq2[:CHUNK], dq2[CHUNK:]), cos_q, sin_q).astype(BF16)
                dkks = [_dot(dsbs[i], qm2s[i], 0, 0) for i in k4]
                dvvs = [_dot(pbs[i], dom2s[i], 0, 0) for i in k4]
                dk_heads.append(_fold_halves((dkks[0] + dkks[1]) + (dkks[2] + dkks[3])))
                dv_heads.append(_fold_halves((dvvs[0] + dvvs[1]) + (dvvs[2] + dvvs[3])))
            dk_rot = _by_half(dk_heads[0], dk_heads[1])
            dv_all = _by_half(dv_heads[0], dv_heads[1])
            dk_p = _rope_transposed(dk_rot[:CHUNK], cos_p, sin_p)
            dk_c = _rope_transposed(dk_rot[CHUNK:], cos_c, sin_c)
            flush(jnp.concatenate([dk_p, dv_all[:CHUNK]], axis=1))
            dsink_ref[...] += dsink
            pend_kv_ref[...] = jnp.concatenate([dk_c, dv_all[CHUNK:]], axis=1)
            xhat, rstd = _layer_norm_parts(proj_ref[:, OFF_V:OFF_V + D_GMLP].astype(F32))
            lng = lng_ref[...]
            vnb = (xhat * lng + lnb_ref[...]).astype(BF16)
            dvn_cols = []
            for g in range(GROUPS):
                cols = slice(g * CHUNK, (g + 1) * CHUNK)
                wm = _masked_spatial(w_ref, g)
                mixed = _dot(wm, vnb[:, cols], 1, 0) + b_ref[g]
                za = proj_ref[:, OFF_ZA + g * CHUNK:OFF_ZA + (g + 1) * CHUNK].astype(F32)
                u = proj_ref[:, OFF_U + g * CHUNK:OFF_U + (g + 1) * CHUNK].astype(F32)
                dya = dcat_ref[:, cols].astype(F32)
                sg = _sigmoid(za)
                sz = za * sg
                pend_ref[:, OFF_U + g * CHUNK:OFF_U + (g + 1) * CHUNK] = (dya * mixed * sz).astype(BF16)
                pend_ref[:, OFF_ZA + g * CHUNK:OFF_ZA + (g + 1) * CHUNK] = (
                    dya * u * mixed * (sg * (1.0 + za * (1.0 - sg)))).astype(BF16)
                dmixed = dya * u * sz
                dmb = dmixed.astype(BF16)
                dbacc_ref[g] += dmixed
                dw_ref[g] += _dot(dmb, vnb[:, cols], 1, 1)
                dvn_cols.append(_dot(wm, dmb, 0, 0))
            dvn = jnp.concatenate(dvn_cols, axis=1)
            dlng_ref[...] += jnp.sum(dvn * xhat, axis=0, keepdims=True)
            dlnb_ref[...] += jnp.sum(dvn, axis=0, keepdims=True)
            dxh = dvn * lng
            dv = rstd * (dxh - jnp.mean(dxh, axis=-1, keepdims=True)
                         - xhat * jnp.mean(dxh * xhat, axis=-1, keepdims=True))
            pend_ref[:, OFF_V:OFF_V + D_GMLP] = dv.astype(BF16)

        @pl.when(n == nb)
        def _():
            flush(jnp.zeros((CHUNK, 2 * D_KV), F32))
            t = lax.broadcasted_iota(jnp.int32, (CHUNK, CHUNK), 0)
            sidx = lax.broadcasted_iota(jnp.int32, (CHUNK, CHUNK), 1)
            lane = _lane_iota((CHUNK, LANES))
            dbt = jnp.zeros((CHUNK, LANES), F32)
            for g in range(GROUPS):
                dw_ref[g] = jnp.where(t >= sidx, dw_ref[g], 0.0)
                dbt = jnp.where(lane == g, jnp.sum(dbacc_ref[g], axis=-1, keepdims=True), dbt)
            db_ref[...] = jnp.transpose(dbt)[:GROUPS, :]

        _after_step(phases, n, nb + 1)

    cur = lambda n: (jnp.minimum(n, nb - 1), 0)
    prev = lambda n: (jnp.clip(n - 1, 0, nb - 1), 0)
    kv_block = OFF_K // (2 * D_KV)
    const2 = lambda n: (0, 0)
    const3 = lambda n: (0, 0, 0)
    return pl.pallas_call(
        body, name="mixer_bwd", grid=(nb + 1,),
        in_specs=[pl.BlockSpec((CHUNK, D_IN), cur),
                  pl.BlockSpec((CHUNK, 2 * D_KV), lambda n: (jnp.clip(n - 1, 0, nb - 1), kv_block)),
                  pl.BlockSpec((CHUNK, D_GMLP + D_ATTN), cur),
                  pl.BlockSpec((1, N_PAIRS, 2 * CHUNK, 2 * CHUNK), lambda n: (jnp.minimum(n, nb - 1), 0, 0, 0)),
                  pl.BlockSpec((CHUNK, 2 * LANES), cur),
                  pl.BlockSpec((CHUNK, 2 * LANES), prev),
                  pl.BlockSpec((1, D_GMLP), const2),
                  pl.BlockSpec((1, D_GMLP), const2),
                  pl.BlockSpec((GROUPS, CHUNK, CHUNK), const3),
                  pl.BlockSpec((GROUPS, CHUNK, CHUNK), const3)] + hosted.in_specs,
        out_specs=[pl.BlockSpec((CHUNK, D_IN), lambda n: (jnp.maximum(n - 1, 0), 0)),
                   pl.BlockSpec((GROUPS, CHUNK, CHUNK), const3),
                   pl.BlockSpec((GROUPS, CHUNK), const2),
                   pl.BlockSpec((1, D_GMLP), const2),
                   pl.BlockSpec((1, D_GMLP), const2),
                   pl.BlockSpec((1, LANES), const2)] + hosted.out_specs,
        scratch_shapes=[pltpu.VMEM((CHUNK, D_IN), BF16), pltpu.VMEM((CHUNK, 2 * D_KV), F32),
                        pltpu.VMEM((GROUPS, CHUNK, CHUNK), F32)] + hosted.scratch,
        out_shape=[jax.ShapeDtypeStruct((s, D_IN), BF16),
                   jax.ShapeDtypeStruct((GROUPS, CHUNK, CHUNK), F32),
                   jax.ShapeDtypeStruct((GROUPS, CHUNK), F32),
                   jax.ShapeDtypeStruct((1, D_GMLP), F32),
                   jax.ShapeDtypeStruct((1, D_GMLP), F32),
                   jax.ShapeDtypeStruct((1, LANES), F32)] + hosted.out_shape,
        compiler_params=_cparams(("arbitrary",)),
    )(proj, proj, dcat, probs, rope, rope, ln_g, ln_b, w_sp, b_sp_rows, *hosted.inputs)


def _adamw_math(w, g, m, v):
    m = ADAM_B1 * m + (1.0 - ADAM_B1) * g
    v = ADAM_B2 * v + (1.0 - ADAM_B2) * (g * g)
    m_hat = m / (1.0 - ADAM_B1 ** ADAM_STEP)
    v_hat = v / (1.0 - ADAM_B2 ** ADAM_STEP)
    delta = -ADAM_LR * (m_hat / (jnp.sqrt(v_hat) + ADAM_EPS) + ADAM_WD * w)
    return delta, m, v


def adamw_shard(terms, w, m, v, name):
    r, c = w.shape
    tr = _tile(r, (224, 256, 128, 8))
    n_terms = len(terms)

    def body(*refs):
        w_ref, m_ref, v_ref, g_ref, d_ref, nm_ref, nv_ref = refs[n_terms:]
        g = None
        for ref, (_, slots) in zip(refs[:n_terms], terms):
            for k in range(slots):
                part = ref[k].astype(F32)
                g = part if g is None else g + part
        g_ref[...] = g
        d_ref[...], nm_ref[...], nv_ref[...] = _adamw_math(w_ref[...], g, m_ref[...], v_ref[...])

    spec = pl.BlockSpec((tr, c), lambda i: (i, 0))
    return pl.pallas_call(
        body, name=name, grid=(r // tr,),
        in_specs=[pl.BlockSpec((slots, tr, c), lambda i: (0, i, 0)) for _, slots in terms] + [spec] * 3,
        out_specs=[spec] * 4, out_shape=[jax.ShapeDtypeStruct((r, c), F32)] * 4,
        compiler_params=_cparams(("arbitrary",)),
    )(*[a for a, _ in terms], w, m, v)


def adamw_small(gathered, lane_windows, params):
    n_par = len(params)

    def body(*refs):
        g_refs = refs[:n_par + 1]
        wmv_refs = refs[n_par + 1:4 * n_par + 1]
        out_refs = refs[4 * n_par + 1:]

        def total(ref):
            acc = ref[0]
            for dev in range(1, N_DEV):
                acc = acc + ref[dev]
            return acc

        for i in range(n_par):
            w_ref, m_ref, v_ref = wmv_refs[3 * i:3 * i + 3]
            g = total(g_refs[i])
            if lane_windows[i] is not None:
                start, size = lane_windows[i]
                g = g[..., start:start + size]
            delta, new_m, new_v = _adamw_math(w_ref[...], g, m_ref[...], v_ref[...])
            for ref, val in zip(out_refs[4 * i:4 * i + 4], (g, delta, new_m, new_v)):
                ref[...] = val
        out_refs[4 * n_par][...] = total(g_refs[n_par])

    flat = [a for wmv in params for a in wmv]
    out_shape = [jax.ShapeDtypeStruct(w.shape, F32) for (w, _, _) in params for _ in range(4)]
    out_shape.append(jax.ShapeDtypeStruct(gathered[-1].shape[1:], F32))
    outs = pl.pallas_call(body, name="adamw_small", out_shape=out_shape, compiler_params=_cparams())(*gathered, *flat)
    return [tuple(outs[4 * i:4 * i + 4]) for i in range(n_par)], outs[-1]


def kernel(x, positions, g_pre, w_in, b_qkv, ln_v_g, ln_v_b, w_spatial, b_spatial, attn_sinks, w_out, g_post, loss_target, m_g_pre, m_w_in, m_b_qkv, m_ln_v_g, m_ln_v_b, m_w_spatial, m_b_spatial, m_attn_sinks, m_w_out, m_g_post, v_g_pre, v_w_in, v_b_qkv, v_ln_v_g, v_ln_v_b, v_w_spatial, v_b_spatial, v_attn_sinks, v_w_out, v_g_post):
    x2, target = x[0], loss_target[0]
    seq = x2.shape[0]
    xi, yi, ci = _my_place()

    wt_shard = w_in[0].T.astype(BF16)
    wo_shard = w_out[0].astype(BF16)
    pos_col = positions.reshape(seq, 1)
    half = HEAD_DIM // 2
    inv_freq = ROPE_THETA ** (-jnp.arange(half, dtype=F32) * (2.0 / HEAD_DIM))
    freq = jnp.tile(inv_freq, LANES // half).reshape(1, LANES)
    sign = jnp.tile(jnp.concatenate([-jnp.ones((half,), F32), jnp.ones((half,), F32)]), LANES // HEAD_DIM)
    sign = sign.reshape(1, LANES)
    bias = jnp.concatenate([jnp.zeros((1, OFF_Q), F32), b_qkv, jnp.zeros((1, D_ATTN), F32)], axis=1)
    proj, h, rope, wt = in_proj_gather(x2, pos_col, freq, sign, g_pre, wt_shard, bias)

    b_rows = jnp.broadcast_to(b_spatial[0][:, :, None], (GROUPS, CHUNK, CHUNK))
    sinks = attn_sinks[0]
    cat, probs, wo = mixer_fwd(proj, rope, ln_v_g, ln_v_b, w_spatial[0], b_rows, sinks,
                               comms=[gather_comm([wo_shard])])
    dy, dout, d_g_post, loss_part = out_proj_loss(cat, wo, x2, target, g_post)

    dcat = matmul_nt(dy, wo, "out_proj_bwd")
    d_wo, _ = matmul_tn(cat, dy, 512, "w_out_grad")
    dproj, d_w_sp, d_b_sp, d_ln_g, d_ln_b, d_sinks, parts_wo = mixer_bwd(
        proj, dcat, probs, rope, ln_v_g, ln_v_b, w_spatial[0], b_rows, comms=[scatter_comm([d_wo])])
    small_parts = [d_ln_g, d_ln_b, d_w_sp, d_b_sp, d_sinks, d_g_post, loss_part]
    d_wt, colsum, *landed = matmul_tn(dproj, h, 768, "w_in_grad", comms=[gather_comm(small_parts, stack=True)])

    owners = jnp.stack([4 * cx + 2 * cy + ci for cx, cy in (_chip_of(xi, yi, r) for r in range(4))]).astype(jnp.int32)
    (got_wt,) = run_comm(pair_comm([d_wt]), "grad_exchange_pair")
    sum_wt = pair_sum(d_wt, got_wt, owners, "grad_pair_sum_w_in")
    grad_x, d_g_pre, far_wt = in_proj_bwd(dproj, wt, x2, g_pre, dout, comms=[chips_comm([sum_wt])])
    late = run_comm(gather_comm([d_g_pre, colsum], stack=True, direct=True), "allgather_late_grads")
    gathered = late + landed
    windows = [None, (OFF_Q, D_QKV), None, None, None, None, (0, N_Q_HEADS), None]
    small = [(g_pre, m_g_pre, v_g_pre), (b_qkv, m_b_qkv, v_b_qkv), (ln_v_g, m_ln_v_g, v_ln_v_g),
             (ln_v_b, m_ln_v_b, v_ln_v_b), (w_spatial[0], m_w_spatial[0], v_w_spatial[0]),
             (b_spatial[0], m_b_spatial[0], v_b_spatial[0]), (attn_sinks, m_attn_sinks, v_attn_sinks),
             (g_post, m_g_post, v_g_post)]
    small_out, loss_row = adamw_small(gathered, windows, small)
    lead = [False, False, False, False, True, True, False, False]
    small_out = [tuple(a[None] if ld else a for a in leaf) for leaf, ld in zip(small_out, lead)]

    wt_out = adamw_shard([(sum_wt, 1), (far_wt, 3)], w_in[0].T, m_w_in[0].T, v_w_in[0].T, "adamw_w_in")
    wo_out = adamw_shard([(parts_wo, N_DEV)], w_out[0], m_w_out[0], v_w_out[0], "adamw_w_out")

    def leaves(k):
        gp, bq, lg, lb, ws, bs, sk, gpo = (leaf[k] for leaf in small_out)
        return [gp, wt_out[k].T[None], bq, lg, lb, ws, bs, sk, wo_out[k][None], gpo]

    return (loss_row[0, 0], grad_x[None], *leaves(0), *leaves(1), *leaves(2), *leaves(3))
```

```python
import functools

import jax
import jax.numpy as jnp
from jax import lax
from jax.experimental import pallas as pl
from jax.experimental.pallas import tpu as pltpu

F32 = jnp.float32
BF16 = jnp.bfloat16

D_MODEL = 2048
D_GMLP = 1024
D_ATTN = 1024
CHUNK = 128
GROUPS = 8
HEAD_DIM = 64
N_Q_HEADS = 16
N_KV_HEADS = 2
D_KV = N_KV_HEADS * HEAD_DIM
D_IN = 3 * D_GMLP + D_ATTN + 2 * D_KV + D_ATTN
OFF_U, OFF_V, OFF_ZA = 0, D_GMLP, 2 * D_GMLP
OFF_Q = 3 * D_GMLP
OFF_K = OFF_Q + D_ATTN
OFF_VA = OFF_K + D_KV
OFF_ZB = OFF_VA + D_KV
D_QKV = D_ATTN + 2 * D_KV
ROPE_THETA = 10000.0
EPS = 1e-6
SCALE = HEAD_DIM ** -0.5
NEG = -1e30
N_PAIRS = N_Q_HEADS // 2
PAIRS_PER_KV = N_PAIRS // N_KV_HEADS

ADAM_LR = 0.001
ADAM_B1 = 0.9
ADAM_B2 = 0.999
ADAM_EPS = 1e-08
ADAM_WD = 0.01
ADAM_STEP = 10

N_DEV = 8
LANES = 128
VMEM_LIMIT = 56 * 1024 * 1024
IN_PROJ_VMEM_LIMIT = 61 * 1024 * 1024

MESH = pl.DeviceIdType.MESH
ANY = pl.BlockSpec(memory_space=pl.ANY)


def _cparams(sem=None):
    return pltpu.CompilerParams(dimension_semantics=sem, vmem_limit_bytes=VMEM_LIMIT)


def _tile(n, prefs):
    for t in prefs:
        if n % t == 0:
            return t
    return n


def _sigmoid(z):
    return 1.0 / (1.0 + jnp.exp(-z))


def _dot(a, b, ca, cb):
    return lax.dot_general(a, b, (((ca,), (cb,)), ((), ())), preferred_element_type=F32)


def _my_place():
    return lax.axis_index("x"), lax.axis_index("y"), lax.axis_index("c")


def _chip_of(x, y, r):
    return (x ^ (r & 1), y ^ (r >> 1))


def _peer(x, y, c, k):
    return (x ^ (k >> 2), y ^ ((k >> 1) & 1), c ^ (k & 1))


def _index(px, py, pc):
    return 4 * px + 2 * py + pc


class _Comm:
    def __init__(self, inputs, out_shape, scratch, bind):
        self.inputs, self.out_shape, self.scratch, self.bind = list(inputs), list(out_shape), list(scratch), bind


def gather_comm(shards, stack=False, direct=False):
    n_arr = len(shards)

    def bind(ins, outs, sems):
        send_sems, recv_sems, local_sems = sems
        x, y, c = _my_place()
        me, sibling = (x, y, c), (x, y, 1 - c)
        chips = [_chip_of(x, y, r) for r in (1, 2, 3)]

        def rows(a, px, py, pc):
            d = _index(px, py, pc)
            if stack:
                return outs[a].at[d]
            m = shards[a].shape[0]
            return outs[a].at[pl.ds(pl.multiple_of(d * m, 8), m), :]

        def copy(a, k, block, to, src=None):
            return pltpu.make_async_remote_copy(
                src_ref=rows(a, *block) if src is None else src, dst_ref=rows(a, *block),
                send_sem=send_sems.at[a * 7 + k], recv_sem=recv_sems.at[a * 7 + k],
                device_id=to, device_id_type=MESH)

        def mine(a):
            return pltpu.make_async_copy(ins[a], rows(a, *me), local_sems.at[a])

        def own_sends(a):
            if direct:
                return [copy(a, k - 1, me, _peer(x, y, c, k), src=ins[a]) for k in range(1, 8)]
            return ([copy(a, 0, me, sibling, src=ins[a])]
                    + [copy(a, 1 + j, me, (*chip, c), src=ins[a]) for j, chip in enumerate(chips)])

        def start():
            for a in range(n_arr):
                mine(a).start()
                for cp in own_sends(a):
                    cp.start()

        def relay():
            if direct:
                return
            for j, chip in enumerate(chips):
                for a in range(n_arr):
                    copy(a, 1 + j, (*chip, c), me).wait_recv()
                    copy(a, 4 + j, (*chip, c), sibling).start()

        def finish():
            for a in range(n_arr):
                if direct:
                    for k in range(1, 8):
                        copy(a, k - 1, _peer(x, y, c, k), me).wait_recv()
                else:
                    copy(a, 0, sibling, me).wait_recv()
                    for j, chip in enumerate(chips):
                        copy(a, 4 + j, (*chip, 1 - c), me).wait_recv()
                        copy(a, 4 + j, (*chip, c), sibling).wait_send()
                for cp in own_sends(a):
                    cp.wait_send()
                mine(a).wait()

        return start, relay, finish

    def gathered(s):
        return (N_DEV, *s.shape) if stack else (N_DEV * s.shape[0], s.shape[1])

    return _Comm(shards, [jax.ShapeDtypeStruct(gathered(s), s.dtype) for s in shards],
                 [pltpu.SemaphoreType.DMA((7 * n_arr,)), pltpu.SemaphoreType.DMA((7 * n_arr,)),
                  pltpu.SemaphoreType.DMA((n_arr,))], bind)


def scatter_comm(parts):
    n_arr = len(parts)

    def bind(ins, outs, sems):
        send_sems, recv_sems, local_sems = sems
        x, y, c = _my_place()
        my_index = _index(x, y, c)

        def block(a, d):
            m = parts[a].shape[0] // N_DEV
            return ins[a].at[pl.ds(pl.multiple_of(d * m, 16), m), :]

        def copy(a, k, slot):
            peer = _peer(x, y, c, k)
            return pltpu.make_async_remote_copy(
                src_ref=block(a, _index(*peer)), dst_ref=outs[a].at[slot],
                send_sem=send_sems.at[a * 7 + k - 1], recv_sem=recv_sems.at[a * 7 + k - 1],
                device_id=peer, device_id_type=MESH)

        def mine(a):
            return pltpu.make_async_copy(block(a, my_index), outs[a].at[my_index], local_sems.at[a])

        def start():
            for a in range(n_arr):
                mine(a).start()
                for k in range(1, 8):
                    copy(a, k, my_index).start()

        def finish():
            for a in range(n_arr):
                for k in range(1, 8):
                    copy(a, k, _index(*_peer(x, y, c, k))).wait_recv()
                    copy(a, k, my_index).wait_send()
                mine(a).wait()

        return start, (lambda: None), finish

    return _Comm(parts, [jax.ShapeDtypeStruct((N_DEV, p.shape[0] // N_DEV, p.shape[1]), p.dtype) for p in parts],
                 [pltpu.SemaphoreType.DMA((7 * n_arr,)), pltpu.SemaphoreType.DMA((7 * n_arr,)),
                  pltpu.SemaphoreType.DMA((n_arr,))], bind)


def pair_comm(parts):
    n_arr = len(parts)

    def bind(ins, outs, sems):
        send_sems, recv_sems = sems
        x, y, c = _my_place()

        def copies():
            out = []
            for a in range(n_arr):
                m = parts[a].shape[0] // N_DEV
                for r in range(4):
                    owner = _index(*_chip_of(x, y, r), 1 - c)
                    out.append(pltpu.make_async_remote_copy(
                        src_ref=ins[a].at[pl.ds(pl.multiple_of(owner * m, 16), m), :], dst_ref=outs[a].at[r],
                        send_sem=send_sems.at[a * 4 + r], recv_sem=recv_sems.at[a * 4 + r],
                        device_id=(x, y, 1 - c), device_id_type=MESH))
            return out

        def start():
            for cp in copies():
                cp.start()

        def finish():
            for cp in copies():
                cp.wait_recv()
                cp.wait_send()

        return start, (lambda: None), finish

    return _Comm(parts, [jax.ShapeDtypeStruct((4, p.shape[0] // N_DEV, p.shape[1]), p.dtype) for p in parts],
                 [pltpu.SemaphoreType.DMA((4 * n_arr,)), pltpu.SemaphoreType.DMA((4 * n_arr,))], bind)


def chips_comm(sums):
    n_arr = len(sums)

    def bind(ins, outs, sems):
        send_sems, recv_sems = sems
        x, y, c = _my_place()

        def copies():
            return [pltpu.make_async_remote_copy(
                src_ref=ins[a].at[r], dst_ref=outs[a].at[r - 1],
                send_sem=send_sems.at[a * 3 + r - 1], recv_sem=recv_sems.at[a * 3 + r - 1],
                device_id=(*_chip_of(x, y, r), c), device_id_type=MESH) for a in range(n_arr) for r in (1, 2, 3)]

        def start():
            for cp in copies():
                cp.start()

        def finish():
            for cp in copies():
                cp.wait_recv()
                cp.wait_send()

        return start, (lambda: None), finish

    return _Comm(sums, [jax.ShapeDtypeStruct((3,) + s.shape[1:], s.dtype) for s in sums],
                 [pltpu.SemaphoreType.DMA((3 * n_arr,)), pltpu.SemaphoreType.DMA((3 * n_arr,))], bind)


def run_comm(comm, name):
    n_in, n_out = len(comm.inputs), len(comm.out_shape)

    def body(*refs):
        start, relay, finish = comm.bind(refs[:n_in], refs[n_in:n_in + n_out], refs[n_in + n_out:])
        start()
        relay()
        finish()

    outs = pl.pallas_call(body, name=name, out_shape=comm.out_shape, in_specs=[ANY] * n_in,
                          out_specs=[ANY] * n_out, scratch_shapes=comm.scratch)(*comm.inputs)
    return list(outs)


class _Hosted:
    def __init__(self, comms):
        self.comms = list(comms)
        self.inputs = [a for cm in self.comms for a in cm.inputs]
        self.out_shape = [s for cm in self.comms for s in cm.out_shape]
        self.scratch = [s for cm in self.comms for s in cm.scratch]
        self.in_specs = [ANY] * len(self.inputs)
        self.out_specs = [ANY] * len(self.out_shape)

    def split(self, refs, n_in, n_out, n_scratch):
        ni, no = len(self.inputs), len(self.out_shape)
        ins, rest = refs[:n_in], refs[n_in:]
        c_ins, rest = rest[:ni], rest[ni:]
        outs, rest = rest[:n_out], rest[n_out:]
        c_outs, rest = rest[:no], rest[no:]
        scratch, c_sems = rest[:n_scratch], rest[n_scratch:]
        phases = []
        for cm in self.comms:
            a, b, s = len(cm.inputs), len(cm.out_shape), len(cm.scratch)
            phases.append(cm.bind(c_ins[:a], c_outs[:b], c_sems[:s]))
            c_ins, c_outs, c_sems = c_ins[a:], c_outs[b:], c_sems[s:]
        return ins, outs, scratch, phases


def _before_step(phases, step, n_steps):
    if not phases:
        return

    @pl.when(step == 0)
    def _():
        for start, _, _ in phases:
            start()

    @pl.when(step == n_steps // 2)
    def _():
        for _, relay, _ in phases:
            relay()


def _after_step(phases, step, n_steps):
    if not phases:
        return

    @pl.when(step == n_steps - 1)
    def _():
        for _, _, finish in phases:
            finish()


def pair_sum(part, got, owners, name):
    m, n = got.shape[1:]

    def body(own_ref, mine_ref, got_ref, out_ref):
        del own_ref
        out_ref[...] = (mine_ref[...].astype(F32) + got_ref[...].astype(F32)).astype(out_ref.dtype)

    return pl.pallas_call(
        body, name=name,
        grid_spec=pltpu.PrefetchScalarGridSpec(
            num_scalar_prefetch=1, grid=(4,),
            in_specs=[pl.BlockSpec((m, n), lambda r, own: (own[r], 0)),
                      pl.BlockSpec((None, m, n), lambda r, own: (r, 0, 0))],
            out_specs=pl.BlockSpec((None, m, n), lambda r, own: (r, 0, 0))),
        out_shape=jax.ShapeDtypeStruct((4, m, n), got.dtype),
        compiler_params=_cparams(("arbitrary",)),
    )(owners, part, got)


def in_proj(x, g_pre, wt, bias, comms=()):
    s, d = x.shape
    tm = _tile(s, (512, 256, 128))
    tn = 768
    ni, nj = s // tm, D_IN // tn
    hosted = _Hosted(comms)

    def body(*refs):
        (x_ref, g_ref, w_ref, b_ref), (proj_ref, h_ref), _, phases = hosted.split(refs, 4, 2, 0)
        step = pl.program_id(0) * nj + pl.program_id(1)
        _before_step(phases, step, ni * nj)

        @pl.when(pl.program_id(1) == 0)
        def _():
            xv = x_ref[...]
            r = lax.rsqrt(jnp.mean(xv * xv, axis=-1, keepdims=True) + EPS)
            h_ref[...] = (xv * r * g_ref[...]).astype(BF16)

        acc = _dot(h_ref[...], w_ref[...], 1, 1)
        proj_ref[...] = (acc + b_ref[...]).astype(BF16)
        _after_step(phases, step, ni * nj)

    return pl.pallas_call(
        body, name="in_proj", grid=(ni, nj),
        in_specs=[pl.BlockSpec((tm, d), lambda i, j: (i, 0)),
                  pl.BlockSpec((1, d), lambda i, j: (0, 0)),
                  pl.BlockSpec((tn, d), lambda i, j: (j, 0)),
                  pl.BlockSpec((1, tn), lambda i, j: (0, j))] + hosted.in_specs,
        out_specs=[pl.BlockSpec((tm, tn), lambda i, j: (i, j)),
                   pl.BlockSpec((tm, d), lambda i, j: (i, 0))] + hosted.out_specs,
        out_shape=[jax.ShapeDtypeStruct((s, D_IN), BF16), jax.ShapeDtypeStruct((s, d), BF16)] + hosted.out_shape,
        scratch_shapes=hosted.scratch,
        compiler_params=_cparams(("arbitrary", "arbitrary")),
    )(x, g_pre, wt, bias, *hosted.inputs)


def in_proj_gather(x, pos_col, freq, sign, g_pre, wt_shard, bias):
    s, d = x.shape
    tm = _tile(s, (512, 256, 128))
    nt = s // tm
    tc = _tile(s, (256, 128))
    nc = s // tc
    m = wt_shard.shape[0]
    half = D_IN // 2
    xi = lax.axis_index("x")
    order = jnp.stack([xi, 1 - xi]).astype(jnp.int32)

    def body(order_ref, x_hbm, pos_hbm, freq_ref, sign_ref, g_ref, b_ref, shard_ref,
             proj_ref, h_hbm, rope_hbm, wt_ref,
             w_vmem, h_vmem, xbuf, posbuf, ropebuf, send_sems, recv_sems, local_sems, in_sems, out_sems):
        del order_ref
        p, i = pl.program_id(0), pl.program_id(1)
        xx, yy, cc = _my_place()
        me, sibling = (xx, yy, cc), (xx, yy, 1 - cc)
        chips = [_chip_of(xx, yy, r) for r in (1, 2, 3)]

        def rows(px, py, pc):
            return wt_ref.at[pl.ds(pl.multiple_of(_index(px, py, pc) * m, 16), m), :]

        def copy(k, block, to, src=None):
            return pltpu.make_async_remote_copy(
                src_ref=rows(*block) if src is None else src, dst_ref=rows(*block),
                send_sem=send_sems.at[k], recv_sem=recv_sems.at[k], device_id=to, device_id_type=MESH)

        def mine():
            return pltpu.make_async_copy(shard_ref, rows(*me), local_sems.at[0])

        def to_sibling():
            return copy(0, me, sibling, src=shard_ref)

        def to_chip(j):
            return copy(1 + j, me, (*chips[j], cc), src=shard_ref)

        def relay(j):
            copy(1 + j, (*chips[j], cc), me).wait_recv()
            copy(4 + j, (*chips[j], cc), sibling).start()

        def relayed(j):
            copy(4 + j, (*chips[j], 1 - cc), me).wait_recv()

        def load_half(which, slot):
            rows_of_half = wt_ref.at[pl.ds(pl.multiple_of(which * half, 16), half), :]
            load = pltpu.make_async_copy(rows_of_half, w_vmem.at[slot], local_sems.at[1 + slot])
            load.start()
            load.wait()

        def piece(ref, c):
            return ref.at[pl.ds(c * tc, tc), :]

        def fetch(c):
            return (pltpu.make_async_copy(piece(x_hbm, c), xbuf.at[c % 2], in_sems.at[c % 2]),
                    pltpu.make_async_copy(piece(pos_hbm, c), posbuf.at[c % 2], in_sems.at[2 + c % 2]))

        def put(c):
            return (pltpu.make_async_copy(piece(h_vmem, c), piece(h_hbm, c), out_sems.at[c % 2]),
                    pltpu.make_async_copy(ropebuf.at[c % 2], piece(rope_hbm, c), out_sems.at[2 + c % 2]))

        def prologue():
            for cp in fetch(0):
                cp.start()
            for c in range(nc):
                if c + 1 < nc:
                    for cp in fetch(c + 1):
                        cp.start()
                for cp in fetch(c):
                    cp.wait()
                if c >= 2:
                    for cp in put(c - 2):
                        cp.wait()
                xv = xbuf[c % 2]
                r = lax.rsqrt(jnp.mean(xv * xv, axis=-1, keepdims=True) + EPS)
                h_vmem[c * tc:(c + 1) * tc, :] = (xv * r * g_ref[...]).astype(BF16)
                ang = posbuf[c % 2].astype(F32) * freq_ref[...]
                ropebuf[c % 2, :, :LANES] = jnp.cos(ang)
                ropebuf[c % 2, :, LANES:] = jnp.sin(ang) * sign_ref[...]
                for cp in put(c):
                    cp.start()
            for c in range(max(nc - 2, 0), nc):
                for cp in put(c):
                    cp.wait()

        @pl.when(jnp.logical_and(p == 0, i == 0))
        def _():
            mine().start()
            to_sibling().start()
            to_chip(1).start()
            to_chip(0).start()
            prologue()
            copy(0, sibling, me).wait_recv()
            relay(1)
            relayed(1)
            mine().wait()
            to_chip(1).wait_send()
            to_chip(0).wait_send()
            to_chip(2).start()
            load_half(xx, 0)

        @pl.when(jnp.logical_and(p == 1, i == 0))
        def _():
            relayed(0)
            relayed(2)
            load_half(1 - xx, 1)

        def project(slot):
            hb = h_vmem[pl.ds(pl.multiple_of(i * tm, tm), tm), :]
            proj_ref[...] = (_dot(hb, w_vmem[slot], 1, 1) + b_ref[...]).astype(BF16)

        @pl.when(p == 0)
        def _():
            project(0)

        @pl.when(p == 1)
        def _():
            project(1)

        @pl.when(jnp.logical_and(p == 0, i == 1))
        def _():
            relay(0)

        @pl.when(jnp.logical_and(p == 0, i == nt - 1))
        def _():
            relay(2)

        @pl.when(jnp.logical_and(p == 1, i == nt - 1))
        def _():
            to_sibling().wait_send()
            to_chip(2).wait_send()
            for j in range(3):
                copy(4 + j, (*chips[j], cc), sibling).wait_send()

    const = lambda p, i, o: (0, 0)
    return pl.pallas_call(
        body, name="in_proj_gather",
        grid_spec=pltpu.PrefetchScalarGridSpec(
            num_scalar_prefetch=1, grid=(2, nt),
            in_specs=[ANY, ANY,
                      pl.BlockSpec((1, LANES), const),
                      pl.BlockSpec((1, LANES), const),
                      pl.BlockSpec((1, d), const),
                      pl.BlockSpec((1, half), lambda p, i, o: (0, o[p])),
                      ANY],
            out_specs=[pl.BlockSpec((tm, half), lambda p, i, o: (i, o[p])), ANY, ANY, ANY],
            scratch_shapes=[pltpu.VMEM((2, half, d), BF16), pltpu.VMEM((s, d), BF16),
                            pltpu.VMEM((2, tc, d), F32), pltpu.VMEM((2, tc, 1), jnp.int32),
                            pltpu.VMEM((2, tc, 2 * LANES), F32),
                            pltpu.SemaphoreType.DMA((7,)), pltpu.SemaphoreType.DMA((7,)),
                            pltpu.SemaphoreType.DMA((3,)), pltpu.SemaphoreType.DMA((4,)),
                            pltpu.SemaphoreType.DMA((4,))]),
        out_shape=[jax.ShapeDtypeStruct((s, D_IN), BF16), jax.ShapeDtypeStruct((s, d), BF16),
                   jax.ShapeDtypeStruct((s, 2 * LANES), F32), jax.ShapeDtypeStruct((D_IN, d), BF16)],
        compiler_params=pltpu.CompilerParams(dimension_semantics=("arbitrary", "arbitrary"),
                                             vmem_limit_bytes=IN_PROJ_VMEM_LIMIT),
    )(order, x, pos_col, freq, sign, g_pre, bias, wt_shard)


def out_proj_loss(cat, w_out, x, target, g_post):
    s, d = x.shape
    tm = _tile(s, (256, 128))

    def body(cat_ref, w_ref, x_ref, t_ref, g_ref, dy_ref, dout_ref, dg_ref, loss_ref):
        @pl.when(pl.program_id(0) == 0)
        def _():
            dg_ref[...] = jnp.zeros_like(dg_ref)
            loss_ref[...] = jnp.zeros_like(loss_ref)

        g = g_ref[...]
        ys = [_dot(cat_ref[c0:c0 + CHUNK, :], w_ref[...], 1, 0) for c0 in range(0, tm, CHUNK)]
        for c0 in range(0, tm, CHUNK):
            rows = slice(c0, c0 + CHUNK)
            yv = ys[c0 // CHUNK]
            r = lax.rsqrt(jnp.mean(yv * yv, axis=-1, keepdims=True) + EPS)
            nrm = yv * r
            err = x_ref[rows, :] + nrm * g - t_ref[rows, :]
            loss_ref[...] += 0.5 * jnp.sum(jnp.sum(err * err, axis=-1, keepdims=True), axis=0, keepdims=True) / d
            dout = err * (1.0 / d)
            dout_ref[rows, :] = dout
            dg_ref[...] += jnp.sum(dout * nrm, axis=0, keepdims=True)
            dn = dout * g
            dy = r * (dn - nrm * jnp.mean(dn * nrm, axis=-1, keepdims=True))
            dy_ref[rows, :] = dy.astype(BF16)

    return pl.pallas_call(
        body, name="out_proj_loss", grid=(s // tm,),
        in_specs=[pl.BlockSpec((tm, d), lambda i: (i, 0)),
                  pl.BlockSpec((d, d), lambda i: (0, 0)),
                  pl.BlockSpec((tm, d), lambda i: (i, 0)),
                  pl.BlockSpec((tm, d), lambda i: (i, 0)),
                  pl.BlockSpec((1, d), lambda i: (0, 0))],
        out_specs=[pl.BlockSpec((tm, d), lambda i: (i, 0)),
                   pl.BlockSpec((tm, d), lambda i: (i, 0)),
                   pl.BlockSpec((1, d), lambda i: (0, 0)),
                   pl.BlockSpec((1, LANES), lambda i: (0, 0))],
        out_shape=[jax.ShapeDtypeStruct((s, d), BF16), jax.ShapeDtypeStruct((s, d), F32),
                   jax.ShapeDtypeStruct((1, d), F32), jax.ShapeDtypeStruct((1, LANES), F32)],
        compiler_params=_cparams(("arbitrary",)),
    )(cat, w_out, x, target, g_post)


def matmul_nt(a, b, name):
    m, k = a.shape
    n = b.shape[0]
    tm = _tile(m, (512, 256, 128))

    def body(a_ref, b_ref, o_ref):
        o_ref[...] = _dot(a_ref[...], b_ref[...], 1, 1).astype(o_ref.dtype)

    return pl.pallas_call(
        body, name=name, grid=(m // tm,),
        in_specs=[pl.BlockSpec((tm, k), lambda i: (i, 0)), pl.BlockSpec((n, k), lambda i: (0, 0))],
        out_specs=pl.BlockSpec((tm, n), lambda i: (i, 0)),
        out_shape=jax.ShapeDtypeStruct((m, n), BF16),
        compiler_params=_cparams(("arbitrary",)),
    )(a, b)


def matmul_tn(a, b, tm, name, comms=()):
    k, m = a.shape
    n = b.shape[1]
    steps = m // tm
    hosted = _Hosted(comms)

    kc = _tile(k, (1024, 128))
    pieces = k // kc

    def body(*refs):
        (a_ref, b_hbm), (o_ref, cs_ref), (b_ref, b_sems), phases = hosted.split(refs, 2, 2, 2)
        step = pl.program_id(0)
        _before_step(phases, step, steps)

        def b_load(j):
            return pltpu.make_async_copy(b_hbm.at[j * kc:(j + 1) * kc, :], b_ref.at[j * kc:(j + 1) * kc, :], b_sems.at[j])

        @pl.when(step == 0)
        def _():
            for j in range(pieces):
                b_load(j).start()
            acc = None
            for j in range(pieces):
                b_load(j).wait()
                part = _dot(a_ref[j * kc:(j + 1) * kc, :], b_ref[j * kc:(j + 1) * kc, :], 0, 0)
                acc = part if acc is None else acc + part
            o_ref[...] = acc.astype(o_ref.dtype)

        @pl.when(step > 0)
        def _():
            o_ref[...] = _dot(a_ref[...], b_ref[...], 0, 0).astype(o_ref.dtype)

        rows = _tile(k, (512, 128))
        cs = jnp.zeros((1, tm), F32)
        for r0 in range(0, k, rows):
            cs = cs + jnp.sum(a_ref[r0:r0 + rows, :].astype(F32), axis=0, keepdims=True)
        cs_ref[...] = cs
        _after_step(phases, step, steps)

    return pl.pallas_call(
        body, name=name, grid=(steps,),
        in_specs=[pl.BlockSpec((k, tm), lambda i: (0, i)), ANY] + hosted.in_specs,
        out_specs=[pl.BlockSpec((tm, n), lambda i: (i, 0)), pl.BlockSpec((1, tm), lambda i: (0, i))] + hosted.out_specs,
        out_shape=[jax.ShapeDtypeStruct((m, n), BF16), jax.ShapeDtypeStruct((1, m), F32)] + hosted.out_shape,
        scratch_shapes=[pltpu.VMEM((k, n), b.dtype), pltpu.SemaphoreType.DMA((pieces,))] + hosted.scratch,
        compiler_params=_cparams(("arbitrary",)),
    )(a, b, *hosted.inputs)


def in_proj_bwd(dproj, wt, x, g_pre, dout, comms=()):
    s, d = x.shape
    tm = _tile(s, (512, 256, 128))
    steps = s // tm
    nsub = tm // CHUNK
    kchunks = [(k0, 1024) for k0 in range(0, 5120, 1024)] + [(5120, 256)]
    ksplit = len(kchunks)
    hosted = _Hosted(comms)

    def body(*refs):
        ((*dp_refs, w_hbm, x_hbm, g_ref, dout_hbm), (gx_hbm, dg_ref),
         (w_ref, w_sems, xbuf, dbuf, gbuf, in_sems, out_sems), phases) = hosted.split(refs, 4 + ksplit, 2, 7)
        step = pl.program_id(0)
        _before_step(phases, step, steps)

        def rows_of(ref, c):
            return ref.at[pl.ds(pl.multiple_of(step * tm + c * CHUNK, CHUNK), CHUNK), :]

        def fetches(c):
            return (pltpu.make_async_copy(rows_of(x_hbm, c), xbuf.at[c], in_sems.at[c]),
                    pltpu.make_async_copy(rows_of(dout_hbm, c), dbuf.at[c], in_sems.at[nsub + c]))

        def put(c):
            return pltpu.make_async_copy(gbuf.at[c % 2], rows_of(gx_hbm, c), out_sems.at[c % 2])

        for c in range(nsub):
            for cp in fetches(c):
                cp.start()

        def w_load(j):
            k0, kw = kchunks[j]
            return pltpu.make_async_copy(w_hbm.at[k0:k0 + kw, :], w_ref.at[k0:k0 + kw, :], w_sems.at[j])

        @pl.when(step == 0)
        def _():
            dg_ref[...] = jnp.zeros_like(dg_ref)
            for j in range(ksplit):
                w_load(j).start()

        dh_all = None
        for j, ((k0, kw), dp_ref) in enumerate(zip(kchunks, dp_refs)):
            @pl.when(step == 0)
            def _():
                w_load(j).wait()

            part = _dot(dp_ref[...], w_ref[k0:k0 + kw, :], 1, 0)
            dh_all = part if dh_all is None else dh_all + part
        for c in range(nsub):
            for cp in fetches(c):
                cp.wait()
            if c >= 2:
                put(c - 2).wait()
            elif c < nsub:
                @pl.when(step > 0)
                def _():
                    put(max(nsub - 2, 0) + c).wait()
            dh = dh_all[c * CHUNK:(c + 1) * CHUNK, :]
            xv = xbuf[c]
            r = lax.rsqrt(jnp.mean(xv * xv, axis=-1, keepdims=True) + EPS)
            xn = xv * r
            dg_ref[...] += jnp.sum(dh * xn, axis=0, keepdims=True)
            dn = dh * g_ref[...]
            gbuf[c % 2] = dbuf[c] + r * (dn - xn * jnp.mean(dn * xn, axis=-1, keepdims=True))
            put(c).start()
        @pl.when(step == steps - 1)
        def _():
            for c in range(max(nsub - 2, 0), nsub):
                put(c).wait()

        _after_step(phases, step, steps)

    side_in, side_out = pltpu.VMEM((nsub, CHUNK, d), F32), pltpu.VMEM((2, CHUNK, d), F32)
    row = pl.BlockSpec((1, d), lambda i: (0, 0))
    return pl.pallas_call(
        body, name="in_proj_bwd", grid=(steps,),
        in_specs=[pl.BlockSpec((tm, kw), functools.partial(lambda j, i: (i, j), k0 // kw)) for k0, kw in kchunks]
        + [ANY, ANY, row, ANY] + hosted.in_specs,
        out_specs=[ANY, row] + hosted.out_specs,
        out_shape=[jax.ShapeDtypeStruct((s, d), F32), jax.ShapeDtypeStruct((1, d), F32)] + hosted.out_shape,
        scratch_shapes=[pltpu.VMEM((D_IN, d), BF16), pltpu.SemaphoreType.DMA((ksplit,)), side_in, side_in, side_out,
                        pltpu.SemaphoreType.DMA((2 * nsub,)), pltpu.SemaphoreType.DMA((2,))] + hosted.scratch,
        compiler_params=_cparams(("arbitrary",)),
    )(*([dproj] * ksplit), wt, x, g_pre, dout, *hosted.inputs)


def _lane_iota(shape):
    return lax.broadcasted_iota(jnp.int32, shape, len(shape) - 1)


def rope_tables(pos_col, freq, sign, comms=()):
    s = pos_col.shape[0]
    tr = _tile(s, (512, 256, 128))
    hosted = _Hosted(comms)

    def body(*refs):
        (pos_ref, freq_ref, sign_ref), (out_ref,), _, phases = hosted.split(refs, 3, 1, 0)
        _before_step(phases, pl.program_id(0), s // tr)
        ang = pos_ref[...].astype(F32) * freq_ref[...]
        out_ref[:, :LANES] = jnp.cos(ang)
        out_ref[:, LANES:] = jnp.sin(ang) * sign_ref[...]
        _after_step(phases, pl.program_id(0), s // tr)

    return pl.pallas_call(
        body, name="rope_tables", grid=(s // tr,),
        in_specs=[pl.BlockSpec((tr, 1), lambda i: (i, 0)), pl.BlockSpec((1, LANES), lambda i: (0, 0)),
                  pl.BlockSpec((1, LANES), lambda i: (0, 0))] + hosted.in_specs,
        out_specs=[pl.BlockSpec((tr, 2 * LANES), lambda i: (i, 0))] + hosted.out_specs,
        out_shape=[jax.ShapeDtypeStruct((s, 2 * LANES), F32)] + hosted.out_shape,
        scratch_shapes=hosted.scratch,
        compiler_params=_cparams(("arbitrary",)),
    )(pos_col, freq, sign, *hosted.inputs)


def _partner(v):
    low = (_lane_iota(v.shape) % HEAD_DIM) < (HEAD_DIM // 2)
    return jnp.where(low, pltpu.roll(v, LANES - HEAD_DIM // 2, 1), pltpu.roll(v, HEAD_DIM // 2, 1))


def _rope(v, cos, sin_signed):
    return v * cos + _partner(v) * sin_signed


def _rope_transposed(dv, cos, sin_signed):
    return dv * cos - _partner(dv) * sin_signed


def _both_halves(v, kv_head):
    keep = (_lane_iota(v.shape) >= HEAD_DIM) if kv_head else (_lane_iota(v.shape) < HEAD_DIM)
    return jnp.where(keep, v, pltpu.roll(v, HEAD_DIM, 1))


def _fold_halves(acc):
    return acc + pltpu.roll(acc, HEAD_DIM, 1)


def _by_half(a, b):
    shape = jnp.broadcast_shapes(jnp.shape(a), jnp.shape(b))
    return jnp.where(_lane_iota(shape) < HEAD_DIM, a, b)


def _stack_heads(pair):
    return jnp.concatenate([_by_half(pair, 0.0), _by_half(0.0, pair)], axis=0)


def _band_bias(has_prev):
    i = lax.broadcasted_iota(jnp.int32, (2 * CHUNK, 2 * CHUNK), 0) % CHUNK
    j = lax.broadcasted_iota(jnp.int32, (2 * CHUNK, 2 * CHUNK), 1)
    band = jnp.logical_and(j > i, j <= i + CHUNK)
    return jnp.where(jnp.logical_and(band, jnp.logical_or(j >= CHUNK, has_prev)), 0.0, NEG)


def _probs(qm2, kk2, bias, sink_col):
    sc = _dot(qm2, kk2, 1, 1) + bias
    mx = jnp.maximum(jnp.max(sc, axis=-1, keepdims=True), sink_col)
    p = jnp.exp(sc - mx)
    es = jnp.exp(sink_col - mx)
    inv = 1.0 / (jnp.sum(p, axis=-1, keepdims=True) + es)
    return p * inv, es * inv


def _probs_staged(qm2s, kk2s, bias, sink_cols):
    k = range(len(qm2s))
    scs = [_dot(qm2s[i], kk2s[i], 1, 1) + bias for i in k]
    mxs = [jnp.maximum(jnp.max(scs[i], axis=-1, keepdims=True), sink_cols[i]) for i in k]
    ps = [jnp.exp(scs[i] - mxs[i]) for i in k]
    ess = [jnp.exp(sink_cols[i] - mxs[i]) for i in k]
    invs = [1.0 / (jnp.sum(ps[i], axis=-1, keepdims=True) + ess[i]) for i in k]
    return [ps[i] * invs[i] for i in k], [ess[i] * invs[i] for i in k]


def _sink_col(sinks_ref, pair):
    row = lax.broadcasted_iota(jnp.int32, (2 * CHUNK, 1), 0)
    return jnp.where(row < CHUNK, sinks_ref[2 * pair], sinks_ref[2 * pair + 1])


def _layer_norm_parts(v):
    mu = jnp.mean(v, axis=-1, keepdims=True)
    xc = v - mu
    rstd = lax.rsqrt(jnp.mean(xc * xc, axis=-1, keepdims=True) + EPS)
    return xc * rstd, rstd


def _masked_spatial(w_ref, g):
    t = lax.broadcasted_iota(jnp.int32, (CHUNK, CHUNK), 0)
    sidx = lax.broadcasted_iota(jnp.int32, (CHUNK, CHUNK), 1)
    return jnp.where(t >= sidx, w_ref[g], 0.0).astype(BF16)


def _keys_values(kv_ref, kvp_ref, rope_ref, ropep_ref):
    cos_c, sin_c = rope_ref[:, :LANES], rope_ref[:, LANES:]
    cos_p, sin_p = ropep_ref[:, :LANES], ropep_ref[:, LANES:]
    k_c = _rope(kv_ref[:, :D_KV].astype(F32), cos_c, sin_c)
    k_p = _rope(kvp_ref[:, :D_KV].astype(F32), cos_p, sin_p)
    keys = jnp.concatenate([k_p, k_c], axis=0)
    vals = jnp.concatenate([kvp_ref[:, D_KV:], kv_ref[:, D_KV:]], axis=0).astype(F32)
    return keys, vals, (cos_c, sin_c, cos_p, sin_p)


def mixer_fwd(proj, rope, ln_g, ln_b, w_sp, b_sp_rows, sinks, comms=()):
    s = proj.shape[0]
    nb = s // CHUNK
    hosted = _Hosted(comms)

    def body(sinks_ref, *refs):
        ((proj_ref, kvp_ref, rope_ref, ropep_ref, lng_ref, lnb_ref, w_ref, b_ref), (cat_ref, p_ref), _,
         phases) = hosted.split(refs, 8, 2, 0)
        n = pl.program_id(0)
        _before_step(phases, n, nb)
        xhat, _ = _layer_norm_parts(proj_ref[:, OFF_V:OFF_V + D_GMLP].astype(F32))
        vnb = (xhat * lng_ref[...] + lnb_ref[...]).astype(BF16)
        mixeds = [_dot(_masked_spatial(w_ref, g), vnb[:, g * CHUNK:(g + 1) * CHUNK], 1, 0) + b_ref[g]
                  for g in range(GROUPS)]
        for g in range(GROUPS):
            za = proj_ref[:, OFF_ZA + g * CHUNK:OFF_ZA + (g + 1) * CHUNK].astype(F32)
            u = proj_ref[:, OFF_U + g * CHUNK:OFF_U + (g + 1) * CHUNK].astype(F32)
            cat_ref[:, g * CHUNK:(g + 1) * CHUNK] = (u * mixeds[g] * (za * _sigmoid(za))).astype(BF16)
        kv_ref = proj_ref.at[:, OFF_K:OFF_K + 2 * D_KV]
        keys, vals, (cos_c, sin_c, _, _) = _keys_values(kv_ref, kvp_ref, rope_ref, ropep_ref)
        cos_q, sin_q = cos_c * SCALE, sin_c * SCALE
        bias = _band_bias(n > 0)
        kk2 = [_both_halves(keys, kvh).astype(BF16) for kvh in range(N_KV_HEADS)]
        vv2 = [_both_halves(vals, kvh).astype(BF16) for kvh in range(N_KV_HEADS)]
        pairs = range(N_PAIRS)
        qms = [_stack_heads(_rope(proj_ref[:, OFF_Q + pair * LANES:OFF_Q + (pair + 1) * LANES].astype(F32),
                                  cos_q, sin_q)).astype(BF16) for pair in pairs]
        probs, sink_probs = _probs_staged(qms, [kk2[pair // PAIRS_PER_KV] for pair in pairs], bias,
                                          [_sink_col(sinks_ref, pair) for pair in pairs])
        pbs = [p.astype(BF16) for p in probs]
        outs = [_dot(pbs[pair], vv2[pair // PAIRS_PER_KV], 1, 0) for pair in pairs]
        first_col = _lane_iota((2 * CHUNK, 2 * CHUNK)) == 0
        for pair in pairs:
            p_ref[0, pair] = jnp.where(first_col, sink_probs[pair].astype(BF16), pbs[pair])
        for pair in pairs:
            out_pair = _by_half(outs[pair][:CHUNK], outs[pair][CHUNK:])
            zb = proj_ref[:, OFF_ZB + pair * LANES:OFF_ZB + (pair + 1) * LANES].astype(F32)
            cat_ref[:, D_GMLP + pair * LANES:D_GMLP + (pair + 1) * LANES] = (
                out_pair * (zb * _sigmoid(zb))).astype(BF16)
        _after_step(phases, n, nb)

    prev = lambda n, *_: (jnp.maximum(n - 1, 0), 0)
    kv_block = OFF_K // (2 * D_KV)
    return pl.pallas_call(
        body, name="mixer_fwd",
        grid_spec=pltpu.PrefetchScalarGridSpec(
            num_scalar_prefetch=1, grid=(nb,),
            in_specs=[pl.BlockSpec((CHUNK, D_IN), lambda n, *_: (n, 0)),
                      pl.BlockSpec((CHUNK, 2 * D_KV), lambda n, *_: (jnp.maximum(n - 1, 0), kv_block)),
                      pl.BlockSpec((CHUNK, 2 * LANES), lambda n, *_: (n, 0)),
                      pl.BlockSpec((CHUNK, 2 * LANES), prev),
                      pl.BlockSpec((1, D_GMLP), lambda n, *_: (0, 0)),
                      pl.BlockSpec((1, D_GMLP), lambda n, *_: (0, 0)),
                      pl.BlockSpec((GROUPS, CHUNK, CHUNK), lambda n, *_: (0, 0, 0)),
                      pl.BlockSpec((GROUPS, CHUNK, CHUNK), lambda n, *_: (0, 0, 0))] + hosted.in_specs,
            out_specs=[pl.BlockSpec((CHUNK, D_GMLP + D_ATTN), lambda n, *_: (n, 0)),
                       pl.BlockSpec((1, N_PAIRS, 2 * CHUNK, 2 * CHUNK), lambda n, *_: (n, 0, 0, 0))]
            + hosted.out_specs,
            scratch_shapes=hosted.scratch),
        out_shape=[jax.ShapeDtypeStruct((s, D_GMLP + D_ATTN), BF16),
                   jax.ShapeDtypeStruct((nb, N_PAIRS, 2 * CHUNK, 2 * CHUNK), BF16)] + hosted.out_shape,
        compiler_params=_cparams(("arbitrary",)),
    )(sinks, proj, proj, rope, rope, ln_g, ln_b, w_sp, b_sp_rows, *hosted.inputs)


def mixer_bwd(proj, dcat, probs, rope, ln_g, ln_b, w_sp, b_sp_rows, comms=()):
    s = proj.shape[0]
    nb = s // CHUNK
    hosted = _Hosted(comms)

    def body(*refs):
        ((proj_ref, kvp_ref, dcat_ref, p_ref, rope_ref, ropep_ref, lng_ref, lnb_ref, w_ref, b_ref),
         (dproj_ref, dw_ref, db_ref, dlng_ref, dlnb_ref, dsink_ref),
         (pend_ref, pend_kv_ref, dbacc_ref), phases) = hosted.split(refs, 10, 6, 3)
        n = pl.program_id(0)
        _before_step(phases, n, nb + 1)

        @pl.when(n == 0)
        def _():
            dw_ref[...] = jnp.zeros_like(dw_ref)
            dbacc_ref[...] = jnp.zeros_like(dbacc_ref)
            dlng_ref[...] = jnp.zeros_like(dlng_ref)
            dlnb_ref[...] = jnp.zeros_like(dlnb_ref)
            dsink_ref[...] = jnp.zeros_like(dsink_ref)

        @pl.when(n > 0)
        def _():
            dproj_ref[...] = pend_ref[...]

        def flush(dkv_prev):
            @pl.when(n > 0)
            def _():
                dproj_ref[:, OFF_K:OFF_K + 2 * D_KV] = (pend_kv_ref[...] + dkv_prev).astype(BF16)

        @pl.when(n < nb)
        def _():
            kv_ref = proj_ref.at[:, OFF_K:OFF_K + 2 * D_KV]
            keys, vals, (cos_c, sin_c, cos_p, sin_p) = _keys_values(kv_ref, kvp_ref, rope_ref, ropep_ref)
            cos_q, sin_q = cos_c * SCALE, sin_c * SCALE
            first_col = _lane_iota((2 * CHUNK, 2 * CHUNK)) == 0
            lane_row = _lane_iota((1, LANES))
            dsink = jnp.zeros((1, LANES), F32)
            dk_heads, dv_heads = [], []
            for kvh in range(N_KV_HEADS):
                kk2 = _both_halves(keys, kvh).astype(BF16)
                vv2 = _both_halves(vals, kvh).astype(BF16)
                pairs = list(range(kvh * PAIRS_PER_KV, (kvh + 1) * PAIRS_PER_KV))
                k4 = range(PAIRS_PER_KV)
                qm2s = [_stack_heads(_rope(proj_ref[:, OFF_Q + pair * LANES:OFF_Q + (pair + 1) * LANES].astype(F32),
                                           cos_q, sin_q)).astype(BF16) for pair in pairs]
                kept = [p_ref[0, pair] for pair in pairs]
                pbs = [jnp.where(first_col, jnp.zeros_like(kp), kp) for kp in kept]
                ps = [pb.astype(F32) for pb in pbs]
                p_sinks = [kp[:, 0:1].astype(F32) for kp in kept]
                o2s = [_dot(pb, vv2, 1, 0) for pb in pbs]
                zbs = [proj_ref[:, OFF_ZB + pair * LANES:OFF_ZB + (pair + 1) * LANES].astype(F32) for pair in pairs]
                sgs = [_sigmoid(zb) for zb in zbs]
                dybs = [dcat_ref[:, D_GMLP + pair * LANES:D_GMLP + (pair + 1) * LANES].astype(F32) for pair in pairs]
                for i, pair in enumerate(pairs):
                    out_pair = _by_half(o2s[i][:CHUNK], o2s[i][CHUNK:])
                    pend_ref[:, OFF_ZB + pair * LANES:OFF_ZB + (pair + 1) * LANES] = (
                        dybs[i] * out_pair * (sgs[i] * (1.0 + zbs[i] * (1.0 - sgs[i])))).astype(BF16)
                dom2s = [_stack_heads(dybs[i] * (zbs[i] * sgs[i])).astype(BF16) for i in k4]
                dps = [_dot(dom2, vv2, 1, 1) for dom2 in dom2s]
                deltas = [jnp.sum(ps[i] * dps[i], axis=-1, keepdims=True) for i in k4]
                dss = [ps[i] * (dps[i] - deltas[i]) for i in k4]
                for i, pair in enumerate(pairs):
                    dsk = -(p_sinks[i] * deltas[i])
                    dsink = dsink + jnp.where(lane_row == 2 * pair,
                                              jnp.sum(dsk[:CHUNK], axis=0, keepdims=True), 0.0)
                    dsink = dsink + jnp.where(lane_row == 2 * pair + 1,
                                              jnp.sum(dsk[CHUNK:], axis=0, keepdims=True), 0.0)
                dsbs = [ds.astype(BF16) for ds in dss]
                dq2s = [_dot(dsb, kk2, 1, 0) for dsb in dsbs]
                for pair, dq2 in zip(pairs, dq2s):
                    pend_ref[:, OFF_Q + pair * LANES:OFF_Q + (pair + 1) * LANES] = _rope_transposed(
                        _by_half(dq2[:CHUNK], dq2[CHUNK:]), cos_q, sin_q).astype(BF16)
                dkks = [_dot(dsbs[i], qm2s[i], 0, 0) for i in k4]
                dvvs = [_dot(pbs[i], dom2s[i], 0, 0) for i in k4]
                dk_heads.append(_fold_halves((dkks[0] + dkks[1]) + (dkks[2] + dkks[3])))
                dv_heads.append(_fold_halves((dvvs[0] + dvvs[1]) + (dvvs[2] + dvvs[3])))
            dk_rot = _by_half(dk_heads[0], dk_heads[1])
            dv_all = _by_half(dv_heads[0], dv_heads[1])
            dk_p = _rope_transposed(dk_rot[:CHUNK], cos_p, sin_p)
            dk_c = _rope_transposed(dk_rot[CHUNK:], cos_c, sin_c)
            flush(jnp.concatenate([dk_p, dv_all[:CHUNK]], axis=1))
            dsink_ref[...] += dsink
            pend_kv_ref[...] = jnp.concatenate([dk_c, dv_all[CHUNK:]], axis=1)
            xhat, rstd = _layer_norm_parts(proj_ref[:, OFF_V:OFF_V + D_GMLP].astype(F32))
            lng = lng_ref[...]
            vnb = (xhat * lng + lnb_ref[...]).astype(BF16)
            dvn_cols = []
            for g in range(GROUPS):
                cols = slice(g * CHUNK, (g + 1) * CHUNK)
                wm = _masked_spatial(w_ref, g)
                mixed = _dot(wm, vnb[:, cols], 1, 0) + b_ref[g]
                za = proj_ref[:, OFF_ZA + g * CHUNK:OFF_ZA + (g + 1) * CHUNK].astype(F32)
                u = proj_ref[:, OFF_U + g * CHUNK:OFF_U + (g + 1) * CHUNK].astype(F32)
                dya = dcat_ref[:, cols].astype(F32)
                sg = _sigmoid(za)
                sz = za * sg
                pend_ref[:, OFF_U + g * CHUNK:OFF_U + (g + 1) * CHUNK] = (dya * mixed * sz).astype(BF16)
                pend_ref[:, OFF_ZA + g * CHUNK:OFF_ZA + (g + 1) * CHUNK] = (
                    dya * u * mixed * (sg * (1.0 + za * (1.0 - sg)))).astype(BF16)
                dmixed = dya * u * sz
                dmb = dmixed.astype(BF16)
                dbacc_ref[g] += dmixed
                dw_ref[g] += _dot(dmb, vnb[:, cols], 1, 1)
                dvn_cols.append(_dot(wm, dmb, 0, 0))
            dvn = jnp.concatenate(dvn_cols, axis=1)
            dlng_ref[...] += jnp.sum(dvn * xhat, axis=0, keepdims=True)
            dlnb_ref[...] += jnp.sum(dvn, axis=0, keepdims=True)
            dxh = dvn * lng
            dv = rstd * (dxh - jnp.mean(dxh, axis=-1, keepdims=True)
                         - xhat * jnp.mean(dxh * xhat, axis=-1, keepdims=True))
            pend_ref[:, OFF_V:OFF_V + D_GMLP] = dv.astype(BF16)

        @pl.when(n == nb)
        def _():
            flush(jnp.zeros((CHUNK, 2 * D_KV), F32))
            t = lax.broadcasted_iota(jnp.int32, (CHUNK, CHUNK), 0)
            sidx = lax.broadcasted_iota(jnp.int32, (CHUNK, CHUNK), 1)
            lane = _lane_iota((CHUNK, LANES))
            dbt = jnp.zeros((CHUNK, LANES), F32)
            for g in range(GROUPS):
                dw_ref[g] = jnp.where(t >= sidx, dw_ref[g], 0.0)
                dbt = jnp.where(lane == g, jnp.sum(dbacc_ref[g], axis=-1, keepdims=True), dbt)
            db_ref[...] = jnp.transpose(dbt)[:GROUPS, :]

        _after_step(phases, n, nb + 1)

    cur = lambda n: (jnp.minimum(n, nb - 1), 0)
    prev = lambda n: (jnp.clip(n - 1, 0, nb - 1), 0)
    kv_block = OFF_K // (2 * D_KV)
    const2 = lambda n: (0, 0)
    const3 = lambda n: (0, 0, 0)
    return pl.pallas_call(
        body, name="mixer_bwd", grid=(nb + 1,),
        in_specs=[pl.BlockSpec((CHUNK, D_IN), cur),
                  pl.BlockSpec((CHUNK, 2 * D_KV), lambda n: (jnp.clip(n - 1, 0, nb - 1), kv_block)),
                  pl.BlockSpec((CHUNK, D_GMLP + D_ATTN), cur),
                  pl.BlockSpec((1, N_PAIRS, 2 * CHUNK, 2 * CHUNK), lambda n: (jnp.minimum(n, nb - 1), 0, 0, 0)),
                  pl.BlockSpec((CHUNK, 2 * LANES), cur),
                  pl.BlockSpec((CHUNK, 2 * LANES), prev),
                  pl.BlockSpec((1, D_GMLP), const2),
                  pl.BlockSpec((1, D_GMLP), const2),
                  pl.BlockSpec((GROUPS, CHUNK, CHUNK), const3),
                  pl.BlockSpec((GROUPS, CHUNK, CHUNK), const3)] + hosted.in_specs,
        out_specs=[pl.BlockSpec((CHUNK, D_IN), lambda n: (jnp.maximum(n - 1, 0), 0)),
                   pl.BlockSpec((GROUPS, CHUNK, CHUNK), const3),
                   pl.BlockSpec((GROUPS, CHUNK), const2),
                   pl.BlockSpec((1, D_GMLP), const2),
                   pl.BlockSpec((1, D_GMLP), const2),
                   pl.BlockSpec((1, LANES), const2)] + hosted.out_specs,
        scratch_shapes=[pltpu.VMEM((CHUNK, D_IN), BF16), pltpu.VMEM((CHUNK, 2 * D_KV), F32),
                        pltpu.VMEM((GROUPS, CHUNK, CHUNK), F32)] + hosted.scratch,
        out_shape=[jax.ShapeDtypeStruct((s, D_IN), BF16),
                   jax.ShapeDtypeStruct((GROUPS, CHUNK, CHUNK), F32),
                   jax.ShapeDtypeStruct((GROUPS, CHUNK), F32),
                   jax.ShapeDtypeStruct((1, D_GMLP), F32),
                   jax.ShapeDtypeStruct((1, D_GMLP), F32),
                   jax.ShapeDtypeStruct((1, LANES), F32)] + hosted.out_shape,
        compiler_params=_cparams(("arbitrary",)),
    )(proj, proj, dcat, probs, rope, rope, ln_g, ln_b, w_sp, b_sp_rows, *hosted.inputs)


def _adamw_math(w, g, m, v):
    m = ADAM_B1 * m + (1.0 - ADAM_B1) * g
    v = ADAM_B2 * v + (1.0 - ADAM_B2) * (g * g)
    m_hat = m / (1.0 - ADAM_B1 ** ADAM_STEP)
    v_hat = v / (1.0 - ADAM_B2 ** ADAM_STEP)
    delta = -ADAM_LR * (m_hat / (jnp.sqrt(v_hat) + ADAM_EPS) + ADAM_WD * w)
    return delta, m, v


def adamw_shard(terms, w, m, v, name):
    r, c = w.shape
    tr = _tile(r, (224, 256, 128, 8))
    n_terms = len(terms)

    def body(*refs):
        w_ref, m_ref, v_ref, g_ref, d_ref, nm_ref, nv_ref = refs[n_terms:]
        g = None
        for ref, (_, slots) in zip(refs[:n_terms], terms):
            for k in range(slots):
                part = ref[k].astype(F32)
                g = part if g is None else g + part
        g_ref[...] = g
        d_ref[...], nm_ref[...], nv_ref[...] = _adamw_math(w_ref[...], g, m_ref[...], v_ref[...])

    spec = pl.BlockSpec((tr, c), lambda i: (i, 0))
    return pl.pallas_call(
        body, name=name, grid=(r // tr,),
        in_specs=[pl.BlockSpec((slots, tr, c), lambda i: (0, i, 0)) for _, slots in terms] + [spec] * 3,
        out_specs=[spec] * 4, out_shape=[jax.ShapeDtypeStruct((r, c), F32)] * 4,
        compiler_params=_cparams(("arbitrary",)),
    )(*[a for a, _ in terms], w, m, v)


def adamw_small(gathered, lane_windows, params):
    n_par = len(params)

    def body(*refs):
        g_refs = refs[:n_par + 1]
        wmv_refs = refs[n_par + 1:4 * n_par + 1]
        out_refs = refs[4 * n_par + 1:]

        def total(ref):
            acc = ref[0]
            for dev in range(1, N_DEV):
                acc = acc + ref[dev]
            return acc

        for i in range(n_par):
            w_ref, m_ref, v_ref = wmv_refs[3 * i:3 * i + 3]
            g = total(g_refs[i])
            if lane_windows[i] is not None:
                start, size = lane_windows[i]
                g = g[..., start:start + size]
            delta, new_m, new_v = _adamw_math(w_ref[...], g, m_ref[...], v_ref[...])
            for ref, val in zip(out_refs[4 * i:4 * i + 4], (g, delta, new_m, new_v)):
                ref[...] = val
        out_refs[4 * n_par][...] = total(g_refs[n_par])

    flat = [a for wmv in params for a in wmv]
    out_shape = [jax.ShapeDtypeStruct(w.shape, F32) for (w, _, _) in params for _ in range(4)]
    out_shape.append(jax.ShapeDtypeStruct(gathered[-1].shape[1:], F32))
    outs = pl.pallas_call(body, name="adamw_small", out_shape=out_shape, compiler_params=_cparams())(*gathered, *flat)
    return [tuple(outs[4 * i:4 * i + 4]) for i in range(n_par)], outs[-1]


def kernel(x, positions, g_pre, w_in, b_qkv, ln_v_g, ln_v_b, w_spatial, b_spatial, attn_sinks, w_out, g_post, loss_target, m_g_pre, m_w_in, m_b_qkv, m_ln_v_g, m_ln_v_b, m_w_spatial, m_b_spatial, m_attn_sinks, m_w_out, m_g_post, v_g_pre, v_w_in, v_b_qkv, v_ln_v_g, v_ln_v_b, v_w_spatial, v_b_spatial, v_attn_sinks, v_w_out, v_g_post):
    x2, target = x[0], loss_target[0]
    seq = x2.shape[0]
    xi, yi, ci = _my_place()

    wt_shard = w_in[0].T.astype(BF16)
    wo_shard = w_out[0].astype(BF16)
    pos_col = positions.reshape(seq, 1)
    half = HEAD_DIM // 2
    inv_freq = ROPE_THETA ** (-jnp.arange(half, dtype=F32) * (2.0 / HEAD_DIM))
    freq = jnp.tile(inv_freq, LANES // half).reshape(1, LANES)
    sign = jnp.tile(jnp.concatenate([-jnp.ones((half,), F32), jnp.ones((half,), F32)]), LANES // HEAD_DIM)
    sign = sign.reshape(1, LANES)
    bias = jnp.concatenate([jnp.zeros((1, OFF_Q), F32), b_qkv, jnp.zeros((1, D_ATTN), F32)], axis=1)
    proj, h, rope, wt = in_proj_gather(x2, pos_col, freq, sign, g_pre, wt_shard, bias)

    b_rows = jnp.broadcast_to(b_spatial[0][:, :, None], (GROUPS, CHUNK, CHUNK))
    sinks = attn_sinks[0]
    cat, probs, wo = mixer_fwd(proj, rope, ln_v_g, ln_v_b, w_spatial[0], b_rows, sinks,
                               comms=[gather_comm([wo_shard])])
    dy, dout, d_g_post, loss_part = out_proj_loss(cat, wo, x2, target, g_post)

    dcat = matmul_nt(dy, wo, "out_proj_bwd")
    d_wo, _ = matmul_tn(cat, dy, 512, "w_out_grad")
    dproj, d_w_sp, d_b_sp, d_ln_g, d_ln_b, d_sinks, parts_wo = mixer_bwd(
        proj, dcat, probs, rope, ln_v_g, ln_v_b, w_spatial[0], b_rows, comms=[scatter_comm([d_wo])])
    small_parts = [d_ln_g, d_ln_b, d_w_sp, d_b_sp, d_sinks, d_g_post, loss_part]
    d_wt, colsum, *landed = matmul_tn(dproj, h, 768, "w_in_grad", comms=[gather_comm(small_parts, stack=True)])

    owners = jnp.stack([4 * cx + 2 * cy + ci for cx, cy in (_chip_of(xi, yi, r) for r in range(4))]).astype(jnp.int32)
    (got_wt,) = run_comm(pair_comm([d_wt]), "grad_exchange_pair")
    sum_wt = pair_sum(d_wt, got_wt, owners, "grad_pair_sum_w_in")
    grad_x, d_g_pre, far_wt = in_proj_bwd(dproj, wt, x2, g_pre, dout, comms=[chips_comm([sum_wt])])
    late = run_comm(gather_comm([d_g_pre, colsum], stack=True, direct=True), "allgather_late_grads")
    gathered = late + landed
    windows = [None, (OFF_Q, D_QKV), None, None, None, None, (0, N_Q_HEADS), None]
    small = [(g_pre, m_g_pre, v_g_pre), (b_qkv, m_b_qkv, v_b_qkv), (ln_v_g, m_ln_v_g, v_ln_v_g),
             (ln_v_b, m_ln_v_b, v_ln_v_b), (w_spatial[0], m_w_spatial[0], v_w_spatial[0]),
             (b_spatial[0], m_b_spatial[0], v_b_spatial[0]), (attn_sinks, m_attn_sinks, v_attn_sinks),
             (g_post, m_g_post, v_g_post)]
    small_out, loss_row = adamw_small(gathered, windows, small)
    lead = [False, False, False, False, True, True, False, False]
    small_out = [tuple(a[None] if ld else a for a in leaf) for leaf, ld in zip(small_out, lead)]

    wt_out = adamw_shard([(sum_wt, 1), (far_wt, 3)], w_in[0].T, m_w_in[0].T, v_w_in[0].T, "adamw_w_in")
    wo_out = adamw_shard([(parts_wo, N_DEV)], w_out[0], m_w_out[0], v_w_out[0], "adamw_w_out")

    def leaves(k):
        gp, bq, lg, lb, ws, bs, sk, gpo = (leaf[k] for leaf in small_out)
        return [gp, wt_out[k].T[None], bq, lg, lb, ws, bs, sk, wo_out[k][None], gpo]

    return (loss_row[0, 0], grad_x[None], *leaves(0), *leaves(1), *leaves(2), *leaves(3))
```

```python
import functools

import jax
import jax.numpy as jnp
from jax import lax
from jax.experimental import pallas as pl
from jax.experimental.pallas import tpu as pltpu

F32 = jnp.float32
BF16 = jnp.bfloat16

D_MODEL = 2048
D_GMLP = 1024
D_ATTN = 1024
CHUNK = 128
GROUPS = 8
HEAD_DIM = 64
N_Q_HEADS = 16
N_KV_HEADS = 2
D_KV = N_KV_HEADS * HEAD_DIM
D_IN = 3 * D_GMLP + D_ATTN + 2 * D_KV + D_ATTN
OFF_U, OFF_V, OFF_ZA = 0, D_GMLP, 2 * D_GMLP
OFF_Q = 3 * D_GMLP
OFF_K = OFF_Q + D_ATTN
OFF_VA = OFF_K + D_KV
OFF_ZB = OFF_VA + D_KV
D_QKV = D_ATTN + 2 * D_KV
ROPE_THETA = 10000.0
EPS = 1e-6
SCALE = HEAD_DIM ** -0.5
NEG = -1e30
N_PAIRS = N_Q_HEADS // 2
PAIRS_PER_KV = N_PAIRS // N_KV_HEADS

ADAM_LR = 0.001
ADAM_B1 = 0.9
ADAM_B2 = 0.999
ADAM_EPS = 1e-08
ADAM_WD = 0.01
ADAM_STEP = 10

N_DEV = 8
LANES = 128
VMEM_LIMIT = 56 * 1024 * 1024
IN_PROJ_VMEM_LIMIT = 61 * 1024 * 1024

MESH = pl.DeviceIdType.MESH
ANY = pl.BlockSpec(memory_space=pl.ANY)


def _cparams(sem=None):
    return pltpu.CompilerParams(dimension_semantics=sem, vmem_limit_bytes=VMEM_LIMIT)


def _tile(n, prefs):
    for t in prefs:
        if n % t == 0:
            return t
    return n


def _sigmoid(z):
    return 1.0 / (1.0 + jnp.exp(-z))


def _dot(a, b, ca, cb):
    return lax.dot_general(a, b, (((ca,), (cb,)), ((), ())), preferred_element_type=F32)


def _my_place():
    return lax.axis_index("x"), lax.axis_index("y"), lax.axis_index("c")


def _chip_of(x, y, r):
    return (x ^ (r & 1), y ^ (r >> 1))


def _peer(x, y, c, k):
    return (x ^ (k >> 2), y ^ ((k >> 1) & 1), c ^ (k & 1))


def _index(px, py, pc):
    return 4 * px + 2 * py + pc


class _Comm:
    def __init__(self, inputs, out_shape, scratch, bind):
        self.inputs, self.out_shape, self.scratch, self.bind = list(inputs), list(out_shape), list(scratch), bind


def gather_comm(shards, stack=False, direct=False):
    n_arr = len(shards)

    def bind(ins, outs, sems):
        send_sems, recv_sems, local_sems = sems
        x, y, c = _my_place()
        me, sibling = (x, y, c), (x, y, 1 - c)
        chips = [_chip_of(x, y, r) for r in (1, 2, 3)]

        def rows(a, px, py, pc):
            d = _index(px, py, pc)
            if stack:
                return outs[a].at[d]
            m = shards[a].shape[0]
            return outs[a].at[pl.ds(pl.multiple_of(d * m, 8), m), :]

        def copy(a, k, block, to, src=None):
            return pltpu.make_async_remote_copy(
                src_ref=rows(a, *block) if src is None else src, dst_ref=rows(a, *block),
                send_sem=send_sems.at[a * 7 + k], recv_sem=recv_sems.at[a * 7 + k],
                device_id=to, device_id_type=MESH)

        def mine(a):
            return pltpu.make_async_copy(ins[a], rows(a, *me), local_sems.at[a])

        def own_sends(a):
            if direct:
                return [copy(a, k - 1, me, _peer(x, y, c, k), src=ins[a]) for k in range(1, 8)]
            return ([copy(a, 0, me, sibling, src=ins[a])]
                    + [copy(a, 1 + j, me, (*chip, c), src=ins[a]) for j, chip in enumerate(chips)])

        def start():
            for a in range(n_arr):
                mine(a).start()
                for cp in own_sends(a):
                    cp.start()

        def relay():
            if direct:
                return
            for j, chip in enumerate(chips):
                for a in range(n_arr):
                    copy(a, 1 + j, (*chip, c), me).wait_recv()
                    copy(a, 4 + j, (*chip, c), sibling).start()

        def finish():
            for a in range(n_arr):
                if direct:
                    for k in range(1, 8):
                        copy(a, k - 1, _peer(x, y, c, k), me).wait_recv()
                else:
                    copy(a, 0, sibling, me).wait_recv()
                    for j, chip in enumerate(chips):
                        copy(a, 4 + j, (*chip, 1 - c), me).wait_recv()
                        copy(a, 4 + j, (*chip, c), sibling).wait_send()
                for cp in own_sends(a):
                    cp.wait_send()
                mine(a).wait()

        return start, relay, finish

    def gathered(s):
        return (N_DEV, *s.shape) if stack else (N_DEV * s.shape[0], s.shape[1])

    return _Comm(shards, [jax.ShapeDtypeStruct(gathered(s), s.dtype) for s in shards],
                 [pltpu.SemaphoreType.DMA((7 * n_arr,)), pltpu.SemaphoreType.DMA((7 * n_arr,)),
                  pltpu.SemaphoreType.DMA((n_arr,))], bind)


def scatter_comm(parts):
    n_arr = len(parts)

    def bind(ins, outs, sems):
        send_sems, recv_sems, local_sems = sems
        x, y, c = _my_place()
        my_index = _index(x, y, c)

        def block(a, d):
            m = parts[a].shape[0] // N_DEV
            return ins[a].at[pl.ds(pl.multiple_of(d * m, 16), m), :]

        def copy(a, k, slot):
            peer = _peer(x, y, c, k)
            return pltpu.make_async_remote_copy(
                src_ref=block(a, _index(*peer)), dst_ref=outs[a].at[slot],
                send_sem=send_sems.at[a * 7 + k - 1], recv_sem=recv_sems.at[a * 7 + k - 1],
                device_id=peer, device_id_type=MESH)

        def mine(a):
            return pltpu.make_async_copy(block(a, my_index), outs[a].at[my_index], local_sems.at[a])

        def start():
            for a in range(n_arr):
                mine(a).start()
                for k in range(1, 8):
                    copy(a, k, my_index).start()

        def finish():
            for a in range(n_arr):
                for k in range(1, 8):
                    copy(a, k, _index(*_peer(x, y, c, k))).wait_recv()
                    copy(a, k, my_index).wait_send()
                mine(a).wait()

        return start, (lambda: None), finish

    return _Comm(parts, [jax.ShapeDtypeStruct((N_DEV, p.shape[0] // N_DEV, p.shape[1]), p.dtype) for p in parts],
                 [pltpu.SemaphoreType.DMA((7 * n_arr,)), pltpu.SemaphoreType.DMA((7 * n_arr,)),
                  pltpu.SemaphoreType.DMA((n_arr,))], bind)


def pair_comm(parts):
    n_arr = len(parts)

    def bind(ins, outs, sems):
        send_sems, recv_sems = sems
        x, y, c = _my_place()

        def copies():
            out = []
            for a in range(n_arr):
                m = parts[a].shape[0] // N_DEV
                for r in range(4):
                    owner = _index(*_chip_of(x, y, r), 1 - c)
                    out.append(pltpu.make_async_remote_copy(
                        src_ref=ins[a].at[pl.ds(pl.multiple_of(owner * m, 16), m), :], dst_ref=outs[a].at[r],
                        send_sem=send_sems.at[a * 4 + r], recv_sem=recv_sems.at[a * 4 + r],
                        device_id=(x, y, 1 - c), device_id_type=MESH))
            return out

        def start():
            for cp in copies():
                cp.start()

        def finish():
            for cp in copies():
                cp.wait_recv()
                cp.wait_send()

        return start, (lambda: None), finish

    return _Comm(parts, [jax.ShapeDtypeStruct((4, p.shape[0] // N_DEV, p.shape[1]), p.dtype) for p in parts],
                 [pltpu.SemaphoreType.DMA((4 * n_arr,)), pltpu.SemaphoreType.DMA((4 * n_arr,))], bind)


def chips_comm(sums):
    n_arr = len(sums)

    def bind(ins, outs, sems):
        send_sems, recv_sems = sems
        x, y, c = _my_place()

        def copies():
            return [pltpu.make_async_remote_copy(
                src_ref=ins[a].at[r], dst_ref=outs[a].at[r - 1],
                send_sem=send_sems.at[a * 3 + r - 1], recv_sem=recv_sems.at[a * 3 + r - 1],
                device_id=(*_chip_of(x, y, r), c), device_id_type=MESH) for a in range(n_arr) for r in (1, 2, 3)]

        def start():
            for cp in copies():
                cp.start()

        def finish():
            for cp in copies():
                cp.wait_recv()
                cp.wait_send()

        return start, (lambda: None), finish

    return _Comm(sums, [jax.ShapeDtypeStruct((3,) + s.shape[1:], s.dtype) for s in sums],
                 [pltpu.SemaphoreType.DMA((3 * n_arr,)), pltpu.SemaphoreType.DMA((3 * n_arr,))], bind)


def run_comm(comm, name):
    n_in, n_out = len(comm.inputs), len(comm.out_shape)

    def body(*refs):
        start, relay, finish = comm.bind(refs[:n_in], refs[n_in:n_in + n_out], refs[n_in + n_out:])
        start()
        relay()
        finish()

    outs = pl.pallas_call(body, name=name, out_shape=comm.out_shape, in_specs=[ANY] * n_in,
                          out_specs=[ANY] * n_out, scratch_shapes=comm.scratch)(*comm.inputs)
    return list(outs)


def _chip_copy(r, src_ref, land_ref, send_sem, recv_sem):
    x, y, c = _my_place()
    return pltpu.make_async_remote_copy(src_ref=src_ref.at[r], dst_ref=land_ref.at[r - 1], send_sem=send_sem,
                                        recv_sem=recv_sem, device_id=(*_chip_of(x, y, r), c), device_id_type=MESH)


def chips_exchange_start(sums):
    def body(src_ref, land_ref, s1, s2, s3, r1, r2, r3, src_thru, land_thru, token):
        del src_thru, land_thru
        for r, send_sem, recv_sem in ((1, s1, r1), (2, s2, r2), (3, s3, r3)):
            _chip_copy(r, src_ref, land_ref, send_sem, recv_sem).start()
        token[...] = jnp.zeros_like(token)

    land = lax.empty((3,) + sums.shape[1:], sums.dtype)
    sem = pltpu.SemaphoreType.DMA(())
    hbm = pl.BlockSpec(memory_space=pltpu.HBM)
    sem_spec = pl.BlockSpec(memory_space=pltpu.SEMAPHORE)
    return pl.pallas_call(
        body, name="grad_exchange_chips_start",
        out_shape=(sem,) * 6 + (pltpu.HBM(sums.shape, sums.dtype), pltpu.HBM(land.shape, land.dtype),
                                jax.ShapeDtypeStruct((8, LANES), F32)),
        in_specs=(hbm, hbm), out_specs=(sem_spec,) * 6 + (hbm, hbm, pl.BlockSpec(memory_space=pltpu.VMEM)),
        input_output_aliases={0: 6, 1: 7},
        compiler_params=pltpu.CompilerParams(has_side_effects=pltpu.SideEffectType.DATAFLOW_SIDE_EFFECTING),
    )(pltpu.with_memory_space_constraint(sums, pltpu.HBM), pltpu.with_memory_space_constraint(land, pltpu.HBM))


def chips_exchange_wait(started, after):
    s1, s2, s3, r1, r2, r3, src_thru, land_thru, _ = started

    def body(src_ref, land_ref, s1, s2, s3, r1, r2, r3, after_ref, src_out, land_out):
        del after_ref, src_out, land_out
        for r, send_sem, recv_sem in ((1, s1, r1), (2, s2, r2), (3, s3, r3)):
            copy = _chip_copy(r, src_ref, land_ref, send_sem, recv_sem)
            copy.wait_send()
            copy.wait_recv()

    hbm = pl.BlockSpec(memory_space=pltpu.HBM)
    sem_spec = pl.BlockSpec(memory_space=pltpu.SEMAPHORE)
    return pl.pallas_call(
        body, name="grad_exchange_chips_wait",
        out_shape=(pltpu.HBM(src_thru.shape, src_thru.dtype), pltpu.HBM(land_thru.shape, land_thru.dtype)),
        in_specs=(hbm, hbm) + (sem_spec,) * 6 + (pl.BlockSpec(memory_space=pl.ANY),), out_specs=(hbm, hbm),
        input_output_aliases={0: 0, 1: 1},
        compiler_params=pltpu.CompilerParams(has_side_effects=pltpu.SideEffectType.DATAFLOW_SIDE_EFFECTING),
    )(src_thru, land_thru, s1, s2, s3, r1, r2, r3, after)


class _Hosted:
    def __init__(self, comms):
        self.comms = list(comms)
        self.inputs = [a for cm in self.comms for a in cm.inputs]
        self.out_shape = [s for cm in self.comms for s in cm.out_shape]
        self.scratch = [s for cm in self.comms for s in cm.scratch]
        self.in_specs = [ANY] * len(self.inputs)
        self.out_specs = [ANY] * len(self.out_shape)

    def split(self, refs, n_in, n_out, n_scratch):
        ni, no = len(self.inputs), len(self.out_shape)
        ins, rest = refs[:n_in], refs[n_in:]
        c_ins, rest = rest[:ni], rest[ni:]
        outs, rest = rest[:n_out], rest[n_out:]
        c_outs, rest = rest[:no], rest[no:]
        scratch, c_sems = rest[:n_scratch], rest[n_scratch:]
        phases = []
        for cm in self.comms:
            a, b, s = len(cm.inputs), len(cm.out_shape), len(cm.scratch)
            phases.append(cm.bind(c_ins[:a], c_outs[:b], c_sems[:s]))
            c_ins, c_outs, c_sems = c_ins[a:], c_outs[b:], c_sems[s:]
        return ins, outs, scratch, phases


def _before_step(phases, step, n_steps):
    if not phases:
        return

    @pl.when(step == 0)
    def _():
        for start, _, _ in phases:
            start()

    @pl.when(step == n_steps // 2)
    def _():
        for _, relay, _ in phases:
            relay()


def _after_step(phases, step, n_steps):
    if not phases:
        return

    @pl.when(step == n_steps - 1)
    def _():
        for _, _, finish in phases:
            finish()


def pair_sum(part, got, owners, name):
    m, n = got.shape[1:]

    def body(own_ref, mine_ref, got_ref, out_ref):
        del own_ref
        out_ref[...] = (mine_ref[...].astype(F32) + got_ref[...].astype(F32)).astype(out_ref.dtype)

    return pl.pallas_call(
        body, name=name,
        grid_spec=pltpu.PrefetchScalarGridSpec(
            num_scalar_prefetch=1, grid=(4,),
            in_specs=[pl.BlockSpec((m, n), lambda r, own: (own[r], 0)),
                      pl.BlockSpec((None, m, n), lambda r, own: (r, 0, 0))],
            out_specs=pl.BlockSpec((None, m, n), lambda r, own: (r, 0, 0))),
        out_shape=jax.ShapeDtypeStruct((4, m, n), got.dtype),
        compiler_params=_cparams(("arbitrary",)),
    )(owners, part, got)


def in_proj_gather(x, pos_col, freq, sign, g_pre, wt_shard, bias):
    s, d = x.shape
    tm = _tile(s, (512, 256, 128))
    nt = s // tm
    tc = _tile(s, (256, 128))
    nc = s // tc
    m = wt_shard.shape[0]
    half = D_IN // 2
    xi = lax.axis_index("x")
    order = jnp.stack([xi, 1 - xi]).astype(jnp.int32)

    def body(order_ref, x_hbm, pos_hbm, freq_ref, sign_ref, g_ref, b_ref, shard_ref,
             proj_ref, h_hbm, rope_hbm, wt_ref,
             w_vmem, h_vmem, xbuf, posbuf, ropebuf, send_sems, recv_sems, local_sems, in_sems, out_sems):
        del order_ref
        p, i = pl.program_id(0), pl.program_id(1)
        xx, yy, cc = _my_place()
        me, sibling = (xx, yy, cc), (xx, yy, 1 - cc)
        chips = [_chip_of(xx, yy, r) for r in (1, 2, 3)]

        def rows(px, py, pc):
            return wt_ref.at[pl.ds(pl.multiple_of(_index(px, py, pc) * m, 16), m), :]

        def copy(k, block, to, src=None):
            return pltpu.make_async_remote_copy(
                src_ref=rows(*block) if src is None else src, dst_ref=rows(*block),
                send_sem=send_sems.at[k], recv_sem=recv_sems.at[k], device_id=to, device_id_type=MESH)

        def mine():
            return pltpu.make_async_copy(shard_ref, rows(*me), local_sems.at[0])

        def to_sibling():
            return copy(0, me, sibling, src=shard_ref)

        def to_chip(j):
            return copy(1 + j, me, (*chips[j], cc), src=shard_ref)

        def relay(j):
            copy(1 + j, (*chips[j], cc), me).wait_recv()
            copy(4 + j, (*chips[j], cc), sibling).start()

        def relayed(j):
            copy(4 + j, (*chips[j], 1 - cc), me).wait_recv()

        def load_half(which, slot):
            rows_of_half = wt_ref.at[pl.ds(pl.multiple_of(which * half, 16), half), :]
            load = pltpu.make_async_copy(rows_of_half, w_vmem.at[slot], local_sems.at[1 + slot])
            load.start()
            load.wait()

        def piece(ref, c):
            return ref.at[pl.ds(c * tc, tc), :]

        def fetch(c):
            return (pltpu.make_async_copy(piece(x_hbm, c), xbuf.at[c % 2], in_sems.at[c % 2]),
                    pltpu.make_async_copy(piece(pos_hbm, c), posbuf.at[c % 2], in_sems.at[2 + c % 2]))

        def put(c):
            return (pltpu.make_async_copy(piece(h_vmem, c), piece(h_hbm, c), out_sems.at[c % 2]),
                    pltpu.make_async_copy(ropebuf.at[c % 2], piece(rope_hbm, c), out_sems.at[2 + c % 2]))

        def prologue():
            for cp in fetch(0):
                cp.start()
            for c in range(nc):
                if c + 1 < nc:
                    for cp in fetch(c + 1):
                        cp.start()
                for cp in fetch(c):
                    cp.wait()
                if c >= 2:
                    for cp in put(c - 2):
                        cp.wait()
                xv = xbuf[c % 2]
                r = lax.rsqrt(jnp.mean(xv * xv, axis=-1, keepdims=True) + EPS)
                h_vmem[c * tc:(c + 1) * tc, :] = (xv * r * g_ref[...]).astype(BF16)
                ang = posbuf[c % 2].astype(F32) * freq_ref[...]
                ropebuf[c % 2, :, :LANES] = jnp.cos(ang)
                ropebuf[c % 2, :, LANES:] = jnp.sin(ang) * sign_ref[...]
                for cp in put(c):
                    cp.start()
            for c in range(max(nc - 2, 0), nc):
                for cp in put(c):
                    cp.wait()

        @pl.when(jnp.logical_and(p == 0, i == 0))
        def _():
            mine().start()
            to_sibling().start()
            to_chip(1).start()
            to_chip(0).start()
            prologue()
            copy(0, sibling, me).wait_recv()
            relay(1)
            relayed(1)
            mine().wait()
            to_chip(1).wait_send()
            to_chip(0).wait_send()
            to_chip(2).start()
            load_half(xx, 0)

        @pl.when(jnp.logical_and(p == 1, i == 0))
        def _():
            relayed(0)
            relayed(2)
            load_half(1 - xx, 1)

        def project(slot):
            hb = h_vmem[pl.ds(pl.multiple_of(i * tm, tm), tm), :]
            proj_ref[...] = (_dot(hb, w_vmem[slot], 1, 1) + b_ref[...]).astype(BF16)

        @pl.when(p == 0)
        def _():
            project(0)

        @pl.when(p == 1)
        def _():
            project(1)

        @pl.when(jnp.logical_and(p == 0, i == 1))
        def _():
            relay(0)

        @pl.when(jnp.logical_and(p == 0, i == nt - 1))
        def _():
            relay(2)

        @pl.when(jnp.logical_and(p == 1, i == nt - 1))
        def _():
            to_sibling().wait_send()
            to_chip(2).wait_send()
            for j in range(3):
                copy(4 + j, (*chips[j], cc), sibling).wait_send()

    const = lambda p, i, o: (0, 0)
    return pl.pallas_call(
        body, name="in_proj_gather",
        grid_spec=pltpu.PrefetchScalarGridSpec(
            num_scalar_prefetch=1, grid=(2, nt),
            in_specs=[ANY, ANY,
                      pl.BlockSpec((1, LANES), const),
                      pl.BlockSpec((1, LANES), const),
                      pl.BlockSpec((1, d), const),
                      pl.BlockSpec((1, half), lambda p, i, o: (0, o[p])),
                      ANY],
            out_specs=[pl.BlockSpec((tm, half), lambda p, i, o: (i, o[p])), ANY, ANY, ANY],
            scratch_shapes=[pltpu.VMEM((2, half, d), BF16), pltpu.VMEM((s, d), BF16),
                            pltpu.VMEM((2, tc, d), F32), pltpu.VMEM((2, tc, 1), jnp.int32),
                            pltpu.VMEM((2, tc, 2 * LANES), F32),
                            pltpu.SemaphoreType.DMA((7,)), pltpu.SemaphoreType.DMA((7,)),
                            pltpu.SemaphoreType.DMA((3,)), pltpu.SemaphoreType.DMA((4,)),
                            pltpu.SemaphoreType.DMA((4,))]),
        out_shape=[jax.ShapeDtypeStruct((s, D_IN), BF16), jax.ShapeDtypeStruct((s, d), BF16),
                   jax.ShapeDtypeStruct((s, 2 * LANES), F32), jax.ShapeDtypeStruct((D_IN, d), BF16)],
        compiler_params=pltpu.CompilerParams(dimension_semantics=("arbitrary", "arbitrary"),
                                             vmem_limit_bytes=IN_PROJ_VMEM_LIMIT),
    )(order, x, pos_col, freq, sign, g_pre, bias, wt_shard)


def out_proj_loss(cat, w_out, x, target, g_post):
    s, d = x.shape
    tm = _tile(s, (256, 128))

    def body(cat_ref, w_ref, x_ref, t_ref, g_ref, dy_ref, dout_ref, dg_ref, loss_ref):
        @pl.when(pl.program_id(0) == 0)
        def _():
            dg_ref[...] = jnp.zeros_like(dg_ref)
            loss_ref[...] = jnp.zeros_like(loss_ref)

        g = g_ref[...]
        ys = [_dot(cat_ref[c0:c0 + CHUNK, :], w_ref[...], 1, 0) for c0 in range(0, tm, CHUNK)]
        for c0 in range(0, tm, CHUNK):
            rows = slice(c0, c0 + CHUNK)
            yv = ys[c0 // CHUNK]
            r = lax.rsqrt(jnp.mean(yv * yv, axis=-1, keepdims=True) + EPS)
            nrm = yv * r
            err = x_ref[rows, :] + nrm * g - t_ref[rows, :]
            loss_ref[...] += 0.5 * jnp.sum(jnp.sum(err * err, axis=-1, keepdims=True), axis=0, keepdims=True) / d
            dout = err * (1.0 / d)
            dout_ref[rows, :] = dout
            dg_ref[...] += jnp.sum(dout * nrm, axis=0, keepdims=True)
            dn = dout * g
            dy = r * (dn - nrm * jnp.mean(dn * nrm, axis=-1, keepdims=True))
            dy_ref[rows, :] = dy.astype(BF16)

    return pl.pallas_call(
        body, name="out_proj_loss", grid=(s // tm,),
        in_specs=[pl.BlockSpec((tm, d), lambda i: (i, 0)),
                  pl.BlockSpec((d, d), lambda i: (0, 0)),
                  pl.BlockSpec((tm, d), lambda i: (i, 0)),
                  pl.BlockSpec((tm, d), lambda i: (i, 0)),
                  pl.BlockSpec((1, d), lambda i: (0, 0))],
        out_specs=[pl.BlockSpec((tm, d), lambda i: (i, 0)),
                   pl.BlockSpec((tm, d), lambda i: (i, 0)),
                   pl.BlockSpec((1, d), lambda i: (0, 0)),
                   pl.BlockSpec((1, LANES), lambda i: (0, 0))],
        out_shape=[jax.ShapeDtypeStruct((s, d), BF16), jax.ShapeDtypeStruct((s, d), F32),
                   jax.ShapeDtypeStruct((1, d), F32), jax.ShapeDtypeStruct((1, LANES), F32)],
        compiler_params=_cparams(("arbitrary",)),
    )(cat, w_out, x, target, g_post)


def matmul_nt(a, b, name):
    m, k = a.shape
    n = b.shape[0]
    tm = _tile(m, (512, 256, 128))

    def body(a_ref, b_ref, o_ref):
        o_ref[...] = _dot(a_ref[...], b_ref[...], 1, 1).astype(o_ref.dtype)

    return pl.pallas_call(
        body, name=name, grid=(m // tm,),
        in_specs=[pl.BlockSpec((tm, k), lambda i: (i, 0)), pl.BlockSpec((n, k), lambda i: (0, 0))],
        out_specs=pl.BlockSpec((tm, n), lambda i: (i, 0)),
        out_shape=jax.ShapeDtypeStruct((m, n), BF16),
        compiler_params=_cparams(("arbitrary",)),
    )(a, b)


def matmul_tn(a, b, tm, name, comms=()):
    k, m = a.shape
    n = b.shape[1]
    steps = m // tm
    hosted = _Hosted(comms)

    kc = _tile(k, (1024, 128))
    pieces = k // kc

    def body(*refs):
        (a_ref, b_hbm), (o_ref, cs_ref), (b_ref, b_sems), phases = hosted.split(refs, 2, 2, 2)
        step = pl.program_id(0)
        _before_step(phases, step, steps)

        def b_load(j):
            return pltpu.make_async_copy(b_hbm.at[j * kc:(j + 1) * kc, :], b_ref.at[j * kc:(j + 1) * kc, :], b_sems.at[j])

        @pl.when(step == 0)
        def _():
            for j in range(pieces):
                b_load(j).start()
            acc = None
            for j in range(pieces):
                b_load(j).wait()
                part = _dot(a_ref[j * kc:(j + 1) * kc, :], b_ref[j * kc:(j + 1) * kc, :], 0, 0)
                acc = part if acc is None else acc + part
            o_ref[...] = acc.astype(o_ref.dtype)

        @pl.when(step > 0)
        def _():
            o_ref[...] = _dot(a_ref[...], b_ref[...], 0, 0).astype(o_ref.dtype)

        rows = _tile(k, (512, 128))
        cs = jnp.zeros((1, tm), F32)
        for r0 in range(0, k, rows):
            cs = cs + jnp.sum(a_ref[r0:r0 + rows, :].astype(F32), axis=0, keepdims=True)
        cs_ref[...] = cs
        _after_step(phases, step, steps)

    return pl.pallas_call(
        body, name=name, grid=(steps,),
        in_specs=[pl.BlockSpec((k, tm), lambda i: (0, i)), ANY] + hosted.in_specs,
        out_specs=[pl.BlockSpec((tm, n), lambda i: (i, 0)), pl.BlockSpec((1, tm), lambda i: (0, i))] + hosted.out_specs,
        out_shape=[jax.ShapeDtypeStruct((m, n), BF16), jax.ShapeDtypeStruct((1, m), F32)] + hosted.out_shape,
        scratch_shapes=[pltpu.VMEM((k, n), b.dtype), pltpu.SemaphoreType.DMA((pieces,))] + hosted.scratch,
        compiler_params=_cparams(("arbitrary",)),
    )(a, b, *hosted.inputs)


def in_proj_bwd(dproj, wt, x, g_pre, dout, comms=()):
    s, d = x.shape
    tm = _tile(s, (512, 256, 128))
    steps = s // tm
    nsub = tm // CHUNK
    kchunks = [(k0, 1024) for k0 in range(0, 5120, 1024)] + [(5120, 256)]
    ksplit = len(kchunks)
    hosted = _Hosted(comms)

    def body(*refs):
        ((*dp_refs, w_hbm, x_hbm, g_ref, dout_hbm), (gx_hbm, dg_ref),
         (w_ref, w_sems, xbuf, dbuf, gbuf, in_sems, out_sems), phases) = hosted.split(refs, 4 + ksplit, 2, 7)
        step = pl.program_id(0)
        _before_step(phases, step, steps)

        def rows_of(ref, c):
            return ref.at[pl.ds(pl.multiple_of(step * tm + c * CHUNK, CHUNK), CHUNK), :]

        def fetches(c):
            return (pltpu.make_async_copy(rows_of(x_hbm, c), xbuf.at[c], in_sems.at[c]),
                    pltpu.make_async_copy(rows_of(dout_hbm, c), dbuf.at[c], in_sems.at[nsub + c]))

        def put(c):
            return pltpu.make_async_copy(gbuf.at[c % 2], rows_of(gx_hbm, c), out_sems.at[c % 2])

        for c in range(nsub):
            for cp in fetches(c):
                cp.start()

        def w_load(j):
            k0, kw = kchunks[j]
            return pltpu.make_async_copy(w_hbm.at[k0:k0 + kw, :], w_ref.at[k0:k0 + kw, :], w_sems.at[j])

        @pl.when(step == 0)
        def _():
            dg_ref[...] = jnp.zeros_like(dg_ref)
            for j in range(ksplit):
                w_load(j).start()

        dh_all = None
        for j, ((k0, kw), dp_ref) in enumerate(zip(kchunks, dp_refs)):
            @pl.when(step == 0)
            def _():
                w_load(j).wait()

            part = _dot(dp_ref[...], w_ref[k0:k0 + kw, :], 1, 0)
            dh_all = part if dh_all is None else dh_all + part
        for c in range(nsub):
            for cp in fetches(c):
                cp.wait()
            if c >= 2:
                put(c - 2).wait()
            elif c < nsub:
                @pl.when(step > 0)
                def _():
                    put(max(nsub - 2, 0) + c).wait()
            dh = dh_all[c * CHUNK:(c + 1) * CHUNK, :]
            xv = xbuf[c]
            r = lax.rsqrt(jnp.mean(xv * xv, axis=-1, keepdims=True) + EPS)
            xn = xv * r
            dg_ref[...] += jnp.sum(dh * xn, axis=0, keepdims=True)
            dn = dh * g_ref[...]
            gbuf[c % 2] = dbuf[c] + r * (dn - xn * jnp.mean(dn * xn, axis=-1, keepdims=True))
            put(c).start()
        @pl.when(step == steps - 1)
        def _():
            for c in range(max(nsub - 2, 0), nsub):
                put(c).wait()

        _after_step(phases, step, steps)

    side_in, side_out = pltpu.VMEM((nsub, CHUNK, d), F32), pltpu.VMEM((2, CHUNK, d), F32)
    row = pl.BlockSpec((1, d), lambda i: (0, 0))
    return pl.pallas_call(
        body, name="in_proj_bwd", grid=(steps,),
        in_specs=[pl.BlockSpec((tm, kw), functools.partial(lambda j, i: (i, j), k0 // kw)) for k0, kw in kchunks]
        + [ANY, ANY, row, ANY] + hosted.in_specs,
        out_specs=[ANY, row] + hosted.out_specs,
        out_shape=[jax.ShapeDtypeStruct((s, d), F32), jax.ShapeDtypeStruct((1, d), F32)] + hosted.out_shape,
        scratch_shapes=[pltpu.VMEM((D_IN, d), BF16), pltpu.SemaphoreType.DMA((ksplit,)), side_in, side_in, side_out,
                        pltpu.SemaphoreType.DMA((2 * nsub,)), pltpu.SemaphoreType.DMA((2,))] + hosted.scratch,
        compiler_params=_cparams(("arbitrary",)),
    )(*([dproj] * ksplit), wt, x, g_pre, dout, *hosted.inputs)


def _lane_iota(shape):
    return lax.broadcasted_iota(jnp.int32, shape, len(shape) - 1)


def _partner(v):
    low = (_lane_iota(v.shape) % HEAD_DIM) < (HEAD_DIM // 2)
    return jnp.where(low, pltpu.roll(v, LANES - HEAD_DIM // 2, 1), pltpu.roll(v, HEAD_DIM // 2, 1))


def _rope(v, cos, sin_signed):
    return v * cos + _partner(v) * sin_signed


def _rope_transposed(dv, cos, sin_signed):
    return dv * cos - _partner(dv) * sin_signed


def _both_halves(v, kv_head):
    keep = (_lane_iota(v.shape) >= HEAD_DIM) if kv_head else (_lane_iota(v.shape) < HEAD_DIM)
    return jnp.where(keep, v, pltpu.roll(v, HEAD_DIM, 1))


def _fold_halves(acc):
    return acc + pltpu.roll(acc, HEAD_DIM, 1)


def _by_half(a, b):
    shape = jnp.broadcast_shapes(jnp.shape(a), jnp.shape(b))
    return jnp.where(_lane_iota(shape) < HEAD_DIM, a, b)


def _stack_heads(pair):
    return jnp.concatenate([_by_half(pair, 0.0), _by_half(0.0, pair)], axis=0)


def _band_bias(has_prev):
    i = lax.broadcasted_iota(jnp.int32, (2 * CHUNK, 2 * CHUNK), 0) % CHUNK
    j = lax.broadcasted_iota(jnp.int32, (2 * CHUNK, 2 * CHUNK), 1)
    band = jnp.logical_and(j > i, j <= i + CHUNK)
    return jnp.where(jnp.logical_and(band, jnp.logical_or(j >= CHUNK, has_prev)), 0.0, NEG)


def _probs_staged(qm2s, kk2s, bias, sink_cols):
    k = range(len(qm2s))
    scs = [_dot(qm2s[i], kk2s[i], 1, 1) + bias for i in k]
    mxs = [jnp.maximum(jnp.max(scs[i], axis=-1, keepdims=True), sink_cols[i]) for i in k]
    ps = [jnp.exp(scs[i] - mxs[i]) for i in k]
    ess = [jnp.exp(sink_cols[i] - mxs[i]) for i in k]
    invs = [1.0 / (jnp.sum(ps[i], axis=-1, keepdims=True) + ess[i]) for i in k]
    return [ps[i] * invs[i] for i in k], [ess[i] * invs[i] for i in k]


def _sink_col(sinks_ref, pair):
    row = lax.broadcasted_iota(jnp.int32, (2 * CHUNK, 1), 0)
    return jnp.where(row < CHUNK, sinks_ref[2 * pair], sinks_ref[2 * pair + 1])


def _layer_norm_parts(v):
    mu = jnp.mean(v, axis=-1, keepdims=True)
    xc = v - mu
    rstd = lax.rsqrt(jnp.mean(xc * xc, axis=-1, keepdims=True) + EPS)
    return xc * rstd, rstd


def _masked_spatial(w_ref, g):
    t = lax.broadcasted_iota(jnp.int32, (CHUNK, CHUNK), 0)
    sidx = lax.broadcasted_iota(jnp.int32, (CHUNK, CHUNK), 1)
    return jnp.where(t >= sidx, w_ref[g], 0.0).astype(BF16)


def _keys_values(kv_ref, kvp_ref, rope_ref, ropep_ref):
    cos_c, sin_c = rope_ref[:, :LANES], rope_ref[:, LANES:]
    cos_p, sin_p = ropep_ref[:, :LANES], ropep_ref[:, LANES:]
    k_c = _rope(kv_ref[:, :D_KV].astype(F32), cos_c, sin_c)
    k_p = _rope(kvp_ref[:, :D_KV].astype(F32), cos_p, sin_p)
    keys = jnp.concatenate([k_p, k_c], axis=0)
    vals = jnp.concatenate([kvp_ref[:, D_KV:], kv_ref[:, D_KV:]], axis=0).astype(F32)
    return keys, vals, (cos_c, sin_c, cos_p, sin_p)


def mixer_fwd(proj, rope, ln_g, ln_b, w_sp, b_sp_rows, sinks, comms=()):
    s = proj.shape[0]
    nb = s // CHUNK
    hosted = _Hosted(comms)

    def body(sinks_ref, *refs):
        ((proj_ref, kvp_ref, rope_ref, ropep_ref, lng_ref, lnb_ref, w_ref, b_ref), (cat_ref, p_ref), _,
         phases) = hosted.split(refs, 8, 2, 0)
        n = pl.program_id(0)
        _before_step(phases, n, nb)
        xhat, _ = _layer_norm_parts(proj_ref[:, OFF_V:OFF_V + D_GMLP].astype(F32))
        vnb = (xhat * lng_ref[...] + lnb_ref[...]).astype(BF16)
        mixeds = [_dot(_masked_spatial(w_ref, g), vnb[:, g * CHUNK:(g + 1) * CHUNK], 1, 0) + b_ref[g]
                  for g in range(GROUPS)]
        for g in range(GROUPS):
            za = proj_ref[:, OFF_ZA + g * CHUNK:OFF_ZA + (g + 1) * CHUNK].astype(F32)
            u = proj_ref[:, OFF_U + g * CHUNK:OFF_U + (g + 1) * CHUNK].astype(F32)
            cat_ref[:, g * CHUNK:(g + 1) * CHUNK] = (u * mixeds[g] * (za * _sigmoid(za))).astype(BF16)
        kv_ref = proj_ref.at[:, OFF_K:OFF_K + 2 * D_KV]
        keys, vals, (cos_c, sin_c, _, _) = _keys_values(kv_ref, kvp_ref, rope_ref, ropep_ref)
        cos_q, sin_q = cos_c * SCALE, sin_c * SCALE
        bias = _band_bias(n > 0)
        kk2 = [_both_halves(keys, kvh).astype(BF16) for kvh in range(N_KV_HEADS)]
        vv2 = [_both_halves(vals, kvh).astype(BF16) for kvh in range(N_KV_HEADS)]
        pairs = range(N_PAIRS)
        qms = [_stack_heads(_rope(proj_ref[:, OFF_Q + pair * LANES:OFF_Q + (pair + 1) * LANES].astype(F32),
                                  cos_q, sin_q)).astype(BF16) for pair in pairs]
        probs, sink_probs = _probs_staged(qms, [kk2[pair // PAIRS_PER_KV] for pair in pairs], bias,
                                          [_sink_col(sinks_ref, pair) for pair in pairs])
        pbs = [p.astype(BF16) for p in probs]
        outs = [_dot(pbs[pair], vv2[pair // PAIRS_PER_KV], 1, 0) for pair in pairs]
        first_col = _lane_iota((2 * CHUNK, 2 * CHUNK)) == 0
        for pair in pairs:
            p_ref[0, pair] = jnp.where(first_col, sink_probs[pair].astype(BF16), pbs[pair])
        for pair in pairs:
            out_pair = _by_half(outs[pair][:CHUNK], outs[pair][CHUNK:])
            zb = proj_ref[:, OFF_ZB + pair * LANES:OFF_ZB + (pair + 1) * LANES].astype(F32)
            cat_ref[:, D_GMLP + pair * LANES:D_GMLP + (pair + 1) * LANES] = (
                out_pair * (zb * _sigmoid(zb))).astype(BF16)
        _after_step(phases, n, nb)

    prev = lambda n, *_: (jnp.maximum(n - 1, 0), 0)
    kv_block = OFF_K // (2 * D_KV)
    return pl.pallas_call(
        body, name="mixer_fwd",
        grid_spec=pltpu.PrefetchScalarGridSpec(
            num_scalar_prefetch=1, grid=(nb,),
            in_specs=[pl.BlockSpec((CHUNK, D_IN), lambda n, *_: (n, 0)),
                      pl.BlockSpec((CHUNK, 2 * D_KV), lambda n, *_: (jnp.maximum(n - 1, 0), kv_block)),
                      pl.BlockSpec((CHUNK, 2 * LANES), lambda n, *_: (n, 0)),
                      pl.BlockSpec((CHUNK, 2 * LANES), prev),
                      pl.BlockSpec((1, D_GMLP), lambda n, *_: (0, 0)),
                      pl.BlockSpec((1, D_GMLP), lambda n, *_: (0, 0)),
                      pl.BlockSpec((GROUPS, CHUNK, CHUNK), lambda n, *_: (0, 0, 0)),
                      pl.BlockSpec((GROUPS, CHUNK, CHUNK), lambda n, *_: (0, 0, 0))] + hosted.in_specs,
            out_specs=[pl.BlockSpec((CHUNK, D_GMLP + D_ATTN), lambda n, *_: (n, 0)),
                       pl.BlockSpec((1, N_PAIRS, 2 * CHUNK, 2 * CHUNK), lambda n, *_: (n, 0, 0, 0))]
            + hosted.out_specs,
            scratch_shapes=hosted.scratch),
        out_shape=[jax.ShapeDtypeStruct((s, D_GMLP + D_ATTN), BF16),
                   jax.ShapeDtypeStruct((nb, N_PAIRS, 2 * CHUNK, 2 * CHUNK), BF16)] + hosted.out_shape,
        compiler_params=_cparams(("arbitrary",)),
    )(sinks, proj, proj, rope, rope, ln_g, ln_b, w_sp, b_sp_rows, *hosted.inputs)


def mixer_bwd(proj, dcat, probs, rope, ln_g, ln_b, w_sp, b_sp_rows, comms=()):
    s = proj.shape[0]
    nb = s // CHUNK
    hosted = _Hosted(comms)

    def body(*refs):
        ((proj_ref, kvp_ref, dcat_ref, p_ref, rope_ref, ropep_ref, lng_ref, lnb_ref, w_ref, b_ref),
         (dproj_ref, dw_ref, db_ref, dlng_ref, dlnb_ref, dsink_ref),
         (pend_ref, pend_kv_ref, dbacc_ref), phases) = hosted.split(refs, 10, 6, 3)
        n = pl.program_id(0)
        _before_step(phases, n, nb + 1)

        @pl.when(n == 0)
        def _():
            dw_ref[...] = jnp.zeros_like(dw_ref)
            dbacc_ref[...] = jnp.zeros_like(dbacc_ref)
            dlng_ref[...] = jnp.zeros_like(dlng_ref)
            dlnb_ref[...] = jnp.zeros_like(dlnb_ref)
            dsink_ref[...] = jnp.zeros_like(dsink_ref)

        @pl.when(n > 0)
        def _():
            dproj_ref[...] = pend_ref[...]

        def flush(dkv_prev):
            @pl.when(n > 0)
            def _():
                dproj_ref[:, OFF_K:OFF_K + 2 * D_KV] = (pend_kv_ref[...] + dkv_prev).astype(BF16)

        @pl.when(n < nb)
        def _():
            kv_ref = proj_ref.at[:, OFF_K:OFF_K + 2 * D_KV]
            keys, vals, (cos_c, sin_c, cos_p, sin_p) = _keys_values(kv_ref, kvp_ref, rope_ref, ropep_ref)
            cos_q, sin_q = cos_c * SCALE, sin_c * SCALE
            first_col = _lane_iota((2 * CHUNK, 2 * CHUNK)) == 0
            lane_row = _lane_iota((1, LANES))
            dsink = jnp.zeros((1, LANES), F32)
            dk_heads, dv_heads = [], []
            for kvh in range(N_KV_HEADS):
                kk2 = _both_halves(keys, kvh).astype(BF16)
                vv2 = _both_halves(vals, kvh).astype(BF16)
                pairs = list(range(kvh * PAIRS_PER_KV, (kvh + 1) * PAIRS_PER_KV))
                k4 = range(PAIRS_PER_KV)
                qm2s = [_stack_heads(_rope(proj_ref[:, OFF_Q + pair * LANES:OFF_Q + (pair + 1) * LANES].astype(F32),
                                           cos_q, sin_q)).astype(BF16) for pair in pairs]
                kept = [p_ref[0, pair] for pair in pairs]
                pbs = [jnp.where(first_col, jnp.zeros_like(kp), kp) for kp in kept]
                ps = [pb.astype(F32) for pb in pbs]
                p_sinks = [kp[:, 0:1].astype(F32) for kp in kept]
                o2s = [_dot(pb, vv2, 1, 0) for pb in pbs]
                zbs = [proj_ref[:, OFF_ZB + pair * LANES:OFF_ZB + (pair + 1) * LANES].astype(F32) for pair in pairs]
                sgs = [_sigmoid(zb) for zb in zbs]
                dybs = [dcat_ref[:, D_GMLP + pair * LANES:D_GMLP + (pair + 1) * LANES].astype(F32) for pair in pairs]
                for i, pair in enumerate(pairs):
                    out_pair = _by_half(o2s[i][:CHUNK], o2s[i][CHUNK:])
                    pend_ref[:, OFF_ZB + pair * LANES:OFF_ZB + (pair + 1) * LANES] = (
                        dybs[i] * out_pair * (sgs[i] * (1.0 + zbs[i] * (1.0 - sgs[i])))).astype(BF16)
                dom2s = [_stack_heads(dybs[i] * (zbs[i] * sgs[i])).astype(BF16) for i in k4]
                dps = [_dot(dom2, vv2, 1, 1) for dom2 in dom2s]
                deltas = [jnp.sum(ps[i] * dps[i], axis=-1, keepdims=True) for i in k4]
                dss = [ps[i] * (dps[i] - deltas[i]) for i in k4]
                for i, pair in enumerate(pairs):
                    dsk = -(p_sinks[i] * deltas[i])
                    dsink = dsink + jnp.where(lane_row == 2 * pair,
                                              jnp.sum(dsk[:CHUNK], axis=0, keepdims=True), 0.0)
                    dsink = dsink + jnp.where(lane_row == 2 * pair + 1,
                                              jnp.sum(dsk[CHUNK:], axis=0, keepdims=True), 0.0)
                dsbs = [ds.astype(BF16) for ds in dss]
                dq2s = [_dot(dsb, kk2, 1, 0) for dsb in dsbs]
                for pair, dq2 in zip(pairs, dq2s):
                    pend_ref[:, OFF_Q + pair * LANES:OFF_Q + (pair + 1) * LANES] = _rope_transposed(
                        _by_half(dq2[:CHUNK], dq2[CHUNK:]), cos_q, sin_q).astype(BF16)
                dkks = [_dot(dsbs[i], qm2s[i], 0, 0) for i in k4]
                dvvs = [_dot(pbs[i], dom2s[i], 0, 0) for i in k4]
                dk_heads.append(_fold_halves((dkks[0] + dkks[1]) + (dkks[2] + dkks[3])))
                dv_heads.append(_fold_halves((dvvs[0] + dvvs[1]) + (dvvs[2] + dvvs[3])))
            dk_rot = _by_half(dk_heads[0], dk_heads[1])
            dv_all = _by_half(dv_heads[0], dv_heads[1])
            dk_p = _rope_transposed(dk_rot[:CHUNK], cos_p, sin_p)
            dk_c = _rope_transposed(dk_rot[CHUNK:], cos_c, sin_c)
            flush(jnp.concatenate([dk_p, dv_all[:CHUNK]], axis=1))
            dsink_ref[...] += dsink
            pend_kv_ref[...] = jnp.concatenate([dk_c, dv_all[CHUNK:]], axis=1)
            xhat, rstd = _layer_norm_parts(proj_ref[:, OFF_V:OFF_V + D_GMLP].astype(F32))
            lng = lng_ref[...]
            vnb = (xhat * lng + lnb_ref[...]).astype(BF16)
            dvn_cols = []
            for g in range(GROUPS):
                cols = slice(g * CHUNK, (g + 1) * CHUNK)
                wm = _masked_spatial(w_ref, g)
                mixed = _dot(wm, vnb[:, cols], 1, 0) + b_ref[g]
                za = proj_ref[:, OFF_ZA + g * CHUNK:OFF_ZA + (g + 1) * CHUNK].astype(F32)
                u = proj_ref[:, OFF_U + g * CHUNK:OFF_U + (g + 1) * CHUNK].astype(F32)
                dya = dcat_ref[:, cols].astype(F32)
                sg = _sigmoid(za)
                sz = za * sg
                pend_ref[:, OFF_U + g * CHUNK:OFF_U + (g + 1) * CHUNK] = (dya * mixed * sz).astype(BF16)
                pend_ref[:, OFF_ZA + g * CHUNK:OFF_ZA + (g + 1) * CHUNK] = (
                    dya * u * mixed * (sg * (1.0 + za * (1.0 - sg)))).astype(BF16)
                dmixed = dya * u * sz
                dmb = dmixed.astype(BF16)
                dbacc_ref[g] += dmixed
                dw_ref[g] += _dot(dmb, vnb[:, cols], 1, 1)
                dvn_cols.append(_dot(wm, dmb, 0, 0))
            dvn = jnp.concatenate(dvn_cols, axis=1)
            dlng_ref[...] += jnp.sum(dvn * xhat, axis=0, keepdims=True)
            dlnb_ref[...] += jnp.sum(dvn, axis=0, keepdims=True)
            dxh = dvn * lng
            dv = rstd * (dxh - jnp.mean(dxh, axis=-1, keepdims=True)
                         - xhat * jnp.mean(dxh * xhat, axis=-1, keepdims=True))
            pend_ref[:, OFF_V:OFF_V + D_GMLP] = dv.astype(BF16)

        @pl.when(n == nb)
        def _():
            flush(jnp.zeros((CHUNK, 2 * D_KV), F32))
            t = lax.broadcasted_iota(jnp.int32, (CHUNK, CHUNK), 0)
            sidx = lax.broadcasted_iota(jnp.int32, (CHUNK, CHUNK), 1)
            lane = _lane_iota((CHUNK, LANES))
            dbt = jnp.zeros((CHUNK, LANES), F32)
            for g in range(GROUPS):
                dw_ref[g] = jnp.where(t >= sidx, dw_ref[g], 0.0)
                dbt = jnp.where(lane == g, jnp.sum(dbacc_ref[g], axis=-1, keepdims=True), dbt)
            db_ref[...] = jnp.transpose(dbt)[:GROUPS, :]

        _after_step(phases, n, nb + 1)

    cur = lambda n: (jnp.minimum(n, nb - 1), 0)
    prev = lambda n: (jnp.clip(n - 1, 0, nb - 1), 0)
    kv_block = OFF_K // (2 * D_KV)
    const2 = lambda n: (0, 0)
    const3 = lambda n: (0, 0, 0)
    return pl.pallas_call(
        body, name="mixer_bwd", grid=(nb + 1,),
        in_specs=[pl.BlockSpec((CHUNK, D_IN), cur),
                  pl.BlockSpec((CHUNK, 2 * D_KV), lambda n: (jnp.clip(n - 1, 0, nb - 1), kv_block)),
                  pl.BlockSpec((CHUNK, D_GMLP + D_ATTN), cur),
                  pl.BlockSpec((1, N_PAIRS, 2 * CHUNK, 2 * CHUNK), lambda n: (jnp.minimum(n, nb - 1), 0, 0, 0)),
                  pl.BlockSpec((CHUNK, 2 * LANES), cur),
                  pl.BlockSpec((CHUNK, 2 * LANES), prev),
                  pl.BlockSpec((1, D_GMLP), const2),
                  pl.BlockSpec((1, D_GMLP), const2),
                  pl.BlockSpec((GROUPS, CHUNK, CHUNK), const3),
                  pl.BlockSpec((GROUPS, CHUNK, CHUNK), const3)] + hosted.in_specs,
        out_specs=[pl.BlockSpec((CHUNK, D_IN), lambda n: (jnp.maximum(n - 1, 0), 0)),
                   pl.BlockSpec((GROUPS, CHUNK, CHUNK), const3),
                   pl.BlockSpec((GROUPS, CHUNK), const2),
                   pl.BlockSpec((1, D_GMLP), const2),
                   pl.BlockSpec((1, D_GMLP), const2),
                   pl.BlockSpec((1, LANES), const2)] + hosted.out_specs,
        scratch_shapes=[pltpu.VMEM((CHUNK, D_IN), BF16), pltpu.VMEM((CHUNK, 2 * D_KV), F32),
                        pltpu.VMEM((GROUPS, CHUNK, CHUNK), F32)] + hosted.scratch,
        out_shape=[jax.ShapeDtypeStruct((s, D_IN), BF16),
                   jax.ShapeDtypeStruct((GROUPS, CHUNK, CHUNK), F32),
                   jax.ShapeDtypeStruct((GROUPS, CHUNK), F32),
                   jax.ShapeDtypeStruct((1, D_GMLP), F32),
                   jax.ShapeDtypeStruct((1, D_GMLP), F32),
                   jax.ShapeDtypeStruct((1, LANES), F32)] + hosted.out_shape,
        compiler_params=_cparams(("arbitrary",)),
    )(proj, proj, dcat, probs, rope, rope, ln_g, ln_b, w_sp, b_sp_rows, *hosted.inputs)


def _adamw_math(w, g, m, v):
    m = ADAM_B1 * m + (1.0 - ADAM_B1) * g
    v = ADAM_B2 * v + (1.0 - ADAM_B2) * (g * g)
    m_hat = m / (1.0 - ADAM_B1 ** ADAM_STEP)
    v_hat = v / (1.0 - ADAM_B2 ** ADAM_STEP)
    delta = -ADAM_LR * (m_hat / (jnp.sqrt(v_hat) + ADAM_EPS) + ADAM_WD * w)
    return delta, m, v


def adamw_shard(terms, w, m, v, name):
    r, c = w.shape
    tr = _tile(r, (224, 256, 128, 8))
    n_terms = len(terms)

    def body(*refs):
        w_ref, m_ref, v_ref, g_ref, d_ref, nm_ref, nv_ref = refs[n_terms:]
        g = None
        for ref, (_, slots) in zip(refs[:n_terms], terms):
            for k in range(slots):
                part = ref[k].astype(F32)
                g = part if g is None else g + part
        g_ref[...] = g
        d_ref[...], nm_ref[...], nv_ref[...] = _adamw_math(w_ref[...], g, m_ref[...], v_ref[...])

    spec = pl.BlockSpec((tr, c), lambda i: (i, 0))
    return pl.pallas_call(
        body, name=name, grid=(r // tr,),
        in_specs=[pl.BlockSpec((slots, tr, c), lambda i: (0, i, 0)) for _, slots in terms] + [spec] * 3,
        out_specs=[spec] * 4, out_shape=[jax.ShapeDtypeStruct((r, c), F32)] * 4,
        compiler_params=_cparams(("arbitrary",)),
    )(*[a for a, _ in terms], w, m, v)


def adamw_small(gathered, lane_windows, params):
    n_par = len(params)

    def body(*refs):
        g_refs = refs[:n_par + 1]
        wmv_refs = refs[n_par + 1:4 * n_par + 1]
        out_refs = refs[4 * n_par + 1:]

        def total(ref):
            acc = ref[0]
            for dev in range(1, N_DEV):
                acc = acc + ref[dev]
            return acc

        for i in range(n_par):
            w_ref, m_ref, v_ref = wmv_refs[3 * i:3 * i + 3]
            g = total(g_refs[i])
            if lane_windows[i] is not None:
                start, size = lane_windows[i]
                g = g[..., start:start + size]
            delta, new_m, new_v = _adamw_math(w_ref[...], g, m_ref[...], v_ref[...])
            for ref, val in zip(out_refs[4 * i:4 * i + 4], (g, delta, new_m, new_v)):
                ref[...] = val
        out_refs[4 * n_par][...] = total(g_refs[n_par])

    flat = [a for wmv in params for a in wmv]
    out_shape = [jax.ShapeDtypeStruct(w.shape, F32) for (w, _, _) in params for _ in range(4)]
    out_shape.append(jax.ShapeDtypeStruct(gathered[-1].shape[1:], F32))
    outs = pl.pallas_call(body, name="adamw_small", out_shape=out_shape, compiler_params=_cparams())(*gathered, *flat)
    return [tuple(outs[4 * i:4 * i + 4]) for i in range(n_par)], outs[-1]


def kernel(x, positions, g_pre, w_in, b_qkv, ln_v_g, ln_v_b, w_spatial, b_spatial, attn_sinks, w_out, g_post, loss_target, m_g_pre, m_w_in, m_b_qkv, m_ln_v_g, m_ln_v_b, m_w_spatial, m_b_spatial, m_attn_sinks, m_w_out, m_g_post, v_g_pre, v_w_in, v_b_qkv, v_ln_v_g, v_ln_v_b, v_w_spatial, v_b_spatial, v_attn_sinks, v_w_out, v_g_post):
    x2, target = x[0], loss_target[0]
    seq = x2.shape[0]
    xi, yi, ci = _my_place()

    wt_shard = w_in[0].T.astype(BF16)
    wo_shard = w_out[0].astype(BF16)
    pos_col = positions.reshape(seq, 1)
    half = HEAD_DIM // 2
    inv_freq = ROPE_THETA ** (-jnp.arange(half, dtype=F32) * (2.0 / HEAD_DIM))
    freq = jnp.tile(inv_freq, LANES // half).reshape(1, LANES)
    sign = jnp.tile(jnp.concatenate([-jnp.ones((half,), F32), jnp.ones((half,), F32)]), LANES // HEAD_DIM)
    sign = sign.reshape(1, LANES)
    bias = jnp.concatenate([jnp.zeros((1, OFF_Q), F32), b_qkv, jnp.zeros((1, D_ATTN), F32)], axis=1)
    proj, h, rope, wt = in_proj_gather(x2, pos_col, freq, sign, g_pre, wt_shard, bias)

    b_rows = jnp.broadcast_to(b_spatial[0][:, :, None], (GROUPS, CHUNK, CHUNK))
    sinks = attn_sinks[0]
    cat, probs, wo = mixer_fwd(proj, rope, ln_v_g, ln_v_b, w_spatial[0], b_rows, sinks,
                               comms=[gather_comm([wo_shard])])
    dy, dout, d_g_post, loss_part = out_proj_loss(cat, wo, x2, target, g_post)

    dcat = matmul_nt(dy, wo, "out_proj_bwd")
    d_wo, _ = matmul_tn(cat, dy, 512, "w_out_grad")
    dproj, d_w_sp, d_b_sp, d_ln_g, d_ln_b, d_sinks, parts_wo = mixer_bwd(
        proj, dcat, probs, rope, ln_v_g, ln_v_b, w_spatial[0], b_rows, comms=[scatter_comm([d_wo])])
    small_parts = [d_ln_g, d_ln_b, d_w_sp, d_b_sp, d_sinks, d_g_post, loss_part]
    d_wt, colsum, *landed = matmul_tn(dproj, h, 768, "w_in_grad", comms=[gather_comm(small_parts, stack=True)])

    owners = jnp.stack([4 * cx + 2 * cy + ci for cx, cy in (_chip_of(xi, yi, r) for r in range(4))]).astype(jnp.int32)
    (got_wt,) = run_comm(pair_comm([d_wt]), "grad_exchange_pair")
    sum_wt = pair_sum(d_wt, got_wt, owners, "grad_pair_sum_w_in")
    started = chips_exchange_start(sum_wt)
    token = started[-1]
    grad_x, d_g_pre = in_proj_bwd(dproj, wt, x2, g_pre + token[:1, :1], dout)
    sum_wt, far_wt = chips_exchange_wait(started, d_g_pre)
    late = run_comm(gather_comm([d_g_pre, colsum], stack=True, direct=True), "allgather_late_grads")
    gathered = late + landed
    windows = [None, (OFF_Q, D_QKV), None, None, None, None, (0, N_Q_HEADS), None]
    small = [(g_pre, m_g_pre, v_g_pre), (b_qkv, m_b_qkv, v_b_qkv), (ln_v_g, m_ln_v_g, v_ln_v_g),
             (ln_v_b, m_ln_v_b, v_ln_v_b), (w_spatial[0], m_w_spatial[0], v_w_spatial[0]),
             (b_spatial[0], m_b_spatial[0], v_b_spatial[0]), (attn_sinks, m_attn_sinks, v_attn_sinks),
             (g_post, m_g_post, v_g_post)]
    small_out, loss_row = adamw_small(gathered, windows, small)
    lead = [False, False, False, False, True, True, False, False]
    small_out = [tuple(a[None] if ld else a for a in leaf) for leaf, ld in zip(small_out, lead)]

    wt_out = adamw_shard([(sum_wt, 1), (far_wt, 3)], w_in[0].T, m_w_in[0].T, v_w_in[0].T, "adamw_w_in")
    wo_out = adamw_shard([(parts_wo, N_DEV)], w_out[0], m_w_out[0], v_w_out[0], "adamw_w_out")

    def leaves(k):
        gp, bq, lg, lb, ws, bs, sk, gpo = (leaf[k] for leaf in small_out)
        return [gp, wt_out[k].T[None], bq, lg, lb, ws, bs, sk, wo_out[k][None], gpo]

    return (loss_row[0, 0], grad_x[None], *leaves(0), *leaves(1), *leaves(2), *leaves(3))
```

```python
import functools

import jax
import jax.numpy as jnp
from jax import lax
from jax.experimental import pallas as pl
from jax.experimental.pallas import tpu as pltpu

F32 = jnp.float32
BF16 = jnp.bfloat16

D_MODEL = 2048
D_GMLP = 1024
D_ATTN = 1024
CHUNK = 128
GROUPS = 8
HEAD_DIM = 64
N_Q_HEADS = 16
N_KV_HEADS = 2
D_KV = N_KV_HEADS * HEAD_DIM
D_IN = 3 * D_GMLP + D_ATTN + 2 * D_KV + D_ATTN
OFF_U, OFF_V, OFF_ZA = 0, D_GMLP, 2 * D_GMLP
OFF_Q = 3 * D_GMLP
OFF_K = OFF_Q + D_ATTN
OFF_VA = OFF_K + D_KV
OFF_ZB = OFF_VA + D_KV
D_QKV = D_ATTN + 2 * D_KV
ROPE_THETA = 10000.0
EPS = 1e-6
SCALE = HEAD_DIM ** -0.5
NEG = -1e30
N_PAIRS = N_Q_HEADS // 2
PAIRS_PER_KV = N_PAIRS // N_KV_HEADS

ADAM_LR = 0.001
ADAM_B1 = 0.9
ADAM_B2 = 0.999
ADAM_EPS = 1e-08
ADAM_WD = 0.01
ADAM_STEP = 10

N_DEV = 8
LANES = 128
VMEM_LIMIT = 56 * 1024 * 1024
IN_PROJ_VMEM_LIMIT = 61 * 1024 * 1024

MESH = pl.DeviceIdType.MESH
ANY = pl.BlockSpec(memory_space=pl.ANY)


def _cparams(sem=None):
    return pltpu.CompilerParams(dimension_semantics=sem, vmem_limit_bytes=VMEM_LIMIT)


def _tile(n, prefs):
    for t in prefs:
        if n % t == 0:
            return t
    return n


def _sigmoid(z):
    return 1.0 / (1.0 + jnp.exp(-z))


def _dot(a, b, ca, cb):
    return lax.dot_general(a, b, (((ca,), (cb,)), ((), ())), preferred_element_type=F32)


def _my_place():
    return lax.axis_index("x"), lax.axis_index("y"), lax.axis_index("c")


def _chip_of(x, y, r):
    return (x ^ (r & 1), y ^ (r >> 1))


def _peer(x, y, c, k):
    return (x ^ (k >> 2), y ^ ((k >> 1) & 1), c ^ (k & 1))


def _index(px, py, pc):
    return 4 * px + 2 * py + pc


class _Comm:
    def __init__(self, inputs, out_shape, scratch, bind):
        self.inputs, self.out_shape, self.scratch, self.bind = list(inputs), list(out_shape), list(scratch), bind


def gather_comm(shards, stack=False, direct=False):
    n_arr = len(shards)

    def bind(ins, outs, sems):
        send_sems, recv_sems, local_sems = sems
        x, y, c = _my_place()
        me, sibling = (x, y, c), (x, y, 1 - c)
        chips = [_chip_of(x, y, r) for r in (1, 2, 3)]

        def rows(a, px, py, pc):
            d = _index(px, py, pc)
            if stack:
                return outs[a].at[d]
            m = shards[a].shape[0]
            return outs[a].at[pl.ds(pl.multiple_of(d * m, 8), m), :]

        def copy(a, k, block, to, src=None):
            return pltpu.make_async_remote_copy(
                src_ref=rows(a, *block) if src is None else src, dst_ref=rows(a, *block),
                send_sem=send_sems.at[a * 7 + k], recv_sem=recv_sems.at[a * 7 + k],
                device_id=to, device_id_type=MESH)

        def mine(a):
            return pltpu.make_async_copy(ins[a], rows(a, *me), local_sems.at[a])

        def own_sends(a):
            if direct:
                return [copy(a, k - 1, me, _peer(x, y, c, k), src=ins[a]) for k in range(1, 8)]
            return ([copy(a, 0, me, sibling, src=ins[a])]
                    + [copy(a, 1 + j, me, (*chip, c), src=ins[a]) for j, chip in enumerate(chips)])

        def start():
            for a in range(n_arr):
                mine(a).start()
                for cp in own_sends(a):
                    cp.start()

        def relay():
            if direct:
                return
            for j, chip in enumerate(chips):
                for a in range(n_arr):
                    copy(a, 1 + j, (*chip, c), me).wait_recv()
                    copy(a, 4 + j, (*chip, c), sibling).start()

        def finish():
            for a in range(n_arr):
                if direct:
                    for k in range(1, 8):
                        copy(a, k - 1, _peer(x, y, c, k), me).wait_recv()
                else:
                    copy(a, 0, sibling, me).wait_recv()
                    for j, chip in enumerate(chips):
                        copy(a, 4 + j, (*chip, 1 - c), me).wait_recv()
                        copy(a, 4 + j, (*chip, c), sibling).wait_send()
                for cp in own_sends(a):
                    cp.wait_send()
                mine(a).wait()

        return start, relay, finish

    def gathered(s):
        return (N_DEV, *s.shape) if stack else (N_DEV * s.shape[0], s.shape[1])

    return _Comm(shards, [jax.ShapeDtypeStruct(gathered(s), s.dtype) for s in shards],
                 [pltpu.SemaphoreType.DMA((7 * n_arr,)), pltpu.SemaphoreType.DMA((7 * n_arr,)),
                  pltpu.SemaphoreType.DMA((n_arr,))], bind)


def scatter_comm(parts):
    n_arr = len(parts)

    def bind(ins, outs, sems):
        send_sems, recv_sems, local_sems = sems
        x, y, c = _my_place()
        my_index = _index(x, y, c)

        def block(a, d):
            m = parts[a].shape[0] // N_DEV
            return ins[a].at[pl.ds(pl.multiple_of(d * m, 16), m), :]

        def copy(a, k, slot):
            peer = _peer(x, y, c, k)
            return pltpu.make_async_remote_copy(
                src_ref=block(a, _index(*peer)), dst_ref=outs[a].at[slot],
                send_sem=send_sems.at[a * 7 + k - 1], recv_sem=recv_sems.at[a * 7 + k - 1],
                device_id=peer, device_id_type=MESH)

        def mine(a):
            return pltpu.make_async_copy(block(a, my_index), outs[a].at[my_index], local_sems.at[a])

        def start():
            for a in range(n_arr):
                mine(a).start()
                for k in range(1, 8):
                    copy(a, k, my_index).start()

        def finish():
            for a in range(n_arr):
                for k in range(1, 8):
                    copy(a, k, _index(*_peer(x, y, c, k))).wait_recv()
                    copy(a, k, my_index).wait_send()
                mine(a).wait()

        return start, (lambda: None), finish

    return _Comm(parts, [jax.ShapeDtypeStruct((N_DEV, p.shape[0] // N_DEV, p.shape[1]), p.dtype) for p in parts],
                 [pltpu.SemaphoreType.DMA((7 * n_arr,)), pltpu.SemaphoreType.DMA((7 * n_arr,)),
                  pltpu.SemaphoreType.DMA((n_arr,))], bind)


def pair_comm(parts):
    n_arr = len(parts)

    def bind(ins, outs, sems):
        send_sems, recv_sems = sems
        x, y, c = _my_place()

        def copies():
            out = []
            for a in range(n_arr):
                m = parts[a].shape[0] // N_DEV
                for r in range(4):
                    owner = _index(*_chip_of(x, y, r), 1 - c)
                    out.append(pltpu.make_async_remote_copy(
                        src_ref=ins[a].at[pl.ds(pl.multiple_of(owner * m, 16), m), :], dst_ref=outs[a].at[r],
                        send_sem=send_sems.at[a * 4 + r], recv_sem=recv_sems.at[a * 4 + r],
                        device_id=(x, y, 1 - c), device_id_type=MESH))
            return out

        def start():
            for cp in copies():
                cp.start()

        def finish():
            for cp in copies():
                cp.wait_recv()
                cp.wait_send()

        return start, (lambda: None), finish

    return _Comm(parts, [jax.ShapeDtypeStruct((4, p.shape[0] // N_DEV, p.shape[1]), p.dtype) for p in parts],
                 [pltpu.SemaphoreType.DMA((4 * n_arr,)), pltpu.SemaphoreType.DMA((4 * n_arr,))], bind)


def chips_comm(sums):
    n_arr = len(sums)

    def bind(ins, outs, sems):
        send_sems, recv_sems = sems
        x, y, c = _my_place()

        def copies():
            return [pltpu.make_async_remote_copy(
                src_ref=ins[a].at[r], dst_ref=outs[a].at[r - 1],
                send_sem=send_sems.at[a * 3 + r - 1], recv_sem=recv_sems.at[a * 3 + r - 1],
                device_id=(*_chip_of(x, y, r), c), device_id_type=MESH) for a in range(n_arr) for r in (1, 2, 3)]

        def start():
            for cp in copies():
                cp.start()

        def finish():
            for cp in copies():
                cp.wait_recv()
                cp.wait_send()

        return start, (lambda: None), finish

    return _Comm(sums, [jax.ShapeDtypeStruct((3,) + s.shape[1:], s.dtype) for s in sums],
                 [pltpu.SemaphoreType.DMA((3 * n_arr,)), pltpu.SemaphoreType.DMA((3 * n_arr,))], bind)


def run_comm(comm, name):
    n_in, n_out = len(comm.inputs), len(comm.out_shape)

    def body(*refs):
        start, relay, finish = comm.bind(refs[:n_in], refs[n_in:n_in + n_out], refs[n_in + n_out:])
        start()
        relay()
        finish()

    outs = pl.pallas_call(body, name=name, out_shape=comm.out_shape, in_specs=[ANY] * n_in,
                          out_specs=[ANY] * n_out, scratch_shapes=comm.scratch)(*comm.inputs)
    return list(outs)


def _chip_copy(r, src_ref, land_ref, send_sem, recv_sem):
    x, y, c = _my_place()
    return pltpu.make_async_remote_copy(src_ref=src_ref.at[r], dst_ref=land_ref.at[r - 1], send_sem=send_sem,
                                        recv_sem=recv_sem, device_id=(*_chip_of(x, y, r), c), device_id_type=MESH)


def chips_exchange_start(sums):
    def body(src_ref, land_ref, s1, s2, s3, r1, r2, r3, src_thru, land_thru, token):
        del src_thru, land_thru
        for r, send_sem, recv_sem in ((1, s1, r1), (2, s2, r2), (3, s3, r3)):
            _chip_copy(r, src_ref, land_ref, send_sem, recv_sem).start()
        token[...] = jnp.zeros_like(token)

    land = lax.empty((3,) + sums.shape[1:], sums.dtype)
    sem = pltpu.SemaphoreType.DMA(())
    hbm = pl.BlockSpec(memory_space=pltpu.HBM)
    sem_spec = pl.BlockSpec(memory_space=pltpu.SEMAPHORE)
    return pl.pallas_call(
        body, name="grad_exchange_chips_start",
        out_shape=(sem,) * 6 + (pltpu.HBM(sums.shape, sums.dtype), pltpu.HBM(land.shape, land.dtype),
                                jax.ShapeDtypeStruct((8, LANES), F32)),
        in_specs=(hbm, hbm), out_specs=(sem_spec,) * 6 + (hbm, hbm, pl.BlockSpec(memory_space=pltpu.VMEM)),
        input_output_aliases={0: 6, 1: 7},
        compiler_params=pltpu.CompilerParams(has_side_effects=pltpu.SideEffectType.DATAFLOW_SIDE_EFFECTING),
    )(pltpu.with_memory_space_constraint(sums, pltpu.HBM), pltpu.with_memory_space_constraint(land, pltpu.HBM))


def chips_exchange_wait(started, after):
    s1, s2, s3, r1, r2, r3, src_thru, land_thru, _ = started

    def body(src_ref, land_ref, s1, s2, s3, r1, r2, r3, after_ref, src_out, land_out):
        del after_ref, src_out, land_out
        for r, send_sem, recv_sem in ((1, s1, r1), (2, s2, r2), (3, s3, r3)):
            copy = _chip_copy(r, src_ref, land_ref, send_sem, recv_sem)
            copy.wait_send()
            copy.wait_recv()

    hbm = pl.BlockSpec(memory_space=pltpu.HBM)
    sem_spec = pl.BlockSpec(memory_space=pltpu.SEMAPHORE)
    return pl.pallas_call(
        body, name="grad_exchange_chips_wait",
        out_shape=(pltpu.HBM(src_thru.shape, src_thru.dtype), pltpu.HBM(land_thru.shape, land_thru.dtype)),
        in_specs=(hbm, hbm) + (sem_spec,) * 6 + (pl.BlockSpec(memory_space=pl.ANY),), out_specs=(hbm, hbm),
        input_output_aliases={0: 0, 1: 1},
        compiler_params=pltpu.CompilerParams(has_side_effects=pltpu.SideEffectType.DATAFLOW_SIDE_EFFECTING),
    )(src_thru, land_thru, s1, s2, s3, r1, r2, r3, after)


class _Hosted:
    def __init__(self, comms):
        self.comms = list(comms)
        self.inputs = [a for cm in self.comms for a in cm.inputs]
        self.out_shape = [s for cm in self.comms for s in cm.out_shape]
        self.scratch = [s for cm in self.comms for s in cm.scratch]
        self.in_specs = [ANY] * len(self.inputs)
        self.out_specs = [ANY] * len(self.out_shape)

    def split(self, refs, n_in, n_out, n_scratch):
        ni, no = len(self.inputs), len(self.out_shape)
        ins, rest = refs[:n_in], refs[n_in:]
        c_ins, rest = rest[:ni], rest[ni:]
        outs, rest = rest[:n_out], rest[n_out:]
        c_outs, rest = rest[:no], rest[no:]
        scratch, c_sems = rest[:n_scratch], rest[n_scratch:]
        phases = []
        for cm in self.comms:
            a, b, s = len(cm.inputs), len(cm.out_shape), len(cm.scratch)
            phases.append(cm.bind(c_ins[:a], c_outs[:b], c_sems[:s]))
            c_ins, c_outs, c_sems = c_ins[a:], c_outs[b:], c_sems[s:]
        return ins, outs, scratch, phases


def _before_step(phases, step, n_steps):
    if not phases:
        return

    @pl.when(step == 0)
    def _():
        for start, _, _ in phases:
            start()

    @pl.when(step == n_steps // 2)
    def _():
        for _, relay, _ in phases:
            relay()


def _after_step(phases, step, n_steps):
    if not phases:
        return

    @pl.when(step == n_steps - 1)
    def _():
        for _, _, finish in phases:
            finish()


def pair_sum(part, got, owners, name):
    m, n = got.shape[1:]

    def body(own_ref, mine_ref, got_ref, out_ref):
        del own_ref
        out_ref[...] = (mine_ref[...].astype(F32) + got_ref[...].astype(F32)).astype(out_ref.dtype)

    return pl.pallas_call(
        body, name=name,
        grid_spec=pltpu.PrefetchScalarGridSpec(
            num_scalar_prefetch=1, grid=(4,),
            in_specs=[pl.BlockSpec((m, n), lambda r, own: (own[r], 0)),
                      pl.BlockSpec((None, m, n), lambda r, own: (r, 0, 0))],
            out_specs=pl.BlockSpec((None, m, n), lambda r, own: (r, 0, 0))),
        out_shape=jax.ShapeDtypeStruct((4, m, n), got.dtype),
        compiler_params=_cparams(("arbitrary",)),
    )(owners, part, got)


def in_proj_gather(x, pos_col, freq, sign, g_pre, wt_shard, bias):
    s, d = x.shape
    tm = _tile(s, (512, 256, 128))
    nt = s // tm
    tc = _tile(s, (256, 128))
    nc = s // tc
    m = wt_shard.shape[0]
    chip_rows = 2 * m
    win = -(-chip_rows // LANES) * LANES
    pad = win - chip_rows
    pass_relation = (0, 2, 1, 3)

    def body(x_hbm, pos_hbm, freq_ref, sign_ref, g_ref, b_ref, shard_ref,
             proj_hbm, h_hbm, rope_hbm, wt_ref,
             w_vmem, h_vmem, stage, xbuf, posbuf, ropebuf,
             send_sems, recv_sems, local_sems, in_sems, out_sems, proj_sems):
        p, i = pl.program_id(0), pl.program_id(1)
        xx, yy, cc = _my_place()
        me, sibling = (xx, yy, cc), (xx, yy, 1 - cc)
        chips = [_chip_of(xx, yy, r) for r in (1, 2, 3)]

        def rows(px, py, pc):
            return wt_ref.at[pl.ds(pl.multiple_of(_index(px, py, pc) * m, 16), m), :]

        def copy(k, block, to, src=None):
            return pltpu.make_async_remote_copy(
                src_ref=rows(*block) if src is None else src, dst_ref=rows(*block),
                send_sem=send_sems.at[k], recv_sem=recv_sems.at[k], device_id=to, device_id_type=MESH)

        def mine():
            return pltpu.make_async_copy(shard_ref, rows(*me), local_sems.at[0])

        def to_sibling():
            return copy(0, me, sibling, src=shard_ref)

        def to_chip(j):
            return copy(1 + j, me, (*chips[j], cc), src=shard_ref)

        def relay(j):
            copy(1 + j, (*chips[j], cc), me).wait_recv()
            copy(4 + j, (*chips[j], cc), sibling).start()

        def relayed(j):
            copy(4 + j, (*chips[j], 1 - cc), me).wait_recv()

        def window(k):
            cx, cy = _chip_of(xx, yy, pass_relation[k])
            return chip_rows * (2 * cx + cy) - pad * cy, pad * cy

        def load_window(k):
            start, own = window(k)
            if k % 2 == 0:
                src = wt_ref.at[pl.ds(pl.multiple_of(start + own, 16), chip_rows), :]
                dst = w_vmem.at[k, pl.ds(pl.multiple_of(own, 16), chip_rows), :]
            else:
                src, dst = wt_ref.at[pl.ds(pl.multiple_of(start, LANES), win), :], w_vmem.at[k]
            load = pltpu.make_async_copy(src, dst, local_sems.at[1])
            load.start()
            load.wait()

        def piece(ref, c):
            return ref.at[pl.ds(c * tc, tc), :]

        def fetch(c):
            return (pltpu.make_async_copy(piece(x_hbm, c), xbuf.at[c % 2], in_sems.at[c % 2]),
                    pltpu.make_async_copy(piece(pos_hbm, c), posbuf.at[c % 2], in_sems.at[2 + c % 2]))

        def put(c):
            return (pltpu.make_async_copy(piece(h_vmem, c), piece(h_hbm, c), out_sems.at[c % 2]),
                    pltpu.make_async_copy(ropebuf.at[c % 2], piece(rope_hbm, c), out_sems.at[2 + c % 2]))

        def prologue():
            for cp in fetch(0):
                cp.start()
            for c in range(nc):
                if c + 1 < nc:
                    for cp in fetch(c + 1):
                        cp.start()
                for cp in fetch(c):
                    cp.wait()
                if c >= 2:
                    for cp in put(c - 2):
                        cp.wait()
                xv = xbuf[c % 2]
                r = lax.rsqrt(jnp.mean(xv * xv, axis=-1, keepdims=True) + EPS)
                h_vmem[c * tc:(c + 1) * tc, :] = (xv * r * g_ref[...]).astype(BF16)
                ang = posbuf[c % 2].astype(F32) * freq_ref[...]
                ropebuf[c % 2, :, :LANES] = jnp.cos(ang)
                ropebuf[c % 2, :, LANES:] = jnp.sin(ang) * sign_ref[...]
                for cp in put(c):
                    cp.start()
            for c in range(max(nc - 2, 0), nc):
                for cp in put(c):
                    cp.wait()

        @pl.when(jnp.logical_and(p == 0, i == 0))
        def _():
            mine().start()
            to_sibling().start()
            to_chip(1).start()
            to_chip(0).start()
            for k in (0, 2):
                w_vmem[k, :pad, :] = jnp.zeros((pad, d), BF16)
                w_vmem[k, chip_rows:, :] = jnp.zeros((pad, d), BF16)
            prologue()
            copy(0, sibling, me).wait_recv()
            mine().wait()
            load_window(0)

        @pl.when(jnp.logical_and(p == 1, i == 0))
        def _():
            relay(1)
            relay(0)
            relayed(1)
            to_chip(1).wait_send()
            to_chip(0).wait_send()
            to_chip(2).start()
            load_window(1)

        @pl.when(jnp.logical_and(p == 2, i == 0))
        def _():
            relayed(0)
            load_window(2)

        @pl.when(jnp.logical_and(p == 3, i == 0))
        def _():
            relayed(2)
            load_window(3)

        step = p * nt + i
        slot = step % 2

        def put_proj(start):
            dst = proj_hbm.at[pl.ds(pl.multiple_of(i * tm, tm), tm), pl.ds(pl.multiple_of(start, LANES), win)]
            return pltpu.make_async_copy(stage.at[slot], dst, proj_sems.at[slot])

        @pl.when(step >= 2)
        def _():
            put_proj(0).wait()

        for k in range(4):
            @pl.when(p == k)
            def _():
                start, _ = window(k)
                hb = h_vmem[pl.ds(pl.multiple_of(i * tm, tm), tm), :]
                bias_win = b_ref[:, pl.ds(pl.multiple_of(start, LANES), win)]
                stage[slot] = (_dot(hb, w_vmem[k], 1, 1) + bias_win).astype(BF16)
                put_proj(start).start()

        @pl.when(jnp.logical_and(p == 2, i == nt - 1))
        def _():
            relay(2)

        @pl.when(jnp.logical_and(p == 3, i == nt - 1))
        def _():
            to_sibling().wait_send()
            to_chip(2).wait_send()
            for j in range(3):
                copy(4 + j, (*chips[j], cc), sibling).wait_send()
            put_proj(0).wait()
            if 4 * nt >= 2:
                pltpu.make_async_copy(stage.at[1 - slot], proj_hbm.at[pl.ds(0, tm), pl.ds(0, win)],
                                      proj_sems.at[1 - slot]).wait()

    const = lambda p, i: (0, 0)
    return pl.pallas_call(
        body, name="in_proj_gather", grid=(4, nt),
        in_specs=[ANY, ANY,
                  pl.BlockSpec((1, LANES), const),
                  pl.BlockSpec((1, LANES), const),
                  pl.BlockSpec((1, d), const),
                  pl.BlockSpec((1, D_IN), const),
                  ANY],
        out_specs=[ANY, ANY, ANY, ANY],
        scratch_shapes=[pltpu.VMEM((4, win, d), BF16), pltpu.VMEM((s, d), BF16), pltpu.VMEM((2, tm, win), BF16),
                        pltpu.VMEM((2, tc, d), F32), pltpu.VMEM((2, tc, 1), jnp.int32),
                        pltpu.VMEM((2, tc, 2 * LANES), F32),
                        pltpu.SemaphoreType.DMA((7,)), pltpu.SemaphoreType.DMA((7,)),
                        pltpu.SemaphoreType.DMA((2,)), pltpu.SemaphoreType.DMA((4,)),
                        pltpu.SemaphoreType.DMA((4,)), pltpu.SemaphoreType.DMA((2,))],
        out_shape=[jax.ShapeDtypeStruct((s, D_IN), BF16), jax.ShapeDtypeStruct((s, d), BF16),
                   jax.ShapeDtypeStruct((s, 2 * LANES), F32), jax.ShapeDtypeStruct((D_IN, d), BF16)],
        compiler_params=pltpu.CompilerParams(dimension_semantics=("arbitrary", "arbitrary"),
                                             vmem_limit_bytes=IN_PROJ_VMEM_LIMIT),
    )(x, pos_col, freq, sign, g_pre, bias, wt_shard)


def out_proj_loss(cat, w_out, x, target, g_post):
    s, d = x.shape
    tm = _tile(s, (256, 128))

    def body(cat_ref, w_ref, x_ref, t_ref, g_ref, dy_ref, dout_ref, dg_ref, loss_ref):
        @pl.when(pl.program_id(0) == 0)
        def _():
            dg_ref[...] = jnp.zeros_like(dg_ref)
            loss_ref[...] = jnp.zeros_like(loss_ref)

        g = g_ref[...]
        ys = [_dot(cat_ref[c0:c0 + CHUNK, :], w_ref[...], 1, 0) for c0 in range(0, tm, CHUNK)]
        for c0 in range(0, tm, CHUNK):
            rows = slice(c0, c0 + CHUNK)
            yv = ys[c0 // CHUNK]
            r = lax.rsqrt(jnp.mean(yv * yv, axis=-1, keepdims=True) + EPS)
            nrm = yv * r
            err = x_ref[rows, :] + nrm * g - t_ref[rows, :]
            loss_ref[...] += 0.5 * jnp.sum(jnp.sum(err * err, axis=-1, keepdims=True), axis=0, keepdims=True) / d
            dout = err * (1.0 / d)
            dout_ref[rows, :] = dout
            dg_ref[...] += jnp.sum(dout * nrm, axis=0, keepdims=True)
            dn = dout * g
            dy = r * (dn - nrm * jnp.mean(dn * nrm, axis=-1, keepdims=True))
            dy_ref[rows, :] = dy.astype(BF16)

    return pl.pallas_call(
        body, name="out_proj_loss", grid=(s // tm,),
        in_specs=[pl.BlockSpec((tm, d), lambda i: (i, 0)),
                  pl.BlockSpec((d, d), lambda i: (0, 0)),
                  pl.BlockSpec((tm, d), lambda i: (i, 0)),
                  pl.BlockSpec((tm, d), lambda i: (i, 0)),
                  pl.BlockSpec((1, d), lambda i: (0, 0))],
        out_specs=[pl.BlockSpec((tm, d), lambda i: (i, 0)),
                   pl.BlockSpec((tm, d), lambda i: (i, 0)),
                   pl.BlockSpec((1, d), lambda i: (0, 0)),
                   pl.BlockSpec((1, LANES), lambda i: (0, 0))],
        out_shape=[jax.ShapeDtypeStruct((s, d), BF16), jax.ShapeDtypeStruct((s, d), F32),
                   jax.ShapeDtypeStruct((1, d), F32), jax.ShapeDtypeStruct((1, LANES), F32)],
        compiler_params=_cparams(("arbitrary",)),
    )(cat, w_out, x, target, g_post)


def matmul_nt(a, b, name):
    m, k = a.shape
    n = b.shape[0]
    tm = _tile(m, (512, 256, 128))

    def body(a_ref, b_ref, o_ref):
        o_ref[...] = _dot(a_ref[...], b_ref[...], 1, 1).astype(o_ref.dtype)

    return pl.pallas_call(
        body, name=name, grid=(m // tm,),
        in_specs=[pl.BlockSpec((tm, k), lambda i: (i, 0)), pl.BlockSpec((n, k), lambda i: (0, 0))],
        out_specs=pl.BlockSpec((tm, n), lambda i: (i, 0)),
        out_shape=jax.ShapeDtypeStruct((m, n), BF16),
        compiler_params=_cparams(("arbitrary",)),
    )(a, b)


def matmul_tn(a, b, tm, name, comms=()):
    k, m = a.shape
    n = b.shape[1]
    steps = m // tm
    hosted = _Hosted(comms)

    kc = _tile(k, (1024, 128))
    pieces = k // kc

    def body(*refs):
        (a_ref, b_hbm), (o_ref, cs_ref), (b_ref, b_sems), phases = hosted.split(refs, 2, 2, 2)
        step = pl.program_id(0)
        _before_step(phases, step, steps)

        def b_load(j):
            return pltpu.make_async_copy(b_hbm.at[j * kc:(j + 1) * kc, :], b_ref.at[j * kc:(j + 1) * kc, :], b_sems.at[j])

        @pl.when(step == 0)
        def _():
            for j in range(pieces):
                b_load(j).start()
            acc = None
            for j in range(pieces):
                b_load(j).wait()
                part = _dot(a_ref[j * kc:(j + 1) * kc, :], b_ref[j * kc:(j + 1) * kc, :], 0, 0)
                acc = part if acc is None else acc + part
            o_ref[...] = acc.astype(o_ref.dtype)

        @pl.when(step > 0)
        def _():
            o_ref[...] = _dot(a_ref[...], b_ref[...], 0, 0).astype(o_ref.dtype)

        rows = _tile(k, (512, 128))
        cs = jnp.zeros((1, tm), F32)
        for r0 in range(0, k, rows):
            cs = cs + jnp.sum(a_ref[r0:r0 + rows, :].astype(F32), axis=0, keepdims=True)
        cs_ref[...] = cs
        _after_step(phases, step, steps)

    return pl.pallas_call(
        body, name=name, grid=(steps,),
        in_specs=[pl.BlockSpec((k, tm), lambda i: (0, i)), ANY] + hosted.in_specs,
        out_specs=[pl.BlockSpec((tm, n), lambda i: (i, 0)), pl.BlockSpec((1, tm), lambda i: (0, i))] + hosted.out_specs,
        out_shape=[jax.ShapeDtypeStruct((m, n), BF16), jax.ShapeDtypeStruct((1, m), F32)] + hosted.out_shape,
        scratch_shapes=[pltpu.VMEM((k, n), b.dtype), pltpu.SemaphoreType.DMA((pieces,))] + hosted.scratch,
        compiler_params=_cparams(("arbitrary",)),
    )(a, b, *hosted.inputs)


def in_proj_bwd(dproj, wt, x, g_pre, dout, comms=()):
    s, d = x.shape
    tm = _tile(s, (512, 256, 128))
    steps = s // tm
    nsub = tm // CHUNK
    kchunks = [(k0, 1024) for k0 in range(0, 5120, 1024)] + [(5120, 256)]
    ksplit = len(kchunks)
    hosted = _Hosted(comms)

    def body(*refs):
        ((*dp_refs, w_hbm, x_hbm, g_ref, dout_hbm), (gx_hbm, dg_ref),
         (w_ref, w_sems, xbuf, dbuf, gbuf, in_sems, out_sems), phases) = hosted.split(refs, 4 + ksplit, 2, 7)
        step = pl.program_id(0)
        _before_step(phases, step, steps)

        def rows_of(ref, c):
            return ref.at[pl.ds(pl.multiple_of(step * tm + c * CHUNK, CHUNK), CHUNK), :]

        def fetches(c):
            return (pltpu.make_async_copy(rows_of(x_hbm, c), xbuf.at[c], in_sems.at[c]),
                    pltpu.make_async_copy(rows_of(dout_hbm, c), dbuf.at[c], in_sems.at[nsub + c]))

        def put(c):
            return pltpu.make_async_copy(gbuf.at[c % 2], rows_of(gx_hbm, c), out_sems.at[c % 2])

        for c in range(nsub):
            for cp in fetches(c):
                cp.start()

        def w_load(j):
            k0, kw = kchunks[j]
            return pltpu.make_async_copy(w_hbm.at[k0:k0 + kw, :], w_ref.at[k0:k0 + kw, :], w_sems.at[j])

        @pl.when(step == 0)
        def _():
            dg_ref[...] = jnp.zeros_like(dg_ref)
            for j in range(ksplit):
                w_load(j).start()

        dh_all = None
        for j, ((k0, kw), dp_ref) in enumerate(zip(kchunks, dp_refs)):
            @pl.when(step == 0)
            def _():
                w_load(j).wait()

            part = _dot(dp_ref[...], w_ref[k0:k0 + kw, :], 1, 0)
            dh_all = part if dh_all is None else dh_all + part
        for c in range(nsub):
            for cp in fetches(c):
                cp.wait()
            if c >= 2:
                put(c - 2).wait()
            elif c < nsub:
                @pl.when(step > 0)
                def _():
                    put(max(nsub - 2, 0) + c).wait()
            dh = dh_all[c * CHUNK:(c + 1) * CHUNK, :]
            xv = xbuf[c]
            r = lax.rsqrt(jnp.mean(xv * xv, axis=-1, keepdims=True) + EPS)
            xn = xv * r
            dg_ref[...] += jnp.sum(dh * xn, axis=0, keepdims=True)
            dn = dh * g_ref[...]
            gbuf[c % 2] = dbuf[c] + r * (dn - xn * jnp.mean(dn * xn, axis=-1, keepdims=True))
            put(c).start()
        @pl.when(step == steps - 1)
        def _():
            for c in range(max(nsub - 2, 0), nsub):
                put(c).wait()

        _after_step(phases, step, steps)

    side_in, side_out = pltpu.VMEM((nsub, CHUNK, d), F32), pltpu.VMEM((2, CHUNK, d), F32)
    row = pl.BlockSpec((1, d), lambda i: (0, 0))
    return pl.pallas_call(
        body, name="in_proj_bwd", grid=(steps,),
        in_specs=[pl.BlockSpec((tm, kw), functools.partial(lambda j, i: (i, j), k0 // kw)) for k0, kw in kchunks]
        + [ANY, ANY, row, ANY] + hosted.in_specs,
        out_specs=[ANY, row] + hosted.out_specs,
        out_shape=[jax.ShapeDtypeStruct((s, d), F32), jax.ShapeDtypeStruct((1, d), F32)] + hosted.out_shape,
        scratch_shapes=[pltpu.VMEM((D_IN, d), BF16), pltpu.SemaphoreType.DMA((ksplit,)), side_in, side_in, side_out,
                        pltpu.SemaphoreType.DMA((2 * nsub,)), pltpu.SemaphoreType.DMA((2,))] + hosted.scratch,
        compiler_params=_cparams(("arbitrary",)),
    )(*([dproj] * ksplit), wt, x, g_pre, dout, *hosted.inputs)


def _lane_iota(shape):
    return lax.broadcasted_iota(jnp.int32, shape, len(shape) - 1)


def _partner(v):
    low = (_lane_iota(v.shape) % HEAD_DIM) < (HEAD_DIM // 2)
    return jnp.where(low, pltpu.roll(v, LANES - HEAD_DIM // 2, 1), pltpu.roll(v, HEAD_DIM // 2, 1))


def _rope(v, cos, sin_signed):
    return v * cos + _partner(v) * sin_signed


def _rope_transposed(dv, cos, sin_signed):
    return dv * cos - _partner(dv) * sin_signed


def _both_halves(v, kv_head):
    keep = (_lane_iota(v.shape) >= HEAD_DIM) if kv_head else (_lane_iota(v.shape) < HEAD_DIM)
    return jnp.where(keep, v, pltpu.roll(v, HEAD_DIM, 1))


def _fold_halves(acc):
    return acc + pltpu.roll(acc, HEAD_DIM, 1)


def _by_half(a, b):
    shape = jnp.broadcast_shapes(jnp.shape(a), jnp.shape(b))
    return jnp.where(_lane_iota(shape) < HEAD_DIM, a, b)


def _stack_heads(pair):
    return jnp.concatenate([_by_half(pair, 0.0), _by_half(0.0, pair)], axis=0)


def _band_bias(has_prev):
    i = lax.broadcasted_iota(jnp.int32, (2 * CHUNK, 2 * CHUNK), 0) % CHUNK
    j = lax.broadcasted_iota(jnp.int32, (2 * CHUNK, 2 * CHUNK), 1)
    band = jnp.logical_and(j > i, j <= i + CHUNK)
    return jnp.where(jnp.logical_and(band, jnp.logical_or(j >= CHUNK, has_prev)), 0.0, NEG)


def _probs_staged(qm2s, kk2s, bias, sink_cols):
    k = range(len(qm2s))
    scs = [_dot(qm2s[i], kk2s[i], 1, 1) + bias for i in k]
    mxs = [jnp.maximum(jnp.max(scs[i], axis=-1, keepdims=True), sink_cols[i]) for i in k]
    ps = [jnp.exp(scs[i] - mxs[i]) for i in k]
    ess = [jnp.exp(sink_cols[i] - mxs[i]) for i in k]
    invs = [1.0 / (jnp.sum(ps[i], axis=-1, keepdims=True) + ess[i]) for i in k]
    return [ps[i] * invs[i] for i in k], [ess[i] * invs[i] for i in k]


def _sink_col(sinks_ref, pair):
    row = lax.broadcasted_iota(jnp.int32, (2 * CHUNK, 1), 0)
    return jnp.where(row < CHUNK, sinks_ref[2 * pair], sinks_ref[2 * pair + 1])


def _layer_norm_parts(v):
    mu = jnp.mean(v, axis=-1, keepdims=True)
    xc = v - mu
    rstd = lax.rsqrt(jnp.mean(xc * xc, axis=-1, keepdims=True) + EPS)
    return xc * rstd, rstd


def _masked_spatial(w_ref, g):
    t = lax.broadcasted_iota(jnp.int32, (CHUNK, CHUNK), 0)
    sidx = lax.broadcasted_iota(jnp.int32, (CHUNK, CHUNK), 1)
    return jnp.where(t >= sidx, w_ref[g], 0.0).astype(BF16)


def _keys_values(kv_ref, kvp_ref, rope_ref, ropep_ref):
    cos_c, sin_c = rope_ref[:, :LANES], rope_ref[:, LANES:]
    cos_p, sin_p = ropep_ref[:, :LANES], ropep_ref[:, LANES:]
    k_c = _rope(kv_ref[:, :D_KV].astype(F32), cos_c, sin_c)
    k_p = _rope(kvp_ref[:, :D_KV].astype(F32), cos_p, sin_p)
    keys = jnp.concatenate([k_p, k_c], axis=0)
    vals = jnp.concatenate([kvp_ref[:, D_KV:], kv_ref[:, D_KV:]], axis=0).astype(F32)
    return keys, vals, (cos_c, sin_c, cos_p, sin_p)


def mixer_fwd(proj, rope, ln_g, ln_b, w_sp, b_sp_rows, sinks, comms=()):
    s = proj.shape[0]
    nb = s // CHUNK
    hosted = _Hosted(comms)

    def body(sinks_ref, *refs):
        ((proj_ref, kvp_ref, rope_ref, ropep_ref, lng_ref, lnb_ref, w_ref, b_ref), (cat_ref, p_ref), _,
         phases) = hosted.split(refs, 8, 2, 0)
        n = pl.program_id(0)
        _before_step(phases, n, nb)
        xhat, _ = _layer_norm_parts(proj_ref[:, OFF_V:OFF_V + D_GMLP].astype(F32))
        vnb = (xhat * lng_ref[...] + lnb_ref[...]).astype(BF16)
        mixeds = [_dot(_masked_spatial(w_ref, g), vnb[:, g * CHUNK:(g + 1) * CHUNK], 1, 0) + b_ref[g]
                  for g in range(GROUPS)]
        for g in range(GROUPS):
            za = proj_ref[:, OFF_ZA + g * CHUNK:OFF_ZA + (g + 1) * CHUNK].astype(F32)
            u = proj_ref[:, OFF_U + g * CHUNK:OFF_U + (g + 1) * CHUNK].astype(F32)
            cat_ref[:, g * CHUNK:(g + 1) * CHUNK] = (u * mixeds[g] * (za * _sigmoid(za))).astype(BF16)
        kv_ref = proj_ref.at[:, OFF_K:OFF_K + 2 * D_KV]
        keys, vals, (cos_c, sin_c, _, _) = _keys_values(kv_ref, kvp_ref, rope_ref, ropep_ref)
        cos_q, sin_q = cos_c * SCALE, sin_c * SCALE
        bias = _band_bias(n > 0)
        kk2 = [_both_halves(keys, kvh).astype(BF16) for kvh in range(N_KV_HEADS)]
        vv2 = [_both_halves(vals, kvh).astype(BF16) for kvh in range(N_KV_HEADS)]
        pairs = range(N_PAIRS)
        qms = [_stack_heads(_rope(proj_ref[:, OFF_Q + pair * LANES:OFF_Q + (pair + 1) * LANES].astype(F32),
                                  cos_q, sin_q)).astype(BF16) for pair in pairs]
        probs, sink_probs = _probs_staged(qms, [kk2[pair // PAIRS_PER_KV] for pair in pairs], bias,
                                          [_sink_col(sinks_ref, pair) for pair in pairs])
        pbs = [p.astype(BF16) for p in probs]
        outs = [_dot(pbs[pair], vv2[pair // PAIRS_PER_KV], 1, 0) for pair in pairs]
        first_col = _lane_iota((2 * CHUNK, 2 * CHUNK)) == 0
        for pair in pairs:
            p_ref[0, pair] = jnp.where(first_col, sink_probs[pair].astype(BF16), pbs[pair])
        for pair in pairs:
            out_pair = _by_half(outs[pair][:CHUNK], outs[pair][CHUNK:])
            zb = proj_ref[:, OFF_ZB + pair * LANES:OFF_ZB + (pair + 1) * LANES].astype(F32)
            cat_ref[:, D_GMLP + pair * LANES:D_GMLP + (pair + 1) * LANES] = (
                out_pair * (zb * _sigmoid(zb))).astype(BF16)
        _after_step(phases, n, nb)

    prev = lambda n, *_: (jnp.maximum(n - 1, 0), 0)
    kv_block = OFF_K // (2 * D_KV)
    return pl.pallas_call(
        body, name="mixer_fwd",
        grid_spec=pltpu.PrefetchScalarGridSpec(
            num_scalar_prefetch=1, grid=(nb,),
            in_specs=[pl.BlockSpec((CHUNK, D_IN), lambda n, *_: (n, 0)),
                      pl.BlockSpec((CHUNK, 2 * D_KV), lambda n, *_: (jnp.maximum(n - 1, 0), kv_block)),
                      pl.BlockSpec((CHUNK, 2 * LANES), lambda n, *_: (n, 0)),
                      pl.BlockSpec((CHUNK, 2 * LANES), prev),
                      pl.BlockSpec((1, D_GMLP), lambda n, *_: (0, 0)),
                      pl.BlockSpec((1, D_GMLP), lambda n, *_: (0, 0)),
                      pl.BlockSpec((GROUPS, CHUNK, CHUNK), lambda n, *_: (0, 0, 0)),
                      pl.BlockSpec((GROUPS, CHUNK, CHUNK), lambda n, *_: (0, 0, 0))] + hosted.in_specs,
            out_specs=[pl.BlockSpec((CHUNK, D_GMLP + D_ATTN), lambda n, *_: (n, 0)),
                       pl.BlockSpec((1, N_PAIRS, 2 * CHUNK, 2 * CHUNK), lambda n, *_: (n, 0, 0, 0))]
            + hosted.out_specs,
            scratch_shapes=hosted.scratch),
        out_shape=[jax.ShapeDtypeStruct((s, D_GMLP + D_ATTN), BF16),
                   jax.ShapeDtypeStruct((nb, N_PAIRS, 2 * CHUNK, 2 * CHUNK), BF16)] + hosted.out_shape,
        compiler_params=_cparams(("arbitrary",)),
    )(sinks, proj, proj, rope, rope, ln_g, ln_b, w_sp, b_sp_rows, *hosted.inputs)


def mixer_bwd(proj, dcat, probs, rope, ln_g, ln_b, w_sp, b_sp_rows, comms=()):
    s = proj.shape[0]
    nb = s // CHUNK
    hosted = _Hosted(comms)

    def body(*refs):
        ((proj_ref, kvp_ref, dcat_ref, p_ref, rope_ref, ropep_ref, lng_ref, lnb_ref, w_ref, b_ref),
         (dproj_ref, dw_ref, db_ref, dlng_ref, dlnb_ref, dsink_ref),
         (pend_ref, pend_kv_ref, dbacc_ref), phases) = hosted.split(refs, 10, 6, 3)
        n = pl.program_id(0)
        _before_step(phases, n, nb + 1)

        @pl.when(n == 0)
        def _():
            dw_ref[...] = jnp.zeros_like(dw_ref)
            dbacc_ref[...] = jnp.zeros_like(dbacc_ref)
            dlng_ref[...] = jnp.zeros_like(dlng_ref)
            dlnb_ref[...] = jnp.zeros_like(dlnb_ref)
            dsink_ref[...] = jnp.zeros_like(dsink_ref)

        @pl.when(n > 0)
        def _():
            dproj_ref[...] = pend_ref[...]

        def flush(dkv_prev):
            @pl.when(n > 0)
            def _():
                dproj_ref[:, OFF_K:OFF_K + 2 * D_KV] = (pend_kv_ref[...] + dkv_prev).astype(BF16)

        @pl.when(n < nb)
        def _():
            kv_ref = proj_ref.at[:, OFF_K:OFF_K + 2 * D_KV]
            keys, vals, (cos_c, sin_c, cos_p, sin_p) = _keys_values(kv_ref, kvp_ref, rope_ref, ropep_ref)
            cos_q, sin_q = cos_c * SCALE, sin_c * SCALE
            first_col = _lane_iota((2 * CHUNK, 2 * CHUNK)) == 0
            lane_row = _lane_iota((1, LANES))
            dsink = jnp.zeros((1, LANES), F32)
            dk_heads, dv_heads = [], []
            for kvh in range(N_KV_HEADS):
                kk2 = _both_halves(keys, kvh).astype(BF16)
                vv2 = _both_halves(vals, kvh).astype(BF16)
                pairs = list(range(kvh * PAIRS_PER_KV, (kvh + 1) * PAIRS_PER_KV))
                k4 = range(PAIRS_PER_KV)
                qm2s = [_stack_heads(_rope(proj_ref[:, OFF_Q + pair * LANES:OFF_Q + (pair + 1) * LANES].astype(F32),
                                           cos_q, sin_q)).astype(BF16) for pair in pairs]
                kept = [p_ref[0, pair] for pair in pairs]
                pbs = [jnp.where(first_col, jnp.zeros_like(kp), kp) for kp in kept]
                ps = [pb.astype(F32) for pb in pbs]
                p_sinks = [kp[:, 0:1].astype(F32) for kp in kept]
                o2s = [_dot(pb, vv2, 1, 0) for pb in pbs]
                zbs = [proj_ref[:, OFF_ZB + pair * LANES:OFF_ZB + (pair + 1) * LANES].astype(F32) for pair in pairs]
                sgs = [_sigmoid(zb) for zb in zbs]
                dybs = [dcat_ref[:, D_GMLP + pair * LANES:D_GMLP + (pair + 1) * LANES].astype(F32) for pair in pairs]
                for i, pair in enumerate(pairs):
                    out_pair = _by_half(o2s[i][:CHUNK], o2s[i][CHUNK:])
                    pend_ref[:, OFF_ZB + pair * LANES:OFF_ZB + (pair + 1) * LANES] = (
                        dybs[i] * out_pair * (sgs[i] * (1.0 + zbs[i] * (1.0 - sgs[i])))).astype(BF16)
                dom2s = [_stack_heads(dybs[i] * (zbs[i] * sgs[i])).astype(BF16) for i in k4]
                dps = [_dot(dom2, vv2, 1, 1) for dom2 in dom2s]
                deltas = [jnp.sum(ps[i] * dps[i], axis=-1, keepdims=True) for i in k4]
                dss = [ps[i] * (dps[i] - deltas[i]) for i in k4]
                for i, pair in enumerate(pairs):
                    dsk = -(p_sinks[i] * deltas[i])
                    dsink = dsink + jnp.where(lane_row == 2 * pair,
                                              jnp.sum(dsk[:CHUNK], axis=0, keepdims=True), 0.0)
                    dsink = dsink + jnp.where(lane_row == 2 * pair + 1,
                                              jnp.sum(dsk[CHUNK:], axis=0, keepdims=True), 0.0)
                dsbs = [ds.astype(BF16) for ds in dss]
                dq2s = [_dot(dsb, kk2, 1, 0) for dsb in dsbs]
                for pair, dq2 in zip(pairs, dq2s):
                    pend_ref[:, OFF_Q + pair * LANES:OFF_Q + (pair + 1) * LANES] = _rope_transposed(
                        _by_half(dq2[:CHUNK], dq2[CHUNK:]), cos_q, sin_q).astype(BF16)
                dkks = [_dot(dsbs[i], qm2s[i], 0, 0) for i in k4]
                dvvs = [_dot(pbs[i], dom2s[i], 0, 0) for i in k4]
                dk_heads.append(_fold_halves((dkks[0] + dkks[1]) + (dkks[2] + dkks[3])))
                dv_heads.append(_fold_halves((dvvs[0] + dvvs[1]) + (dvvs[2] + dvvs[3])))
            dk_rot = _by_half(dk_heads[0], dk_heads[1])
            dv_all = _by_half(dv_heads[0], dv_heads[1])
            dk_p = _rope_transposed(dk_rot[:CHUNK], cos_p, sin_p)
            dk_c = _rope_transposed(dk_rot[CHUNK:], cos_c, sin_c)
            flush(jnp.concatenate([dk_p, dv_all[:CHUNK]], axis=1))
            dsink_ref[...] += dsink
            pend_kv_ref[...] = jnp.concatenate([dk_c, dv_all[CHUNK:]], axis=1)
            xhat, rstd = _layer_norm_parts(proj_ref[:, OFF_V:OFF_V + D_GMLP].astype(F32))
            lng = lng_ref[...]
            vnb = (xhat * lng + lnb_ref[...]).astype(BF16)
            dvn_cols = []
            for g in range(GROUPS):
                cols = slice(g * CHUNK, (g + 1) * CHUNK)
                wm = _masked_spatial(w_ref, g)
                mixed = _dot(wm, vnb[:, cols], 1, 0) + b_ref[g]
                za = proj_ref[:, OFF_ZA + g * CHUNK:OFF_ZA + (g + 1) * CHUNK].astype(F32)
                u = proj_ref[:, OFF_U + g * CHUNK:OFF_U + (g + 1) * CHUNK].astype(F32)
                dya = dcat_ref[:, cols].astype(F32)
                sg = _sigmoid(za)
                sz = za * sg
                pend_ref[:, OFF_U + g * CHUNK:OFF_U + (g + 1) * CHUNK] = (dya * mixed * sz).astype(BF16)
                pend_ref[:, OFF_ZA + g * CHUNK:OFF_ZA + (g + 1) * CHUNK] = (
                    dya * u * mixed * (sg * (1.0 + za * (1.0 - sg)))).astype(BF16)
                dmixed = dya * u * sz
                dmb = dmixed.astype(BF16)
                dbacc_ref[g] += dmixed
                dw_ref[g] += _dot(dmb, vnb[:, cols], 1, 1)
                dvn_cols.append(_dot(wm, dmb, 0, 0))
            dvn = jnp.concatenate(dvn_cols, axis=1)
            dlng_ref[...] += jnp.sum(dvn * xhat, axis=0, keepdims=True)
            dlnb_ref[...] += jnp.sum(dvn, axis=0, keepdims=True)
            dxh = dvn * lng
            dv = rstd * (dxh - jnp.mean(dxh, axis=-1, keepdims=True)
                         - xhat * jnp.mean(dxh * xhat, axis=-1, keepdims=True))
            pend_ref[:, OFF_V:OFF_V + D_GMLP] = dv.astype(BF16)

        @pl.when(n == nb)
        def _():
            flush(jnp.zeros((CHUNK, 2 * D_KV), F32))
            t = lax.broadcasted_iota(jnp.int32, (CHUNK, CHUNK), 0)
            sidx = lax.broadcasted_iota(jnp.int32, (CHUNK, CHUNK), 1)
            lane = _lane_iota((CHUNK, LANES))
            dbt = jnp.zeros((CHUNK, LANES), F32)
            for g in range(GROUPS):
                dw_ref[g] = jnp.where(t >= sidx, dw_ref[g], 0.0)
                dbt = jnp.where(lane == g, jnp.sum(dbacc_ref[g], axis=-1, keepdims=True), dbt)
            db_ref[...] = jnp.transpose(dbt)[:GROUPS, :]

        _after_step(phases, n, nb + 1)

    cur = lambda n: (jnp.minimum(n, nb - 1), 0)
    prev = lambda n: (jnp.clip(n - 1, 0, nb - 1), 0)
    kv_block = OFF_K // (2 * D_KV)
    const2 = lambda n: (0, 0)
    const3 = lambda n: (0, 0, 0)
    return pl.pallas_call(
        body, name="mixer_bwd", grid=(nb + 1,),
        in_specs=[pl.BlockSpec((CHUNK, D_IN), cur),
                  pl.BlockSpec((CHUNK, 2 * D_KV), lambda n: (jnp.clip(n - 1, 0, nb - 1), kv_block)),
                  pl.BlockSpec((CHUNK, D_GMLP + D_ATTN), cur),
                  pl.BlockSpec((1, N_PAIRS, 2 * CHUNK, 2 * CHUNK), lambda n: (jnp.minimum(n, nb - 1), 0, 0, 0)),
                  pl.BlockSpec((CHUNK, 2 * LANES), cur),
                  pl.BlockSpec((CHUNK, 2 * LANES), prev),
                  pl.BlockSpec((1, D_GMLP), const2),
                  pl.BlockSpec((1, D_GMLP), const2),
                  pl.BlockSpec((GROUPS, CHUNK, CHUNK), const3),
                  pl.BlockSpec((GROUPS, CHUNK, CHUNK), const3)] + hosted.in_specs,
        out_specs=[pl.BlockSpec((CHUNK, D_IN), lambda n: (jnp.maximum(n - 1, 0), 0)),
                   pl.BlockSpec((GROUPS, CHUNK, CHUNK), const3),
                   pl.BlockSpec((GROUPS, CHUNK), const2),
                   pl.BlockSpec((1, D_GMLP), const2),
                   pl.BlockSpec((1, D_GMLP), const2),
                   pl.BlockSpec((1, LANES), const2)] + hosted.out_specs,
        scratch_shapes=[pltpu.VMEM((CHUNK, D_IN), BF16), pltpu.VMEM((CHUNK, 2 * D_KV), F32),
                        pltpu.VMEM((GROUPS, CHUNK, CHUNK), F32)] + hosted.scratch,
        out_shape=[jax.ShapeDtypeStruct((s, D_IN), BF16),
                   jax.ShapeDtypeStruct((GROUPS, CHUNK, CHUNK), F32),
                   jax.ShapeDtypeStruct((GROUPS, CHUNK), F32),
                   jax.ShapeDtypeStruct((1, D_GMLP), F32),
                   jax.ShapeDtypeStruct((1, D_GMLP), F32),
                   jax.ShapeDtypeStruct((1, LANES), F32)] + hosted.out_shape,
        compiler_params=_cparams(("arbitrary",)),
    )(proj, proj, dcat, probs, rope, rope, ln_g, ln_b, w_sp, b_sp_rows, *hosted.inputs)


def _adamw_math(w, g, m, v):
    m = ADAM_B1 * m + (1.0 - ADAM_B1) * g
    v = ADAM_B2 * v + (1.0 - ADAM_B2) * (g * g)
    m_hat = m / (1.0 - ADAM_B1 ** ADAM_STEP)
    v_hat = v / (1.0 - ADAM_B2 ** ADAM_STEP)
    delta = -ADAM_LR * (m_hat / (jnp.sqrt(v_hat) + ADAM_EPS) + ADAM_WD * w)
    return delta, m, v


def adamw_shard(terms, w, m, v, name):
    r, c = w.shape
    tr = _tile(r, (224, 256, 128, 8))
    n_terms = len(terms)

    def body(*refs):
        w_ref, m_ref, v_ref, g_ref, d_ref, nm_ref, nv_ref = refs[n_terms:]
        g = None
        for ref, (_, slots) in zip(refs[:n_terms], terms):
            for k in range(slots):
                part = ref[k].astype(F32)
                g = part if g is None else g + part
        g_ref[...] = g
        d_ref[...], nm_ref[...], nv_ref[...] = _adamw_math(w_ref[...], g, m_ref[...], v_ref[...])

    spec = pl.BlockSpec((tr, c), lambda i: (i, 0))
    return pl.pallas_call(
        body, name=name, grid=(r // tr,),
        in_specs=[pl.BlockSpec((slots, tr, c), lambda i: (0, i, 0)) for _, slots in terms] + [spec] * 3,
        out_specs=[spec] * 4, out_shape=[jax.ShapeDtypeStruct((r, c), F32)] * 4,
        compiler_params=_cparams(("arbitrary",)),
    )(*[a for a, _ in terms], w, m, v)


def adamw_small(gathered, lane_windows, params):
    n_par = len(params)

    def body(*refs):
        g_refs = refs[:n_par + 1]
        wmv_refs = refs[n_par + 1:4 * n_par + 1]
        out_refs = refs[4 * n_par + 1:]

        def total(ref):
            acc = ref[0]
            for dev in range(1, N_DEV):
                acc = acc + ref[dev]
            return acc

        for i in range(n_par):
            w_ref, m_ref, v_ref = wmv_refs[3 * i:3 * i + 3]
            g = total(g_refs[i])
            if lane_windows[i] is not None:
                start, size = lane_windows[i]
                g = g[..., start:start + size]
            delta, new_m, new_v = _adamw_math(w_ref[...], g, m_ref[...], v_ref[...])
            for ref, val in zip(out_refs[4 * i:4 * i + 4], (g, delta, new_m, new_v)):
                ref[...] = val
        out_refs[4 * n_par][...] = total(g_refs[n_par])

    flat = [a for wmv in params for a in wmv]
    out_shape = [jax.ShapeDtypeStruct(w.shape, F32) for (w, _, _) in params for _ in range(4)]
    out_shape.append(jax.ShapeDtypeStruct(gathered[-1].shape[1:], F32))
    outs = pl.pallas_call(body, name="adamw_small", out_shape=out_shape, compiler_params=_cparams())(*gathered, *flat)
    return [tuple(outs[4 * i:4 * i + 4]) for i in range(n_par)], outs[-1]


def kernel(x, positions, g_pre, w_in, b_qkv, ln_v_g, ln_v_b, w_spatial, b_spatial, attn_sinks, w_out, g_post, loss_target, m_g_pre, m_w_in, m_b_qkv, m_ln_v_g, m_ln_v_b, m_w_spatial, m_b_spatial, m_attn_sinks, m_w_out, m_g_post, v_g_pre, v_w_in, v_b_qkv, v_ln_v_g, v_ln_v_b, v_w_spatial, v_b_spatial, v_attn_sinks, v_w_out, v_g_post):
    x2, target = x[0], loss_target[0]
    seq = x2.shape[0]
    xi, yi, ci = _my_place()

    wt_shard = w_in[0].T.astype(BF16)
    wo_shard = w_out[0].astype(BF16)
    pos_col = positions.reshape(seq, 1)
    half = HEAD_DIM // 2
    inv_freq = ROPE_THETA ** (-jnp.arange(half, dtype=F32) * (2.0 / HEAD_DIM))
    freq = jnp.tile(inv_freq, LANES // half).reshape(1, LANES)
    sign = jnp.tile(jnp.concatenate([-jnp.ones((half,), F32), jnp.ones((half,), F32)]), LANES // HEAD_DIM)
    sign = sign.reshape(1, LANES)
    bias = jnp.concatenate([jnp.zeros((1, OFF_Q), F32), b_qkv, jnp.zeros((1, D_ATTN), F32)], axis=1)
    proj, h, rope, wt = in_proj_gather(x2, pos_col, freq, sign, g_pre, wt_shard, bias)

    b_rows = jnp.broadcast_to(b_spatial[0][:, :, None], (GROUPS, CHUNK, CHUNK))
    sinks = attn_sinks[0]
    cat, probs, wo = mixer_fwd(proj, rope, ln_v_g, ln_v_b, w_spatial[0], b_rows, sinks,
                               comms=[gather_comm([wo_shard])])
    dy, dout, d_g_post, loss_part = out_proj_loss(cat, wo, x2, target, g_post)

    dcat = matmul_nt(dy, wo, "out_proj_bwd")
    d_wo, _ = matmul_tn(cat, dy, 512, "w_out_grad")
    dproj, d_w_sp, d_b_sp, d_ln_g, d_ln_b, d_sinks, parts_wo = mixer_bwd(
        proj, dcat, probs, rope, ln_v_g, ln_v_b, w_spatial[0], b_rows, comms=[scatter_comm([d_wo])])
    small_parts = [d_ln_g, d_ln_b, d_w_sp, d_b_sp, d_sinks, d_g_post, loss_part]
    d_wt, colsum, *landed = matmul_tn(dproj, h, 768, "w_in_grad", comms=[gather_comm(small_parts, stack=True)])

    owners = jnp.stack([4 * cx + 2 * cy + ci for cx, cy in (_chip_of(xi, yi, r) for r in range(4))]).astype(jnp.int32)
    (got_wt,) = run_comm(pair_comm([d_wt]), "grad_exchange_pair")
    sum_wt = pair_sum(d_wt, got_wt, owners, "grad_pair_sum_w_in")
    started = chips_exchange_start(sum_wt)
    token = started[-1]
    grad_x, d_g_pre = in_proj_bwd(dproj, wt, x2, g_pre + token[:1, :1], dout)
    sum_wt, far_wt = chips_exchange_wait(started, d_g_pre)
    late = run_comm(gather_comm([d_g_pre, colsum], stack=True, direct=True), "allgather_late_grads")
    gathered = late + landed
    windows = [None, (OFF_Q, D_QKV), None, None, None, None, (0, N_Q_HEADS), None]
    small = [(g_pre, m_g_pre, v_g_pre), (b_qkv, m_b_qkv, v_b_qkv), (ln_v_g, m_ln_v_g, v_ln_v_g),
             (ln_v_b, m_ln_v_b, v_ln_v_b), (w_spatial[0], m_w_spatial[0], v_w_spatial[0]),
             (b_spatial[0], m_b_spatial[0], v_b_spatial[0]), (attn_sinks, m_attn_sinks, v_attn_sinks),
             (g_post, m_g_post, v_g_post)]
    small_out, loss_row = adamw_small(gathered, windows, small)
    lead = [False, False, False, False, True, True, False, False]
    small_out = [tuple(a[None] if ld else a for a in leaf) for leaf, ld in zip(small_out, lead)]

    wt_out = adamw_shard([(sum_wt, 1), (far_wt, 3)], w_in[0].T, m_w_in[0].T, v_w_in[0].T, "adamw_w_in")
    wo_out = adamw_shard([(parts_wo, N_DEV)], w_out[0], m_w_out[0], v_w_out[0], "adamw_w_out")

    def leaves(k):
        gp, bq, lg, lb, ws, bs, sk, gpo = (leaf[k] for leaf in small_out)
        return [gp, wt_out[k].T[None], bq, lg, lb, ws, bs, sk, wo_out[k][None], gpo]

    return (loss_row[0, 0], grad_x[None], *leaves(0), *leaves(1), *leaves(2), *leaves(3))
```

```python
import functools

import jax
import jax.numpy as jnp
from jax import lax
from jax.experimental import pallas as pl
from jax.experimental.pallas import tpu as pltpu

F32 = jnp.float32
BF16 = jnp.bfloat16

D_MODEL = 2048
D_GMLP = 1024
D_ATTN = 1024
CHUNK = 128
GROUPS = 8
HEAD_DIM = 64
N_Q_HEADS = 16
N_KV_HEADS = 2
D_KV = N_KV_HEADS * HEAD_DIM
D_IN = 3 * D_GMLP + D_ATTN + 2 * D_KV + D_ATTN
OFF_U, OFF_V, OFF_ZA = 0, D_GMLP, 2 * D_GMLP
OFF_Q = 3 * D_GMLP
OFF_K = OFF_Q + D_ATTN
OFF_VA = OFF_K + D_KV
OFF_ZB = OFF_VA + D_KV
D_QKV = D_ATTN + 2 * D_KV
ROPE_THETA = 10000.0
EPS = 1e-6
SCALE = HEAD_DIM ** -0.5
NEG = -1e30
N_PAIRS = N_Q_HEADS // 2
PAIRS_PER_KV = N_PAIRS // N_KV_HEADS

ADAM_LR = 0.001
ADAM_B1 = 0.9
ADAM_B2 = 0.999
ADAM_EPS = 1e-08
ADAM_WD = 0.01
ADAM_STEP = 10

N_DEV = 8
LANES = 128
VMEM_LIMIT = 56 * 1024 * 1024
IN_PROJ_VMEM_LIMIT = 61 * 1024 * 1024

MESH = pl.DeviceIdType.MESH
ANY = pl.BlockSpec(memory_space=pl.ANY)


def _cparams(sem=None):
    return pltpu.CompilerParams(dimension_semantics=sem, vmem_limit_bytes=VMEM_LIMIT)


def _tile(n, prefs):
    for t in prefs:
        if n % t == 0:
            return t
    return n


def _sigmoid(z):
    return 1.0 / (1.0 + jnp.exp(-z))


def _dot(a, b, ca, cb):
    return lax.dot_general(a, b, (((ca,), (cb,)), ((), ())), preferred_element_type=F32)


def _my_place():
    return lax.axis_index("x"), lax.axis_index("y"), lax.axis_index("c")


def _chip_of(x, y, r):
    return (x ^ (r & 1), y ^ (r >> 1))


def _peer(x, y, c, k):
    return (x ^ (k >> 2), y ^ ((k >> 1) & 1), c ^ (k & 1))


def _index(px, py, pc):
    return 4 * px + 2 * py + pc


class _Comm:
    def __init__(self, inputs, out_shape, scratch, bind):
        self.inputs, self.out_shape, self.scratch, self.bind = list(inputs), list(out_shape), list(scratch), bind


def gather_comm(shards, stack=False, direct=False):
    n_arr = len(shards)

    def bind(ins, outs, sems):
        send_sems, recv_sems, local_sems = sems
        x, y, c = _my_place()
        me, sibling = (x, y, c), (x, y, 1 - c)
        chips = [_chip_of(x, y, r) for r in (1, 2, 3)]

        def rows(a, px, py, pc):
            d = _index(px, py, pc)
            if stack:
                return outs[a].at[d]
            m = shards[a].shape[0]
            return outs[a].at[pl.ds(pl.multiple_of(d * m, 8), m), :]

        def copy(a, k, block, to, src=None):
            return pltpu.make_async_remote_copy(
                src_ref=rows(a, *block) if src is None else src, dst_ref=rows(a, *block),
                send_sem=send_sems.at[a * 7 + k], recv_sem=recv_sems.at[a * 7 + k],
                device_id=to, device_id_type=MESH)

        def mine(a):
            return pltpu.make_async_copy(ins[a], rows(a, *me), local_sems.at[a])

        def own_sends(a):
            if direct:
                return [copy(a, k - 1, me, _peer(x, y, c, k), src=ins[a]) for k in range(1, 8)]
            return ([copy(a, 0, me, sibling, src=ins[a])]
                    + [copy(a, 1 + j, me, (*chip, c), src=ins[a]) for j, chip in enumerate(chips)])

        def start():
            for a in range(n_arr):
                mine(a).start()
                for cp in own_sends(a):
                    cp.start()

        def relay():
            if direct:
                return
            for j, chip in enumerate(chips):
                for a in range(n_arr):
                    copy(a, 1 + j, (*chip, c), me).wait_recv()
                    copy(a, 4 + j, (*chip, c), sibling).start()

        def finish():
            for a in range(n_arr):
                if direct:
                    for k in range(1, 8):
                        copy(a, k - 1, _peer(x, y, c, k), me).wait_recv()
                else:
                    copy(a, 0, sibling, me).wait_recv()
                    for j, chip in enumerate(chips):
                        copy(a, 4 + j, (*chip, 1 - c), me).wait_recv()
                        copy(a, 4 + j, (*chip, c), sibling).wait_send()
                for cp in own_sends(a):
                    cp.wait_send()
                mine(a).wait()

        return start, relay, finish

    def gathered(s):
        return (N_DEV, *s.shape) if stack else (N_DEV * s.shape[0], s.shape[1])

    return _Comm(shards, [jax.ShapeDtypeStruct(gathered(s), s.dtype) for s in shards],
                 [pltpu.SemaphoreType.DMA((7 * n_arr,)), pltpu.SemaphoreType.DMA((7 * n_arr,)),
                  pltpu.SemaphoreType.DMA((n_arr,))], bind)


def scatter_comm(parts):
    n_arr = len(parts)

    def bind(ins, outs, sems):
        send_sems, recv_sems, local_sems = sems
        x, y, c = _my_place()
        my_index = _index(x, y, c)

        def block(a, d):
            m = parts[a].shape[0] // N_DEV
            return ins[a].at[pl.ds(pl.multiple_of(d * m, 16), m), :]

        def copy(a, k, slot):
            peer = _peer(x, y, c, k)
            return pltpu.make_async_remote_copy(
                src_ref=block(a, _index(*peer)), dst_ref=outs[a].at[slot],
                send_sem=send_sems.at[a * 7 + k - 1], recv_sem=recv_sems.at[a * 7 + k - 1],
                device_id=peer, device_id_type=MESH)

        def mine(a):
            return pltpu.make_async_copy(block(a, my_index), outs[a].at[my_index], local_sems.at[a])

        def start():
            for a in range(n_arr):
                mine(a).start()
                for k in range(1, 8):
                    copy(a, k, my_index).start()

        def finish():
            for a in range(n_arr):
                for k in range(1, 8):
                    copy(a, k, _index(*_peer(x, y, c, k))).wait_recv()
                    copy(a, k, my_index).wait_send()
                mine(a).wait()

        return start, (lambda: None), finish

    return _Comm(parts, [jax.ShapeDtypeStruct((N_DEV, p.shape[0] // N_DEV, p.shape[1]), p.dtype) for p in parts],
                 [pltpu.SemaphoreType.DMA((7 * n_arr,)), pltpu.SemaphoreType.DMA((7 * n_arr,)),
                  pltpu.SemaphoreType.DMA((n_arr,))], bind)


def run_comm(comm, name):
    n_in, n_out = len(comm.inputs), len(comm.out_shape)

    def body(*refs):
        start, relay, finish = comm.bind(refs[:n_in], refs[n_in:n_in + n_out], refs[n_in + n_out:])
        start()
        relay()
        finish()

    outs = pl.pallas_call(body, name=name, out_shape=comm.out_shape, in_specs=[ANY] * n_in,
                          out_specs=[ANY] * n_out, scratch_shapes=comm.scratch)(*comm.inputs)
    return list(outs)


def _chip_copy(r, src_ref, land_ref, send_sem, recv_sem):
    x, y, c = _my_place()
    return pltpu.make_async_remote_copy(src_ref=src_ref.at[r], dst_ref=land_ref.at[r - 1], send_sem=send_sem,
                                        recv_sem=recv_sem, device_id=(*_chip_of(x, y, r), c), device_id_type=MESH)


def chips_exchange_start(sums):
    def body(src_ref, land_ref, s1, s2, s3, r1, r2, r3, src_thru, land_thru, token):
        del src_thru, land_thru
        for r, send_sem, recv_sem in ((1, s1, r1), (2, s2, r2), (3, s3, r3)):
            _chip_copy(r, src_ref, land_ref, send_sem, recv_sem).start()
        token[...] = jnp.zeros_like(token)

    land = lax.empty((3,) + sums.shape[1:], sums.dtype)
    sem = pltpu.SemaphoreType.DMA(())
    hbm = pl.BlockSpec(memory_space=pltpu.HBM)
    sem_spec = pl.BlockSpec(memory_space=pltpu.SEMAPHORE)
    return pl.pallas_call(
        body, name="grad_exchange_chips_start",
        out_shape=(sem,) * 6 + (pltpu.HBM(sums.shape, sums.dtype), pltpu.HBM(land.shape, land.dtype),
                                jax.ShapeDtypeStruct((8, LANES), F32)),
        in_specs=(hbm, hbm), out_specs=(sem_spec,) * 6 + (hbm, hbm, pl.BlockSpec(memory_space=pltpu.VMEM)),
        input_output_aliases={0: 6, 1: 7},
        compiler_params=pltpu.CompilerParams(has_side_effects=pltpu.SideEffectType.DATAFLOW_SIDE_EFFECTING),
    )(pltpu.with_memory_space_constraint(sums, pltpu.HBM), pltpu.with_memory_space_constraint(land, pltpu.HBM))


def chips_exchange_wait(started, after):
    s1, s2, s3, r1, r2, r3, src_thru, land_thru, _ = started

    def body(src_ref, land_ref, s1, s2, s3, r1, r2, r3, after_ref, src_out, land_out):
        del after_ref, src_out, land_out
        for r, send_sem, recv_sem in ((1, s1, r1), (2, s2, r2), (3, s3, r3)):
            copy = _chip_copy(r, src_ref, land_ref, send_sem, recv_sem)
            copy.wait_send()
            copy.wait_recv()

    hbm = pl.BlockSpec(memory_space=pltpu.HBM)
    sem_spec = pl.BlockSpec(memory_space=pltpu.SEMAPHORE)
    return pl.pallas_call(
        body, name="grad_exchange_chips_wait",
        out_shape=(pltpu.HBM(src_thru.shape, src_thru.dtype), pltpu.HBM(land_thru.shape, land_thru.dtype)),
        in_specs=(hbm, hbm) + (sem_spec,) * 6 + (pl.BlockSpec(memory_space=pl.ANY),), out_specs=(hbm, hbm),
        input_output_aliases={0: 0, 1: 1},
        compiler_params=pltpu.CompilerParams(has_side_effects=pltpu.SideEffectType.DATAFLOW_SIDE_EFFECTING),
    )(src_thru, land_thru, s1, s2, s3, r1, r2, r3, after)


class _Hosted:
    def __init__(self, comms):
        self.comms = list(comms)
        self.inputs = [a for cm in self.comms for a in cm.inputs]
        self.out_shape = [s for cm in self.comms for s in cm.out_shape]
        self.scratch = [s for cm in self.comms for s in cm.scratch]
        self.in_specs = [ANY] * len(self.inputs)
        self.out_specs = [ANY] * len(self.out_shape)

    def split(self, refs, n_in, n_out, n_scratch):
        ni, no = len(self.inputs), len(self.out_shape)
        ins, rest = refs[:n_in], refs[n_in:]
        c_ins, rest = rest[:ni], rest[ni:]
        outs, rest = rest[:n_out], rest[n_out:]
        c_outs, rest = rest[:no], rest[no:]
        scratch, c_sems = rest[:n_scratch], rest[n_scratch:]
        phases = []
        for cm in self.comms:
            a, b, s = len(cm.inputs), len(cm.out_shape), len(cm.scratch)
            phases.append(cm.bind(c_ins[:a], c_outs[:b], c_sems[:s]))
            c_ins, c_outs, c_sems = c_ins[a:], c_outs[b:], c_sems[s:]
        return ins, outs, scratch, phases


def _before_step(phases, step, n_steps):
    if not phases:
        return

    @pl.when(step == 0)
    def _():
        for start, _, _ in phases:
            start()

    @pl.when(step == n_steps // 2)
    def _():
        for _, relay, _ in phases:
            relay()


def _after_step(phases, step, n_steps):
    if not phases:
        return

    @pl.when(step == n_steps - 1)
    def _():
        for _, _, finish in phases:
            finish()


def pair_exchange_sum(part, name):
    m, n = part.shape[0] // N_DEV, part.shape[1]

    def body(part_ref, out_ref, got, mine, summed, send_sems, recv_sems, in_sems, out_sems):
        x, y, c = _my_place()

        def rows(r, core):
            owner = _index(*_chip_of(x, y, r), core)
            return part_ref.at[pl.ds(pl.multiple_of(owner * m, 16), m), :]

        def to_sibling(r):
            return pltpu.make_async_remote_copy(src_ref=rows(r, 1 - c), dst_ref=got.at[r], send_sem=send_sems.at[r],
                                                recv_sem=recv_sems.at[r], device_id=(x, y, 1 - c), device_id_type=MESH)

        def fetch(r):
            return pltpu.make_async_copy(rows(r, c), mine.at[r % 2], in_sems.at[r % 2])

        def put(r):
            return pltpu.make_async_copy(summed.at[r % 2], out_ref.at[r], out_sems.at[r % 2])

        for r in range(4):
            to_sibling(r).start()
        fetch(0).start()
        for r in range(4):
            if r + 1 < 4:
                fetch(r + 1).start()
            fetch(r).wait()
            to_sibling(r).wait_recv()
            if r >= 2:
                put(r - 2).wait()
            summed[r % 2] = (mine[r % 2].astype(F32) + got[r].astype(F32)).astype(summed.dtype)
            put(r).start()
        for r in (2, 3):
            put(r).wait()
        for r in range(4):
            to_sibling(r).wait_send()

    return pl.pallas_call(
        body, name=name, out_shape=jax.ShapeDtypeStruct((4, m, n), part.dtype),
        in_specs=[ANY], out_specs=ANY,
        scratch_shapes=[pltpu.VMEM((4, m, n), part.dtype), pltpu.VMEM((2, m, n), part.dtype),
                        pltpu.VMEM((2, m, n), part.dtype), pltpu.SemaphoreType.DMA((4,)),
                        pltpu.SemaphoreType.DMA((4,)), pltpu.SemaphoreType.DMA((2,)), pltpu.SemaphoreType.DMA((2,))],
        compiler_params=_cparams(),
    )(part)


def in_proj_gather(x, pos_col, freq, sign, g_pre, wt_shard, bias):
    s, d = x.shape
    tm = _tile(s, (512, 256, 128))
    nt = s // tm
    tc = _tile(s, (256, 128))
    nc = s // tc
    m = wt_shard.shape[0]
    half = D_IN // 2
    xi = lax.axis_index("x")
    order = jnp.stack([xi, 1 - xi]).astype(jnp.int32)

    def body(order_ref, x_hbm, pos_hbm, freq_ref, sign_ref, g_ref, b_ref, shard_ref,
             proj_ref, h_hbm, rope_hbm, wt_ref,
             w_vmem, h_vmem, xbuf, posbuf, ropebuf, send_sems, recv_sems, local_sems, in_sems, out_sems):
        del order_ref
        p, i = pl.program_id(0), pl.program_id(1)
        xx, yy, cc = _my_place()
        me, sibling = (xx, yy, cc), (xx, yy, 1 - cc)
        chips = [_chip_of(xx, yy, r) for r in (1, 2, 3)]

        def rows(px, py, pc):
            return wt_ref.at[pl.ds(pl.multiple_of(_index(px, py, pc) * m, 16), m), :]

        def copy(k, block, to, src=None):
            return pltpu.make_async_remote_copy(
                src_ref=rows(*block) if src is None else src, dst_ref=rows(*block),
                send_sem=send_sems.at[k], recv_sem=recv_sems.at[k], device_id=to, device_id_type=MESH)

        def mine():
            return pltpu.make_async_copy(shard_ref, rows(*me), local_sems.at[0])

        def to_sibling():
            return copy(0, me, sibling, src=shard_ref)

        def to_chip(j):
            return copy(1 + j, me, (*chips[j], cc), src=shard_ref)

        def relay(j):
            copy(1 + j, (*chips[j], cc), me).wait_recv()
            copy(4 + j, (*chips[j], cc), sibling).start()

        def relayed(j):
            copy(4 + j, (*chips[j], 1 - cc), me).wait_recv()

        def load_half(which, slot):
            rows_of_half = wt_ref.at[pl.ds(pl.multiple_of(which * half, 16), half), :]
            load = pltpu.make_async_copy(rows_of_half, w_vmem.at[slot], local_sems.at[1 + slot])
            load.start()
            load.wait()

        def piece(ref, c):
            return ref.at[pl.ds(c * tc, tc), :]

        def fetch(c):
            return (pltpu.make_async_copy(piece(x_hbm, c), xbuf.at[c % 2], in_sems.at[c % 2]),
                    pltpu.make_async_copy(piece(pos_hbm, c), posbuf.at[c % 2], in_sems.at[2 + c % 2]))

        def put(c):
            return (pltpu.make_async_copy(piece(h_vmem, c), piece(h_hbm, c), out_sems.at[c % 2]),
                    pltpu.make_async_copy(ropebuf.at[c % 2], piece(rope_hbm, c), out_sems.at[2 + c % 2]))

        def prologue():
            for cp in fetch(0):
                cp.start()
            for c in range(nc):
                if c + 1 < nc:
                    for cp in fetch(c + 1):
                        cp.start()
                for cp in fetch(c):
                    cp.wait()
                if c >= 2:
                    for cp in put(c - 2):
                        cp.wait()
                xv = xbuf[c % 2]
                r = lax.rsqrt(jnp.mean(xv * xv, axis=-1, keepdims=True) + EPS)
                h_vmem[c * tc:(c + 1) * tc, :] = (xv * r * g_ref[...]).astype(BF16)
                ang = posbuf[c % 2].astype(F32) * freq_ref[...]
                ropebuf[c % 2, :, :LANES] = jnp.cos(ang)
                ropebuf[c % 2, :, LANES:] = jnp.sin(ang) * sign_ref[...]
                for cp in put(c):
                    cp.start()
            for c in range(max(nc - 2, 0), nc):
                for cp in put(c):
                    cp.wait()

        @pl.when(jnp.logical_and(p == 0, i == 0))
        def _():
            mine().start()
            to_sibling().start()
            to_chip(1).start()
            to_chip(0).start()
            prologue()
            copy(0, sibling, me).wait_recv()
            relay(1)
            relayed(1)
            mine().wait()
            to_chip(1).wait_send()
            to_chip(0).wait_send()
            to_chip(2).start()
            load_half(xx, 0)

        @pl.when(jnp.logical_and(p == 1, i == 0))
        def _():
            relayed(0)
            relayed(2)
            load_half(1 - xx, 1)

        def project(slot):
            hb = h_vmem[pl.ds(pl.multiple_of(i * tm, tm), tm), :]
            proj_ref[...] = (_dot(hb, w_vmem[slot], 1, 1) + b_ref[...]).astype(BF16)

        @pl.when(p == 0)
        def _():
            project(0)

        @pl.when(p == 1)
        def _():
            project(1)

        @pl.when(jnp.logical_and(p == 0, i == 1))
        def _():
            relay(0)

        @pl.when(jnp.logical_and(p == 0, i == nt - 1))
        def _():
            relay(2)

        @pl.when(jnp.logical_and(p == 1, i == nt - 1))
        def _():
            to_sibling().wait_send()
            to_chip(2).wait_send()
            for j in range(3):
                copy(4 + j, (*chips[j], cc), sibling).wait_send()

    const = lambda p, i, o: (0, 0)
    return pl.pallas_call(
        body, name="in_proj_gather",
        grid_spec=pltpu.PrefetchScalarGridSpec(
            num_scalar_prefetch=1, grid=(2, nt),
            in_specs=[ANY, ANY,
                      pl.BlockSpec((1, LANES), const),
                      pl.BlockSpec((1, LANES), const),
                      pl.BlockSpec((1, d), const),
                      pl.BlockSpec((1, half), lambda p, i, o: (0, o[p])),
                      ANY],
            out_specs=[pl.BlockSpec((tm, half), lambda p, i, o: (i, o[p])), ANY, ANY, ANY],
            scratch_shapes=[pltpu.VMEM((2, half, d), BF16), pltpu.VMEM((s, d), BF16),
                            pltpu.VMEM((2, tc, d), F32), pltpu.VMEM((2, tc, 1), jnp.int32),
                            pltpu.VMEM((2, tc, 2 * LANES), F32),
                            pltpu.SemaphoreType.DMA((7,)), pltpu.SemaphoreType.DMA((7,)),
                            pltpu.SemaphoreType.DMA((3,)), pltpu.SemaphoreType.DMA((4,)),
                            pltpu.SemaphoreType.DMA((4,))]),
        out_shape=[jax.ShapeDtypeStruct((s, D_IN), BF16), jax.ShapeDtypeStruct((s, d), BF16),
                   jax.ShapeDtypeStruct((s, 2 * LANES), F32), jax.ShapeDtypeStruct((D_IN, d), BF16)],
        compiler_params=pltpu.CompilerParams(dimension_semantics=("arbitrary", "arbitrary"),
                                             vmem_limit_bytes=IN_PROJ_VMEM_LIMIT),
    )(order, x, pos_col, freq, sign, g_pre, bias, wt_shard)


def out_proj_loss(cat, w_out, x, target, g_post):
    s, d = x.shape
    tm = _tile(s, (256, 128))

    def body(cat_ref, w_ref, x_ref, t_ref, g_ref, dy_ref, dout_ref, dg_ref, loss_ref):
        @pl.when(pl.program_id(0) == 0)
        def _():
            dg_ref[...] = jnp.zeros_like(dg_ref)
            loss_ref[...] = jnp.zeros_like(loss_ref)

        g = g_ref[...]
        ys = [_dot(cat_ref[c0:c0 + CHUNK, :], w_ref[...], 1, 0) for c0 in range(0, tm, CHUNK)]
        for c0 in range(0, tm, CHUNK):
            rows = slice(c0, c0 + CHUNK)
            yv = ys[c0 // CHUNK]
            r = lax.rsqrt(jnp.mean(yv * yv, axis=-1, keepdims=True) + EPS)
            nrm = yv * r
            err = x_ref[rows, :] + nrm * g - t_ref[rows, :]
            loss_ref[...] += 0.5 * jnp.sum(jnp.sum(err * err, axis=-1, keepdims=True), axis=0, keepdims=True) / d
            dout = err * (1.0 / d)
            dout_ref[rows, :] = dout
            dg_ref[...] += jnp.sum(dout * nrm, axis=0, keepdims=True)
            dn = dout * g
            dy = r * (dn - nrm * jnp.mean(dn * nrm, axis=-1, keepdims=True))
            dy_ref[rows, :] = dy.astype(BF16)

    return pl.pallas_call(
        body, name="out_proj_loss", grid=(s // tm,),
        in_specs=[pl.BlockSpec((tm, d), lambda i: (i, 0)),
                  pl.BlockSpec((d, d), lambda i: (0, 0)),
                  pl.BlockSpec((tm, d), lambda i: (i, 0)),
                  pl.BlockSpec((tm, d), lambda i: (i, 0)),
                  pl.BlockSpec((1, d), lambda i: (0, 0))],
        out_specs=[pl.BlockSpec((tm, d), lambda i: (i, 0)),
                   pl.BlockSpec((tm, d), lambda i: (i, 0)),
                   pl.BlockSpec((1, d), lambda i: (0, 0)),
                   pl.BlockSpec((1, LANES), lambda i: (0, 0))],
        out_shape=[jax.ShapeDtypeStruct((s, d), BF16), jax.ShapeDtypeStruct((s, d), F32),
                   jax.ShapeDtypeStruct((1, d), F32), jax.ShapeDtypeStruct((1, LANES), F32)],
        compiler_params=_cparams(("arbitrary",)),
    )(cat, w_out, x, target, g_post)


def matmul_nt(a, b, name):
    m, k = a.shape
    n = b.shape[0]
    tm = _tile(m, (512, 256, 128))

    def body(a_ref, b_ref, o_ref):
        o_ref[...] = _dot(a_ref[...], b_ref[...], 1, 1).astype(o_ref.dtype)

    return pl.pallas_call(
        body, name=name, grid=(m // tm,),
        in_specs=[pl.BlockSpec((tm, k), lambda i: (i, 0)), pl.BlockSpec((n, k), lambda i: (0, 0))],
        out_specs=pl.BlockSpec((tm, n), lambda i: (i, 0)),
        out_shape=jax.ShapeDtypeStruct((m, n), BF16),
        compiler_params=_cparams(("arbitrary",)),
    )(a, b)


def matmul_tn(a, b, tm, name, comms=()):
    k, m = a.shape
    n = b.shape[1]
    steps = m // tm
    hosted = _Hosted(comms)

    kc = _tile(k, (1024, 128))
    pieces = k // kc

    def body(*refs):
        (a_ref, b_hbm), (o_ref, cs_ref), (b_ref, b_sems), phases = hosted.split(refs, 2, 2, 2)
        step = pl.program_id(0)
        _before_step(phases, step, steps)

        def b_load(j):
            return pltpu.make_async_copy(b_hbm.at[j * kc:(j + 1) * kc, :], b_ref.at[j * kc:(j + 1) * kc, :], b_sems.at[j])

        @pl.when(step == 0)
        def _():
            for j in range(pieces):
                b_load(j).start()
            acc = None
            for j in range(pieces):
                b_load(j).wait()
                part = _dot(a_ref[j * kc:(j + 1) * kc, :], b_ref[j * kc:(j + 1) * kc, :], 0, 0)
                acc = part if acc is None else acc + part
            o_ref[...] = acc.astype(o_ref.dtype)

        @pl.when(step > 0)
        def _():
            o_ref[...] = _dot(a_ref[...], b_ref[...], 0, 0).astype(o_ref.dtype)

        rows = _tile(k, (512, 128))
        cs = jnp.zeros((1, tm), F32)
        for r0 in range(0, k, rows):
            cs = cs + jnp.sum(a_ref[r0:r0 + rows, :].astype(F32), axis=0, keepdims=True)
        cs_ref[...] = cs
        _after_step(phases, step, steps)

    return pl.pallas_call(
        body, name=name, grid=(steps,),
        in_specs=[pl.BlockSpec((k, tm), lambda i: (0, i)), ANY] + hosted.in_specs,
        out_specs=[pl.BlockSpec((tm, n), lambda i: (i, 0)), pl.BlockSpec((1, tm), lambda i: (0, i))] + hosted.out_specs,
        out_shape=[jax.ShapeDtypeStruct((m, n), BF16), jax.ShapeDtypeStruct((1, m), F32)] + hosted.out_shape,
        scratch_shapes=[pltpu.VMEM((k, n), b.dtype), pltpu.SemaphoreType.DMA((pieces,))] + hosted.scratch,
        compiler_params=_cparams(("arbitrary",)),
    )(a, b, *hosted.inputs)


def in_proj_bwd(dproj, wt, x, g_pre, dout, comms=()):
    s, d = x.shape
    tm = _tile(s, (512, 256, 128))
    steps = s // tm
    nsub = tm // CHUNK
    kw = 8 * LANES
    kchunks = [(k0, kw) for k0 in range(0, D_IN - D_IN % kw, kw)]
    if D_IN % kw:
        kchunks.append((D_IN - D_IN % kw, D_IN % kw))
    ksplit = len(kchunks)
    hosted = _Hosted(comms)

    def body(*refs):
        ((*dp_refs, w_hbm, x_hbm, g_ref, dout_hbm), (gx_hbm, dg_ref),
         (w_ref, w_sems, xbuf, dbuf, gbuf, in_sems, out_sems), phases) = hosted.split(refs, 4 + ksplit, 2, 7)
        step = pl.program_id(0)
        _before_step(phases, step, steps)

        def rows_of(ref, c):
            return ref.at[pl.ds(pl.multiple_of(step * tm + c * CHUNK, CHUNK), CHUNK), :]

        def fetches(c):
            return (pltpu.make_async_copy(rows_of(x_hbm, c), xbuf.at[c], in_sems.at[c]),
                    pltpu.make_async_copy(rows_of(dout_hbm, c), dbuf.at[c], in_sems.at[nsub + c]))

        def put(c):
            return pltpu.make_async_copy(gbuf.at[c % 2], rows_of(gx_hbm, c), out_sems.at[c % 2])

        for c in range(nsub):
            for cp in fetches(c):
                cp.start()

        def w_load(j):
            k0, kw = kchunks[j]
            return pltpu.make_async_copy(w_hbm.at[k0:k0 + kw, :], w_ref.at[k0:k0 + kw, :], w_sems.at[j])

        @pl.when(step == 0)
        def _():
            dg_ref[...] = jnp.zeros_like(dg_ref)
            for j in range(ksplit):
                w_load(j).start()

        dh_all = None
        for j, ((k0, kw), dp_ref) in enumerate(zip(kchunks, dp_refs)):
            @pl.when(step == 0)
            def _():
                w_load(j).wait()

            part = _dot(dp_ref[...], w_ref[k0:k0 + kw, :], 1, 0)
            dh_all = part if dh_all is None else dh_all + part
        for c in range(nsub):
            for cp in fetches(c):
                cp.wait()
            if c >= 2:
                put(c - 2).wait()
            elif c < nsub:
                @pl.when(step > 0)
                def _():
                    put(max(nsub - 2, 0) + c).wait()
            dh = dh_all[c * CHUNK:(c + 1) * CHUNK, :]
            xv = xbuf[c]
            r = lax.rsqrt(jnp.mean(xv * xv, axis=-1, keepdims=True) + EPS)
            xn = xv * r
            dg_ref[...] += jnp.sum(dh * xn, axis=0, keepdims=True)
            dn = dh * g_ref[...]
            gbuf[c % 2] = dbuf[c] + r * (dn - xn * jnp.mean(dn * xn, axis=-1, keepdims=True))
            put(c).start()
        @pl.when(step == steps - 1)
        def _():
            for c in range(max(nsub - 2, 0), nsub):
                put(c).wait()

        _after_step(phases, step, steps)

    side_in, side_out = pltpu.VMEM((nsub, CHUNK, d), F32), pltpu.VMEM((2, CHUNK, d), F32)
    row = pl.BlockSpec((1, d), lambda i: (0, 0))
    return pl.pallas_call(
        body, name="in_proj_bwd", grid=(steps,),
        in_specs=[pl.BlockSpec((tm, kw), functools.partial(lambda j, i: (i, j), k0 // kw)) for k0, kw in kchunks]
        + [ANY, ANY, row, ANY] + hosted.in_specs,
        out_specs=[ANY, row] + hosted.out_specs,
        out_shape=[jax.ShapeDtypeStruct((s, d), F32), jax.ShapeDtypeStruct((1, d), F32)] + hosted.out_shape,
        scratch_shapes=[pltpu.VMEM((D_IN, d), BF16), pltpu.SemaphoreType.DMA((ksplit,)), side_in, side_in, side_out,
                        pltpu.SemaphoreType.DMA((2 * nsub,)), pltpu.SemaphoreType.DMA((2,))] + hosted.scratch,
        compiler_params=_cparams(("arbitrary",)),
    )(*([dproj] * ksplit), wt, x, g_pre, dout, *hosted.inputs)


def _lane_iota(shape):
    return lax.broadcasted_iota(jnp.int32, shape, len(shape) - 1)


def _partner(v):
    low = (_lane_iota(v.shape) % HEAD_DIM) < (HEAD_DIM // 2)
    return jnp.where(low, pltpu.roll(v, LANES - HEAD_DIM // 2, 1), pltpu.roll(v, HEAD_DIM // 2, 1))


def _rope(v, cos, sin_signed):
    return v * cos + _partner(v) * sin_signed


def _rope_transposed(dv, cos, sin_signed):
    return dv * cos - _partner(dv) * sin_signed


def _both_halves(v, kv_head):
    keep = (_lane_iota(v.shape) >= HEAD_DIM) if kv_head else (_lane_iota(v.shape) < HEAD_DIM)
    return jnp.where(keep, v, pltpu.roll(v, HEAD_DIM, 1))


def _fold_halves(acc):
    return acc + pltpu.roll(acc, HEAD_DIM, 1)


def _by_half(a, b):
    shape = jnp.broadcast_shapes(jnp.shape(a), jnp.shape(b))
    return jnp.where(_lane_iota(shape) < HEAD_DIM, a, b)


def _stack_heads(pair):
    return jnp.concatenate([_by_half(pair, 0.0), _by_half(0.0, pair)], axis=0)


def _band_bias(has_prev):
    i = lax.broadcasted_iota(jnp.int32, (2 * CHUNK, 2 * CHUNK), 0) % CHUNK
    j = lax.broadcasted_iota(jnp.int32, (2 * CHUNK, 2 * CHUNK), 1)
    band = jnp.logical_and(j > i, j <= i + CHUNK)
    return jnp.where(jnp.logical_and(band, jnp.logical_or(j >= CHUNK, has_prev)), 0.0, NEG)


def _probs_staged(qm2s, kk2s, bias, sink_cols):
    k = range(len(qm2s))
    scs = [_dot(qm2s[i], kk2s[i], 1, 1) + bias for i in k]
    mxs = [jnp.maximum(jnp.max(scs[i], axis=-1, keepdims=True), sink_cols[i]) for i in k]
    ps = [jnp.exp(scs[i] - mxs[i]) for i in k]
    ess = [jnp.exp(sink_cols[i] - mxs[i]) for i in k]
    invs = [1.0 / (jnp.sum(ps[i], axis=-1, keepdims=True) + ess[i]) for i in k]
    return [ps[i] * invs[i] for i in k], [ess[i] * invs[i] for i in k]


def _sink_col(sinks_ref, pair):
    row = lax.broadcasted_iota(jnp.int32, (2 * CHUNK, 1), 0)
    return jnp.where(row < CHUNK, sinks_ref[2 * pair], sinks_ref[2 * pair + 1])


def _layer_norm_parts(v):
    mu = jnp.mean(v, axis=-1, keepdims=True)
    xc = v - mu
    rstd = lax.rsqrt(jnp.mean(xc * xc, axis=-1, keepdims=True) + EPS)
    return xc * rstd, rstd


def _masked_spatial(w_ref, g):
    t = lax.broadcasted_iota(jnp.int32, (CHUNK, CHUNK), 0)
    sidx = lax.broadcasted_iota(jnp.int32, (CHUNK, CHUNK), 1)
    return jnp.where(t >= sidx, w_ref[g], 0.0).astype(BF16)


def _keys_values(kv_ref, kvp_ref, rope_ref, ropep_ref):
    cos_c, sin_c = rope_ref[:, :LANES], rope_ref[:, LANES:]
    cos_p, sin_p = ropep_ref[:, :LANES], ropep_ref[:, LANES:]
    k_c = _rope(kv_ref[:, :D_KV].astype(F32), cos_c, sin_c)
    k_p = _rope(kvp_ref[:, :D_KV].astype(F32), cos_p, sin_p)
    keys = jnp.concatenate([k_p, k_c], axis=0)
    vals = jnp.concatenate([kvp_ref[:, D_KV:], kv_ref[:, D_KV:]], axis=0).astype(F32)
    return keys, vals, (cos_c, sin_c, cos_p, sin_p)


def mixer_fwd(proj, rope, ln_g, ln_b, w_sp, b_sp_rows, sinks, comms=()):
    s = proj.shape[0]
    nb = s // CHUNK
    hosted = _Hosted(comms)

    def body(sinks_ref, *refs):
        ((proj_ref, kvp_ref, rope_ref, ropep_ref, lng_ref, lnb_ref, w_ref, b_ref), (cat_ref, p_ref), _,
         phases) = hosted.split(refs, 8, 2, 0)
        n = pl.program_id(0)
        _before_step(phases, n, nb)
        xhat, _ = _layer_norm_parts(proj_ref[:, OFF_V:OFF_V + D_GMLP].astype(F32))
        vnb = (xhat * lng_ref[...] + lnb_ref[...]).astype(BF16)
        mixeds = [_dot(_masked_spatial(w_ref, g), vnb[:, g * CHUNK:(g + 1) * CHUNK], 1, 0) + b_ref[g]
                  for g in range(GROUPS)]
        for g in range(GROUPS):
            za = proj_ref[:, OFF_ZA + g * CHUNK:OFF_ZA + (g + 1) * CHUNK].astype(F32)
            u = proj_ref[:, OFF_U + g * CHUNK:OFF_U + (g + 1) * CHUNK].astype(F32)
            cat_ref[:, g * CHUNK:(g + 1) * CHUNK] = (u * mixeds[g] * (za * _sigmoid(za))).astype(BF16)
        kv_ref = proj_ref.at[:, OFF_K:OFF_K + 2 * D_KV]
        keys, vals, (cos_c, sin_c, _, _) = _keys_values(kv_ref, kvp_ref, rope_ref, ropep_ref)
        cos_q, sin_q = cos_c * SCALE, sin_c * SCALE
        bias = _band_bias(n > 0)
        kk2 = [_both_halves(keys, kvh).astype(BF16) for kvh in range(N_KV_HEADS)]
        vv2 = [_both_halves(vals, kvh).astype(BF16) for kvh in range(N_KV_HEADS)]
        pairs = range(N_PAIRS)
        qms = [_stack_heads(_rope(proj_ref[:, OFF_Q + pair * LANES:OFF_Q + (pair + 1) * LANES].astype(F32),
                                  cos_q, sin_q)).astype(BF16) for pair in pairs]
        probs, sink_probs = _probs_staged(qms, [kk2[pair // PAIRS_PER_KV] for pair in pairs], bias,
                                          [_sink_col(sinks_ref, pair) for pair in pairs])
        pbs = [p.astype(BF16) for p in probs]
        outs = [_dot(pbs[pair], vv2[pair // PAIRS_PER_KV], 1, 0) for pair in pairs]
        first_col = _lane_iota((2 * CHUNK, 2 * CHUNK)) == 0
        for pair in pairs:
            p_ref[0, pair] = jnp.where(first_col, sink_probs[pair].astype(BF16), pbs[pair])
        for pair in pairs:
            out_pair = _by_half(outs[pair][:CHUNK], outs[pair][CHUNK:])
            zb = proj_ref[:, OFF_ZB + pair * LANES:OFF_ZB + (pair + 1) * LANES].astype(F32)
            cat_ref[:, D_GMLP + pair * LANES:D_GMLP + (pair + 1) * LANES] = (
                out_pair * (zb * _sigmoid(zb))).astype(BF16)
        _after_step(phases, n, nb)

    prev = lambda n, *_: (jnp.maximum(n - 1, 0), 0)
    kv_block = OFF_K // (2 * D_KV)
    return pl.pallas_call(
        body, name="mixer_fwd",
        grid_spec=pltpu.PrefetchScalarGridSpec(
            num_scalar_prefetch=1, grid=(nb,),
            in_specs=[pl.BlockSpec((CHUNK, D_IN), lambda n, *_: (n, 0)),
                      pl.BlockSpec((CHUNK, 2 * D_KV), lambda n, *_: (jnp.maximum(n - 1, 0), kv_block)),
                      pl.BlockSpec((CHUNK, 2 * LANES), lambda n, *_: (n, 0)),
                      pl.BlockSpec((CHUNK, 2 * LANES), prev),
                      pl.BlockSpec((1, D_GMLP), lambda n, *_: (0, 0)),
                      pl.BlockSpec((1, D_GMLP), lambda n, *_: (0, 0)),
                      pl.BlockSpec((GROUPS, CHUNK, CHUNK), lambda n, *_: (0, 0, 0)),
                      pl.BlockSpec((GROUPS, CHUNK, CHUNK), lambda n, *_: (0, 0, 0))] + hosted.in_specs,
            out_specs=[pl.BlockSpec((CHUNK, D_GMLP + D_ATTN), lambda n, *_: (n, 0)),
                       pl.BlockSpec((1, N_PAIRS, 2 * CHUNK, 2 * CHUNK), lambda n, *_: (n, 0, 0, 0))]
            + hosted.out_specs,
            scratch_shapes=hosted.scratch),
        out_shape=[jax.ShapeDtypeStruct((s, D_GMLP + D_ATTN), BF16),
                   jax.ShapeDtypeStruct((nb, N_PAIRS, 2 * CHUNK, 2 * CHUNK), BF16)] + hosted.out_shape,
        compiler_params=_cparams(("arbitrary",)),
    )(sinks, proj, proj, rope, rope, ln_g, ln_b, w_sp, b_sp_rows, *hosted.inputs)


def mixer_bwd(proj, dcat, probs, rope, ln_g, ln_b, w_sp, b_sp_rows, comms=()):
    s = proj.shape[0]
    nb = s // CHUNK
    hosted = _Hosted(comms)

    def body(*refs):
        ((proj_ref, kvp_ref, dcat_ref, p_ref, rope_ref, ropep_ref, lng_ref, lnb_ref, w_ref, b_ref),
         (dproj_ref, dw_ref, db_ref, dlng_ref, dlnb_ref, dsink_ref),
         (pend_ref, pend_kv_ref, dbacc_ref), phases) = hosted.split(refs, 10, 6, 3)
        n = pl.program_id(0)
        _before_step(phases, n, nb + 1)

        @pl.when(n == 0)
        def _():
            dw_ref[...] = jnp.zeros_like(dw_ref)
            dbacc_ref[...] = jnp.zeros_like(dbacc_ref)
            dlng_ref[...] = jnp.zeros_like(dlng_ref)
            dlnb_ref[...] = jnp.zeros_like(dlnb_ref)
            dsink_ref[...] = jnp.zeros_like(dsink_ref)

        @pl.when(n > 0)
        def _():
            dproj_ref[...] = pend_ref[...]

        def flush(dkv_prev):
            @pl.when(n > 0)
            def _():
                dproj_ref[:, OFF_K:OFF_K + 2 * D_KV] = (pend_kv_ref[...] + dkv_prev).astype(BF16)

        @pl.when(n < nb)
        def _():
            kv_ref = proj_ref.at[:, OFF_K:OFF_K + 2 * D_KV]
            keys, vals, (cos_c, sin_c, cos_p, sin_p) = _keys_values(kv_ref, kvp_ref, rope_ref, ropep_ref)
            cos_q, sin_q = cos_c * SCALE, sin_c * SCALE
            first_col = _lane_iota((2 * CHUNK, 2 * CHUNK)) == 0
            lane_row = _lane_iota((1, LANES))
            dsink = jnp.zeros((1, LANES), F32)
            dk_heads, dv_heads = [], []
            for kvh in range(N_KV_HEADS):
                kk2 = _both_halves(keys, kvh).astype(BF16)
                vv2 = _both_halves(vals, kvh).astype(BF16)
                pairs = list(range(kvh * PAIRS_PER_KV, (kvh + 1) * PAIRS_PER_KV))
                k4 = range(PAIRS_PER_KV)
                qm2s = [_stack_heads(_rope(proj_ref[:, OFF_Q + pair * LANES:OFF_Q + (pair + 1) * LANES].astype(F32),
                                           cos_q, sin_q)).astype(BF16) for pair in pairs]
                kept = [p_ref[0, pair] for pair in pairs]
                pbs = [jnp.where(first_col, jnp.zeros_like(kp), kp) for kp in kept]
                ps = [pb.astype(F32) for pb in pbs]
                p_sinks = [kp[:, 0:1].astype(F32) for kp in kept]
                o2s = [_dot(pb, vv2, 1, 0) for pb in pbs]
                zbs = [proj_ref[:, OFF_ZB + pair * LANES:OFF_ZB + (pair + 1) * LANES].astype(F32) for pair in pairs]
                sgs = [_sigmoid(zb) for zb in zbs]
                dybs = [dcat_ref[:, D_GMLP + pair * LANES:D_GMLP + (pair + 1) * LANES].astype(F32) for pair in pairs]
                for i, pair in enumerate(pairs):
                    out_pair = _by_half(o2s[i][:CHUNK], o2s[i][CHUNK:])
                    pend_ref[:, OFF_ZB + pair * LANES:OFF_ZB + (pair + 1) * LANES] = (
                        dybs[i] * out_pair * (sgs[i] * (1.0 + zbs[i] * (1.0 - sgs[i])))).astype(BF16)
                dom2s = [_stack_heads(dybs[i] * (zbs[i] * sgs[i])).astype(BF16) for i in k4]
                dps = [_dot(dom2, vv2, 1, 1) for dom2 in dom2s]
                deltas = [jnp.sum(ps[i] * dps[i], axis=-1, keepdims=True) for i in k4]
                dss = [ps[i] * (dps[i] - deltas[i]) for i in k4]
                for i, pair in enumerate(pairs):
                    dsk = -(p_sinks[i] * deltas[i])
                    dsink = dsink + jnp.where(lane_row == 2 * pair,
                                              jnp.sum(dsk[:CHUNK], axis=0, keepdims=True), 0.0)
                    dsink = dsink + jnp.where(lane_row == 2 * pair + 1,
                                              jnp.sum(dsk[CHUNK:], axis=0, keepdims=True), 0.0)
                dsbs = [ds.astype(BF16) for ds in dss]
                dq2s = [_dot(dsb, kk2, 1, 0) for dsb in dsbs]
                for pair, dq2 in zip(pairs, dq2s):
                    pend_ref[:, OFF_Q + pair * LANES:OFF_Q + (pair + 1) * LANES] = _rope_transposed(
                        _by_half(dq2[:CHUNK], dq2[CHUNK:]), cos_q, sin_q).astype(BF16)
                dkks = [_dot(dsbs[i], qm2s[i], 0, 0) for i in k4]
                dvvs = [_dot(pbs[i], dom2s[i], 0, 0) for i in k4]
                dk_heads.append(_fold_halves((dkks[0] + dkks[1]) + (dkks[2] + dkks[3])))
                dv_heads.append(_fold_halves((dvvs[0] + dvvs[1]) + (dvvs[2] + dvvs[3])))
            dk_rot = _by_half(dk_heads[0], dk_heads[1])
            dv_all = _by_half(dv_heads[0], dv_heads[1])
            dk_p = _rope_transposed(dk_rot[:CHUNK], cos_p, sin_p)
            dk_c = _rope_transposed(dk_rot[CHUNK:], cos_c, sin_c)
            flush(jnp.concatenate([dk_p, dv_all[:CHUNK]], axis=1))
            dsink_ref[...] += dsink
            pend_kv_ref[...] = jnp.concatenate([dk_c, dv_all[CHUNK:]], axis=1)
            xhat, rstd = _layer_norm_parts(proj_ref[:, OFF_V:OFF_V + D_GMLP].astype(F32))
            lng = lng_ref[...]
            vnb = (xhat * lng + lnb_ref[...]).astype(BF16)
            dvn_cols = []
            for g in range(GROUPS):
                cols = slice(g * CHUNK, (g + 1) * CHUNK)
                wm = _masked_spatial(w_ref, g)
                mixed = _dot(wm, vnb[:, cols], 1, 0) + b_ref[g]
                za = proj_ref[:, OFF_ZA + g * CHUNK:OFF_ZA + (g + 1) * CHUNK].astype(F32)
                u = proj_ref[:, OFF_U + g * CHUNK:OFF_U + (g + 1) * CHUNK].astype(F32)
                dya = dcat_ref[:, cols].astype(F32)
                sg = _sigmoid(za)
                sz = za * sg
                pend_ref[:, OFF_U + g * CHUNK:OFF_U + (g + 1) * CHUNK] = (dya * mixed * sz).astype(BF16)
                pend_ref[:, OFF_ZA + g * CHUNK:OFF_ZA + (g + 1) * CHUNK] = (
                    dya * u * mixed * (sg * (1.0 + za * (1.0 - sg)))).astype(BF16)
                dmixed = dya * u * sz
                dmb = dmixed.astype(BF16)
                dbacc_ref[g] += dmixed
                dw_ref[g] += _dot(dmb, vnb[:, cols], 1, 1)
                dvn_cols.append(_dot(wm, dmb, 0, 0))
            dvn = jnp.concatenate(dvn_cols, axis=1)
            dlng_ref[...] += jnp.sum(dvn * xhat, axis=0, keepdims=True)
            dlnb_ref[...] += jnp.sum(dvn, axis=0, keepdims=True)
            dxh = dvn * lng
            dv = rstd * (dxh - jnp.mean(dxh, axis=-1, keepdims=True)
                         - xhat * jnp.mean(dxh * xhat, axis=-1, keepdims=True))
            pend_ref[:, OFF_V:OFF_V + D_GMLP] = dv.astype(BF16)

        @pl.when(n == nb)
        def _():
            flush(jnp.zeros((CHUNK, 2 * D_KV), F32))
            t = lax.broadcasted_iota(jnp.int32, (CHUNK, CHUNK), 0)
            sidx = lax.broadcasted_iota(jnp.int32, (CHUNK, CHUNK), 1)
            lane = _lane_iota((CHUNK, LANES))
            dbt = jnp.zeros((CHUNK, LANES), F32)
            for g in range(GROUPS):
                dw_ref[g] = jnp.where(t >= sidx, dw_ref[g], 0.0)
                dbt = jnp.where(lane == g, jnp.sum(dbacc_ref[g], axis=-1, keepdims=True), dbt)
            db_ref[...] = jnp.transpose(dbt)[:GROUPS, :]

        _after_step(phases, n, nb + 1)

    cur = lambda n: (jnp.minimum(n, nb - 1), 0)
    prev = lambda n: (jnp.clip(n - 1, 0, nb - 1), 0)
    kv_block = OFF_K // (2 * D_KV)
    const2 = lambda n: (0, 0)
    const3 = lambda n: (0, 0, 0)
    return pl.pallas_call(
        body, name="mixer_bwd", grid=(nb + 1,),
        in_specs=[pl.BlockSpec((CHUNK, D_IN), cur),
                  pl.BlockSpec((CHUNK, 2 * D_KV), lambda n: (jnp.clip(n - 1, 0, nb - 1), kv_block)),
                  pl.BlockSpec((CHUNK, D_GMLP + D_ATTN), cur),
                  pl.BlockSpec((1, N_PAIRS, 2 * CHUNK, 2 * CHUNK), lambda n: (jnp.minimum(n, nb - 1), 0, 0, 0)),
                  pl.BlockSpec((CHUNK, 2 * LANES), cur),
                  pl.BlockSpec((CHUNK, 2 * LANES), prev),
                  pl.BlockSpec((1, D_GMLP), const2),
                  pl.BlockSpec((1, D_GMLP), const2),
                  pl.BlockSpec((GROUPS, CHUNK, CHUNK), const3),
                  pl.BlockSpec((GROUPS, CHUNK, CHUNK), const3)] + hosted.in_specs,
        out_specs=[pl.BlockSpec((CHUNK, D_IN), lambda n: (jnp.maximum(n - 1, 0), 0)),
                   pl.BlockSpec((GROUPS, CHUNK, CHUNK), const3),
                   pl.BlockSpec((GROUPS, CHUNK), const2),
                   pl.BlockSpec((1, D_GMLP), const2),
                   pl.BlockSpec((1, D_GMLP), const2),
                   pl.BlockSpec((1, LANES), const2)] + hosted.out_specs,
        scratch_shapes=[pltpu.VMEM((CHUNK, D_IN), BF16), pltpu.VMEM((CHUNK, 2 * D_KV), F32),
                        pltpu.VMEM((GROUPS, CHUNK, CHUNK), F32)] + hosted.scratch,
        out_shape=[jax.ShapeDtypeStruct((s, D_IN), BF16),
                   jax.ShapeDtypeStruct((GROUPS, CHUNK, CHUNK), F32),
                   jax.ShapeDtypeStruct((GROUPS, CHUNK), F32),
                   jax.ShapeDtypeStruct((1, D_GMLP), F32),
                   jax.ShapeDtypeStruct((1, D_GMLP), F32),
                   jax.ShapeDtypeStruct((1, LANES), F32)] + hosted.out_shape,
        compiler_params=_cparams(("arbitrary",)),
    )(proj, proj, dcat, probs, rope, rope, ln_g, ln_b, w_sp, b_sp_rows, *hosted.inputs)


def _adamw_math(w, g, m, v):
    m = ADAM_B1 * m + (1.0 - ADAM_B1) * g
    v = ADAM_B2 * v + (1.0 - ADAM_B2) * (g * g)
    m_hat = m / (1.0 - ADAM_B1 ** ADAM_STEP)
    v_hat = v / (1.0 - ADAM_B2 ** ADAM_STEP)
    delta = -ADAM_LR * (m_hat / (jnp.sqrt(v_hat) + ADAM_EPS) + ADAM_WD * w)
    return delta, m, v


def adamw_shard(terms, w, m, v, name):
    r, c = w.shape
    tr = _tile(r, (224, 256, 128, 8))
    n_terms = len(terms)

    def body(*refs):
        w_ref, m_ref, v_ref, g_ref, d_ref, nm_ref, nv_ref = refs[n_terms:]
        g = None
        for ref, (_, slots) in zip(refs[:n_terms], terms):
            for k in range(slots):
                part = ref[k].astype(F32)
                g = part if g is None else g + part
        g_ref[...] = g
        d_ref[...], nm_ref[...], nv_ref[...] = _adamw_math(w_ref[...], g, m_ref[...], v_ref[...])

    spec = pl.BlockSpec((tr, c), lambda i: (i, 0))
    return pl.pallas_call(
        body, name=name, grid=(r // tr,),
        in_specs=[pl.BlockSpec((slots, tr, c), lambda i: (0, i, 0)) for _, slots in terms] + [spec] * 3,
        out_specs=[spec] * 4, out_shape=[jax.ShapeDtypeStruct((r, c), F32)] * 4,
        compiler_params=_cparams(("arbitrary",)),
    )(*[a for a, _ in terms], w, m, v)


def adamw_small(gathered, lane_windows, params):
    n_par = len(params)

    def body(*refs):
        g_refs = refs[:n_par + 1]
        wmv_refs = refs[n_par + 1:4 * n_par + 1]
        out_refs = refs[4 * n_par + 1:]

        def total(ref):
            acc = ref[0]
            for dev in range(1, N_DEV):
                acc = acc + ref[dev]
            return acc

        for i in range(n_par):
            w_ref, m_ref, v_ref = wmv_refs[3 * i:3 * i + 3]
            g = total(g_refs[i])
            if lane_windows[i] is not None:
                start, size = lane_windows[i]
                g = g[..., start:start + size]
            delta, new_m, new_v = _adamw_math(w_ref[...], g, m_ref[...], v_ref[...])
            for ref, val in zip(out_refs[4 * i:4 * i + 4], (g, delta, new_m, new_v)):
                ref[...] = val
        out_refs[4 * n_par][...] = total(g_refs[n_par])

    flat = [a for wmv in params for a in wmv]
    out_shape = [jax.ShapeDtypeStruct(w.shape, F32) for (w, _, _) in params for _ in range(4)]
    out_shape.append(jax.ShapeDtypeStruct(gathered[-1].shape[1:], F32))
    outs = pl.pallas_call(body, name="adamw_small", out_shape=out_shape, compiler_params=_cparams())(*gathered, *flat)
    return [tuple(outs[4 * i:4 * i + 4]) for i in range(n_par)], outs[-1]


def kernel(x, positions, g_pre, w_in, b_qkv, ln_v_g, ln_v_b, w_spatial, b_spatial, attn_sinks, w_out, g_post, loss_target, m_g_pre, m_w_in, m_b_qkv, m_ln_v_g, m_ln_v_b, m_w_spatial, m_b_spatial, m_attn_sinks, m_w_out, m_g_post, v_g_pre, v_w_in, v_b_qkv, v_ln_v_g, v_ln_v_b, v_w_spatial, v_b_spatial, v_attn_sinks, v_w_out, v_g_post):
    x2, target = x[0], loss_target[0]
    seq = x2.shape[0]

    wt_shard = w_in[0].T.astype(BF16)
    wo_shard = w_out[0].astype(BF16)
    pos_col = positions.reshape(seq, 1)
    half = HEAD_DIM // 2
    inv_freq = ROPE_THETA ** (-jnp.arange(half, dtype=F32) * (2.0 / HEAD_DIM))
    freq = jnp.tile(inv_freq, LANES // half).reshape(1, LANES)
    sign = jnp.tile(jnp.concatenate([-jnp.ones((half,), F32), jnp.ones((half,), F32)]), LANES // HEAD_DIM)
    sign = sign.reshape(1, LANES)
    bias = jnp.concatenate([jnp.zeros((1, OFF_Q), F32), b_qkv, jnp.zeros((1, D_ATTN), F32)], axis=1)
    proj, h, rope, wt = in_proj_gather(x2, pos_col, freq, sign, g_pre, wt_shard, bias)

    b_rows = jnp.broadcast_to(b_spatial[0][:, :, None], (GROUPS, CHUNK, CHUNK))
    sinks = attn_sinks[0]
    cat, probs, wo = mixer_fwd(proj, rope, ln_v_g, ln_v_b, w_spatial[0], b_rows, sinks,
                               comms=[gather_comm([wo_shard])])
    dy, dout, d_g_post, loss_part = out_proj_loss(cat, wo, x2, target, g_post)

    dcat = matmul_nt(dy, wo, "out_proj_bwd")
    d_wo, _ = matmul_tn(cat, dy, 512, "w_out_grad")
    dproj, d_w_sp, d_b_sp, d_ln_g, d_ln_b, d_sinks, parts_wo = mixer_bwd(
        proj, dcat, probs, rope, ln_v_g, ln_v_b, w_spatial[0], b_rows, comms=[scatter_comm([d_wo])])
    small_parts = [d_ln_g, d_ln_b, d_w_sp, d_b_sp, d_sinks, d_g_post, loss_part]
    d_wt, colsum, *landed = matmul_tn(dproj, h, 768, "w_in_grad", comms=[gather_comm(small_parts, stack=True)])

    sum_wt = pair_exchange_sum(d_wt, "grad_pair_sum_w_in")
    started = chips_exchange_start(sum_wt)
    token = started[-1]
    grad_x, d_g_pre = in_proj_bwd(dproj, wt, x2, g_pre + token[:1, :1], dout)
    sum_wt, far_wt = chips_exchange_wait(started, d_g_pre)
    late = run_comm(gather_comm([d_g_pre, colsum], stack=True, direct=True), "allgather_late_grads")
    gathered = late + landed
    windows = [None, (OFF_Q, D_QKV), None, None, None, None, (0, N_Q_HEADS), None]
    small = [(g_pre, m_g_pre, v_g_pre), (b_qkv, m_b_qkv, v_b_qkv), (ln_v_g, m_ln_v_g, v_ln_v_g),
             (ln_v_b, m_ln_v_b, v_ln_v_b), (w_spatial[0], m_w_spatial[0], v_w_spatial[0]),
             (b_spatial[0], m_b_spatial[0], v_b_spatial[0]), (attn_sinks, m_attn_sinks, v_attn_sinks),
             (g_post, m_g_post, v_g_post)]
    small_out, loss_row = adamw_small(gathered, windows, small)
    lead = [False, False, False, False, True, True, False, False]
    small_out = [tuple(a[None] if ld else a for a in leaf) for leaf, ld in zip(small_out, lead)]

    wt_out = adamw_shard([(sum_wt, 1), (far_wt, 3)], w_in[0].T, m_w_in[0].T, v_w_in[0].T, "adamw_w_in")
    wo_out = adamw_shard([(parts_wo, N_DEV)], w_out[0], m_w_out[0], v_w_out[0], "adamw_w_out")

    def leaves(k):
        gp, bq, lg, lb, ws, bs, sk, gpo = (leaf[k] for leaf in small_out)
        return [gp, wt_out[k].T[None], bq, lg, lb, ws, bs, sk, wo_out[k][None], gpo]

    return (loss_row[0, 0], grad_x[None], *leaves(0), *leaves(1), *leaves(2), *leaves(3))
```

```python
import functools

import jax
import jax.numpy as jnp
from jax import lax
from jax.experimental import pallas as pl
from jax.experimental.pallas import tpu as pltpu

F32 = jnp.float32
BF16 = jnp.bfloat16

D_MODEL = 2048
D_GMLP = 1024
D_ATTN = 1024
CHUNK = 128
GROUPS = 8
HEAD_DIM = 64
N_Q_HEADS = 16
N_KV_HEADS = 2
D_KV = N_KV_HEADS * HEAD_DIM
D_IN = 3 * D_GMLP + D_ATTN + 2 * D_KV + D_ATTN
OFF_U, OFF_V, OFF_ZA = 0, D_GMLP, 2 * D_GMLP
OFF_Q = 3 * D_GMLP
OFF_K = OFF_Q + D_ATTN
OFF_VA = OFF_K + D_KV
OFF_ZB = OFF_VA + D_KV
D_QKV = D_ATTN + 2 * D_KV
ROPE_THETA = 10000.0
EPS = 1e-6
SCALE = HEAD_DIM ** -0.5
NEG = -1e30
N_PAIRS = N_Q_HEADS // 2
PAIRS_PER_KV = N_PAIRS // N_KV_HEADS

ADAM_LR = 0.001
ADAM_B1 = 0.9
ADAM_B2 = 0.999
ADAM_EPS = 1e-08
ADAM_WD = 0.01
ADAM_STEP = 10

N_DEV = 8
LANES = 128
VMEM_LIMIT = 56 * 1024 * 1024
IN_PROJ_VMEM_LIMIT = 61 * 1024 * 1024

MESH = pl.DeviceIdType.MESH
ANY = pl.BlockSpec(memory_space=pl.ANY)


def _cparams(sem=None):
    return pltpu.CompilerParams(dimension_semantics=sem, vmem_limit_bytes=VMEM_LIMIT)


def _tile(n, prefs):
    for t in prefs:
        if n % t == 0:
            return t
    return n


def _sigmoid(z):
    return 1.0 / (1.0 + jnp.exp(-z))


def _dot(a, b, ca, cb):
    return lax.dot_general(a, b, (((ca,), (cb,)), ((), ())), preferred_element_type=F32)


def _my_place():
    return lax.axis_index("x"), lax.axis_index("y"), lax.axis_index("c")


def _chip_of(x, y, r):
    return (x ^ (r & 1), y ^ (r >> 1))


def _peer(x, y, c, k):
    return (x ^ (k >> 2), y ^ ((k >> 1) & 1), c ^ (k & 1))


def _index(px, py, pc):
    return 4 * px + 2 * py + pc


class _Comm:
    def __init__(self, inputs, out_shape, scratch, bind):
        self.inputs, self.out_shape, self.scratch, self.bind = list(inputs), list(out_shape), list(scratch), bind


def gather_comm(shards, stack=False, direct=False):
    n_arr = len(shards)

    def bind(ins, outs, sems):
        send_sems, recv_sems, local_sems = sems
        x, y, c = _my_place()
        me, sibling = (x, y, c), (x, y, 1 - c)
        chips = [_chip_of(x, y, r) for r in (1, 2, 3)]

        def rows(a, px, py, pc):
            d = _index(px, py, pc)
            if stack:
                return outs[a].at[d]
            m = shards[a].shape[0]
            return outs[a].at[pl.ds(pl.multiple_of(d * m, 8), m), :]

        def copy(a, k, block, to, src=None):
            return pltpu.make_async_remote_copy(
                src_ref=rows(a, *block) if src is None else src, dst_ref=rows(a, *block),
                send_sem=send_sems.at[a * 7 + k], recv_sem=recv_sems.at[a * 7 + k],
                device_id=to, device_id_type=MESH)

        def mine(a):
            return pltpu.make_async_copy(ins[a], rows(a, *me), local_sems.at[a])

        def own_sends(a):
            if direct:
                return [copy(a, k - 1, me, _peer(x, y, c, k), src=ins[a]) for k in range(1, 8)]
            return ([copy(a, 0, me, sibling, src=ins[a])]
                    + [copy(a, 1 + j, me, (*chip, c), src=ins[a]) for j, chip in enumerate(chips)])

        def start():
            for a in range(n_arr):
                mine(a).start()
                for cp in own_sends(a):
                    cp.start()

        def relay():
            if direct:
                return
            for j, chip in enumerate(chips):
                for a in range(n_arr):
                    copy(a, 1 + j, (*chip, c), me).wait_recv()
                    copy(a, 4 + j, (*chip, c), sibling).start()

        def finish():
            for a in range(n_arr):
                if direct:
                    for k in range(1, 8):
                        copy(a, k - 1, _peer(x, y, c, k), me).wait_recv()
                else:
                    copy(a, 0, sibling, me).wait_recv()
                    for j, chip in enumerate(chips):
                        copy(a, 4 + j, (*chip, 1 - c), me).wait_recv()
                        copy(a, 4 + j, (*chip, c), sibling).wait_send()
                for cp in own_sends(a):
                    cp.wait_send()
                mine(a).wait()

        return start, relay, finish

    def gathered(s):
        return (N_DEV, *s.shape) if stack else (N_DEV * s.shape[0], s.shape[1])

    return _Comm(shards, [jax.ShapeDtypeStruct(gathered(s), s.dtype) for s in shards],
                 [pltpu.SemaphoreType.DMA((7 * n_arr,)), pltpu.SemaphoreType.DMA((7 * n_arr,)),
                  pltpu.SemaphoreType.DMA((n_arr,))], bind)


def scatter_comm(parts):
    n_arr = len(parts)

    def bind(ins, outs, sems):
        send_sems, recv_sems, local_sems = sems
        x, y, c = _my_place()
        my_index = _index(x, y, c)

        def block(a, d):
            m = parts[a].shape[0] // N_DEV
            return ins[a].at[pl.ds(pl.multiple_of(d * m, 16), m), :]

        def copy(a, k, slot):
            peer = _peer(x, y, c, k)
            return pltpu.make_async_remote_copy(
                src_ref=block(a, _index(*peer)), dst_ref=outs[a].at[slot],
                send_sem=send_sems.at[a * 7 + k - 1], recv_sem=recv_sems.at[a * 7 + k - 1],
                device_id=peer, device_id_type=MESH)

        def mine(a):
            return pltpu.make_async_copy(block(a, my_index), outs[a].at[my_index], local_sems.at[a])

        def start():
            for a in range(n_arr):
                mine(a).start()
                for k in range(1, 8):
                    copy(a, k, my_index).start()

        def finish():
            for a in range(n_arr):
                for k in range(1, 8):
                    copy(a, k, _index(*_peer(x, y, c, k))).wait_recv()
                    copy(a, k, my_index).wait_send()
                mine(a).wait()

        return start, (lambda: None), finish

    return _Comm(parts, [jax.ShapeDtypeStruct((N_DEV, p.shape[0] // N_DEV, p.shape[1]), p.dtype) for p in parts],
                 [pltpu.SemaphoreType.DMA((7 * n_arr,)), pltpu.SemaphoreType.DMA((7 * n_arr,)),
                  pltpu.SemaphoreType.DMA((n_arr,))], bind)


def run_comm(comm, name):
    n_in, n_out = len(comm.inputs), len(comm.out_shape)

    def body(*refs):
        start, relay, finish = comm.bind(refs[:n_in], refs[n_in:n_in + n_out], refs[n_in + n_out:])
        start()
        relay()
        finish()

    outs = pl.pallas_call(body, name=name, out_shape=comm.out_shape, in_specs=[ANY] * n_in,
                          out_specs=[ANY] * n_out, scratch_shapes=comm.scratch)(*comm.inputs)
    return list(outs)


def _chip_copy(r, src_ref, land_ref, send_sem, recv_sem):
    x, y, c = _my_place()
    return pltpu.make_async_remote_copy(src_ref=src_ref.at[r], dst_ref=land_ref.at[r - 1], send_sem=send_sem,
                                        recv_sem=recv_sem, device_id=(*_chip_of(x, y, r), c), device_id_type=MESH)


def chips_exchange_start(sums):
    def body(src_ref, land_ref, s1, s2, s3, r1, r2, r3, src_thru, land_thru, token):
        del src_thru, land_thru
        for r, send_sem, recv_sem in ((1, s1, r1), (2, s2, r2), (3, s3, r3)):
            _chip_copy(r, src_ref, land_ref, send_sem, recv_sem).start()
        token[...] = jnp.zeros_like(token)

    land = lax.empty((3,) + sums.shape[1:], sums.dtype)
    sem = pltpu.SemaphoreType.DMA(())
    hbm = pl.BlockSpec(memory_space=pltpu.HBM)
    sem_spec = pl.BlockSpec(memory_space=pltpu.SEMAPHORE)
    return pl.pallas_call(
        body, name="grad_exchange_chips_start",
        out_shape=(sem,) * 6 + (pltpu.HBM(sums.shape, sums.dtype), pltpu.HBM(land.shape, land.dtype),
                                jax.ShapeDtypeStruct((8, LANES), F32)),
        in_specs=(hbm, hbm), out_specs=(sem_spec,) * 6 + (hbm, hbm, pl.BlockSpec(memory_space=pltpu.VMEM)),
        input_output_aliases={0: 6, 1: 7},
        compiler_params=pltpu.CompilerParams(has_side_effects=pltpu.SideEffectType.DATAFLOW_SIDE_EFFECTING),
    )(pltpu.with_memory_space_constraint(sums, pltpu.HBM), pltpu.with_memory_space_constraint(land, pltpu.HBM))


def chips_exchange_wait(started, after):
    s1, s2, s3, r1, r2, r3, src_thru, land_thru, _ = started

    def body(src_ref, land_ref, s1, s2, s3, r1, r2, r3, after_ref, src_out, land_out):
        del after_ref, src_out, land_out
        for r, send_sem, recv_sem in ((1, s1, r1), (2, s2, r2), (3, s3, r3)):
            copy = _chip_copy(r, src_ref, land_ref, send_sem, recv_sem)
            copy.wait_send()
            copy.wait_recv()

    hbm = pl.BlockSpec(memory_space=pltpu.HBM)
    sem_spec = pl.BlockSpec(memory_space=pltpu.SEMAPHORE)
    return pl.pallas_call(
        body, name="grad_exchange_chips_wait",
        out_shape=(pltpu.HBM(src_thru.shape, src_thru.dtype), pltpu.HBM(land_thru.shape, land_thru.dtype)),
        in_specs=(hbm, hbm) + (sem_spec,) * 6 + (pl.BlockSpec(memory_space=pl.ANY),), out_specs=(hbm, hbm),
        input_output_aliases={0: 0, 1: 1},
        compiler_params=pltpu.CompilerParams(has_side_effects=pltpu.SideEffectType.DATAFLOW_SIDE_EFFECTING),
    )(src_thru, land_thru, s1, s2, s3, r1, r2, r3, after)


def gather_start(shard):
    def copy(k, src_ref, land_ref, send_sem, recv_sem):
        x, y, c = _my_place()
        return pltpu.make_async_remote_copy(src_ref=src_ref, dst_ref=land_ref.at[_index(x, y, c)], send_sem=send_sem,
                                            recv_sem=recv_sem, device_id=_peer(x, y, c, k), device_id_type=MESH)

    def start_body(src_ref, land_ref, *rest):
        sems, token = rest[:14], rest[16]
        for k in range(1, 8):
            copy(k, src_ref, land_ref, sems[k - 1], sems[7 + k - 1]).start()
        token[...] = jnp.zeros_like(token)

    land = lax.empty((N_DEV,) + shard.shape, shard.dtype)
    sem = pltpu.SemaphoreType.DMA(())
    hbm = pl.BlockSpec(memory_space=pltpu.HBM)
    sem_spec = pl.BlockSpec(memory_space=pltpu.SEMAPHORE)
    started = pl.pallas_call(
        start_body, name="allgather_late_grads_start",
        out_shape=(sem,) * 14 + (pltpu.HBM(shard.shape, shard.dtype), pltpu.HBM(land.shape, land.dtype),
                                 jax.ShapeDtypeStruct((8, LANES), F32)),
        in_specs=(hbm, hbm), out_specs=(sem_spec,) * 14 + (hbm, hbm, pl.BlockSpec(memory_space=pltpu.VMEM)),
        input_output_aliases={0: 14, 1: 15},
        compiler_params=pltpu.CompilerParams(has_side_effects=pltpu.SideEffectType.DATAFLOW_SIDE_EFFECTING),
    )(pltpu.with_memory_space_constraint(shard, pltpu.HBM), pltpu.with_memory_space_constraint(land, pltpu.HBM))
    return started, copy


def gather_wait(started, copy, after):
    sems, src_thru, land_thru = started[:14], started[14], started[15]

    def wait_body(src_ref, land_ref, *rest):
        for k in range(1, 8):
            cp = copy(k, src_ref, land_ref, rest[k - 1], rest[7 + k - 1])
            cp.wait_send()
            cp.wait_recv()

    hbm = pl.BlockSpec(memory_space=pltpu.HBM)
    sem_spec = pl.BlockSpec(memory_space=pltpu.SEMAPHORE)
    return pl.pallas_call(
        wait_body, name="allgather_late_grads_wait",
        out_shape=(pltpu.HBM(src_thru.shape, src_thru.dtype), pltpu.HBM(land_thru.shape, land_thru.dtype)),
        in_specs=(hbm, hbm) + (sem_spec,) * 14 + (pl.BlockSpec(memory_space=pl.ANY),), out_specs=(hbm, hbm),
        input_output_aliases={0: 0, 1: 1},
        compiler_params=pltpu.CompilerParams(has_side_effects=pltpu.SideEffectType.DATAFLOW_SIDE_EFFECTING),
    )(src_thru, land_thru, *sems, after)[1]


class _Hosted:
    def __init__(self, comms):
        self.comms = list(comms)
        self.inputs = [a for cm in self.comms for a in cm.inputs]
        self.out_shape = [s for cm in self.comms for s in cm.out_shape]
        self.scratch = [s for cm in self.comms for s in cm.scratch]
        self.in_specs = [ANY] * len(self.inputs)
        self.out_specs = [ANY] * len(self.out_shape)

    def split(self, refs, n_in, n_out, n_scratch):
        ni, no = len(self.inputs), len(self.out_shape)
        ins, rest = refs[:n_in], refs[n_in:]
        c_ins, rest = rest[:ni], rest[ni:]
        outs, rest = rest[:n_out], rest[n_out:]
        c_outs, rest = rest[:no], rest[no:]
        scratch, c_sems = rest[:n_scratch], rest[n_scratch:]
        phases = []
        for cm in self.comms:
            a, b, s = len(cm.inputs), len(cm.out_shape), len(cm.scratch)
            phases.append(cm.bind(c_ins[:a], c_outs[:b], c_sems[:s]))
            c_ins, c_outs, c_sems = c_ins[a:], c_outs[b:], c_sems[s:]
        return ins, outs, scratch, phases


def _before_step(phases, step, n_steps):
    if not phases:
        return

    @pl.when(step == 0)
    def _():
        for start, _, _ in phases:
            start()

    @pl.when(step == n_steps // 2)
    def _():
        for _, relay, _ in phases:
            relay()


def _after_step(phases, step, n_steps):
    if not phases:
        return

    @pl.when(step == n_steps - 1)
    def _():
        for _, _, finish in phases:
            finish()


def pair_exchange_sum(part, name):
    m, n = part.shape[0] // N_DEV, part.shape[1]

    def body(part_ref, out_ref, got, mine, summed, send_sems, recv_sems, in_sems, out_sems):
        x, y, c = _my_place()

        def rows(r, core):
            owner = _index(*_chip_of(x, y, r), core)
            return part_ref.at[pl.ds(pl.multiple_of(owner * m, 16), m), :]

        def to_sibling(r):
            return pltpu.make_async_remote_copy(src_ref=rows(r, 1 - c), dst_ref=got.at[r], send_sem=send_sems.at[r],
                                                recv_sem=recv_sems.at[r], device_id=(x, y, 1 - c), device_id_type=MESH)

        def fetch(r):
            return pltpu.make_async_copy(rows(r, c), mine.at[r % 2], in_sems.at[r % 2])

        def put(r):
            return pltpu.make_async_copy(summed.at[r % 2], out_ref.at[r], out_sems.at[r % 2])

        for r in range(4):
            to_sibling(r).start()
        fetch(0).start()
        for r in range(4):
            if r + 1 < 4:
                fetch(r + 1).start()
            fetch(r).wait()
            to_sibling(r).wait_recv()
            if r >= 2:
                put(r - 2).wait()
            summed[r % 2] = (mine[r % 2].astype(F32) + got[r].astype(F32)).astype(summed.dtype)
            put(r).start()
        for r in (2, 3):
            put(r).wait()
        for r in range(4):
            to_sibling(r).wait_send()

    return pl.pallas_call(
        body, name=name, out_shape=jax.ShapeDtypeStruct((4, m, n), part.dtype),
        in_specs=[ANY], out_specs=ANY,
        scratch_shapes=[pltpu.VMEM((4, m, n), part.dtype), pltpu.VMEM((2, m, n), part.dtype),
                        pltpu.VMEM((2, m, n), part.dtype), pltpu.SemaphoreType.DMA((4,)),
                        pltpu.SemaphoreType.DMA((4,)), pltpu.SemaphoreType.DMA((2,)), pltpu.SemaphoreType.DMA((2,))],
        compiler_params=_cparams(),
    )(part)


def in_proj_gather(x, pos_col, freq, sign, g_pre, wt_shard, bias):
    s, d = x.shape
    tm = _tile(s, (512, 256, 128))
    nt = s // tm
    tc = _tile(s, (256, 128))
    nc = s // tc
    m = wt_shard.shape[0]
    half = D_IN // 2
    xi = lax.axis_index("x")
    order = jnp.stack([xi, 1 - xi]).astype(jnp.int32)

    def body(order_ref, x_hbm, pos_hbm, freq_ref, sign_ref, g_ref, b_ref, shard_ref,
             proj_ref, h_hbm, rope_hbm, wt_ref,
             w_vmem, h_vmem, xbuf, posbuf, ropebuf, send_sems, recv_sems, local_sems, in_sems, out_sems):
        del order_ref
        p, i = pl.program_id(0), pl.program_id(1)
        xx, yy, cc = _my_place()
        me, sibling = (xx, yy, cc), (xx, yy, 1 - cc)
        chips = [_chip_of(xx, yy, r) for r in (1, 2, 3)]

        def rows(px, py, pc):
            return wt_ref.at[pl.ds(pl.multiple_of(_index(px, py, pc) * m, 16), m), :]

        def copy(k, block, to, src=None):
            return pltpu.make_async_remote_copy(
                src_ref=rows(*block) if src is None else src, dst_ref=rows(*block),
                send_sem=send_sems.at[k], recv_sem=recv_sems.at[k], device_id=to, device_id_type=MESH)

        def mine():
            return pltpu.make_async_copy(shard_ref, rows(*me), local_sems.at[0])

        def to_sibling():
            return copy(0, me, sibling, src=shard_ref)

        def to_chip(j):
            return copy(1 + j, me, (*chips[j], cc), src=shard_ref)

        def relay(j):
            copy(1 + j, (*chips[j], cc), me).wait_recv()
            copy(4 + j, (*chips[j], cc), sibling).start()

        def relayed(j):
            copy(4 + j, (*chips[j], 1 - cc), me).wait_recv()

        def load_half(which, slot):
            rows_of_half = wt_ref.at[pl.ds(pl.multiple_of(which * half, 16), half), :]
            load = pltpu.make_async_copy(rows_of_half, w_vmem.at[slot], local_sems.at[1 + slot])
            load.start()
            load.wait()

        def piece(ref, c):
            return ref.at[pl.ds(c * tc, tc), :]

        def fetch(c):
            return (pltpu.make_async_copy(piece(x_hbm, c), xbuf.at[c % 2], in_sems.at[c % 2]),
                    pltpu.make_async_copy(piece(pos_hbm, c), posbuf.at[c % 2], in_sems.at[2 + c % 2]))

        def put(c):
            return (pltpu.make_async_copy(piece(h_vmem, c), piece(h_hbm, c), out_sems.at[c % 2]),
                    pltpu.make_async_copy(ropebuf.at[c % 2], piece(rope_hbm, c), out_sems.at[2 + c % 2]))

        def prologue():
            for cp in fetch(0):
                cp.start()
            for c in range(nc):
                if c + 1 < nc:
                    for cp in fetch(c + 1):
                        cp.start()
                for cp in fetch(c):
                    cp.wait()
                if c >= 2:
                    for cp in put(c - 2):
                        cp.wait()
                xv = xbuf[c % 2]
                r = lax.rsqrt(jnp.mean(xv * xv, axis=-1, keepdims=True) + EPS)
                h_vmem[c * tc:(c + 1) * tc, :] = (xv * r * g_ref[...]).astype(BF16)
                ang = posbuf[c % 2].astype(F32) * freq_ref[...]
                ropebuf[c % 2, :, :LANES] = jnp.cos(ang)
                ropebuf[c % 2, :, LANES:] = jnp.sin(ang) * sign_ref[...]
                for cp in put(c):
                    cp.start()
            for c in range(max(nc - 2, 0), nc):
                for cp in put(c):
                    cp.wait()

        @pl.when(jnp.logical_and(p == 0, i == 0))
        def _():
            mine().start()
            to_sibling().start()
            to_chip(1).start()
            to_chip(0).start()
            prologue()
            copy(0, sibling, me).wait_recv()
            relay(1)
            relayed(1)
            mine().wait()
            to_chip(1).wait_send()
            to_chip(0).wait_send()
            to_chip(2).start()
            load_half(xx, 0)

        @pl.when(jnp.logical_and(p == 1, i == 0))
        def _():
            relayed(0)
            relayed(2)
            load_half(1 - xx, 1)

        def project(slot):
            hb = h_vmem[pl.ds(pl.multiple_of(i * tm, tm), tm), :]
            proj_ref[...] = (_dot(hb, w_vmem[slot], 1, 1) + b_ref[...]).astype(BF16)

        @pl.when(p == 0)
        def _():
            project(0)

        @pl.when(p == 1)
        def _():
            project(1)

        @pl.when(jnp.logical_and(p == 0, i == 1))
        def _():
            relay(0)

        @pl.when(jnp.logical_and(p == 0, i == nt - 1))
        def _():
            relay(2)

        @pl.when(jnp.logical_and(p == 1, i == nt - 1))
        def _():
            to_sibling().wait_send()
            to_chip(2).wait_send()
            for j in range(3):
                copy(4 + j, (*chips[j], cc), sibling).wait_send()

    const = lambda p, i, o: (0, 0)
    return pl.pallas_call(
        body, name="in_proj_gather",
        grid_spec=pltpu.PrefetchScalarGridSpec(
            num_scalar_prefetch=1, grid=(2, nt),
            in_specs=[ANY, ANY,
                      pl.BlockSpec((1, LANES), const),
                      pl.BlockSpec((1, LANES), const),
                      pl.BlockSpec((1, d), const),
                      pl.BlockSpec((1, half), lambda p, i, o: (0, o[p])),
                      ANY],
            out_specs=[pl.BlockSpec((tm, half), lambda p, i, o: (i, o[p])), ANY, ANY, ANY],
            scratch_shapes=[pltpu.VMEM((2, half, d), BF16), pltpu.VMEM((s, d), BF16),
                            pltpu.VMEM((2, tc, d), F32), pltpu.VMEM((2, tc, 1), jnp.int32),
                            pltpu.VMEM((2, tc, 2 * LANES), F32),
                            pltpu.SemaphoreType.DMA((7,)), pltpu.SemaphoreType.DMA((7,)),
                            pltpu.SemaphoreType.DMA((3,)), pltpu.SemaphoreType.DMA((4,)),
                            pltpu.SemaphoreType.DMA((4,))]),
        out_shape=[jax.ShapeDtypeStruct((s, D_IN), BF16), jax.ShapeDtypeStruct((s, d), BF16),
                   jax.ShapeDtypeStruct((s, 2 * LANES), F32), jax.ShapeDtypeStruct((D_IN, d), BF16)],
        compiler_params=pltpu.CompilerParams(dimension_semantics=("arbitrary", "arbitrary"),
                                             vmem_limit_bytes=IN_PROJ_VMEM_LIMIT),
    )(order, x, pos_col, freq, sign, g_pre, bias, wt_shard)


def out_proj_loss(cat, w_out, x, target, g_post):
    s, d = x.shape
    tm = _tile(s, (256, 128))

    def body(cat_ref, w_ref, x_ref, t_ref, g_ref, dy_ref, dout_ref, dg_ref, loss_ref):
        @pl.when(pl.program_id(0) == 0)
        def _():
            dg_ref[...] = jnp.zeros_like(dg_ref)
            loss_ref[...] = jnp.zeros_like(loss_ref)

        g = g_ref[...]
        ys = [_dot(cat_ref[c0:c0 + CHUNK, :], w_ref[...], 1, 0) for c0 in range(0, tm, CHUNK)]
        for c0 in range(0, tm, CHUNK):
            rows = slice(c0, c0 + CHUNK)
            yv = ys[c0 // CHUNK]
            r = lax.rsqrt(jnp.mean(yv * yv, axis=-1, keepdims=True) + EPS)
            nrm = yv * r
            err = x_ref[rows, :] + nrm * g - t_ref[rows, :]
            loss_ref[...] += 0.5 * jnp.sum(jnp.sum(err * err, axis=-1, keepdims=True), axis=0, keepdims=True) / d
            dout = err * (1.0 / d)
            dout_ref[rows, :] = dout
            dg_ref[...] += jnp.sum(dout * nrm, axis=0, keepdims=True)
            dn = dout * g
            dy = r * (dn - nrm * jnp.mean(dn * nrm, axis=-1, keepdims=True))
            dy_ref[rows, :] = dy.astype(BF16)

    return pl.pallas_call(
        body, name="out_proj_loss", grid=(s // tm,),
        in_specs=[pl.BlockSpec((tm, d), lambda i: (i, 0)),
                  pl.BlockSpec((d, d), lambda i: (0, 0)),
                  pl.BlockSpec((tm, d), lambda i: (i, 0)),
                  pl.BlockSpec((tm, d), lambda i: (i, 0)),
                  pl.BlockSpec((1, d), lambda i: (0, 0))],
        out_specs=[pl.BlockSpec((tm, d), lambda i: (i, 0)),
                   pl.BlockSpec((tm, d), lambda i: (i, 0)),
                   pl.BlockSpec((1, d), lambda i: (0, 0)),
                   pl.BlockSpec((1, LANES), lambda i: (0, 0))],
        out_shape=[jax.ShapeDtypeStruct((s, d), BF16), jax.ShapeDtypeStruct((s, d), F32),
                   jax.ShapeDtypeStruct((1, d), F32), jax.ShapeDtypeStruct((1, LANES), F32)],
        compiler_params=_cparams(("arbitrary",)),
    )(cat, w_out, x, target, g_post)


def matmul_nt(a, b, name):
    m, k = a.shape
    n = b.shape[0]
    tm = _tile(m, (512, 256, 128))

    def body(a_ref, b_ref, o_ref):
        o_ref[...] = _dot(a_ref[...], b_ref[...], 1, 1).astype(o_ref.dtype)

    return pl.pallas_call(
        body, name=name, grid=(m // tm,),
        in_specs=[pl.BlockSpec((tm, k), lambda i: (i, 0)), pl.BlockSpec((n, k), lambda i: (0, 0))],
        out_specs=pl.BlockSpec((tm, n), lambda i: (i, 0)),
        out_shape=jax.ShapeDtypeStruct((m, n), BF16),
        compiler_params=_cparams(("arbitrary",)),
    )(a, b)


def matmul_tn(a, b, tm, name, comms=()):
    k, m = a.shape
    n = b.shape[1]
    steps = m // tm
    hosted = _Hosted(comms)

    kc = _tile(k, (1024, 128))
    pieces = k // kc

    def body(*refs):
        (a_ref, b_hbm), (o_ref, cs_ref), (b_ref, b_sems), phases = hosted.split(refs, 2, 2, 2)
        step = pl.program_id(0)
        _before_step(phases, step, steps)

        def b_load(j):
            return pltpu.make_async_copy(b_hbm.at[j * kc:(j + 1) * kc, :], b_ref.at[j * kc:(j + 1) * kc, :], b_sems.at[j])

        @pl.when(step == 0)
        def _():
            for j in range(pieces):
                b_load(j).start()
            acc = None
            for j in range(pieces):
                b_load(j).wait()
                part = _dot(a_ref[j * kc:(j + 1) * kc, :], b_ref[j * kc:(j + 1) * kc, :], 0, 0)
                acc = part if acc is None else acc + part
            o_ref[...] = acc.astype(o_ref.dtype)

        @pl.when(step > 0)
        def _():
            o_ref[...] = _dot(a_ref[...], b_ref[...], 0, 0).astype(o_ref.dtype)

        rows = _tile(k, (512, 128))
        cs = jnp.zeros((1, tm), F32)
        for r0 in range(0, k, rows):
            cs = cs + jnp.sum(a_ref[r0:r0 + rows, :].astype(F32), axis=0, keepdims=True)
        cs_ref[...] = cs
        _after_step(phases, step, steps)

    return pl.pallas_call(
        body, name=name, grid=(steps,),
        in_specs=[pl.BlockSpec((k, tm), lambda i: (0, i)), ANY] + hosted.in_specs,
        out_specs=[pl.BlockSpec((tm, n), lambda i: (i, 0)), pl.BlockSpec((1, tm), lambda i: (0, i))] + hosted.out_specs,
        out_shape=[jax.ShapeDtypeStruct((m, n), BF16), jax.ShapeDtypeStruct((1, m), F32)] + hosted.out_shape,
        scratch_shapes=[pltpu.VMEM((k, n), b.dtype), pltpu.SemaphoreType.DMA((pieces,))] + hosted.scratch,
        compiler_params=_cparams(("arbitrary",)),
    )(a, b, *hosted.inputs)


def in_proj_bwd(dproj, wt, x, g_pre, dout, comms=()):
    s, d = x.shape
    tm = _tile(s, (512, 256, 128))
    steps = s // tm
    nsub = tm // CHUNK
    kw = 8 * LANES
    kchunks = [(k0, kw) for k0 in range(0, D_IN - D_IN % kw, kw)]
    if D_IN % kw:
        kchunks.append((D_IN - D_IN % kw, D_IN % kw))
    ksplit = len(kchunks)
    hosted = _Hosted(comms)

    def body(*refs):
        ((*dp_refs, w_hbm, x_hbm, g_ref, dout_hbm), (gx_hbm, dg_ref),
         (w_ref, w_sems, xbuf, dbuf, gbuf, in_sems, out_sems), phases) = hosted.split(refs, 4 + ksplit, 2, 7)
        step = pl.program_id(0)
        _before_step(phases, step, steps)

        def rows_of(ref, c):
            return ref.at[pl.ds(pl.multiple_of(step * tm + c * CHUNK, CHUNK), CHUNK), :]

        def fetches(c):
            return (pltpu.make_async_copy(rows_of(x_hbm, c), xbuf.at[c], in_sems.at[c]),
                    pltpu.make_async_copy(rows_of(dout_hbm, c), dbuf.at[c], in_sems.at[nsub + c]))

        def put(c):
            return pltpu.make_async_copy(gbuf.at[c % 2], rows_of(gx_hbm, c), out_sems.at[c % 2])

        for c in range(nsub):
            for cp in fetches(c):
                cp.start()

        def w_load(j):
            k0, kw = kchunks[j]
            return pltpu.make_async_copy(w_hbm.at[k0:k0 + kw, :], w_ref.at[k0:k0 + kw, :], w_sems.at[j])

        @pl.when(step == 0)
        def _():
            dg_ref[...] = jnp.zeros_like(dg_ref)
            for j in range(ksplit):
                w_load(j).start()

        dh_all = None
        for j, ((k0, kw), dp_ref) in enumerate(zip(kchunks, dp_refs)):
            @pl.when(step == 0)
            def _():
                w_load(j).wait()

            part = _dot(dp_ref[...], w_ref[k0:k0 + kw, :], 1, 0)
            dh_all = part if dh_all is None else dh_all + part
        for c in range(nsub):
            for cp in fetches(c):
                cp.wait()
            if c >= 2:
                put(c - 2).wait()
            elif c < nsub:
                @pl.when(step > 0)
                def _():
                    put(max(nsub - 2, 0) + c).wait()
            dh = dh_all[c * CHUNK:(c + 1) * CHUNK, :]
            xv = xbuf[c]
            r = lax.rsqrt(jnp.mean(xv * xv, axis=-1, keepdims=True) + EPS)
            xn = xv * r
            dg_ref[...] += jnp.sum(dh * xn, axis=0, keepdims=True)
            dn = dh * g_ref[...]
            gbuf[c % 2] = dbuf[c] + r * (dn - xn * jnp.mean(dn * xn, axis=-1, keepdims=True))
            put(c).start()
        @pl.when(step == steps - 1)
        def _():
            for c in range(max(nsub - 2, 0), nsub):
                put(c).wait()

        _after_step(phases, step, steps)

    side_in, side_out = pltpu.VMEM((nsub, CHUNK, d), F32), pltpu.VMEM((2, CHUNK, d), F32)
    row = pl.BlockSpec((1, d), lambda i: (0, 0))
    return pl.pallas_call(
        body, name="in_proj_bwd", grid=(steps,),
        in_specs=[pl.BlockSpec((tm, kw), functools.partial(lambda j, i: (i, j), k0 // kw)) for k0, kw in kchunks]
        + [ANY, ANY, row, ANY] + hosted.in_specs,
        out_specs=[ANY, row] + hosted.out_specs,
        out_shape=[jax.ShapeDtypeStruct((s, d), F32), jax.ShapeDtypeStruct((1, d), F32)] + hosted.out_shape,
        scratch_shapes=[pltpu.VMEM((D_IN, d), BF16), pltpu.SemaphoreType.DMA((ksplit,)), side_in, side_in, side_out,
                        pltpu.SemaphoreType.DMA((2 * nsub,)), pltpu.SemaphoreType.DMA((2,))] + hosted.scratch,
        compiler_params=_cparams(("arbitrary",)),
    )(*([dproj] * ksplit), wt, x, g_pre, dout, *hosted.inputs)


def _lane_iota(shape):
    return lax.broadcasted_iota(jnp.int32, shape, len(shape) - 1)


def _partner(v):
    low = (_lane_iota(v.shape) % HEAD_DIM) < (HEAD_DIM // 2)
    return jnp.where(low, pltpu.roll(v, LANES - HEAD_DIM // 2, 1), pltpu.roll(v, HEAD_DIM // 2, 1))


def _rope(v, cos, sin_signed):
    return v * cos + _partner(v) * sin_signed


def _rope_transposed(dv, cos, sin_signed):
    return dv * cos - _partner(dv) * sin_signed


def _both_halves(v, kv_head):
    keep = (_lane_iota(v.shape) >= HEAD_DIM) if kv_head else (_lane_iota(v.shape) < HEAD_DIM)
    return jnp.where(keep, v, pltpu.roll(v, HEAD_DIM, 1))


def _fold_halves(acc):
    return acc + pltpu.roll(acc, HEAD_DIM, 1)


def _by_half(a, b):
    shape = jnp.broadcast_shapes(jnp.shape(a), jnp.shape(b))
    return jnp.where(_lane_iota(shape) < HEAD_DIM, a, b)


def _stack_heads(pair):
    return jnp.concatenate([_by_half(pair, 0.0), _by_half(0.0, pair)], axis=0)


def _band_bias(has_prev):
    i = lax.broadcasted_iota(jnp.int32, (2 * CHUNK, 2 * CHUNK), 0) % CHUNK
    j = lax.broadcasted_iota(jnp.int32, (2 * CHUNK, 2 * CHUNK), 1)
    band = jnp.logical_and(j > i, j <= i + CHUNK)
    return jnp.where(jnp.logical_and(band, jnp.logical_or(j >= CHUNK, has_prev)), 0.0, NEG)


def _probs_staged(qm2s, kk2s, bias, sink_cols):
    k = range(len(qm2s))
    scs = [_dot(qm2s[i], kk2s[i], 1, 1) + bias for i in k]
    mxs = [jnp.maximum(jnp.max(scs[i], axis=-1, keepdims=True), sink_cols[i]) for i in k]
    ps = [jnp.exp(scs[i] - mxs[i]) for i in k]
    ess = [jnp.exp(sink_cols[i] - mxs[i]) for i in k]
    invs = [1.0 / (jnp.sum(ps[i], axis=-1, keepdims=True) + ess[i]) for i in k]
    return [ps[i] * invs[i] for i in k], [ess[i] * invs[i] for i in k]


def _sink_col(sinks_ref, pair):
    row = lax.broadcasted_iota(jnp.int32, (2 * CHUNK, 1), 0)
    return jnp.where(row < CHUNK, sinks_ref[2 * pair], sinks_ref[2 * pair + 1])


def _layer_norm_parts(v):
    mu = jnp.mean(v, axis=-1, keepdims=True)
    xc = v - mu
    rstd = lax.rsqrt(jnp.mean(xc * xc, axis=-1, keepdims=True) + EPS)
    return xc * rstd, rstd


def _masked_spatial(w_ref, g):
    t = lax.broadcasted_iota(jnp.int32, (CHUNK, CHUNK), 0)
    sidx = lax.broadcasted_iota(jnp.int32, (CHUNK, CHUNK), 1)
    return jnp.where(t >= sidx, w_ref[g], 0.0).astype(BF16)


def _keys_values(kv_ref, kvp_ref, rope_ref, ropep_ref):
    cos_c, sin_c = rope_ref[:, :LANES], rope_ref[:, LANES:]
    cos_p, sin_p = ropep_ref[:, :LANES], ropep_ref[:, LANES:]
    k_c = _rope(kv_ref[:, :D_KV].astype(F32), cos_c, sin_c)
    k_p = _rope(kvp_ref[:, :D_KV].astype(F32), cos_p, sin_p)
    keys = jnp.concatenate([k_p, k_c], axis=0)
    vals = jnp.concatenate([kvp_ref[:, D_KV:], kv_ref[:, D_KV:]], axis=0).astype(F32)
    return keys, vals, (cos_c, sin_c, cos_p, sin_p)


def mixer_fwd(proj, rope, ln_g, ln_b, w_sp, b_sp_rows, sinks, comms=()):
    s = proj.shape[0]
    nb = s // CHUNK
    hosted = _Hosted(comms)

    def body(sinks_ref, *refs):
        ((proj_ref, kvp_ref, rope_ref, ropep_ref, lng_ref, lnb_ref, w_ref, b_ref), (cat_ref, p_ref), _,
         phases) = hosted.split(refs, 8, 2, 0)
        n = pl.program_id(0)
        _before_step(phases, n, nb)
        xhat, _ = _layer_norm_parts(proj_ref[:, OFF_V:OFF_V + D_GMLP].astype(F32))
        vnb = (xhat * lng_ref[...] + lnb_ref[...]).astype(BF16)
        mixeds = [_dot(_masked_spatial(w_ref, g), vnb[:, g * CHUNK:(g + 1) * CHUNK], 1, 0) + b_ref[g]
                  for g in range(GROUPS)]
        for g in range(GROUPS):
            za = proj_ref[:, OFF_ZA + g * CHUNK:OFF_ZA + (g + 1) * CHUNK].astype(F32)
            u = proj_ref[:, OFF_U + g * CHUNK:OFF_U + (g + 1) * CHUNK].astype(F32)
            cat_ref[:, g * CHUNK:(g + 1) * CHUNK] = (u * mixeds[g] * (za * _sigmoid(za))).astype(BF16)
        kv_ref = proj_ref.at[:, OFF_K:OFF_K + 2 * D_KV]
        keys, vals, (cos_c, sin_c, _, _) = _keys_values(kv_ref, kvp_ref, rope_ref, ropep_ref)
        cos_q, sin_q = cos_c * SCALE, sin_c * SCALE
        bias = _band_bias(n > 0)
        kk2 = [_both_halves(keys, kvh).astype(BF16) for kvh in range(N_KV_HEADS)]
        vv2 = [_both_halves(vals, kvh).astype(BF16) for kvh in range(N_KV_HEADS)]
        pairs = range(N_PAIRS)
        qms = [_stack_heads(_rope(proj_ref[:, OFF_Q + pair * LANES:OFF_Q + (pair + 1) * LANES].astype(F32),
                                  cos_q, sin_q)).astype(BF16) for pair in pairs]
        probs, sink_probs = _probs_staged(qms, [kk2[pair // PAIRS_PER_KV] for pair in pairs], bias,
                                          [_sink_col(sinks_ref, pair) for pair in pairs])
        pbs = [p.astype(BF16) for p in probs]
        outs = [_dot(pbs[pair], vv2[pair // PAIRS_PER_KV], 1, 0) for pair in pairs]
        first_col = _lane_iota((2 * CHUNK, 2 * CHUNK)) == 0
        for pair in pairs:
            p_ref[0, pair] = jnp.where(first_col, sink_probs[pair].astype(BF16), pbs[pair])
        for pair in pairs:
            out_pair = _by_half(outs[pair][:CHUNK], outs[pair][CHUNK:])
            zb = proj_ref[:, OFF_ZB + pair * LANES:OFF_ZB + (pair + 1) * LANES].astype(F32)
            cat_ref[:, D_GMLP + pair * LANES:D_GMLP + (pair + 1) * LANES] = (
                out_pair * (zb * _sigmoid(zb))).astype(BF16)
        _after_step(phases, n, nb)

    prev = lambda n, *_: (jnp.maximum(n - 1, 0), 0)
    kv_block = OFF_K // (2 * D_KV)
    return pl.pallas_call(
        body, name="mixer_fwd",
        grid_spec=pltpu.PrefetchScalarGridSpec(
            num_scalar_prefetch=1, grid=(nb,),
            in_specs=[pl.BlockSpec((CHUNK, D_IN), lambda n, *_: (n, 0)),
                      pl.BlockSpec((CHUNK, 2 * D_KV), lambda n, *_: (jnp.maximum(n - 1, 0), kv_block)),
                      pl.BlockSpec((CHUNK, 2 * LANES), lambda n, *_: (n, 0)),
                      pl.BlockSpec((CHUNK, 2 * LANES), prev),
                      pl.BlockSpec((1, D_GMLP), lambda n, *_: (0, 0)),
                      pl.BlockSpec((1, D_GMLP), lambda n, *_: (0, 0)),
                      pl.BlockSpec((GROUPS, CHUNK, CHUNK), lambda n, *_: (0, 0, 0)),
                      pl.BlockSpec((GROUPS, CHUNK, CHUNK), lambda n, *_: (0, 0, 0))] + hosted.in_specs,
            out_specs=[pl.BlockSpec((CHUNK, D_GMLP + D_ATTN), lambda n, *_: (n, 0)),
                       pl.BlockSpec((1, N_PAIRS, 2 * CHUNK, 2 * CHUNK), lambda n, *_: (n, 0, 0, 0))]
            + hosted.out_specs,
            scratch_shapes=hosted.scratch),
        out_shape=[jax.ShapeDtypeStruct((s, D_GMLP + D_ATTN), BF16),
                   jax.ShapeDtypeStruct((nb, N_PAIRS, 2 * CHUNK, 2 * CHUNK), BF16)] + hosted.out_shape,
        compiler_params=_cparams(("arbitrary",)),
    )(sinks, proj, proj, rope, rope, ln_g, ln_b, w_sp, b_sp_rows, *hosted.inputs)


def mixer_bwd(proj, dcat, probs, rope, ln_g, ln_b, w_sp, b_sp_rows, comms=()):
    s = proj.shape[0]
    nb = s // CHUNK
    hosted = _Hosted(comms)

    def body(*refs):
        ((proj_ref, kvp_ref, dcat_ref, p_ref, rope_ref, ropep_ref, lng_ref, lnb_ref, w_ref, b_ref),
         (dproj_ref, dw_ref, db_ref, dlng_ref, dlnb_ref, dsink_ref),
         (pend_ref, pend_kv_ref, dbacc_ref), phases) = hosted.split(refs, 10, 6, 3)
        n = pl.program_id(0)
        _before_step(phases, n, nb + 1)

        @pl.when(n == 0)
        def _():
            dw_ref[...] = jnp.zeros_like(dw_ref)
            dbacc_ref[...] = jnp.zeros_like(dbacc_ref)
            dlng_ref[...] = jnp.zeros_like(dlng_ref)
            dlnb_ref[...] = jnp.zeros_like(dlnb_ref)
            dsink_ref[...] = jnp.zeros_like(dsink_ref)

        @pl.when(n > 0)
        def _():
            dproj_ref[...] = pend_ref[...]

        def flush(dkv_prev):
            @pl.when(n > 0)
            def _():
                dproj_ref[:, OFF_K:OFF_K + 2 * D_KV] = (pend_kv_ref[...] + dkv_prev).astype(BF16)

        @pl.when(n < nb)
        def _():
            kv_ref = proj_ref.at[:, OFF_K:OFF_K + 2 * D_KV]
            keys, vals, (cos_c, sin_c, cos_p, sin_p) = _keys_values(kv_ref, kvp_ref, rope_ref, ropep_ref)
            cos_q, sin_q = cos_c * SCALE, sin_c * SCALE
            first_col = _lane_iota((2 * CHUNK, 2 * CHUNK)) == 0
            lane_row = _lane_iota((1, LANES))
            dsink = jnp.zeros((1, LANES), F32)
            dk_heads, dv_heads = [], []
            for kvh in range(N_KV_HEADS):
                kk2 = _both_halves(keys, kvh).astype(BF16)
                vv2 = _both_halves(vals, kvh).astype(BF16)
                pairs = list(range(kvh * PAIRS_PER_KV, (kvh + 1) * PAIRS_PER_KV))
                k4 = range(PAIRS_PER_KV)
                qm2s = [_stack_heads(_rope(proj_ref[:, OFF_Q + pair * LANES:OFF_Q + (pair + 1) * LANES].astype(F32),
                                           cos_q, sin_q)).astype(BF16) for pair in pairs]
                kept = [p_ref[0, pair] for pair in pairs]
                pbs = [jnp.where(first_col, jnp.zeros_like(kp), kp) for kp in kept]
                ps = [pb.astype(F32) for pb in pbs]
                p_sinks = [kp[:, 0:1].astype(F32) for kp in kept]
                o2s = [_dot(pb, vv2, 1, 0) for pb in pbs]
                zbs = [proj_ref[:, OFF_ZB + pair * LANES:OFF_ZB + (pair + 1) * LANES].astype(F32) for pair in pairs]
                sgs = [_sigmoid(zb) for zb in zbs]
                dybs = [dcat_ref[:, D_GMLP + pair * LANES:D_GMLP + (pair + 1) * LANES].astype(F32) for pair in pairs]
                for i, pair in enumerate(pairs):
                    out_pair = _by_half(o2s[i][:CHUNK], o2s[i][CHUNK:])
                    pend_ref[:, OFF_ZB + pair * LANES:OFF_ZB + (pair + 1) * LANES] = (
                        dybs[i] * out_pair * (sgs[i] * (1.0 + zbs[i] * (1.0 - sgs[i])))).astype(BF16)
                dom2s = [_stack_heads(dybs[i] * (zbs[i] * sgs[i])).astype(BF16) for i in k4]
                dps = [_dot(dom2, vv2, 1, 1) for dom2 in dom2s]
                deltas = [jnp.sum(ps[i] * dps[i], axis=-1, keepdims=True) for i in k4]
                dss = [ps[i] * (dps[i] - deltas[i]) for i in k4]
                for i, pair in enumerate(pairs):
                    dsk = -(p_sinks[i] * deltas[i])
                    dsink = dsink + jnp.where(lane_row == 2 * pair,
                                              jnp.sum(dsk[:CHUNK], axis=0, keepdims=True), 0.0)
                    dsink = dsink + jnp.where(lane_row == 2 * pair + 1,
                                              jnp.sum(dsk[CHUNK:], axis=0, keepdims=True), 0.0)
                dsbs = [ds.astype(BF16) for ds in dss]
                dq2s = [_dot(dsb, kk2, 1, 0) for dsb in dsbs]
                for pair, dq2 in zip(pairs, dq2s):
                    pend_ref[:, OFF_Q + pair * LANES:OFF_Q + (pair + 1) * LANES] = _rope_transposed(
                        _by_half(dq2[:CHUNK], dq2[CHUNK:]), cos_q, sin_q).astype(BF16)
                dkks = [_dot(dsbs[i], qm2s[i], 0, 0) for i in k4]
                dvvs = [_dot(pbs[i], dom2s[i], 0, 0) for i in k4]
                dk_heads.append(_fold_halves((dkks[0] + dkks[1]) + (dkks[2] + dkks[3])))
                dv_heads.append(_fold_halves((dvvs[0] + dvvs[1]) + (dvvs[2] + dvvs[3])))
            dk_rot = _by_half(dk_heads[0], dk_heads[1])
            dv_all = _by_half(dv_heads[0], dv_heads[1])
            dk_p = _rope_transposed(dk_rot[:CHUNK], cos_p, sin_p)
            dk_c = _rope_transposed(dk_rot[CHUNK:], cos_c, sin_c)
            flush(jnp.concatenate([dk_p, dv_all[:CHUNK]], axis=1))
            dsink_ref[...] += dsink
            pend_kv_ref[...] = jnp.concatenate([dk_c, dv_all[CHUNK:]], axis=1)
            xhat, rstd = _layer_norm_parts(proj_ref[:, OFF_V:OFF_V + D_GMLP].astype(F32))
            lng = lng_ref[...]
            vnb = (xhat * lng + lnb_ref[...]).astype(BF16)
            dvn_cols = []
            for g in range(GROUPS):
                cols = slice(g * CHUNK, (g + 1) * CHUNK)
                wm = _masked_spatial(w_ref, g)
                mixed = _dot(wm, vnb[:, cols], 1, 0) + b_ref[g]
                za = proj_ref[:, OFF_ZA + g * CHUNK:OFF_ZA + (g + 1) * CHUNK].astype(F32)
                u = proj_ref[:, OFF_U + g * CHUNK:OFF_U + (g + 1) * CHUNK].astype(F32)
                dya = dcat_ref[:, cols].astype(F32)
                sg = _sigmoid(za)
                sz = za * sg
                pend_ref[:, OFF_U + g * CHUNK:OFF_U + (g + 1) * CHUNK] = (dya * mixed * sz).astype(BF16)
                pend_ref[:, OFF_ZA + g * CHUNK:OFF_ZA + (g + 1) * CHUNK] = (
                    dya * u * mixed * (sg * (1.0 + za * (1.0 - sg)))).astype(BF16)
                dmixed = dya * u * sz
                dmb = dmixed.astype(BF16)
                dbacc_ref[g] += dmixed
                dw_ref[g] += _dot(dmb, vnb[:, cols], 1, 1)
                dvn_cols.append(_dot(wm, dmb, 0, 0))
            dvn = jnp.concatenate(dvn_cols, axis=1)
            dlng_ref[...] += jnp.sum(dvn * xhat, axis=0, keepdims=True)
            dlnb_ref[...] += jnp.sum(dvn, axis=0, keepdims=True)
            dxh = dvn * lng
            dv = rstd * (dxh - jnp.mean(dxh, axis=-1, keepdims=True)
                         - xhat * jnp.mean(dxh * xhat, axis=-1, keepdims=True))
            pend_ref[:, OFF_V:OFF_V + D_GMLP] = dv.astype(BF16)

        @pl.when(n == nb)
        def _():
            flush(jnp.zeros((CHUNK, 2 * D_KV), F32))
            t = lax.broadcasted_iota(jnp.int32, (CHUNK, CHUNK), 0)
            sidx = lax.broadcasted_iota(jnp.int32, (CHUNK, CHUNK), 1)
            lane = _lane_iota((CHUNK, LANES))
            dbt = jnp.zeros((CHUNK, LANES), F32)
            for g in range(GROUPS):
                dw_ref[g] = jnp.where(t >= sidx, dw_ref[g], 0.0)
                dbt = jnp.where(lane == g, jnp.sum(dbacc_ref[g], axis=-1, keepdims=True), dbt)
            db_ref[...] = jnp.transpose(dbt)[:GROUPS, :]

        _after_step(phases, n, nb + 1)

    cur = lambda n: (jnp.minimum(n, nb - 1), 0)
    prev = lambda n: (jnp.clip(n - 1, 0, nb - 1), 0)
    kv_block = OFF_K // (2 * D_KV)
    const2 = lambda n: (0, 0)
    const3 = lambda n: (0, 0, 0)
    return pl.pallas_call(
        body, name="mixer_bwd", grid=(nb + 1,),
        in_specs=[pl.BlockSpec((CHUNK, D_IN), cur),
                  pl.BlockSpec((CHUNK, 2 * D_KV), lambda n: (jnp.clip(n - 1, 0, nb - 1), kv_block)),
                  pl.BlockSpec((CHUNK, D_GMLP + D_ATTN), cur),
                  pl.BlockSpec((1, N_PAIRS, 2 * CHUNK, 2 * CHUNK), lambda n: (jnp.minimum(n, nb - 1), 0, 0, 0)),
                  pl.BlockSpec((CHUNK, 2 * LANES), cur),
                  pl.BlockSpec((CHUNK, 2 * LANES), prev),
                  pl.BlockSpec((1, D_GMLP), const2),
                  pl.BlockSpec((1, D_GMLP), const2),
                  pl.BlockSpec((GROUPS, CHUNK, CHUNK), const3),
                  pl.BlockSpec((GROUPS, CHUNK, CHUNK), const3)] + hosted.in_specs,
        out_specs=[pl.BlockSpec((CHUNK, D_IN), lambda n: (jnp.maximum(n - 1, 0), 0)),
                   pl.BlockSpec((GROUPS, CHUNK, CHUNK), const3),
                   pl.BlockSpec((GROUPS, CHUNK), const2),
                   pl.BlockSpec((1, D_GMLP), const2),
                   pl.BlockSpec((1, D_GMLP), const2),
                   pl.BlockSpec((1, LANES), const2)] + hosted.out_specs,
        scratch_shapes=[pltpu.VMEM((CHUNK, D_IN), BF16), pltpu.VMEM((CHUNK, 2 * D_KV), F32),
                        pltpu.VMEM((GROUPS, CHUNK, CHUNK), F32)] + hosted.scratch,
        out_shape=[jax.ShapeDtypeStruct((s, D_IN), BF16),
                   jax.ShapeDtypeStruct((GROUPS, CHUNK, CHUNK), F32),
                   jax.ShapeDtypeStruct((GROUPS, CHUNK), F32),
                   jax.ShapeDtypeStruct((1, D_GMLP), F32),
                   jax.ShapeDtypeStruct((1, D_GMLP), F32),
                   jax.ShapeDtypeStruct((1, LANES), F32)] + hosted.out_shape,
        compiler_params=_cparams(("arbitrary",)),
    )(proj, proj, dcat, probs, rope, rope, ln_g, ln_b, w_sp, b_sp_rows, *hosted.inputs)


def _adamw_math(w, g, m, v):
    m = ADAM_B1 * m + (1.0 - ADAM_B1) * g
    v = ADAM_B2 * v + (1.0 - ADAM_B2) * (g * g)
    m_hat = m / (1.0 - ADAM_B1 ** ADAM_STEP)
    v_hat = v / (1.0 - ADAM_B2 ** ADAM_STEP)
    delta = -ADAM_LR * (m_hat / (jnp.sqrt(v_hat) + ADAM_EPS) + ADAM_WD * w)
    return delta, m, v


def adamw_shard(terms, w, m, v, name, after=None):
    r, c = w.shape
    tr = _tile(r, (224, 256, 128, 8))
    n_terms = len(terms)
    order_only = [] if after is None else [after]

    def body(*refs):
        w_ref, m_ref, v_ref, g_ref, d_ref, nm_ref, nv_ref = refs[n_terms:n_terms + 3] + refs[-4:]
        g = None
        for ref, (_, slots) in zip(refs[:n_terms], terms):
            for k in range(slots):
                part = ref[k].astype(F32)
                g = part if g is None else g + part
        g_ref[...] = g
        d_ref[...], nm_ref[...], nv_ref[...] = _adamw_math(w_ref[...], g, m_ref[...], v_ref[...])

    spec = pl.BlockSpec((tr, c), lambda i: (i, 0))
    return pl.pallas_call(
        body, name=name, grid=(r // tr,),
        in_specs=[pl.BlockSpec((slots, tr, c), lambda i: (0, i, 0)) for _, slots in terms] + [spec] * 3
        + [ANY] * len(order_only),
        out_specs=[spec] * 4, out_shape=[jax.ShapeDtypeStruct((r, c), F32)] * 4,
        compiler_params=_cparams(("arbitrary",)),
    )(*[a for a, _ in terms], w, m, v, *order_only)


def adamw_small(gathered, lane_windows, params):
    n_par = len(params)

    def body(*refs):
        g_refs = refs[:n_par + 1]
        wmv_refs = refs[n_par + 1:4 * n_par + 1]
        out_refs = refs[4 * n_par + 1:]

        def total(ref):
            acc = ref[0]
            for dev in range(1, N_DEV):
                acc = acc + ref[dev]
            return acc

        for i in range(n_par):
            w_ref, m_ref, v_ref = wmv_refs[3 * i:3 * i + 3]
            g = total(g_refs[i])
            if lane_windows[i] is not None:
                start, size = lane_windows[i]
                g = g[..., start:start + size]
            delta, new_m, new_v = _adamw_math(w_ref[...], g, m_ref[...], v_ref[...])
            for ref, val in zip(out_refs[4 * i:4 * i + 4], (g, delta, new_m, new_v)):
                ref[...] = val
        out_refs[4 * n_par][...] = total(g_refs[n_par])

    flat = [a for wmv in params for a in wmv]
    out_shape = [jax.ShapeDtypeStruct(w.shape, F32) for (w, _, _) in params for _ in range(4)]
    out_shape.append(jax.ShapeDtypeStruct(gathered[-1].shape[1:], F32))
    outs = pl.pallas_call(body, name="adamw_small", out_shape=out_shape, compiler_params=_cparams())(*gathered, *flat)
    return [tuple(outs[4 * i:4 * i + 4]) for i in range(n_par)], outs[-1]


def kernel(x, positions, g_pre, w_in, b_qkv, ln_v_g, ln_v_b, w_spatial, b_spatial, attn_sinks, w_out, g_post, loss_target, m_g_pre, m_w_in, m_b_qkv, m_ln_v_g, m_ln_v_b, m_w_spatial, m_b_spatial, m_attn_sinks, m_w_out, m_g_post, v_g_pre, v_w_in, v_b_qkv, v_ln_v_g, v_ln_v_b, v_w_spatial, v_b_spatial, v_attn_sinks, v_w_out, v_g_post):
    x2, target = x[0], loss_target[0]
    seq = x2.shape[0]

    wt_shard = w_in[0].T.astype(BF16)
    wo_shard = w_out[0].astype(BF16)
    pos_col = positions.reshape(seq, 1)
    half = HEAD_DIM // 2
    inv_freq = ROPE_THETA ** (-jnp.arange(half, dtype=F32) * (2.0 / HEAD_DIM))
    freq = jnp.tile(inv_freq, LANES // half).reshape(1, LANES)
    sign = jnp.tile(jnp.concatenate([-jnp.ones((half,), F32), jnp.ones((half,), F32)]), LANES // HEAD_DIM)
    sign = sign.reshape(1, LANES)
    bias = jnp.concatenate([jnp.zeros((1, OFF_Q), F32), b_qkv, jnp.zeros((1, D_ATTN), F32)], axis=1)
    proj, h, rope, wt = in_proj_gather(x2, pos_col, freq, sign, g_pre, wt_shard, bias)

    b_rows = jnp.broadcast_to(b_spatial[0][:, :, None], (GROUPS, CHUNK, CHUNK))
    sinks = attn_sinks[0]
    cat, probs, wo = mixer_fwd(proj, rope, ln_v_g, ln_v_b, w_spatial[0], b_rows, sinks,
                               comms=[gather_comm([wo_shard])])
    dy, dout, d_g_post, loss_part = out_proj_loss(cat, wo, x2, target, g_post)

    dcat = matmul_nt(dy, wo, "out_proj_bwd")
    d_wo, _ = matmul_tn(cat, dy, 512, "w_out_grad")
    dproj, d_w_sp, d_b_sp, d_ln_g, d_ln_b, d_sinks, parts_wo = mixer_bwd(
        proj, dcat, probs, rope, ln_v_g, ln_v_b, w_spatial[0], b_rows, comms=[scatter_comm([d_wo])])
    small_parts = [d_ln_g, d_ln_b, d_w_sp, d_b_sp, d_sinks, d_g_post, loss_part]
    d_wt, colsum, *landed = matmul_tn(dproj, h, 768, "w_in_grad", comms=[gather_comm(small_parts, stack=True)])

    sum_wt = pair_exchange_sum(d_wt, "grad_pair_sum_w_in")
    started = chips_exchange_start(sum_wt)
    token = started[-1]
    grad_x, d_g_pre = in_proj_bwd(dproj, wt, x2, g_pre + token[:1, :1], dout)
    sum_wt, far_wt = chips_exchange_wait(started, d_g_pre)
    late = jnp.concatenate([d_g_pre, colsum], axis=1)
    late_started, late_copy = gather_start(late)

    wo_out = adamw_shard([(parts_wo, N_DEV)], w_out[0], m_w_out[0], v_w_out[0], "adamw_w_out", after=late_started[-1])
    wt_out = adamw_shard([(sum_wt, 1), (far_wt, 3)], w_in[0].T, m_w_in[0].T, v_w_in[0].T, "adamw_w_in",
                         after=wo_out[0])
    landed_late = gather_wait(late_started, late_copy, wt_out[0])
    my_index = _index(*_my_place())
    all_late = lax.dynamic_update_slice(landed_late, late[None], (my_index, 0, 0))
    gathered = [all_late, all_late] + landed
    windows = [(0, D_MODEL), (D_MODEL + OFF_Q, D_QKV), None, None, None, None, (0, N_Q_HEADS), None]
    small = [(g_pre, m_g_pre, v_g_pre), (b_qkv, m_b_qkv, v_b_qkv), (ln_v_g, m_ln_v_g, v_ln_v_g),
             (ln_v_b, m_ln_v_b, v_ln_v_b), (w_spatial[0], m_w_spatial[0], v_w_spatial[0]),
             (b_spatial[0], m_b_spatial[0], v_b_spatial[0]), (attn_sinks, m_attn_sinks, v_attn_sinks),
             (g_post, m_g_post, v_g_post)]
    small_out, loss_row = adamw_small(gathered, windows, small)
    lead = [False, False, False, False, True, True, False, False]
    small_out = [tuple(a[None] if ld else a for a in leaf) for leaf, ld in zip(small_out, lead)]

    def leaves(k):
        gp, bq, lg, lb, ws, bs, sk, gpo = (leaf[k] for leaf in small_out)
        return [gp, wt_out[k].T[None], bq, lg, lb, ws, bs, sk, wo_out[k][None], gpo]

    return (loss_row[0, 0], grad_x[None], *leaves(0), *leaves(1), *leaves(2), *leaves(3))
```

```python
import functools

import jax
import jax.numpy as jnp
from jax import lax
from jax.experimental import pallas as pl
from jax.experimental.pallas import tpu as pltpu

F32 = jnp.float32
BF16 = jnp.bfloat16

D_MODEL = 2048
D_GMLP = 1024
D_ATTN = 1024
CHUNK = 128
GROUPS = 8
HEAD_DIM = 64
N_Q_HEADS = 16
N_KV_HEADS = 2
D_KV = N_KV_HEADS * HEAD_DIM
D_IN = 3 * D_GMLP + D_ATTN + 2 * D_KV + D_ATTN
OFF_U, OFF_V, OFF_ZA = 0, D_GMLP, 2 * D_GMLP
OFF_Q = 3 * D_GMLP
OFF_K = OFF_Q + D_ATTN
OFF_VA = OFF_K + D_KV
OFF_ZB = OFF_VA + D_KV
D_QKV = D_ATTN + 2 * D_KV
ROPE_THETA = 10000.0
EPS = 1e-6
SCALE = HEAD_DIM ** -0.5
NEG = -1e30
N_PAIRS = N_Q_HEADS // 2
PAIRS_PER_KV = N_PAIRS // N_KV_HEADS

ADAM_LR = 0.001
ADAM_B1 = 0.9
ADAM_B2 = 0.999
ADAM_EPS = 1e-08
ADAM_WD = 0.01
ADAM_STEP = 10

N_DEV = 8
LANES = 128
VMEM_LIMIT = 56 * 1024 * 1024
IN_PROJ_VMEM_LIMIT = 61 * 1024 * 1024

MESH = pl.DeviceIdType.MESH
ANY = pl.BlockSpec(memory_space=pl.ANY)


def _cparams(sem=None):
    return pltpu.CompilerParams(dimension_semantics=sem, vmem_limit_bytes=VMEM_LIMIT)


def _tile(n, prefs):
    for t in prefs:
        if n % t == 0:
            return t
    return n


def _sigmoid(z):
    return 1.0 / (1.0 + jnp.exp(-z))


def _dot(a, b, ca, cb):
    return lax.dot_general(a, b, (((ca,), (cb,)), ((), ())), preferred_element_type=F32)


def _my_place():
    return lax.axis_index("x"), lax.axis_index("y"), lax.axis_index("c")


def _chip_of(x, y, r):
    return (x ^ (r & 1), y ^ (r >> 1))


def _peer(x, y, c, k):
    return (x ^ (k >> 2), y ^ ((k >> 1) & 1), c ^ (k & 1))


def _index(px, py, pc):
    return 4 * px + 2 * py + pc


class _Comm:
    def __init__(self, inputs, out_shape, scratch, bind):
        self.inputs, self.out_shape, self.scratch, self.bind = list(inputs), list(out_shape), list(scratch), bind


def gather_comm(shards, stack=False, direct=False):
    n_arr = len(shards)

    def bind(ins, outs, sems):
        send_sems, recv_sems, local_sems = sems
        x, y, c = _my_place()
        me, sibling = (x, y, c), (x, y, 1 - c)
        chips = [_chip_of(x, y, r) for r in (1, 2, 3)]

        def rows(a, px, py, pc):
            d = _index(px, py, pc)
            if stack:
                return outs[a].at[d]
            m = shards[a].shape[0]
            return outs[a].at[pl.ds(pl.multiple_of(d * m, 8), m), :]

        def copy(a, k, block, to, src=None):
            return pltpu.make_async_remote_copy(
                src_ref=rows(a, *block) if src is None else src, dst_ref=rows(a, *block),
                send_sem=send_sems.at[a * 7 + k], recv_sem=recv_sems.at[a * 7 + k],
                device_id=to, device_id_type=MESH)

        def mine(a):
            return pltpu.make_async_copy(ins[a], rows(a, *me), local_sems.at[a])

        def own_sends(a):
            if direct:
                return [copy(a, k - 1, me, _peer(x, y, c, k), src=ins[a]) for k in range(1, 8)]
            return ([copy(a, 0, me, sibling, src=ins[a])]
                    + [copy(a, 1 + j, me, (*chip, c), src=ins[a]) for j, chip in enumerate(chips)])

        def start():
            for a in range(n_arr):
                mine(a).start()
                for cp in own_sends(a):
                    cp.start()

        def relay():
            if direct:
                return
            for j, chip in enumerate(chips):
                for a in range(n_arr):
                    copy(a, 1 + j, (*chip, c), me).wait_recv()
                    copy(a, 4 + j, (*chip, c), sibling).start()

        def finish():
            for a in range(n_arr):
                if direct:
                    for k in range(1, 8):
                        copy(a, k - 1, _peer(x, y, c, k), me).wait_recv()
                else:
                    copy(a, 0, sibling, me).wait_recv()
                    for j, chip in enumerate(chips):
                        copy(a, 4 + j, (*chip, 1 - c), me).wait_recv()
                        copy(a, 4 + j, (*chip, c), sibling).wait_send()
                for cp in own_sends(a):
                    cp.wait_send()
                mine(a).wait()

        return start, relay, finish

    def gathered(s):
        return (N_DEV, *s.shape) if stack else (N_DEV * s.shape[0], s.shape[1])

    return _Comm(shards, [jax.ShapeDtypeStruct(gathered(s), s.dtype) for s in shards],
                 [pltpu.SemaphoreType.DMA((7 * n_arr,)), pltpu.SemaphoreType.DMA((7 * n_arr,)),
                  pltpu.SemaphoreType.DMA((n_arr,))], bind)


def scatter_comm(parts):
    n_arr = len(parts)

    def bind(ins, outs, sems):
        send_sems, recv_sems, local_sems = sems
        x, y, c = _my_place()
        my_index = _index(x, y, c)

        def block(a, d):
            m = parts[a].shape[0] // N_DEV
            return ins[a].at[pl.ds(pl.multiple_of(d * m, 16), m), :]

        def copy(a, k, slot):
            peer = _peer(x, y, c, k)
            return pltpu.make_async_remote_copy(
                src_ref=block(a, _index(*peer)), dst_ref=outs[a].at[slot],
                send_sem=send_sems.at[a * 7 + k - 1], recv_sem=recv_sems.at[a * 7 + k - 1],
                device_id=peer, device_id_type=MESH)

        def mine(a):
            return pltpu.make_async_copy(block(a, my_index), outs[a].at[my_index], local_sems.at[a])

        def start():
            for a in range(n_arr):
                mine(a).start()
                for k in range(1, 8):
                    copy(a, k, my_index).start()

        def finish():
            for a in range(n_arr):
                for k in range(1, 8):
                    copy(a, k, _index(*_peer(x, y, c, k))).wait_recv()
                    copy(a, k, my_index).wait_send()
                mine(a).wait()

        return start, (lambda: None), finish

    return _Comm(parts, [jax.ShapeDtypeStruct((N_DEV, p.shape[0] // N_DEV, p.shape[1]), p.dtype) for p in parts],
                 [pltpu.SemaphoreType.DMA((7 * n_arr,)), pltpu.SemaphoreType.DMA((7 * n_arr,)),
                  pltpu.SemaphoreType.DMA((n_arr,))], bind)


def run_comm(comm, name):
    n_in, n_out = len(comm.inputs), len(comm.out_shape)

    def body(*refs):
        start, relay, finish = comm.bind(refs[:n_in], refs[n_in:n_in + n_out], refs[n_in + n_out:])
        start()
        relay()
        finish()

    outs = pl.pallas_call(body, name=name, out_shape=comm.out_shape, in_specs=[ANY] * n_in,
                          out_specs=[ANY] * n_out, scratch_shapes=comm.scratch)(*comm.inputs)
    return list(outs)


def _chip_copy(r, src_ref, land_ref, send_sem, recv_sem):
    x, y, c = _my_place()
    return pltpu.make_async_remote_copy(src_ref=src_ref.at[r], dst_ref=land_ref.at[r - 1], send_sem=send_sem,
                                        recv_sem=recv_sem, device_id=(*_chip_of(x, y, r), c), device_id_type=MESH)


def chips_exchange_start(sums):
    def body(src_ref, land_ref, s1, s2, s3, r1, r2, r3, src_thru, land_thru, token):
        del src_thru, land_thru
        for r, send_sem, recv_sem in ((1, s1, r1), (2, s2, r2), (3, s3, r3)):
            _chip_copy(r, src_ref, land_ref, send_sem, recv_sem).start()
        token[...] = jnp.zeros_like(token)

    land = lax.empty((3,) + sums.shape[1:], sums.dtype)
    sem = pltpu.SemaphoreType.DMA(())
    hbm = pl.BlockSpec(memory_space=pltpu.HBM)
    sem_spec = pl.BlockSpec(memory_space=pltpu.SEMAPHORE)
    return pl.pallas_call(
        body, name="grad_exchange_chips_start",
        out_shape=(sem,) * 6 + (pltpu.HBM(sums.shape, sums.dtype), pltpu.HBM(land.shape, land.dtype),
                                jax.ShapeDtypeStruct((8, LANES), F32)),
        in_specs=(hbm, hbm), out_specs=(sem_spec,) * 6 + (hbm, hbm, pl.BlockSpec(memory_space=pltpu.VMEM)),
        input_output_aliases={0: 6, 1: 7},
        compiler_params=pltpu.CompilerParams(has_side_effects=pltpu.SideEffectType.DATAFLOW_SIDE_EFFECTING),
    )(pltpu.with_memory_space_constraint(sums, pltpu.HBM), pltpu.with_memory_space_constraint(land, pltpu.HBM))


def chips_exchange_wait(started, after):
    s1, s2, s3, r1, r2, r3, src_thru, land_thru, _ = started

    def body(src_ref, land_ref, s1, s2, s3, r1, r2, r3, after_ref, src_out, land_out):
        del after_ref, src_out, land_out
        for r, send_sem, recv_sem in ((1, s1, r1), (2, s2, r2), (3, s3, r3)):
            copy = _chip_copy(r, src_ref, land_ref, send_sem, recv_sem)
            copy.wait_send()
            copy.wait_recv()

    hbm = pl.BlockSpec(memory_space=pltpu.HBM)
    sem_spec = pl.BlockSpec(memory_space=pltpu.SEMAPHORE)
    return pl.pallas_call(
        body, name="grad_exchange_chips_wait",
        out_shape=(pltpu.HBM(src_thru.shape, src_thru.dtype), pltpu.HBM(land_thru.shape, land_thru.dtype)),
        in_specs=(hbm, hbm) + (sem_spec,) * 6 + (pl.BlockSpec(memory_space=pl.ANY),), out_specs=(hbm, hbm),
        input_output_aliases={0: 0, 1: 1},
        compiler_params=pltpu.CompilerParams(has_side_effects=pltpu.SideEffectType.DATAFLOW_SIDE_EFFECTING),
    )(src_thru, land_thru, s1, s2, s3, r1, r2, r3, after)


def gather_start(shard):
    def copy(k, src_ref, land_ref, send_sem, recv_sem):
        x, y, c = _my_place()
        return pltpu.make_async_remote_copy(src_ref=src_ref, dst_ref=land_ref.at[_index(x, y, c)], send_sem=send_sem,
                                            recv_sem=recv_sem, device_id=_peer(x, y, c, k), device_id_type=MESH)

    def start_body(src_ref, land_ref, *rest):
        sems, token = rest[:14], rest[16]
        for k in range(1, 8):
            copy(k, src_ref, land_ref, sems[k - 1], sems[7 + k - 1]).start()
        token[...] = jnp.zeros_like(token)

    land = lax.empty((N_DEV,) + shard.shape, shard.dtype)
    sem = pltpu.SemaphoreType.DMA(())
    hbm = pl.BlockSpec(memory_space=pltpu.HBM)
    sem_spec = pl.BlockSpec(memory_space=pltpu.SEMAPHORE)
    started = pl.pallas_call(
        start_body, name="allgather_late_grads_start",
        out_shape=(sem,) * 14 + (pltpu.HBM(shard.shape, shard.dtype), pltpu.HBM(land.shape, land.dtype),
                                 jax.ShapeDtypeStruct((8, LANES), F32)),
        in_specs=(hbm, hbm), out_specs=(sem_spec,) * 14 + (hbm, hbm, pl.BlockSpec(memory_space=pltpu.VMEM)),
        input_output_aliases={0: 14, 1: 15},
        compiler_params=pltpu.CompilerParams(has_side_effects=pltpu.SideEffectType.DATAFLOW_SIDE_EFFECTING),
    )(pltpu.with_memory_space_constraint(shard, pltpu.HBM), pltpu.with_memory_space_constraint(land, pltpu.HBM))
    return started, copy


def gather_wait(started, copy, after):
    sems, src_thru, land_thru = started[:14], started[14], started[15]

    def wait_body(src_ref, land_ref, *rest):
        for k in range(1, 8):
            cp = copy(k, src_ref, land_ref, rest[k - 1], rest[7 + k - 1])
            cp.wait_send()
            cp.wait_recv()

    hbm = pl.BlockSpec(memory_space=pltpu.HBM)
    sem_spec = pl.BlockSpec(memory_space=pltpu.SEMAPHORE)
    return pl.pallas_call(
        wait_body, name="allgather_late_grads_wait",
        out_shape=(pltpu.HBM(src_thru.shape, src_thru.dtype), pltpu.HBM(land_thru.shape, land_thru.dtype)),
        in_specs=(hbm, hbm) + (sem_spec,) * 14 + (pl.BlockSpec(memory_space=pl.ANY),), out_specs=(hbm, hbm),
        input_output_aliases={0: 0, 1: 1},
        compiler_params=pltpu.CompilerParams(has_side_effects=pltpu.SideEffectType.DATAFLOW_SIDE_EFFECTING),
    )(src_thru, land_thru, *sems, after)[1]


class _Hosted:
    def __init__(self, comms):
        self.comms = list(comms)
        self.inputs = [a for cm in self.comms for a in cm.inputs]
        self.out_shape = [s for cm in self.comms for s in cm.out_shape]
        self.scratch = [s for cm in self.comms for s in cm.scratch]
        self.in_specs = [ANY] * len(self.inputs)
        self.out_specs = [ANY] * len(self.out_shape)

    def split(self, refs, n_in, n_out, n_scratch):
        ni, no = len(self.inputs), len(self.out_shape)
        ins, rest = refs[:n_in], refs[n_in:]
        c_ins, rest = rest[:ni], rest[ni:]
        outs, rest = rest[:n_out], rest[n_out:]
        c_outs, rest = rest[:no], rest[no:]
        scratch, c_sems = rest[:n_scratch], rest[n_scratch:]
        phases = []
        for cm in self.comms:
            a, b, s = len(cm.inputs), len(cm.out_shape), len(cm.scratch)
            phases.append(cm.bind(c_ins[:a], c_outs[:b], c_sems[:s]))
            c_ins, c_outs, c_sems = c_ins[a:], c_outs[b:], c_sems[s:]
        return ins, outs, scratch, phases


def _before_step(phases, step, n_steps):
    if not phases:
        return

    @pl.when(step == 0)
    def _():
        for start, _, _ in phases:
            start()

    @pl.when(step == n_steps // 2)
    def _():
        for _, relay, _ in phases:
            relay()


def _after_step(phases, step, n_steps):
    if not phases:
        return

    @pl.when(step == n_steps - 1)
    def _():
        for _, _, finish in phases:
            finish()


def pair_exchange_sum(part, name):
    m, n = part.shape[0] // N_DEV, part.shape[1]

    def body(part_ref, out_ref, got, mine, summed, send_sems, recv_sems, in_sems, out_sems):
        x, y, c = _my_place()

        def rows(r, core):
            owner = _index(*_chip_of(x, y, r), core)
            return part_ref.at[pl.ds(pl.multiple_of(owner * m, 16), m), :]

        def to_sibling(r):
            return pltpu.make_async_remote_copy(src_ref=rows(r, 1 - c), dst_ref=got.at[r], send_sem=send_sems.at[r],
                                                recv_sem=recv_sems.at[r], device_id=(x, y, 1 - c), device_id_type=MESH)

        def fetch(r):
            return pltpu.make_async_copy(rows(r, c), mine.at[r % 2], in_sems.at[r % 2])

        def put(r):
            return pltpu.make_async_copy(summed.at[r % 2], out_ref.at[r], out_sems.at[r % 2])

        for r in range(4):
            to_sibling(r).start()
        fetch(0).start()
        for r in range(4):
            if r + 1 < 4:
                fetch(r + 1).start()
            fetch(r).wait()
            to_sibling(r).wait_recv()
            if r >= 2:
                put(r - 2).wait()
            summed[r % 2] = (mine[r % 2].astype(F32) + got[r].astype(F32)).astype(summed.dtype)
            put(r).start()
        for r in (2, 3):
            put(r).wait()
        for r in range(4):
            to_sibling(r).wait_send()

    return pl.pallas_call(
        body, name=name, out_shape=jax.ShapeDtypeStruct((4, m, n), part.dtype),
        in_specs=[ANY], out_specs=ANY,
        scratch_shapes=[pltpu.VMEM((4, m, n), part.dtype), pltpu.VMEM((2, m, n), part.dtype),
                        pltpu.VMEM((2, m, n), part.dtype), pltpu.SemaphoreType.DMA((4,)),
                        pltpu.SemaphoreType.DMA((4,)), pltpu.SemaphoreType.DMA((2,)), pltpu.SemaphoreType.DMA((2,))],
        compiler_params=_cparams(),
    )(part)


def in_proj_gather(x, pos_col, freq, sign, g_pre, wt_shard, bias):
    s, d = x.shape
    tm = _tile(s, (512, 256, 128))
    nt = s // tm
    tc = _tile(s, (256, 128))
    nc = s // tc
    m = wt_shard.shape[0]
    half = D_IN // 2
    xi = lax.axis_index("x")
    order = jnp.stack([xi, 1 - xi]).astype(jnp.int32)

    def body(order_ref, x_hbm, pos_hbm, freq_ref, sign_ref, g_ref, b_ref, shard_ref,
             proj_ref, h_hbm, rope_hbm, wt_ref,
             w_vmem, h_vmem, xbuf, posbuf, ropebuf, send_sems, recv_sems, local_sems, in_sems, out_sems):
        del order_ref
        p, i = pl.program_id(0), pl.program_id(1)
        xx, yy, cc = _my_place()
        me, sibling = (xx, yy, cc), (xx, yy, 1 - cc)
        chips = [_chip_of(xx, yy, r) for r in (1, 2, 3)]

        def rows(px, py, pc):
            return wt_ref.at[pl.ds(pl.multiple_of(_index(px, py, pc) * m, 16), m), :]

        def copy(k, block, to, src=None):
            return pltpu.make_async_remote_copy(
                src_ref=rows(*block) if src is None else src, dst_ref=rows(*block),
                send_sem=send_sems.at[k], recv_sem=recv_sems.at[k], device_id=to, device_id_type=MESH)

        def mine():
            return pltpu.make_async_copy(shard_ref, rows(*me), local_sems.at[0])

        def to_sibling():
            return copy(0, me, sibling, src=shard_ref)

        def to_chip(j):
            return copy(1 + j, me, (*chips[j], cc), src=shard_ref)

        def relay(j):
            copy(1 + j, (*chips[j], cc), me).wait_recv()
            copy(4 + j, (*chips[j], cc), sibling).start()

        def relayed(j):
            copy(4 + j, (*chips[j], 1 - cc), me).wait_recv()

        def load_half(which, slot):
            rows_of_half = wt_ref.at[pl.ds(pl.multiple_of(which * half, 16), half), :]
            load = pltpu.make_async_copy(rows_of_half, w_vmem.at[slot], local_sems.at[1 + slot])
            load.start()
            load.wait()

        def piece(ref, c):
            return ref.at[pl.ds(c * tc, tc), :]

        def fetch(c):
            return (pltpu.make_async_copy(piece(x_hbm, c), xbuf.at[c % 2], in_sems.at[c % 2]),
                    pltpu.make_async_copy(piece(pos_hbm, c), posbuf.at[c % 2], in_sems.at[2 + c % 2]))

        def put(c):
            return (pltpu.make_async_copy(piece(h_vmem, c), piece(h_hbm, c), out_sems.at[c % 2]),
                    pltpu.make_async_copy(ropebuf.at[c % 2], piece(rope_hbm, c), out_sems.at[2 + c % 2]))

        def prologue():
            for cp in fetch(0):
                cp.start()
            for c in range(nc):
                if c + 1 < nc:
                    for cp in fetch(c + 1):
                        cp.start()
                for cp in fetch(c):
                    cp.wait()
                if c >= 2:
                    for cp in put(c - 2):
                        cp.wait()
                xv = xbuf[c % 2]
                r = lax.rsqrt(jnp.mean(xv * xv, axis=-1, keepdims=True) + EPS)
                h_vmem[c * tc:(c + 1) * tc, :] = (xv * r * g_ref[...]).astype(BF16)
                ang = posbuf[c % 2].astype(F32) * freq_ref[...]
                ropebuf[c % 2, :, :LANES] = jnp.cos(ang)
                ropebuf[c % 2, :, LANES:] = jnp.sin(ang) * sign_ref[...]
                for cp in put(c):
                    cp.start()
            for c in range(max(nc - 2, 0), nc):
                for cp in put(c):
                    cp.wait()

        @pl.when(jnp.logical_and(p == 0, i == 0))
        def _():
            mine().start()
            to_sibling().start()
            to_chip(1).start()
            to_chip(0).start()
            prologue()
            copy(0, sibling, me).wait_recv()
            relay(1)
            relayed(1)
            mine().wait()
            to_chip(1).wait_send()
            to_chip(0).wait_send()
            to_chip(2).start()
            load_half(xx, 0)

        @pl.when(jnp.logical_and(p == 1, i == 0))
        def _():
            relayed(0)
            relayed(2)
            load_half(1 - xx, 1)

        def project(slot):
            hb = h_vmem[pl.ds(pl.multiple_of(i * tm, tm), tm), :]
            proj_ref[...] = (_dot(hb, w_vmem[slot], 1, 1) + b_ref[...]).astype(BF16)

        @pl.when(p == 0)
        def _():
            project(0)

        @pl.when(p == 1)
        def _():
            project(1)

        @pl.when(jnp.logical_and(p == 0, i == 1))
        def _():
            relay(0)

        @pl.when(jnp.logical_and(p == 0, i == nt - 1))
        def _():
            relay(2)

        @pl.when(jnp.logical_and(p == 1, i == nt - 1))
        def _():
            to_sibling().wait_send()
            to_chip(2).wait_send()
            for j in range(3):
                copy(4 + j, (*chips[j], cc), sibling).wait_send()

    const = lambda p, i, o: (0, 0)
    return pl.pallas_call(
        body, name="in_proj_gather",
        grid_spec=pltpu.PrefetchScalarGridSpec(
            num_scalar_prefetch=1, grid=(2, nt),
            in_specs=[ANY, ANY,
                      pl.BlockSpec((1, LANES), const),
                      pl.BlockSpec((1, LANES), const),
                      pl.BlockSpec((1, d), const),
                      pl.BlockSpec((1, half), lambda p, i, o: (0, o[p])),
                      ANY],
            out_specs=[pl.BlockSpec((tm, half), lambda p, i, o: (i, o[p])), ANY, ANY, ANY],
            scratch_shapes=[pltpu.VMEM((2, half, d), BF16), pltpu.VMEM((s, d), BF16),
                            pltpu.VMEM((2, tc, d), F32), pltpu.VMEM((2, tc, 1), jnp.int32),
                            pltpu.VMEM((2, tc, 2 * LANES), F32),
                            pltpu.SemaphoreType.DMA((7,)), pltpu.SemaphoreType.DMA((7,)),
                            pltpu.SemaphoreType.DMA((3,)), pltpu.SemaphoreType.DMA((4,)),
                            pltpu.SemaphoreType.DMA((4,))]),
        out_shape=[jax.ShapeDtypeStruct((s, D_IN), BF16), jax.ShapeDtypeStruct((s, d), BF16),
                   jax.ShapeDtypeStruct((s, 2 * LANES), F32), jax.ShapeDtypeStruct((D_IN, d), BF16)],
        compiler_params=pltpu.CompilerParams(dimension_semantics=("arbitrary", "arbitrary"),
                                             vmem_limit_bytes=IN_PROJ_VMEM_LIMIT),
    )(order, x, pos_col, freq, sign, g_pre, bias, wt_shard)


def out_proj_loss(cat, w_out, x, target, g_post):
    s, d = x.shape
    tm = _tile(s, (256, 128))
    kc = _tile(d, (512, 128))
    pieces = w_out.shape[0] // kc

    def body(cat_ref, w_hbm, x_ref, t_ref, g_ref, dy_ref, dout_ref, dg_ref, loss_ref, w_ref, w_sems):
        step = pl.program_id(0)

        def w_load(j):
            return pltpu.make_async_copy(w_hbm.at[j * kc:(j + 1) * kc, :], w_ref.at[j * kc:(j + 1) * kc, :], w_sems.at[j])

        @pl.when(step == 0)
        def _():
            for j in range(pieces):
                w_load(j).start()
            dg_ref[...] = jnp.zeros_like(dg_ref)
            loss_ref[...] = jnp.zeros_like(loss_ref)
            ys = [None] * (tm // CHUNK)
            for j in range(pieces):
                w_load(j).wait()
                for c in range(tm // CHUNK):
                    part = _dot(cat_ref[c * CHUNK:(c + 1) * CHUNK, j * kc:(j + 1) * kc], w_ref[j * kc:(j + 1) * kc, :], 1, 0)
                    ys[c] = part if ys[c] is None else ys[c] + part
            loss_and_back(ys, x_ref, t_ref, g_ref, dy_ref, dout_ref, dg_ref, loss_ref)

        @pl.when(step > 0)
        def _():
            ys = [_dot(cat_ref[c0:c0 + CHUNK, :], w_ref[...], 1, 0) for c0 in range(0, tm, CHUNK)]
            loss_and_back(ys, x_ref, t_ref, g_ref, dy_ref, dout_ref, dg_ref, loss_ref)

    def loss_and_back(ys, x_ref, t_ref, g_ref, dy_ref, dout_ref, dg_ref, loss_ref):
        g = g_ref[...]
        for c0 in range(0, tm, CHUNK):
            rows = slice(c0, c0 + CHUNK)
            yv = ys[c0 // CHUNK]
            r = lax.rsqrt(jnp.mean(yv * yv, axis=-1, keepdims=True) + EPS)
            nrm = yv * r
            err = x_ref[rows, :] + nrm * g - t_ref[rows, :]
            loss_ref[...] += 0.5 * jnp.sum(jnp.sum(err * err, axis=-1, keepdims=True), axis=0, keepdims=True) / d
            dout = err * (1.0 / d)
            dout_ref[rows, :] = dout
            dg_ref[...] += jnp.sum(dout * nrm, axis=0, keepdims=True)
            dn = dout * g
            dy = r * (dn - nrm * jnp.mean(dn * nrm, axis=-1, keepdims=True))
            dy_ref[rows, :] = dy.astype(BF16)

    return pl.pallas_call(
        body, name="out_proj_loss", grid=(s // tm,),
        in_specs=[pl.BlockSpec((tm, d), lambda i: (i, 0)),
                  ANY,
                  pl.BlockSpec((tm, d), lambda i: (i, 0)),
                  pl.BlockSpec((tm, d), lambda i: (i, 0)),
                  pl.BlockSpec((1, d), lambda i: (0, 0))],
        out_specs=[pl.BlockSpec((tm, d), lambda i: (i, 0)),
                   pl.BlockSpec((tm, d), lambda i: (i, 0)),
                   pl.BlockSpec((1, d), lambda i: (0, 0)),
                   pl.BlockSpec((1, LANES), lambda i: (0, 0))],
        out_shape=[jax.ShapeDtypeStruct((s, d), BF16), jax.ShapeDtypeStruct((s, d), F32),
                   jax.ShapeDtypeStruct((1, d), F32), jax.ShapeDtypeStruct((1, LANES), F32)],
        scratch_shapes=[pltpu.VMEM(w_out.shape, w_out.dtype), pltpu.SemaphoreType.DMA((pieces,))],
        compiler_params=_cparams(("arbitrary",)),
    )(cat, w_out, x, target, g_post)


def matmul_nt(a, b, name):
    m, k = a.shape
    n = b.shape[0]
    tm = _tile(m, (512, 256, 128))

    def body(a_ref, b_ref, o_ref):
        o_ref[...] = _dot(a_ref[...], b_ref[...], 1, 1).astype(o_ref.dtype)

    return pl.pallas_call(
        body, name=name, grid=(m // tm,),
        in_specs=[pl.BlockSpec((tm, k), lambda i: (i, 0)), pl.BlockSpec((n, k), lambda i: (0, 0))],
        out_specs=pl.BlockSpec((tm, n), lambda i: (i, 0)),
        out_shape=jax.ShapeDtypeStruct((m, n), BF16),
        compiler_params=_cparams(("arbitrary",)),
    )(a, b)


def matmul_tn(a, b, tm, name, comms=()):
    k, m = a.shape
    n = b.shape[1]
    steps = m // tm
    hosted = _Hosted(comms)

    kc = _tile(k, (1024, 128))
    pieces = k // kc

    def body(*refs):
        (a_ref, b_hbm), (o_ref, cs_ref), (b_ref, b_sems), phases = hosted.split(refs, 2, 2, 2)
        step = pl.program_id(0)
        _before_step(phases, step, steps)

        def b_load(j):
            return pltpu.make_async_copy(b_hbm.at[j * kc:(j + 1) * kc, :], b_ref.at[j * kc:(j + 1) * kc, :], b_sems.at[j])

        @pl.when(step == 0)
        def _():
            for j in range(pieces):
                b_load(j).start()
            acc = None
            for j in range(pieces):
                b_load(j).wait()
                part = _dot(a_ref[j * kc:(j + 1) * kc, :], b_ref[j * kc:(j + 1) * kc, :], 0, 0)
                acc = part if acc is None else acc + part
            o_ref[...] = acc.astype(o_ref.dtype)

        @pl.when(step > 0)
        def _():
            o_ref[...] = _dot(a_ref[...], b_ref[...], 0, 0).astype(o_ref.dtype)

        rows = _tile(k, (512, 128))
        cs = jnp.zeros((1, tm), F32)
        for r0 in range(0, k, rows):
            cs = cs + jnp.sum(a_ref[r0:r0 + rows, :].astype(F32), axis=0, keepdims=True)
        cs_ref[...] = cs
        _after_step(phases, step, steps)

    return pl.pallas_call(
        body, name=name, grid=(steps,),
        in_specs=[pl.BlockSpec((k, tm), lambda i: (0, i)), ANY] + hosted.in_specs,
        out_specs=[pl.BlockSpec((tm, n), lambda i: (i, 0)), pl.BlockSpec((1, tm), lambda i: (0, i))] + hosted.out_specs,
        out_shape=[jax.ShapeDtypeStruct((m, n), BF16), jax.ShapeDtypeStruct((1, m), F32)] + hosted.out_shape,
        scratch_shapes=[pltpu.VMEM((k, n), b.dtype), pltpu.SemaphoreType.DMA((pieces,))] + hosted.scratch,
        compiler_params=_cparams(("arbitrary",)),
    )(a, b, *hosted.inputs)


def in_proj_bwd(dproj, wt, x, g_pre, dout, comms=()):
    s, d = x.shape
    tm = _tile(s, (512, 256, 128))
    steps = s // tm
    nsub = tm // CHUNK
    kw = 8 * LANES
    kchunks = [(k0, kw) for k0 in range(0, D_IN - D_IN % kw, kw)]
    if D_IN % kw:
        kchunks.append((D_IN - D_IN % kw, D_IN % kw))
    ksplit = len(kchunks)
    hosted = _Hosted(comms)

    def body(*refs):
        ((*dp_refs, w_hbm, x_hbm, g_ref, dout_hbm), (gx_hbm, dg_ref),
         (w_ref, w_sems, xbuf, dbuf, gbuf, in_sems, out_sems), phases) = hosted.split(refs, 4 + ksplit, 2, 7)
        step = pl.program_id(0)
        _before_step(phases, step, steps)

        def rows_of(ref, c):
            return ref.at[pl.ds(pl.multiple_of(step * tm + c * CHUNK, CHUNK), CHUNK), :]

        def fetches(c):
            return (pltpu.make_async_copy(rows_of(x_hbm, c), xbuf.at[c], in_sems.at[c]),
                    pltpu.make_async_copy(rows_of(dout_hbm, c), dbuf.at[c], in_sems.at[nsub + c]))

        def put(c):
            return pltpu.make_async_copy(gbuf.at[c % 2], rows_of(gx_hbm, c), out_sems.at[c % 2])

        for c in range(nsub):
            for cp in fetches(c):
                cp.start()

        def w_load(j):
            k0, kw = kchunks[j]
            return pltpu.make_async_copy(w_hbm.at[k0:k0 + kw, :], w_ref.at[k0:k0 + kw, :], w_sems.at[j])

        @pl.when(step == 0)
        def _():
            dg_ref[...] = jnp.zeros_like(dg_ref)
            for j in range(ksplit):
                w_load(j).start()

        dh_all = None
        for j, ((k0, kw), dp_ref) in enumerate(zip(kchunks, dp_refs)):
            @pl.when(step == 0)
            def _():
                w_load(j).wait()

            part = _dot(dp_ref[...], w_ref[k0:k0 + kw, :], 1, 0)
            dh_all = part if dh_all is None else dh_all + part
        for c in range(nsub):
            for cp in fetches(c):
                cp.wait()
            if c >= 2:
                put(c - 2).wait()
            elif c < nsub:
                @pl.when(step > 0)
                def _():
                    put(max(nsub - 2, 0) + c).wait()
            dh = dh_all[c * CHUNK:(c + 1) * CHUNK, :]
            xv = xbuf[c]
            r = lax.rsqrt(jnp.mean(xv * xv, axis=-1, keepdims=True) + EPS)
            xn = xv * r
            dg_ref[...] += jnp.sum(dh * xn, axis=0, keepdims=True)
            dn = dh * g_ref[...]
            gbuf[c % 2] = dbuf[c] + r * (dn - xn * jnp.mean(dn * xn, axis=-1, keepdims=True))
            put(c).start()
        @pl.when(step == steps - 1)
        def _():
            for c in range(max(nsub - 2, 0), nsub):
                put(c).wait()

        _after_step(phases, step, steps)

    side_in, side_out = pltpu.VMEM((nsub, CHUNK, d), F32), pltpu.VMEM((2, CHUNK, d), F32)
    row = pl.BlockSpec((1, d), lambda i: (0, 0))
    return pl.pallas_call(
        body, name="in_proj_bwd", grid=(steps,),
        in_specs=[pl.BlockSpec((tm, kw), functools.partial(lambda j, i: (i, j), k0 // kw)) for k0, kw in kchunks]
        + [ANY, ANY, row, ANY] + hosted.in_specs,
        out_specs=[ANY, row] + hosted.out_specs,
        out_shape=[jax.ShapeDtypeStruct((s, d), F32), jax.ShapeDtypeStruct((1, d), F32)] + hosted.out_shape,
        scratch_shapes=[pltpu.VMEM((D_IN, d), BF16), pltpu.SemaphoreType.DMA((ksplit,)), side_in, side_in, side_out,
                        pltpu.SemaphoreType.DMA((2 * nsub,)), pltpu.SemaphoreType.DMA((2,))] + hosted.scratch,
        compiler_params=_cparams(("arbitrary",)),
    )(*([dproj] * ksplit), wt, x, g_pre, dout, *hosted.inputs)


def _lane_iota(shape):
    return lax.broadcasted_iota(jnp.int32, shape, len(shape) - 1)


def _partner(v):
    low = (_lane_iota(v.shape) % HEAD_DIM) < (HEAD_DIM // 2)
    return jnp.where(low, pltpu.roll(v, LANES - HEAD_DIM // 2, 1), pltpu.roll(v, HEAD_DIM // 2, 1))


def _rope(v, cos, sin_signed):
    return v * cos + _partner(v) * sin_signed


def _rope_transposed(dv, cos, sin_signed):
    return dv * cos - _partner(dv) * sin_signed


def _both_halves(v, kv_head):
    keep = (_lane_iota(v.shape) >= HEAD_DIM) if kv_head else (_lane_iota(v.shape) < HEAD_DIM)
    return jnp.where(keep, v, pltpu.roll(v, HEAD_DIM, 1))


def _fold_halves(acc):
    return acc + pltpu.roll(acc, HEAD_DIM, 1)


def _by_half(a, b):
    shape = jnp.broadcast_shapes(jnp.shape(a), jnp.shape(b))
    return jnp.where(_lane_iota(shape) < HEAD_DIM, a, b)


def _stack_heads(pair):
    return jnp.concatenate([_by_half(pair, 0.0), _by_half(0.0, pair)], axis=0)


def _band_bias(has_prev):
    i = lax.broadcasted_iota(jnp.int32, (2 * CHUNK, 2 * CHUNK), 0) % CHUNK
    j = lax.broadcasted_iota(jnp.int32, (2 * CHUNK, 2 * CHUNK), 1)
    band = jnp.logical_and(j > i, j <= i + CHUNK)
    return jnp.where(jnp.logical_and(band, jnp.logical_or(j >= CHUNK, has_prev)), 0.0, NEG)


def _probs_staged(qm2s, kk2s, bias, sink_cols):
    k = range(len(qm2s))
    scs = [_dot(qm2s[i], kk2s[i], 1, 1) + bias for i in k]
    mxs = [jnp.maximum(jnp.max(scs[i], axis=-1, keepdims=True), sink_cols[i]) for i in k]
    ps = [jnp.exp(scs[i] - mxs[i]) for i in k]
    ess = [jnp.exp(sink_cols[i] - mxs[i]) for i in k]
    invs = [1.0 / (jnp.sum(ps[i], axis=-1, keepdims=True) + ess[i]) for i in k]
    return [ps[i] * invs[i] for i in k], [ess[i] * invs[i] for i in k]


def _sink_col(sinks_ref, pair):
    row = lax.broadcasted_iota(jnp.int32, (2 * CHUNK, 1), 0)
    return jnp.where(row < CHUNK, sinks_ref[2 * pair], sinks_ref[2 * pair + 1])


def _layer_norm_parts(v):
    mu = jnp.mean(v, axis=-1, keepdims=True)
    xc = v - mu
    rstd = lax.rsqrt(jnp.mean(xc * xc, axis=-1, keepdims=True) + EPS)
    return xc * rstd, rstd


def _masked_spatial(w_ref, g):
    t = lax.broadcasted_iota(jnp.int32, (CHUNK, CHUNK), 0)
    sidx = lax.broadcasted_iota(jnp.int32, (CHUNK, CHUNK), 1)
    return jnp.where(t >= sidx, w_ref[g], 0.0).astype(BF16)


def _keys_values(kv_ref, kvp_ref, rope_ref, ropep_ref):
    cos_c, sin_c = rope_ref[:, :LANES], rope_ref[:, LANES:]
    cos_p, sin_p = ropep_ref[:, :LANES], ropep_ref[:, LANES:]
    k_c = _rope(kv_ref[:, :D_KV].astype(F32), cos_c, sin_c)
    k_p = _rope(kvp_ref[:, :D_KV].astype(F32), cos_p, sin_p)
    keys = jnp.concatenate([k_p, k_c], axis=0)
    vals = jnp.concatenate([kvp_ref[:, D_KV:], kv_ref[:, D_KV:]], axis=0).astype(F32)
    return keys, vals, (cos_c, sin_c, cos_p, sin_p)


def mixer_fwd(proj, rope, ln_g, ln_b, w_sp, b_sp_rows, sinks, comms=()):
    s = proj.shape[0]
    nb = s // CHUNK
    hosted = _Hosted(comms)

    def body(sinks_ref, *refs):
        ((proj_ref, kvp_ref, rope_ref, ropep_ref, lng_ref, lnb_ref, w_ref, b_ref), (cat_ref, p_ref), _,
         phases) = hosted.split(refs, 8, 2, 0)
        n = pl.program_id(0)
        _before_step(phases, n, nb)
        xhat, _ = _layer_norm_parts(proj_ref[:, OFF_V:OFF_V + D_GMLP].astype(F32))
        vnb = (xhat * lng_ref[...] + lnb_ref[...]).astype(BF16)
        mixeds = [_dot(_masked_spatial(w_ref, g), vnb[:, g * CHUNK:(g + 1) * CHUNK], 1, 0) + b_ref[g]
                  for g in range(GROUPS)]
        for g in range(GROUPS):
            za = proj_ref[:, OFF_ZA + g * CHUNK:OFF_ZA + (g + 1) * CHUNK].astype(F32)
            u = proj_ref[:, OFF_U + g * CHUNK:OFF_U + (g + 1) * CHUNK].astype(F32)
            cat_ref[:, g * CHUNK:(g + 1) * CHUNK] = (u * mixeds[g] * (za * _sigmoid(za))).astype(BF16)
        kv_ref = proj_ref.at[:, OFF_K:OFF_K + 2 * D_KV]
        keys, vals, (cos_c, sin_c, _, _) = _keys_values(kv_ref, kvp_ref, rope_ref, ropep_ref)
        cos_q, sin_q = cos_c * SCALE, sin_c * SCALE
        bias = _band_bias(n > 0)
        kk2 = [_both_halves(keys, kvh).astype(BF16) for kvh in range(N_KV_HEADS)]
        vv2 = [_both_halves(vals, kvh).astype(BF16) for kvh in range(N_KV_HEADS)]
        pairs = range(N_PAIRS)
        qms = [_stack_heads(_rope(proj_ref[:, OFF_Q + pair * LANES:OFF_Q + (pair + 1) * LANES].astype(F32),
                                  cos_q, sin_q)).astype(BF16) for pair in pairs]
        probs, sink_probs = _probs_staged(qms, [kk2[pair // PAIRS_PER_KV] for pair in pairs], bias,
                                          [_sink_col(sinks_ref, pair) for pair in pairs])
        pbs = [p.astype(BF16) for p in probs]
        outs = [_dot(pbs[pair], vv2[pair // PAIRS_PER_KV], 1, 0) for pair in pairs]
        first_col = _lane_iota((2 * CHUNK, 2 * CHUNK)) == 0
        for pair in pairs:
            p_ref[0, pair] = jnp.where(first_col, sink_probs[pair].astype(BF16), pbs[pair])
        for pair in pairs:
            out_pair = _by_half(outs[pair][:CHUNK], outs[pair][CHUNK:])
            zb = proj_ref[:, OFF_ZB + pair * LANES:OFF_ZB + (pair + 1) * LANES].astype(F32)
            cat_ref[:, D_GMLP + pair * LANES:D_GMLP + (pair + 1) * LANES] = (
                out_pair * (zb * _sigmoid(zb))).astype(BF16)
        _after_step(phases, n, nb)

    prev = lambda n, *_: (jnp.maximum(n - 1, 0), 0)
    kv_block = OFF_K // (2 * D_KV)
    return pl.pallas_call(
        body, name="mixer_fwd",
        grid_spec=pltpu.PrefetchScalarGridSpec(
            num_scalar_prefetch=1, grid=(nb,),
            in_specs=[pl.BlockSpec((CHUNK, D_IN), lambda n, *_: (n, 0)),
                      pl.BlockSpec((CHUNK, 2 * D_KV), lambda n, *_: (jnp.maximum(n - 1, 0), kv_block)),
                      pl.BlockSpec((CHUNK, 2 * LANES), lambda n, *_: (n, 0)),
                      pl.BlockSpec((CHUNK, 2 * LANES), prev),
                      pl.BlockSpec((1, D_GMLP), lambda n, *_: (0, 0)),
                      pl.BlockSpec((1, D_GMLP), lambda n, *_: (0, 0)),
                      pl.BlockSpec((GROUPS, CHUNK, CHUNK), lambda n, *_: (0, 0, 0)),
                      pl.BlockSpec((GROUPS, CHUNK, CHUNK), lambda n, *_: (0, 0, 0))] + hosted.in_specs,
            out_specs=[pl.BlockSpec((CHUNK, D_GMLP + D_ATTN), lambda n, *_: (n, 0)),
                       pl.BlockSpec((1, N_PAIRS, 2 * CHUNK, 2 * CHUNK), lambda n, *_: (n, 0, 0, 0))]
            + hosted.out_specs,
            scratch_shapes=hosted.scratch),
        out_shape=[jax.ShapeDtypeStruct((s, D_GMLP + D_ATTN), BF16),
                   jax.ShapeDtypeStruct((nb, N_PAIRS, 2 * CHUNK, 2 * CHUNK), BF16)] + hosted.out_shape,
        compiler_params=_cparams(("arbitrary",)),
    )(sinks, proj, proj, rope, rope, ln_g, ln_b, w_sp, b_sp_rows, *hosted.inputs)


def mixer_bwd(proj, dcat, probs, rope, ln_g, ln_b, w_sp, b_sp_rows, comms=()):
    s = proj.shape[0]
    nb = s // CHUNK
    hosted = _Hosted(comms)

    def body(*refs):
        ((proj_ref, kvp_ref, dcat_ref, p_ref, rope_ref, ropep_ref, lng_ref, lnb_ref, w_ref, b_ref),
         (dproj_ref, dw_ref, db_ref, dlng_ref, dlnb_ref, dsink_ref),
         (pend_ref, pend_kv_ref, dbacc_ref), phases) = hosted.split(refs, 10, 6, 3)
        n = pl.program_id(0)
        _before_step(phases, n, nb + 1)

        @pl.when(n == 0)
        def _():
            dw_ref[...] = jnp.zeros_like(dw_ref)
            dbacc_ref[...] = jnp.zeros_like(dbacc_ref)
            dlng_ref[...] = jnp.zeros_like(dlng_ref)
            dlnb_ref[...] = jnp.zeros_like(dlnb_ref)
            dsink_ref[...] = jnp.zeros_like(dsink_ref)

        @pl.when(n > 0)
        def _():
            dproj_ref[...] = pend_ref[...]

        def flush(dkv_prev):
            @pl.when(n > 0)
            def _():
                dproj_ref[:, OFF_K:OFF_K + 2 * D_KV] = (pend_kv_ref[...] + dkv_prev).astype(BF16)

        @pl.when(n < nb)
        def _():
            kv_ref = proj_ref.at[:, OFF_K:OFF_K + 2 * D_KV]
            keys, vals, (cos_c, sin_c, cos_p, sin_p) = _keys_values(kv_ref, kvp_ref, rope_ref, ropep_ref)
            cos_q, sin_q = cos_c * SCALE, sin_c * SCALE
            first_col = _lane_iota((2 * CHUNK, 2 * CHUNK)) == 0
            lane_row = _lane_iota((1, LANES))
            dsink = jnp.zeros((1, LANES), F32)
            dk_heads, dv_heads = [], []
            for kvh in range(N_KV_HEADS):
                kk2 = _both_halves(keys, kvh).astype(BF16)
                vv2 = _both_halves(vals, kvh).astype(BF16)
                pairs = list(range(kvh * PAIRS_PER_KV, (kvh + 1) * PAIRS_PER_KV))
                k4 = range(PAIRS_PER_KV)
                qm2s = [_stack_heads(_rope(proj_ref[:, OFF_Q + pair * LANES:OFF_Q + (pair + 1) * LANES].astype(F32),
                                           cos_q, sin_q)).astype(BF16) for pair in pairs]
                kept = [p_ref[0, pair] for pair in pairs]
                pbs = [jnp.where(first_col, jnp.zeros_like(kp), kp) for kp in kept]
                ps = [pb.astype(F32) for pb in pbs]
                p_sinks = [kp[:, 0:1].astype(F32) for kp in kept]
                o2s = [_dot(pb, vv2, 1, 0) for pb in pbs]
                zbs = [proj_ref[:, OFF_ZB + pair * LANES:OFF_ZB + (pair + 1) * LANES].astype(F32) for pair in pairs]
                sgs = [_sigmoid(zb) for zb in zbs]
                dybs = [dcat_ref[:, D_GMLP + pair * LANES:D_GMLP + (pair + 1) * LANES].astype(F32) for pair in pairs]
                for i, pair in enumerate(pairs):
                    out_pair = _by_half(o2s[i][:CHUNK], o2s[i][CHUNK:])
                    pend_ref[:, OFF_ZB + pair * LANES:OFF_ZB + (pair + 1) * LANES] = (
                        dybs[i] * out_pair * (sgs[i] * (1.0 + zbs[i] * (1.0 - sgs[i])))).astype(BF16)
                dom2s = [_stack_heads(dybs[i] * (zbs[i] * sgs[i])).astype(BF16) for i in k4]
                dps = [_dot(dom2, vv2, 1, 1) for dom2 in dom2s]
                deltas = [jnp.sum(ps[i] * dps[i], axis=-1, keepdims=True) for i in k4]
                dss = [ps[i] * (dps[i] - deltas[i]) for i in k4]
                for i, pair in enumerate(pairs):
                    dsk = -(p_sinks[i] * deltas[i])
                    dsink = dsink + jnp.where(lane_row == 2 * pair,
                                              jnp.sum(dsk[:CHUNK], axis=0, keepdims=True), 0.0)
                    dsink = dsink + jnp.where(lane_row == 2 * pair + 1,
                                              jnp.sum(dsk[CHUNK:], axis=0, keepdims=True), 0.0)
                dsbs = [ds.astype(BF16) for ds in dss]
                dq2s = [_dot(dsb, kk2, 1, 0) for dsb in dsbs]
                for pair, dq2 in zip(pairs, dq2s):
                    pend_ref[:, OFF_Q + pair * LANES:OFF_Q + (pair + 1) * LANES] = _rope_transposed(
                        _by_half(dq2[:CHUNK], dq2[CHUNK:]), cos_q, sin_q).astype(BF16)
                dkks = [_dot(dsbs[i], qm2s[i], 0, 0) for i in k4]
                dvvs = [_dot(pbs[i], dom2s[i], 0, 0) for i in k4]
                dk_heads.append(_fold_halves((dkks[0] + dkks[1]) + (dkks[2] + dkks[3])))
                dv_heads.append(_fold_halves((dvvs[0] + dvvs[1]) + (dvvs[2] + dvvs[3])))
            dk_rot = _by_half(dk_heads[0], dk_heads[1])
            dv_all = _by_half(dv_heads[0], dv_heads[1])
            dk_p = _rope_transposed(dk_rot[:CHUNK], cos_p, sin_p)
            dk_c = _rope_transposed(dk_rot[CHUNK:], cos_c, sin_c)
            flush(jnp.concatenate([dk_p, dv_all[:CHUNK]], axis=1))
            dsink_ref[...] += dsink
            pend_kv_ref[...] = jnp.concatenate([dk_c, dv_all[CHUNK:]], axis=1)
            xhat, rstd = _layer_norm_parts(proj_ref[:, OFF_V:OFF_V + D_GMLP].astype(F32))
            lng = lng_ref[...]
            vnb = (xhat * lng + lnb_ref[...]).astype(BF16)
            dvn_cols = []
            for g in range(GROUPS):
                cols = slice(g * CHUNK, (g + 1) * CHUNK)
                wm = _masked_spatial(w_ref, g)
                mixed = _dot(wm, vnb[:, cols], 1, 0) + b_ref[g]
                za = proj_ref[:, OFF_ZA + g * CHUNK:OFF_ZA + (g + 1) * CHUNK].astype(F32)
                u = proj_ref[:, OFF_U + g * CHUNK:OFF_U + (g + 1) * CHUNK].astype(F32)
                dya = dcat_ref[:, cols].astype(F32)
                sg = _sigmoid(za)
                sz = za * sg
                pend_ref[:, OFF_U + g * CHUNK:OFF_U + (g + 1) * CHUNK] = (dya * mixed * sz).astype(BF16)
                pend_ref[:, OFF_ZA + g * CHUNK:OFF_ZA + (g + 1) * CHUNK] = (
                    dya * u * mixed * (sg * (1.0 + za * (1.0 - sg)))).astype(BF16)
                dmixed = dya * u * sz
                dmb = dmixed.astype(BF16)
                dbacc_ref[g] += dmixed
                dw_ref[g] += _dot(dmb, vnb[:, cols], 1, 1)
                dvn_cols.append(_dot(wm, dmb, 0, 0))
            dvn = jnp.concatenate(dvn_cols, axis=1)
            dlng_ref[...] += jnp.sum(dvn * xhat, axis=0, keepdims=True)
            dlnb_ref[...] += jnp.sum(dvn, axis=0, keepdims=True)
            dxh = dvn * lng
            dv = rstd * (dxh - jnp.mean(dxh, axis=-1, keepdims=True)
                         - xhat * jnp.mean(dxh * xhat, axis=-1, keepdims=True))
            pend_ref[:, OFF_V:OFF_V + D_GMLP] = dv.astype(BF16)

        @pl.when(n == nb)
        def _():
            flush(jnp.zeros((CHUNK, 2 * D_KV), F32))
            t = lax.broadcasted_iota(jnp.int32, (CHUNK, CHUNK), 0)
            sidx = lax.broadcasted_iota(jnp.int32, (CHUNK, CHUNK), 1)
            lane = _lane_iota((CHUNK, LANES))
            dbt = jnp.zeros((CHUNK, LANES), F32)
            for g in range(GROUPS):
                dw_ref[g] = jnp.where(t >= sidx, dw_ref[g], 0.0)
                dbt = jnp.where(lane == g, jnp.sum(dbacc_ref[g], axis=-1, keepdims=True), dbt)
            db_ref[...] = jnp.transpose(dbt)[:GROUPS, :]

        _after_step(phases, n, nb + 1)

    cur = lambda n: (jnp.minimum(n, nb - 1), 0)
    prev = lambda n: (jnp.clip(n - 1, 0, nb - 1), 0)
    kv_block = OFF_K // (2 * D_KV)
    const2 = lambda n: (0, 0)
    const3 = lambda n: (0, 0, 0)
    return pl.pallas_call(
        body, name="mixer_bwd", grid=(nb + 1,),
        in_specs=[pl.BlockSpec((CHUNK, D_IN), cur),
                  pl.BlockSpec((CHUNK, 2 * D_KV), lambda n: (jnp.clip(n - 1, 0, nb - 1), kv_block)),
                  pl.BlockSpec((CHUNK, D_GMLP + D_ATTN), cur),
                  pl.BlockSpec((1, N_PAIRS, 2 * CHUNK, 2 * CHUNK), lambda n: (jnp.minimum(n, nb - 1), 0, 0, 0)),
                  pl.BlockSpec((CHUNK, 2 * LANES), cur),
                  pl.BlockSpec((CHUNK, 2 * LANES), prev),
                  pl.BlockSpec((1, D_GMLP), const2),
                  pl.BlockSpec((1, D_GMLP), const2),
                  pl.BlockSpec((GROUPS, CHUNK, CHUNK), const3),
                  pl.BlockSpec((GROUPS, CHUNK, CHUNK), const3)] + hosted.in_specs,
        out_specs=[pl.BlockSpec((CHUNK, D_IN), lambda n: (jnp.maximum(n - 1, 0), 0)),
                   pl.BlockSpec((GROUPS, CHUNK, CHUNK), const3),
                   pl.BlockSpec((GROUPS, CHUNK), const2),
                   pl.BlockSpec((1, D_GMLP), const2),
                   pl.BlockSpec((1, D_GMLP), const2),
                   pl.BlockSpec((1, LANES), const2)] + hosted.out_specs,
        scratch_shapes=[pltpu.VMEM((CHUNK, D_IN), BF16), pltpu.VMEM((CHUNK, 2 * D_KV), F32),
                        pltpu.VMEM((GROUPS, CHUNK, CHUNK), F32)] + hosted.scratch,
        out_shape=[jax.ShapeDtypeStruct((s, D_IN), BF16),
                   jax.ShapeDtypeStruct((GROUPS, CHUNK, CHUNK), F32),
                   jax.ShapeDtypeStruct((GROUPS, CHUNK), F32),
                   jax.ShapeDtypeStruct((1, D_GMLP), F32),
                   jax.ShapeDtypeStruct((1, D_GMLP), F32),
                   jax.ShapeDtypeStruct((1, LANES), F32)] + hosted.out_shape,
        compiler_params=_cparams(("arbitrary",)),
    )(proj, proj, dcat, probs, rope, rope, ln_g, ln_b, w_sp, b_sp_rows, *hosted.inputs)


def _adamw_math(w, g, m, v):
    m = ADAM_B1 * m + (1.0 - ADAM_B1) * g
    v = ADAM_B2 * v + (1.0 - ADAM_B2) * (g * g)
    m_hat = m / (1.0 - ADAM_B1 ** ADAM_STEP)
    v_hat = v / (1.0 - ADAM_B2 ** ADAM_STEP)
    delta = -ADAM_LR * (m_hat / (jnp.sqrt(v_hat) + ADAM_EPS) + ADAM_WD * w)
    return delta, m, v


def adamw_shard(terms, w, m, v, name, after=None):
    r, c = w.shape
    tr = _tile(r, (224, 256, 128, 8))
    n_terms = len(terms)
    order_only = [] if after is None else [after]

    def body(*refs):
        w_ref, m_ref, v_ref, g_ref, d_ref, nm_ref, nv_ref = refs[n_terms:n_terms + 3] + refs[-4:]
        g = None
        for ref, (_, slots) in zip(refs[:n_terms], terms):
            for k in range(slots):
                part = ref[k].astype(F32)
                g = part if g is None else g + part
        g_ref[...] = g
        d_ref[...], nm_ref[...], nv_ref[...] = _adamw_math(w_ref[...], g, m_ref[...], v_ref[...])

    spec = pl.BlockSpec((tr, c), lambda i: (i, 0))
    return pl.pallas_call(
        body, name=name, grid=(r // tr,),
        in_specs=[pl.BlockSpec((slots, tr, c), lambda i: (0, i, 0)) for _, slots in terms] + [spec] * 3
        + [ANY] * len(order_only),
        out_specs=[spec] * 4, out_shape=[jax.ShapeDtypeStruct((r, c), F32)] * 4,
        compiler_params=_cparams(("arbitrary",)),
    )(*[a for a, _ in terms], w, m, v, *order_only)


def adamw_small(gathered, lane_windows, params):
    n_par = len(params)

    def body(*refs):
        g_refs = refs[:n_par + 1]
        wmv_refs = refs[n_par + 1:4 * n_par + 1]
        out_refs = refs[4 * n_par + 1:]

        def total(ref):
            acc = ref[0]
            for dev in range(1, N_DEV):
                acc = acc + ref[dev]
            return acc

        for i in range(n_par):
            w_ref, m_ref, v_ref = wmv_refs[3 * i:3 * i + 3]
            g = total(g_refs[i])
            if lane_windows[i] is not None:
                start, size = lane_windows[i]
                g = g[..., start:start + size]
            delta, new_m, new_v = _adamw_math(w_ref[...], g, m_ref[...], v_ref[...])
            for ref, val in zip(out_refs[4 * i:4 * i + 4], (g, delta, new_m, new_v)):
                ref[...] = val
        out_refs[4 * n_par][...] = total(g_refs[n_par])

    flat = [a for wmv in params for a in wmv]
    out_shape = [jax.ShapeDtypeStruct(w.shape, F32) for (w, _, _) in params for _ in range(4)]
    out_shape.append(jax.ShapeDtypeStruct(gathered[-1].shape[1:], F32))
    outs = pl.pallas_call(body, name="adamw_small", out_shape=out_shape, compiler_params=_cparams())(*gathered, *flat)
    return [tuple(outs[4 * i:4 * i + 4]) for i in range(n_par)], outs[-1]


def kernel(x, positions, g_pre, w_in, b_qkv, ln_v_g, ln_v_b, w_spatial, b_spatial, attn_sinks, w_out, g_post, loss_target, m_g_pre, m_w_in, m_b_qkv, m_ln_v_g, m_ln_v_b, m_w_spatial, m_b_spatial, m_attn_sinks, m_w_out, m_g_post, v_g_pre, v_w_in, v_b_qkv, v_ln_v_g, v_ln_v_b, v_w_spatial, v_b_spatial, v_attn_sinks, v_w_out, v_g_post):
    x2, target = x[0], loss_target[0]
    seq = x2.shape[0]

    wt_shard = w_in[0].T.astype(BF16)
    wo_shard = w_out[0].astype(BF16)
    pos_col = positions.reshape(seq, 1)
    half = HEAD_DIM // 2
    inv_freq = ROPE_THETA ** (-jnp.arange(half, dtype=F32) * (2.0 / HEAD_DIM))
    freq = jnp.tile(inv_freq, LANES // half).reshape(1, LANES)
    sign = jnp.tile(jnp.concatenate([-jnp.ones((half,), F32), jnp.ones((half,), F32)]), LANES // HEAD_DIM)
    sign = sign.reshape(1, LANES)
    bias = jnp.concatenate([jnp.zeros((1, OFF_Q), F32), b_qkv, jnp.zeros((1, D_ATTN), F32)], axis=1)
    proj, h, rope, wt = in_proj_gather(x2, pos_col, freq, sign, g_pre, wt_shard, bias)

    b_rows = jnp.broadcast_to(b_spatial[0][:, :, None], (GROUPS, CHUNK, CHUNK))
    sinks = attn_sinks[0]
    cat, probs, wo = mixer_fwd(proj, rope, ln_v_g, ln_v_b, w_spatial[0], b_rows, sinks,
                               comms=[gather_comm([wo_shard])])
    dy, dout, d_g_post, loss_part = out_proj_loss(cat, wo, x2, target, g_post)

    dcat = matmul_nt(dy, wo, "out_proj_bwd")
    d_wo, _ = matmul_tn(cat, dy, 512, "w_out_grad")
    dproj, d_w_sp, d_b_sp, d_ln_g, d_ln_b, d_sinks, parts_wo = mixer_bwd(
        proj, dcat, probs, rope, ln_v_g, ln_v_b, w_spatial[0], b_rows, comms=[scatter_comm([d_wo])])
    small_parts = [d_ln_g, d_ln_b, d_w_sp, d_b_sp, d_sinks, d_g_post, loss_part]
    d_wt, colsum, *landed = matmul_tn(dproj, h, 768, "w_in_grad", comms=[gather_comm(small_parts, stack=True)])

    sum_wt = pair_exchange_sum(d_wt, "grad_pair_sum_w_in")
    started = chips_exchange_start(sum_wt)
    token = started[-1]
    grad_x, d_g_pre = in_proj_bwd(dproj, wt, x2, g_pre + token[:1, :1], dout)
    sum_wt, far_wt = chips_exchange_wait(started, d_g_pre)
    late = jnp.concatenate([d_g_pre, colsum], axis=1)
    late_started, late_copy = gather_start(late)

    wo_out = adamw_shard([(parts_wo, N_DEV)], w_out[0], m_w_out[0], v_w_out[0], "adamw_w_out", after=late_started[-1])
    wt_out = adamw_shard([(sum_wt, 1), (far_wt, 3)], w_in[0].T, m_w_in[0].T, v_w_in[0].T, "adamw_w_in",
                         after=wo_out[0])
    landed_late = gather_wait(late_started, late_copy, wt_out[0])
    my_index = _index(*_my_place())
    all_late = lax.dynamic_update_slice(landed_late, late[None], (my_index, 0, 0))
    gathered = [all_late, all_late] + landed
    windows = [(0, D_MODEL), (D_MODEL + OFF_Q, D_QKV), None, None, None, None, (0, N_Q_HEADS), None]
    small = [(g_pre, m_g_pre, v_g_pre), (b_qkv, m_b_qkv, v_b_qkv), (ln_v_g, m_ln_v_g, v_ln_v_g),
             (ln_v_b, m_ln_v_b, v_ln_v_b), (w_spatial[0], m_w_spatial[0], v_w_spatial[0]),
             (b_spatial[0], m_b_spatial[0], v_b_spatial[0]), (attn_sinks, m_attn_sinks, v_attn_sinks),
             (g_post, m_g_post, v_g_post)]
    small_out, loss_row = adamw_small(gathered, windows, small)
    lead = [False, False, False, False, True, True, False, False]
    small_out = [tuple(a[None] if ld else a for a in leaf) for leaf, ld in zip(small_out, lead)]

    def leaves(k):
        gp, bq, lg, lb, ws, bs, sk, gpo = (leaf[k] for leaf in small_out)
        return [gp, wt_out[k].T[None], bq, lg, lb, ws, bs, sk, wo_out[k][None], gpo]

    return (loss_row[0, 0], grad_x[None], *leaves(0), *leaves(1), *leaves(2), *leaves(3))
```

```python
import functools

import jax
import jax.numpy as jnp
from jax import lax
from jax.experimental import pallas as pl
from jax.experimental.pallas import tpu as pltpu

F32 = jnp.float32
BF16 = jnp.bfloat16

D_MODEL = 2048
D_GMLP = 1024
D_ATTN = 1024
CHUNK = 128
GROUPS = 8
HEAD_DIM = 64
N_Q_HEADS = 16
N_KV_HEADS = 2
D_KV = N_KV_HEADS * HEAD_DIM
D_IN = 3 * D_GMLP + D_ATTN + 2 * D_KV + D_ATTN
OFF_U, OFF_V, OFF_ZA = 0, D_GMLP, 2 * D_GMLP
OFF_Q = 3 * D_GMLP
OFF_K = OFF_Q + D_ATTN
OFF_VA = OFF_K + D_KV
OFF_ZB = OFF_VA + D_KV
D_QKV = D_ATTN + 2 * D_KV
ROPE_THETA = 10000.0
EPS = 1e-6
SCALE = HEAD_DIM ** -0.5
NEG = -1e30
N_PAIRS = N_Q_HEADS // 2
PAIRS_PER_KV = N_PAIRS // N_KV_HEADS

ADAM_LR = 0.001
ADAM_B1 = 0.9
ADAM_B2 = 0.999
ADAM_EPS = 1e-08
ADAM_WD = 0.01
ADAM_STEP = 10

N_DEV = 8
LANES = 128
VMEM_LIMIT = 56 * 1024 * 1024
IN_PROJ_VMEM_LIMIT = 61 * 1024 * 1024

MESH = pl.DeviceIdType.MESH
ANY = pl.BlockSpec(memory_space=pl.ANY)


def _cparams(sem=None):
    return pltpu.CompilerParams(dimension_semantics=sem, vmem_limit_bytes=VMEM_LIMIT)


def _tile(n, prefs):
    for t in prefs:
        if n % t == 0:
            return t
    return n


def _sigmoid(z):
    return 1.0 / (1.0 + jnp.exp(-z))


def _dot(a, b, ca, cb):
    return lax.dot_general(a, b, (((ca,), (cb,)), ((), ())), preferred_element_type=F32)


def _my_place():
    return lax.axis_index("x"), lax.axis_index("y"), lax.axis_index("c")


def _chip_of(x, y, r):
    return (x ^ (r & 1), y ^ (r >> 1))


def _peer(x, y, c, k):
    return (x ^ (k >> 2), y ^ ((k >> 1) & 1), c ^ (k & 1))


def _index(px, py, pc):
    return 4 * px + 2 * py + pc


class _Comm:
    def __init__(self, inputs, out_shape, scratch, bind):
        self.inputs, self.out_shape, self.scratch, self.bind = list(inputs), list(out_shape), list(scratch), bind


def gather_comm(shards, stack=False):
    n_arr = len(shards)

    def bind(ins, outs, sems):
        send_sems, recv_sems, local_sems = sems
        x, y, c = _my_place()
        me, sibling = (x, y, c), (x, y, 1 - c)
        chips = [_chip_of(x, y, r) for r in (1, 2, 3)]

        def rows(a, px, py, pc):
            d = _index(px, py, pc)
            if stack:
                return outs[a].at[d]
            m = shards[a].shape[0]
            return outs[a].at[pl.ds(pl.multiple_of(d * m, 8), m), :]

        def copy(a, k, block, to, src=None):
            return pltpu.make_async_remote_copy(
                src_ref=rows(a, *block) if src is None else src, dst_ref=rows(a, *block),
                send_sem=send_sems.at[a * 7 + k], recv_sem=recv_sems.at[a * 7 + k],
                device_id=to, device_id_type=MESH)

        def mine(a):
            return pltpu.make_async_copy(ins[a], rows(a, *me), local_sems.at[a])

        def own_sends(a):
            return ([copy(a, 0, me, sibling, src=ins[a])]
                    + [copy(a, 1 + j, me, (*chip, c), src=ins[a]) for j, chip in enumerate(chips)])

        def start():
            for a in range(n_arr):
                mine(a).start()
                for cp in own_sends(a):
                    cp.start()

        def relay():
            for j, chip in enumerate(chips):
                for a in range(n_arr):
                    copy(a, 1 + j, (*chip, c), me).wait_recv()
                    copy(a, 4 + j, (*chip, c), sibling).start()

        def finish():
            for a in range(n_arr):
                copy(a, 0, sibling, me).wait_recv()
                for j, chip in enumerate(chips):
                    copy(a, 4 + j, (*chip, 1 - c), me).wait_recv()
                    copy(a, 4 + j, (*chip, c), sibling).wait_send()
                for cp in own_sends(a):
                    cp.wait_send()
                mine(a).wait()

        return start, relay, finish

    def gathered(s):
        return (N_DEV, *s.shape) if stack else (N_DEV * s.shape[0], s.shape[1])

    return _Comm(shards, [jax.ShapeDtypeStruct(gathered(s), s.dtype) for s in shards],
                 [pltpu.SemaphoreType.DMA((7 * n_arr,)), pltpu.SemaphoreType.DMA((7 * n_arr,)),
                  pltpu.SemaphoreType.DMA((n_arr,))], bind)


def scatter_comm(parts):
    n_arr = len(parts)

    def bind(ins, outs, sems):
        send_sems, recv_sems, local_sems = sems
        x, y, c = _my_place()
        my_index = _index(x, y, c)

        def block(a, d):
            m = parts[a].shape[0] // N_DEV
            return ins[a].at[pl.ds(pl.multiple_of(d * m, 16), m), :]

        def copy(a, k, slot):
            peer = _peer(x, y, c, k)
            return pltpu.make_async_remote_copy(
                src_ref=block(a, _index(*peer)), dst_ref=outs[a].at[slot],
                send_sem=send_sems.at[a * 7 + k - 1], recv_sem=recv_sems.at[a * 7 + k - 1],
                device_id=peer, device_id_type=MESH)

        def mine(a):
            return pltpu.make_async_copy(block(a, my_index), outs[a].at[my_index], local_sems.at[a])

        def start():
            for a in range(n_arr):
                mine(a).start()
                for k in range(1, 8):
                    copy(a, k, my_index).start()

        def finish():
            for a in range(n_arr):
                for k in range(1, 8):
                    copy(a, k, _index(*_peer(x, y, c, k))).wait_recv()
                    copy(a, k, my_index).wait_send()
                mine(a).wait()

        return start, (lambda: None), finish

    return _Comm(parts, [jax.ShapeDtypeStruct((N_DEV, p.shape[0] // N_DEV, p.shape[1]), p.dtype) for p in parts],
                 [pltpu.SemaphoreType.DMA((7 * n_arr,)), pltpu.SemaphoreType.DMA((7 * n_arr,)),
                  pltpu.SemaphoreType.DMA((n_arr,))], bind)


def run_comm(comm, name):
    n_in, n_out = len(comm.inputs), len(comm.out_shape)

    def body(*refs):
        start, relay, finish = comm.bind(refs[:n_in], refs[n_in:n_in + n_out], refs[n_in + n_out:])
        start()
        relay()
        finish()

    outs = pl.pallas_call(body, name=name, out_shape=comm.out_shape, in_specs=[ANY] * n_in,
                          out_specs=[ANY] * n_out, scratch_shapes=comm.scratch)(*comm.inputs)
    return list(outs)


def _chip_copy(r, src_ref, land_ref, send_sem, recv_sem):
    x, y, c = _my_place()
    return pltpu.make_async_remote_copy(src_ref=src_ref.at[r], dst_ref=land_ref.at[r - 1], send_sem=send_sem,
                                        recv_sem=recv_sem, device_id=(*_chip_of(x, y, r), c), device_id_type=MESH)


def chips_exchange_start(sums):
    def body(src_ref, land_ref, s1, s2, s3, r1, r2, r3, src_thru, land_thru, token):
        del src_thru, land_thru
        for r, send_sem, recv_sem in ((1, s1, r1), (2, s2, r2), (3, s3, r3)):
            _chip_copy(r, src_ref, land_ref, send_sem, recv_sem).start()
        token[...] = jnp.zeros_like(token)

    land = lax.empty((3,) + sums.shape[1:], sums.dtype)
    sem = pltpu.SemaphoreType.DMA(())
    hbm = pl.BlockSpec(memory_space=pltpu.HBM)
    sem_spec = pl.BlockSpec(memory_space=pltpu.SEMAPHORE)
    return pl.pallas_call(
        body, name="grad_exchange_chips_start",
        out_shape=(sem,) * 6 + (pltpu.HBM(sums.shape, sums.dtype), pltpu.HBM(land.shape, land.dtype),
                                jax.ShapeDtypeStruct((8, LANES), F32)),
        in_specs=(hbm, hbm), out_specs=(sem_spec,) * 6 + (hbm, hbm, pl.BlockSpec(memory_space=pltpu.VMEM)),
        input_output_aliases={0: 6, 1: 7},
        compiler_params=pltpu.CompilerParams(has_side_effects=pltpu.SideEffectType.DATAFLOW_SIDE_EFFECTING),
    )(pltpu.with_memory_space_constraint(sums, pltpu.HBM), pltpu.with_memory_space_constraint(land, pltpu.HBM))


def chips_exchange_wait(started, after):
    s1, s2, s3, r1, r2, r3, src_thru, land_thru, _ = started

    def body(src_ref, land_ref, s1, s2, s3, r1, r2, r3, after_ref, src_out, land_out):
        del after_ref, src_out, land_out
        for r, send_sem, recv_sem in ((1, s1, r1), (2, s2, r2), (3, s3, r3)):
            copy = _chip_copy(r, src_ref, land_ref, send_sem, recv_sem)
            copy.wait_send()
            copy.wait_recv()

    hbm = pl.BlockSpec(memory_space=pltpu.HBM)
    sem_spec = pl.BlockSpec(memory_space=pltpu.SEMAPHORE)
    return pl.pallas_call(
        body, name="grad_exchange_chips_wait",
        out_shape=(pltpu.HBM(src_thru.shape, src_thru.dtype), pltpu.HBM(land_thru.shape, land_thru.dtype)),
        in_specs=(hbm, hbm) + (sem_spec,) * 6 + (pl.BlockSpec(memory_space=pl.ANY),), out_specs=(hbm, hbm),
        input_output_aliases={0: 0, 1: 1},
        compiler_params=pltpu.CompilerParams(has_side_effects=pltpu.SideEffectType.DATAFLOW_SIDE_EFFECTING),
    )(src_thru, land_thru, s1, s2, s3, r1, r2, r3, after)


def gather_start(shard):
    def copy(k, src_ref, land_ref, send_sem, recv_sem):
        x, y, c = _my_place()
        return pltpu.make_async_remote_copy(src_ref=src_ref, dst_ref=land_ref.at[_index(x, y, c)], send_sem=send_sem,
                                            recv_sem=recv_sem, device_id=_peer(x, y, c, k), device_id_type=MESH)

    def start_body(src_ref, land_ref, *rest):
        sems, token = rest[:14], rest[16]
        for k in range(1, 8):
            copy(k, src_ref, land_ref, sems[k - 1], sems[7 + k - 1]).start()
        token[...] = jnp.zeros_like(token)

    land = lax.empty((N_DEV,) + shard.shape, shard.dtype)
    sem = pltpu.SemaphoreType.DMA(())
    hbm = pl.BlockSpec(memory_space=pltpu.HBM)
    sem_spec = pl.BlockSpec(memory_space=pltpu.SEMAPHORE)
    started = pl.pallas_call(
        start_body, name="allgather_late_grads_start",
        out_shape=(sem,) * 14 + (pltpu.HBM(shard.shape, shard.dtype), pltpu.HBM(land.shape, land.dtype),
                                 jax.ShapeDtypeStruct((8, LANES), F32)),
        in_specs=(hbm, hbm), out_specs=(sem_spec,) * 14 + (hbm, hbm, pl.BlockSpec(memory_space=pltpu.VMEM)),
        input_output_aliases={0: 14, 1: 15},
        compiler_params=pltpu.CompilerParams(has_side_effects=pltpu.SideEffectType.DATAFLOW_SIDE_EFFECTING),
    )(pltpu.with_memory_space_constraint(shard, pltpu.HBM), pltpu.with_memory_space_constraint(land, pltpu.HBM))
    return started, copy


def gather_wait(started, copy, after):
    sems, src_thru, land_thru = started[:14], started[14], started[15]

    def wait_body(src_ref, land_ref, *rest):
        for k in range(1, 8):
            cp = copy(k, src_ref, land_ref, rest[k - 1], rest[7 + k - 1])
            cp.wait_send()
            cp.wait_recv()

    hbm = pl.BlockSpec(memory_space=pltpu.HBM)
    sem_spec = pl.BlockSpec(memory_space=pltpu.SEMAPHORE)
    return pl.pallas_call(
        wait_body, name="allgather_late_grads_wait",
        out_shape=(pltpu.HBM(src_thru.shape, src_thru.dtype), pltpu.HBM(land_thru.shape, land_thru.dtype)),
        in_specs=(hbm, hbm) + (sem_spec,) * 14 + (pl.BlockSpec(memory_space=pl.ANY),), out_specs=(hbm, hbm),
        input_output_aliases={0: 0, 1: 1},
        compiler_params=pltpu.CompilerParams(has_side_effects=pltpu.SideEffectType.DATAFLOW_SIDE_EFFECTING),
    )(src_thru, land_thru, *sems, after)[1]


class _Hosted:
    def __init__(self, comms):
        self.comms = list(comms)
        self.inputs = [a for cm in self.comms for a in cm.inputs]
        self.out_shape = [s for cm in self.comms for s in cm.out_shape]
        self.scratch = [s for cm in self.comms for s in cm.scratch]
        self.in_specs = [ANY] * len(self.inputs)
        self.out_specs = [ANY] * len(self.out_shape)

    def split(self, refs, n_in, n_out, n_scratch):
        ni, no = len(self.inputs), len(self.out_shape)
        ins, rest = refs[:n_in], refs[n_in:]
        c_ins, rest = rest[:ni], rest[ni:]
        outs, rest = rest[:n_out], rest[n_out:]
        c_outs, rest = rest[:no], rest[no:]
        scratch, c_sems = rest[:n_scratch], rest[n_scratch:]
        phases = []
        for cm in self.comms:
            a, b, s = len(cm.inputs), len(cm.out_shape), len(cm.scratch)
            phases.append(cm.bind(c_ins[:a], c_outs[:b], c_sems[:s]))
            c_ins, c_outs, c_sems = c_ins[a:], c_outs[b:], c_sems[s:]
        return ins, outs, scratch, phases


def _before_step(phases, step, n_steps):
    if not phases:
        return

    @pl.when(step == 0)
    def _():
        for start, _, _ in phases:
            start()

    @pl.when(step == n_steps // 2)
    def _():
        for _, relay, _ in phases:
            relay()


def _after_step(phases, step, n_steps):
    if not phases:
        return

    @pl.when(step == n_steps - 1)
    def _():
        for _, _, finish in phases:
            finish()


def pair_exchange_sum(part, name):
    m, n = part.shape[0] // N_DEV, part.shape[1]

    def body(part_ref, out_ref, got, mine, summed, send_sems, recv_sems, in_sems, out_sems):
        x, y, c = _my_place()

        def rows(r, core):
            owner = _index(*_chip_of(x, y, r), core)
            return part_ref.at[pl.ds(pl.multiple_of(owner * m, 16), m), :]

        def to_sibling(r):
            return pltpu.make_async_remote_copy(src_ref=rows(r, 1 - c), dst_ref=got.at[r], send_sem=send_sems.at[r],
                                                recv_sem=recv_sems.at[r], device_id=(x, y, 1 - c), device_id_type=MESH)

        def fetch(r):
            return pltpu.make_async_copy(rows(r, c), mine.at[r % 2], in_sems.at[r % 2])

        def put(r):
            return pltpu.make_async_copy(summed.at[r % 2], out_ref.at[r], out_sems.at[r % 2])

        for r in range(4):
            to_sibling(r).start()
        fetch(0).start()
        for r in range(4):
            if r + 1 < 4:
                fetch(r + 1).start()
            fetch(r).wait()
            to_sibling(r).wait_recv()
            if r >= 2:
                put(r - 2).wait()
            summed[r % 2] = (mine[r % 2].astype(F32) + got[r].astype(F32)).astype(summed.dtype)
            put(r).start()
        for r in (2, 3):
            put(r).wait()
        for r in range(4):
            to_sibling(r).wait_send()

    return pl.pallas_call(
        body, name=name, out_shape=jax.ShapeDtypeStruct((4, m, n), part.dtype),
        in_specs=[ANY], out_specs=ANY,
        scratch_shapes=[pltpu.VMEM((4, m, n), part.dtype), pltpu.VMEM((2, m, n), part.dtype),
                        pltpu.VMEM((2, m, n), part.dtype), pltpu.SemaphoreType.DMA((4,)),
                        pltpu.SemaphoreType.DMA((4,)), pltpu.SemaphoreType.DMA((2,)), pltpu.SemaphoreType.DMA((2,))],
        compiler_params=_cparams(),
    )(part)


def in_proj_gather(x, pos_col, freq, sign, g_pre, wt_shard, bias):
    s, d = x.shape
    tm = _tile(s, (512, 256, 128))
    nt = s // tm
    tc = _tile(s, (256, 128))
    nc = s // tc
    m = wt_shard.shape[0]
    half = D_IN // 2
    xi = lax.axis_index("x")
    order = jnp.stack([xi, 1 - xi]).astype(jnp.int32)

    def body(order_ref, x_hbm, pos_hbm, freq_ref, sign_ref, g_ref, b_ref, shard_ref,
             proj_ref, h_hbm, rope_hbm, wt_ref,
             w_vmem, h_vmem, xbuf, posbuf, ropebuf, send_sems, recv_sems, local_sems, in_sems, out_sems):
        del order_ref
        p, i = pl.program_id(0), pl.program_id(1)
        xx, yy, cc = _my_place()
        me, sibling = (xx, yy, cc), (xx, yy, 1 - cc)
        chips = [_chip_of(xx, yy, r) for r in (1, 2, 3)]

        def rows(px, py, pc):
            return wt_ref.at[pl.ds(pl.multiple_of(_index(px, py, pc) * m, 16), m), :]

        def copy(k, block, to, src=None):
            return pltpu.make_async_remote_copy(
                src_ref=rows(*block) if src is None else src, dst_ref=rows(*block),
                send_sem=send_sems.at[k], recv_sem=recv_sems.at[k], device_id=to, device_id_type=MESH)

        def mine():
            return pltpu.make_async_copy(shard_ref, rows(*me), local_sems.at[0])

        def to_sibling():
            return copy(0, me, sibling, src=shard_ref)

        def to_chip(j):
            return copy(1 + j, me, (*chips[j], cc), src=shard_ref)

        def relay(j):
            copy(1 + j, (*chips[j], cc), me).wait_recv()
            copy(4 + j, (*chips[j], cc), sibling).start()

        def relayed(j):
            copy(4 + j, (*chips[j], 1 - cc), me).wait_recv()

        def load_half(which, slot):
            rows_of_half = wt_ref.at[pl.ds(pl.multiple_of(which * half, 16), half), :]
            load = pltpu.make_async_copy(rows_of_half, w_vmem.at[slot], local_sems.at[1 + slot])
            load.start()
            load.wait()

        def piece(ref, c):
            return ref.at[pl.ds(c * tc, tc), :]

        def fetch(c):
            return (pltpu.make_async_copy(piece(x_hbm, c), xbuf.at[c % 2], in_sems.at[c % 2]),
                    pltpu.make_async_copy(piece(pos_hbm, c), posbuf.at[c % 2], in_sems.at[2 + c % 2]))

        def put(c):
            return (pltpu.make_async_copy(piece(h_vmem, c), piece(h_hbm, c), out_sems.at[c % 2]),
                    pltpu.make_async_copy(ropebuf.at[c % 2], piece(rope_hbm, c), out_sems.at[2 + c % 2]))

        def prologue():
            for cp in fetch(0):
                cp.start()
            for c in range(nc):
                if c + 1 < nc:
                    for cp in fetch(c + 1):
                        cp.start()
                for cp in fetch(c):
                    cp.wait()
                if c >= 2:
                    for cp in put(c - 2):
                        cp.wait()
                xv = xbuf[c % 2]
                r = lax.rsqrt(jnp.mean(xv * xv, axis=-1, keepdims=True) + EPS)
                h_vmem[c * tc:(c + 1) * tc, :] = (xv * r * g_ref[...]).astype(BF16)
                ang = posbuf[c % 2].astype(F32) * freq_ref[...]
                ropebuf[c % 2, :, :LANES] = jnp.cos(ang)
                ropebuf[c % 2, :, LANES:] = jnp.sin(ang) * sign_ref[...]
                for cp in put(c):
                    cp.start()
            for c in range(max(nc - 2, 0), nc):
                for cp in put(c):
                    cp.wait()

        @pl.when(jnp.logical_and(p == 0, i == 0))
        def _():
            mine().start()
            to_sibling().start()
            to_chip(1).start()
            to_chip(0).start()
            prologue()
            copy(0, sibling, me).wait_recv()
            relay(1)
            relayed(1)
            mine().wait()
            to_chip(1).wait_send()
            to_chip(0).wait_send()
            to_chip(2).start()
            load_half(xx, 0)

        @pl.when(jnp.logical_and(p == 1, i == 0))
        def _():
            relayed(0)
            relayed(2)
            load_half(1 - xx, 1)

        def project(slot):
            hb = h_vmem[pl.ds(pl.multiple_of(i * tm, tm), tm), :]
            proj_ref[...] = (_dot(hb, w_vmem[slot], 1, 1) + b_ref[...]).astype(BF16)

        @pl.when(p == 0)
        def _():
            project(0)

        @pl.when(p == 1)
        def _():
            project(1)

        @pl.when(jnp.logical_and(p == 0, i == 1))
        def _():
            relay(0)

        @pl.when(jnp.logical_and(p == 0, i == nt - 1))
        def _():
            relay(2)

        @pl.when(jnp.logical_and(p == 1, i == nt - 1))
        def _():
            to_sibling().wait_send()
            to_chip(2).wait_send()
            for j in range(3):
                copy(4 + j, (*chips[j], cc), sibling).wait_send()

    const = lambda p, i, o: (0, 0)
    return pl.pallas_call(
        body, name="in_proj_gather",
        grid_spec=pltpu.PrefetchScalarGridSpec(
            num_scalar_prefetch=1, grid=(2, nt),
            in_specs=[ANY, ANY,
                      pl.BlockSpec((1, LANES), const),
                      pl.BlockSpec((1, LANES), const),
                      pl.BlockSpec((1, d), const),
                      pl.BlockSpec((1, half), lambda p, i, o: (0, o[p])),
                      ANY],
            out_specs=[pl.BlockSpec((tm, half), lambda p, i, o: (i, o[p])), ANY, ANY, ANY],
            scratch_shapes=[pltpu.VMEM((2, half, d), BF16), pltpu.VMEM((s, d), BF16),
                            pltpu.VMEM((2, tc, d), F32), pltpu.VMEM((2, tc, 1), jnp.int32),
                            pltpu.VMEM((2, tc, 2 * LANES), F32),
                            pltpu.SemaphoreType.DMA((7,)), pltpu.SemaphoreType.DMA((7,)),
                            pltpu.SemaphoreType.DMA((3,)), pltpu.SemaphoreType.DMA((4,)),
                            pltpu.SemaphoreType.DMA((4,))]),
        out_shape=[jax.ShapeDtypeStruct((s, D_IN), BF16), jax.ShapeDtypeStruct((s, d), BF16),
                   jax.ShapeDtypeStruct((s, 2 * LANES), F32), jax.ShapeDtypeStruct((D_IN, d), BF16)],
        compiler_params=pltpu.CompilerParams(dimension_semantics=("arbitrary", "arbitrary"),
                                             vmem_limit_bytes=IN_PROJ_VMEM_LIMIT),
    )(order, x, pos_col, freq, sign, g_pre, bias, wt_shard)


def out_proj_loss(cat, w_out, x, target, g_post):
    s, d = x.shape
    tm = _tile(s, (256, 128))
    kc = _tile(d, (512, 128))
    pieces = w_out.shape[0] // kc

    def body(cat_ref, w_hbm, x_ref, t_ref, g_ref, dy_ref, dout_ref, dg_ref, loss_ref, w_ref, w_sems):
        step = pl.program_id(0)

        def w_load(j):
            return pltpu.make_async_copy(w_hbm.at[j * kc:(j + 1) * kc, :], w_ref.at[j * kc:(j + 1) * kc, :], w_sems.at[j])

        @pl.when(step == 0)
        def _():
            for j in range(pieces):
                w_load(j).start()
            dg_ref[...] = jnp.zeros_like(dg_ref)
            loss_ref[...] = jnp.zeros_like(loss_ref)
            ys = [None] * (tm // CHUNK)
            for j in range(pieces):
                w_load(j).wait()
                for c in range(tm // CHUNK):
                    part = _dot(cat_ref[c * CHUNK:(c + 1) * CHUNK, j * kc:(j + 1) * kc], w_ref[j * kc:(j + 1) * kc, :], 1, 0)
                    ys[c] = part if ys[c] is None else ys[c] + part
            loss_and_back(ys, x_ref, t_ref, g_ref, dy_ref, dout_ref, dg_ref, loss_ref)

        @pl.when(step > 0)
        def _():
            ys = [_dot(cat_ref[c0:c0 + CHUNK, :], w_ref[...], 1, 0) for c0 in range(0, tm, CHUNK)]
            loss_and_back(ys, x_ref, t_ref, g_ref, dy_ref, dout_ref, dg_ref, loss_ref)

    def loss_and_back(ys, x_ref, t_ref, g_ref, dy_ref, dout_ref, dg_ref, loss_ref):
        g = g_ref[...]
        for c0 in range(0, tm, CHUNK):
            rows = slice(c0, c0 + CHUNK)
            yv = ys[c0 // CHUNK]
            r = lax.rsqrt(jnp.mean(yv * yv, axis=-1, keepdims=True) + EPS)
            nrm = yv * r
            err = x_ref[rows, :] + nrm * g - t_ref[rows, :]
            loss_ref[...] += 0.5 * jnp.sum(jnp.sum(err * err, axis=-1, keepdims=True), axis=0, keepdims=True) / d
            dout = err * (1.0 / d)
            dout_ref[rows, :] = dout
            dg_ref[...] += jnp.sum(dout * nrm, axis=0, keepdims=True)
            dn = dout * g
            dy = r * (dn - nrm * jnp.mean(dn * nrm, axis=-1, keepdims=True))
            dy_ref[rows, :] = dy.astype(BF16)

    return pl.pallas_call(
        body, name="out_proj_loss", grid=(s // tm,),
        in_specs=[pl.BlockSpec((tm, d), lambda i: (i, 0)),
                  ANY,
                  pl.BlockSpec((tm, d), lambda i: (i, 0)),
                  pl.BlockSpec((tm, d), lambda i: (i, 0)),
                  pl.BlockSpec((1, d), lambda i: (0, 0))],
        out_specs=[pl.BlockSpec((tm, d), lambda i: (i, 0)),
                   pl.BlockSpec((tm, d), lambda i: (i, 0)),
                   pl.BlockSpec((1, d), lambda i: (0, 0)),
                   pl.BlockSpec((1, LANES), lambda i: (0, 0))],
        out_shape=[jax.ShapeDtypeStruct((s, d), BF16), jax.ShapeDtypeStruct((s, d), F32),
                   jax.ShapeDtypeStruct((1, d), F32), jax.ShapeDtypeStruct((1, LANES), F32)],
        scratch_shapes=[pltpu.VMEM(w_out.shape, w_out.dtype), pltpu.SemaphoreType.DMA((pieces,))],
        compiler_params=_cparams(("arbitrary",)),
    )(cat, w_out, x, target, g_post)


def matmul_nt(a, b, name):
    m, k = a.shape
    n = b.shape[0]
    tm = _tile(m, (512, 256, 128))

    def body(a_ref, b_ref, o_ref):
        o_ref[...] = _dot(a_ref[...], b_ref[...], 1, 1).astype(o_ref.dtype)

    return pl.pallas_call(
        body, name=name, grid=(m // tm,),
        in_specs=[pl.BlockSpec((tm, k), lambda i: (i, 0)), pl.BlockSpec((n, k), lambda i: (0, 0))],
        out_specs=pl.BlockSpec((tm, n), lambda i: (i, 0)),
        out_shape=jax.ShapeDtypeStruct((m, n), BF16),
        compiler_params=_cparams(("arbitrary",)),
    )(a, b)


def matmul_tn(a, b, tm, name, comms=()):
    k, m = a.shape
    n = b.shape[1]
    steps = m // tm
    hosted = _Hosted(comms)

    kc = _tile(k, (1024, 128))
    pieces = k // kc

    def body(*refs):
        (a_ref, b_hbm), (o_ref, cs_ref), (b_ref, b_sems), phases = hosted.split(refs, 2, 2, 2)
        step = pl.program_id(0)
        _before_step(phases, step, steps)

        def b_load(j):
            return pltpu.make_async_copy(b_hbm.at[j * kc:(j + 1) * kc, :], b_ref.at[j * kc:(j + 1) * kc, :], b_sems.at[j])

        @pl.when(step == 0)
        def _():
            for j in range(pieces):
                b_load(j).start()
            acc = None
            for j in range(pieces):
                b_load(j).wait()
                part = _dot(a_ref[j * kc:(j + 1) * kc, :], b_ref[j * kc:(j + 1) * kc, :], 0, 0)
                acc = part if acc is None else acc + part
            o_ref[...] = acc.astype(o_ref.dtype)

        @pl.when(step > 0)
        def _():
            o_ref[...] = _dot(a_ref[...], b_ref[...], 0, 0).astype(o_ref.dtype)

        rows = _tile(k, (512, 128))
        cs = jnp.zeros((1, tm), F32)
        for r0 in range(0, k, rows):
            cs = cs + jnp.sum(a_ref[r0:r0 + rows, :].astype(F32), axis=0, keepdims=True)
        cs_ref[...] = cs
        _after_step(phases, step, steps)

    return pl.pallas_call(
        body, name=name, grid=(steps,),
        in_specs=[pl.BlockSpec((k, tm), lambda i: (0, i)), ANY] + hosted.in_specs,
        out_specs=[pl.BlockSpec((tm, n), lambda i: (i, 0)), pl.BlockSpec((1, tm), lambda i: (0, i))] + hosted.out_specs,
        out_shape=[jax.ShapeDtypeStruct((m, n), BF16), jax.ShapeDtypeStruct((1, m), F32)] + hosted.out_shape,
        scratch_shapes=[pltpu.VMEM((k, n), b.dtype), pltpu.SemaphoreType.DMA((pieces,))] + hosted.scratch,
        compiler_params=_cparams(("arbitrary",)),
    )(a, b, *hosted.inputs)


def in_proj_bwd(dproj, wt, x, g_pre, dout, comms=(), after=None):
    s, d = x.shape
    tm = _tile(s, (512, 256, 128))
    steps = s // tm
    nsub = tm // CHUNK
    kw = 8 * LANES
    kchunks = [(k0, kw) for k0 in range(0, D_IN - D_IN % kw, kw)]
    if D_IN % kw:
        kchunks.append((D_IN - D_IN % kw, D_IN % kw))
    ksplit = len(kchunks)
    hosted = _Hosted(comms)
    order_only = [] if after is None else [after]

    def body(*refs):
        ((*dp_refs, w_hbm, x_hbm, g_ref, dout_hbm), (gx_hbm, dg_ref),
         (w_ref, w_sems, xbuf, dbuf, gbuf, in_sems, out_sems), phases) = hosted.split(
             refs[:4 + ksplit] + refs[4 + ksplit + len(order_only):], 4 + ksplit, 2, 7)
        step = pl.program_id(0)
        _before_step(phases, step, steps)

        def rows_of(ref, c):
            return ref.at[pl.ds(pl.multiple_of(step * tm + c * CHUNK, CHUNK), CHUNK), :]

        def fetches(c):
            return (pltpu.make_async_copy(rows_of(x_hbm, c), xbuf.at[c], in_sems.at[c]),
                    pltpu.make_async_copy(rows_of(dout_hbm, c), dbuf.at[c], in_sems.at[nsub + c]))

        def put(c):
            return pltpu.make_async_copy(gbuf.at[c % 2], rows_of(gx_hbm, c), out_sems.at[c % 2])

        for c in range(nsub):
            for cp in fetches(c):
                cp.start()

        def w_load(j):
            k0, kw = kchunks[j]
            return pltpu.make_async_copy(w_hbm.at[k0:k0 + kw, :], w_ref.at[k0:k0 + kw, :], w_sems.at[j])

        @pl.when(step == 0)
        def _():
            dg_ref[...] = jnp.zeros_like(dg_ref)
            for j in range(ksplit):
                w_load(j).start()

        dh_all = None
        for j, ((k0, kw), dp_ref) in enumerate(zip(kchunks, dp_refs)):
            @pl.when(step == 0)
            def _():
                w_load(j).wait()

            part = _dot(dp_ref[...], w_ref[k0:k0 + kw, :], 1, 0)
            dh_all = part if dh_all is None else dh_all + part
        for c in range(nsub):
            for cp in fetches(c):
                cp.wait()
            if c >= 2:
                put(c - 2).wait()
            elif c < nsub:
                @pl.when(step > 0)
                def _():
                    put(max(nsub - 2, 0) + c).wait()
            dh = dh_all[c * CHUNK:(c + 1) * CHUNK, :]
            xv = xbuf[c]
            r = lax.rsqrt(jnp.mean(xv * xv, axis=-1, keepdims=True) + EPS)
            xn = xv * r
            dg_ref[...] += jnp.sum(dh * xn, axis=0, keepdims=True)
            dn = dh * g_ref[...]
            gbuf[c % 2] = dbuf[c] + r * (dn - xn * jnp.mean(dn * xn, axis=-1, keepdims=True))
            put(c).start()
        @pl.when(step == steps - 1)
        def _():
            for c in range(max(nsub - 2, 0), nsub):
                put(c).wait()

        _after_step(phases, step, steps)

    side_in, side_out = pltpu.VMEM((nsub, CHUNK, d), F32), pltpu.VMEM((2, CHUNK, d), F32)
    row = pl.BlockSpec((1, d), lambda i: (0, 0))
    return pl.pallas_call(
        body, name="in_proj_bwd", grid=(steps,),
        in_specs=[pl.BlockSpec((tm, kw), functools.partial(lambda j, i: (i, j), k0 // kw)) for k0, kw in kchunks]
        + [ANY, ANY, row, ANY] + [ANY] * len(order_only) + hosted.in_specs,
        out_specs=[ANY, row] + hosted.out_specs,
        out_shape=[jax.ShapeDtypeStruct((s, d), F32), jax.ShapeDtypeStruct((1, d), F32)] + hosted.out_shape,
        scratch_shapes=[pltpu.VMEM((D_IN, d), BF16), pltpu.SemaphoreType.DMA((ksplit,)), side_in, side_in, side_out,
                        pltpu.SemaphoreType.DMA((2 * nsub,)), pltpu.SemaphoreType.DMA((2,))] + hosted.scratch,
        compiler_params=_cparams(("arbitrary",)),
    )(*([dproj] * ksplit), wt, x, g_pre, dout, *order_only, *hosted.inputs)


def _lane_iota(shape):
    return lax.broadcasted_iota(jnp.int32, shape, len(shape) - 1)


def _partner(v):
    low = (_lane_iota(v.shape) % HEAD_DIM) < (HEAD_DIM // 2)
    return jnp.where(low, pltpu.roll(v, LANES - HEAD_DIM // 2, 1), pltpu.roll(v, HEAD_DIM // 2, 1))


def _rope(v, cos, sin_signed):
    return v * cos + _partner(v) * sin_signed


def _rope_transposed(dv, cos, sin_signed):
    return dv * cos - _partner(dv) * sin_signed


def _both_halves(v, kv_head):
    keep = (_lane_iota(v.shape) >= HEAD_DIM) if kv_head else (_lane_iota(v.shape) < HEAD_DIM)
    return jnp.where(keep, v, pltpu.roll(v, HEAD_DIM, 1))


def _fold_halves(acc):
    return acc + pltpu.roll(acc, HEAD_DIM, 1)


def _by_half(a, b):
    shape = jnp.broadcast_shapes(jnp.shape(a), jnp.shape(b))
    return jnp.where(_lane_iota(shape) < HEAD_DIM, a, b)


def _stack_heads(pair):
    return jnp.concatenate([_by_half(pair, 0.0), _by_half(0.0, pair)], axis=0)


def _band_bias(has_prev):
    i = lax.broadcasted_iota(jnp.int32, (2 * CHUNK, 2 * CHUNK), 0) % CHUNK
    j = lax.broadcasted_iota(jnp.int32, (2 * CHUNK, 2 * CHUNK), 1)
    band = jnp.logical_and(j > i, j <= i + CHUNK)
    return jnp.where(jnp.logical_and(band, jnp.logical_or(j >= CHUNK, has_prev)), 0.0, NEG)


def _probs_staged(qm2s, kk2s, bias, sink_cols):
    k = range(len(qm2s))
    scs = [_dot(qm2s[i], kk2s[i], 1, 1) + bias for i in k]
    mxs = [jnp.maximum(jnp.max(scs[i], axis=-1, keepdims=True), sink_cols[i]) for i in k]
    ps = [jnp.exp(scs[i] - mxs[i]) for i in k]
    ess = [jnp.exp(sink_cols[i] - mxs[i]) for i in k]
    invs = [1.0 / (jnp.sum(ps[i], axis=-1, keepdims=True) + ess[i]) for i in k]
    return [ps[i] * invs[i] for i in k], [ess[i] * invs[i] for i in k]


def _sink_col(sinks_ref, pair):
    row = lax.broadcasted_iota(jnp.int32, (2 * CHUNK, 1), 0)
    return jnp.where(row < CHUNK, sinks_ref[2 * pair], sinks_ref[2 * pair + 1])


def _layer_norm_parts(v):
    mu = jnp.mean(v, axis=-1, keepdims=True)
    xc = v - mu
    rstd = lax.rsqrt(jnp.mean(xc * xc, axis=-1, keepdims=True) + EPS)
    return xc * rstd, rstd


def _masked_spatial(w_ref, g):
    t = lax.broadcasted_iota(jnp.int32, (CHUNK, CHUNK), 0)
    sidx = lax.broadcasted_iota(jnp.int32, (CHUNK, CHUNK), 1)
    return jnp.where(t >= sidx, w_ref[g], 0.0).astype(BF16)


def _keys_values(kv_ref, kvp_ref, rope_ref, ropep_ref):
    cos_c, sin_c = rope_ref[:, :LANES], rope_ref[:, LANES:]
    cos_p, sin_p = ropep_ref[:, :LANES], ropep_ref[:, LANES:]
    k_c = _rope(kv_ref[:, :D_KV].astype(F32), cos_c, sin_c)
    k_p = _rope(kvp_ref[:, :D_KV].astype(F32), cos_p, sin_p)
    keys = jnp.concatenate([k_p, k_c], axis=0)
    vals = jnp.concatenate([kvp_ref[:, D_KV:], kv_ref[:, D_KV:]], axis=0).astype(F32)
    return keys, vals, (cos_c, sin_c, cos_p, sin_p)


def mixer_fwd(proj, rope, ln_g, ln_b, w_sp, b_sp_rows, sinks, comms=()):
    s = proj.shape[0]
    nb = s // CHUNK
    hosted = _Hosted(comms)

    def body(sinks_ref, *refs):
        ((proj_ref, kvp_ref, rope_ref, ropep_ref, lng_ref, lnb_ref, w_ref, b_ref), (cat_ref, p_ref), _,
         phases) = hosted.split(refs, 8, 2, 0)
        n = pl.program_id(0)
        _before_step(phases, n, nb)
        xhat, _ = _layer_norm_parts(proj_ref[:, OFF_V:OFF_V + D_GMLP].astype(F32))
        vnb = (xhat * lng_ref[...] + lnb_ref[...]).astype(BF16)
        mixeds = [_dot(_masked_spatial(w_ref, g), vnb[:, g * CHUNK:(g + 1) * CHUNK], 1, 0) + b_ref[g]
                  for g in range(GROUPS)]
        for g in range(GROUPS):
            za = proj_ref[:, OFF_ZA + g * CHUNK:OFF_ZA + (g + 1) * CHUNK].astype(F32)
            u = proj_ref[:, OFF_U + g * CHUNK:OFF_U + (g + 1) * CHUNK].astype(F32)
            cat_ref[:, g * CHUNK:(g + 1) * CHUNK] = (u * mixeds[g] * (za * _sigmoid(za))).astype(BF16)
        kv_ref = proj_ref.at[:, OFF_K:OFF_K + 2 * D_KV]
        keys, vals, (cos_c, sin_c, _, _) = _keys_values(kv_ref, kvp_ref, rope_ref, ropep_ref)
        cos_q, sin_q = cos_c * SCALE, sin_c * SCALE
        bias = _band_bias(n > 0)
        kk2 = [_both_halves(keys, kvh).astype(BF16) for kvh in range(N_KV_HEADS)]
        vv2 = [_both_halves(vals, kvh).astype(BF16) for kvh in range(N_KV_HEADS)]
        first_col = _lane_iota((2 * CHUNK, 2 * CHUNK)) == 0
        for kvh in range(N_KV_HEADS):
            pairs = range(kvh * PAIRS_PER_KV, (kvh + 1) * PAIRS_PER_KV)
            qms = [_stack_heads(_rope(proj_ref[:, OFF_Q + pair * LANES:OFF_Q + (pair + 1) * LANES].astype(F32),
                                      cos_q, sin_q)).astype(BF16) for pair in pairs]
            probs, sink_probs = _probs_staged(qms, [kk2[kvh]] * PAIRS_PER_KV, bias,
                                              [_sink_col(sinks_ref, pair) for pair in pairs])
            pbs = [p.astype(BF16) for p in probs]
            outs = [_dot(pb, vv2[kvh], 1, 0) for pb in pbs]
            for pair, pb, sink_prob in zip(pairs, pbs, sink_probs):
                p_ref[0, pair] = jnp.where(first_col, sink_prob.astype(BF16), pb)
            for pair, out in zip(pairs, outs):
                out_pair = _by_half(out[:CHUNK], out[CHUNK:])
                zb = proj_ref[:, OFF_ZB + pair * LANES:OFF_ZB + (pair + 1) * LANES].astype(F32)
                cat_ref[:, D_GMLP + pair * LANES:D_GMLP + (pair + 1) * LANES] = (
                    out_pair * (zb * _sigmoid(zb))).astype(BF16)
        _after_step(phases, n, nb)

    prev = lambda n, *_: (jnp.maximum(n - 1, 0), 0)
    kv_block = OFF_K // (2 * D_KV)
    return pl.pallas_call(
        body, name="mixer_fwd",
        grid_spec=pltpu.PrefetchScalarGridSpec(
            num_scalar_prefetch=1, grid=(nb,),
            in_specs=[pl.BlockSpec((CHUNK, D_IN), lambda n, *_: (n, 0)),
                      pl.BlockSpec((CHUNK, 2 * D_KV), lambda n, *_: (jnp.maximum(n - 1, 0), kv_block)),
                      pl.BlockSpec((CHUNK, 2 * LANES), lambda n, *_: (n, 0)),
                      pl.BlockSpec((CHUNK, 2 * LANES), prev),
                      pl.BlockSpec((1, D_GMLP), lambda n, *_: (0, 0)),
                      pl.BlockSpec((1, D_GMLP), lambda n, *_: (0, 0)),
                      pl.BlockSpec((GROUPS, CHUNK, CHUNK), lambda n, *_: (0, 0, 0)),
                      pl.BlockSpec((GROUPS, CHUNK, CHUNK), lambda n, *_: (0, 0, 0))] + hosted.in_specs,
            out_specs=[pl.BlockSpec((CHUNK, D_GMLP + D_ATTN), lambda n, *_: (n, 0)),
                       pl.BlockSpec((1, N_PAIRS, 2 * CHUNK, 2 * CHUNK), lambda n, *_: (n, 0, 0, 0))]
            + hosted.out_specs,
            scratch_shapes=hosted.scratch),
        out_shape=[jax.ShapeDtypeStruct((s, D_GMLP + D_ATTN), BF16),
                   jax.ShapeDtypeStruct((nb, N_PAIRS, 2 * CHUNK, 2 * CHUNK), BF16)] + hosted.out_shape,
        compiler_params=_cparams(("arbitrary",)),
    )(sinks, proj, proj, rope, rope, ln_g, ln_b, w_sp, b_sp_rows, *hosted.inputs)


def mixer_bwd(proj, dcat, probs, rope, ln_g, ln_b, w_sp, b_sp_rows, comms=()):
    s = proj.shape[0]
    nb = s // CHUNK
    hosted = _Hosted(comms)

    def body(*refs):
        ((proj_ref, kvp_ref, dcat_ref, p_ref, rope_ref, ropep_ref, lng_ref, lnb_ref, w_ref, b_ref),
         (dproj_ref, dw_ref, db_ref, dlng_ref, dlnb_ref, dsink_ref),
         (pend_ref, pend_kv_ref, dbacc_ref), phases) = hosted.split(refs, 10, 6, 3)
        n = pl.program_id(0)
        _before_step(phases, n, nb + 1)

        @pl.when(n == 0)
        def _():
            dw_ref[...] = jnp.zeros_like(dw_ref)
            dbacc_ref[...] = jnp.zeros_like(dbacc_ref)
            dlng_ref[...] = jnp.zeros_like(dlng_ref)
            dlnb_ref[...] = jnp.zeros_like(dlnb_ref)
            dsink_ref[...] = jnp.zeros_like(dsink_ref)

        @pl.when(n > 0)
        def _():
            dproj_ref[...] = pend_ref[...]

        def flush(dkv_prev):
            @pl.when(n > 0)
            def _():
                dproj_ref[:, OFF_K:OFF_K + 2 * D_KV] = (pend_kv_ref[...] + dkv_prev).astype(BF16)

        @pl.when(n < nb)
        def _():
            kv_ref = proj_ref.at[:, OFF_K:OFF_K + 2 * D_KV]
            keys, vals, (cos_c, sin_c, cos_p, sin_p) = _keys_values(kv_ref, kvp_ref, rope_ref, ropep_ref)
            cos_q, sin_q = cos_c * SCALE, sin_c * SCALE
            first_col = _lane_iota((2 * CHUNK, 2 * CHUNK)) == 0
            lane_row = _lane_iota((1, LANES))
            dsink = jnp.zeros((1, LANES), F32)
            dk_heads, dv_heads = [], []
            for kvh in range(N_KV_HEADS):
                kk2 = _both_halves(keys, kvh).astype(BF16)
                vv2 = _both_halves(vals, kvh).astype(BF16)
                pairs = list(range(kvh * PAIRS_PER_KV, (kvh + 1) * PAIRS_PER_KV))
                k4 = range(PAIRS_PER_KV)
                qm2s = [_stack_heads(_rope(proj_ref[:, OFF_Q + pair * LANES:OFF_Q + (pair + 1) * LANES].astype(F32),
                                           cos_q, sin_q)).astype(BF16) for pair in pairs]
                kept = [p_ref[0, pair] for pair in pairs]
                pbs = [jnp.where(first_col, jnp.zeros_like(kp), kp) for kp in kept]
                ps = [pb.astype(F32) for pb in pbs]
                p_sinks = [kp[:, 0:1].astype(F32) for kp in kept]
                o2s = [_dot(pb, vv2, 1, 0) for pb in pbs]
                zbs = [proj_ref[:, OFF_ZB + pair * LANES:OFF_ZB + (pair + 1) * LANES].astype(F32) for pair in pairs]
                sgs = [_sigmoid(zb) for zb in zbs]
                dybs = [dcat_ref[:, D_GMLP + pair * LANES:D_GMLP + (pair + 1) * LANES].astype(F32) for pair in pairs]
                for i, pair in enumerate(pairs):
                    out_pair = _by_half(o2s[i][:CHUNK], o2s[i][CHUNK:])
                    pend_ref[:, OFF_ZB + pair * LANES:OFF_ZB + (pair + 1) * LANES] = (
                        dybs[i] * out_pair * (sgs[i] * (1.0 + zbs[i] * (1.0 - sgs[i])))).astype(BF16)
                dom2s = [_stack_heads(dybs[i] * (zbs[i] * sgs[i])).astype(BF16) for i in k4]
                dps = [_dot(dom2, vv2, 1, 1) for dom2 in dom2s]
                deltas = [jnp.sum(ps[i] * dps[i], axis=-1, keepdims=True) for i in k4]
                dss = [ps[i] * (dps[i] - deltas[i]) for i in k4]
                for i, pair in enumerate(pairs):
                    dsk = -(p_sinks[i] * deltas[i])
                    dsink = dsink + jnp.where(lane_row == 2 * pair,
                                              jnp.sum(dsk[:CHUNK], axis=0, keepdims=True), 0.0)
                    dsink = dsink + jnp.where(lane_row == 2 * pair + 1,
                                              jnp.sum(dsk[CHUNK:], axis=0, keepdims=True), 0.0)
                dsbs = [ds.astype(BF16) for ds in dss]
                dq2s = [_dot(dsb, kk2, 1, 0) for dsb in dsbs]
                for pair, dq2 in zip(pairs, dq2s):
                    pend_ref[:, OFF_Q + pair * LANES:OFF_Q + (pair + 1) * LANES] = _rope_transposed(
                        _by_half(dq2[:CHUNK], dq2[CHUNK:]), cos_q, sin_q).astype(BF16)
                dkks = [_dot(dsbs[i], qm2s[i], 0, 0) for i in k4]
                dvvs = [_dot(pbs[i], dom2s[i], 0, 0) for i in k4]
                dk_heads.append(_fold_halves((dkks[0] + dkks[1]) + (dkks[2] + dkks[3])))
                dv_heads.append(_fold_halves((dvvs[0] + dvvs[1]) + (dvvs[2] + dvvs[3])))
            dk_rot = _by_half(dk_heads[0], dk_heads[1])
            dv_all = _by_half(dv_heads[0], dv_heads[1])
            dk_p = _rope_transposed(dk_rot[:CHUNK], cos_p, sin_p)
            dk_c = _rope_transposed(dk_rot[CHUNK:], cos_c, sin_c)
            flush(jnp.concatenate([dk_p, dv_all[:CHUNK]], axis=1))
            dsink_ref[...] += dsink
            pend_kv_ref[...] = jnp.concatenate([dk_c, dv_all[CHUNK:]], axis=1)
            xhat, rstd = _layer_norm_parts(proj_ref[:, OFF_V:OFF_V + D_GMLP].astype(F32))
            lng = lng_ref[...]
            vnb = (xhat * lng + lnb_ref[...]).astype(BF16)
            dvn_cols = []
            for g in range(GROUPS):
                cols = slice(g * CHUNK, (g + 1) * CHUNK)
                wm = _masked_spatial(w_ref, g)
                mixed = _dot(wm, vnb[:, cols], 1, 0) + b_ref[g]
                za = proj_ref[:, OFF_ZA + g * CHUNK:OFF_ZA + (g + 1) * CHUNK].astype(F32)
                u = proj_ref[:, OFF_U + g * CHUNK:OFF_U + (g + 1) * CHUNK].astype(F32)
                dya = dcat_ref[:, cols].astype(F32)
                sg = _sigmoid(za)
                sz = za * sg
                pend_ref[:, OFF_U + g * CHUNK:OFF_U + (g + 1) * CHUNK] = (dya * mixed * sz).astype(BF16)
                pend_ref[:, OFF_ZA + g * CHUNK:OFF_ZA + (g + 1) * CHUNK] = (
                    dya * u * mixed * (sg * (1.0 + za * (1.0 - sg)))).astype(BF16)
                dmixed = dya * u * sz
                dmb = dmixed.astype(BF16)
                dbacc_ref[g] += dmixed
                dw_ref[g] += _dot(dmb, vnb[:, cols], 1, 1)
                dvn_cols.append(_dot(wm, dmb, 0, 0))
            dvn = jnp.concatenate(dvn_cols, axis=1)
            dlng_ref[...] += jnp.sum(dvn * xhat, axis=0, keepdims=True)
            dlnb_ref[...] += jnp.sum(dvn, axis=0, keepdims=True)
            dxh = dvn * lng
            dv = rstd * (dxh - jnp.mean(dxh, axis=-1, keepdims=True)
                         - xhat * jnp.mean(dxh * xhat, axis=-1, keepdims=True))
            pend_ref[:, OFF_V:OFF_V + D_GMLP] = dv.astype(BF16)

        @pl.when(n == nb)
        def _():
            flush(jnp.zeros((CHUNK, 2 * D_KV), F32))
            t = lax.broadcasted_iota(jnp.int32, (CHUNK, CHUNK), 0)
            sidx = lax.broadcasted_iota(jnp.int32, (CHUNK, CHUNK), 1)
            lane = _lane_iota((CHUNK, LANES))
            dbt = jnp.zeros((CHUNK, LANES), F32)
            for g in range(GROUPS):
                dw_ref[g] = jnp.where(t >= sidx, dw_ref[g], 0.0)
                dbt = jnp.where(lane == g, jnp.sum(dbacc_ref[g], axis=-1, keepdims=True), dbt)
            db_ref[...] = jnp.transpose(dbt)[:GROUPS, :]

        _after_step(phases, n, nb + 1)

    cur = lambda n: (jnp.minimum(n, nb - 1), 0)
    prev = lambda n: (jnp.clip(n - 1, 0, nb - 1), 0)
    kv_block = OFF_K // (2 * D_KV)
    const2 = lambda n: (0, 0)
    const3 = lambda n: (0, 0, 0)
    return pl.pallas_call(
        body, name="mixer_bwd", grid=(nb + 1,),
        in_specs=[pl.BlockSpec((CHUNK, D_IN), cur),
                  pl.BlockSpec((CHUNK, 2 * D_KV), lambda n: (jnp.clip(n - 1, 0, nb - 1), kv_block)),
                  pl.BlockSpec((CHUNK, D_GMLP + D_ATTN), cur),
                  pl.BlockSpec((1, N_PAIRS, 2 * CHUNK, 2 * CHUNK), lambda n: (jnp.minimum(n, nb - 1), 0, 0, 0)),
                  pl.BlockSpec((CHUNK, 2 * LANES), cur),
                  pl.BlockSpec((CHUNK, 2 * LANES), prev),
                  pl.BlockSpec((1, D_GMLP), const2),
                  pl.BlockSpec((1, D_GMLP), const2),
                  pl.BlockSpec((GROUPS, CHUNK, CHUNK), const3),
                  pl.BlockSpec((GROUPS, CHUNK, CHUNK), const3)] + hosted.in_specs,
        out_specs=[pl.BlockSpec((CHUNK, D_IN), lambda n: (jnp.maximum(n - 1, 0), 0)),
                   pl.BlockSpec((GROUPS, CHUNK, CHUNK), const3),
                   pl.BlockSpec((GROUPS, CHUNK), const2),
                   pl.BlockSpec((1, D_GMLP), const2),
                   pl.BlockSpec((1, D_GMLP), const2),
                   pl.BlockSpec((1, LANES), const2)] + hosted.out_specs,
        scratch_shapes=[pltpu.VMEM((CHUNK, D_IN), BF16), pltpu.VMEM((CHUNK, 2 * D_KV), F32),
                        pltpu.VMEM((GROUPS, CHUNK, CHUNK), F32)] + hosted.scratch,
        out_shape=[jax.ShapeDtypeStruct((s, D_IN), BF16),
                   jax.ShapeDtypeStruct((GROUPS, CHUNK, CHUNK), F32),
                   jax.ShapeDtypeStruct((GROUPS, CHUNK), F32),
                   jax.ShapeDtypeStruct((1, D_GMLP), F32),
                   jax.ShapeDtypeStruct((1, D_GMLP), F32),
                   jax.ShapeDtypeStruct((1, LANES), F32)] + hosted.out_shape,
        compiler_params=_cparams(("arbitrary",)),
    )(proj, proj, dcat, probs, rope, rope, ln_g, ln_b, w_sp, b_sp_rows, *hosted.inputs)


def _adamw_math(w, g, m, v):
    m = ADAM_B1 * m + (1.0 - ADAM_B1) * g
    v = ADAM_B2 * v + (1.0 - ADAM_B2) * (g * g)
    m_hat = m / (1.0 - ADAM_B1 ** ADAM_STEP)
    v_hat = v / (1.0 - ADAM_B2 ** ADAM_STEP)
    delta = -ADAM_LR * (m_hat / (jnp.sqrt(v_hat) + ADAM_EPS) + ADAM_WD * w)
    return delta, m, v


def adamw_shard(terms, w, m, v, name, after=None):
    r, c = w.shape
    tr = _tile(r, (224, 256, 128, 8))
    n_terms = len(terms)
    order_only = [] if after is None else [after]

    def body(*refs):
        w_ref, m_ref, v_ref, g_ref, d_ref, nm_ref, nv_ref = refs[n_terms:n_terms + 3] + refs[-4:]
        g = None
        for ref, (_, slots) in zip(refs[:n_terms], terms):
            for k in range(slots):
                part = ref[k].astype(F32)
                g = part if g is None else g + part
        g_ref[...] = g
        d_ref[...], nm_ref[...], nv_ref[...] = _adamw_math(w_ref[...], g, m_ref[...], v_ref[...])

    spec = pl.BlockSpec((tr, c), lambda i: (i, 0))
    return pl.pallas_call(
        body, name=name, grid=(r // tr,),
        in_specs=[pl.BlockSpec((slots, tr, c), lambda i: (0, i, 0)) for _, slots in terms] + [spec] * 3
        + [ANY] * len(order_only),
        out_specs=[spec] * 4, out_shape=[jax.ShapeDtypeStruct((r, c), F32)] * 4,
        compiler_params=_cparams(("arbitrary",)),
    )(*[a for a, _ in terms], w, m, v, *order_only)


def adamw_small(gathered, lane_windows, params):
    n_par = len(params)

    def body(*refs):
        g_refs = refs[:n_par + 1]
        wmv_refs = refs[n_par + 1:4 * n_par + 1]
        out_refs = refs[4 * n_par + 1:]

        def total(ref):
            acc = ref[0]
            for dev in range(1, N_DEV):
                acc = acc + ref[dev]
            return acc

        for i in range(n_par):
            w_ref, m_ref, v_ref = wmv_refs[3 * i:3 * i + 3]
            g = total(g_refs[i])
            if lane_windows[i] is not None:
                start, size = lane_windows[i]
                g = g[..., start:start + size]
            delta, new_m, new_v = _adamw_math(w_ref[...], g, m_ref[...], v_ref[...])
            for ref, val in zip(out_refs[4 * i:4 * i + 4], (g, delta, new_m, new_v)):
                ref[...] = val
        out_refs[4 * n_par][...] = total(g_refs[n_par])

    flat = [a for wmv in params for a in wmv]
    out_shape = [jax.ShapeDtypeStruct(w.shape, F32) for (w, _, _) in params for _ in range(4)]
    out_shape.append(jax.ShapeDtypeStruct(gathered[-1].shape[1:], F32))
    outs = pl.pallas_call(body, name="adamw_small", out_shape=out_shape, compiler_params=_cparams())(*gathered, *flat)
    return [tuple(outs[4 * i:4 * i + 4]) for i in range(n_par)], outs[-1]


def kernel(x, positions, g_pre, w_in, b_qkv, ln_v_g, ln_v_b, w_spatial, b_spatial, attn_sinks, w_out, g_post, loss_target, m_g_pre, m_w_in, m_b_qkv, m_ln_v_g, m_ln_v_b, m_w_spatial, m_b_spatial, m_attn_sinks, m_w_out, m_g_post, v_g_pre, v_w_in, v_b_qkv, v_ln_v_g, v_ln_v_b, v_w_spatial, v_b_spatial, v_attn_sinks, v_w_out, v_g_post):
    x2, target = x[0], loss_target[0]
    seq = x2.shape[0]

    wt_shard = w_in[0].T.astype(BF16)
    wo_shard = w_out[0].astype(BF16)
    pos_col = positions.reshape(seq, 1)
    half = HEAD_DIM // 2
    inv_freq = ROPE_THETA ** (-jnp.arange(half, dtype=F32) * (2.0 / HEAD_DIM))
    freq = jnp.tile(inv_freq, LANES // half).reshape(1, LANES)
    sign = jnp.tile(jnp.concatenate([-jnp.ones((half,), F32), jnp.ones((half,), F32)]), LANES // HEAD_DIM)
    sign = sign.reshape(1, LANES)
    bias = jnp.concatenate([jnp.zeros((1, OFF_Q), F32), b_qkv, jnp.zeros((1, D_ATTN), F32)], axis=1)
    proj, h, rope, wt = in_proj_gather(x2, pos_col, freq, sign, g_pre, wt_shard, bias)

    b_rows = jnp.broadcast_to(b_spatial[0][:, :, None], (GROUPS, CHUNK, CHUNK))
    sinks = attn_sinks[0]
    cat, probs, wo = mixer_fwd(proj, rope, ln_v_g, ln_v_b, w_spatial[0], b_rows, sinks,
                               comms=[gather_comm([wo_shard])])
    dy, dout, d_g_post, loss_part = out_proj_loss(cat, wo, x2, target, g_post)

    dcat = matmul_nt(dy, wo, "out_proj_bwd")
    d_wo, _ = matmul_tn(cat, dy, 512, "w_out_grad")
    dproj, d_w_sp, d_b_sp, d_ln_g, d_ln_b, d_sinks, parts_wo = mixer_bwd(
        proj, dcat, probs, rope, ln_v_g, ln_v_b, w_spatial[0], b_rows, comms=[scatter_comm([d_wo])])
    small_parts = [d_ln_g, d_ln_b, d_w_sp, d_b_sp, d_sinks, d_g_post, loss_part]
    d_wt, colsum, *landed = matmul_tn(dproj, h, 768, "w_in_grad", comms=[gather_comm(small_parts, stack=True)])

    sum_wt = pair_exchange_sum(d_wt, "grad_pair_sum_w_in")
    started = chips_exchange_start(sum_wt)
    grad_x, d_g_pre = in_proj_bwd(dproj, wt, x2, g_pre, dout, after=started[-1])
    sum_wt, far_wt = chips_exchange_wait(started, d_g_pre)
    late = jnp.concatenate([d_g_pre, colsum], axis=1)
    late_started, late_copy = gather_start(late)

    wo_out = adamw_shard([(parts_wo, N_DEV)], w_out[0], m_w_out[0], v_w_out[0], "adamw_w_out", after=late_started[-1])
    wt_out = adamw_shard([(sum_wt, 1), (far_wt, 3)], w_in[0].T, m_w_in[0].T, v_w_in[0].T, "adamw_w_in",
                         after=wo_out[0])
    landed_late = gather_wait(late_started, late_copy, wt_out[0])
    my_index = _index(*_my_place())
    all_late = lax.dynamic_update_slice(landed_late, late[None], (my_index, 0, 0))
    gathered = [all_late, all_late] + landed
    windows = [(0, D_MODEL), (D_MODEL + OFF_Q, D_QKV), None, None, None, None, (0, N_Q_HEADS), None]
    small = [(g_pre, m_g_pre, v_g_pre), (b_qkv, m_b_qkv, v_b_qkv), (ln_v_g, m_ln_v_g, v_ln_v_g),
             (ln_v_b, m_ln_v_b, v_ln_v_b), (w_spatial[0], m_w_spatial[0], v_w_spatial[0]),
             (b_spatial[0], m_b_spatial[0], v_b_spatial[0]), (attn_sinks, m_attn_sinks, v_attn_sinks),
             (g_post, m_g_post, v_g_post)]
    small_out, loss_row = adamw_small(gathered, windows, small)
    lead = [False, False, False, False, True, True, False, False]
    small_out = [tuple(a[None] if ld else a for a in leaf) for leaf, ld in zip(small_out, lead)]

    def leaves(k):
        gp, bq, lg, lb, ws, bs, sk, gpo = (leaf[k] for leaf in small_out)
        return [gp, wt_out[k].T[None], bq, lg, lb, ws, bs, sk, wo_out[k][None], gpo]

    return (loss_row[0, 0], grad_x[None], *leaves(0), *leaves(1), *leaves(2), *leaves(3))
```

```python
import functools

import jax
import jax.numpy as jnp
from jax import lax
from jax.experimental import pallas as pl
from jax.experimental.pallas import tpu as pltpu

F32 = jnp.float32
BF16 = jnp.bfloat16

D_MODEL = 2048
D_GMLP = 1024
D_ATTN = 1024
CHUNK = 128
GROUPS = 8
HEAD_DIM = 64
N_Q_HEADS = 16
N_KV_HEADS = 2
D_KV = N_KV_HEADS * HEAD_DIM
D_IN = 3 * D_GMLP + D_ATTN + 2 * D_KV + D_ATTN
OFF_U, OFF_V, OFF_ZA = 0, D_GMLP, 2 * D_GMLP
OFF_Q = 3 * D_GMLP
OFF_K = OFF_Q + D_ATTN
OFF_VA = OFF_K + D_KV
OFF_ZB = OFF_VA + D_KV
D_QKV = D_ATTN + 2 * D_KV
ROPE_THETA = 10000.0
EPS = 1e-6
SCALE = HEAD_DIM ** -0.5
NEG = -1e30
N_PAIRS = N_Q_HEADS // 2
PAIRS_PER_KV = N_PAIRS // N_KV_HEADS

ADAM_LR = 0.001
ADAM_B1 = 0.9
ADAM_B2 = 0.999
ADAM_EPS = 1e-08
ADAM_WD = 0.01
ADAM_STEP = 10

N_DEV = 8
LANES = 128
VMEM_LIMIT = 56 * 1024 * 1024
IN_PROJ_VMEM_LIMIT = 61 * 1024 * 1024

MESH = pl.DeviceIdType.MESH
ANY = pl.BlockSpec(memory_space=pl.ANY)


def _cparams(sem=None):
    return pltpu.CompilerParams(dimension_semantics=sem, vmem_limit_bytes=VMEM_LIMIT)


def _tile(n, prefs):
    for t in prefs:
        if n % t == 0:
            return t
    return n


def _sigmoid(z):
    return 1.0 / (1.0 + jnp.exp(-z))


def _dot(a, b, ca, cb):
    return lax.dot_general(a, b, (((ca,), (cb,)), ((), ())), preferred_element_type=F32)


def _my_place():
    return lax.axis_index("x"), lax.axis_index("y"), lax.axis_index("c")


def _chip_of(x, y, r):
    return (x ^ (r & 1), y ^ (r >> 1))


def _peer(x, y, c, k):
    return (x ^ (k >> 2), y ^ ((k >> 1) & 1), c ^ (k & 1))


def _index(px, py, pc):
    return 4 * px + 2 * py + pc


class _Comm:
    def __init__(self, inputs, out_shape, scratch, bind):
        self.inputs, self.out_shape, self.scratch, self.bind = list(inputs), list(out_shape), list(scratch), bind


def gather_comm(shards, stack=False):
    n_arr = len(shards)

    def bind(ins, outs, sems):
        send_sems, recv_sems, local_sems = sems
        x, y, c = _my_place()
        me, sibling = (x, y, c), (x, y, 1 - c)
        chips = [_chip_of(x, y, r) for r in (1, 2, 3)]

        def rows(a, px, py, pc):
            d = _index(px, py, pc)
            if stack:
                return outs[a].at[d]
            m = shards[a].shape[0]
            return outs[a].at[pl.ds(pl.multiple_of(d * m, 8), m), :]

        def copy(a, k, block, to, src=None):
            return pltpu.make_async_remote_copy(
                src_ref=rows(a, *block) if src is None else src, dst_ref=rows(a, *block),
                send_sem=send_sems.at[a * 7 + k], recv_sem=recv_sems.at[a * 7 + k],
                device_id=to, device_id_type=MESH)

        def mine(a):
            return pltpu.make_async_copy(ins[a], rows(a, *me), local_sems.at[a])

        def own_sends(a):
            return ([copy(a, 0, me, sibling, src=ins[a])]
                    + [copy(a, 1 + j, me, (*chip, c), src=ins[a]) for j, chip in enumerate(chips)])

        def start():
            for a in range(n_arr):
                mine(a).start()
                for cp in own_sends(a):
                    cp.start()

        def relay():
            for j, chip in enumerate(chips):
                for a in range(n_arr):
                    copy(a, 1 + j, (*chip, c), me).wait_recv()
                    copy(a, 4 + j, (*chip, c), sibling).start()

        def finish():
            for a in range(n_arr):
                copy(a, 0, sibling, me).wait_recv()
                for j, chip in enumerate(chips):
                    copy(a, 4 + j, (*chip, 1 - c), me).wait_recv()
                    copy(a, 4 + j, (*chip, c), sibling).wait_send()
                for cp in own_sends(a):
                    cp.wait_send()
                mine(a).wait()

        return start, relay, finish

    def gathered(s):
        return (N_DEV, *s.shape) if stack else (N_DEV * s.shape[0], s.shape[1])

    return _Comm(shards, [jax.ShapeDtypeStruct(gathered(s), s.dtype) for s in shards],
                 [pltpu.SemaphoreType.DMA((7 * n_arr,)), pltpu.SemaphoreType.DMA((7 * n_arr,)),
                  pltpu.SemaphoreType.DMA((n_arr,))], bind)


def scatter_comm(parts):
    n_arr = len(parts)

    def bind(ins, outs, sems):
        send_sems, recv_sems, local_sems = sems
        x, y, c = _my_place()
        my_index = _index(x, y, c)

        def block(a, d):
            m = parts[a].shape[0] // N_DEV
            return ins[a].at[pl.ds(pl.multiple_of(d * m, 16), m), :]

        def copy(a, k, slot):
            peer = _peer(x, y, c, k)
            return pltpu.make_async_remote_copy(
                src_ref=block(a, _index(*peer)), dst_ref=outs[a].at[slot],
                send_sem=send_sems.at[a * 7 + k - 1], recv_sem=recv_sems.at[a * 7 + k - 1],
                device_id=peer, device_id_type=MESH)

        def mine(a):
            return pltpu.make_async_copy(block(a, my_index), outs[a].at[my_index], local_sems.at[a])

        def start():
            for a in range(n_arr):
                mine(a).start()
                for k in range(1, 8):
                    copy(a, k, my_index).start()

        def finish():
            for a in range(n_arr):
                for k in range(1, 8):
                    copy(a, k, _index(*_peer(x, y, c, k))).wait_recv()
                    copy(a, k, my_index).wait_send()
                mine(a).wait()

        return start, (lambda: None), finish

    return _Comm(parts, [jax.ShapeDtypeStruct((N_DEV, p.shape[0] // N_DEV, p.shape[1]), p.dtype) for p in parts],
                 [pltpu.SemaphoreType.DMA((7 * n_arr,)), pltpu.SemaphoreType.DMA((7 * n_arr,)),
                  pltpu.SemaphoreType.DMA((n_arr,))], bind)


def run_comm(comm, name):
    n_in, n_out = len(comm.inputs), len(comm.out_shape)

    def body(*refs):
        start, relay, finish = comm.bind(refs[:n_in], refs[n_in:n_in + n_out], refs[n_in + n_out:])
        start()
        relay()
        finish()

    outs = pl.pallas_call(body, name=name, out_shape=comm.out_shape, in_specs=[ANY] * n_in,
                          out_specs=[ANY] * n_out, scratch_shapes=comm.scratch)(*comm.inputs)
    return list(outs)


def _chip_copy(r, src_ref, land_ref, send_sem, recv_sem):
    x, y, c = _my_place()
    return pltpu.make_async_remote_copy(src_ref=src_ref.at[r], dst_ref=land_ref.at[r - 1], send_sem=send_sem,
                                        recv_sem=recv_sem, device_id=(*_chip_of(x, y, r), c), device_id_type=MESH)


def chips_exchange_start(sums):
    def body(src_ref, land_ref, s1, s2, s3, r1, r2, r3, src_thru, land_thru, token):
        del src_thru, land_thru
        for r, send_sem, recv_sem in ((1, s1, r1), (2, s2, r2), (3, s3, r3)):
            _chip_copy(r, src_ref, land_ref, send_sem, recv_sem).start()
        token[...] = jnp.zeros_like(token)

    land = lax.empty((3,) + sums.shape[1:], sums.dtype)
    sem = pltpu.SemaphoreType.DMA(())
    hbm = pl.BlockSpec(memory_space=pltpu.HBM)
    sem_spec = pl.BlockSpec(memory_space=pltpu.SEMAPHORE)
    return pl.pallas_call(
        body, name="grad_exchange_chips_start",
        out_shape=(sem,) * 6 + (pltpu.HBM(sums.shape, sums.dtype), pltpu.HBM(land.shape, land.dtype),
                                jax.ShapeDtypeStruct((8, LANES), F32)),
        in_specs=(hbm, hbm), out_specs=(sem_spec,) * 6 + (hbm, hbm, pl.BlockSpec(memory_space=pltpu.VMEM)),
        input_output_aliases={0: 6, 1: 7},
        compiler_params=pltpu.CompilerParams(has_side_effects=pltpu.SideEffectType.DATAFLOW_SIDE_EFFECTING),
    )(pltpu.with_memory_space_constraint(sums, pltpu.HBM), pltpu.with_memory_space_constraint(land, pltpu.HBM))


def chips_exchange_wait(started, after):
    s1, s2, s3, r1, r2, r3, src_thru, land_thru, _ = started

    def body(src_ref, land_ref, s1, s2, s3, r1, r2, r3, after_ref, src_out, land_out):
        del after_ref, src_out, land_out
        for r, send_sem, recv_sem in ((1, s1, r1), (2, s2, r2), (3, s3, r3)):
            copy = _chip_copy(r, src_ref, land_ref, send_sem, recv_sem)
            copy.wait_send()
            copy.wait_recv()

    hbm = pl.BlockSpec(memory_space=pltpu.HBM)
    sem_spec = pl.BlockSpec(memory_space=pltpu.SEMAPHORE)
    return pl.pallas_call(
        body, name="grad_exchange_chips_wait",
        out_shape=(pltpu.HBM(src_thru.shape, src_thru.dtype), pltpu.HBM(land_thru.shape, land_thru.dtype)),
        in_specs=(hbm, hbm) + (sem_spec,) * 6 + (pl.BlockSpec(memory_space=pl.ANY),), out_specs=(hbm, hbm),
        input_output_aliases={0: 0, 1: 1},
        compiler_params=pltpu.CompilerParams(has_side_effects=pltpu.SideEffectType.DATAFLOW_SIDE_EFFECTING),
    )(src_thru, land_thru, s1, s2, s3, r1, r2, r3, after)


def gather_start(shard):
    def copy(k, src_ref, land_ref, send_sem, recv_sem):
        x, y, c = _my_place()
        return pltpu.make_async_remote_copy(src_ref=src_ref, dst_ref=land_ref.at[_index(x, y, c)], send_sem=send_sem,
                                            recv_sem=recv_sem, device_id=_peer(x, y, c, k), device_id_type=MESH)

    def start_body(src_ref, land_ref, *rest):
        sems, token = rest[:14], rest[16]
        for k in range(1, 8):
            copy(k, src_ref, land_ref, sems[k - 1], sems[7 + k - 1]).start()
        token[...] = jnp.zeros_like(token)

    land = lax.empty((N_DEV,) + shard.shape, shard.dtype)
    sem = pltpu.SemaphoreType.DMA(())
    hbm = pl.BlockSpec(memory_space=pltpu.HBM)
    sem_spec = pl.BlockSpec(memory_space=pltpu.SEMAPHORE)
    started = pl.pallas_call(
        start_body, name="allgather_late_grads_start",
        out_shape=(sem,) * 14 + (pltpu.HBM(shard.shape, shard.dtype), pltpu.HBM(land.shape, land.dtype),
                                 jax.ShapeDtypeStruct((8, LANES), F32)),
        in_specs=(hbm, hbm), out_specs=(sem_spec,) * 14 + (hbm, hbm, pl.BlockSpec(memory_space=pltpu.VMEM)),
        input_output_aliases={0: 14, 1: 15},
        compiler_params=pltpu.CompilerParams(has_side_effects=pltpu.SideEffectType.DATAFLOW_SIDE_EFFECTING),
    )(pltpu.with_memory_space_constraint(shard, pltpu.HBM), pltpu.with_memory_space_constraint(land, pltpu.HBM))
    return started, copy


def gather_wait(started, copy, after):
    sems, src_thru, land_thru = started[:14], started[14], started[15]

    def wait_body(src_ref, land_ref, *rest):
        for k in range(1, 8):
            cp = copy(k, src_ref, land_ref, rest[k - 1], rest[7 + k - 1])
            cp.wait_send()
            cp.wait_recv()

    hbm = pl.BlockSpec(memory_space=pltpu.HBM)
    sem_spec = pl.BlockSpec(memory_space=pltpu.SEMAPHORE)
    return pl.pallas_call(
        wait_body, name="allgather_late_grads_wait",
        out_shape=(pltpu.HBM(src_thru.shape, src_thru.dtype), pltpu.HBM(land_thru.shape, land_thru.dtype)),
        in_specs=(hbm, hbm) + (sem_spec,) * 14 + (pl.BlockSpec(memory_space=pl.ANY),), out_specs=(hbm, hbm),
        input_output_aliases={0: 0, 1: 1},
        compiler_params=pltpu.CompilerParams(has_side_effects=pltpu.SideEffectType.DATAFLOW_SIDE_EFFECTING),
    )(src_thru, land_thru, *sems, after)[1]


class _Hosted:
    def __init__(self, comms):
        self.comms = list(comms)
        self.inputs = [a for cm in self.comms for a in cm.inputs]
        self.out_shape = [s for cm in self.comms for s in cm.out_shape]
        self.scratch = [s for cm in self.comms for s in cm.scratch]
        self.in_specs = [ANY] * len(self.inputs)
        self.out_specs = [ANY] * len(self.out_shape)

    def split(self, refs, n_in, n_out, n_scratch):
        ni, no = len(self.inputs), len(self.out_shape)
        ins, rest = refs[:n_in], refs[n_in:]
        c_ins, rest = rest[:ni], rest[ni:]
        outs, rest = rest[:n_out], rest[n_out:]
        c_outs, rest = rest[:no], rest[no:]
        scratch, c_sems = rest[:n_scratch], rest[n_scratch:]
        phases = []
        for cm in self.comms:
            a, b, s = len(cm.inputs), len(cm.out_shape), len(cm.scratch)
            phases.append(cm.bind(c_ins[:a], c_outs[:b], c_sems[:s]))
            c_ins, c_outs, c_sems = c_ins[a:], c_outs[b:], c_sems[s:]
        return ins, outs, scratch, phases


def _before_step(phases, step, n_steps):
    if not phases:
        return

    @pl.when(step == 0)
    def _():
        for start, _, _ in phases:
            start()

    @pl.when(step == n_steps // 2)
    def _():
        for _, relay, _ in phases:
            relay()


def _after_step(phases, step, n_steps):
    if not phases:
        return

    @pl.when(step == n_steps - 1)
    def _():
        for _, _, finish in phases:
            finish()


def pair_exchange_sum(part, name):
    m, n = part.shape[0] // N_DEV, part.shape[1]

    def body(part_ref, out_ref, got, mine, summed, send_sems, recv_sems, in_sems, out_sems):
        x, y, c = _my_place()

        def rows(r, core):
            owner = _index(*_chip_of(x, y, r), core)
            return part_ref.at[pl.ds(pl.multiple_of(owner * m, 16), m), :]

        def to_sibling(r):
            return pltpu.make_async_remote_copy(src_ref=rows(r, 1 - c), dst_ref=got.at[r], send_sem=send_sems.at[r],
                                                recv_sem=recv_sems.at[r], device_id=(x, y, 1 - c), device_id_type=MESH)

        def fetch(r):
            return pltpu.make_async_copy(rows(r, c), mine.at[r % 2], in_sems.at[r % 2])

        def put(r):
            return pltpu.make_async_copy(summed.at[r % 2], out_ref.at[r], out_sems.at[r % 2])

        for r in range(4):
            to_sibling(r).start()
        fetch(0).start()
        for r in range(4):
            if r + 1 < 4:
                fetch(r + 1).start()
            fetch(r).wait()
            to_sibling(r).wait_recv()
            if r >= 2:
                put(r - 2).wait()
            summed[r % 2] = (mine[r % 2].astype(F32) + got[r].astype(F32)).astype(summed.dtype)
            put(r).start()
        for r in (2, 3):
            put(r).wait()
        for r in range(4):
            to_sibling(r).wait_send()

    return pl.pallas_call(
        body, name=name, out_shape=jax.ShapeDtypeStruct((4, m, n), part.dtype),
        in_specs=[ANY], out_specs=ANY,
        scratch_shapes=[pltpu.VMEM((4, m, n), part.dtype), pltpu.VMEM((2, m, n), part.dtype),
                        pltpu.VMEM((2, m, n), part.dtype), pltpu.SemaphoreType.DMA((4,)),
                        pltpu.SemaphoreType.DMA((4,)), pltpu.SemaphoreType.DMA((2,)), pltpu.SemaphoreType.DMA((2,))],
        compiler_params=_cparams(),
    )(part)


def in_proj_gather(x, pos_col, freq, sign, g_pre, wt_shard, bias):
    s, d = x.shape
    tm = _tile(s, (512, 256, 128))
    nt = s // tm
    tc = _tile(s, (256, 128))
    nc = s // tc
    m = wt_shard.shape[0]
    half = D_IN // 2
    xi = lax.axis_index("x")
    order = jnp.stack([xi, 1 - xi]).astype(jnp.int32)

    def body(order_ref, x_hbm, pos_hbm, freq_ref, sign_ref, g_ref, b_ref, shard_ref,
             proj_ref, h_hbm, rope_hbm, wt_ref,
             w_vmem, h_vmem, xbuf, posbuf, ropebuf, send_sems, recv_sems, local_sems, in_sems, out_sems):
        del order_ref
        p, i = pl.program_id(0), pl.program_id(1)
        xx, yy, cc = _my_place()
        me, sibling = (xx, yy, cc), (xx, yy, 1 - cc)
        chips = [_chip_of(xx, yy, r) for r in (1, 2, 3)]

        def rows(px, py, pc):
            return wt_ref.at[pl.ds(pl.multiple_of(_index(px, py, pc) * m, 16), m), :]

        def copy(k, block, to, src=None):
            return pltpu.make_async_remote_copy(
                src_ref=rows(*block) if src is None else src, dst_ref=rows(*block),
                send_sem=send_sems.at[k], recv_sem=recv_sems.at[k], device_id=to, device_id_type=MESH)

        def mine():
            return pltpu.make_async_copy(shard_ref, rows(*me), local_sems.at[0])

        def to_sibling():
            return copy(0, me, sibling, src=shard_ref)

        def to_chip(j):
            return copy(1 + j, me, (*chips[j], cc), src=shard_ref)

        def relay(j):
            copy(1 + j, (*chips[j], cc), me).wait_recv()
            copy(4 + j, (*chips[j], cc), sibling).start()

        def relayed(j):
            copy(4 + j, (*chips[j], 1 - cc), me).wait_recv()

        def load_half(which, slot):
            rows_of_half = wt_ref.at[pl.ds(pl.multiple_of(which * half, 16), half), :]
            load = pltpu.make_async_copy(rows_of_half, w_vmem.at[slot], local_sems.at[1 + slot])
            load.start()
            load.wait()

        def piece(ref, c):
            return ref.at[pl.ds(c * tc, tc), :]

        def fetch(c):
            return (pltpu.make_async_copy(piece(x_hbm, c), xbuf.at[c % 2], in_sems.at[c % 2]),
                    pltpu.make_async_copy(piece(pos_hbm, c), posbuf.at[c % 2], in_sems.at[2 + c % 2]))

        def put(c):
            return (pltpu.make_async_copy(piece(h_vmem, c), piece(h_hbm, c), out_sems.at[c % 2]),
                    pltpu.make_async_copy(ropebuf.at[c % 2], piece(rope_hbm, c), out_sems.at[2 + c % 2]))

        def prologue():
            for cp in fetch(0):
                cp.start()
            for c in range(nc):
                if c + 1 < nc:
                    for cp in fetch(c + 1):
                        cp.start()
                for cp in fetch(c):
                    cp.wait()
                if c >= 2:
                    for cp in put(c - 2):
                        cp.wait()
                xv = xbuf[c % 2]
                r = lax.rsqrt(jnp.mean(xv * xv, axis=-1, keepdims=True) + EPS)
                h_vmem[c * tc:(c + 1) * tc, :] = (xv * r * g_ref[...]).astype(BF16)
                ang = posbuf[c % 2].astype(F32) * freq_ref[...]
                ropebuf[c % 2, :, :LANES] = jnp.cos(ang)
                ropebuf[c % 2, :, LANES:] = jnp.sin(ang) * sign_ref[...]
                for cp in put(c):
                    cp.start()
            for c in range(max(nc - 2, 0), nc):
                for cp in put(c):
                    cp.wait()

        @pl.when(jnp.logical_and(p == 0, i == 0))
        def _():
            mine().start()
            to_sibling().start()
            to_chip(1).start()
            to_chip(0).start()
            prologue()
            copy(0, sibling, me).wait_recv()
            relay(1)
            relayed(1)
            mine().wait()
            to_chip(1).wait_send()
            to_chip(0).wait_send()
            to_chip(2).start()
            load_half(xx, 0)

        @pl.when(jnp.logical_and(p == 1, i == 0))
        def _():
            relayed(0)
            relayed(2)
            load_half(1 - xx, 1)

        def project(slot):
            hb = h_vmem[pl.ds(pl.multiple_of(i * tm, tm), tm), :]
            proj_ref[...] = (_dot(hb, w_vmem[slot], 1, 1) + b_ref[...]).astype(BF16)

        @pl.when(p == 0)
        def _():
            project(0)

        @pl.when(p == 1)
        def _():
            project(1)

        @pl.when(jnp.logical_and(p == 0, i == 1))
        def _():
            relay(0)

        @pl.when(jnp.logical_and(p == 0, i == nt - 1))
        def _():
            relay(2)

        @pl.when(jnp.logical_and(p == 1, i == nt - 1))
        def _():
            to_sibling().wait_send()
            to_chip(2).wait_send()
            for j in range(3):
                copy(4 + j, (*chips[j], cc), sibling).wait_send()

    const = lambda p, i, o: (0, 0)
    return pl.pallas_call(
        body, name="in_proj_gather",
        grid_spec=pltpu.PrefetchScalarGridSpec(
            num_scalar_prefetch=1, grid=(2, nt),
            in_specs=[ANY, ANY,
                      pl.BlockSpec((1, LANES), const),
                      pl.BlockSpec((1, LANES), const),
                      pl.BlockSpec((1, d), const),
                      pl.BlockSpec((1, half), lambda p, i, o: (0, o[p])),
                      ANY],
            out_specs=[pl.BlockSpec((tm, half), lambda p, i, o: (i, o[p])), ANY, ANY, ANY],
            scratch_shapes=[pltpu.VMEM((2, half, d), BF16), pltpu.VMEM((s, d), BF16),
                            pltpu.VMEM((2, tc, d), F32), pltpu.VMEM((2, tc, 1), jnp.int32),
                            pltpu.VMEM((2, tc, 2 * LANES), F32),
                            pltpu.SemaphoreType.DMA((7,)), pltpu.SemaphoreType.DMA((7,)),
                            pltpu.SemaphoreType.DMA((3,)), pltpu.SemaphoreType.DMA((4,)),
                            pltpu.SemaphoreType.DMA((4,))]),
        out_shape=[jax.ShapeDtypeStruct((s, D_IN), BF16), jax.ShapeDtypeStruct((s, d), BF16),
                   jax.ShapeDtypeStruct((s, 2 * LANES), F32), jax.ShapeDtypeStruct((D_IN, d), BF16)],
        compiler_params=pltpu.CompilerParams(dimension_semantics=("arbitrary", "arbitrary"),
                                             vmem_limit_bytes=IN_PROJ_VMEM_LIMIT),
    )(order, x, pos_col, freq, sign, g_pre, bias, wt_shard)


def out_proj_loss(cat, w_out, x, target, g_post):
    s, d = x.shape
    tm = _tile(s, (256, 128))
    kc = _tile(d, (512, 128))
    pieces = w_out.shape[0] // kc

    def body(cat_ref, w_hbm, x_ref, t_ref, g_ref, dy_ref, dout_ref, dg_ref, loss_ref, w_ref, w_sems):
        step = pl.program_id(0)

        def w_load(j):
            return pltpu.make_async_copy(w_hbm.at[j * kc:(j + 1) * kc, :], w_ref.at[j * kc:(j + 1) * kc, :], w_sems.at[j])

        @pl.when(step == 0)
        def _():
            for j in range(pieces):
                w_load(j).start()
            dg_ref[...] = jnp.zeros_like(dg_ref)
            loss_ref[...] = jnp.zeros_like(loss_ref)
            ys = [None] * (tm // CHUNK)
            for j in range(pieces):
                w_load(j).wait()
                for c in range(tm // CHUNK):
                    part = _dot(cat_ref[c * CHUNK:(c + 1) * CHUNK, j * kc:(j + 1) * kc], w_ref[j * kc:(j + 1) * kc, :], 1, 0)
                    ys[c] = part if ys[c] is None else ys[c] + part
            loss_and_back(ys, x_ref, t_ref, g_ref, dy_ref, dout_ref, dg_ref, loss_ref)

        @pl.when(step > 0)
        def _():
            ys = [_dot(cat_ref[c0:c0 + CHUNK, :], w_ref[...], 1, 0) for c0 in range(0, tm, CHUNK)]
            loss_and_back(ys, x_ref, t_ref, g_ref, dy_ref, dout_ref, dg_ref, loss_ref)

    def loss_and_back(ys, x_ref, t_ref, g_ref, dy_ref, dout_ref, dg_ref, loss_ref):
        g = g_ref[...]
        for c0 in range(0, tm, CHUNK):
            rows = slice(c0, c0 + CHUNK)
            yv = ys[c0 // CHUNK]
            r = lax.rsqrt(jnp.mean(yv * yv, axis=-1, keepdims=True) + EPS)
            nrm = yv * r
            err = x_ref[rows, :] + nrm * g - t_ref[rows, :]
            loss_ref[...] += 0.5 * jnp.sum(jnp.sum(err * err, axis=-1, keepdims=True), axis=0, keepdims=True) / d
            dout = err * (1.0 / d)
            dout_ref[rows, :] = dout
            dg_ref[...] += jnp.sum(dout * nrm, axis=0, keepdims=True)
            dn = dout * g
            dy = r * (dn - nrm * jnp.mean(dn * nrm, axis=-1, keepdims=True))
            dy_ref[rows, :] = dy.astype(BF16)

    return pl.pallas_call(
        body, name="out_proj_loss", grid=(s // tm,),
        in_specs=[pl.BlockSpec((tm, d), lambda i: (i, 0)),
                  ANY,
                  pl.BlockSpec((tm, d), lambda i: (i, 0)),
                  pl.BlockSpec((tm, d), lambda i: (i, 0)),
                  pl.BlockSpec((1, d), lambda i: (0, 0))],
        out_specs=[pl.BlockSpec((tm, d), lambda i: (i, 0)),
                   pl.BlockSpec((tm, d), lambda i: (i, 0)),
                   pl.BlockSpec((1, d), lambda i: (0, 0)),
                   pl.BlockSpec((1, LANES), lambda i: (0, 0))],
        out_shape=[jax.ShapeDtypeStruct((s, d), BF16), jax.ShapeDtypeStruct((s, d), F32),
                   jax.ShapeDtypeStruct((1, d), F32), jax.ShapeDtypeStruct((1, LANES), F32)],
        scratch_shapes=[pltpu.VMEM(w_out.shape, w_out.dtype), pltpu.SemaphoreType.DMA((pieces,))],
        compiler_params=_cparams(("arbitrary",)),
    )(cat, w_out, x, target, g_post)


def matmul_nt(a, b, name):
    m, k = a.shape
    n = b.shape[0]
    tm = _tile(m, (512, 256, 128))

    def body(a_ref, b_ref, o_ref):
        o_ref[...] = _dot(a_ref[...], b_ref[...], 1, 1).astype(o_ref.dtype)

    return pl.pallas_call(
        body, name=name, grid=(m // tm,),
        in_specs=[pl.BlockSpec((tm, k), lambda i: (i, 0)), pl.BlockSpec((n, k), lambda i: (0, 0))],
        out_specs=pl.BlockSpec((tm, n), lambda i: (i, 0)),
        out_shape=jax.ShapeDtypeStruct((m, n), BF16),
        compiler_params=_cparams(("arbitrary",)),
    )(a, b)


def matmul_tn(a, b, tm, name, comms=()):
    k, m = a.shape
    n = b.shape[1]
    steps = m // tm
    hosted = _Hosted(comms)

    kc = _tile(k, (1024, 128))
    pieces = k // kc

    def body(*refs):
        (a_ref, b_hbm), (o_ref, cs_ref), (b_ref, b_sems), phases = hosted.split(refs, 2, 2, 2)
        step = pl.program_id(0)
        _before_step(phases, step, steps)

        def b_load(j):
            return pltpu.make_async_copy(b_hbm.at[j * kc:(j + 1) * kc, :], b_ref.at[j * kc:(j + 1) * kc, :], b_sems.at[j])

        @pl.when(step == 0)
        def _():
            for j in range(pieces):
                b_load(j).start()
            acc = None
            for j in range(pieces):
                b_load(j).wait()
                part = _dot(a_ref[j * kc:(j + 1) * kc, :], b_ref[j * kc:(j + 1) * kc, :], 0, 0)
                acc = part if acc is None else acc + part
            o_ref[...] = acc.astype(o_ref.dtype)

        @pl.when(step > 0)
        def _():
            o_ref[...] = _dot(a_ref[...], b_ref[...], 0, 0).astype(o_ref.dtype)

        rows = _tile(k, (512, 128))
        cs = jnp.zeros((1, tm), F32)
        for r0 in range(0, k, rows):
            cs = cs + jnp.sum(a_ref[r0:r0 + rows, :].astype(F32), axis=0, keepdims=True)
        cs_ref[...] = cs
        _after_step(phases, step, steps)

    return pl.pallas_call(
        body, name=name, grid=(steps,),
        in_specs=[pl.BlockSpec((k, tm), lambda i: (0, i)), ANY] + hosted.in_specs,
        out_specs=[pl.BlockSpec((tm, n), lambda i: (i, 0)), pl.BlockSpec((1, tm), lambda i: (0, i))] + hosted.out_specs,
        out_shape=[jax.ShapeDtypeStruct((m, n), BF16), jax.ShapeDtypeStruct((1, m), F32)] + hosted.out_shape,
        scratch_shapes=[pltpu.VMEM((k, n), b.dtype), pltpu.SemaphoreType.DMA((pieces,))] + hosted.scratch,
        compiler_params=_cparams(("arbitrary",)),
    )(a, b, *hosted.inputs)


def in_proj_bwd(dproj, wt, x, g_pre, dout, comms=(), after=None):
    s, d = x.shape
    tm = _tile(s, (512, 256, 128))
    steps = s // tm
    nsub = tm // CHUNK
    kw = 8 * LANES
    kchunks = [(k0, kw) for k0 in range(0, D_IN - D_IN % kw, kw)]
    if D_IN % kw:
        kchunks.append((D_IN - D_IN % kw, D_IN % kw))
    ksplit = len(kchunks)
    hosted = _Hosted(comms)
    order_only = [] if after is None else [after]

    def body(*refs):
        ((*dp_refs, w_hbm, x_hbm, g_ref, dout_hbm), (gx_hbm, dg_ref),
         (w_ref, w_sems, xbuf, dbuf, gbuf, in_sems, out_sems), phases) = hosted.split(
             refs[:4 + ksplit] + refs[4 + ksplit + len(order_only):], 4 + ksplit, 2, 7)
        step = pl.program_id(0)
        _before_step(phases, step, steps)

        def rows_of(ref, c):
            return ref.at[pl.ds(pl.multiple_of(step * tm + c * CHUNK, CHUNK), CHUNK), :]

        def fetches(c):
            return (pltpu.make_async_copy(rows_of(x_hbm, c), xbuf.at[c], in_sems.at[c]),
                    pltpu.make_async_copy(rows_of(dout_hbm, c), dbuf.at[c], in_sems.at[nsub + c]))

        def put(c):
            return pltpu.make_async_copy(gbuf.at[c % 2], rows_of(gx_hbm, c), out_sems.at[c % 2])

        for c in range(nsub):
            for cp in fetches(c):
                cp.start()

        def w_load(j):
            k0, kw = kchunks[j]
            return pltpu.make_async_copy(w_hbm.at[k0:k0 + kw, :], w_ref.at[k0:k0 + kw, :], w_sems.at[j])

        @pl.when(step == 0)
        def _():
            dg_ref[...] = jnp.zeros_like(dg_ref)
            for j in range(ksplit):
                w_load(j).start()

        dh_all = None
        for j, ((k0, kw), dp_ref) in enumerate(zip(kchunks, dp_refs)):
            @pl.when(step == 0)
            def _():
                w_load(j).wait()

            part = _dot(dp_ref[...], w_ref[k0:k0 + kw, :], 1, 0)
            dh_all = part if dh_all is None else dh_all + part
        for c in range(nsub):
            for cp in fetches(c):
                cp.wait()
            if c >= 2:
                put(c - 2).wait()
            elif c < nsub:
                @pl.when(step > 0)
                def _():
                    put(max(nsub - 2, 0) + c).wait()
            dh = dh_all[c * CHUNK:(c + 1) * CHUNK, :]
            xv = xbuf[c]
            r = lax.rsqrt(jnp.mean(xv * xv, axis=-1, keepdims=True) + EPS)
            xn = xv * r
            dg_ref[...] += jnp.sum(dh * xn, axis=0, keepdims=True)
            dn = dh * g_ref[...]
            gbuf[c % 2] = dbuf[c] + r * (dn - xn * jnp.mean(dn * xn, axis=-1, keepdims=True))
            put(c).start()
        @pl.when(step == steps - 1)
        def _():
            for c in range(max(nsub - 2, 0), nsub):
                put(c).wait()

        _after_step(phases, step, steps)

    side_in, side_out = pltpu.VMEM((nsub, CHUNK, d), F32), pltpu.VMEM((2, CHUNK, d), F32)
    row = pl.BlockSpec((1, d), lambda i: (0, 0))
    return pl.pallas_call(
        body, name="in_proj_bwd", grid=(steps,),
        in_specs=[pl.BlockSpec((tm, kw), functools.partial(lambda j, i: (i, j), k0 // kw)) for k0, kw in kchunks]
        + [ANY, ANY, row, ANY] + [ANY] * len(order_only) + hosted.in_specs,
        out_specs=[ANY, row] + hosted.out_specs,
        out_shape=[jax.ShapeDtypeStruct((s, d), F32), jax.ShapeDtypeStruct((1, d), F32)] + hosted.out_shape,
        scratch_shapes=[pltpu.VMEM((D_IN, d), BF16), pltpu.SemaphoreType.DMA((ksplit,)), side_in, side_in, side_out,
                        pltpu.SemaphoreType.DMA((2 * nsub,)), pltpu.SemaphoreType.DMA((2,))] + hosted.scratch,
        compiler_params=_cparams(("arbitrary",)),
    )(*([dproj] * ksplit), wt, x, g_pre, dout, *order_only, *hosted.inputs)


def _lane_iota(shape):
    return lax.broadcasted_iota(jnp.int32, shape, len(shape) - 1)


def _partner(v):
    low = (_lane_iota(v.shape) % HEAD_DIM) < (HEAD_DIM // 2)
    return jnp.where(low, pltpu.roll(v, LANES - HEAD_DIM // 2, 1), pltpu.roll(v, HEAD_DIM // 2, 1))


def _rope(v, cos, sin_signed):
    return v * cos + _partner(v) * sin_signed


def _rope_transposed(dv, cos, sin_signed):
    return dv * cos - _partner(dv) * sin_signed


def _both_halves(v, kv_head):
    keep = (_lane_iota(v.shape) >= HEAD_DIM) if kv_head else (_lane_iota(v.shape) < HEAD_DIM)
    return jnp.where(keep, v, pltpu.roll(v, HEAD_DIM, 1))


def _fold_halves(acc):
    return acc + pltpu.roll(acc, HEAD_DIM, 1)


def _by_half(a, b):
    shape = jnp.broadcast_shapes(jnp.shape(a), jnp.shape(b))
    return jnp.where(_lane_iota(shape) < HEAD_DIM, a, b)


def _stack_heads(pair):
    return jnp.concatenate([_by_half(pair, 0.0), _by_half(0.0, pair)], axis=0)


def _band_bias(has_prev):
    i = lax.broadcasted_iota(jnp.int32, (2 * CHUNK, 2 * CHUNK), 0) % CHUNK
    j = lax.broadcasted_iota(jnp.int32, (2 * CHUNK, 2 * CHUNK), 1)
    band = jnp.logical_and(j > i, j <= i + CHUNK)
    return jnp.where(jnp.logical_and(band, jnp.logical_or(j >= CHUNK, has_prev)), 0.0, NEG)


def _probs_staged(qm2s, kk2s, bias, sink_cols):
    k = range(len(qm2s))
    scs = [_dot(qm2s[i], kk2s[i], 1, 1) + bias for i in k]
    mxs = [jnp.maximum(jnp.max(scs[i], axis=-1, keepdims=True), sink_cols[i]) for i in k]
    ps = [jnp.exp(scs[i] - mxs[i]) for i in k]
    ess = [jnp.exp(sink_cols[i] - mxs[i]) for i in k]
    invs = [1.0 / (jnp.sum(ps[i], axis=-1, keepdims=True) + ess[i]) for i in k]
    return [ps[i] * invs[i] for i in k], [ess[i] * invs[i] for i in k]


def _sink_col(sinks_ref, pair):
    row = lax.broadcasted_iota(jnp.int32, (2 * CHUNK, 1), 0)
    return jnp.where(row < CHUNK, sinks_ref[2 * pair], sinks_ref[2 * pair + 1])


def _layer_norm_parts(v):
    mu = jnp.mean(v, axis=-1, keepdims=True)
    xc = v - mu
    rstd = lax.rsqrt(jnp.mean(xc * xc, axis=-1, keepdims=True) + EPS)
    return xc * rstd, rstd


def _masked_spatial(w_ref, g):
    t = lax.broadcasted_iota(jnp.int32, (CHUNK, CHUNK), 0)
    sidx = lax.broadcasted_iota(jnp.int32, (CHUNK, CHUNK), 1)
    return jnp.where(t >= sidx, w_ref[g], 0.0).astype(BF16)


def _keys_values(kv_ref, kvp_ref, rope_ref, ropep_ref):
    cos_c, sin_c = rope_ref[:, :LANES], rope_ref[:, LANES:]
    cos_p, sin_p = ropep_ref[:, :LANES], ropep_ref[:, LANES:]
    k_c = _rope(kv_ref[:, :D_KV].astype(F32), cos_c, sin_c)
    k_p = _rope(kvp_ref[:, :D_KV].astype(F32), cos_p, sin_p)
    keys = jnp.concatenate([k_p, k_c], axis=0)
    vals = jnp.concatenate([kvp_ref[:, D_KV:], kv_ref[:, D_KV:]], axis=0).astype(F32)
    return keys, vals, (cos_c, sin_c, cos_p, sin_p)


def mixer_fwd(proj, rope, ln_g, ln_b, w_sp, b_sp_rows, sinks, comms=()):
    s = proj.shape[0]
    nb = s // CHUNK
    hosted = _Hosted(comms)

    def body(sinks_ref, *refs):
        ((proj_ref, kvp_ref, rope_ref, ropep_ref, lng_ref, lnb_ref, w_ref, b_ref), (cat_ref, p_ref), _,
         phases) = hosted.split(refs, 8, 2, 0)
        n = pl.program_id(0)
        _before_step(phases, n, nb)
        xhat, _ = _layer_norm_parts(proj_ref[:, OFF_V:OFF_V + D_GMLP].astype(F32))
        vnb = (xhat * lng_ref[...] + lnb_ref[...]).astype(BF16)
        mixeds = [_dot(_masked_spatial(w_ref, g), vnb[:, g * CHUNK:(g + 1) * CHUNK], 1, 0) + b_ref[g]
                  for g in range(GROUPS)]
        for g in range(GROUPS):
            za = proj_ref[:, OFF_ZA + g * CHUNK:OFF_ZA + (g + 1) * CHUNK].astype(F32)
            u = proj_ref[:, OFF_U + g * CHUNK:OFF_U + (g + 1) * CHUNK].astype(F32)
            cat_ref[:, g * CHUNK:(g + 1) * CHUNK] = (u * mixeds[g] * (za * _sigmoid(za))).astype(BF16)
        kv_ref = proj_ref.at[:, OFF_K:OFF_K + 2 * D_KV]
        keys, vals, (cos_c, sin_c, _, _) = _keys_values(kv_ref, kvp_ref, rope_ref, ropep_ref)
        cos_q, sin_q = cos_c * SCALE, sin_c * SCALE
        bias = _band_bias(n > 0)
        kk2 = [_both_halves(keys, kvh).astype(BF16) for kvh in range(N_KV_HEADS)]
        vv2 = [_both_halves(vals, kvh).astype(BF16) for kvh in range(N_KV_HEADS)]
        first_col = _lane_iota((2 * CHUNK, 2 * CHUNK)) == 0
        for kvh in range(N_KV_HEADS):
            pairs = range(kvh * PAIRS_PER_KV, (kvh + 1) * PAIRS_PER_KV)
            qms = [_stack_heads(_rope(proj_ref[:, OFF_Q + pair * LANES:OFF_Q + (pair + 1) * LANES].astype(F32),
                                      cos_q, sin_q)).astype(BF16) for pair in pairs]
            probs, sink_probs = _probs_staged(qms, [kk2[kvh]] * PAIRS_PER_KV, bias,
                                              [_sink_col(sinks_ref, pair) for pair in pairs])
            pbs = [p.astype(BF16) for p in probs]
            outs = [_dot(pb, vv2[kvh], 1, 0) for pb in pbs]
            for pair, pb, sink_prob in zip(pairs, pbs, sink_probs):
                p_ref[0, pair] = jnp.where(first_col, sink_prob.astype(BF16), pb)
            for pair, out in zip(pairs, outs):
                out_pair = _by_half(out[:CHUNK], out[CHUNK:])
                zb = proj_ref[:, OFF_ZB + pair * LANES:OFF_ZB + (pair + 1) * LANES].astype(F32)
                cat_ref[:, D_GMLP + pair * LANES:D_GMLP + (pair + 1) * LANES] = (
                    out_pair * (zb * _sigmoid(zb))).astype(BF16)
        _after_step(phases, n, nb)

    prev = lambda n, *_: (jnp.maximum(n - 1, 0), 0)
    kv_block = OFF_K // (2 * D_KV)
    return pl.pallas_call(
        body, name="mixer_fwd",
        grid_spec=pltpu.PrefetchScalarGridSpec(
            num_scalar_prefetch=1, grid=(nb,),
            in_specs=[pl.BlockSpec((CHUNK, D_IN), lambda n, *_: (n, 0)),
                      pl.BlockSpec((CHUNK, 2 * D_KV), lambda n, *_: (jnp.maximum(n - 1, 0), kv_block)),
                      pl.BlockSpec((CHUNK, 2 * LANES), lambda n, *_: (n, 0)),
                      pl.BlockSpec((CHUNK, 2 * LANES), prev),
                      pl.BlockSpec((1, D_GMLP), lambda n, *_: (0, 0)),
                      pl.BlockSpec((1, D_GMLP), lambda n, *_: (0, 0)),
                      pl.BlockSpec((GROUPS, CHUNK, CHUNK), lambda n, *_: (0, 0, 0)),
                      pl.BlockSpec((GROUPS, CHUNK, CHUNK), lambda n, *_: (0, 0, 0))] + hosted.in_specs,
            out_specs=[pl.BlockSpec((CHUNK, D_GMLP + D_ATTN), lambda n, *_: (n, 0)),
                       pl.BlockSpec((1, N_PAIRS, 2 * CHUNK, 2 * CHUNK), lambda n, *_: (n, 0, 0, 0))]
            + hosted.out_specs,
            scratch_shapes=hosted.scratch),
        out_shape=[jax.ShapeDtypeStruct((s, D_GMLP + D_ATTN), BF16),
                   jax.ShapeDtypeStruct((nb, N_PAIRS, 2 * CHUNK, 2 * CHUNK), BF16)] + hosted.out_shape,
        compiler_params=_cparams(("arbitrary",)),
    )(sinks, proj, proj, rope, rope, ln_g, ln_b, w_sp, b_sp_rows, *hosted.inputs)


def mixer_bwd(proj, dcat, probs, rope, ln_g, ln_b, w_sp, b_sp_rows, comms=()):
    s = proj.shape[0]
    nb = s // CHUNK
    hosted = _Hosted(comms)

    def body(*refs):
        ((proj_ref, kvp_ref, dcat_ref, p_ref, rope_ref, ropep_ref, lng_ref, lnb_ref, w_ref, b_ref),
         (dproj_ref, dw_ref, db_ref, dlng_ref, dlnb_ref, dsink_ref),
         (pend_ref, pend_kv_ref, dbacc_ref), phases) = hosted.split(refs, 10, 6, 3)
        n = pl.program_id(0)
        _before_step(phases, n, nb + 1)

        @pl.when(n == 0)
        def _():
            dw_ref[...] = jnp.zeros_like(dw_ref)
            dbacc_ref[...] = jnp.zeros_like(dbacc_ref)
            dlng_ref[...] = jnp.zeros_like(dlng_ref)
            dlnb_ref[...] = jnp.zeros_like(dlnb_ref)
            dsink_ref[...] = jnp.zeros_like(dsink_ref)

        @pl.when(n > 0)
        def _():
            dproj_ref[...] = pend_ref[...]

        def flush(dkv_prev):
            @pl.when(n > 0)
            def _():
                dproj_ref[:, OFF_K:OFF_K + 2 * D_KV] = (pend_kv_ref[...] + dkv_prev).astype(BF16)

        @pl.when(n < nb)
        def _():
            kv_ref = proj_ref.at[:, OFF_K:OFF_K + 2 * D_KV]
            keys, vals, (cos_c, sin_c, cos_p, sin_p) = _keys_values(kv_ref, kvp_ref, rope_ref, ropep_ref)
            cos_q, sin_q = cos_c * SCALE, sin_c * SCALE
            first_col = _lane_iota((2 * CHUNK, 2 * CHUNK)) == 0
            lane_row = _lane_iota((1, LANES))
            dsink = jnp.zeros((1, LANES), F32)
            dk_heads, dv_heads = [], []
            for kvh in range(N_KV_HEADS):
                kk2 = _both_halves(keys, kvh).astype(BF16)
                vv2 = _both_halves(vals, kvh).astype(BF16)
                pairs = list(range(kvh * PAIRS_PER_KV, (kvh + 1) * PAIRS_PER_KV))
                k4 = range(PAIRS_PER_KV)
                qm2s = [_stack_heads(_rope(proj_ref[:, OFF_Q + pair * LANES:OFF_Q + (pair + 1) * LANES].astype(F32),
                                           cos_q, sin_q)).astype(BF16) for pair in pairs]
                kept = [p_ref[0, pair] for pair in pairs]
                pbs = [jnp.where(first_col, jnp.zeros_like(kp), kp) for kp in kept]
                ps = [pb.astype(F32) for pb in pbs]
                p_sinks = [kp[:, 0:1].astype(F32) for kp in kept]
                o2s = [_dot(pb, vv2, 1, 0) for pb in pbs]
                zbs = [proj_ref[:, OFF_ZB + pair * LANES:OFF_ZB + (pair + 1) * LANES].astype(F32) for pair in pairs]
                sgs = [_sigmoid(zb) for zb in zbs]
                dybs = [dcat_ref[:, D_GMLP + pair * LANES:D_GMLP + (pair + 1) * LANES].astype(F32) for pair in pairs]
                for i, pair in enumerate(pairs):
                    out_pair = _by_half(o2s[i][:CHUNK], o2s[i][CHUNK:])
                    pend_ref[:, OFF_ZB + pair * LANES:OFF_ZB + (pair + 1) * LANES] = (
                        dybs[i] * out_pair * (sgs[i] * (1.0 + zbs[i] * (1.0 - sgs[i])))).astype(BF16)
                dom2s = [_stack_heads(dybs[i] * (zbs[i] * sgs[i])).astype(BF16) for i in k4]
                dps = [_dot(dom2, vv2, 1, 1) for dom2 in dom2s]
                deltas = [jnp.sum(ps[i] * dps[i], axis=-1, keepdims=True) for i in k4]
                dss = [ps[i] * (dps[i] - deltas[i]) for i in k4]
                for i, pair in enumerate(pairs):
                    dsk = -(p_sinks[i] * deltas[i])
                    dsink = dsink + jnp.where(lane_row == 2 * pair,
                                              jnp.sum(dsk[:CHUNK], axis=0, keepdims=True), 0.0)
                    dsink = dsink + jnp.where(lane_row == 2 * pair + 1,
                                              jnp.sum(dsk[CHUNK:], axis=0, keepdims=True), 0.0)
                dsbs = [ds.astype(BF16) for ds in dss]
                dq2s = [_dot(dsb, kk2, 1, 0) for dsb in dsbs]
                for pair, dq2 in zip(pairs, dq2s):
                    pend_ref[:, OFF_Q + pair * LANES:OFF_Q + (pair + 1) * LANES] = _rope_transposed(
                        _by_half(dq2[:CHUNK], dq2[CHUNK:]), cos_q, sin_q).astype(BF16)
                dkks = [_dot(dsbs[i], qm2s[i], 0, 0) for i in k4]
                dvvs = [_dot(pbs[i], dom2s[i], 0, 0) for i in k4]
                dk_heads.append(_fold_halves((dkks[0] + dkks[1]) + (dkks[2] + dkks[3])))
                dv_heads.append(_fold_halves((dvvs[0] + dvvs[1]) + (dvvs[2] + dvvs[3])))
            dk_rot = _by_half(dk_heads[0], dk_heads[1])
            dv_all = _by_half(dv_heads[0], dv_heads[1])
            dk_p = _rope_transposed(dk_rot[:CHUNK], cos_p, sin_p)
            dk_c = _rope_transposed(dk_rot[CHUNK:], cos_c, sin_c)
            flush(jnp.concatenate([dk_p, dv_all[:CHUNK]], axis=1))
            dsink_ref[...] += dsink
            pend_kv_ref[...] = jnp.concatenate([dk_c, dv_all[CHUNK:]], axis=1)
            xhat, rstd = _layer_norm_parts(proj_ref[:, OFF_V:OFF_V + D_GMLP].astype(F32))
            lng = lng_ref[...]
            vnb = (xhat * lng + lnb_ref[...]).astype(BF16)
            dvn_cols = []
            for g in range(GROUPS):
                cols = slice(g * CHUNK, (g + 1) * CHUNK)
                wm = _masked_spatial(w_ref, g)
                mixed = _dot(wm, vnb[:, cols], 1, 0) + b_ref[g]
                za = proj_ref[:, OFF_ZA + g * CHUNK:OFF_ZA + (g + 1) * CHUNK].astype(F32)
                u = proj_ref[:, OFF_U + g * CHUNK:OFF_U + (g + 1) * CHUNK].astype(F32)
                dya = dcat_ref[:, cols].astype(F32)
                sg = _sigmoid(za)
                sz = za * sg
                pend_ref[:, OFF_U + g * CHUNK:OFF_U + (g + 1) * CHUNK] = (dya * mixed * sz).astype(BF16)
                pend_ref[:, OFF_ZA + g * CHUNK:OFF_ZA + (g + 1) * CHUNK] = (
                    dya * u * mixed * (sg * (1.0 + za * (1.0 - sg)))).astype(BF16)
                dmixed = dya * u * sz
                dmb = dmixed.astype(BF16)
                dbacc_ref[g] += dmixed
                dw_ref[g] += _dot(dmb, vnb[:, cols], 1, 1)
                dvn_cols.append(_dot(wm, dmb, 0, 0))
            dvn = jnp.concatenate(dvn_cols, axis=1)
            dlng_ref[...] += jnp.sum(dvn * xhat, axis=0, keepdims=True)
            dlnb_ref[...] += jnp.sum(dvn, axis=0, keepdims=True)
            dxh = dvn * lng
            dv = rstd * (dxh - jnp.mean(dxh, axis=-1, keepdims=True)
                         - xhat * jnp.mean(dxh * xhat, axis=-1, keepdims=True))
            pend_ref[:, OFF_V:OFF_V + D_GMLP] = dv.astype(BF16)

        @pl.when(n == nb)
        def _():
            flush(jnp.zeros((CHUNK, 2 * D_KV), F32))
            t = lax.broadcasted_iota(jnp.int32, (CHUNK, CHUNK), 0)
            sidx = lax.broadcasted_iota(jnp.int32, (CHUNK, CHUNK), 1)
            lane = _lane_iota((CHUNK, LANES))
            dbt = jnp.zeros((CHUNK, LANES), F32)
            for g in range(GROUPS):
                dw_ref[g] = jnp.where(t >= sidx, dw_ref[g], 0.0)
                dbt = jnp.where(lane == g, jnp.sum(dbacc_ref[g], axis=-1, keepdims=True), dbt)
            db_ref[...] = jnp.transpose(dbt)[:GROUPS, :]

        _after_step(phases, n, nb + 1)

    cur = lambda n: (jnp.minimum(n, nb - 1), 0)
    prev = lambda n: (jnp.clip(n - 1, 0, nb - 1), 0)
    kv_block = OFF_K // (2 * D_KV)
    const2 = lambda n: (0, 0)
    const3 = lambda n: (0, 0, 0)
    return pl.pallas_call(
        body, name="mixer_bwd", grid=(nb + 1,),
        in_specs=[pl.BlockSpec((CHUNK, D_IN), cur),
                  pl.BlockSpec((CHUNK, 2 * D_KV), lambda n: (jnp.clip(n - 1, 0, nb - 1), kv_block)),
                  pl.BlockSpec((CHUNK, D_GMLP + D_ATTN), cur),
                  pl.BlockSpec((1, N_PAIRS, 2 * CHUNK, 2 * CHUNK), lambda n: (jnp.minimum(n, nb - 1), 0, 0, 0)),
                  pl.BlockSpec((CHUNK, 2 * LANES), cur),
                  pl.BlockSpec((CHUNK, 2 * LANES), prev),
                  pl.BlockSpec((1, D_GMLP), const2),
                  pl.BlockSpec((1, D_GMLP), const2),
                  pl.BlockSpec((GROUPS, CHUNK, CHUNK), const3),
                  pl.BlockSpec((GROUPS, CHUNK, CHUNK), const3)] + hosted.in_specs,
        out_specs=[pl.BlockSpec((CHUNK, D_IN), lambda n: (jnp.maximum(n - 1, 0), 0)),
                   pl.BlockSpec((GROUPS, CHUNK, CHUNK), const3),
                   pl.BlockSpec((GROUPS, CHUNK), const2),
                   pl.BlockSpec((1, D_GMLP), const2),
                   pl.BlockSpec((1, D_GMLP), const2),
                   pl.BlockSpec((1, LANES), const2)] + hosted.out_specs,
        scratch_shapes=[pltpu.VMEM((CHUNK, D_IN), BF16), pltpu.VMEM((CHUNK, 2 * D_KV), F32),
                        pltpu.VMEM((GROUPS, CHUNK, CHUNK), F32)] + hosted.scratch,
        out_shape=[jax.ShapeDtypeStruct((s, D_IN), BF16),
                   jax.ShapeDtypeStruct((GROUPS, CHUNK, CHUNK), F32),
                   jax.ShapeDtypeStruct((GROUPS, CHUNK), F32),
                   jax.ShapeDtypeStruct((1, D_GMLP), F32),
                   jax.ShapeDtypeStruct((1, D_GMLP), F32),
                   jax.ShapeDtypeStruct((1, LANES), F32)] + hosted.out_shape,
        compiler_params=_cparams(("arbitrary",)),
    )(proj, proj, dcat, probs, rope, rope, ln_g, ln_b, w_sp, b_sp_rows, *hosted.inputs)


def _adamw_math(w, g, m, v):
    m = ADAM_B1 * m + (1.0 - ADAM_B1) * g
    v = ADAM_B2 * v + (1.0 - ADAM_B2) * (g * g)
    m_hat = m / (1.0 - ADAM_B1 ** ADAM_STEP)
    v_hat = v / (1.0 - ADAM_B2 ** ADAM_STEP)
    delta = -ADAM_LR * (m_hat / (jnp.sqrt(v_hat) + ADAM_EPS) + ADAM_WD * w)
    return delta, m, v


def adamw_shard(terms, w, m, v, name, after=None):
    r, c = w.shape
    tr = _tile(r, (224, 128, 8))
    n_terms = len(terms)
    order_only = [] if after is None else [after]

    def body(*refs):
        w_ref, m_ref, v_ref, g_ref, d_ref, nm_ref, nv_ref = refs[n_terms:n_terms + 3] + refs[-4:]
        g = None
        for ref, (_, slots) in zip(refs[:n_terms], terms):
            for k in range(slots):
                part = ref[k].astype(F32)
                g = part if g is None else g + part
        g_ref[...] = g
        d_ref[...], nm_ref[...], nv_ref[...] = _adamw_math(w_ref[...], g, m_ref[...], v_ref[...])

    spec = pl.BlockSpec((tr, c), lambda i: (i, 0))
    return pl.pallas_call(
        body, name=name, grid=(r // tr,),
        in_specs=[pl.BlockSpec((slots, tr, c), lambda i: (0, i, 0)) for _, slots in terms] + [spec] * 3
        + [ANY] * len(order_only),
        out_specs=[spec] * 4, out_shape=[jax.ShapeDtypeStruct((r, c), F32)] * 4,
        compiler_params=_cparams(("arbitrary",)),
    )(*[a for a, _ in terms], w, m, v, *order_only)


def adamw_small(gathered, lane_windows, params):
    n_par = len(params)

    def body(*refs):
        g_refs = refs[:n_par + 1]
        wmv_refs = refs[n_par + 1:4 * n_par + 1]
        out_refs = refs[4 * n_par + 1:]

        def total(ref):
            acc = ref[0]
            for dev in range(1, N_DEV):
                acc = acc + ref[dev]
            return acc

        for i in range(n_par):
            w_ref, m_ref, v_ref = wmv_refs[3 * i:3 * i + 3]
            g = total(g_refs[i])
            if lane_windows[i] is not None:
                start, size = lane_windows[i]
                g = g[..., start:start + size]
            delta, new_m, new_v = _adamw_math(w_ref[...], g, m_ref[...], v_ref[...])
            for ref, val in zip(out_refs[4 * i:4 * i + 4], (g, delta, new_m, new_v)):
                ref[...] = val
        out_refs[4 * n_par][...] = total(g_refs[n_par])

    flat = [a for wmv in params for a in wmv]
    out_shape = [jax.ShapeDtypeStruct(w.shape, F32) for (w, _, _) in params for _ in range(4)]
    out_shape.append(jax.ShapeDtypeStruct(gathered[-1].shape[1:], F32))
    outs = pl.pallas_call(body, name="adamw_small", out_shape=out_shape, compiler_params=_cparams())(*gathered, *flat)
    return [tuple(outs[4 * i:4 * i + 4]) for i in range(n_par)], outs[-1]


def kernel(x, positions, g_pre, w_in, b_qkv, ln_v_g, ln_v_b, w_spatial, b_spatial, attn_sinks, w_out, g_post, loss_target, m_g_pre, m_w_in, m_b_qkv, m_ln_v_g, m_ln_v_b, m_w_spatial, m_b_spatial, m_attn_sinks, m_w_out, m_g_post, v_g_pre, v_w_in, v_b_qkv, v_ln_v_g, v_ln_v_b, v_w_spatial, v_b_spatial, v_attn_sinks, v_w_out, v_g_post):
    x2, target = x[0], loss_target[0]
    seq = x2.shape[0]

    wt_shard = w_in[0].T.astype(BF16)
    wo_shard = w_out[0].astype(BF16)
    pos_col = positions.reshape(seq, 1)
    half = HEAD_DIM // 2
    inv_freq = ROPE_THETA ** (-jnp.arange(half, dtype=F32) * (2.0 / HEAD_DIM))
    freq = jnp.tile(inv_freq, LANES // half).reshape(1, LANES)
    sign = jnp.tile(jnp.concatenate([-jnp.ones((half,), F32), jnp.ones((half,), F32)]), LANES // HEAD_DIM)
    sign = sign.reshape(1, LANES)
    bias = jnp.concatenate([jnp.zeros((1, OFF_Q), F32), b_qkv, jnp.zeros((1, D_ATTN), F32)], axis=1)
    proj, h, rope, wt = in_proj_gather(x2, pos_col, freq, sign, g_pre, wt_shard, bias)

    b_rows = jnp.broadcast_to(b_spatial[0][:, :, None], (GROUPS, CHUNK, CHUNK))
    sinks = attn_sinks[0]
    cat, probs, wo = mixer_fwd(proj, rope, ln_v_g, ln_v_b, w_spatial[0], b_rows, sinks,
                               comms=[gather_comm([wo_shard])])
    dy, dout, d_g_post, loss_part = out_proj_loss(cat, wo, x2, target, g_post)

    dcat = matmul_nt(dy, wo, "out_proj_bwd")
    d_wo, _ = matmul_tn(cat, dy, 512, "w_out_grad")
    dproj, d_w_sp, d_b_sp, d_ln_g, d_ln_b, d_sinks, parts_wo = mixer_bwd(
        proj, dcat, probs, rope, ln_v_g, ln_v_b, w_spatial[0], b_rows, comms=[scatter_comm([d_wo])])
    small_parts = [d_ln_g, d_ln_b, d_w_sp, d_b_sp, d_sinks, d_g_post, loss_part]
    d_wt, colsum, *landed = matmul_tn(dproj, h, 768, "w_in_grad", comms=[gather_comm(small_parts, stack=True)])

    sum_wt = pair_exchange_sum(d_wt, "grad_pair_sum_w_in")
    started = chips_exchange_start(sum_wt)
    grad_x, d_g_pre = in_proj_bwd(dproj, wt, x2, g_pre, dout, after=started[-1])
    sum_wt, far_wt = chips_exchange_wait(started, d_g_pre)
    late = jnp.concatenate([d_g_pre, colsum], axis=1)
    late_started, late_copy = gather_start(late)

    wo_out = adamw_shard([(parts_wo, N_DEV)], w_out[0], m_w_out[0], v_w_out[0], "adamw_w_out", after=late_started[-1])
    wt_out = adamw_shard([(sum_wt, 1), (far_wt, 3)], w_in[0].T, m_w_in[0].T, v_w_in[0].T, "adamw_w_in",
                         after=wo_out[0])
    landed_late = gather_wait(late_started, late_copy, wt_out[0])
    my_index = _index(*_my_place())
    all_late = lax.dynamic_update_slice(landed_late, late[None], (my_index, 0, 0))
    gathered = [all_late, all_late] + landed
    windows = [(0, D_MODEL), (D_MODEL + OFF_Q, D_QKV), None, None, None, None, (0, N_Q_HEADS), None]
    small = [(g_pre, m_g_pre, v_g_pre), (b_qkv, m_b_qkv, v_b_qkv), (ln_v_g, m_ln_v_g, v_ln_v_g),
             (ln_v_b, m_ln_v_b, v_ln_v_b), (w_spatial[0], m_w_spatial[0], v_w_spatial[0]),
             (b_spatial[0], m_b_spatial[0], v_b_spatial[0]), (attn_sinks, m_attn_sinks, v_attn_sinks),
             (g_post, m_g_post, v_g_post)]
    small_out, loss_row = adamw_small(gathered, windows, small)
    lead = [False, False, False, False, True, True, False, False]
    small_out = [tuple(a[None] if ld else a for a in leaf) for leaf, ld in zip(small_out, lead)]

    def leaves(k):
        gp, bq, lg, lb, ws, bs, sk, gpo = (leaf[k] for leaf in small_out)
        return [gp, wt_out[k].T[None], bq, lg, lb, ws, bs, sk, wo_out[k][None], gpo]

    return (loss_row[0, 0], grad_x[None], *leaves(0), *leaves(1), *leaves(2), *leaves(3))
```

```python
import functools

import jax
import jax.numpy as jnp
from jax import lax
from jax.experimental import pallas as pl
from jax.experimental.pallas import tpu as pltpu

F32 = jnp.float32
BF16 = jnp.bfloat16

D_MODEL = 2048
D_GMLP = 1024
D_ATTN = 1024
CHUNK = 128
GROUPS = 8
HEAD_DIM = 64
N_Q_HEADS = 16
N_KV_HEADS = 2
D_KV = N_KV_HEADS * HEAD_DIM
D_IN = 3 * D_GMLP + D_ATTN + 2 * D_KV + D_ATTN
OFF_U, OFF_V, OFF_ZA = 0, D_GMLP, 2 * D_GMLP
OFF_Q = 3 * D_GMLP
OFF_K = OFF_Q + D_ATTN
OFF_VA = OFF_K + D_KV
OFF_ZB = OFF_VA + D_KV
D_QKV = D_ATTN + 2 * D_KV
ROPE_THETA = 10000.0
EPS = 1e-6
SCALE = HEAD_DIM ** -0.5
NEG = -1e30
N_PAIRS = N_Q_HEADS // 2
PAIRS_PER_KV = N_PAIRS // N_KV_HEADS

ADAM_LR = 0.001
ADAM_B1 = 0.9
ADAM_B2 = 0.999
ADAM_EPS = 1e-08
ADAM_WD = 0.01
ADAM_STEP = 10

N_DEV = 8
LANES = 128
VMEM_LIMIT = 56 * 1024 * 1024
IN_PROJ_VMEM_LIMIT = 61 * 1024 * 1024

MESH = pl.DeviceIdType.MESH
ANY = pl.BlockSpec(memory_space=pl.ANY)


def _cparams(sem=None):
    return pltpu.CompilerParams(dimension_semantics=sem, vmem_limit_bytes=VMEM_LIMIT)


def _tile(n, prefs):
    for t in prefs:
        if n % t == 0:
            return t
    return n


def _sigmoid(z):
    return 1.0 / (1.0 + jnp.exp(-z))


def _dot(a, b, ca, cb):
    return lax.dot_general(a, b, (((ca,), (cb,)), ((), ())), preferred_element_type=F32)


def _my_place():
    return lax.axis_index("x"), lax.axis_index("y"), lax.axis_index("c")


def _chip_of(x, y, r):
    return (x ^ (r & 1), y ^ (r >> 1))


def _peer(x, y, c, k):
    return (x ^ (k >> 2), y ^ ((k >> 1) & 1), c ^ (k & 1))


def _index(px, py, pc):
    return 4 * px + 2 * py + pc


class _Comm:
    def __init__(self, inputs, out_shape, scratch, bind):
        self.inputs, self.out_shape, self.scratch, self.bind = list(inputs), list(out_shape), list(scratch), bind


def gather_comm(shards, stack=False):
    n_arr = len(shards)

    def bind(ins, outs, sems):
        send_sems, recv_sems, local_sems = sems
        x, y, c = _my_place()
        me, sibling = (x, y, c), (x, y, 1 - c)
        chips = [_chip_of(x, y, r) for r in (1, 2, 3)]

        def rows(a, px, py, pc):
            d = _index(px, py, pc)
            if stack:
                return outs[a].at[d]
            m = shards[a].shape[0]
            return outs[a].at[pl.ds(pl.multiple_of(d * m, 8), m), :]

        def copy(a, k, block, to, src=None):
            return pltpu.make_async_remote_copy(
                src_ref=rows(a, *block) if src is None else src, dst_ref=rows(a, *block),
                send_sem=send_sems.at[a * 7 + k], recv_sem=recv_sems.at[a * 7 + k],
                device_id=to, device_id_type=MESH)

        def mine(a):
            return pltpu.make_async_copy(ins[a], rows(a, *me), local_sems.at[a])

        def own_sends(a):
            return ([copy(a, 0, me, sibling, src=ins[a])]
                    + [copy(a, 1 + j, me, (*chip, c), src=ins[a]) for j, chip in enumerate(chips)])

        def start():
            for a in range(n_arr):
                mine(a).start()
                for cp in own_sends(a):
                    cp.start()

        def relay():
            for j, chip in enumerate(chips):
                for a in range(n_arr):
                    copy(a, 1 + j, (*chip, c), me).wait_recv()
                    copy(a, 4 + j, (*chip, c), sibling).start()

        def finish():
            for a in range(n_arr):
                copy(a, 0, sibling, me).wait_recv()
                for j, chip in enumerate(chips):
                    copy(a, 4 + j, (*chip, 1 - c), me).wait_recv()
                    copy(a, 4 + j, (*chip, c), sibling).wait_send()
                for cp in own_sends(a):
                    cp.wait_send()
                mine(a).wait()

        return start, relay, finish

    def gathered(s):
        return (N_DEV, *s.shape) if stack else (N_DEV * s.shape[0], s.shape[1])

    return _Comm(shards, [jax.ShapeDtypeStruct(gathered(s), s.dtype) for s in shards],
                 [pltpu.SemaphoreType.DMA((7 * n_arr,)), pltpu.SemaphoreType.DMA((7 * n_arr,)),
                  pltpu.SemaphoreType.DMA((n_arr,))], bind)


def scatter_comm(parts):
    n_arr = len(parts)

    def bind(ins, outs, sems):
        send_sems, recv_sems, local_sems = sems
        x, y, c = _my_place()
        my_index = _index(x, y, c)

        def block(a, d):
            m = parts[a].shape[0] // N_DEV
            return ins[a].at[pl.ds(pl.multiple_of(d * m, 16), m), :]

        def copy(a, k, slot):
            peer = _peer(x, y, c, k)
            return pltpu.make_async_remote_copy(
                src_ref=block(a, _index(*peer)), dst_ref=outs[a].at[slot],
                send_sem=send_sems.at[a * 7 + k - 1], recv_sem=recv_sems.at[a * 7 + k - 1],
                device_id=peer, device_id_type=MESH)

        def mine(a):
            return pltpu.make_async_copy(block(a, my_index), outs[a].at[my_index], local_sems.at[a])

        def start():
            for a in range(n_arr):
                mine(a).start()
                for k in range(1, 8):
                    copy(a, k, my_index).start()

        def finish():
            for a in range(n_arr):
                for k in range(1, 8):
                    copy(a, k, _index(*_peer(x, y, c, k))).wait_recv()
                    copy(a, k, my_index).wait_send()
                mine(a).wait()

        return start, (lambda: None), finish

    return _Comm(parts, [jax.ShapeDtypeStruct((N_DEV, p.shape[0] // N_DEV, p.shape[1]), p.dtype) for p in parts],
                 [pltpu.SemaphoreType.DMA((7 * n_arr,)), pltpu.SemaphoreType.DMA((7 * n_arr,)),
                  pltpu.SemaphoreType.DMA((n_arr,))], bind)


def run_comm(comm, name):
    n_in, n_out = len(comm.inputs), len(comm.out_shape)

    def body(*refs):
        start, relay, finish = comm.bind(refs[:n_in], refs[n_in:n_in + n_out], refs[n_in + n_out:])
        start()
        relay()
        finish()

    outs = pl.pallas_call(body, name=name, out_shape=comm.out_shape, in_specs=[ANY] * n_in,
                          out_specs=[ANY] * n_out, scratch_shapes=comm.scratch)(*comm.inputs)
    return list(outs)


def _chip_copy(r, src_ref, land_ref, send_sem, recv_sem):
    x, y, c = _my_place()
    return pltpu.make_async_remote_copy(src_ref=src_ref.at[r], dst_ref=land_ref.at[r - 1], send_sem=send_sem,
                                        recv_sem=recv_sem, device_id=(*_chip_of(x, y, r), c), device_id_type=MESH)


def chips_exchange_start(sums):
    def body(src_ref, land_ref, s1, s2, s3, r1, r2, r3, src_thru, land_thru, token):
        del src_thru, land_thru
        for r, send_sem, recv_sem in ((1, s1, r1), (2, s2, r2), (3, s3, r3)):
            _chip_copy(r, src_ref, land_ref, send_sem, recv_sem).start()
        token[...] = jnp.zeros_like(token)

    land = lax.empty((3,) + sums.shape[1:], sums.dtype)
    sem = pltpu.SemaphoreType.DMA(())
    hbm = pl.BlockSpec(memory_space=pltpu.HBM)
    sem_spec = pl.BlockSpec(memory_space=pltpu.SEMAPHORE)
    return pl.pallas_call(
        body, name="grad_exchange_chips_start",
        out_shape=(sem,) * 6 + (pltpu.HBM(sums.shape, sums.dtype), pltpu.HBM(land.shape, land.dtype),
                                jax.ShapeDtypeStruct((8, LANES), F32)),
        in_specs=(hbm, hbm), out_specs=(sem_spec,) * 6 + (hbm, hbm, pl.BlockSpec(memory_space=pltpu.VMEM)),
        input_output_aliases={0: 6, 1: 7},
        compiler_params=pltpu.CompilerParams(has_side_effects=pltpu.SideEffectType.DATAFLOW_SIDE_EFFECTING),
    )(pltpu.with_memory_space_constraint(sums, pltpu.HBM), pltpu.with_memory_space_constraint(land, pltpu.HBM))


def chips_exchange_wait(started, after):
    s1, s2, s3, r1, r2, r3, src_thru, land_thru, _ = started

    def body(src_ref, land_ref, s1, s2, s3, r1, r2, r3, after_ref, src_out, land_out):
        del after_ref, src_out, land_out
        for r, send_sem, recv_sem in ((1, s1, r1), (2, s2, r2), (3, s3, r3)):
            copy = _chip_copy(r, src_ref, land_ref, send_sem, recv_sem)
            copy.wait_send()
            copy.wait_recv()

    hbm = pl.BlockSpec(memory_space=pltpu.HBM)
    sem_spec = pl.BlockSpec(memory_space=pltpu.SEMAPHORE)
    return pl.pallas_call(
        body, name="grad_exchange_chips_wait",
        out_shape=(pltpu.HBM(src_thru.shape, src_thru.dtype), pltpu.HBM(land_thru.shape, land_thru.dtype)),
        in_specs=(hbm, hbm) + (sem_spec,) * 6 + (pl.BlockSpec(memory_space=pl.ANY),), out_specs=(hbm, hbm),
        input_output_aliases={0: 0, 1: 1},
        compiler_params=pltpu.CompilerParams(has_side_effects=pltpu.SideEffectType.DATAFLOW_SIDE_EFFECTING),
    )(src_thru, land_thru, s1, s2, s3, r1, r2, r3, after)


def gather_start(shard):
    def copy(k, src_ref, land_ref, send_sem, recv_sem):
        x, y, c = _my_place()
        return pltpu.make_async_remote_copy(src_ref=src_ref, dst_ref=land_ref.at[_index(x, y, c)], send_sem=send_sem,
                                            recv_sem=recv_sem, device_id=_peer(x, y, c, k), device_id_type=MESH)

    def start_body(src_ref, land_ref, *rest):
        sems, token = rest[:14], rest[16]
        for k in range(1, 8):
            copy(k, src_ref, land_ref, sems[k - 1], sems[7 + k - 1]).start()
        token[...] = jnp.zeros_like(token)

    land = lax.empty((N_DEV,) + shard.shape, shard.dtype)
    sem = pltpu.SemaphoreType.DMA(())
    hbm = pl.BlockSpec(memory_space=pltpu.HBM)
    sem_spec = pl.BlockSpec(memory_space=pltpu.SEMAPHORE)
    started = pl.pallas_call(
        start_body, name="allgather_late_grads_start",
        out_shape=(sem,) * 14 + (pltpu.HBM(shard.shape, shard.dtype), pltpu.HBM(land.shape, land.dtype),
                                 jax.ShapeDtypeStruct((8, LANES), F32)),
        in_specs=(hbm, hbm), out_specs=(sem_spec,) * 14 + (hbm, hbm, pl.BlockSpec(memory_space=pltpu.VMEM)),
        input_output_aliases={0: 14, 1: 15},
        compiler_params=pltpu.CompilerParams(has_side_effects=pltpu.SideEffectType.DATAFLOW_SIDE_EFFECTING),
    )(pltpu.with_memory_space_constraint(shard, pltpu.HBM), pltpu.with_memory_space_constraint(land, pltpu.HBM))
    return started, copy


def gather_wait(started, copy, after):
    sems, src_thru, land_thru = started[:14], started[14], started[15]

    def wait_body(src_ref, land_ref, *rest):
        for k in range(1, 8):
            cp = copy(k, src_ref, land_ref, rest[k - 1], rest[7 + k - 1])
            cp.wait_send()
            cp.wait_recv()

    hbm = pl.BlockSpec(memory_space=pltpu.HBM)
    sem_spec = pl.BlockSpec(memory_space=pltpu.SEMAPHORE)
    return pl.pallas_call(
        wait_body, name="allgather_late_grads_wait",
        out_shape=(pltpu.HBM(src_thru.shape, src_thru.dtype), pltpu.HBM(land_thru.shape, land_thru.dtype)),
        in_specs=(hbm, hbm) + (sem_spec,) * 14 + (pl.BlockSpec(memory_space=pl.ANY),), out_specs=(hbm, hbm),
        input_output_aliases={0: 0, 1: 1},
        compiler_params=pltpu.CompilerParams(has_side_effects=pltpu.SideEffectType.DATAFLOW_SIDE_EFFECTING),
    )(src_thru, land_thru, *sems, after)[1]


class _Hosted:
    def __init__(self, comms):
        self.comms = list(comms)
        self.inputs = [a for cm in self.comms for a in cm.inputs]
        self.out_shape = [s for cm in self.comms for s in cm.out_shape]
        self.scratch = [s for cm in self.comms for s in cm.scratch]
        self.in_specs = [ANY] * len(self.inputs)
        self.out_specs = [ANY] * len(self.out_shape)

    def split(self, refs, n_in, n_out, n_scratch):
        ni, no = len(self.inputs), len(self.out_shape)
        ins, rest = refs[:n_in], refs[n_in:]
        c_ins, rest = rest[:ni], rest[ni:]
        outs, rest = rest[:n_out], rest[n_out:]
        c_outs, rest = rest[:no], rest[no:]
        scratch, c_sems = rest[:n_scratch], rest[n_scratch:]
        phases = []
        for cm in self.comms:
            a, b, s = len(cm.inputs), len(cm.out_shape), len(cm.scratch)
            phases.append(cm.bind(c_ins[:a], c_outs[:b], c_sems[:s]))
            c_ins, c_outs, c_sems = c_ins[a:], c_outs[b:], c_sems[s:]
        return ins, outs, scratch, phases


def _before_step(phases, step, n_steps):
    if not phases:
        return

    @pl.when(step == 0)
    def _():
        for start, _, _ in phases:
            start()

    @pl.when(step == n_steps // 2)
    def _():
        for _, relay, _ in phases:
            relay()


def _after_step(phases, step, n_steps):
    if not phases:
        return

    @pl.when(step == n_steps - 1)
    def _():
        for _, _, finish in phases:
            finish()


def pair_exchange_sum(part, name):
    m, n = part.shape[0] // N_DEV, part.shape[1]

    def body(part_ref, out_ref, got, mine, summed, send_sems, recv_sems, in_sems, out_sems):
        x, y, c = _my_place()

        def rows(r, core):
            owner = _index(*_chip_of(x, y, r), core)
            return part_ref.at[pl.ds(pl.multiple_of(owner * m, 16), m), :]

        def to_sibling(r):
            return pltpu.make_async_remote_copy(src_ref=rows(r, 1 - c), dst_ref=got.at[r], send_sem=send_sems.at[r],
                                                recv_sem=recv_sems.at[r], device_id=(x, y, 1 - c), device_id_type=MESH)

        def fetch(r):
            return pltpu.make_async_copy(rows(r, c), mine.at[r % 2], in_sems.at[r % 2])

        def put(r):
            return pltpu.make_async_copy(summed.at[r % 2], out_ref.at[r], out_sems.at[r % 2])

        for r in range(4):
            to_sibling(r).start()
        fetch(0).start()
        for r in range(4):
            if r + 1 < 4:
                fetch(r + 1).start()
            fetch(r).wait()
            to_sibling(r).wait_recv()
            if r >= 2:
                put(r - 2).wait()
            summed[r % 2] = (mine[r % 2].astype(F32) + got[r].astype(F32)).astype(summed.dtype)
            put(r).start()
        for r in (2, 3):
            put(r).wait()
        for r in range(4):
            to_sibling(r).wait_send()

    return pl.pallas_call(
        body, name=name, out_shape=jax.ShapeDtypeStruct((4, m, n), part.dtype),
        in_specs=[ANY], out_specs=ANY,
        scratch_shapes=[pltpu.VMEM((4, m, n), part.dtype), pltpu.VMEM((2, m, n), part.dtype),
                        pltpu.VMEM((2, m, n), part.dtype), pltpu.SemaphoreType.DMA((4,)),
                        pltpu.SemaphoreType.DMA((4,)), pltpu.SemaphoreType.DMA((2,)), pltpu.SemaphoreType.DMA((2,))],
        compiler_params=_cparams(),
    )(part)


def in_proj_gather(x, pos_col, freq, sign, g_pre, wt_shard, bias):
    s, d = x.shape
    tm = _tile(s, (512, 256, 128))
    nt = s // tm
    tc = _tile(s, (256, 128))
    nc = s // tc
    m = wt_shard.shape[0]
    half = D_IN // 2
    xi = lax.axis_index("x")
    order = jnp.stack([xi, 1 - xi]).astype(jnp.int32)

    def body(order_ref, x_hbm, pos_hbm, freq_ref, sign_ref, g_ref, b_ref, shard_ref,
             proj_ref, h_hbm, rope_hbm, wt_ref,
             w_vmem, h_vmem, xbuf, posbuf, ropebuf, send_sems, recv_sems, local_sems, in_sems, out_sems):
        del order_ref
        p, i = pl.program_id(0), pl.program_id(1)
        xx, yy, cc = _my_place()
        me, sibling = (xx, yy, cc), (xx, yy, 1 - cc)
        chips = [_chip_of(xx, yy, r) for r in (1, 2, 3)]

        def rows(px, py, pc):
            return wt_ref.at[pl.ds(pl.multiple_of(_index(px, py, pc) * m, 16), m), :]

        def copy(k, block, to, src=None):
            return pltpu.make_async_remote_copy(
                src_ref=rows(*block) if src is None else src, dst_ref=rows(*block),
                send_sem=send_sems.at[k], recv_sem=recv_sems.at[k], device_id=to, device_id_type=MESH)

        def mine():
            return pltpu.make_async_copy(shard_ref, rows(*me), local_sems.at[0])

        def to_sibling():
            return copy(0, me, sibling, src=shard_ref)

        def to_chip(j):
            return copy(1 + j, me, (*chips[j], cc), src=shard_ref)

        def relay(j):
            copy(1 + j, (*chips[j], cc), me).wait_recv()
            copy(4 + j, (*chips[j], cc), sibling).start()

        def relayed(j):
            copy(4 + j, (*chips[j], 1 - cc), me).wait_recv()

        def load_half(which, slot):
            rows_of_half = wt_ref.at[pl.ds(pl.multiple_of(which * half, 16), half), :]
            load = pltpu.make_async_copy(rows_of_half, w_vmem.at[slot], local_sems.at[1 + slot])
            load.start()
            load.wait()

        def piece(ref, c):
            return ref.at[pl.ds(c * tc, tc), :]

        def fetch(c):
            return (pltpu.make_async_copy(piece(x_hbm, c), xbuf.at[c % 2], in_sems.at[c % 2]),
                    pltpu.make_async_copy(piece(pos_hbm, c), posbuf.at[c % 2], in_sems.at[2 + c % 2]))

        def put(c):
            return (pltpu.make_async_copy(piece(h_vmem, c), piece(h_hbm, c), out_sems.at[c % 2]),
                    pltpu.make_async_copy(ropebuf.at[c % 2], piece(rope_hbm, c), out_sems.at[2 + c % 2]))

        def prologue():
            for cp in fetch(0):
                cp.start()
            for c in range(nc):
                if c + 1 < nc:
                    for cp in fetch(c + 1):
                        cp.start()
                for cp in fetch(c):
                    cp.wait()
                if c >= 2:
                    for cp in put(c - 2):
                        cp.wait()
                xv = xbuf[c % 2]
                r = lax.rsqrt(jnp.mean(xv * xv, axis=-1, keepdims=True) + EPS)
                h_vmem[c * tc:(c + 1) * tc, :] = (xv * r * g_ref[...]).astype(BF16)
                ang = posbuf[c % 2].astype(F32) * freq_ref[...]
                ropebuf[c % 2, :, :LANES] = jnp.cos(ang)
                ropebuf[c % 2, :, LANES:] = jnp.sin(ang) * sign_ref[...]
                for cp in put(c):
                    cp.start()
            for c in range(max(nc - 2, 0), nc):
                for cp in put(c):
                    cp.wait()

        @pl.when(jnp.logical_and(p == 0, i == 0))
        def _():
            mine().start()
            to_sibling().start()
            to_chip(1).start()
            to_chip(0).start()
            prologue()
            copy(0, sibling, me).wait_recv()
            relay(1)
            relayed(1)
            mine().wait()
            to_chip(1).wait_send()
            to_chip(0).wait_send()
            to_chip(2).start()
            load_half(xx, 0)

        @pl.when(jnp.logical_and(p == 1, i == 0))
        def _():
            relayed(0)
            relayed(2)
            load_half(1 - xx, 1)

        def project(slot):
            hb = h_vmem[pl.ds(pl.multiple_of(i * tm, tm), tm), :]
            proj_ref[...] = (_dot(hb, w_vmem[slot], 1, 1) + b_ref[...]).astype(BF16)

        @pl.when(p == 0)
        def _():
            project(0)

        @pl.when(p == 1)
        def _():
            project(1)

        @pl.when(jnp.logical_and(p == 0, i == 1))
        def _():
            relay(0)

        @pl.when(jnp.logical_and(p == 0, i == nt - 1))
        def _():
            relay(2)

        @pl.when(jnp.logical_and(p == 1, i == nt - 1))
        def _():
            to_sibling().wait_send()
            to_chip(2).wait_send()
            for j in range(3):
                copy(4 + j, (*chips[j], cc), sibling).wait_send()

    const = lambda p, i, o: (0, 0)
    return pl.pallas_call(
        body, name="in_proj_gather",
        grid_spec=pltpu.PrefetchScalarGridSpec(
            num_scalar_prefetch=1, grid=(2, nt),
            in_specs=[ANY, ANY,
                      pl.BlockSpec((1, LANES), const),
                      pl.BlockSpec((1, LANES), const),
                      pl.BlockSpec((1, d), const),
                      pl.BlockSpec((1, half), lambda p, i, o: (0, o[p])),
                      ANY],
            out_specs=[pl.BlockSpec((tm, half), lambda p, i, o: (i, o[p])), ANY, ANY, ANY],
            scratch_shapes=[pltpu.VMEM((2, half, d), BF16), pltpu.VMEM((s, d), BF16),
                            pltpu.VMEM((2, tc, d), F32), pltpu.VMEM((2, tc, 1), jnp.int32),
                            pltpu.VMEM((2, tc, 2 * LANES), F32),
                            pltpu.SemaphoreType.DMA((7,)), pltpu.SemaphoreType.DMA((7,)),
                            pltpu.SemaphoreType.DMA((3,)), pltpu.SemaphoreType.DMA((4,)),
                            pltpu.SemaphoreType.DMA((4,))]),
        out_shape=[jax.ShapeDtypeStruct((s, D_IN), BF16), jax.ShapeDtypeStruct((s, d), BF16),
                   jax.ShapeDtypeStruct((s, 2 * LANES), F32), jax.ShapeDtypeStruct((D_IN, d), BF16)],
        compiler_params=pltpu.CompilerParams(dimension_semantics=("arbitrary", "arbitrary"),
                                             vmem_limit_bytes=IN_PROJ_VMEM_LIMIT),
    )(order, x, pos_col, freq, sign, g_pre, bias, wt_shard)


def out_proj_loss(cat, w_out, x, target, g_post):
    s, d = x.shape
    tm = _tile(s, (256, 128))
    kc = _tile(d, (512, 128))
    pieces = w_out.shape[0] // kc

    def body(cat_ref, w_hbm, x_ref, t_ref, g_ref, dy_ref, dout_ref, dg_ref, loss_ref, w_ref, w_sems):
        step = pl.program_id(0)

        def w_load(j):
            return pltpu.make_async_copy(w_hbm.at[j * kc:(j + 1) * kc, :], w_ref.at[j * kc:(j + 1) * kc, :], w_sems.at[j])

        @pl.when(step == 0)
        def _():
            for j in range(pieces):
                w_load(j).start()
            dg_ref[...] = jnp.zeros_like(dg_ref)
            loss_ref[...] = jnp.zeros_like(loss_ref)
            ys = [None] * (tm // CHUNK)
            for j in range(pieces):
                w_load(j).wait()
                for c in range(tm // CHUNK):
                    part = _dot(cat_ref[c * CHUNK:(c + 1) * CHUNK, j * kc:(j + 1) * kc], w_ref[j * kc:(j + 1) * kc, :], 1, 0)
                    ys[c] = part if ys[c] is None else ys[c] + part
            loss_and_back(ys, x_ref, t_ref, g_ref, dy_ref, dout_ref, dg_ref, loss_ref)

        @pl.when(step > 0)
        def _():
            ys = [_dot(cat_ref[c0:c0 + CHUNK, :], w_ref[...], 1, 0) for c0 in range(0, tm, CHUNK)]
            loss_and_back(ys, x_ref, t_ref, g_ref, dy_ref, dout_ref, dg_ref, loss_ref)

    def loss_and_back(ys, x_ref, t_ref, g_ref, dy_ref, dout_ref, dg_ref, loss_ref):
        g = g_ref[...]
        for c0 in range(0, tm, CHUNK):
            rows = slice(c0, c0 + CHUNK)
            yv = ys[c0 // CHUNK]
            r = lax.rsqrt(jnp.mean(yv * yv, axis=-1, keepdims=True) + EPS)
            nrm = yv * r
            err = x_ref[rows, :] + nrm * g - t_ref[rows, :]
            loss_ref[...] += 0.5 * jnp.sum(jnp.sum(err * err, axis=-1, keepdims=True), axis=0, keepdims=True) / d
            dout = err * (1.0 / d)
            dout_ref[rows, :] = dout
            dg_ref[...] += jnp.sum(dout * nrm, axis=0, keepdims=True)
            dn = dout * g
            dy = r * (dn - nrm * jnp.mean(dn * nrm, axis=-1, keepdims=True))
            dy_ref[rows, :] = dy.astype(BF16)

    return pl.pallas_call(
        body, name="out_proj_loss", grid=(s // tm,),
        in_specs=[pl.BlockSpec((tm, d), lambda i: (i, 0)),
                  ANY,
                  pl.BlockSpec((tm, d), lambda i: (i, 0)),
                  pl.BlockSpec((tm, d), lambda i: (i, 0)),
                  pl.BlockSpec((1, d), lambda i: (0, 0))],
        out_specs=[pl.BlockSpec((tm, d), lambda i: (i, 0)),
                   pl.BlockSpec((tm, d), lambda i: (i, 0)),
                   pl.BlockSpec((1, d), lambda i: (0, 0)),
                   pl.BlockSpec((1, LANES), lambda i: (0, 0))],
        out_shape=[jax.ShapeDtypeStruct((s, d), BF16), jax.ShapeDtypeStruct((s, d), F32),
                   jax.ShapeDtypeStruct((1, d), F32), jax.ShapeDtypeStruct((1, LANES), F32)],
        scratch_shapes=[pltpu.VMEM(w_out.shape, w_out.dtype), pltpu.SemaphoreType.DMA((pieces,))],
        compiler_params=_cparams(("arbitrary",)),
    )(cat, w_out, x, target, g_post)


def matmul_nt(a, b, name):
    m, k = a.shape
    n = b.shape[0]
    tm = _tile(m, (512, 256, 128))

    def body(a_ref, b_ref, o_ref):
        o_ref[...] = _dot(a_ref[...], b_ref[...], 1, 1).astype(o_ref.dtype)

    return pl.pallas_call(
        body, name=name, grid=(m // tm,),
        in_specs=[pl.BlockSpec((tm, k), lambda i: (i, 0)), pl.BlockSpec((n, k), lambda i: (0, 0))],
        out_specs=pl.BlockSpec((tm, n), lambda i: (i, 0)),
        out_shape=jax.ShapeDtypeStruct((m, n), BF16),
        compiler_params=_cparams(("arbitrary",)),
    )(a, b)


def matmul_tn(a, b, tm, name, comms=()):
    k, m = a.shape
    n = b.shape[1]
    steps = m // tm
    hosted = _Hosted(comms)

    kc = _tile(k, (1024, 128))
    pieces = k // kc

    def body(*refs):
        (a_ref, b_hbm), (o_ref, cs_ref), (b_ref, b_sems), phases = hosted.split(refs, 2, 2, 2)
        step = pl.program_id(0)
        _before_step(phases, step, steps)

        def b_load(j):
            return pltpu.make_async_copy(b_hbm.at[j * kc:(j + 1) * kc, :], b_ref.at[j * kc:(j + 1) * kc, :], b_sems.at[j])

        @pl.when(step == 0)
        def _():
            for j in range(pieces):
                b_load(j).start()
            acc = None
            for j in range(pieces):
                b_load(j).wait()
                part = _dot(a_ref[j * kc:(j + 1) * kc, :], b_ref[j * kc:(j + 1) * kc, :], 0, 0)
                acc = part if acc is None else acc + part
            o_ref[...] = acc.astype(o_ref.dtype)

        @pl.when(step > 0)
        def _():
            o_ref[...] = _dot(a_ref[...], b_ref[...], 0, 0).astype(o_ref.dtype)

        rows = _tile(k, (512, 128))
        cs = jnp.zeros((1, tm), F32)
        for r0 in range(0, k, rows):
            cs = cs + jnp.sum(a_ref[r0:r0 + rows, :].astype(F32), axis=0, keepdims=True)
        cs_ref[...] = cs
        _after_step(phases, step, steps)

    return pl.pallas_call(
        body, name=name, grid=(steps,),
        in_specs=[pl.BlockSpec((k, tm), lambda i: (0, i)), ANY] + hosted.in_specs,
        out_specs=[pl.BlockSpec((tm, n), lambda i: (i, 0)), pl.BlockSpec((1, tm), lambda i: (0, i))] + hosted.out_specs,
        out_shape=[jax.ShapeDtypeStruct((m, n), BF16), jax.ShapeDtypeStruct((1, m), F32)] + hosted.out_shape,
        scratch_shapes=[pltpu.VMEM((k, n), b.dtype), pltpu.SemaphoreType.DMA((pieces,))] + hosted.scratch,
        compiler_params=_cparams(("arbitrary",)),
    )(a, b, *hosted.inputs)


def in_proj_bwd(dproj, wt, x, g_pre, dout, comms=(), after=None):
    s, d = x.shape
    tm = _tile(s, (512, 256, 128))
    steps = s // tm
    nsub = tm // CHUNK
    kw = 8 * LANES
    kchunks = [(k0, kw) for k0 in range(0, D_IN - D_IN % kw, kw)]
    if D_IN % kw:
        kchunks.append((D_IN - D_IN % kw, D_IN % kw))
    ksplit = len(kchunks)
    hosted = _Hosted(comms)
    order_only = [] if after is None else [after]

    def body(*refs):
        ((*dp_refs, w_hbm, x_hbm, g_ref, dout_hbm), (gx_hbm, dg_ref),
         (w_ref, w_sems, xbuf, dbuf, gbuf, in_sems, out_sems), phases) = hosted.split(
             refs[:4 + ksplit] + refs[4 + ksplit + len(order_only):], 4 + ksplit, 2, 7)
        step = pl.program_id(0)
        _before_step(phases, step, steps)

        def rows_of(ref, c):
            return ref.at[pl.ds(pl.multiple_of(step * tm + c * CHUNK, CHUNK), CHUNK), :]

        def fetches(c):
            return (pltpu.make_async_copy(rows_of(x_hbm, c), xbuf.at[c], in_sems.at[c]),
                    pltpu.make_async_copy(rows_of(dout_hbm, c), dbuf.at[c], in_sems.at[nsub + c]))

        def put(c):
            return pltpu.make_async_copy(gbuf.at[c % 2], rows_of(gx_hbm, c), out_sems.at[c % 2])

        for c in range(nsub):
            for cp in fetches(c):
                cp.start()

        def w_load(j):
            k0, kw = kchunks[j]
            return pltpu.make_async_copy(w_hbm.at[k0:k0 + kw, :], w_ref.at[k0:k0 + kw, :], w_sems.at[j])

        @pl.when(step == 0)
        def _():
            dg_ref[...] = jnp.zeros_like(dg_ref)
            for j in range(ksplit):
                w_load(j).start()

        dh_all = None
        for j, ((k0, kw), dp_ref) in enumerate(zip(kchunks, dp_refs)):
            @pl.when(step == 0)
            def _():
                w_load(j).wait()

            part = _dot(dp_ref[...], w_ref[k0:k0 + kw, :], 1, 0)
            dh_all = part if dh_all is None else dh_all + part
        for c in range(nsub):
            for cp in fetches(c):
                cp.wait()
            if c >= 2:
                put(c - 2).wait()
            elif c < nsub:
                @pl.when(step > 0)
                def _():
                    put(max(nsub - 2, 0) + c).wait()
            dh = dh_all[c * CHUNK:(c + 1) * CHUNK, :]
            xv = xbuf[c]
            r = lax.rsqrt(jnp.mean(xv * xv, axis=-1, keepdims=True) + EPS)
            xn = xv * r
            dg_ref[...] += jnp.sum(dh * xn, axis=0, keepdims=True)
            dn = dh * g_ref[...]
            gbuf[c % 2] = dbuf[c] + r * (dn - xn * jnp.mean(dn * xn, axis=-1, keepdims=True))
            put(c).start()
        @pl.when(step == steps - 1)
        def _():
            for c in range(max(nsub - 2, 0), nsub):
                put(c).wait()

        _after_step(phases, step, steps)

    side_in, side_out = pltpu.VMEM((nsub, CHUNK, d), F32), pltpu.VMEM((2, CHUNK, d), F32)
    row = pl.BlockSpec((1, d), lambda i: (0, 0))
    return pl.pallas_call(
        body, name="in_proj_bwd", grid=(steps,),
        in_specs=[pl.BlockSpec((tm, kw), functools.partial(lambda j, i: (i, j), k0 // kw)) for k0, kw in kchunks]
        + [ANY, ANY, row, ANY] + [ANY] * len(order_only) + hosted.in_specs,
        out_specs=[ANY, row] + hosted.out_specs,
        out_shape=[jax.ShapeDtypeStruct((s, d), F32), jax.ShapeDtypeStruct((1, d), F32)] + hosted.out_shape,
        scratch_shapes=[pltpu.VMEM((D_IN, d), BF16), pltpu.SemaphoreType.DMA((ksplit,)), side_in, side_in, side_out,
                        pltpu.SemaphoreType.DMA((2 * nsub,)), pltpu.SemaphoreType.DMA((2,))] + hosted.scratch,
        compiler_params=_cparams(("arbitrary",)),
    )(*([dproj] * ksplit), wt, x, g_pre, dout, *order_only, *hosted.inputs)


def _lane_iota(shape):
    return lax.broadcasted_iota(jnp.int32, shape, len(shape) - 1)


def _partner(v):
    low = (_lane_iota(v.shape) % HEAD_DIM) < (HEAD_DIM // 2)
    return jnp.where(low, pltpu.roll(v, LANES - HEAD_DIM // 2, 1), pltpu.roll(v, HEAD_DIM // 2, 1))


def _rope(v, cos, sin_signed):
    return v * cos + _partner(v) * sin_signed


def _rope_transposed(dv, cos, sin_signed):
    return dv * cos - _partner(dv) * sin_signed


def _both_halves(v, kv_head):
    keep = (_lane_iota(v.shape) >= HEAD_DIM) if kv_head else (_lane_iota(v.shape) < HEAD_DIM)
    return jnp.where(keep, v, pltpu.roll(v, HEAD_DIM, 1))


def _fold_halves(acc):
    return acc + pltpu.roll(acc, HEAD_DIM, 1)


def _by_half(a, b):
    shape = jnp.broadcast_shapes(jnp.shape(a), jnp.shape(b))
    return jnp.where(_lane_iota(shape) < HEAD_DIM, a, b)


def _stack_heads(pair):
    return jnp.concatenate([_by_half(pair, 0.0), _by_half(0.0, pair)], axis=0)


def _band_bias(has_prev):
    i = lax.broadcasted_iota(jnp.int32, (2 * CHUNK, 2 * CHUNK), 0) % CHUNK
    j = lax.broadcasted_iota(jnp.int32, (2 * CHUNK, 2 * CHUNK), 1)
    band = jnp.logical_and(j > i, j <= i + CHUNK)
    return jnp.where(jnp.logical_and(band, jnp.logical_or(j >= CHUNK, has_prev)), 0.0, NEG)


def _probs_staged(qm2s, kk2s, bias, sink_cols):
    k = range(len(qm2s))
    scs = [_dot(qm2s[i], kk2s[i], 1, 1) + bias for i in k]
    mxs = [jnp.maximum(jnp.max(scs[i], axis=-1, keepdims=True), sink_cols[i]) for i in k]
    ps = [jnp.exp(scs[i] - mxs[i]) for i in k]
    ess = [jnp.exp(sink_cols[i] - mxs[i]) for i in k]
    invs = [1.0 / (jnp.sum(ps[i], axis=-1, keepdims=True) + ess[i]) for i in k]
    return [ps[i] * invs[i] for i in k], [ess[i] * invs[i] for i in k]


def _sink_col(sinks_ref, pair):
    row = lax.broadcasted_iota(jnp.int32, (2 * CHUNK, 1), 0)
    return jnp.where(row < CHUNK, sinks_ref[2 * pair], sinks_ref[2 * pair + 1])


def _layer_norm_parts(v):
    mu = jnp.mean(v, axis=-1, keepdims=True)
    xc = v - mu
    rstd = lax.rsqrt(jnp.mean(xc * xc, axis=-1, keepdims=True) + EPS)
    return xc * rstd, rstd


def _masked_spatial(w_ref, g):
    t = lax.broadcasted_iota(jnp.int32, (CHUNK, CHUNK), 0)
    sidx = lax.broadcasted_iota(jnp.int32, (CHUNK, CHUNK), 1)
    return jnp.where(t >= sidx, w_ref[g], 0.0).astype(BF16)


def _keys_values(kv_ref, kvp_ref, rope_ref, ropep_ref):
    cos_c, sin_c = rope_ref[:, :LANES], rope_ref[:, LANES:]
    cos_p, sin_p = ropep_ref[:, :LANES], ropep_ref[:, LANES:]
    k_c = _rope(kv_ref[:, :D_KV].astype(F32), cos_c, sin_c)
    k_p = _rope(kvp_ref[:, :D_KV].astype(F32), cos_p, sin_p)
    keys = jnp.concatenate([k_p, k_c], axis=0)
    vals = jnp.concatenate([kvp_ref[:, D_KV:], kv_ref[:, D_KV:]], axis=0).astype(F32)
    return keys, vals, (cos_c, sin_c, cos_p, sin_p)


def mixer_fwd(proj, rope, ln_g, ln_b, w_sp, b_sp_rows, sinks, comms=()):
    s = proj.shape[0]
    nb = s // CHUNK
    hosted = _Hosted(comms)

    def body(sinks_ref, *refs):
        ((proj_ref, kvp_ref, rope_ref, ropep_ref, lng_ref, lnb_ref, w_ref, b_ref), (cat_ref, p_ref), _,
         phases) = hosted.split(refs, 8, 2, 0)
        n = pl.program_id(0)
        _before_step(phases, n, nb)
        xhat, _ = _layer_norm_parts(proj_ref[:, OFF_V:OFF_V + D_GMLP].astype(F32))
        vnb = (xhat * lng_ref[...] + lnb_ref[...]).astype(BF16)
        mixeds = [_dot(_masked_spatial(w_ref, g), vnb[:, g * CHUNK:(g + 1) * CHUNK], 1, 0) + b_ref[g]
                  for g in range(GROUPS)]
        for g in range(GROUPS):
            za = proj_ref[:, OFF_ZA + g * CHUNK:OFF_ZA + (g + 1) * CHUNK].astype(F32)
            u = proj_ref[:, OFF_U + g * CHUNK:OFF_U + (g + 1) * CHUNK].astype(F32)
            cat_ref[:, g * CHUNK:(g + 1) * CHUNK] = (u * mixeds[g] * (za * _sigmoid(za))).astype(BF16)
        kv_ref = proj_ref.at[:, OFF_K:OFF_K + 2 * D_KV]
        keys, vals, (cos_c, sin_c, _, _) = _keys_values(kv_ref, kvp_ref, rope_ref, ropep_ref)
        cos_q, sin_q = cos_c * SCALE, sin_c * SCALE
        bias = _band_bias(n > 0)
        kk2 = [_both_halves(keys, kvh).astype(BF16) for kvh in range(N_KV_HEADS)]
        vv2 = [_both_halves(vals, kvh).astype(BF16) for kvh in range(N_KV_HEADS)]
        first_col = _lane_iota((2 * CHUNK, 2 * CHUNK)) == 0
        for kvh in range(N_KV_HEADS):
            pairs = range(kvh * PAIRS_PER_KV, (kvh + 1) * PAIRS_PER_KV)
            qms = [_stack_heads(_rope(proj_ref[:, OFF_Q + pair * LANES:OFF_Q + (pair + 1) * LANES].astype(F32),
                                      cos_q, sin_q)).astype(BF16) for pair in pairs]
            probs, sink_probs = _probs_staged(qms, [kk2[kvh]] * PAIRS_PER_KV, bias,
                                              [_sink_col(sinks_ref, pair) for pair in pairs])
            pbs = [p.astype(BF16) for p in probs]
            outs = [_dot(pb, vv2[kvh], 1, 0) for pb in pbs]
            for pair, pb, sink_prob in zip(pairs, pbs, sink_probs):
                p_ref[0, pair] = jnp.where(first_col, sink_prob.astype(BF16), pb)
            for pair, out in zip(pairs, outs):
                out_pair = _by_half(out[:CHUNK], out[CHUNK:])
                zb = proj_ref[:, OFF_ZB + pair * LANES:OFF_ZB + (pair + 1) * LANES].astype(F32)
                cat_ref[:, D_GMLP + pair * LANES:D_GMLP + (pair + 1) * LANES] = (
                    out_pair * (zb * _sigmoid(zb))).astype(BF16)
        _after_step(phases, n, nb)

    prev = lambda n, *_: (jnp.maximum(n - 1, 0), 0)
    kv_block = OFF_K // (2 * D_KV)
    return pl.pallas_call(
        body, name="mixer_fwd",
        grid_spec=pltpu.PrefetchScalarGridSpec(
            num_scalar_prefetch=1, grid=(nb,),
            in_specs=[pl.BlockSpec((CHUNK, D_IN), lambda n, *_: (n, 0)),
                      pl.BlockSpec((CHUNK, 2 * D_KV), lambda n, *_: (jnp.maximum(n - 1, 0), kv_block)),
                      pl.BlockSpec((CHUNK, 2 * LANES), lambda n, *_: (n, 0)),
                      pl.BlockSpec((CHUNK, 2 * LANES), prev),
                      pl.BlockSpec((1, D_GMLP), lambda n, *_: (0, 0)),
                      pl.BlockSpec((1, D_GMLP), lambda n, *_: (0, 0)),
                      pl.BlockSpec((GROUPS, CHUNK, CHUNK), lambda n, *_: (0, 0, 0)),
                      pl.BlockSpec((GROUPS, CHUNK, CHUNK), lambda n, *_: (0, 0, 0))] + hosted.in_specs,
            out_specs=[pl.BlockSpec((CHUNK, D_GMLP + D_ATTN), lambda n, *_: (n, 0)),
                       pl.BlockSpec((1, N_PAIRS, 2 * CHUNK, 2 * CHUNK), lambda n, *_: (n, 0, 0, 0))]
            + hosted.out_specs,
            scratch_shapes=hosted.scratch),
        out_shape=[jax.ShapeDtypeStruct((s, D_GMLP + D_ATTN), BF16),
                   jax.ShapeDtypeStruct((nb, N_PAIRS, 2 * CHUNK, 2 * CHUNK), BF16)] + hosted.out_shape,
        compiler_params=_cparams(("arbitrary",)),
    )(sinks, proj, proj, rope, rope, ln_g, ln_b, w_sp, b_sp_rows, *hosted.inputs)


def mixer_bwd(proj, dcat, probs, rope, ln_g, ln_b, w_sp, b_sp_rows, comms=()):
    s = proj.shape[0]
    nb = s // CHUNK
    hosted = _Hosted(comms)

    def body(*refs):
        ((proj_ref, kvp_ref, dcat_ref, p_ref, rope_ref, ropep_ref, lng_ref, lnb_ref, w_ref, b_ref),
         (dproj_ref, dw_ref, db_ref, dlng_ref, dlnb_ref, dsink_ref),
         (pend_ref, pend_kv_ref, dbacc_ref), phases) = hosted.split(refs, 10, 6, 3)
        n = pl.program_id(0)
        _before_step(phases, n, nb + 1)

        @pl.when(n == 0)
        def _():
            dw_ref[...] = jnp.zeros_like(dw_ref)
            dbacc_ref[...] = jnp.zeros_like(dbacc_ref)
            dlng_ref[...] = jnp.zeros_like(dlng_ref)
            dlnb_ref[...] = jnp.zeros_like(dlnb_ref)
            dsink_ref[...] = jnp.zeros_like(dsink_ref)

        @pl.when(n > 0)
        def _():
            dproj_ref[...] = pend_ref[...]

        def flush(dkv_prev):
            @pl.when(n > 0)
            def _():
                dproj_ref[:, OFF_K:OFF_K + 2 * D_KV] = (pend_kv_ref[...] + dkv_prev).astype(BF16)

        @pl.when(n < nb)
        def _():
            kv_ref = proj_ref.at[:, OFF_K:OFF_K + 2 * D_KV]
            keys, vals, (cos_c, sin_c, cos_p, sin_p) = _keys_values(kv_ref, kvp_ref, rope_ref, ropep_ref)
            cos_q, sin_q = cos_c * SCALE, sin_c * SCALE
            first_col = _lane_iota((2 * CHUNK, 2 * CHUNK)) == 0
            lane_row = _lane_iota((1, LANES))
            dsink = jnp.zeros((1, LANES), F32)
            dk_heads, dv_heads = [], []
            for kvh in range(N_KV_HEADS):
                kk2 = _both_halves(keys, kvh).astype(BF16)
                vv2 = _both_halves(vals, kvh).astype(BF16)
                pairs = list(range(kvh * PAIRS_PER_KV, (kvh + 1) * PAIRS_PER_KV))
                k4 = range(PAIRS_PER_KV)
                qm2s = [_stack_heads(_rope(proj_ref[:, OFF_Q + pair * LANES:OFF_Q + (pair + 1) * LANES].astype(F32),
                                           cos_q, sin_q)).astype(BF16) for pair in pairs]
                kept = [p_ref[0, pair] for pair in pairs]
                pbs = [jnp.where(first_col, jnp.zeros_like(kp), kp) for kp in kept]
                ps = [pb.astype(F32) for pb in pbs]
                p_sinks = [kp[:, 0:1].astype(F32) for kp in kept]
                o2s = [_dot(pb, vv2, 1, 0) for pb in pbs]
                zbs = [proj_ref[:, OFF_ZB + pair * LANES:OFF_ZB + (pair + 1) * LANES].astype(F32) for pair in pairs]
                sgs = [_sigmoid(zb) for zb in zbs]
                dybs = [dcat_ref[:, D_GMLP + pair * LANES:D_GMLP + (pair + 1) * LANES].astype(F32) for pair in pairs]
                for i, pair in enumerate(pairs):
                    out_pair = _by_half(o2s[i][:CHUNK], o2s[i][CHUNK:])
                    pend_ref[:, OFF_ZB + pair * LANES:OFF_ZB + (pair + 1) * LANES] = (
                        dybs[i] * out_pair * (sgs[i] * (1.0 + zbs[i] * (1.0 - sgs[i])))).astype(BF16)
                dom2s = [_stack_heads(dybs[i] * (zbs[i] * sgs[i])).astype(BF16) for i in k4]
                dps = [_dot(dom2, vv2, 1, 1) for dom2 in dom2s]
                deltas = [jnp.sum(ps[i] * dps[i], axis=-1, keepdims=True) for i in k4]
                dss = [ps[i] * (dps[i] - deltas[i]) for i in k4]
                for i, pair in enumerate(pairs):
                    dsk = -(p_sinks[i] * deltas[i])
                    dsink = dsink + jnp.where(lane_row == 2 * pair,
                                              jnp.sum(dsk[:CHUNK], axis=0, keepdims=True), 0.0)
                    dsink = dsink + jnp.where(lane_row == 2 * pair + 1,
                                              jnp.sum(dsk[CHUNK:], axis=0, keepdims=True), 0.0)
                dsbs = [ds.astype(BF16) for ds in dss]
                dq2s = [_dot(dsb, kk2, 1, 0) for dsb in dsbs]
                for pair, dq2 in zip(pairs, dq2s):
                    pend_ref[:, OFF_Q + pair * LANES:OFF_Q + (pair + 1) * LANES] = _rope_transposed(
                        _by_half(dq2[:CHUNK], dq2[CHUNK:]), cos_q, sin_q).astype(BF16)
                dkks = [_dot(dsbs[i], qm2s[i], 0, 0) for i in k4]
                dvvs = [_dot(pbs[i], dom2s[i], 0, 0) for i in k4]
                dk_heads.append(_fold_halves((dkks[0] + dkks[1]) + (dkks[2] + dkks[3])))
                dv_heads.append(_fold_halves((dvvs[0] + dvvs[1]) + (dvvs[2] + dvvs[3])))
            dk_rot = _by_half(dk_heads[0], dk_heads[1])
            dv_all = _by_half(dv_heads[0], dv_heads[1])
            dk_p = _rope_transposed(dk_rot[:CHUNK], cos_p, sin_p)
            dk_c = _rope_transposed(dk_rot[CHUNK:], cos_c, sin_c)
            flush(jnp.concatenate([dk_p, dv_all[:CHUNK]], axis=1))
            dsink_ref[...] += dsink
            pend_kv_ref[...] = jnp.concatenate([dk_c, dv_all[CHUNK:]], axis=1)
            xhat, rstd = _layer_norm_parts(proj_ref[:, OFF_V:OFF_V + D_GMLP].astype(F32))
            lng = lng_ref[...]
            vnb = (xhat * lng + lnb_ref[...]).astype(BF16)
            dvn_cols = []
            for g in range(GROUPS):
                cols = slice(g * CHUNK, (g + 1) * CHUNK)
                wm = _masked_spatial(w_ref, g)
                mixed = _dot(wm, vnb[:, cols], 1, 0) + b_ref[g]
                za = proj_ref[:, OFF_ZA + g * CHUNK:OFF_ZA + (g + 1) * CHUNK].astype(F32)
                u = proj_ref[:, OFF_U + g * CHUNK:OFF_U + (g + 1) * CHUNK].astype(F32)
                dya = dcat_ref[:, cols].astype(F32)
                sg = _sigmoid(za)
                sz = za * sg
                pend_ref[:, OFF_U + g * CHUNK:OFF_U + (g + 1) * CHUNK] = (dya * mixed * sz).astype(BF16)
                pend_ref[:, OFF_ZA + g * CHUNK:OFF_ZA + (g + 1) * CHUNK] = (
                    dya * u * mixed * (sg * (1.0 + za * (1.0 - sg)))).astype(BF16)
                dmixed = dya * u * sz
                dmb = dmixed.astype(BF16)
                dbacc_ref[g] += dmixed
                dw_ref[g] += _dot(dmb, vnb[:, cols], 1, 1)
                dvn_cols.append(_dot(wm, dmb, 0, 0))
            dvn = jnp.concatenate(dvn_cols, axis=1)
            dlng_ref[...] += jnp.sum(dvn * xhat, axis=0, keepdims=True)
            dlnb_ref[...] += jnp.sum(dvn, axis=0, keepdims=True)
            dxh = dvn * lng
            dv = rstd * (dxh - jnp.mean(dxh, axis=-1, keepdims=True)
                         - xhat * jnp.mean(dxh * xhat, axis=-1, keepdims=True))
            pend_ref[:, OFF_V:OFF_V + D_GMLP] = dv.astype(BF16)

        @pl.when(n == nb)
        def _():
            flush(jnp.zeros((CHUNK, 2 * D_KV), F32))
            t = lax.broadcasted_iota(jnp.int32, (CHUNK, CHUNK), 0)
            sidx = lax.broadcasted_iota(jnp.int32, (CHUNK, CHUNK), 1)
            lane = _lane_iota((CHUNK, LANES))
            dbt = jnp.zeros((CHUNK, LANES), F32)
            for g in range(GROUPS):
                dw_ref[g] = jnp.where(t >= sidx, dw_ref[g], 0.0)
                dbt = jnp.where(lane == g, jnp.sum(dbacc_ref[g], axis=-1, keepdims=True), dbt)
            db_ref[...] = jnp.transpose(dbt)[:GROUPS, :]

        _after_step(phases, n, nb + 1)

    cur = lambda n: (jnp.minimum(n, nb - 1), 0)
    prev = lambda n: (jnp.clip(n - 1, 0, nb - 1), 0)
    kv_block = OFF_K // (2 * D_KV)
    const2 = lambda n: (0, 0)
    const3 = lambda n: (0, 0, 0)
    return pl.pallas_call(
        body, name="mixer_bwd", grid=(nb + 1,),
        in_specs=[pl.BlockSpec((CHUNK, D_IN), cur),
                  pl.BlockSpec((CHUNK, 2 * D_KV), lambda n: (jnp.clip(n - 1, 0, nb - 1), kv_block)),
                  pl.BlockSpec((CHUNK, D_GMLP + D_ATTN), cur),
                  pl.BlockSpec((1, N_PAIRS, 2 * CHUNK, 2 * CHUNK), lambda n: (jnp.minimum(n, nb - 1), 0, 0, 0)),
                  pl.BlockSpec((CHUNK, 2 * LANES), cur),
                  pl.BlockSpec((CHUNK, 2 * LANES), prev),
                  pl.BlockSpec((1, D_GMLP), const2),
                  pl.BlockSpec((1, D_GMLP), const2),
                  pl.BlockSpec((GROUPS, CHUNK, CHUNK), const3),
                  pl.BlockSpec((GROUPS, CHUNK, CHUNK), const3)] + hosted.in_specs,
        out_specs=[pl.BlockSpec((CHUNK, D_IN), lambda n: (jnp.maximum(n - 1, 0), 0)),
                   pl.BlockSpec((GROUPS, CHUNK, CHUNK), const3),
                   pl.BlockSpec((GROUPS, CHUNK), const2),
                   pl.BlockSpec((1, D_GMLP), const2),
                   pl.BlockSpec((1, D_GMLP), const2),
                   pl.BlockSpec((1, LANES), const2)] + hosted.out_specs,
        scratch_shapes=[pltpu.VMEM((CHUNK, D_IN), BF16), pltpu.VMEM((CHUNK, 2 * D_KV), F32),
                        pltpu.VMEM((GROUPS, CHUNK, CHUNK), F32)] + hosted.scratch,
        out_shape=[jax.ShapeDtypeStruct((s, D_IN), BF16),
                   jax.ShapeDtypeStruct((GROUPS, CHUNK, CHUNK), F32),
                   jax.ShapeDtypeStruct((GROUPS, CHUNK), F32),
                   jax.ShapeDtypeStruct((1, D_GMLP), F32),
                   jax.ShapeDtypeStruct((1, D_GMLP), F32),
                   jax.ShapeDtypeStruct((1, LANES), F32)] + hosted.out_shape,
        compiler_params=_cparams(("arbitrary",)),
    )(proj, proj, dcat, probs, rope, rope, ln_g, ln_b, w_sp, b_sp_rows, *hosted.inputs)


def _adamw_math(w, g, m, v):
    m = ADAM_B1 * m + (1.0 - ADAM_B1) * g
    v = ADAM_B2 * v + (1.0 - ADAM_B2) * (g * g)
    m_hat = m / (1.0 - ADAM_B1 ** ADAM_STEP)
    v_hat = v / (1.0 - ADAM_B2 ** ADAM_STEP)
    delta = -ADAM_LR * (m_hat / (jnp.sqrt(v_hat) + ADAM_EPS) + ADAM_WD * w)
    return delta, m, v


def adamw_shard(terms, w, m, v, name, after=None):
    r, c = w.shape
    tr = _tile(r, (336, 128, 8))
    n_terms = len(terms)
    order_only = [] if after is None else [after]

    def body(*refs):
        w_ref, m_ref, v_ref, g_ref, d_ref, nm_ref, nv_ref = refs[n_terms:n_terms + 3] + refs[-4:]
        g = None
        for ref, (_, slots) in zip(refs[:n_terms], terms):
            for k in range(slots):
                part = ref[k].astype(F32)
                g = part if g is None else g + part
        g_ref[...] = g
        d_ref[...], nm_ref[...], nv_ref[...] = _adamw_math(w_ref[...], g, m_ref[...], v_ref[...])

    spec = pl.BlockSpec((tr, c), lambda i: (i, 0))
    return pl.pallas_call(
        body, name=name, grid=(r // tr,),
        in_specs=[pl.BlockSpec((slots, tr, c), lambda i: (0, i, 0)) for _, slots in terms] + [spec] * 3
        + [ANY] * len(order_only),
        out_specs=[spec] * 4, out_shape=[jax.ShapeDtypeStruct((r, c), F32)] * 4,
        compiler_params=_cparams(("arbitrary",)),
    )(*[a for a, _ in terms], w, m, v, *order_only)


def adamw_small(gathered, lane_windows, params):
    n_par = len(params)

    def body(*refs):
        g_refs = refs[:n_par + 1]
        wmv_refs = refs[n_par + 1:4 * n_par + 1]
        out_refs = refs[4 * n_par + 1:]

        def total(ref):
            acc = ref[0]
            for dev in range(1, N_DEV):
                acc = acc + ref[dev]
            return acc

        for i in range(n_par):
            w_ref, m_ref, v_ref = wmv_refs[3 * i:3 * i + 3]
            g = total(g_refs[i])
            if lane_windows[i] is not None:
                start, size = lane_windows[i]
                g = g[..., start:start + size]
            delta, new_m, new_v = _adamw_math(w_ref[...], g, m_ref[...], v_ref[...])
            for ref, val in zip(out_refs[4 * i:4 * i + 4], (g, delta, new_m, new_v)):
                ref[...] = val
        out_refs[4 * n_par][...] = total(g_refs[n_par])

    flat = [a for wmv in params for a in wmv]
    out_shape = [jax.ShapeDtypeStruct(w.shape, F32) for (w, _, _) in params for _ in range(4)]
    out_shape.append(jax.ShapeDtypeStruct(gathered[-1].shape[1:], F32))
    outs = pl.pallas_call(body, name="adamw_small", out_shape=out_shape, compiler_params=_cparams())(*gathered, *flat)
    return [tuple(outs[4 * i:4 * i + 4]) for i in range(n_par)], outs[-1]


def kernel(x, positions, g_pre, w_in, b_qkv, ln_v_g, ln_v_b, w_spatial, b_spatial, attn_sinks, w_out, g_post, loss_target, m_g_pre, m_w_in, m_b_qkv, m_ln_v_g, m_ln_v_b, m_w_spatial, m_b_spatial, m_attn_sinks, m_w_out, m_g_post, v_g_pre, v_w_in, v_b_qkv, v_ln_v_g, v_ln_v_b, v_w_spatial, v_b_spatial, v_attn_sinks, v_w_out, v_g_post):
    x2, target = x[0], loss_target[0]
    seq = x2.shape[0]

    wt_shard = w_in[0].T.astype(BF16)
    wo_shard = w_out[0].astype(BF16)
    pos_col = positions.reshape(seq, 1)
    half = HEAD_DIM // 2
    inv_freq = ROPE_THETA ** (-jnp.arange(half, dtype=F32) * (2.0 / HEAD_DIM))
    freq = jnp.tile(inv_freq, LANES // half).reshape(1, LANES)
    sign = jnp.tile(jnp.concatenate([-jnp.ones((half,), F32), jnp.ones((half,), F32)]), LANES // HEAD_DIM)
    sign = sign.reshape(1, LANES)
    bias = jnp.concatenate([jnp.zeros((1, OFF_Q), F32), b_qkv, jnp.zeros((1, D_ATTN), F32)], axis=1)
    proj, h, rope, wt = in_proj_gather(x2, pos_col, freq, sign, g_pre, wt_shard, bias)

    b_rows = jnp.broadcast_to(b_spatial[0][:, :, None], (GROUPS, CHUNK, CHUNK))
    sinks = attn_sinks[0]
    cat, probs, wo = mixer_fwd(proj, rope, ln_v_g, ln_v_b, w_spatial[0], b_rows, sinks,
                               comms=[gather_comm([wo_shard])])
    dy, dout, d_g_post, loss_part = out_proj_loss(cat, wo, x2, target, g_post)

    dcat = matmul_nt(dy, wo, "out_proj_bwd")
    d_wo, _ = matmul_tn(cat, dy, 512, "w_out_grad")
    dproj, d_w_sp, d_b_sp, d_ln_g, d_ln_b, d_sinks, parts_wo = mixer_bwd(
        proj, dcat, probs, rope, ln_v_g, ln_v_b, w_spatial[0], b_rows, comms=[scatter_comm([d_wo])])
    small_parts = [d_ln_g, d_ln_b, d_w_sp, d_b_sp, d_sinks, d_g_post, loss_part]
    d_wt, colsum, *landed = matmul_tn(dproj, h, 768, "w_in_grad", comms=[gather_comm(small_parts, stack=True)])

    sum_wt = pair_exchange_sum(d_wt, "grad_pair_sum_w_in")
    started = chips_exchange_start(sum_wt)
    grad_x, d_g_pre = in_proj_bwd(dproj, wt, x2, g_pre, dout, after=started[-1])
    sum_wt, far_wt = chips_exchange_wait(started, d_g_pre)
    late = jnp.concatenate([d_g_pre, colsum], axis=1)
    late_started, late_copy = gather_start(late)

    wo_out = adamw_shard([(parts_wo, N_DEV)], w_out[0], m_w_out[0], v_w_out[0], "adamw_w_out", after=late_started[-1])
    wt_out = adamw_shard([(sum_wt, 1), (far_wt, 3)], w_in[0].T, m_w_in[0].T, v_w_in[0].T, "adamw_w_in",
                         after=wo_out[0])
    landed_late = gather_wait(late_started, late_copy, wt_out[0])
    my_index = _index(*_my_place())
    all_late = lax.dynamic_update_slice(landed_late, late[None], (my_index, 0, 0))
    gathered = [all_late, all_late] + landed
    windows = [(0, D_MODEL), (D_MODEL + OFF_Q, D_QKV), None, None, None, None, (0, N_Q_HEADS), None]
    small = [(g_pre, m_g_pre, v_g_pre), (b_qkv, m_b_qkv, v_b_qkv), (ln_v_g, m_ln_v_g, v_ln_v_g),
             (ln_v_b, m_ln_v_b, v_ln_v_b), (w_spatial[0], m_w_spatial[0], v_w_spatial[0]),
             (b_spatial[0], m_b_spatial[0], v_b_spatial[0]), (attn_sinks, m_attn_sinks, v_attn_sinks),
             (g_post, m_g_post, v_g_post)]
    small_out, loss_row = adamw_small(gathered, windows, small)
    lead = [False, False, False, False, True, True, False, False]
    small_out = [tuple(a[None] if ld else a for a in leaf) for leaf, ld in zip(small_out, lead)]

    def leaves(k):
        gp, bq, lg, lb, ws, bs, sk, gpo = (leaf[k] for leaf in small_out)
        return [gp, wt_out[k].T[None], bq, lg, lb, ws, bs, sk, wo_out[k][None], gpo]

    return (loss_row[0, 0], grad_x[None], *leaves(0), *leaves(1), *leaves(2), *leaves(3))
```

```python
import functools

import jax
import jax.numpy as jnp
from jax import lax
from jax.experimental import pallas as pl
from jax.experimental.pallas import tpu as pltpu

F32 = jnp.float32
BF16 = jnp.bfloat16

D_MODEL = 2048
D_GMLP = 1024
D_ATTN = 1024
CHUNK = 128
GROUPS = 8
HEAD_DIM = 64
N_Q_HEADS = 16
N_KV_HEADS = 2
D_KV = N_KV_HEADS * HEAD_DIM
D_IN = 3 * D_GMLP + D_ATTN + 2 * D_KV + D_ATTN
OFF_U, OFF_V, OFF_ZA = 0, D_GMLP, 2 * D_GMLP
OFF_Q = 3 * D_GMLP
OFF_K = OFF_Q + D_ATTN
OFF_VA = OFF_K + D_KV
OFF_ZB = OFF_VA + D_KV
D_QKV = D_ATTN + 2 * D_KV
ROPE_THETA = 10000.0
EPS = 1e-6
SCALE = HEAD_DIM ** -0.5
NEG = -1e30
N_PAIRS = N_Q_HEADS // 2
PAIRS_PER_KV = N_PAIRS // N_KV_HEADS

ADAM_LR = 0.001
ADAM_B1 = 0.9
ADAM_B2 = 0.999
ADAM_EPS = 1e-08
ADAM_WD = 0.01
ADAM_STEP = 10

N_DEV = 8
LANES = 128
VMEM_LIMIT = 56 * 1024 * 1024
IN_PROJ_VMEM_LIMIT = 61 * 1024 * 1024

MESH = pl.DeviceIdType.MESH
ANY = pl.BlockSpec(memory_space=pl.ANY)


def _cparams(sem=None):
    return pltpu.CompilerParams(dimension_semantics=sem, vmem_limit_bytes=VMEM_LIMIT)


def _tile(n, prefs):
    for t in prefs:
        if n % t == 0:
            return t
    return n


def _sigmoid(z):
    return 1.0 / (1.0 + jnp.exp(-z))


def _dot(a, b, ca, cb):
    return lax.dot_general(a, b, (((ca,), (cb,)), ((), ())), preferred_element_type=F32)


def _my_place():
    return lax.axis_index("x"), lax.axis_index("y"), lax.axis_index("c")


def _chip_of(x, y, r):
    return (x ^ (r & 1), y ^ (r >> 1))


def _peer(x, y, c, k):
    return (x ^ (k >> 2), y ^ ((k >> 1) & 1), c ^ (k & 1))


def _index(px, py, pc):
    return 4 * px + 2 * py + pc


class _Comm:
    def __init__(self, inputs, out_shape, scratch, bind):
        self.inputs, self.out_shape, self.scratch, self.bind = list(inputs), list(out_shape), list(scratch), bind


def gather_comm(shards, stack=False):
    n_arr = len(shards)

    def bind(ins, outs, sems):
        send_sems, recv_sems, local_sems = sems
        x, y, c = _my_place()
        me, sibling = (x, y, c), (x, y, 1 - c)
        chips = [_chip_of(x, y, r) for r in (1, 2, 3)]

        def rows(a, px, py, pc):
            d = _index(px, py, pc)
            if stack:
                return outs[a].at[d]
            m = shards[a].shape[0]
            return outs[a].at[pl.ds(pl.multiple_of(d * m, 8), m), :]

        def copy(a, k, block, to, src=None):
            return pltpu.make_async_remote_copy(
                src_ref=rows(a, *block) if src is None else src, dst_ref=rows(a, *block),
                send_sem=send_sems.at[a * 7 + k], recv_sem=recv_sems.at[a * 7 + k],
                device_id=to, device_id_type=MESH)

        def mine(a):
            return pltpu.make_async_copy(ins[a], rows(a, *me), local_sems.at[a])

        def own_sends(a):
            return ([copy(a, 0, me, sibling, src=ins[a])]
                    + [copy(a, 1 + j, me, (*chip, c), src=ins[a]) for j, chip in enumerate(chips)])

        def start():
            for a in range(n_arr):
                mine(a).start()
                for cp in own_sends(a):
                    cp.start()

        def relay():
            for j, chip in enumerate(chips):
                for a in range(n_arr):
                    copy(a, 1 + j, (*chip, c), me).wait_recv()
                    copy(a, 4 + j, (*chip, c), sibling).start()

        def finish():
            for a in range(n_arr):
                copy(a, 0, sibling, me).wait_recv()
                for j, chip in enumerate(chips):
                    copy(a, 4 + j, (*chip, 1 - c), me).wait_recv()
                    copy(a, 4 + j, (*chip, c), sibling).wait_send()
                for cp in own_sends(a):
                    cp.wait_send()
                mine(a).wait()

        return start, relay, finish

    def gathered(s):
        return (N_DEV, *s.shape) if stack else (N_DEV * s.shape[0], s.shape[1])

    return _Comm(shards, [jax.ShapeDtypeStruct(gathered(s), s.dtype) for s in shards],
                 [pltpu.SemaphoreType.DMA((7 * n_arr,)), pltpu.SemaphoreType.DMA((7 * n_arr,)),
                  pltpu.SemaphoreType.DMA((n_arr,))], bind)


def scatter_comm(parts):
    n_arr = len(parts)

    def bind(ins, outs, sems):
        send_sems, recv_sems, local_sems = sems
        x, y, c = _my_place()
        my_index = _index(x, y, c)

        def block(a, d):
            m = parts[a].shape[0] // N_DEV
            return ins[a].at[pl.ds(pl.multiple_of(d * m, 16), m), :]

        def copy(a, k, slot):
            peer = _peer(x, y, c, k)
            return pltpu.make_async_remote_copy(
                src_ref=block(a, _index(*peer)), dst_ref=outs[a].at[slot],
                send_sem=send_sems.at[a * 7 + k - 1], recv_sem=recv_sems.at[a * 7 + k - 1],
                device_id=peer, device_id_type=MESH)

        def mine(a):
            return pltpu.make_async_copy(block(a, my_index), outs[a].at[my_index], local_sems.at[a])

        def start():
            for a in range(n_arr):
                mine(a).start()
                for k in range(1, 8):
                    copy(a, k, my_index).start()

        def finish():
            for a in range(n_arr):
                for k in range(1, 8):
                    copy(a, k, _index(*_peer(x, y, c, k))).wait_recv()
                    copy(a, k, my_index).wait_send()
                mine(a).wait()

        return start, (lambda: None), finish

    return _Comm(parts, [jax.ShapeDtypeStruct((N_DEV, p.shape[0] // N_DEV, p.shape[1]), p.dtype) for p in parts],
                 [pltpu.SemaphoreType.DMA((7 * n_arr,)), pltpu.SemaphoreType.DMA((7 * n_arr,)),
                  pltpu.SemaphoreType.DMA((n_arr,))], bind)


def run_comm(comm, name):
    n_in, n_out = len(comm.inputs), len(comm.out_shape)

    def body(*refs):
        start, relay, finish = comm.bind(refs[:n_in], refs[n_in:n_in + n_out], refs[n_in + n_out:])
        start()
        relay()
        finish()

    outs = pl.pallas_call(body, name=name, out_shape=comm.out_shape, in_specs=[ANY] * n_in,
                          out_specs=[ANY] * n_out, scratch_shapes=comm.scratch)(*comm.inputs)
    return list(outs)


def _chip_copy(r, src_ref, land_ref, send_sem, recv_sem):
    x, y, c = _my_place()
    return pltpu.make_async_remote_copy(src_ref=src_ref.at[r], dst_ref=land_ref.at[r - 1], send_sem=send_sem,
                                        recv_sem=recv_sem, device_id=(*_chip_of(x, y, r), c), device_id_type=MESH)


def chips_exchange_start(sums):
    def body(src_ref, land_ref, s1, s2, s3, r1, r2, r3, src_thru, land_thru, token):
        del src_thru, land_thru
        for r, send_sem, recv_sem in ((1, s1, r1), (2, s2, r2), (3, s3, r3)):
            _chip_copy(r, src_ref, land_ref, send_sem, recv_sem).start()
        token[...] = jnp.zeros_like(token)

    land = lax.empty((3,) + sums.shape[1:], sums.dtype)
    sem = pltpu.SemaphoreType.DMA(())
    hbm = pl.BlockSpec(memory_space=pltpu.HBM)
    sem_spec = pl.BlockSpec(memory_space=pltpu.SEMAPHORE)
    return pl.pallas_call(
        body, name="grad_exchange_chips_start",
        out_shape=(sem,) * 6 + (pltpu.HBM(sums.shape, sums.dtype), pltpu.HBM(land.shape, land.dtype),
                                jax.ShapeDtypeStruct((8, LANES), F32)),
        in_specs=(hbm, hbm), out_specs=(sem_spec,) * 6 + (hbm, hbm, pl.BlockSpec(memory_space=pltpu.VMEM)),
        input_output_aliases={0: 6, 1: 7},
        compiler_params=pltpu.CompilerParams(has_side_effects=pltpu.SideEffectType.DATAFLOW_SIDE_EFFECTING),
    )(pltpu.with_memory_space_constraint(sums, pltpu.HBM), pltpu.with_memory_space_constraint(land, pltpu.HBM))


def chips_exchange_wait(started, after):
    s1, s2, s3, r1, r2, r3, src_thru, land_thru, _ = started

    def body(src_ref, land_ref, s1, s2, s3, r1, r2, r3, after_ref, src_out, land_out):
        del after_ref, src_out, land_out
        for r, send_sem, recv_sem in ((1, s1, r1), (2, s2, r2), (3, s3, r3)):
            copy = _chip_copy(r, src_ref, land_ref, send_sem, recv_sem)
            copy.wait_send()
            copy.wait_recv()

    hbm = pl.BlockSpec(memory_space=pltpu.HBM)
    sem_spec = pl.BlockSpec(memory_space=pltpu.SEMAPHORE)
    return pl.pallas_call(
        body, name="grad_exchange_chips_wait",
        out_shape=(pltpu.HBM(src_thru.shape, src_thru.dtype), pltpu.HBM(land_thru.shape, land_thru.dtype)),
        in_specs=(hbm, hbm) + (sem_spec,) * 6 + (pl.BlockSpec(memory_space=pl.ANY),), out_specs=(hbm, hbm),
        input_output_aliases={0: 0, 1: 1},
        compiler_params=pltpu.CompilerParams(has_side_effects=pltpu.SideEffectType.DATAFLOW_SIDE_EFFECTING),
    )(src_thru, land_thru, s1, s2, s3, r1, r2, r3, after)


def gather_start(shard):
    def copy(k, src_ref, land_ref, send_sem, recv_sem):
        x, y, c = _my_place()
        return pltpu.make_async_remote_copy(src_ref=src_ref, dst_ref=land_ref.at[_index(x, y, c)], send_sem=send_sem,
                                            recv_sem=recv_sem, device_id=_peer(x, y, c, k), device_id_type=MESH)

    def start_body(src_ref, land_ref, *rest):
        sems, token = rest[:14], rest[16]
        for k in range(1, 8):
            copy(k, src_ref, land_ref, sems[k - 1], sems[7 + k - 1]).start()
        token[...] = jnp.zeros_like(token)

    land = lax.empty((N_DEV,) + shard.shape, shard.dtype)
    sem = pltpu.SemaphoreType.DMA(())
    hbm = pl.BlockSpec(memory_space=pltpu.HBM)
    sem_spec = pl.BlockSpec(memory_space=pltpu.SEMAPHORE)
    started = pl.pallas_call(
        start_body, name="allgather_late_grads_start",
        out_shape=(sem,) * 14 + (pltpu.HBM(shard.shape, shard.dtype), pltpu.HBM(land.shape, land.dtype),
                                 jax.ShapeDtypeStruct((8, LANES), F32)),
        in_specs=(hbm, hbm), out_specs=(sem_spec,) * 14 + (hbm, hbm, pl.BlockSpec(memory_space=pltpu.VMEM)),
        input_output_aliases={0: 14, 1: 15},
        compiler_params=pltpu.CompilerParams(has_side_effects=pltpu.SideEffectType.DATAFLOW_SIDE_EFFECTING),
    )(pltpu.with_memory_space_constraint(shard, pltpu.HBM), pltpu.with_memory_space_constraint(land, pltpu.HBM))
    return started, copy


def gather_wait(started, copy, after):
    sems, src_thru, land_thru = started[:14], started[14], started[15]

    def wait_body(src_ref, land_ref, *rest):
        for k in range(1, 8):
            cp = copy(k, src_ref, land_ref, rest[k - 1], rest[7 + k - 1])
            cp.wait_send()
            cp.wait_recv()

    hbm = pl.BlockSpec(memory_space=pltpu.HBM)
    sem_spec = pl.BlockSpec(memory_space=pltpu.SEMAPHORE)
    return pl.pallas_call(
        wait_body, name="allgather_late_grads_wait",
        out_shape=(pltpu.HBM(src_thru.shape, src_thru.dtype), pltpu.HBM(land_thru.shape, land_thru.dtype)),
        in_specs=(hbm, hbm) + (sem_spec,) * 14 + (pl.BlockSpec(memory_space=pl.ANY),), out_specs=(hbm, hbm),
        input_output_aliases={0: 0, 1: 1},
        compiler_params=pltpu.CompilerParams(has_side_effects=pltpu.SideEffectType.DATAFLOW_SIDE_EFFECTING),
    )(src_thru, land_thru, *sems, after)[1]


class _Hosted:
    def __init__(self, comms):
        self.comms = list(comms)
        self.inputs = [a for cm in self.comms for a in cm.inputs]
        self.out_shape = [s for cm in self.comms for s in cm.out_shape]
        self.scratch = [s for cm in self.comms for s in cm.scratch]
        self.in_specs = [ANY] * len(self.inputs)
        self.out_specs = [ANY] * len(self.out_shape)

    def split(self, refs, n_in, n_out, n_scratch):
        ni, no = len(self.inputs), len(self.out_shape)
        ins, rest = refs[:n_in], refs[n_in:]
        c_ins, rest = rest[:ni], rest[ni:]
        outs, rest = rest[:n_out], rest[n_out:]
        c_outs, rest = rest[:no], rest[no:]
        scratch, c_sems = rest[:n_scratch], rest[n_scratch:]
        phases = []
        for cm in self.comms:
            a, b, s = len(cm.inputs), len(cm.out_shape), len(cm.scratch)
            phases.append(cm.bind(c_ins[:a], c_outs[:b], c_sems[:s]))
            c_ins, c_outs, c_sems = c_ins[a:], c_outs[b:], c_sems[s:]
        return ins, outs, scratch, phases


def _before_step(phases, step, n_steps):
    if not phases:
        return

    @pl.when(step == 0)
    def _():
        for start, _, _ in phases:
            start()

    @pl.when(step == n_steps // 2)
    def _():
        for _, relay, _ in phases:
            relay()


def _after_step(phases, step, n_steps):
    if not phases:
        return

    @pl.when(step == n_steps - 1)
    def _():
        for _, _, finish in phases:
            finish()


def pair_exchange_sum(part, name):
    m, n = part.shape[0] // N_DEV, part.shape[1]

    def body(part_ref, out_ref, got, mine, summed, send_sems, recv_sems, in_sems, out_sems):
        x, y, c = _my_place()

        def rows(r, core):
            owner = _index(*_chip_of(x, y, r), core)
            return part_ref.at[pl.ds(pl.multiple_of(owner * m, 16), m), :]

        def to_sibling(r):
            return pltpu.make_async_remote_copy(src_ref=rows(r, 1 - c), dst_ref=got.at[r], send_sem=send_sems.at[r],
                                                recv_sem=recv_sems.at[r], device_id=(x, y, 1 - c), device_id_type=MESH)

        def fetch(r):
            return pltpu.make_async_copy(rows(r, c), mine.at[r % 2], in_sems.at[r % 2])

        def put(r):
            return pltpu.make_async_copy(summed.at[r % 2], out_ref.at[r], out_sems.at[r % 2])

        for r in range(4):
            to_sibling(r).start()
        fetch(0).start()
        for r in range(4):
            if r + 1 < 4:
                fetch(r + 1).start()
            fetch(r).wait()
            to_sibling(r).wait_recv()
            if r >= 2:
                put(r - 2).wait()
            summed[r % 2] = (mine[r % 2].astype(F32) + got[r].astype(F32)).astype(summed.dtype)
            put(r).start()
        for r in (2, 3):
            put(r).wait()
        for r in range(4):
            to_sibling(r).wait_send()

    return pl.pallas_call(
        body, name=name, out_shape=jax.ShapeDtypeStruct((4, m, n), part.dtype),
        in_specs=[ANY], out_specs=ANY,
        scratch_shapes=[pltpu.VMEM((4, m, n), part.dtype), pltpu.VMEM((2, m, n), part.dtype),
                        pltpu.VMEM((2, m, n), part.dtype), pltpu.SemaphoreType.DMA((4,)),
                        pltpu.SemaphoreType.DMA((4,)), pltpu.SemaphoreType.DMA((2,)), pltpu.SemaphoreType.DMA((2,))],
        compiler_params=_cparams(),
    )(part)


def in_proj_gather(x, pos_col, freq, sign, g_pre, wt_shard, bias):
    s, d = x.shape
    tm = _tile(s, (512, 256, 128))
    nt = s // tm
    tc = _tile(s, (256, 128))
    nc = s // tc
    m = wt_shard.shape[0]
    half = D_IN // 2
    xi = lax.axis_index("x")
    order = jnp.stack([xi, 1 - xi]).astype(jnp.int32)

    def body(order_ref, x_hbm, pos_hbm, freq_ref, sign_ref, g_ref, b_ref, shard_ref,
             proj_ref, h_hbm, rope_hbm, wt_ref,
             w_vmem, h_vmem, xbuf, posbuf, ropebuf, send_sems, recv_sems, local_sems, in_sems, out_sems):
        del order_ref
        p, i = pl.program_id(0), pl.program_id(1)
        xx, yy, cc = _my_place()
        me, sibling = (xx, yy, cc), (xx, yy, 1 - cc)
        chips = [_chip_of(xx, yy, r) for r in (1, 2, 3)]

        def rows(px, py, pc):
            return wt_ref.at[pl.ds(pl.multiple_of(_index(px, py, pc) * m, 16), m), :]

        def copy(k, block, to, src=None):
            return pltpu.make_async_remote_copy(
                src_ref=rows(*block) if src is None else src, dst_ref=rows(*block),
                send_sem=send_sems.at[k], recv_sem=recv_sems.at[k], device_id=to, device_id_type=MESH)

        def mine():
            return pltpu.make_async_copy(shard_ref, rows(*me), local_sems.at[0])

        def to_sibling():
            return copy(0, me, sibling, src=shard_ref)

        def to_chip(j):
            return copy(1 + j, me, (*chips[j], cc), src=shard_ref)

        def relay(j):
            copy(1 + j, (*chips[j], cc), me).wait_recv()
            copy(4 + j, (*chips[j], cc), sibling).start()

        def relayed(j):
            copy(4 + j, (*chips[j], 1 - cc), me).wait_recv()

        def load_half(which, slot):
            rows_of_half = wt_ref.at[pl.ds(pl.multiple_of(which * half, 16), half), :]
            load = pltpu.make_async_copy(rows_of_half, w_vmem.at[slot], local_sems.at[1 + slot])
            load.start()
            load.wait()

        def piece(ref, c):
            return ref.at[pl.ds(c * tc, tc), :]

        def fetch(c):
            return (pltpu.make_async_copy(piece(x_hbm, c), xbuf.at[c % 2], in_sems.at[c % 2]),
                    pltpu.make_async_copy(piece(pos_hbm, c), posbuf.at[c % 2], in_sems.at[2 + c % 2]))

        def put(c):
            return (pltpu.make_async_copy(piece(h_vmem, c), piece(h_hbm, c), out_sems.at[c % 2]),
                    pltpu.make_async_copy(ropebuf.at[c % 2], piece(rope_hbm, c), out_sems.at[2 + c % 2]))

        def prologue():
            for cp in fetch(0):
                cp.start()
            for c in range(nc):
                if c + 1 < nc:
                    for cp in fetch(c + 1):
                        cp.start()
                for cp in fetch(c):
                    cp.wait()
                if c >= 2:
                    for cp in put(c - 2):
                        cp.wait()
                xv = xbuf[c % 2]
                r = lax.rsqrt(jnp.mean(xv * xv, axis=-1, keepdims=True) + EPS)
                h_vmem[c * tc:(c + 1) * tc, :] = (xv * r * g_ref[...]).astype(BF16)
                ang = posbuf[c % 2].astype(F32) * freq_ref[...]
                ropebuf[c % 2, :, :LANES] = jnp.cos(ang)
                ropebuf[c % 2, :, LANES:] = jnp.sin(ang) * sign_ref[...]
                for cp in put(c):
                    cp.start()
            for c in range(max(nc - 2, 0), nc):
                for cp in put(c):
                    cp.wait()

        @pl.when(jnp.logical_and(p == 0, i == 0))
        def _():
            mine().start()
            to_sibling().start()
            to_chip(1).start()
            to_chip(0).start()
            prologue()
            copy(0, sibling, me).wait_recv()
            relay(1)
            relayed(1)
            mine().wait()
            to_chip(1).wait_send()
            to_chip(0).wait_send()
            to_chip(2).start()
            load_half(xx, 0)

        @pl.when(jnp.logical_and(p == 1, i == 0))
        def _():
            relayed(0)
            relayed(2)
            load_half(1 - xx, 1)

        def project(slot):
            hb = h_vmem[pl.ds(pl.multiple_of(i * tm, tm), tm), :]
            proj_ref[...] = (_dot(hb, w_vmem[slot], 1, 1) + b_ref[...]).astype(BF16)

        @pl.when(p == 0)
        def _():
            project(0)

        @pl.when(p == 1)
        def _():
            project(1)

        @pl.when(jnp.logical_and(p == 0, i == 1))
        def _():
            relay(0)

        @pl.when(jnp.logical_and(p == 0, i == nt - 1))
        def _():
            relay(2)

        @pl.when(jnp.logical_and(p == 1, i == nt - 1))
        def _():
            to_sibling().wait_send()
            to_chip(2).wait_send()
            for j in range(3):
                copy(4 + j, (*chips[j], cc), sibling).wait_send()

    const = lambda p, i, o: (0, 0)
    return pl.pallas_call(
        body, name="in_proj_gather",
        grid_spec=pltpu.PrefetchScalarGridSpec(
            num_scalar_prefetch=1, grid=(2, nt),
            in_specs=[ANY, ANY,
                      pl.BlockSpec((1, LANES), const),
                      pl.BlockSpec((1, LANES), const),
                      pl.BlockSpec((1, d), const),
                      pl.BlockSpec((1, half), lambda p, i, o: (0, o[p])),
                      ANY],
            out_specs=[pl.BlockSpec((tm, half), lambda p, i, o: (i, o[p])), ANY, ANY, ANY],
            scratch_shapes=[pltpu.VMEM((2, half, d), BF16), pltpu.VMEM((s, d), BF16),
                            pltpu.VMEM((2, tc, d), F32), pltpu.VMEM((2, tc, 1), jnp.int32),
                            pltpu.VMEM((2, tc, 2 * LANES), F32),
                            pltpu.SemaphoreType.DMA((7,)), pltpu.SemaphoreType.DMA((7,)),
                            pltpu.SemaphoreType.DMA((3,)), pltpu.SemaphoreType.DMA((4,)),
                            pltpu.SemaphoreType.DMA((4,))]),
        out_shape=[jax.ShapeDtypeStruct((s, D_IN), BF16), jax.ShapeDtypeStruct((s, d), BF16),
                   jax.ShapeDtypeStruct((s, 2 * LANES), F32), jax.ShapeDtypeStruct((D_IN, d), BF16)],
        compiler_params=pltpu.CompilerParams(dimension_semantics=("arbitrary", "arbitrary"),
                                             vmem_limit_bytes=IN_PROJ_VMEM_LIMIT),
    )(order, x, pos_col, freq, sign, g_pre, bias, wt_shard)


def out_proj_loss(cat, w_out, x, target, g_post):
    s, d = x.shape
    tm = _tile(s, (256, 128))
    kc = _tile(d, (512, 128))
    pieces = w_out.shape[0] // kc

    steps = s // tm
    ring = 3

    def body(cat_ref, w_hbm, x_hbm, t_hbm, g_ref, dy_ref, dout_ref, dg_ref, loss_ref, w_ref, w_sems, xbuf, tbuf, in_sems):
        step = pl.program_id(0)
        slot = step % ring
        x_ref, t_ref = xbuf.at[slot], tbuf.at[slot]

        def w_load(j):
            return pltpu.make_async_copy(w_hbm.at[j * kc:(j + 1) * kc, :], w_ref.at[j * kc:(j + 1) * kc, :], w_sems.at[j])

        def fetch(at_step, into):
            rows = pl.ds(pl.multiple_of(at_step * tm, tm), tm)
            return (pltpu.make_async_copy(x_hbm.at[rows, :], xbuf.at[into], in_sems.at[into]),
                    pltpu.make_async_copy(t_hbm.at[rows, :], tbuf.at[into], in_sems.at[ring + into]))

        def arrived():
            for cp in fetch(step, slot):
                cp.wait()

        @pl.when(step == 0)
        def _():
            for j in range(pieces):
                w_load(j).start()
            for ahead in range(min(ring - 1, steps)):
                for cp in fetch(ahead, ahead):
                    cp.start()

        @pl.when(step + ring - 1 < steps)
        def _():
            for cp in fetch(step + ring - 1, (step + ring - 1) % ring):
                cp.start()

        @pl.when(step == 0)
        def _():
            dg_ref[...] = jnp.zeros_like(dg_ref)
            loss_ref[...] = jnp.zeros_like(loss_ref)
            ys = [None] * (tm // CHUNK)
            for j in range(pieces):
                w_load(j).wait()
                for c in range(tm // CHUNK):
                    part = _dot(cat_ref[c * CHUNK:(c + 1) * CHUNK, j * kc:(j + 1) * kc], w_ref[j * kc:(j + 1) * kc, :], 1, 0)
                    ys[c] = part if ys[c] is None else ys[c] + part
            arrived()
            loss_and_back(ys, x_ref, t_ref, g_ref, dy_ref, dout_ref, dg_ref, loss_ref)

        @pl.when(step > 0)
        def _():
            ys = [_dot(cat_ref[c0:c0 + CHUNK, :], w_ref[...], 1, 0) for c0 in range(0, tm, CHUNK)]
            arrived()
            loss_and_back(ys, x_ref, t_ref, g_ref, dy_ref, dout_ref, dg_ref, loss_ref)

    def loss_and_back(ys, x_ref, t_ref, g_ref, dy_ref, dout_ref, dg_ref, loss_ref):
        g = g_ref[...]
        for c0 in range(0, tm, CHUNK):
            rows = slice(c0, c0 + CHUNK)
            yv = ys[c0 // CHUNK]
            r = lax.rsqrt(jnp.mean(yv * yv, axis=-1, keepdims=True) + EPS)
            nrm = yv * r
            err = x_ref[rows, :] + nrm * g - t_ref[rows, :]
            loss_ref[...] += 0.5 * jnp.sum(jnp.sum(err * err, axis=-1, keepdims=True), axis=0, keepdims=True) / d
            dout = err * (1.0 / d)
            dout_ref[rows, :] = dout
            dg_ref[...] += jnp.sum(dout * nrm, axis=0, keepdims=True)
            dn = dout * g
            dy = r * (dn - nrm * jnp.mean(dn * nrm, axis=-1, keepdims=True))
            dy_ref[rows, :] = dy.astype(BF16)

    return pl.pallas_call(
        body, name="out_proj_loss", grid=(steps,),
        in_specs=[pl.BlockSpec((tm, d), lambda i: (i, 0)), ANY, ANY, ANY, pl.BlockSpec((1, d), lambda i: (0, 0))],
        out_specs=[pl.BlockSpec((tm, d), lambda i: (i, 0)),
                   pl.BlockSpec((tm, d), lambda i: (i, 0)),
                   pl.BlockSpec((1, d), lambda i: (0, 0)),
                   pl.BlockSpec((1, LANES), lambda i: (0, 0))],
        out_shape=[jax.ShapeDtypeStruct((s, d), BF16), jax.ShapeDtypeStruct((s, d), F32),
                   jax.ShapeDtypeStruct((1, d), F32), jax.ShapeDtypeStruct((1, LANES), F32)],
        scratch_shapes=[pltpu.VMEM(w_out.shape, w_out.dtype), pltpu.SemaphoreType.DMA((pieces,)),
                        pltpu.VMEM((ring, tm, d), F32), pltpu.VMEM((ring, tm, d), F32),
                        pltpu.SemaphoreType.DMA((2 * ring,))],
        compiler_params=_cparams(("arbitrary",)),
    )(cat, w_out, x, target, g_post)


def matmul_nt(a, b, name):
    m, k = a.shape
    n = b.shape[0]
    tm = _tile(m, (512, 256, 128))

    def body(a_ref, b_ref, o_ref):
        o_ref[...] = _dot(a_ref[...], b_ref[...], 1, 1).astype(o_ref.dtype)

    return pl.pallas_call(
        body, name=name, grid=(m // tm,),
        in_specs=[pl.BlockSpec((tm, k), lambda i: (i, 0)), pl.BlockSpec((n, k), lambda i: (0, 0))],
        out_specs=pl.BlockSpec((tm, n), lambda i: (i, 0)),
        out_shape=jax.ShapeDtypeStruct((m, n), BF16),
        compiler_params=_cparams(("arbitrary",)),
    )(a, b)


def matmul_tn(a, b, tm, name, comms=()):
    k, m = a.shape
    n = b.shape[1]
    steps = m // tm
    hosted = _Hosted(comms)

    kc = _tile(k, (1024, 128))
    pieces = k // kc

    def body(*refs):
        (a_ref, b_hbm), (o_ref, cs_ref), (b_ref, b_sems), phases = hosted.split(refs, 2, 2, 2)
        step = pl.program_id(0)
        _before_step(phases, step, steps)

        def b_load(j):
            return pltpu.make_async_copy(b_hbm.at[j * kc:(j + 1) * kc, :], b_ref.at[j * kc:(j + 1) * kc, :], b_sems.at[j])

        @pl.when(step == 0)
        def _():
            for j in range(pieces):
                b_load(j).start()
            acc = None
            for j in range(pieces):
                b_load(j).wait()
                part = _dot(a_ref[j * kc:(j + 1) * kc, :], b_ref[j * kc:(j + 1) * kc, :], 0, 0)
                acc = part if acc is None else acc + part
            o_ref[...] = acc.astype(o_ref.dtype)

        @pl.when(step > 0)
        def _():
            o_ref[...] = _dot(a_ref[...], b_ref[...], 0, 0).astype(o_ref.dtype)

        rows = _tile(k, (512, 128))
        cs = jnp.zeros((1, tm), F32)
        for r0 in range(0, k, rows):
            cs = cs + jnp.sum(a_ref[r0:r0 + rows, :].astype(F32), axis=0, keepdims=True)
        cs_ref[...] = cs
        _after_step(phases, step, steps)

    return pl.pallas_call(
        body, name=name, grid=(steps,),
        in_specs=[pl.BlockSpec((k, tm), lambda i: (0, i)), ANY] + hosted.in_specs,
        out_specs=[pl.BlockSpec((tm, n), lambda i: (i, 0)), pl.BlockSpec((1, tm), lambda i: (0, i))] + hosted.out_specs,
        out_shape=[jax.ShapeDtypeStruct((m, n), BF16), jax.ShapeDtypeStruct((1, m), F32)] + hosted.out_shape,
        scratch_shapes=[pltpu.VMEM((k, n), b.dtype), pltpu.SemaphoreType.DMA((pieces,))] + hosted.scratch,
        compiler_params=_cparams(("arbitrary",)),
    )(a, b, *hosted.inputs)


def in_proj_bwd(dproj, wt, x, g_pre, dout, comms=(), after=None):
    s, d = x.shape
    tm = _tile(s, (512, 256, 128))
    steps = s // tm
    nsub = tm // CHUNK
    kw = 8 * LANES
    kchunks = [(k0, kw) for k0 in range(0, D_IN - D_IN % kw, kw)]
    if D_IN % kw:
        kchunks.append((D_IN - D_IN % kw, D_IN % kw))
    ksplit = len(kchunks)
    hosted = _Hosted(comms)
    order_only = [] if after is None else [after]

    def body(*refs):
        ((*dp_refs, w_hbm, x_hbm, g_ref, dout_hbm), (gx_hbm, dg_ref),
         (w_ref, w_sems, xbuf, dbuf, gbuf, in_sems, out_sems), phases) = hosted.split(
             refs[:4 + ksplit] + refs[4 + ksplit + len(order_only):], 4 + ksplit, 2, 7)
        step = pl.program_id(0)
        _before_step(phases, step, steps)

        def rows_of(ref, c):
            return ref.at[pl.ds(pl.multiple_of(step * tm + c * CHUNK, CHUNK), CHUNK), :]

        def fetches(c):
            return (pltpu.make_async_copy(rows_of(x_hbm, c), xbuf.at[c], in_sems.at[c]),
                    pltpu.make_async_copy(rows_of(dout_hbm, c), dbuf.at[c], in_sems.at[nsub + c]))

        def put(c):
            return pltpu.make_async_copy(gbuf.at[c % 2], rows_of(gx_hbm, c), out_sems.at[c % 2])

        for c in range(nsub):
            for cp in fetches(c):
                cp.start()

        def w_load(j):
            k0, kw = kchunks[j]
            return pltpu.make_async_copy(w_hbm.at[k0:k0 + kw, :], w_ref.at[k0:k0 + kw, :], w_sems.at[j])

        @pl.when(step == 0)
        def _():
            dg_ref[...] = jnp.zeros_like(dg_ref)
            for j in range(ksplit):
                w_load(j).start()

        dh_all = None
        for j, ((k0, kw), dp_ref) in enumerate(zip(kchunks, dp_refs)):
            @pl.when(step == 0)
            def _():
                w_load(j).wait()

            part = _dot(dp_ref[...], w_ref[k0:k0 + kw, :], 1, 0)
            dh_all = part if dh_all is None else dh_all + part
        for c in range(nsub):
            for cp in fetches(c):
                cp.wait()
            if c >= 2:
                put(c - 2).wait()
            elif c < nsub:
                @pl.when(step > 0)
                def _():
                    put(max(nsub - 2, 0) + c).wait()
            dh = dh_all[c * CHUNK:(c + 1) * CHUNK, :]
            xv = xbuf[c]
            r = lax.rsqrt(jnp.mean(xv * xv, axis=-1, keepdims=True) + EPS)
            xn = xv * r
            dg_ref[...] += jnp.sum(dh * xn, axis=0, keepdims=True)
            dn = dh * g_ref[...]
            gbuf[c % 2] = dbuf[c] + r * (dn - xn * jnp.mean(dn * xn, axis=-1, keepdims=True))
            put(c).start()
        @pl.when(step == steps - 1)
        def _():
            for c in range(max(nsub - 2, 0), nsub):
                put(c).wait()

        _after_step(phases, step, steps)

    side_in, side_out = pltpu.VMEM((nsub, CHUNK, d), F32), pltpu.VMEM((2, CHUNK, d), F32)
    row = pl.BlockSpec((1, d), lambda i: (0, 0))
    return pl.pallas_call(
        body, name="in_proj_bwd", grid=(steps,),
        in_specs=[pl.BlockSpec((tm, kw), functools.partial(lambda j, i: (i, j), k0 // kw)) for k0, kw in kchunks]
        + [ANY, ANY, row, ANY] + [ANY] * len(order_only) + hosted.in_specs,
        out_specs=[ANY, row] + hosted.out_specs,
        out_shape=[jax.ShapeDtypeStruct((s, d), F32), jax.ShapeDtypeStruct((1, d), F32)] + hosted.out_shape,
        scratch_shapes=[pltpu.VMEM((D_IN, d), BF16), pltpu.SemaphoreType.DMA((ksplit,)), side_in, side_in, side_out,
                        pltpu.SemaphoreType.DMA((2 * nsub,)), pltpu.SemaphoreType.DMA((2,))] + hosted.scratch,
        compiler_params=_cparams(("arbitrary",)),
    )(*([dproj] * ksplit), wt, x, g_pre, dout, *order_only, *hosted.inputs)


def _lane_iota(shape):
    return lax.broadcasted_iota(jnp.int32, shape, len(shape) - 1)


def _partner(v):
    low = (_lane_iota(v.shape) % HEAD_DIM) < (HEAD_DIM // 2)
    return jnp.where(low, pltpu.roll(v, LANES - HEAD_DIM // 2, 1), pltpu.roll(v, HEAD_DIM // 2, 1))


def _rope(v, cos, sin_signed):
    return v * cos + _partner(v) * sin_signed


def _rope_transposed(dv, cos, sin_signed):
    return dv * cos - _partner(dv) * sin_signed


def _both_halves(v, kv_head):
    keep = (_lane_iota(v.shape) >= HEAD_DIM) if kv_head else (_lane_iota(v.shape) < HEAD_DIM)
    return jnp.where(keep, v, pltpu.roll(v, HEAD_DIM, 1))


def _fold_halves(acc):
    return acc + pltpu.roll(acc, HEAD_DIM, 1)


def _by_half(a, b):
    shape = jnp.broadcast_shapes(jnp.shape(a), jnp.shape(b))
    return jnp.where(_lane_iota(shape) < HEAD_DIM, a, b)


def _stack_heads(pair):
    return jnp.concatenate([_by_half(pair, 0.0), _by_half(0.0, pair)], axis=0)


def _band_bias(has_prev):
    i = lax.broadcasted_iota(jnp.int32, (2 * CHUNK, 2 * CHUNK), 0) % CHUNK
    j = lax.broadcasted_iota(jnp.int32, (2 * CHUNK, 2 * CHUNK), 1)
    band = jnp.logical_and(j > i, j <= i + CHUNK)
    return jnp.where(jnp.logical_and(band, jnp.logical_or(j >= CHUNK, has_prev)), 0.0, NEG)


def _probs_staged(qm2s, kk2s, bias, sink_cols):
    k = range(len(qm2s))
    scs = [_dot(qm2s[i], kk2s[i], 1, 1) + bias for i in k]
    mxs = [jnp.maximum(jnp.max(scs[i], axis=-1, keepdims=True), sink_cols[i]) for i in k]
    ps = [jnp.exp(scs[i] - mxs[i]) for i in k]
    ess = [jnp.exp(sink_cols[i] - mxs[i]) for i in k]
    invs = [1.0 / (jnp.sum(ps[i], axis=-1, keepdims=True) + ess[i]) for i in k]
    return [ps[i] * invs[i] for i in k], [ess[i] * invs[i] for i in k]


def _sink_col(sinks_ref, pair):
    row = lax.broadcasted_iota(jnp.int32, (2 * CHUNK, 1), 0)
    return jnp.where(row < CHUNK, sinks_ref[2 * pair], sinks_ref[2 * pair + 1])


def _layer_norm_parts(v):
    mu = jnp.mean(v, axis=-1, keepdims=True)
    xc = v - mu
    rstd = lax.rsqrt(jnp.mean(xc * xc, axis=-1, keepdims=True) + EPS)
    return xc * rstd, rstd


def _masked_spatial(w_ref, g):
    t = lax.broadcasted_iota(jnp.int32, (CHUNK, CHUNK), 0)
    sidx = lax.broadcasted_iota(jnp.int32, (CHUNK, CHUNK), 1)
    return jnp.where(t >= sidx, w_ref[g], 0.0).astype(BF16)


def _keys_values(kv_ref, kvp_ref, rope_ref, ropep_ref):
    cos_c, sin_c = rope_ref[:, :LANES], rope_ref[:, LANES:]
    cos_p, sin_p = ropep_ref[:, :LANES], ropep_ref[:, LANES:]
    k_c = _rope(kv_ref[:, :D_KV].astype(F32), cos_c, sin_c)
    k_p = _rope(kvp_ref[:, :D_KV].astype(F32), cos_p, sin_p)
    keys = jnp.concatenate([k_p, k_c], axis=0)
    vals = jnp.concatenate([kvp_ref[:, D_KV:], kv_ref[:, D_KV:]], axis=0).astype(F32)
    return keys, vals, (cos_c, sin_c, cos_p, sin_p)


def mixer_fwd(proj, rope, ln_g, ln_b, w_sp, b_sp_rows, sinks, comms=()):
    s = proj.shape[0]
    nb = s // CHUNK
    hosted = _Hosted(comms)

    def body(sinks_ref, *refs):
        ((proj_ref, kvp_ref, rope_ref, ropep_ref, lng_ref, lnb_ref, w_ref, b_ref), (cat_ref, p_ref), _,
         phases) = hosted.split(refs, 8, 2, 0)
        n = pl.program_id(0)
        _before_step(phases, n, nb)
        xhat, _ = _layer_norm_parts(proj_ref[:, OFF_V:OFF_V + D_GMLP].astype(F32))
        vnb = (xhat * lng_ref[...] + lnb_ref[...]).astype(BF16)
        mixeds = [_dot(_masked_spatial(w_ref, g), vnb[:, g * CHUNK:(g + 1) * CHUNK], 1, 0) + b_ref[g]
                  for g in range(GROUPS)]
        for g in range(GROUPS):
            za = proj_ref[:, OFF_ZA + g * CHUNK:OFF_ZA + (g + 1) * CHUNK].astype(F32)
            u = proj_ref[:, OFF_U + g * CHUNK:OFF_U + (g + 1) * CHUNK].astype(F32)
            cat_ref[:, g * CHUNK:(g + 1) * CHUNK] = (u * mixeds[g] * (za * _sigmoid(za))).astype(BF16)
        kv_ref = proj_ref.at[:, OFF_K:OFF_K + 2 * D_KV]
        keys, vals, (cos_c, sin_c, _, _) = _keys_values(kv_ref, kvp_ref, rope_ref, ropep_ref)
        cos_q, sin_q = cos_c * SCALE, sin_c * SCALE
        bias = _band_bias(n > 0)
        kk2 = [_both_halves(keys, kvh).astype(BF16) for kvh in range(N_KV_HEADS)]
        vv2 = [_both_halves(vals, kvh).astype(BF16) for kvh in range(N_KV_HEADS)]
        first_col = _lane_iota((2 * CHUNK, 2 * CHUNK)) == 0
        for kvh in range(N_KV_HEADS):
            pairs = range(kvh * PAIRS_PER_KV, (kvh + 1) * PAIRS_PER_KV)
            qms = [_stack_heads(_rope(proj_ref[:, OFF_Q + pair * LANES:OFF_Q + (pair + 1) * LANES].astype(F32),
                                      cos_q, sin_q)).astype(BF16) for pair in pairs]
            probs, sink_probs = _probs_staged(qms, [kk2[kvh]] * PAIRS_PER_KV, bias,
                                              [_sink_col(sinks_ref, pair) for pair in pairs])
            pbs = [p.astype(BF16) for p in probs]
            outs = [_dot(pb, vv2[kvh], 1, 0) for pb in pbs]
            for pair, pb, sink_prob in zip(pairs, pbs, sink_probs):
                p_ref[0, pair] = jnp.where(first_col, sink_prob.astype(BF16), pb)
            for pair, out in zip(pairs, outs):
                out_pair = _by_half(out[:CHUNK], out[CHUNK:])
                zb = proj_ref[:, OFF_ZB + pair * LANES:OFF_ZB + (pair + 1) * LANES].astype(F32)
                cat_ref[:, D_GMLP + pair * LANES:D_GMLP + (pair + 1) * LANES] = (
                    out_pair * (zb * _sigmoid(zb))).astype(BF16)
        _after_step(phases, n, nb)

    prev = lambda n, *_: (jnp.maximum(n - 1, 0), 0)
    kv_block = OFF_K // (2 * D_KV)
    return pl.pallas_call(
        body, name="mixer_fwd",
        grid_spec=pltpu.PrefetchScalarGridSpec(
            num_scalar_prefetch=1, grid=(nb,),
            in_specs=[pl.BlockSpec((CHUNK, D_IN), lambda n, *_: (n, 0)),
                      pl.BlockSpec((CHUNK, 2 * D_KV), lambda n, *_: (jnp.maximum(n - 1, 0), kv_block)),
                      pl.BlockSpec((CHUNK, 2 * LANES), lambda n, *_: (n, 0)),
                      pl.BlockSpec((CHUNK, 2 * LANES), prev),
                      pl.BlockSpec((1, D_GMLP), lambda n, *_: (0, 0)),
                      pl.BlockSpec((1, D_GMLP), lambda n, *_: (0, 0)),
                      pl.BlockSpec((GROUPS, CHUNK, CHUNK), lambda n, *_: (0, 0, 0)),
                      pl.BlockSpec((GROUPS, CHUNK, CHUNK), lambda n, *_: (0, 0, 0))] + hosted.in_specs,
            out_specs=[pl.BlockSpec((CHUNK, D_GMLP + D_ATTN), lambda n, *_: (n, 0)),
                       pl.BlockSpec((1, N_PAIRS, 2 * CHUNK, 2 * CHUNK), lambda n, *_: (n, 0, 0, 0))]
            + hosted.out_specs,
            scratch_shapes=hosted.scratch),
        out_shape=[jax.ShapeDtypeStruct((s, D_GMLP + D_ATTN), BF16),
                   jax.ShapeDtypeStruct((nb, N_PAIRS, 2 * CHUNK, 2 * CHUNK), BF16)] + hosted.out_shape,
        compiler_params=_cparams(("arbitrary",)),
    )(sinks, proj, proj, rope, rope, ln_g, ln_b, w_sp, b_sp_rows, *hosted.inputs)


def mixer_bwd(proj, dcat, probs, rope, ln_g, ln_b, w_sp, b_sp_rows, comms=()):
    s = proj.shape[0]
    nb = s // CHUNK
    hosted = _Hosted(comms)

    def body(*refs):
        ((proj_ref, kvp_ref, dcat_ref, p_ref, rope_ref, ropep_ref, lng_ref, lnb_ref, w_ref, b_ref),
         (dproj_ref, dw_ref, db_ref, dlng_ref, dlnb_ref, dsink_ref),
         (pend_ref, pend_kv_ref, dbacc_ref), phases) = hosted.split(refs, 10, 6, 3)
        n = pl.program_id(0)
        _before_step(phases, n, nb + 1)

        @pl.when(n == 0)
        def _():
            dw_ref[...] = jnp.zeros_like(dw_ref)
            dbacc_ref[...] = jnp.zeros_like(dbacc_ref)
            dlng_ref[...] = jnp.zeros_like(dlng_ref)
            dlnb_ref[...] = jnp.zeros_like(dlnb_ref)
            dsink_ref[...] = jnp.zeros_like(dsink_ref)

        @pl.when(n > 0)
        def _():
            dproj_ref[...] = pend_ref[...]

        def flush(dkv_prev):
            @pl.when(n > 0)
            def _():
                dproj_ref[:, OFF_K:OFF_K + 2 * D_KV] = (pend_kv_ref[...] + dkv_prev).astype(BF16)

        @pl.when(n < nb)
        def _():
            kv_ref = proj_ref.at[:, OFF_K:OFF_K + 2 * D_KV]
            keys, vals, (cos_c, sin_c, cos_p, sin_p) = _keys_values(kv_ref, kvp_ref, rope_ref, ropep_ref)
            cos_q, sin_q = cos_c * SCALE, sin_c * SCALE
            first_col = _lane_iota((2 * CHUNK, 2 * CHUNK)) == 0
            lane_row = _lane_iota((1, LANES))
            dsink = jnp.zeros((1, LANES), F32)
            dk_heads, dv_heads = [], []
            for kvh in range(N_KV_HEADS):
                kk2 = _both_halves(keys, kvh).astype(BF16)
                vv2 = _both_halves(vals, kvh).astype(BF16)
                pairs = list(range(kvh * PAIRS_PER_KV, (kvh + 1) * PAIRS_PER_KV))
                k4 = range(PAIRS_PER_KV)
                qm2s = [_stack_heads(_rope(proj_ref[:, OFF_Q + pair * LANES:OFF_Q + (pair + 1) * LANES].astype(F32),
                                           cos_q, sin_q)).astype(BF16) for pair in pairs]
                kept = [p_ref[0, pair] for pair in pairs]
                pbs = [jnp.where(first_col, jnp.zeros_like(kp), kp) for kp in kept]
                ps = [pb.astype(F32) for pb in pbs]
                p_sinks = [kp[:, 0:1].astype(F32) for kp in kept]
                o2s = [_dot(pb, vv2, 1, 0) for pb in pbs]
                zbs = [proj_ref[:, OFF_ZB + pair * LANES:OFF_ZB + (pair + 1) * LANES].astype(F32) for pair in pairs]
                sgs = [_sigmoid(zb) for zb in zbs]
                dybs = [dcat_ref[:, D_GMLP + pair * LANES:D_GMLP + (pair + 1) * LANES].astype(F32) for pair in pairs]
                for i, pair in enumerate(pairs):
                    out_pair = _by_half(o2s[i][:CHUNK], o2s[i][CHUNK:])
                    pend_ref[:, OFF_ZB + pair * LANES:OFF_ZB + (pair + 1) * LANES] = (
                        dybs[i] * out_pair * (sgs[i] * (1.0 + zbs[i] * (1.0 - sgs[i])))).astype(BF16)
                dom2s = [_stack_heads(dybs[i] * (zbs[i] * sgs[i])).astype(BF16) for i in k4]
                dps = [_dot(dom2, vv2, 1, 1) for dom2 in dom2s]
                deltas = [jnp.sum(ps[i] * dps[i], axis=-1, keepdims=True) for i in k4]
                dss = [ps[i] * (dps[i] - deltas[i]) for i in k4]
                for i, pair in enumerate(pairs):
                    dsk = -(p_sinks[i] * deltas[i])
                    dsink = dsink + jnp.where(lane_row == 2 * pair,
                                              jnp.sum(dsk[:CHUNK], axis=0, keepdims=True), 0.0)
                    dsink = dsink + jnp.where(lane_row == 2 * pair + 1,
                                              jnp.sum(dsk[CHUNK:], axis=0, keepdims=True), 0.0)
                dsbs = [ds.astype(BF16) for ds in dss]
                dq2s = [_dot(dsb, kk2, 1, 0) for dsb in dsbs]
                for pair, dq2 in zip(pairs, dq2s):
                    pend_ref[:, OFF_Q + pair * LANES:OFF_Q + (pair + 1) * LANES] = _rope_transposed(
                        _by_half(dq2[:CHUNK], dq2[CHUNK:]), cos_q, sin_q).astype(BF16)
                dkks = [_dot(dsbs[i], qm2s[i], 0, 0) for i in k4]
                dvvs = [_dot(pbs[i], dom2s[i], 0, 0) for i in k4]
                dk_heads.append(_fold_halves((dkks[0] + dkks[1]) + (dkks[2] + dkks[3])))
                dv_heads.append(_fold_halves((dvvs[0] + dvvs[1]) + (dvvs[2] + dvvs[3])))
            dk_rot = _by_half(dk_heads[0], dk_heads[1])
            dv_all = _by_half(dv_heads[0], dv_heads[1])
            dk_p = _rope_transposed(dk_rot[:CHUNK], cos_p, sin_p)
            dk_c = _rope_transposed(dk_rot[CHUNK:], cos_c, sin_c)
            flush(jnp.concatenate([dk_p, dv_all[:CHUNK]], axis=1))
            dsink_ref[...] += dsink
            pend_kv_ref[...] = jnp.concatenate([dk_c, dv_all[CHUNK:]], axis=1)
            xhat, rstd = _layer_norm_parts(proj_ref[:, OFF_V:OFF_V + D_GMLP].astype(F32))
            lng = lng_ref[...]
            vnb = (xhat * lng + lnb_ref[...]).astype(BF16)
            dvn_cols = []
            for g in range(GROUPS):
                cols = slice(g * CHUNK, (g + 1) * CHUNK)
                wm = _masked_spatial(w_ref, g)
                mixed = _dot(wm, vnb[:, cols], 1, 0) + b_ref[g]
                za = proj_ref[:, OFF_ZA + g * CHUNK:OFF_ZA + (g + 1) * CHUNK].astype(F32)
                u = proj_ref[:, OFF_U + g * CHUNK:OFF_U + (g + 1) * CHUNK].astype(F32)
                dya = dcat_ref[:, cols].astype(F32)
                sg = _sigmoid(za)
                sz = za * sg
                pend_ref[:, OFF_U + g * CHUNK:OFF_U + (g + 1) * CHUNK] = (dya * mixed * sz).astype(BF16)
                pend_ref[:, OFF_ZA + g * CHUNK:OFF_ZA + (g + 1) * CHUNK] = (
                    dya * u * mixed * (sg * (1.0 + za * (1.0 - sg)))).astype(BF16)
                dmixed = dya * u * sz
                dmb = dmixed.astype(BF16)
                dbacc_ref[g] += dmixed
                dw_ref[g] += _dot(dmb, vnb[:, cols], 1, 1)
                dvn_cols.append(_dot(wm, dmb, 0, 0))
            dvn = jnp.concatenate(dvn_cols, axis=1)
            dlng_ref[...] += jnp.sum(dvn * xhat, axis=0, keepdims=True)
            dlnb_ref[...] += jnp.sum(dvn, axis=0, keepdims=True)
            dxh = dvn * lng
            dv = rstd * (dxh - jnp.mean(dxh, axis=-1, keepdims=True)
                         - xhat * jnp.mean(dxh * xhat, axis=-1, keepdims=True))
            pend_ref[:, OFF_V:OFF_V + D_GMLP] = dv.astype(BF16)

        @pl.when(n == nb)
        def _():
            flush(jnp.zeros((CHUNK, 2 * D_KV), F32))
            t = lax.broadcasted_iota(jnp.int32, (CHUNK, CHUNK), 0)
            sidx = lax.broadcasted_iota(jnp.int32, (CHUNK, CHUNK), 1)
            lane = _lane_iota((CHUNK, LANES))
            dbt = jnp.zeros((CHUNK, LANES), F32)
            for g in range(GROUPS):
                dw_ref[g] = jnp.where(t >= sidx, dw_ref[g], 0.0)
                dbt = jnp.where(lane == g, jnp.sum(dbacc_ref[g], axis=-1, keepdims=True), dbt)
            db_ref[...] = jnp.transpose(dbt)[:GROUPS, :]

        _after_step(phases, n, nb + 1)

    cur = lambda n: (jnp.minimum(n, nb - 1), 0)
    prev = lambda n: (jnp.clip(n - 1, 0, nb - 1), 0)
    kv_block = OFF_K // (2 * D_KV)
    const2 = lambda n: (0, 0)
    const3 = lambda n: (0, 0, 0)
    return pl.pallas_call(
        body, name="mixer_bwd", grid=(nb + 1,),
        in_specs=[pl.BlockSpec((CHUNK, D_IN), cur),
                  pl.BlockSpec((CHUNK, 2 * D_KV), lambda n: (jnp.clip(n - 1, 0, nb - 1), kv_block)),
                  pl.BlockSpec((CHUNK, D_GMLP + D_ATTN), cur),
                  pl.BlockSpec((1, N_PAIRS, 2 * CHUNK, 2 * CHUNK), lambda n: (jnp.minimum(n, nb - 1), 0, 0, 0)),
                  pl.BlockSpec((CHUNK, 2 * LANES), cur),
                  pl.BlockSpec((CHUNK, 2 * LANES), prev),
                  pl.BlockSpec((1, D_GMLP), const2),
                  pl.BlockSpec((1, D_GMLP), const2),
                  pl.BlockSpec((GROUPS, CHUNK, CHUNK), const3),
                  pl.BlockSpec((GROUPS, CHUNK, CHUNK), const3)] + hosted.in_specs,
        out_specs=[pl.BlockSpec((CHUNK, D_IN), lambda n: (jnp.maximum(n - 1, 0), 0)),
                   pl.BlockSpec((GROUPS, CHUNK, CHUNK), const3),
                   pl.BlockSpec((GROUPS, CHUNK), const2),
                   pl.BlockSpec((1, D_GMLP), const2),
                   pl.BlockSpec((1, D_GMLP), const2),
                   pl.BlockSpec((1, LANES), const2)] + hosted.out_specs,
        scratch_shapes=[pltpu.VMEM((CHUNK, D_IN), BF16), pltpu.VMEM((CHUNK, 2 * D_KV), F32),
                        pltpu.VMEM((GROUPS, CHUNK, CHUNK), F32)] + hosted.scratch,
        out_shape=[jax.ShapeDtypeStruct((s, D_IN), BF16),
                   jax.ShapeDtypeStruct((GROUPS, CHUNK, CHUNK), F32),
                   jax.ShapeDtypeStruct((GROUPS, CHUNK), F32),
                   jax.ShapeDtypeStruct((1, D_GMLP), F32),
                   jax.ShapeDtypeStruct((1, D_GMLP), F32),
                   jax.ShapeDtypeStruct((1, LANES), F32)] + hosted.out_shape,
        compiler_params=_cparams(("arbitrary",)),
    )(proj, proj, dcat, probs, rope, rope, ln_g, ln_b, w_sp, b_sp_rows, *hosted.inputs)


def _adamw_math(w, g, m, v):
    m = ADAM_B1 * m + (1.0 - ADAM_B1) * g
    v = ADAM_B2 * v + (1.0 - ADAM_B2) * (g * g)
    m_hat = m / (1.0 - ADAM_B1 ** ADAM_STEP)
    v_hat = v / (1.0 - ADAM_B2 ** ADAM_STEP)
    delta = -ADAM_LR * (m_hat / (jnp.sqrt(v_hat) + ADAM_EPS) + ADAM_WD * w)
    return delta, m, v


def adamw_shard(terms, w, m, v, name, after=None):
    r, c = w.shape
    tr = _tile(r, (336, 128, 8))
    n_terms = len(terms)
    order_only = [] if after is None else [after]

    def body(*refs):
        w_ref, m_ref, v_ref, g_ref, d_ref, nm_ref, nv_ref = refs[n_terms:n_terms + 3] + refs[-4:]
        g = None
        for ref, (_, slots) in zip(refs[:n_terms], terms):
            for k in range(slots):
                part = ref[k].astype(F32)
                g = part if g is None else g + part
        g_ref[...] = g
        d_ref[...], nm_ref[...], nv_ref[...] = _adamw_math(w_ref[...], g, m_ref[...], v_ref[...])

    spec = pl.BlockSpec((tr, c), lambda i: (i, 0))
    return pl.pallas_call(
        body, name=name, grid=(r // tr,),
        in_specs=[pl.BlockSpec((slots, tr, c), lambda i: (0, i, 0)) for _, slots in terms] + [spec] * 3
        + [ANY] * len(order_only),
        out_specs=[spec] * 4, out_shape=[jax.ShapeDtypeStruct((r, c), F32)] * 4,
        compiler_params=_cparams(("arbitrary",)),
    )(*[a for a, _ in terms], w, m, v, *order_only)


def adamw_small(gathered, lane_windows, params):
    n_par = len(params)

    def body(*refs):
        g_refs = refs[:n_par + 1]
        wmv_refs = refs[n_par + 1:4 * n_par + 1]
        out_refs = refs[4 * n_par + 1:]

        def total(ref):
            acc = ref[0]
            for dev in range(1, N_DEV):
                acc = acc + ref[dev]
            return acc

        for i in range(n_par):
            w_ref, m_ref, v_ref = wmv_refs[3 * i:3 * i + 3]
            g = total(g_refs[i])
            if lane_windows[i] is not None:
                start, size = lane_windows[i]
                g = g[..., start:start + size]
            delta, new_m, new_v = _adamw_math(w_ref[...], g, m_ref[...], v_ref[...])
            for ref, val in zip(out_refs[4 * i:4 * i + 4], (g, delta, new_m, new_v)):
                ref[...] = val
        out_refs[4 * n_par][...] = total(g_refs[n_par])

    flat = [a for wmv in params for a in wmv]
    out_shape = [jax.ShapeDtypeStruct(w.shape, F32) for (w, _, _) in params for _ in range(4)]
    out_shape.append(jax.ShapeDtypeStruct(gathered[-1].shape[1:], F32))
    outs = pl.pallas_call(body, name="adamw_small", out_shape=out_shape, compiler_params=_cparams())(*gathered, *flat)
    return [tuple(outs[4 * i:4 * i + 4]) for i in range(n_par)], outs[-1]


def kernel(x, positions, g_pre, w_in, b_qkv, ln_v_g, ln_v_b, w_spatial, b_spatial, attn_sinks, w_out, g_post, loss_target, m_g_pre, m_w_in, m_b_qkv, m_ln_v_g, m_ln_v_b, m_w_spatial, m_b_spatial, m_attn_sinks, m_w_out, m_g_post, v_g_pre, v_w_in, v_b_qkv, v_ln_v_g, v_ln_v_b, v_w_spatial, v_b_spatial, v_attn_sinks, v_w_out, v_g_post):
    x2, target = x[0], loss_target[0]
    seq = x2.shape[0]

    wt_shard = w_in[0].T.astype(BF16)
    wo_shard = w_out[0].astype(BF16)
    pos_col = positions.reshape(seq, 1)
    half = HEAD_DIM // 2
    inv_freq = ROPE_THETA ** (-jnp.arange(half, dtype=F32) * (2.0 / HEAD_DIM))
    freq = jnp.tile(inv_freq, LANES // half).reshape(1, LANES)
    sign = jnp.tile(jnp.concatenate([-jnp.ones((half,), F32), jnp.ones((half,), F32)]), LANES // HEAD_DIM)
    sign = sign.reshape(1, LANES)
    bias = jnp.concatenate([jnp.zeros((1, OFF_Q), F32), b_qkv, jnp.zeros((1, D_ATTN), F32)], axis=1)
    proj, h, rope, wt = in_proj_gather(x2, pos_col, freq, sign, g_pre, wt_shard, bias)

    b_rows = jnp.broadcast_to(b_spatial[0][:, :, None], (GROUPS, CHUNK, CHUNK))
    sinks = attn_sinks[0]
    cat, probs, wo = mixer_fwd(proj, rope, ln_v_g, ln_v_b, w_spatial[0], b_rows, sinks,
                               comms=[gather_comm([wo_shard])])
    dy, dout, d_g_post, loss_part = out_proj_loss(cat, wo, x2, target, g_post)

    dcat = matmul_nt(dy, wo, "out_proj_bwd")
    d_wo, _ = matmul_tn(cat, dy, 512, "w_out_grad")
    dproj, d_w_sp, d_b_sp, d_ln_g, d_ln_b, d_sinks, parts_wo = mixer_bwd(
        proj, dcat, probs, rope, ln_v_g, ln_v_b, w_spatial[0], b_rows, comms=[scatter_comm([d_wo])])
    small_parts = [d_ln_g, d_ln_b, d_w_sp, d_b_sp, d_sinks, d_g_post, loss_part]
    d_wt, colsum, *landed = matmul_tn(dproj, h, 768, "w_in_grad", comms=[gather_comm(small_parts, stack=True)])

    sum_wt = pair_exchange_sum(d_wt, "grad_pair_sum_w_in")
    started = chips_exchange_start(sum_wt)
    grad_x, d_g_pre = in_proj_bwd(dproj, wt, x2, g_pre, dout, after=started[-1])
    sum_wt, far_wt = chips_exchange_wait(started, d_g_pre)
    late = jnp.concatenate([d_g_pre, colsum], axis=1)
    late_started, late_copy = gather_start(late)

    wo_out = adamw_shard([(parts_wo, N_DEV)], w_out[0], m_w_out[0], v_w_out[0], "adamw_w_out", after=late_started[-1])
    wt_out = adamw_shard([(sum_wt, 1), (far_wt, 3)], w_in[0].T, m_w_in[0].T, v_w_in[0].T, "adamw_w_in",
                         after=wo_out[0])
    landed_late = gather_wait(late_started, late_copy, wt_out[0])
    my_index = _index(*_my_place())
    all_late = lax.dynamic_update_slice(landed_late, late[None], (my_index, 0, 0))
    gathered = [all_late, all_late] + landed
    windows = [(0, D_MODEL), (D_MODEL + OFF_Q, D_QKV), None, None, None, None, (0, N_Q_HEADS), None]
    small = [(g_pre, m_g_pre, v_g_pre), (b_qkv, m_b_qkv, v_b_qkv), (ln_v_g, m_ln_v_g, v_ln_v_g),
             (ln_v_b, m_ln_v_b, v_ln_v_b), (w_spatial[0], m_w_spatial[0], v_w_spatial[0]),
             (b_spatial[0], m_b_spatial[0], v_b_spatial[0]), (attn_sinks, m_attn_sinks, v_attn_sinks),
             (g_post, m_g_post, v_g_post)]
    small_out, loss_row = adamw_small(gathered, windows, small)
    lead = [False, False, False, False, True, True, False, False]
    small_out = [tuple(a[None] if ld else a for a in leaf) for leaf, ld in zip(small_out, lead)]

    def leaves(k):
        gp, bq, lg, lb, ws, bs, sk, gpo = (leaf[k] for leaf in small_out)
        return [gp, wt_out[k].T[None], bq, lg, lb, ws, bs, sk, wo_out[k][None], gpo]

    return (loss_row[0, 0], grad_x[None], *leaves(0), *leaves(1), *leaves(2), *leaves(3))
```

```python
import functools

import jax
import jax.numpy as jnp
from jax import lax
from jax.experimental import pallas as pl
from jax.experimental.pallas import tpu as pltpu

F32 = jnp.float32
BF16 = jnp.bfloat16

D_MODEL = 2048
D_GMLP = 1024
D_ATTN = 1024
CHUNK = 128
GROUPS = 8
HEAD_DIM = 64
N_Q_HEADS = 16
N_KV_HEADS = 2
D_KV = N_KV_HEADS * HEAD_DIM
D_IN = 3 * D_GMLP + D_ATTN + 2 * D_KV + D_ATTN
OFF_U, OFF_V, OFF_ZA = 0, D_GMLP, 2 * D_GMLP
OFF_Q = 3 * D_GMLP
OFF_K = OFF_Q + D_ATTN
OFF_VA = OFF_K + D_KV
OFF_ZB = OFF_VA + D_KV
D_QKV = D_ATTN + 2 * D_KV
ROPE_THETA = 10000.0
EPS = 1e-6
SCALE = HEAD_DIM ** -0.5
NEG = -1e30
N_PAIRS = N_Q_HEADS // 2
PAIRS_PER_KV = N_PAIRS // N_KV_HEADS

ADAM_LR = 0.001
ADAM_B1 = 0.9
ADAM_B2 = 0.999
ADAM_EPS = 1e-08
ADAM_WD = 0.01
ADAM_STEP = 10

N_DEV = 8
LANES = 128
VMEM_LIMIT = 56 * 1024 * 1024
IN_PROJ_VMEM_LIMIT = 61 * 1024 * 1024

MESH = pl.DeviceIdType.MESH
ANY = pl.BlockSpec(memory_space=pl.ANY)


def _cparams(sem=None):
    return pltpu.CompilerParams(dimension_semantics=sem, vmem_limit_bytes=VMEM_LIMIT)


def _tile(n, prefs):
    for t in prefs:
        if n % t == 0:
            return t
    return n


def _sigmoid(z):
    return 1.0 / (1.0 + jnp.exp(-z))


def _dot(a, b, ca, cb):
    return lax.dot_general(a, b, (((ca,), (cb,)), ((), ())), preferred_element_type=F32)


def _my_place():
    return lax.axis_index("x"), lax.axis_index("y"), lax.axis_index("c")


def _chip_of(x, y, r):
    return (x ^ (r & 1), y ^ (r >> 1))


def _peer(x, y, c, k):
    return (x ^ (k >> 2), y ^ ((k >> 1) & 1), c ^ (k & 1))


def _index(px, py, pc):
    return 4 * px + 2 * py + pc


class _Comm:
    def __init__(self, inputs, out_shape, scratch, bind):
        self.inputs, self.out_shape, self.scratch, self.bind = list(inputs), list(out_shape), list(scratch), bind


def gather_comm(shards, stack=False):
    n_arr = len(shards)

    def bind(ins, outs, sems):
        send_sems, recv_sems, local_sems = sems
        x, y, c = _my_place()
        me, sibling = (x, y, c), (x, y, 1 - c)
        chips = [_chip_of(x, y, r) for r in (1, 2, 3)]

        def rows(a, px, py, pc):
            d = _index(px, py, pc)
            if stack:
                return outs[a].at[d]
            m = shards[a].shape[0]
            return outs[a].at[pl.ds(pl.multiple_of(d * m, 8), m), :]

        def copy(a, k, block, to, src=None):
            return pltpu.make_async_remote_copy(
                src_ref=rows(a, *block) if src is None else src, dst_ref=rows(a, *block),
                send_sem=send_sems.at[a * 7 + k], recv_sem=recv_sems.at[a * 7 + k],
                device_id=to, device_id_type=MESH)

        def mine(a):
            return pltpu.make_async_copy(ins[a], rows(a, *me), local_sems.at[a])

        def own_sends(a):
            return ([copy(a, 0, me, sibling, src=ins[a])]
                    + [copy(a, 1 + j, me, (*chip, c), src=ins[a]) for j, chip in enumerate(chips)])

        def start():
            for a in range(n_arr):
                mine(a).start()
                for cp in own_sends(a):
                    cp.start()

        def relay():
            for j, chip in enumerate(chips):
                for a in range(n_arr):
                    copy(a, 1 + j, (*chip, c), me).wait_recv()
                    copy(a, 4 + j, (*chip, c), sibling).start()

        def finish():
            for a in range(n_arr):
                copy(a, 0, sibling, me).wait_recv()
                for j, chip in enumerate(chips):
                    copy(a, 4 + j, (*chip, 1 - c), me).wait_recv()
                    copy(a, 4 + j, (*chip, c), sibling).wait_send()
                for cp in own_sends(a):
                    cp.wait_send()
                mine(a).wait()

        return start, relay, finish

    def gathered(s):
        return (N_DEV, *s.shape) if stack else (N_DEV * s.shape[0], s.shape[1])

    return _Comm(shards, [jax.ShapeDtypeStruct(gathered(s), s.dtype) for s in shards],
                 [pltpu.SemaphoreType.DMA((7 * n_arr,)), pltpu.SemaphoreType.DMA((7 * n_arr,)),
                  pltpu.SemaphoreType.DMA((n_arr,))], bind)


def scatter_comm(parts):
    n_arr = len(parts)

    def bind(ins, outs, sems):
        send_sems, recv_sems, local_sems = sems
        x, y, c = _my_place()
        my_index = _index(x, y, c)

        def block(a, d):
            m = parts[a].shape[0] // N_DEV
            return ins[a].at[pl.ds(pl.multiple_of(d * m, 16), m), :]

        def copy(a, k, slot):
            peer = _peer(x, y, c, k)
            return pltpu.make_async_remote_copy(
                src_ref=block(a, _index(*peer)), dst_ref=outs[a].at[slot],
                send_sem=send_sems.at[a * 7 + k - 1], recv_sem=recv_sems.at[a * 7 + k - 1],
                device_id=peer, device_id_type=MESH)

        def mine(a):
            return pltpu.make_async_copy(block(a, my_index), outs[a].at[my_index], local_sems.at[a])

        def start():
            for a in range(n_arr):
                mine(a).start()
                for k in range(1, 8):
                    copy(a, k, my_index).start()

        def finish():
            for a in range(n_arr):
                for k in range(1, 8):
                    copy(a, k, _index(*_peer(x, y, c, k))).wait_recv()
                    copy(a, k, my_index).wait_send()
                mine(a).wait()

        return start, (lambda: None), finish

    return _Comm(parts, [jax.ShapeDtypeStruct((N_DEV, p.shape[0] // N_DEV, p.shape[1]), p.dtype) for p in parts],
                 [pltpu.SemaphoreType.DMA((7 * n_arr,)), pltpu.SemaphoreType.DMA((7 * n_arr,)),
                  pltpu.SemaphoreType.DMA((n_arr,))], bind)


def run_comm(comm, name):
    n_in, n_out = len(comm.inputs), len(comm.out_shape)

    def body(*refs):
        start, relay, finish = comm.bind(refs[:n_in], refs[n_in:n_in + n_out], refs[n_in + n_out:])
        start()
        relay()
        finish()

    outs = pl.pallas_call(body, name=name, out_shape=comm.out_shape, in_specs=[ANY] * n_in,
                          out_specs=[ANY] * n_out, scratch_shapes=comm.scratch)(*comm.inputs)
    return list(outs)


def _chip_copy(r, src_ref, land_ref, send_sem, recv_sem):
    x, y, c = _my_place()
    return pltpu.make_async_remote_copy(src_ref=src_ref.at[r], dst_ref=land_ref.at[r - 1], send_sem=send_sem,
                                        recv_sem=recv_sem, device_id=(*_chip_of(x, y, r), c), device_id_type=MESH)


def chips_exchange_start(sums):
    def body(src_ref, land_ref, s1, s2, s3, r1, r2, r3, src_thru, land_thru, token):
        del src_thru, land_thru
        for r, send_sem, recv_sem in ((1, s1, r1), (2, s2, r2), (3, s3, r3)):
            _chip_copy(r, src_ref, land_ref, send_sem, recv_sem).start()
        token[...] = jnp.zeros_like(token)

    land = lax.empty((3,) + sums.shape[1:], sums.dtype)
    sem = pltpu.SemaphoreType.DMA(())
    hbm = pl.BlockSpec(memory_space=pltpu.HBM)
    sem_spec = pl.BlockSpec(memory_space=pltpu.SEMAPHORE)
    return pl.pallas_call(
        body, name="grad_exchange_chips_start",
        out_shape=(sem,) * 6 + (pltpu.HBM(sums.shape, sums.dtype), pltpu.HBM(land.shape, land.dtype),
                                jax.ShapeDtypeStruct((8, LANES), F32)),
        in_specs=(hbm, hbm), out_specs=(sem_spec,) * 6 + (hbm, hbm, pl.BlockSpec(memory_space=pltpu.VMEM)),
        input_output_aliases={0: 6, 1: 7},
        compiler_params=pltpu.CompilerParams(has_side_effects=pltpu.SideEffectType.DATAFLOW_SIDE_EFFECTING),
    )(pltpu.with_memory_space_constraint(sums, pltpu.HBM), pltpu.with_memory_space_constraint(land, pltpu.HBM))


def chips_exchange_wait(started, after):
    s1, s2, s3, r1, r2, r3, src_thru, land_thru, _ = started

    def body(src_ref, land_ref, s1, s2, s3, r1, r2, r3, after_ref, src_out, land_out):
        del after_ref, src_out, land_out
        for r, send_sem, recv_sem in ((1, s1, r1), (2, s2, r2), (3, s3, r3)):
            copy = _chip_copy(r, src_ref, land_ref, send_sem, recv_sem)
            copy.wait_send()
            copy.wait_recv()

    hbm = pl.BlockSpec(memory_space=pltpu.HBM)
    sem_spec = pl.BlockSpec(memory_space=pltpu.SEMAPHORE)
    return pl.pallas_call(
        body, name="grad_exchange_chips_wait",
        out_shape=(pltpu.HBM(src_thru.shape, src_thru.dtype), pltpu.HBM(land_thru.shape, land_thru.dtype)),
        in_specs=(hbm, hbm) + (sem_spec,) * 6 + (pl.BlockSpec(memory_space=pl.ANY),), out_specs=(hbm, hbm),
        input_output_aliases={0: 0, 1: 1},
        compiler_params=pltpu.CompilerParams(has_side_effects=pltpu.SideEffectType.DATAFLOW_SIDE_EFFECTING),
    )(src_thru, land_thru, s1, s2, s3, r1, r2, r3, after)


def gather_start(shard):
    def copy(k, src_ref, land_ref, send_sem, recv_sem):
        x, y, c = _my_place()
        return pltpu.make_async_remote_copy(src_ref=src_ref, dst_ref=land_ref.at[_index(x, y, c)], send_sem=send_sem,
                                            recv_sem=recv_sem, device_id=_peer(x, y, c, k), device_id_type=MESH)

    def start_body(src_ref, land_ref, *rest):
        sems, token = rest[:14], rest[16]
        for k in range(1, 8):
            copy(k, src_ref, land_ref, sems[k - 1], sems[7 + k - 1]).start()
        token[...] = jnp.zeros_like(token)

    land = lax.empty((N_DEV,) + shard.shape, shard.dtype)
    sem = pltpu.SemaphoreType.DMA(())
    hbm = pl.BlockSpec(memory_space=pltpu.HBM)
    sem_spec = pl.BlockSpec(memory_space=pltpu.SEMAPHORE)
    started = pl.pallas_call(
        start_body, name="allgather_late_grads_start",
        out_shape=(sem,) * 14 + (pltpu.HBM(shard.shape, shard.dtype), pltpu.HBM(land.shape, land.dtype),
                                 jax.ShapeDtypeStruct((8, LANES), F32)),
        in_specs=(hbm, hbm), out_specs=(sem_spec,) * 14 + (hbm, hbm, pl.BlockSpec(memory_space=pltpu.VMEM)),
        input_output_aliases={0: 14, 1: 15},
        compiler_params=pltpu.CompilerParams(has_side_effects=pltpu.SideEffectType.DATAFLOW_SIDE_EFFECTING),
    )(pltpu.with_memory_space_constraint(shard, pltpu.HBM), pltpu.with_memory_space_constraint(land, pltpu.HBM))
    return started, copy


def gather_wait(started, copy, after):
    sems, src_thru, land_thru = started[:14], started[14], started[15]

    def wait_body(src_ref, land_ref, *rest):
        for k in range(1, 8):
            cp = copy(k, src_ref, land_ref, rest[k - 1], rest[7 + k - 1])
            cp.wait_send()
            cp.wait_recv()

    hbm = pl.BlockSpec(memory_space=pltpu.HBM)
    sem_spec = pl.BlockSpec(memory_space=pltpu.SEMAPHORE)
    return pl.pallas_call(
        wait_body, name="allgather_late_grads_wait",
        out_shape=(pltpu.HBM(src_thru.shape, src_thru.dtype), pltpu.HBM(land_thru.shape, land_thru.dtype)),
        in_specs=(hbm, hbm) + (sem_spec,) * 14 + (pl.BlockSpec(memory_space=pl.ANY),), out_specs=(hbm, hbm),
        input_output_aliases={0: 0, 1: 1},
        compiler_params=pltpu.CompilerParams(has_side_effects=pltpu.SideEffectType.DATAFLOW_SIDE_EFFECTING),
    )(src_thru, land_thru, *sems, after)[1]


class _Hosted:
    def __init__(self, comms):
        self.comms = list(comms)
        self.inputs = [a for cm in self.comms for a in cm.inputs]
        self.out_shape = [s for cm in self.comms for s in cm.out_shape]
        self.scratch = [s for cm in self.comms for s in cm.scratch]
        self.in_specs = [ANY] * len(self.inputs)
        self.out_specs = [ANY] * len(self.out_shape)

    def split(self, refs, n_in, n_out, n_scratch):
        ni, no = len(self.inputs), len(self.out_shape)
        ins, rest = refs[:n_in], refs[n_in:]
        c_ins, rest = rest[:ni], rest[ni:]
        outs, rest = rest[:n_out], rest[n_out:]
        c_outs, rest = rest[:no], rest[no:]
        scratch, c_sems = rest[:n_scratch], rest[n_scratch:]
        phases = []
        for cm in self.comms:
            a, b, s = len(cm.inputs), len(cm.out_shape), len(cm.scratch)
            phases.append(cm.bind(c_ins[:a], c_outs[:b], c_sems[:s]))
            c_ins, c_outs, c_sems = c_ins[a:], c_outs[b:], c_sems[s:]
        return ins, outs, scratch, phases


def _before_step(phases, step, n_steps):
    if not phases:
        return

    @pl.when(step == 0)
    def _():
        for start, _, _ in phases:
            start()

    @pl.when(step == n_steps // 2)
    def _():
        for _, relay, _ in phases:
            relay()


def _after_step(phases, step, n_steps):
    if not phases:
        return

    @pl.when(step == n_steps - 1)
    def _():
        for _, _, finish in phases:
            finish()


def pair_exchange_sum(part, name):
    m, n = part.shape[0] // N_DEV, part.shape[1]

    def body(part_ref, out_ref, got, mine, summed, send_sems, recv_sems, in_sems, out_sems):
        x, y, c = _my_place()

        def rows(r, core):
            owner = _index(*_chip_of(x, y, r), core)
            return part_ref.at[pl.ds(pl.multiple_of(owner * m, 16), m), :]

        def to_sibling(r):
            return pltpu.make_async_remote_copy(src_ref=rows(r, 1 - c), dst_ref=got.at[r], send_sem=send_sems.at[r],
                                                recv_sem=recv_sems.at[r], device_id=(x, y, 1 - c), device_id_type=MESH)

        def fetch(r):
            return pltpu.make_async_copy(rows(r, c), mine.at[r % 2], in_sems.at[r % 2])

        def put(r):
            return pltpu.make_async_copy(summed.at[r % 2], out_ref.at[r], out_sems.at[r % 2])

        for r in range(4):
            to_sibling(r).start()
        fetch(0).start()
        for r in range(4):
            if r + 1 < 4:
                fetch(r + 1).start()
            fetch(r).wait()
            to_sibling(r).wait_recv()
            if r >= 2:
                put(r - 2).wait()
            summed[r % 2] = (mine[r % 2].astype(F32) + got[r].astype(F32)).astype(summed.dtype)
            put(r).start()
        for r in (2, 3):
            put(r).wait()
        for r in range(4):
            to_sibling(r).wait_send()

    return pl.pallas_call(
        body, name=name, out_shape=jax.ShapeDtypeStruct((4, m, n), part.dtype),
        in_specs=[ANY], out_specs=ANY,
        scratch_shapes=[pltpu.VMEM((4, m, n), part.dtype), pltpu.VMEM((2, m, n), part.dtype),
                        pltpu.VMEM((2, m, n), part.dtype), pltpu.SemaphoreType.DMA((4,)),
                        pltpu.SemaphoreType.DMA((4,)), pltpu.SemaphoreType.DMA((2,)), pltpu.SemaphoreType.DMA((2,))],
        compiler_params=_cparams(),
    )(part)


def in_proj_gather(x, pos_col, freq, sign, g_pre, wt_shard, bias):
    s, d = x.shape
    tm = _tile(s, (512, 256, 128))
    nt = s // tm
    tc = _tile(s, (256, 128))
    nc = s // tc
    m = wt_shard.shape[0]
    half = D_IN // 2
    xi = lax.axis_index("x")
    order = jnp.stack([xi, 1 - xi]).astype(jnp.int32)

    def body(order_ref, x_hbm, pos_hbm, freq_ref, sign_ref, g_ref, b_ref, shard_ref,
             proj_ref, h_hbm, rope_hbm, wt_ref,
             w_vmem, h_vmem, xbuf, posbuf, ropebuf, send_sems, recv_sems, local_sems, in_sems, out_sems):
        del order_ref
        p, i = pl.program_id(0), pl.program_id(1)
        xx, yy, cc = _my_place()
        me, sibling = (xx, yy, cc), (xx, yy, 1 - cc)
        chips = [_chip_of(xx, yy, r) for r in (1, 2, 3)]

        def rows(px, py, pc):
            return wt_ref.at[pl.ds(pl.multiple_of(_index(px, py, pc) * m, 16), m), :]

        def copy(k, block, to, src=None):
            return pltpu.make_async_remote_copy(
                src_ref=rows(*block) if src is None else src, dst_ref=rows(*block),
                send_sem=send_sems.at[k], recv_sem=recv_sems.at[k], device_id=to, device_id_type=MESH)

        def mine():
            return pltpu.make_async_copy(shard_ref, rows(*me), local_sems.at[0])

        def to_sibling():
            return copy(0, me, sibling, src=shard_ref)

        def to_chip(j):
            return copy(1 + j, me, (*chips[j], cc), src=shard_ref)

        def relay(j):
            copy(1 + j, (*chips[j], cc), me).wait_recv()
            copy(4 + j, (*chips[j], cc), sibling).start()

        def relayed(j):
            copy(4 + j, (*chips[j], 1 - cc), me).wait_recv()

        def load_half(which, slot):
            rows_of_half = wt_ref.at[pl.ds(pl.multiple_of(which * half, 16), half), :]
            load = pltpu.make_async_copy(rows_of_half, w_vmem.at[slot], local_sems.at[1 + slot])
            load.start()
            load.wait()

        def piece(ref, c):
            return ref.at[pl.ds(c * tc, tc), :]

        def fetch(c):
            return (pltpu.make_async_copy(piece(x_hbm, c), xbuf.at[c % 2], in_sems.at[c % 2]),
                    pltpu.make_async_copy(piece(pos_hbm, c), posbuf.at[c % 2], in_sems.at[2 + c % 2]))

        def put(c):
            return (pltpu.make_async_copy(piece(h_vmem, c), piece(h_hbm, c), out_sems.at[c % 2]),
                    pltpu.make_async_copy(ropebuf.at[c % 2], piece(rope_hbm, c), out_sems.at[2 + c % 2]))

        def prologue():
            for cp in fetch(0):
                cp.start()
            for c in range(nc):
                if c + 1 < nc:
                    for cp in fetch(c + 1):
                        cp.start()
                for cp in fetch(c):
                    cp.wait()
                if c >= 2:
                    for cp in put(c - 2):
                        cp.wait()
                xv = xbuf[c % 2]
                r = lax.rsqrt(jnp.mean(xv * xv, axis=-1, keepdims=True) + EPS)
                h_vmem[c * tc:(c + 1) * tc, :] = (xv * r * g_ref[...]).astype(BF16)
                ang = posbuf[c % 2].astype(F32) * freq_ref[...]
                ropebuf[c % 2, :, :LANES] = jnp.cos(ang)
                ropebuf[c % 2, :, LANES:] = jnp.sin(ang) * sign_ref[...]
                for cp in put(c):
                    cp.start()
            for c in range(max(nc - 2, 0), nc):
                for cp in put(c):
                    cp.wait()

        @pl.when(jnp.logical_and(p == 0, i == 0))
        def _():
            mine().start()
            to_sibling().start()
            to_chip(1).start()
            to_chip(0).start()
            prologue()
            copy(0, sibling, me).wait_recv()
            relay(1)
            relayed(1)
            mine().wait()
            to_chip(1).wait_send()
            to_chip(0).wait_send()
            to_chip(2).start()
            load_half(xx, 0)

        @pl.when(jnp.logical_and(p == 1, i == 0))
        def _():
            relayed(0)
            relayed(2)
            load_half(1 - xx, 1)

        def project(slot):
            hb = h_vmem[pl.ds(pl.multiple_of(i * tm, tm), tm), :]
            proj_ref[...] = (_dot(hb, w_vmem[slot], 1, 1) + b_ref[...]).astype(BF16)

        @pl.when(p == 0)
        def _():
            project(0)

        @pl.when(p == 1)
        def _():
            project(1)

        @pl.when(jnp.logical_and(p == 0, i == 1))
        def _():
            relay(0)

        @pl.when(jnp.logical_and(p == 0, i == nt - 1))
        def _():
            relay(2)

        @pl.when(jnp.logical_and(p == 1, i == nt - 1))
        def _():
            to_sibling().wait_send()
            to_chip(2).wait_send()
            for j in range(3):
                copy(4 + j, (*chips[j], cc), sibling).wait_send()

    const = lambda p, i, o: (0, 0)
    return pl.pallas_call(
        body, name="in_proj_gather",
        grid_spec=pltpu.PrefetchScalarGridSpec(
            num_scalar_prefetch=1, grid=(2, nt),
            in_specs=[ANY, ANY,
                      pl.BlockSpec((1, LANES), const),
                      pl.BlockSpec((1, LANES), const),
                      pl.BlockSpec((1, d), const),
                      pl.BlockSpec((1, half), lambda p, i, o: (0, o[p])),
                      ANY],
            out_specs=[pl.BlockSpec((tm, half), lambda p, i, o: (i, o[p])), ANY, ANY, ANY],
            scratch_shapes=[pltpu.VMEM((2, half, d), BF16), pltpu.VMEM((s, d), BF16),
                            pltpu.VMEM((2, tc, d), F32), pltpu.VMEM((2, tc, 1), jnp.int32),
                            pltpu.VMEM((2, tc, 2 * LANES), F32),
                            pltpu.SemaphoreType.DMA((7,)), pltpu.SemaphoreType.DMA((7,)),
                            pltpu.SemaphoreType.DMA((3,)), pltpu.SemaphoreType.DMA((4,)),
                            pltpu.SemaphoreType.DMA((4,))]),
        out_shape=[jax.ShapeDtypeStruct((s, D_IN), BF16), jax.ShapeDtypeStruct((s, d), BF16),
                   jax.ShapeDtypeStruct((s, 2 * LANES), F32), jax.ShapeDtypeStruct((D_IN, d), BF16)],
        compiler_params=pltpu.CompilerParams(dimension_semantics=("arbitrary", "arbitrary"),
                                             vmem_limit_bytes=IN_PROJ_VMEM_LIMIT),
    )(order, x, pos_col, freq, sign, g_pre, bias, wt_shard)


def out_proj_loss(cat, w_out, x, target, g_post):
    s, d = x.shape
    tm = _tile(s, (256, 128))
    kc = _tile(d, (512, 128))
    pieces = w_out.shape[0] // kc

    steps = s // tm
    ring = 3

    def body(cat_ref, w_hbm, x_hbm, t_hbm, g_ref, dy_ref, dout_ref, dg_ref, loss_ref, w_ref, w_sems, xbuf, tbuf, in_sems):
        step = pl.program_id(0)
        slot = step % ring
        x_ref, t_ref = xbuf.at[slot], tbuf.at[slot]

        def w_load(j):
            return pltpu.make_async_copy(w_hbm.at[j * kc:(j + 1) * kc, :], w_ref.at[j * kc:(j + 1) * kc, :], w_sems.at[j])

        def fetch(at_step, into):
            rows = pl.ds(pl.multiple_of(at_step * tm, tm), tm)
            return (pltpu.make_async_copy(x_hbm.at[rows, :], xbuf.at[into], in_sems.at[into]),
                    pltpu.make_async_copy(t_hbm.at[rows, :], tbuf.at[into], in_sems.at[ring + into]))

        def arrived():
            for cp in fetch(step, slot):
                cp.wait()

        @pl.when(step == 0)
        def _():
            for j in range(pieces):
                w_load(j).start()
            for ahead in range(min(ring - 1, steps)):
                for cp in fetch(ahead, ahead):
                    cp.start()

        @pl.when(step + ring - 1 < steps)
        def _():
            for cp in fetch(step + ring - 1, (step + ring - 1) % ring):
                cp.start()

        @pl.when(step == 0)
        def _():
            dg_ref[...] = jnp.zeros_like(dg_ref)
            loss_ref[...] = jnp.zeros_like(loss_ref)
            ys = [None] * (tm // CHUNK)
            for j in range(pieces):
                w_load(j).wait()
                for c in range(tm // CHUNK):
                    part = _dot(cat_ref[c * CHUNK:(c + 1) * CHUNK, j * kc:(j + 1) * kc], w_ref[j * kc:(j + 1) * kc, :], 1, 0)
                    ys[c] = part if ys[c] is None else ys[c] + part
            arrived()
            loss_and_back(ys, x_ref, t_ref, g_ref, dy_ref, dout_ref, dg_ref, loss_ref)

        @pl.when(step > 0)
        def _():
            arrived()
            ys = [_dot(cat_ref[c0:c0 + CHUNK, :], w_ref[...], 1, 0) for c0 in range(0, tm, CHUNK)]
            loss_and_back(ys, x_ref, t_ref, g_ref, dy_ref, dout_ref, dg_ref, loss_ref)

    def loss_and_back(ys, x_ref, t_ref, g_ref, dy_ref, dout_ref, dg_ref, loss_ref):
        g = g_ref[...]
        for c0 in range(0, tm, CHUNK):
            rows = slice(c0, c0 + CHUNK)
            yv = ys[c0 // CHUNK]
            r = lax.rsqrt(jnp.mean(yv * yv, axis=-1, keepdims=True) + EPS)
            nrm = yv * r
            err = x_ref[rows, :] + nrm * g - t_ref[rows, :]
            loss_ref[...] += 0.5 * jnp.sum(jnp.sum(err * err, axis=-1, keepdims=True), axis=0, keepdims=True) / d
            dout = err * (1.0 / d)
            dout_ref[rows, :] = dout
            dg_ref[...] += jnp.sum(dout * nrm, axis=0, keepdims=True)
            dn = dout * g
            dy = r * (dn - nrm * jnp.mean(dn * nrm, axis=-1, keepdims=True))
            dy_ref[rows, :] = dy.astype(BF16)

    return pl.pallas_call(
        body, name="out_proj_loss", grid=(steps,),
        in_specs=[pl.BlockSpec((tm, d), lambda i: (i, 0)), ANY, ANY, ANY, pl.BlockSpec((1, d), lambda i: (0, 0))],
        out_specs=[pl.BlockSpec((tm, d), lambda i: (i, 0)),
                   pl.BlockSpec((tm, d), lambda i: (i, 0)),
                   pl.BlockSpec((1, d), lambda i: (0, 0)),
                   pl.BlockSpec((1, LANES), lambda i: (0, 0))],
        out_shape=[jax.ShapeDtypeStruct((s, d), BF16), jax.ShapeDtypeStruct((s, d), F32),
                   jax.ShapeDtypeStruct((1, d), F32), jax.ShapeDtypeStruct((1, LANES), F32)],
        scratch_shapes=[pltpu.VMEM(w_out.shape, w_out.dtype), pltpu.SemaphoreType.DMA((pieces,)),
                        pltpu.VMEM((ring, tm, d), F32), pltpu.VMEM((ring, tm, d), F32),
                        pltpu.SemaphoreType.DMA((2 * ring,))],
        compiler_params=_cparams(("arbitrary",)),
    )(cat, w_out, x, target, g_post)


def matmul_nt(a, b, name):
    m, k = a.shape
    n = b.shape[0]
    tm = _tile(m, (512, 256, 128))

    def body(a_ref, b_ref, o_ref):
        o_ref[...] = _dot(a_ref[...], b_ref[...], 1, 1).astype(o_ref.dtype)

    return pl.pallas_call(
        body, name=name, grid=(m // tm,),
        in_specs=[pl.BlockSpec((tm, k), lambda i: (i, 0)), pl.BlockSpec((n, k), lambda i: (0, 0))],
        out_specs=pl.BlockSpec((tm, n), lambda i: (i, 0)),
        out_shape=jax.ShapeDtypeStruct((m, n), BF16),
        compiler_params=_cparams(("arbitrary",)),
    )(a, b)


def matmul_tn(a, b, tm, name, comms=()):
    k, m = a.shape
    n = b.shape[1]
    steps = m // tm
    hosted = _Hosted(comms)

    kc = _tile(k, (1024, 128))
    pieces = k // kc

    def body(*refs):
        (a_ref, b_hbm), (o_ref, cs_ref), (b_ref, b_sems), phases = hosted.split(refs, 2, 2, 2)
        step = pl.program_id(0)
        _before_step(phases, step, steps)

        def b_load(j):
            return pltpu.make_async_copy(b_hbm.at[j * kc:(j + 1) * kc, :], b_ref.at[j * kc:(j + 1) * kc, :], b_sems.at[j])

        @pl.when(step == 0)
        def _():
            for j in range(pieces):
                b_load(j).start()
            acc = None
            for j in range(pieces):
                b_load(j).wait()
                part = _dot(a_ref[j * kc:(j + 1) * kc, :], b_ref[j * kc:(j + 1) * kc, :], 0, 0)
                acc = part if acc is None else acc + part
            o_ref[...] = acc.astype(o_ref.dtype)

        @pl.when(step > 0)
        def _():
            o_ref[...] = _dot(a_ref[...], b_ref[...], 0, 0).astype(o_ref.dtype)

        rows = _tile(k, (512, 128))
        cs = jnp.zeros((1, tm), F32)
        for r0 in range(0, k, rows):
            cs = cs + jnp.sum(a_ref[r0:r0 + rows, :].astype(F32), axis=0, keepdims=True)
        cs_ref[...] = cs
        _after_step(phases, step, steps)

    return pl.pallas_call(
        body, name=name, grid=(steps,),
        in_specs=[pl.BlockSpec((k, tm), lambda i: (0, i)), ANY] + hosted.in_specs,
        out_specs=[pl.BlockSpec((tm, n), lambda i: (i, 0)), pl.BlockSpec((1, tm), lambda i: (0, i))] + hosted.out_specs,
        out_shape=[jax.ShapeDtypeStruct((m, n), BF16), jax.ShapeDtypeStruct((1, m), F32)] + hosted.out_shape,
        scratch_shapes=[pltpu.VMEM((k, n), b.dtype), pltpu.SemaphoreType.DMA((pieces,))] + hosted.scratch,
        compiler_params=_cparams(("arbitrary",)),
    )(a, b, *hosted.inputs)


def in_proj_bwd(dproj, wt, x, g_pre, dout, comms=(), after=None):
    s, d = x.shape
    tm = _tile(s, (512, 256, 128))
    steps = s // tm
    nsub = tm // CHUNK
    kw = 8 * LANES
    kchunks = [(k0, kw) for k0 in range(0, D_IN - D_IN % kw, kw)]
    if D_IN % kw:
        kchunks.append((D_IN - D_IN % kw, D_IN % kw))
    ksplit = len(kchunks)
    hosted = _Hosted(comms)
    order_only = [] if after is None else [after]

    def body(*refs):
        ((*dp_refs, w_hbm, x_hbm, g_ref, dout_hbm), (gx_hbm, dg_ref),
         (w_ref, w_sems, xbuf, dbuf, gbuf, in_sems, out_sems), phases) = hosted.split(
             refs[:4 + ksplit] + refs[4 + ksplit + len(order_only):], 4 + ksplit, 2, 7)
        step = pl.program_id(0)
        _before_step(phases, step, steps)

        def rows_of(ref, c):
            return ref.at[pl.ds(pl.multiple_of(step * tm + c * CHUNK, CHUNK), CHUNK), :]

        def fetches(c):
            return (pltpu.make_async_copy(rows_of(x_hbm, c), xbuf.at[c], in_sems.at[c]),
                    pltpu.make_async_copy(rows_of(dout_hbm, c), dbuf.at[c], in_sems.at[nsub + c]))

        def put(c):
            return pltpu.make_async_copy(gbuf.at[c % 2], rows_of(gx_hbm, c), out_sems.at[c % 2])

        for c in range(nsub):
            for cp in fetches(c):
                cp.start()

        def w_load(j):
            k0, kw = kchunks[j]
            return pltpu.make_async_copy(w_hbm.at[k0:k0 + kw, :], w_ref.at[k0:k0 + kw, :], w_sems.at[j])

        @pl.when(step == 0)
        def _():
            dg_ref[...] = jnp.zeros_like(dg_ref)
            for j in range(ksplit):
                w_load(j).start()

        dh_all = None
        for j, ((k0, kw), dp_ref) in enumerate(zip(kchunks, dp_refs)):
            @pl.when(step == 0)
            def _():
                w_load(j).wait()

            part = _dot(dp_ref[...], w_ref[k0:k0 + kw, :], 1, 0)
            dh_all = part if dh_all is None else dh_all + part
        for c in range(nsub):
            for cp in fetches(c):
                cp.wait()
            if c >= 2:
                put(c - 2).wait()
            elif c < nsub:
                @pl.when(step > 0)
                def _():
                    put(max(nsub - 2, 0) + c).wait()
            dh = dh_all[c * CHUNK:(c + 1) * CHUNK, :]
            xv = xbuf[c]
            r = lax.rsqrt(jnp.mean(xv * xv, axis=-1, keepdims=True) + EPS)
            xn = xv * r
            dg_ref[...] += jnp.sum(dh * xn, axis=0, keepdims=True)
            dn = dh * g_ref[...]
            gbuf[c % 2] = dbuf[c] + r * (dn - xn * jnp.mean(dn * xn, axis=-1, keepdims=True))
            put(c).start()
        @pl.when(step == steps - 1)
        def _():
            for c in range(max(nsub - 2, 0), nsub):
                put(c).wait()

        _after_step(phases, step, steps)

    side_in, side_out = pltpu.VMEM((nsub, CHUNK, d), F32), pltpu.VMEM((2, CHUNK, d), F32)
    row = pl.BlockSpec((1, d), lambda i: (0, 0))
    return pl.pallas_call(
        body, name="in_proj_bwd", grid=(steps,),
        in_specs=[pl.BlockSpec((tm, kw), functools.partial(lambda j, i: (i, j), k0 // kw)) for k0, kw in kchunks]
        + [ANY, ANY, row, ANY] + [ANY] * len(order_only) + hosted.in_specs,
        out_specs=[ANY, row] + hosted.out_specs,
        out_shape=[jax.ShapeDtypeStruct((s, d), F32), jax.ShapeDtypeStruct((1, d), F32)] + hosted.out_shape,
        scratch_shapes=[pltpu.VMEM((D_IN, d), BF16), pltpu.SemaphoreType.DMA((ksplit,)), side_in, side_in, side_out,
                        pltpu.SemaphoreType.DMA((2 * nsub,)), pltpu.SemaphoreType.DMA((2,))] + hosted.scratch,
        compiler_params=_cparams(("arbitrary",)),
    )(*([dproj] * ksplit), wt, x, g_pre, dout, *order_only, *hosted.inputs)


def _lane_iota(shape):
    return lax.broadcasted_iota(jnp.int32, shape, len(shape) - 1)


def _partner(v):
    low = (_lane_iota(v.shape) % HEAD_DIM) < (HEAD_DIM // 2)
    return jnp.where(low, pltpu.roll(v, LANES - HEAD_DIM // 2, 1), pltpu.roll(v, HEAD_DIM // 2, 1))


def _rope(v, cos, sin_signed):
    return v * cos + _partner(v) * sin_signed


def _rope_transposed(dv, cos, sin_signed):
    return dv * cos - _partner(dv) * sin_signed


def _both_halves(v, kv_head):
    keep = (_lane_iota(v.shape) >= HEAD_DIM) if kv_head else (_lane_iota(v.shape) < HEAD_DIM)
    return jnp.where(keep, v, pltpu.roll(v, HEAD_DIM, 1))


def _fold_halves(acc):
    return acc + pltpu.roll(acc, HEAD_DIM, 1)


def _by_half(a, b):
    shape = jnp.broadcast_shapes(jnp.shape(a), jnp.shape(b))
    return jnp.where(_lane_iota(shape) < HEAD_DIM, a, b)


def _stack_heads(pair):
    return jnp.concatenate([_by_half(pair, 0.0), _by_half(0.0, pair)], axis=0)


def _band_bias(has_prev):
    i = lax.broadcasted_iota(jnp.int32, (2 * CHUNK, 2 * CHUNK), 0) % CHUNK
    j = lax.broadcasted_iota(jnp.int32, (2 * CHUNK, 2 * CHUNK), 1)
    band = jnp.logical_and(j > i, j <= i + CHUNK)
    return jnp.where(jnp.logical_and(band, jnp.logical_or(j >= CHUNK, has_prev)), 0.0, NEG)


def _probs_staged(qm2s, kk2s, bias, sink_cols):
    k = range(len(qm2s))
    scs = [_dot(qm2s[i], kk2s[i], 1, 1) + bias for i in k]
    mxs = [jnp.maximum(jnp.max(scs[i], axis=-1, keepdims=True), sink_cols[i]) for i in k]
    ps = [jnp.exp(scs[i] - mxs[i]) for i in k]
    ess = [jnp.exp(sink_cols[i] - mxs[i]) for i in k]
    invs = [1.0 / (jnp.sum(ps[i], axis=-1, keepdims=True) + ess[i]) for i in k]
    return [ps[i] * invs[i] for i in k], [ess[i] * invs[i] for i in k]


def _sink_col(sinks_ref, pair):
    row = lax.broadcasted_iota(jnp.int32, (2 * CHUNK, 1), 0)
    return jnp.where(row < CHUNK, sinks_ref[2 * pair], sinks_ref[2 * pair + 1])


def _layer_norm_parts(v):
    mu = jnp.mean(v, axis=-1, keepdims=True)
    xc = v - mu
    rstd = lax.rsqrt(jnp.mean(xc * xc, axis=-1, keepdims=True) + EPS)
    return xc * rstd, rstd


def _masked_spatial(w_ref, g):
    t = lax.broadcasted_iota(jnp.int32, (CHUNK, CHUNK), 0)
    sidx = lax.broadcasted_iota(jnp.int32, (CHUNK, CHUNK), 1)
    return jnp.where(t >= sidx, w_ref[g], 0.0).astype(BF16)


def _keys_values(kv_ref, kvp_ref, rope_ref, ropep_ref):
    cos_c, sin_c = rope_ref[:, :LANES], rope_ref[:, LANES:]
    cos_p, sin_p = ropep_ref[:, :LANES], ropep_ref[:, LANES:]
    k_c = _rope(kv_ref[:, :D_KV].astype(F32), cos_c, sin_c)
    k_p = _rope(kvp_ref[:, :D_KV].astype(F32), cos_p, sin_p)
    keys = jnp.concatenate([k_p, k_c], axis=0)
    vals = jnp.concatenate([kvp_ref[:, D_KV:], kv_ref[:, D_KV:]], axis=0).astype(F32)
    return keys, vals, (cos_c, sin_c, cos_p, sin_p)


def mixer_fwd(proj, rope, ln_g, ln_b, w_sp, b_sp_rows, sinks, comms=()):
    s = proj.shape[0]
    nb = s // CHUNK
    hosted = _Hosted(comms)

    def body(sinks_ref, *refs):
        ((proj_ref, kvp_ref, rope_ref, ropep_ref, lng_ref, lnb_ref, w_ref, b_ref), (cat_ref, p_ref), _,
         phases) = hosted.split(refs, 8, 2, 0)
        n = pl.program_id(0)
        _before_step(phases, n, nb)
        xhat, _ = _layer_norm_parts(proj_ref[:, OFF_V:OFF_V + D_GMLP].astype(F32))
        vnb = (xhat * lng_ref[...] + lnb_ref[...]).astype(BF16)
        mixeds = [_dot(_masked_spatial(w_ref, g), vnb[:, g * CHUNK:(g + 1) * CHUNK], 1, 0) + b_ref[g]
                  for g in range(GROUPS)]
        for g in range(GROUPS):
            za = proj_ref[:, OFF_ZA + g * CHUNK:OFF_ZA + (g + 1) * CHUNK].astype(F32)
            u = proj_ref[:, OFF_U + g * CHUNK:OFF_U + (g + 1) * CHUNK].astype(F32)
            cat_ref[:, g * CHUNK:(g + 1) * CHUNK] = (u * mixeds[g] * (za * _sigmoid(za))).astype(BF16)
        kv_ref = proj_ref.at[:, OFF_K:OFF_K + 2 * D_KV]
        keys, vals, (cos_c, sin_c, _, _) = _keys_values(kv_ref, kvp_ref, rope_ref, ropep_ref)
        cos_q, sin_q = cos_c * SCALE, sin_c * SCALE
        bias = _band_bias(n > 0)
        kk2 = [_both_halves(keys, kvh).astype(BF16) for kvh in range(N_KV_HEADS)]
        vv2 = [_both_halves(vals, kvh).astype(BF16) for kvh in range(N_KV_HEADS)]
        first_col = _lane_iota((2 * CHUNK, 2 * CHUNK)) == 0
        for kvh in range(N_KV_HEADS):
            pairs = range(kvh * PAIRS_PER_KV, (kvh + 1) * PAIRS_PER_KV)
            qms = [_stack_heads(_rope(proj_ref[:, OFF_Q + pair * LANES:OFF_Q + (pair + 1) * LANES].astype(F32),
                                      cos_q, sin_q)).astype(BF16) for pair in pairs]
            probs, sink_probs = _probs_staged(qms, [kk2[kvh]] * PAIRS_PER_KV, bias,
                                              [_sink_col(sinks_ref, pair) for pair in pairs])
            pbs = [p.astype(BF16) for p in probs]
            outs = [_dot(pb, vv2[kvh], 1, 0) for pb in pbs]
            for pair, pb, sink_prob in zip(pairs, pbs, sink_probs):
                p_ref[0, pair] = jnp.where(first_col, sink_prob.astype(BF16), pb)
            for pair, out in zip(pairs, outs):
                out_pair = _by_half(out[:CHUNK], out[CHUNK:])
                zb = proj_ref[:, OFF_ZB + pair * LANES:OFF_ZB + (pair + 1) * LANES].astype(F32)
                cat_ref[:, D_GMLP + pair * LANES:D_GMLP + (pair + 1) * LANES] = (
                    out_pair * (zb * _sigmoid(zb))).astype(BF16)
        _after_step(phases, n, nb)

    prev = lambda n, *_: (jnp.maximum(n - 1, 0), 0)
    kv_block = OFF_K // (2 * D_KV)
    return pl.pallas_call(
        body, name="mixer_fwd",
        grid_spec=pltpu.PrefetchScalarGridSpec(
            num_scalar_prefetch=1, grid=(nb,),
            in_specs=[pl.BlockSpec((CHUNK, D_IN), lambda n, *_: (n, 0)),
                      pl.BlockSpec((CHUNK, 2 * D_KV), lambda n, *_: (jnp.maximum(n - 1, 0), kv_block)),
                      pl.BlockSpec((CHUNK, 2 * LANES), lambda n, *_: (n, 0)),
                      pl.BlockSpec((CHUNK, 2 * LANES), prev),
                      pl.BlockSpec((1, D_GMLP), lambda n, *_: (0, 0)),
                      pl.BlockSpec((1, D_GMLP), lambda n, *_: (0, 0)),
                      pl.BlockSpec((GROUPS, CHUNK, CHUNK), lambda n, *_: (0, 0, 0)),
                      pl.BlockSpec((GROUPS, CHUNK, CHUNK), lambda n, *_: (0, 0, 0))] + hosted.in_specs,
            out_specs=[pl.BlockSpec((CHUNK, D_GMLP + D_ATTN), lambda n, *_: (n, 0)),
                       pl.BlockSpec((1, N_PAIRS, 2 * CHUNK, 2 * CHUNK), lambda n, *_: (n, 0, 0, 0))]
            + hosted.out_specs,
            scratch_shapes=hosted.scratch),
        out_shape=[jax.ShapeDtypeStruct((s, D_GMLP + D_ATTN), BF16),
                   jax.ShapeDtypeStruct((nb, N_PAIRS, 2 * CHUNK, 2 * CHUNK), BF16)] + hosted.out_shape,
        compiler_params=_cparams(("arbitrary",)),
    )(sinks, proj, proj, rope, rope, ln_g, ln_b, w_sp, b_sp_rows, *hosted.inputs)


def mixer_bwd(proj, dcat, probs, rope, ln_g, ln_b, w_sp, b_sp_rows, comms=()):
    s = proj.shape[0]
    nb = s // CHUNK
    hosted = _Hosted(comms)

    def body(*refs):
        ((proj_ref, kvp_ref, dcat_ref, p_ref, rope_ref, ropep_ref, lng_ref, lnb_ref, w_ref, b_ref),
         (dproj_ref, dw_ref, db_ref, dlng_ref, dlnb_ref, dsink_ref),
         (pend_ref, pend_kv_ref, dbacc_ref), phases) = hosted.split(refs, 10, 6, 3)
        n = pl.program_id(0)
        _before_step(phases, n, nb + 1)

        @pl.when(n == 0)
        def _():
            dw_ref[...] = jnp.zeros_like(dw_ref)
            dbacc_ref[...] = jnp.zeros_like(dbacc_ref)
            dlng_ref[...] = jnp.zeros_like(dlng_ref)
            dlnb_ref[...] = jnp.zeros_like(dlnb_ref)
            dsink_ref[...] = jnp.zeros_like(dsink_ref)

        @pl.when(n > 0)
        def _():
            dproj_ref[...] = pend_ref[...]

        def flush(dkv_prev):
            @pl.when(n > 0)
            def _():
                dproj_ref[:, OFF_K:OFF_K + 2 * D_KV] = (pend_kv_ref[...] + dkv_prev).astype(BF16)

        @pl.when(n < nb)
        def _():
            kv_ref = proj_ref.at[:, OFF_K:OFF_K + 2 * D_KV]
            keys, vals, (cos_c, sin_c, cos_p, sin_p) = _keys_values(kv_ref, kvp_ref, rope_ref, ropep_ref)
            cos_q, sin_q = cos_c * SCALE, sin_c * SCALE
            first_col = _lane_iota((2 * CHUNK, 2 * CHUNK)) == 0
            lane_row = _lane_iota((1, LANES))
            dsink = jnp.zeros((1, LANES), F32)
            dk_heads, dv_heads = [], []
            for kvh in range(N_KV_HEADS):
                kk2 = _both_halves(keys, kvh).astype(BF16)
                vv2 = _both_halves(vals, kvh).astype(BF16)
                pairs = list(range(kvh * PAIRS_PER_KV, (kvh + 1) * PAIRS_PER_KV))
                k4 = range(PAIRS_PER_KV)
                qm2s = [_stack_heads(_rope(proj_ref[:, OFF_Q + pair * LANES:OFF_Q + (pair + 1) * LANES].astype(F32),
                                           cos_q, sin_q)).astype(BF16) for pair in pairs]
                kept = [p_ref[0, pair] for pair in pairs]
                pbs = [jnp.where(first_col, jnp.zeros_like(kp), kp) for kp in kept]
                ps = [pb.astype(F32) for pb in pbs]
                p_sinks = [kp[:, 0:1].astype(F32) for kp in kept]
                o2s = [_dot(pb, vv2, 1, 0) for pb in pbs]
                zbs = [proj_ref[:, OFF_ZB + pair * LANES:OFF_ZB + (pair + 1) * LANES].astype(F32) for pair in pairs]
                sgs = [_sigmoid(zb) for zb in zbs]
                dybs = [dcat_ref[:, D_GMLP + pair * LANES:D_GMLP + (pair + 1) * LANES].astype(F32) for pair in pairs]
                for i, pair in enumerate(pairs):
                    out_pair = _by_half(o2s[i][:CHUNK], o2s[i][CHUNK:])
                    pend_ref[:, OFF_ZB + pair * LANES:OFF_ZB + (pair + 1) * LANES] = (
                        dybs[i] * out_pair * (sgs[i] * (1.0 + zbs[i] * (1.0 - sgs[i])))).astype(BF16)
                dom2s = [_stack_heads(dybs[i] * (zbs[i] * sgs[i])).astype(BF16) for i in k4]
                dps = [_dot(dom2, vv2, 1, 1) for dom2 in dom2s]
                deltas = [jnp.sum(ps[i] * dps[i], axis=-1, keepdims=True) for i in k4]
                dss = [ps[i] * (dps[i] - deltas[i]) for i in k4]
                for i, pair in enumerate(pairs):
                    dsk = -(p_sinks[i] * deltas[i])
                    dsink = dsink + jnp.where(lane_row == 2 * pair,
                                              jnp.sum(dsk[:CHUNK], axis=0, keepdims=True), 0.0)
                    dsink = dsink + jnp.where(lane_row == 2 * pair + 1,
                                              jnp.sum(dsk[CHUNK:], axis=0, keepdims=True), 0.0)
                dsbs = [ds.astype(BF16) for ds in dss]
                dq2s = [_dot(dsb, kk2, 1, 0) for dsb in dsbs]
                for pair, dq2 in zip(pairs, dq2s):
                    pend_ref[:, OFF_Q + pair * LANES:OFF_Q + (pair + 1) * LANES] = _rope_transposed(
                        _by_half(dq2[:CHUNK], dq2[CHUNK:]), cos_q, sin_q).astype(BF16)
                dkks = [_dot(dsbs[i], qm2s[i], 0, 0) for i in k4]
                dvvs = [_dot(pbs[i], dom2s[i], 0, 0) for i in k4]
                dk_heads.append(_fold_halves((dkks[0] + dkks[1]) + (dkks[2] + dkks[3])))
                dv_heads.append(_fold_halves((dvvs[0] + dvvs[1]) + (dvvs[2] + dvvs[3])))
            dk_rot = _by_half(dk_heads[0], dk_heads[1])
            dv_all = _by_half(dv_heads[0], dv_heads[1])
            dk_p = _rope_transposed(dk_rot[:CHUNK], cos_p, sin_p)
            dk_c = _rope_transposed(dk_rot[CHUNK:], cos_c, sin_c)
            flush(jnp.concatenate([dk_p, dv_all[:CHUNK]], axis=1))
            dsink_ref[...] += dsink
            pend_kv_ref[...] = jnp.concatenate([dk_c, dv_all[CHUNK:]], axis=1)
            xhat, rstd = _layer_norm_parts(proj_ref[:, OFF_V:OFF_V + D_GMLP].astype(F32))
            lng = lng_ref[...]
            vnb = (xhat * lng + lnb_ref[...]).astype(BF16)
            dvn_cols = []
            for g in range(GROUPS):
                cols = slice(g * CHUNK, (g + 1) * CHUNK)
                wm = _masked_spatial(w_ref, g)
                mixed = _dot(wm, vnb[:, cols], 1, 0) + b_ref[g]
                za = proj_ref[:, OFF_ZA + g * CHUNK:OFF_ZA + (g + 1) * CHUNK].astype(F32)
                u = proj_ref[:, OFF_U + g * CHUNK:OFF_U + (g + 1) * CHUNK].astype(F32)
                dya = dcat_ref[:, cols].astype(F32)
                sg = _sigmoid(za)
                sz = za * sg
                pend_ref[:, OFF_U + g * CHUNK:OFF_U + (g + 1) * CHUNK] = (dya * mixed * sz).astype(BF16)
                pend_ref[:, OFF_ZA + g * CHUNK:OFF_ZA + (g + 1) * CHUNK] = (
                    dya * u * mixed * (sg * (1.0 + za * (1.0 - sg)))).astype(BF16)
                dmixed = dya * u * sz
                dmb = dmixed.astype(BF16)
                dbacc_ref[g] += dmixed
                dw_ref[g] += _dot(dmb, vnb[:, cols], 1, 1)
                dvn_cols.append(_dot(wm, dmb, 0, 0))
            dvn = jnp.concatenate(dvn_cols, axis=1)
            dlng_ref[...] += jnp.sum(dvn * xhat, axis=0, keepdims=True)
            dlnb_ref[...] += jnp.sum(dvn, axis=0, keepdims=True)
            dxh = dvn * lng
            dv = rstd * (dxh - jnp.mean(dxh, axis=-1, keepdims=True)
                         - xhat * jnp.mean(dxh * xhat, axis=-1, keepdims=True))
            pend_ref[:, OFF_V:OFF_V + D_GMLP] = dv.astype(BF16)

        @pl.when(n == nb)
        def _():
            flush(jnp.zeros((CHUNK, 2 * D_KV), F32))
            t = lax.broadcasted_iota(jnp.int32, (CHUNK, CHUNK), 0)
            sidx = lax.broadcasted_iota(jnp.int32, (CHUNK, CHUNK), 1)
            lane = _lane_iota((CHUNK, LANES))
            dbt = jnp.zeros((CHUNK, LANES), F32)
            for g in range(GROUPS):
                dw_ref[g] = jnp.where(t >= sidx, dw_ref[g], 0.0)
                dbt = jnp.where(lane == g, jnp.sum(dbacc_ref[g], axis=-1, keepdims=True), dbt)
            db_ref[...] = jnp.transpose(dbt)[:GROUPS, :]

        _after_step(phases, n, nb + 1)

    cur = lambda n: (jnp.minimum(n, nb - 1), 0)
    prev = lambda n: (jnp.clip(n - 1, 0, nb - 1), 0)
    kv_block = OFF_K // (2 * D_KV)
    const2 = lambda n: (0, 0)
    const3 = lambda n: (0, 0, 0)
    return pl.pallas_call(
        body, name="mixer_bwd", grid=(nb + 1,),
        in_specs=[pl.BlockSpec((CHUNK, D_IN), cur),
                  pl.BlockSpec((CHUNK, 2 * D_KV), lambda n: (jnp.clip(n - 1, 0, nb - 1), kv_block)),
                  pl.BlockSpec((CHUNK, D_GMLP + D_ATTN), cur),
                  pl.BlockSpec((1, N_PAIRS, 2 * CHUNK, 2 * CHUNK), lambda n: (jnp.minimum(n, nb - 1), 0, 0, 0)),
                  pl.BlockSpec((CHUNK, 2 * LANES), cur),
                  pl.BlockSpec((CHUNK, 2 * LANES), prev),
                  pl.BlockSpec((1, D_GMLP), const2),
                  pl.BlockSpec((1, D_GMLP), const2),
                  pl.BlockSpec((GROUPS, CHUNK, CHUNK), const3),
                  pl.BlockSpec((GROUPS, CHUNK, CHUNK), const3)] + hosted.in_specs,
        out_specs=[pl.BlockSpec((CHUNK, D_IN), lambda n: (jnp.maximum(n - 1, 0), 0)),
                   pl.BlockSpec((GROUPS, CHUNK, CHUNK), const3),
                   pl.BlockSpec((GROUPS, CHUNK), const2),
                   pl.BlockSpec((1, D_GMLP), const2),
                   pl.BlockSpec((1, D_GMLP), const2),
                   pl.BlockSpec((1, LANES), const2)] + hosted.out_specs,
        scratch_shapes=[pltpu.VMEM((CHUNK, D_IN), BF16), pltpu.VMEM((CHUNK, 2 * D_KV), F32),
                        pltpu.VMEM((GROUPS, CHUNK, CHUNK), F32)] + hosted.scratch,
        out_shape=[jax.ShapeDtypeStruct((s, D_IN), BF16),
                   jax.ShapeDtypeStruct((GROUPS, CHUNK, CHUNK), F32),
                   jax.ShapeDtypeStruct((GROUPS, CHUNK), F32),
                   jax.ShapeDtypeStruct((1, D_GMLP), F32),
                   jax.ShapeDtypeStruct((1, D_GMLP), F32),
                   jax.ShapeDtypeStruct((1, LANES), F32)] + hosted.out_shape,
        compiler_params=_cparams(("arbitrary",)),
    )(proj, proj, dcat, probs, rope, rope, ln_g, ln_b, w_sp, b_sp_rows, *hosted.inputs)


def _adamw_math(w, g, m, v):
    m = ADAM_B1 * m + (1.0 - ADAM_B1) * g
    v = ADAM_B2 * v + (1.0 - ADAM_B2) * (g * g)
    m_hat = m / (1.0 - ADAM_B1 ** ADAM_STEP)
    v_hat = v / (1.0 - ADAM_B2 ** ADAM_STEP)
    delta = -ADAM_LR * (m_hat / (jnp.sqrt(v_hat) + ADAM_EPS) + ADAM_WD * w)
    return delta, m, v


def adamw_shard(terms, w, m, v, name, after=None):
    r, c = w.shape
    tr = _tile(r, (336, 128, 8))
    n_terms = len(terms)
    order_only = [] if after is None else [after]

    def body(*refs):
        w_ref, m_ref, v_ref, g_ref, d_ref, nm_ref, nv_ref = refs[n_terms:n_terms + 3] + refs[-4:]
        g = None
        for ref, (_, slots) in zip(refs[:n_terms], terms):
            for k in range(slots):
                part = ref[k].astype(F32)
                g = part if g is None else g + part
        g_ref[...] = g
        d_ref[...], nm_ref[...], nv_ref[...] = _adamw_math(w_ref[...], g, m_ref[...], v_ref[...])

    spec = pl.BlockSpec((tr, c), lambda i: (i, 0))
    return pl.pallas_call(
        body, name=name, grid=(r // tr,),
        in_specs=[pl.BlockSpec((slots, tr, c), lambda i: (0, i, 0)) for _, slots in terms] + [spec] * 3
        + [ANY] * len(order_only),
        out_specs=[spec] * 4, out_shape=[jax.ShapeDtypeStruct((r, c), F32)] * 4,
        compiler_params=_cparams(("arbitrary",)),
    )(*[a for a, _ in terms], w, m, v, *order_only)


def adamw_small(gathered, lane_windows, params):
    n_par = len(params)

    def body(*refs):
        g_refs = refs[:n_par + 1]
        wmv_refs = refs[n_par + 1:4 * n_par + 1]
        out_refs = refs[4 * n_par + 1:]

        def total(ref):
            acc = ref[0]
            for dev in range(1, N_DEV):
                acc = acc + ref[dev]
            return acc

        for i in range(n_par):
            w_ref, m_ref, v_ref = wmv_refs[3 * i:3 * i + 3]
            g = total(g_refs[i])
            if lane_windows[i] is not None:
                start, size = lane_windows[i]
                g = g[..., start:start + size]
            delta, new_m, new_v = _adamw_math(w_ref[...], g, m_ref[...], v_ref[...])
            for ref, val in zip(out_refs[4 * i:4 * i + 4], (g, delta, new_m, new_v)):
                ref[...] = val
        out_refs[4 * n_par][...] = total(g_refs[n_par])

    flat = [a for wmv in params for a in wmv]
    out_shape = [jax.ShapeDtypeStruct(w.shape, F32) for (w, _, _) in params for _ in range(4)]
    out_shape.append(jax.ShapeDtypeStruct(gathered[-1].shape[1:], F32))
    outs = pl.pallas_call(body, name="adamw_small", out_shape=out_shape, compiler_params=_cparams())(*gathered, *flat)
    return [tuple(outs[4 * i:4 * i + 4]) for i in range(n_par)], outs[-1]


def kernel(x, positions, g_pre, w_in, b_qkv, ln_v_g, ln_v_b, w_spatial, b_spatial, attn_sinks, w_out, g_post, loss_target, m_g_pre, m_w_in, m_b_qkv, m_ln_v_g, m_ln_v_b, m_w_spatial, m_b_spatial, m_attn_sinks, m_w_out, m_g_post, v_g_pre, v_w_in, v_b_qkv, v_ln_v_g, v_ln_v_b, v_w_spatial, v_b_spatial, v_attn_sinks, v_w_out, v_g_post):
    x2, target = x[0], loss_target[0]
    seq = x2.shape[0]

    wt_shard = w_in[0].T.astype(BF16)
    wo_shard = w_out[0].astype(BF16)
    pos_col = positions.reshape(seq, 1)
    half = HEAD_DIM // 2
    inv_freq = ROPE_THETA ** (-jnp.arange(half, dtype=F32) * (2.0 / HEAD_DIM))
    freq = jnp.tile(inv_freq, LANES // half).reshape(1, LANES)
    sign = jnp.tile(jnp.concatenate([-jnp.ones((half,), F32), jnp.ones((half,), F32)]), LANES // HEAD_DIM)
    sign = sign.reshape(1, LANES)
    bias = jnp.concatenate([jnp.zeros((1, OFF_Q), F32), b_qkv, jnp.zeros((1, D_ATTN), F32)], axis=1)
    proj, h, rope, wt = in_proj_gather(x2, pos_col, freq, sign, g_pre, wt_shard, bias)

    b_rows = jnp.broadcast_to(b_spatial[0][:, :, None], (GROUPS, CHUNK, CHUNK))
    sinks = attn_sinks[0]
    cat, probs, wo = mixer_fwd(proj, rope, ln_v_g, ln_v_b, w_spatial[0], b_rows, sinks,
                               comms=[gather_comm([wo_shard])])
    dy, dout, d_g_post, loss_part = out_proj_loss(cat, wo, x2, target, g_post)

    dcat = matmul_nt(dy, wo, "out_proj_bwd")
    d_wo, _ = matmul_tn(cat, dy, 512, "w_out_grad")
    dproj, d_w_sp, d_b_sp, d_ln_g, d_ln_b, d_sinks, parts_wo = mixer_bwd(
        proj, dcat, probs, rope, ln_v_g, ln_v_b, w_spatial[0], b_rows, comms=[scatter_comm([d_wo])])
    small_parts = [d_ln_g, d_ln_b, d_w_sp, d_b_sp, d_sinks, d_g_post, loss_part]
    d_wt, colsum, *landed = matmul_tn(dproj, h, 768, "w_in_grad", comms=[gather_comm(small_parts, stack=True)])

    sum_wt = pair_exchange_sum(d_wt, "grad_pair_sum_w_in")
    started = chips_exchange_start(sum_wt)
    grad_x, d_g_pre = in_proj_bwd(dproj, wt, x2, g_pre, dout, after=started[-1])
    sum_wt, far_wt = chips_exchange_wait(started, d_g_pre)
    late = jnp.concatenate([d_g_pre, colsum], axis=1)
    late_started, late_copy = gather_start(late)

    wo_out = adamw_shard([(parts_wo, N_DEV)], w_out[0], m_w_out[0], v_w_out[0], "adamw_w_out", after=late_started[-1])
    wt_out = adamw_shard([(sum_wt, 1), (far_wt, 3)], w_in[0].T, m_w_in[0].T, v_w_in[0].T, "adamw_w_in",
                         after=wo_out[0])
    landed_late = gather_wait(late_started, late_copy, wt_out[0])
    my_index = _index(*_my_place())
    all_late = lax.dynamic_update_slice(landed_late, late[None], (my_index, 0, 0))
    gathered = [all_late, all_late] + landed
    windows = [(0, D_MODEL), (D_MODEL + OFF_Q, D_QKV), None, None, None, None, (0, N_Q_HEADS), None]
    small = [(g_pre, m_g_pre, v_g_pre), (b_qkv, m_b_qkv, v_b_qkv), (ln_v_g, m_ln_v_g, v_ln_v_g),
             (ln_v_b, m_ln_v_b, v_ln_v_b), (w_spatial[0], m_w_spatial[0], v_w_spatial[0]),
             (b_spatial[0], m_b_spatial[0], v_b_spatial[0]), (attn_sinks, m_attn_sinks, v_attn_sinks),
             (g_post, m_g_post, v_g_post)]
    small_out, loss_row = adamw_small(gathered, windows, small)
    lead = [False, False, False, False, True, True, False, False]
    small_out = [tuple(a[None] if ld else a for a in leaf) for leaf, ld in zip(small_out, lead)]

    def leaves(k):
        gp, bq, lg, lb, ws, bs, sk, gpo = (leaf[k] for leaf in small_out)
        return [gp, wt_out[k].T[None], bq, lg, lb, ws, bs, sk, wo_out[k][None], gpo]

    return (loss_row[0, 0], grad_x[None], *leaves(0), *leaves(1), *leaves(2), *leaves(3))
```
